```python
import math
import jax, jax.numpy as jnp
from jax import lax
import numpy as np

D_MODEL = 1024
BATCH = 32
SEQ = 2048
DEPTH = 1

EPS = 1e-6
NEG = -1e30
D_FF = 2816

MLA_HEADS = 8
Q_LORA = 256
KV_LORA = 128
QK_NOPE = 64
QK_ROPE = 32
V_HEAD = 64
ROPE_THETA = 10000.0
Q_BLOCK = 128

SWA_HEADS = 8
SWA_KV_HEADS = 2
SWA_HEAD_DIM = 64
WINDOW = 128

REL_BUCKETS = 32
REL_MAX_DIST = 128

MLA_OUT = MLA_HEADS * V_HEAD
SWA_OUT = SWA_HEADS * SWA_HEAD_DIM
D_MIX = MLA_OUT + SWA_OUT
IN_WIDTHS = (Q_LORA, KV_LORA + QK_ROPE, SWA_HEADS * SWA_HEAD_DIM,
             SWA_KV_HEADS * SWA_HEAD_DIM, SWA_KV_HEADS * SWA_HEAD_DIM)
D_IN = sum(IN_WIDTHS)
IN_SPLITS = tuple(int(s) for s in np.cumsum(IN_WIDTHS)[:-1])

kernel_name = "hymba_mla_swa_macaron_t5"


def rmsnorm(x, g):
    xf = x.astype(jnp.float32)
    y = xf * lax.rsqrt(jnp.mean(xf * xf, axis=-1, keepdims=True) + EPS)
    return (y * g.astype(jnp.float32)).astype(x.dtype)


def swiglu(x, w_gate, w_up, w_down):
    return (jax.nn.silu(x @ w_gate) * (x @ w_up)) @ w_down


def rope(x, cos, sin):
    half = x.shape[-1] // 2
    xf = x.astype(jnp.float32)
    x1, x2 = xf[..., :half], xf[..., half:]
    return jnp.concatenate([x1 * cos - x2 * sin, x2 * cos + x1 * sin], axis=-1).astype(x.dtype)


def t5_bucket(dist):
    n = jnp.maximum(dist, 0)
    max_exact = REL_BUCKETS // 2
    nf = jnp.maximum(n, 1).astype(jnp.float32)
    large = max_exact + (jnp.log(nf / max_exact) / math.log(REL_MAX_DIST / max_exact)
                         * (REL_BUCKETS - max_exact)).astype(jnp.int32)
    large = jnp.minimum(large, REL_BUCKETS - 1)
    return jnp.where(n < max_exact, n, large)


def mla_group(c_q, ckv_pe, g_q_a, w_q_b, g_kv_a, w_kv_b):
    B, S, _ = c_q.shape
    pos = jnp.arange(S, dtype=jnp.float32)
    inv_freq = ROPE_THETA ** (-jnp.arange(0, QK_ROPE, 2, dtype=jnp.float32) / QK_ROPE)
    ang = pos[:, None] * inv_freq[None, :]
    cos, sin = jnp.cos(ang), jnp.sin(ang)

    q = (rmsnorm(c_q, g_q_a) @ w_q_b).reshape(B, S, MLA_HEADS, QK_NOPE + QK_ROPE)
    q_nope = q[..., :QK_NOPE]
    q_pe = rope(q[..., QK_NOPE:], cos[:, None, :], sin[:, None, :])
    c_kv = ckv_pe[..., :KV_LORA]
    k_pe = rope(ckv_pe[..., KV_LORA:], cos, sin)
    kv = (rmsnorm(c_kv, g_kv_a) @ w_kv_b).reshape(B, S, MLA_HEADS, QK_NOPE + V_HEAD)
    k_nope, v = kv[..., :QK_NOPE], kv[..., QK_NOPE:]
    scale = (QK_NOPE + QK_ROPE) ** -0.5

    nb = S // Q_BLOCK
    qn_b = q_nope.reshape(B, nb, Q_BLOCK, MLA_HEADS, QK_NOPE).transpose(1, 0, 2, 3, 4)
    qp_b = q_pe.reshape(B, nb, Q_BLOCK, MLA_HEADS, QK_ROPE).transpose(1, 0, 2, 3, 4)
    kpos = jnp.arange(S)

    def block(args):
        qn, qp, i = args
        s = (jnp.einsum('bqhd,bkhd->bhqk', qn, k_nope)
             + jnp.einsum('bqhr,bkr->bhqk', qp, k_pe)).astype(jnp.float32) * scale
        qpos = i * Q_BLOCK + jnp.arange(Q_BLOCK)
        causal = kpos[None, :] <= qpos[:, None]
        s = jnp.where(causal[None, None], s, NEG)
        p = jax.nn.softmax(s, axis=-1).astype(v.dtype)
        return jnp.einsum('bhqk,bkhd->bqhd', p, v)

    o = lax.map(block, (qn_b, qp_b, jnp.arange(nb)))
    return o.transpose(1, 0, 2, 3, 4).reshape(B, S, MLA_OUT)


def swa_group(q, k, v, sinks, rel_bias):
    B, S, _ = q.shape
    nb = S // WINDOW
    G = SWA_HEADS // SWA_KV_HEADS
    dh = SWA_HEAD_DIM
    q = q.reshape(B, nb, WINDOW, SWA_KV_HEADS, G, dh)
    k = k.reshape(B, nb, WINDOW, SWA_KV_HEADS, dh)
    v = v.reshape(B, nb, WINDOW, SWA_KV_HEADS, dh)

    def band(t):
        prev = jnp.pad(t, ((0, 0), (1, 0), (0, 0), (0, 0), (0, 0)))[:, :-1]
        return jnp.concatenate([prev, t], axis=2)

    kb, vb = band(k), band(v)
    s = jnp.einsum('bnqhgd,bnkhd->bnhgqk', q, kb).astype(jnp.float32) * (dh ** -0.5)

    qi = jnp.arange(WINDOW)[:, None]
    kj = jnp.arange(2 * WINDOW)[None, :]
    dist = qi + WINDOW - kj
    bias = rel_bias[t5_bucket(dist)]
    bias = bias.transpose(2, 0, 1).reshape(SWA_KV_HEADS, G, WINDOW, 2 * WINDOW)
    kpos = jnp.arange(nb)[:, None, None] * WINDOW - WINDOW + kj[None]
    valid = (dist >= 0)[None] & (dist < WINDOW)[None] & (kpos >= 0)

    s = s + bias.astype(jnp.float32)[None, None]
    s = jnp.where(valid[None, :, None, None], s, NEG)
    sink = sinks.astype(jnp.float32).reshape(SWA_KV_HEADS, G)[None, None, :, :, None, None]
    m = jnp.maximum(jnp.max(s, axis=-1, keepdims=True), sink)
    p = jnp.exp(s - m)
    denom = jnp.sum(p, axis=-1, keepdims=True) + jnp.exp(sink - m)
    o = jnp.einsum('bnhgqk,bnkhd->bnqhgd', (p / denom).astype(vb.dtype), vb)
    return o.reshape(B, S, SWA_OUT)


def _fwd_setup_inputs(seed: int = 0) -> dict:
    key = jax.random.key(seed)
    ks = iter(jax.random.split(key, 32))
    L = DEPTH

    def w(shape, fan_in):
        return jax.random.normal(next(ks), shape, jnp.float32) * fan_in ** -0.5

    def gain(shape):
        return 1.0 + 0.02 * jax.random.normal(next(ks), shape, jnp.float32)

    return {
        "x": jax.random.normal(next(ks), (BATCH, SEQ, D_MODEL), jnp.float32),
        "g_ffn1": gain((L, D_MODEL)),
        "w_ffn1_gate": w((L, D_MODEL, D_FF), D_MODEL),
        "w_ffn1_up": w((L, D_MODEL, D_FF), D_MODEL),
        "w_ffn1_down": w((L, D_FF, D_MODEL), D_FF),
        "g_mix": gain((L, D_MODEL)),
        "w_in": w((L, D_MODEL, D_IN), D_MODEL),
        "g_q_a": gain((L, Q_LORA)),
        "w_q_b": w((L, Q_LORA, MLA_HEADS * (QK_NOPE + QK_ROPE)), Q_LORA),
        "g_kv_a": gain((L, KV_LORA)),
        "w_kv_b": w((L, KV_LORA, MLA_HEADS * (QK_NOPE + V_HEAD)), KV_LORA),
        "attn_sinks": 0.5 * jax.random.normal(next(ks), (L, SWA_HEADS), jnp.float32),
        "rel_bias": 0.5 * jax.random.normal(next(ks), (REL_BUCKETS, SWA_HEADS), jnp.float32),
        "g_out_mla": gain((L, MLA_OUT)),
        "g_out_swa": gain((L, SWA_OUT)),
        "w_o": w((L, D_MIX, D_MODEL), D_MIX),
        "g_ffn2": gain((L, D_MODEL)),
        "w_ffn2_gate": w((L, D_MODEL, D_FF), D_MODEL),
        "w_ffn2_up": w((L, D_MODEL, D_FF), D_MODEL),
        "w_ffn2_down": w((L, D_FF, D_MODEL), D_FF),
        "g_final": gain((D_MODEL,)),
    }


def _fwd_reference(x, g_ffn1, w_ffn1_gate, w_ffn1_up, w_ffn1_down, g_mix, w_in, g_q_a, w_q_b,
              g_kv_a, w_kv_b, attn_sinks, rel_bias, g_out_mla, g_out_swa, w_o, g_ffn2,
              w_ffn2_gate, w_ffn2_up, w_ffn2_down, g_final):
    h = x
    for l in range(DEPTH):
        h = h + 0.5 * swiglu(rmsnorm(h, g_ffn1[l]), w_ffn1_gate[l], w_ffn1_up[l], w_ffn1_down[l])
        u = rmsnorm(h, g_mix[l])
        proj = u @ w_in[l]
        c_q, ckv_pe, q_s, k_s, v_s = jnp.split(proj, IN_SPLITS, axis=-1)
        o_mla = mla_group(c_q, ckv_pe, g_q_a[l], w_q_b[l], g_kv_a[l], w_kv_b[l])
        o_swa = swa_group(q_s, k_s, v_s, attn_sinks[l], rel_bias)
        o = jnp.concatenate([rmsnorm(o_mla, g_out_mla[l]), rmsnorm(o_swa, g_out_swa[l])], axis=-1)
        h = h + o @ w_o[l]
        h = h + 0.5 * swiglu(rmsnorm(h, g_ffn2[l]), w_ffn2_gate[l], w_ffn2_up[l], w_ffn2_down[l])
    return rmsnorm(h, g_final)


import jax as _jax
import jax.numpy as _jnp

TWIN_FORMAT = 'train_step'
FWD_PARAMS = ['x', 'g_ffn1', 'w_ffn1_gate', 'w_ffn1_up', 'w_ffn1_down', 'g_mix', 'w_in', 'g_q_a', 'w_q_b', 'g_kv_a', 'w_kv_b', 'attn_sinks', 'rel_bias', 'g_out_mla', 'g_out_swa', 'w_o', 'g_ffn2', 'w_ffn2_gate', 'w_ffn2_up', 'w_ffn2_down', 'g_final']
TWIN_WEIGHTS = ['g_ffn1', 'w_ffn1_gate', 'w_ffn1_up', 'w_ffn1_down', 'g_mix', 'w_in', 'g_q_a', 'w_q_b', 'g_kv_a', 'w_kv_b', 'attn_sinks', 'rel_bias', 'g_out_mla', 'g_out_swa', 'w_o', 'g_ffn2', 'w_ffn2_gate', 'w_ffn2_up', 'w_ffn2_down', 'g_final']
TWIN_DIFF_INPUT = 'x'
TWIN_INPUTS = ['x', 'g_ffn1', 'w_ffn1_gate', 'w_ffn1_up', 'w_ffn1_down', 'g_mix', 'w_in', 'g_q_a', 'w_q_b', 'g_kv_a', 'w_kv_b', 'attn_sinks', 'rel_bias', 'g_out_mla', 'g_out_swa', 'w_o', 'g_ffn2', 'w_ffn2_gate', 'w_ffn2_up', 'w_ffn2_down', 'g_final', 'loss_target', 'm_g_ffn1', 'm_w_ffn1_gate', 'm_w_ffn1_up', 'm_w_ffn1_down', 'm_g_mix', 'm_w_in', 'm_g_q_a', 'm_w_q_b', 'm_g_kv_a', 'm_w_kv_b', 'm_attn_sinks', 'm_rel_bias', 'm_g_out_mla', 'm_g_out_swa', 'm_w_o', 'm_g_ffn2', 'm_w_ffn2_gate', 'm_w_ffn2_up', 'm_w_ffn2_down', 'm_g_final', 'v_g_ffn1', 'v_w_ffn1_gate', 'v_w_ffn1_up', 'v_w_ffn1_down', 'v_g_mix', 'v_w_in', 'v_g_q_a', 'v_w_q_b', 'v_g_kv_a', 'v_w_kv_b', 'v_attn_sinks', 'v_rel_bias', 'v_g_out_mla', 'v_g_out_swa', 'v_w_o', 'v_g_ffn2', 'v_w_ffn2_gate', 'v_w_ffn2_up', 'v_w_ffn2_down', 'v_g_final']
TWIN_OUTPUTS = ['loss', 'grad_x', 'grad_g_ffn1', 'grad_w_ffn1_gate', 'grad_w_ffn1_up', 'grad_w_ffn1_down', 'grad_g_mix', 'grad_w_in', 'grad_g_q_a', 'grad_w_q_b', 'grad_g_kv_a', 'grad_w_kv_b', 'grad_attn_sinks', 'grad_rel_bias', 'grad_g_out_mla', 'grad_g_out_swa', 'grad_w_o', 'grad_g_ffn2', 'grad_w_ffn2_gate', 'grad_w_ffn2_up', 'grad_w_ffn2_down', 'grad_g_final', 'delta_g_ffn1', 'delta_w_ffn1_gate', 'delta_w_ffn1_up', 'delta_w_ffn1_down', 'delta_g_mix', 'delta_w_in', 'delta_g_q_a', 'delta_w_q_b', 'delta_g_kv_a', 'delta_w_kv_b', 'delta_attn_sinks', 'delta_rel_bias', 'delta_g_out_mla', 'delta_g_out_swa', 'delta_w_o', 'delta_g_ffn2', 'delta_w_ffn2_gate', 'delta_w_ffn2_up', 'delta_w_ffn2_down', 'delta_g_final', 'new_m_g_ffn1', 'new_m_w_ffn1_gate', 'new_m_w_ffn1_up', 'new_m_w_ffn1_down', 'new_m_g_mix', 'new_m_w_in', 'new_m_g_q_a', 'new_m_w_q_b', 'new_m_g_kv_a', 'new_m_w_kv_b', 'new_m_attn_sinks', 'new_m_rel_bias', 'new_m_g_out_mla', 'new_m_g_out_swa', 'new_m_w_o', 'new_m_g_ffn2', 'new_m_w_ffn2_gate', 'new_m_w_ffn2_up', 'new_m_w_ffn2_down', 'new_m_g_final', 'new_v_g_ffn1', 'new_v_w_ffn1_gate', 'new_v_w_ffn1_up', 'new_v_w_ffn1_down', 'new_v_g_mix', 'new_v_w_in', 'new_v_g_q_a', 'new_v_w_q_b', 'new_v_g_kv_a', 'new_v_w_kv_b', 'new_v_attn_sinks', 'new_v_rel_bias', 'new_v_g_out_mla', 'new_v_g_out_swa', 'new_v_w_o', 'new_v_g_ffn2', 'new_v_w_ffn2_gate', 'new_v_w_ffn2_up', 'new_v_w_ffn2_down', 'new_v_g_final']
TWIN_LEAF_KINDS = {'loss': 'loss', 'grad_x': 'grad_x', 'grad_g_ffn1': 'grad_w', 'grad_w_ffn1_gate': 'grad_w', 'grad_w_ffn1_up': 'grad_w', 'grad_w_ffn1_down': 'grad_w', 'grad_g_mix': 'grad_w', 'grad_w_in': 'grad_w', 'grad_g_q_a': 'grad_w', 'grad_w_q_b': 'grad_w', 'grad_g_kv_a': 'grad_w', 'grad_w_kv_b': 'grad_w', 'grad_attn_sinks': 'grad_w', 'grad_rel_bias': 'grad_w', 'grad_g_out_mla': 'grad_w', 'grad_g_out_swa': 'grad_w', 'grad_w_o': 'grad_w', 'grad_g_ffn2': 'grad_w', 'grad_w_ffn2_gate': 'grad_w', 'grad_w_ffn2_up': 'grad_w', 'grad_w_ffn2_down': 'grad_w', 'grad_g_final': 'grad_w', 'delta_g_ffn1': 'delta_w', 'delta_w_ffn1_gate': 'delta_w', 'delta_w_ffn1_up': 'delta_w', 'delta_w_ffn1_down': 'delta_w', 'delta_g_mix': 'delta_w', 'delta_w_in': 'delta_w', 'delta_g_q_a': 'delta_w', 'delta_w_q_b': 'delta_w', 'delta_g_kv_a': 'delta_w', 'delta_w_kv_b': 'delta_w', 'delta_attn_sinks': 'delta_w', 'delta_rel_bias': 'delta_w', 'delta_g_out_mla': 'delta_w', 'delta_g_out_swa': 'delta_w', 'delta_w_o': 'delta_w', 'delta_g_ffn2': 'delta_w', 'delta_w_ffn2_gate': 'delta_w', 'delta_w_ffn2_up': 'delta_w', 'delta_w_ffn2_down': 'delta_w', 'delta_g_final': 'delta_w', 'new_m_g_ffn1': 'new_m', 'new_m_w_ffn1_gate': 'new_m', 'new_m_w_ffn1_up': 'new_m', 'new_m_w_ffn1_down': 'new_m', 'new_m_g_mix': 'new_m', 'new_m_w_in': 'new_m', 'new_m_g_q_a': 'new_m', 'new_m_w_q_b': 'new_m', 'new_m_g_kv_a': 'new_m', 'new_m_w_kv_b': 'new_m', 'new_m_attn_sinks': 'new_m', 'new_m_rel_bias': 'new_m', 'new_m_g_out_mla': 'new_m', 'new_m_g_out_swa': 'new_m', 'new_m_w_o': 'new_m', 'new_m_g_ffn2': 'new_m', 'new_m_w_ffn2_gate': 'new_m', 'new_m_w_ffn2_up': 'new_m', 'new_m_w_ffn2_down': 'new_m', 'new_m_g_final': 'new_m', 'new_v_g_ffn1': 'new_v', 'new_v_w_ffn1_gate': 'new_v', 'new_v_w_ffn1_up': 'new_v', 'new_v_w_ffn1_down': 'new_v', 'new_v_g_mix': 'new_v', 'new_v_w_in': 'new_v', 'new_v_g_q_a': 'new_v', 'new_v_w_q_b': 'new_v', 'new_v_g_kv_a': 'new_v', 'new_v_w_kv_b': 'new_v', 'new_v_attn_sinks': 'new_v', 'new_v_rel_bias': 'new_v', 'new_v_g_out_mla': 'new_v', 'new_v_g_out_swa': 'new_v', 'new_v_w_o': 'new_v', 'new_v_g_ffn2': 'new_v', 'new_v_w_ffn2_gate': 'new_v', 'new_v_w_ffn2_up': 'new_v', 'new_v_w_ffn2_down': 'new_v', 'new_v_g_final': 'new_v'}


def _forward(args):
    return _fwd_reference(*[args[k] for k in FWD_PARAMS])


def _output_shape():
    out = _jax.eval_shape(lambda: _forward(_fwd_setup_inputs(0)))
    return out.shape, out.dtype

N_MICROBATCH = 1
ADAM_LR = 0.001
ADAM_B1 = 0.9
ADAM_B2 = 0.999
ADAM_EPS = 1e-08
ADAM_WD = 0.01
ADAM_STEP = 10
PER_EXAMPLE_BATCH_AXIS = {'x': 0, 'loss_target': 0}
SHARED_INPUTS = []
_WEIGHT_DTYPES = {'g_ffn1': _jnp.float32, 'w_ffn1_gate': _jnp.float32, 'w_ffn1_up': _jnp.float32, 'w_ffn1_down': _jnp.float32, 'g_mix': _jnp.float32, 'w_in': _jnp.float32, 'g_q_a': _jnp.float32, 'w_q_b': _jnp.float32, 'g_kv_a': _jnp.float32, 'w_kv_b': _jnp.float32, 'attn_sinks': _jnp.float32, 'rel_bias': _jnp.float32, 'g_out_mla': _jnp.float32, 'g_out_swa': _jnp.float32, 'w_o': _jnp.float32, 'g_ffn2': _jnp.float32, 'w_ffn2_gate': _jnp.float32, 'w_ffn2_up': _jnp.float32, 'w_ffn2_down': _jnp.float32, 'g_final': _jnp.float32}
MOMENT_SCALE = {'g_ffn1': 1.436270e-01, 'w_ffn1_gate': 5.976803e-02, 'w_ffn1_up': 5.792537e-02, 'w_ffn1_down': 9.614712e-02, 'g_mix': 2.845491e-01, 'w_in': 2.631862e-01, 'g_q_a': 2.497245e-01, 'w_q_b': 1.480883e-01, 'g_kv_a': 7.897380e-01, 'w_kv_b': 1.859186e-01, 'attn_sinks': 5.152534e-02, 'rel_bias': 2.563277e-01, 'g_out_mla': 1.809094e-01, 'g_out_swa': 2.068863e-01, 'w_o': 1.798277e-01, 'g_ffn2': 7.365656e-02, 'w_ffn2_gate': 3.166989e-02, 'w_ffn2_up': 3.078610e-02, 'w_ffn2_down': 5.102694e-02, 'g_final': 6.390963e+01}


def _to_microbatches(a, axis):
    t = _jnp.moveaxis(a, axis, 0)
    t = t.reshape((N_MICROBATCH, t.shape[0] // N_MICROBATCH) + t.shape[1:])
    return _jnp.moveaxis(t, 1, axis + 1)


def setup_inputs(seed: int = 0) -> dict:
    inp = _fwd_setup_inputs(seed)
    key = _jax.random.fold_in(_jax.random.key(seed), 7919)
    shape, _ = _output_shape()
    out = dict(inp)
    out["loss_target"] = _jax.random.normal(_jax.random.fold_in(key, 0), shape, _jnp.float32)
    for i, name in enumerate(TWIN_WEIGHTS):
        w = inp[name].astype(_jnp.float32)
        if MOMENT_SCALE is None:
            s = _jnp.sqrt(_jnp.mean(_jnp.square(w)) + 1e-30)
        else:
            s = MOMENT_SCALE[name]
        km, kv = _jax.random.split(_jax.random.fold_in(key, i + 1))
        out[name] = w
        out["m_" + name] = s * _jax.random.normal(km, w.shape, _jnp.float32)
        out["v_" + name] = (s * s) * _jax.random.uniform(kv, w.shape, _jnp.float32, 0.5, 1.5)
    if N_MICROBATCH > 1:
        for name, axis in PER_EXAMPLE_BATCH_AXIS.items():
            out[name] = _to_microbatches(out[name], axis)
    return {'x': out['x'], 'g_ffn1': out['g_ffn1'], 'w_ffn1_gate': out['w_ffn1_gate'], 'w_ffn1_up': out['w_ffn1_up'], 'w_ffn1_down': out['w_ffn1_down'], 'g_mix': out['g_mix'], 'w_in': out['w_in'], 'g_q_a': out['g_q_a'], 'w_q_b': out['w_q_b'], 'g_kv_a': out['g_kv_a'], 'w_kv_b': out['w_kv_b'], 'attn_sinks': out['attn_sinks'], 'rel_bias': out['rel_bias'], 'g_out_mla': out['g_out_mla'], 'g_out_swa': out['g_out_swa'], 'w_o': out['w_o'], 'g_ffn2': out['g_ffn2'], 'w_ffn2_gate': out['w_ffn2_gate'], 'w_ffn2_up': out['w_ffn2_up'], 'w_ffn2_down': out['w_ffn2_down'], 'g_final': out['g_final'], 'loss_target': out['loss_target'], 'm_g_ffn1': out['m_g_ffn1'], 'm_w_ffn1_gate': out['m_w_ffn1_gate'], 'm_w_ffn1_up': out['m_w_ffn1_up'], 'm_w_ffn1_down': out['m_w_ffn1_down'], 'm_g_mix': out['m_g_mix'], 'm_w_in': out['m_w_in'], 'm_g_q_a': out['m_g_q_a'], 'm_w_q_b': out['m_w_q_b'], 'm_g_kv_a': out['m_g_kv_a'], 'm_w_kv_b': out['m_w_kv_b'], 'm_attn_sinks': out['m_attn_sinks'], 'm_rel_bias': out['m_rel_bias'], 'm_g_out_mla': out['m_g_out_mla'], 'm_g_out_swa': out['m_g_out_swa'], 'm_w_o': out['m_w_o'], 'm_g_ffn2': out['m_g_ffn2'], 'm_w_ffn2_gate': out['m_w_ffn2_gate'], 'm_w_ffn2_up': out['m_w_ffn2_up'], 'm_w_ffn2_down': out['m_w_ffn2_down'], 'm_g_final': out['m_g_final'], 'v_g_ffn1': out['v_g_ffn1'], 'v_w_ffn1_gate': out['v_w_ffn1_gate'], 'v_w_ffn1_up': out['v_w_ffn1_up'], 'v_w_ffn1_down': out['v_w_ffn1_down'], 'v_g_mix': out['v_g_mix'], 'v_w_in': out['v_w_in'], 'v_g_q_a': out['v_g_q_a'], 'v_w_q_b': out['v_w_q_b'], 'v_g_kv_a': out['v_g_kv_a'], 'v_w_kv_b': out['v_w_kv_b'], 'v_attn_sinks': out['v_attn_sinks'], 'v_rel_bias': out['v_rel_bias'], 'v_g_out_mla': out['v_g_out_mla'], 'v_g_out_swa': out['v_g_out_swa'], 'v_w_o': out['v_w_o'], 'v_g_ffn2': out['v_g_ffn2'], 'v_w_ffn2_gate': out['v_w_ffn2_gate'], 'v_w_ffn2_up': out['v_w_ffn2_up'], 'v_w_ffn2_down': out['v_w_ffn2_down'], 'v_g_final': out['v_g_final']}


def _loss(weights, diff, rest, loss_target):
    with _jax.named_scope("forward"):
        args = {**rest, TWIN_DIFF_INPUT: diff, **{k: w.astype(_WEIGHT_DTYPES[k]) for k, w in weights.items()}}
        y = _forward(args)
    with _jax.named_scope("loss_head"):
        err = _jnp.square(y.astype(_jnp.float32) - loss_target)
        return 0.5 * _jnp.sum(_jnp.mean(err, axis=-1)) if err.ndim else 0.5 * err


def _adamw(w, g, m, v):
    m = ADAM_B1 * m + (1.0 - ADAM_B1) * g
    v = ADAM_B2 * v + (1.0 - ADAM_B2) * _jnp.square(g)
    m_hat = m / (1.0 - ADAM_B1 ** ADAM_STEP)
    v_hat = v / (1.0 - ADAM_B2 ** ADAM_STEP)
    delta = -ADAM_LR * (m_hat / (_jnp.sqrt(v_hat) + ADAM_EPS) + ADAM_WD * w)
    return delta, m, v


def reference(x, g_ffn1, w_ffn1_gate, w_ffn1_up, w_ffn1_down, g_mix, w_in, g_q_a, w_q_b, g_kv_a, w_kv_b, attn_sinks, rel_bias, g_out_mla, g_out_swa, w_o, g_ffn2, w_ffn2_gate, w_ffn2_up, w_ffn2_down, g_final, loss_target, m_g_ffn1, m_w_ffn1_gate, m_w_ffn1_up, m_w_ffn1_down, m_g_mix, m_w_in, m_g_q_a, m_w_q_b, m_g_kv_a, m_w_kv_b, m_attn_sinks, m_rel_bias, m_g_out_mla, m_g_out_swa, m_w_o, m_g_ffn2, m_w_ffn2_gate, m_w_ffn2_up, m_w_ffn2_down, m_g_final, v_g_ffn1, v_w_ffn1_gate, v_w_ffn1_up, v_w_ffn1_down, v_g_mix, v_w_in, v_g_q_a, v_w_q_b, v_g_kv_a, v_w_kv_b, v_attn_sinks, v_rel_bias, v_g_out_mla, v_g_out_swa, v_w_o, v_g_ffn2, v_w_ffn2_gate, v_w_ffn2_up, v_w_ffn2_down, v_g_final):
    given = dict(x=x, g_ffn1=g_ffn1, w_ffn1_gate=w_ffn1_gate, w_ffn1_up=w_ffn1_up, w_ffn1_down=w_ffn1_down, g_mix=g_mix, w_in=w_in, g_q_a=g_q_a, w_q_b=w_q_b, g_kv_a=g_kv_a, w_kv_b=w_kv_b, attn_sinks=attn_sinks, rel_bias=rel_bias, g_out_mla=g_out_mla, g_out_swa=g_out_swa, w_o=w_o, g_ffn2=g_ffn2, w_ffn2_gate=w_ffn2_gate, w_ffn2_up=w_ffn2_up, w_ffn2_down=w_ffn2_down, g_final=g_final, loss_target=loss_target, m_g_ffn1=m_g_ffn1, m_w_ffn1_gate=m_w_ffn1_gate, m_w_ffn1_up=m_w_ffn1_up, m_w_ffn1_down=m_w_ffn1_down, m_g_mix=m_g_mix, m_w_in=m_w_in, m_g_q_a=m_g_q_a, m_w_q_b=m_w_q_b, m_g_kv_a=m_g_kv_a, m_w_kv_b=m_w_kv_b, m_attn_sinks=m_attn_sinks, m_rel_bias=m_rel_bias, m_g_out_mla=m_g_out_mla, m_g_out_swa=m_g_out_swa, m_w_o=m_w_o, m_g_ffn2=m_g_ffn2, m_w_ffn2_gate=m_w_ffn2_gate, m_w_ffn2_up=m_w_ffn2_up, m_w_ffn2_down=m_w_ffn2_down, m_g_final=m_g_final, v_g_ffn1=v_g_ffn1, v_w_ffn1_gate=v_w_ffn1_gate, v_w_ffn1_up=v_w_ffn1_up, v_w_ffn1_down=v_w_ffn1_down, v_g_mix=v_g_mix, v_w_in=v_w_in, v_g_q_a=v_g_q_a, v_w_q_b=v_w_q_b, v_g_kv_a=v_g_kv_a, v_w_kv_b=v_w_kv_b, v_attn_sinks=v_attn_sinks, v_rel_bias=v_rel_bias, v_g_out_mla=v_g_out_mla, v_g_out_swa=v_g_out_swa, v_w_o=v_w_o, v_g_ffn2=v_g_ffn2, v_w_ffn2_gate=v_w_ffn2_gate, v_w_ffn2_up=v_w_ffn2_up, v_w_ffn2_down=v_w_ffn2_down, v_g_final=v_g_final)
    weights = {n: given[n] for n in TWIN_WEIGHTS}
    shared = {n: given[n] for n in SHARED_INPUTS}
    per_example = {n: given[n] for n in ['x']}
    grad_fn = _jax.value_and_grad(_loss, argnums=(0, 1))

    def one_microbatch(ex, loss_target):
        ex = dict(ex)
        diff = ex.pop(TWIN_DIFF_INPUT)
        return grad_fn(weights, diff, {**shared, **ex}, loss_target)

    if N_MICROBATCH == 1:
        loss, (grad_w, grad_x) = one_microbatch(per_example, given["loss_target"])
    else:
        def body(carry, xs):
            loss_sum, grad_sum = carry
            l_k, (gw_k, gx_k) = one_microbatch(xs[0], xs[1])
            with _jax.named_scope("update"):
                return (loss_sum + l_k, _jax.tree.map(_jnp.add, grad_sum, gw_k)), gx_k

        init = (_jnp.zeros((), _jnp.float32), _jax.tree.map(_jnp.zeros_like, weights))
        (loss, grad_w), grad_x = _jax.lax.scan(body, init, (per_example, given["loss_target"]))
    with _jax.named_scope("update"):
        delta_w, new_m, new_v = {}, {}, {}
        for n in TWIN_WEIGHTS:
            delta_w[n], new_m[n], new_v[n] = _adamw(weights[n], grad_w[n], given["m_" + n], given["v_" + n])
    return (loss, grad_x, *[grad_w[n] for n in TWIN_WEIGHTS], *[delta_w[n] for n in TWIN_WEIGHTS],
            *[new_m[n] for n in TWIN_WEIGHTS], *[new_v[n] for n in TWIN_WEIGHTS])
```

```python
import functools
import math

import jax
import jax.numpy as jnp
from jax import lax
from jax.experimental import pallas as pl
from jax.experimental.pallas import tpu as pltpu

F32 = jnp.float32
BF16 = jnp.bfloat16
MESH = pl.DeviceIdType.MESH

EPS = 1e-6
NEG = -1e30
N_DEV = 8
N_CHIP = 4

MLA_HEADS = 8
Q_LORA = 256
KV_LORA = 128
QK_NOPE = 64
QK_ROPE = 32
V_HEAD = 64
ROPE_THETA = 10000.0
SWA_HEADS = 8
SWA_KV_HEADS = 2
SWA_HEAD_DIM = 64
WINDOW = 128
REL_BUCKETS = 32
REL_MAX_DIST = 128
HEAD_BLOCK = 128
MLA_SCALE = (QK_NOPE + QK_ROPE) ** -0.5
SWA_SCALE = SWA_HEAD_DIM ** -0.5

ADAM_LR = 0.001
ADAM_B1 = 0.9
ADAM_B2 = 0.999
ADAM_EPS = 1e-08
ADAM_WD = 0.01
ADAM_STEP = 10

VMEM_LIMIT = 56 * 1024 * 1024
SMALL_ROWS = 48


def _cparams(semantics=None):
    return pltpu.CompilerParams(dimension_semantics=semantics, vmem_limit_bytes=VMEM_LIMIT)


def _dot(a, b):
    return jnp.dot(a, b, preferred_element_type=F32)


def _dot_nt(a, b):
    return lax.dot_general(a, b, (((1,), (1,)), ((), ())), preferred_element_type=F32)


def _dot_tn(a, b):
    return lax.dot_general(a, b, (((0,), (0,)), ((), ())), preferred_element_type=F32)


def _rms_fwd(x, g):
    r = lax.rsqrt(jnp.mean(x * x, axis=-1, keepdims=True) + EPS)
    return x * r * g, r


def _rms_bwd(x, g, r, dy):
    xh = x * r
    dyg = dy * g
    dx = r * (dyg - xh * jnp.mean(dyg * xh, axis=-1, keepdims=True))
    return dx, jnp.sum(dy * xh, axis=0, keepdims=True)


def _swap16(x):
    n = x.shape[-1]
    up = pltpu.roll(x, n - 16, 1)
    down = pltpu.roll(x, 16, 1)
    lane = lax.broadcasted_iota(jnp.int32, x.shape, 1) % HEAD_BLOCK
    return jnp.where((lane >= 64) & (lane < 80), up, jnp.where((lane >= 80) & (lane < 96), down, 0.0))


def _const_spec(shape):
    return pl.BlockSpec(shape, lambda *_: (0,) * len(shape), pipeline_mode=pl.Buffered(1))


def _acc_spec(shape):
    return pl.BlockSpec(shape, lambda *_: (0,) * len(shape))


def _row_spec(tm, width):
    return pl.BlockSpec((tm, width), lambda i: (i, 0))


def _cast_bf16(arrays):
    n = len(arrays)

    def body(*refs):
        for src, dst in zip(refs[:n], refs[n:]):
            dst[...] = src[...].astype(BF16)

    return pl.pallas_call(
        body, name="cast_weights",
        out_shape=[jax.ShapeDtypeStruct(a.shape, BF16) for a in arrays],
        compiler_params=_cparams(),
    )(*arrays)


def _all_gather(arrays, name):
    n = len(arrays)

    def body(*refs):
        ins, outs = refs[:n], refs[n:2 * n]
        send_sems, recv_sems, local_sems = refs[2 * n:]
        x, y, c = lax.axis_index("x"), lax.axis_index("y"), lax.axis_index("c")
        me, sibling = (x, y, c), (x, y, 1 - c)
        chips = [(1 - x, y), (x, 1 - y), (1 - x, 1 - y)]

        def slot(a, dev):
            return outs[a].at[4 * dev[0] + 2 * dev[1] + dev[2]]

        def copy(a, k, block, to, src=None):
            return pltpu.make_async_remote_copy(
                src_ref=slot(a, block) if src is None else src, dst_ref=slot(a, block),
                send_sem=send_sems.at[a, k], recv_sem=recv_sems.at[a, k], device_id=to, device_id_type=MESH)

        mine = [pltpu.make_async_copy(ins[a], slot(a, me), local_sems.at[a]) for a in range(n)]
        for cp in mine:
            cp.start()
        first = []
        for a in range(n):
            first.append(copy(a, 0, me, sibling, src=ins[a]))
            first += [copy(a, 1 + j, me, (*chip, c), src=ins[a]) for j, chip in enumerate(chips)]
        for cp in first:
            cp.start()
        passed = []
        for j, chip in enumerate(chips):
            for a in range(n):
                copy(a, 1 + j, (*chip, c), me).wait_recv()
                cp = copy(a, 4 + j, (*chip, c), sibling)
                cp.start()
                passed.append(cp)
        for a in range(n):
            copy(a, 0, sibling, me).wait_recv()
            for j, chip in enumerate(chips):
                copy(a, 4 + j, (*chip, 1 - c), me).wait_recv()
        for cp in first + passed:
            cp.wait_send()
        for cp in mine:
            cp.wait()

    any_spec = pl.BlockSpec(memory_space=pl.ANY)
    return pl.pallas_call(
        body, name=name,
        out_shape=[jax.ShapeDtypeStruct((N_DEV,) + a.shape, a.dtype) for a in arrays],
        in_specs=[any_spec] * n, out_specs=[any_spec] * n,
        scratch_shapes=[pltpu.SemaphoreType.DMA((n, 7)), pltpu.SemaphoreType.DMA((n, 7)), pltpu.SemaphoreType.DMA((n,))],
    )(*arrays)


def _exchange_cores(arrays):
    n = len(arrays)

    def body(*refs):
        ins, outs = refs[:n], refs[n:2 * n]
        send_sems, recv_sems = refs[2 * n:]
        x, y, c = lax.axis_index("x"), lax.axis_index("y"), lax.axis_index("c")
        copies = [pltpu.make_async_remote_copy(
            src_ref=ins[a].at[1 - c], dst_ref=outs[a], send_sem=send_sems.at[a], recv_sem=recv_sems.at[a],
            device_id=(x, y, 1 - c), device_id_type=MESH) for a in range(n)]
        for cp in copies:
            cp.start()
        for cp in copies:
            cp.wait()

    any_spec = pl.BlockSpec(memory_space=pl.ANY)
    return pl.pallas_call(
        body, name="grad_exchange_cores",
        out_shape=[jax.ShapeDtypeStruct(a.shape[1:], a.dtype) for a in arrays],
        in_specs=[any_spec] * n, out_specs=[any_spec] * n,
        scratch_shapes=[pltpu.SemaphoreType.DMA((n,)), pltpu.SemaphoreType.DMA((n,))],
    )(*arrays)


def _exchange_chips(arrays):
    n = len(arrays)

    def body(*refs):
        ins, outs = refs[:n], refs[n:2 * n]
        send_sems, recv_sems, local_sems = refs[2 * n:]
        x, y, c = lax.axis_index("x"), lax.axis_index("y"), lax.axis_index("c")
        my_chip = 2 * x + y
        chips = [(1 - x, y), (x, 1 - y), (1 - x, 1 - y)]
        mine = [pltpu.make_async_copy(ins[a].at[my_chip], outs[a].at[my_chip], local_sems.at[a]) for a in range(n)]
        for cp in mine:
            cp.start()
        def copy(a, j, src_chip, dst_chip, to):
            return pltpu.make_async_remote_copy(
                src_ref=ins[a].at[dst_chip], dst_ref=outs[a].at[src_chip],
                send_sem=send_sems.at[a, j], recv_sem=recv_sems.at[a, j], device_id=to, device_id_type=MESH)

        sends = [copy(a, j, my_chip, 2 * chip[0] + chip[1], (*chip, c)) for a in range(n) for j, chip in enumerate(chips)]
        for cp in sends:
            cp.start()
        for a in range(n):
            for j, chip in enumerate(chips):
                copy(a, j, 2 * chip[0] + chip[1], my_chip, (x, y, c)).wait_recv()
        for cp in sends:
            cp.wait_send()
        for cp in mine:
            cp.wait()

    any_spec = pl.BlockSpec(memory_space=pl.ANY)
    return pl.pallas_call(
        body, name="grad_exchange_chips",
        out_shape=[jax.ShapeDtypeStruct(a.shape, a.dtype) for a in arrays],
        in_specs=[any_spec] * n, out_specs=[any_spec] * n,
        scratch_shapes=[pltpu.SemaphoreType.DMA((n, 3)), pltpu.SemaphoreType.DMA((n, 3)), pltpu.SemaphoreType.DMA((n,))],
    )(*arrays)


def _add_core_halves(mine_all, recv, core):
    _, nchip, r, c = mine_all.shape
    tr = r if r <= 512 else (512 if r % 512 == 0 else 352)
    assert r % tr == 0

    def body(core_ref, a_ref, b_ref, o32_ref, o16_ref):
        s = a_ref[...] + b_ref[...]
        o32_ref[...] = s
        o16_ref[...] = s.astype(BF16)

    grid_spec = pltpu.PrefetchScalarGridSpec(
        num_scalar_prefetch=1, grid=(nchip, r // tr),
        in_specs=[pl.BlockSpec((None, None, tr, c), lambda q, i, s: (s[0], q, i, 0)),
                  pl.BlockSpec((None, tr, c), lambda q, i, s: (q, i, 0))],
        out_specs=[pl.BlockSpec((None, tr, c), lambda q, i, s: (q, i, 0)),
                   pl.BlockSpec((None, tr, c), lambda q, i, s: (q, i, 0))])
    return pl.pallas_call(
        body, name="grad_add_cores", grid_spec=grid_spec,
        out_shape=[jax.ShapeDtypeStruct((nchip, r, c), F32), jax.ShapeDtypeStruct((nchip, r, c), BF16)],
        compiler_params=_cparams(("arbitrary", "arbitrary")),
    )(core, mine_all, recv)


def _adam_math(w, g, m, v):
    m = ADAM_B1 * m + (1.0 - ADAM_B1) * g
    v = ADAM_B2 * v + (1.0 - ADAM_B2) * (g * g)
    m_hat = m / (1.0 - ADAM_B1 ** ADAM_STEP)
    v_hat = v / (1.0 - ADAM_B2 ** ADAM_STEP)
    delta = -ADAM_LR * (m_hat / (jnp.sqrt(v_hat) + ADAM_EPS) + ADAM_WD * w)
    return delta, m, v


def _adam_big(w, m, v, sum32, recv16, chip):
    r, c = w.shape
    tr = r if r <= 512 else (512 if r % 512 == 0 else 352)
    assert r % tr == 0

    def body(chip_ref, w_ref, m_ref, v_ref, s_ref, r_ref, g_out, d_out, m_out, v_out):
        me = chip_ref[0]
        g = None
        for q in range(N_CHIP):
            term = jnp.where(q == me, s_ref[...], r_ref[q].astype(F32))
            g = term if g is None else g + term
        d, mn, vn = _adam_math(w_ref[...], g, m_ref[...], v_ref[...])
        g_out[...] = g
        d_out[...] = d
        m_out[...] = mn
        v_out[...] = vn

    blk = pl.BlockSpec((tr, c), lambda i, s: (i, 0))
    grid_spec = pltpu.PrefetchScalarGridSpec(
        num_scalar_prefetch=1, grid=(r // tr,),
        in_specs=[blk, blk, blk, pl.BlockSpec((None, tr, c), lambda i, s: (s[0], i, 0)),
                  pl.BlockSpec((N_CHIP, tr, c), lambda i, s: (0, i, 0))],
        out_specs=[blk, blk, blk, blk])
    return pl.pallas_call(
        body, name="adamw_big", grid_spec=grid_spec,
        out_shape=[jax.ShapeDtypeStruct((r, c), F32)] * 4,
        compiler_params=_cparams(("arbitrary",)),
    )(chip, w, m, v, sum32, recv16)


def _adam_small(w, m, v, gathered):
    def body(w_ref, m_ref, v_ref, p_ref, g_out, d_out, m_out, v_out):
        g = p_ref[0]
        for dev in range(1, N_DEV):
            g = g + p_ref[dev]
        d, mn, vn = _adam_math(w_ref[...], g, m_ref[...], v_ref[...])
        g_out[...] = g
        d_out[...] = d
        m_out[...] = mn
        v_out[...] = vn

    return pl.pallas_call(
        body, name="adamw_small",
        out_shape=[jax.ShapeDtypeStruct(w.shape, F32)] * 4,
        compiler_params=_cparams(),
    )(w, m, v, gathered)


def _pick(n, prefer):
    for t in prefer:
        if n % t == 0:
            return t
    return n


def _ffn_fwd(x, g, wg, wu, wd, name):
    T, D = x.shape
    F = wg.shape[1]
    tm = _pick(T, (512, 256, 128))
    fc = _pick(F, (256, 128))

    def body(x_ref, g_ref, wg_ref, wu_ref, wd_ref, h_ref, n_ref, gate_ref, up_ref):
        xv = x_ref[...]
        n, _ = _rms_fwd(xv, g_ref[...])
        nb = n.astype(BF16)
        n_ref[...] = nb
        acc = jnp.zeros((tm, D), F32)
        for f0 in range(0, F, fc):
            gate = _dot(nb, wg_ref[:, f0:f0 + fc])
            up = _dot(nb, wu_ref[:, f0:f0 + fc])
            gate_ref[:, f0:f0 + fc] = gate.astype(BF16)
            up_ref[:, f0:f0 + fc] = up.astype(BF16)
            act = gate * (1.0 / (1.0 + jnp.exp(-gate))) * up
            acc = acc + _dot(act.astype(BF16), wd_ref[f0:f0 + fc, :])
        h_ref[...] = xv + 0.5 * acc

    return pl.pallas_call(
        body, name=name, grid=(T // tm,),
        in_specs=[_row_spec(tm, D), _const_spec((1, D)), _const_spec((D, F)), _const_spec((D, F)), _const_spec((F, D))],
        out_specs=[_row_spec(tm, D), _row_spec(tm, D), _row_spec(tm, F), _row_spec(tm, F)],
        out_shape=[jax.ShapeDtypeStruct((T, D), F32), jax.ShapeDtypeStruct((T, D), BF16),
                   jax.ShapeDtypeStruct((T, F), BF16), jax.ShapeDtypeStruct((T, F), BF16)],
        compiler_params=_cparams(("arbitrary",)),
    )(x, g, wg, wu, wd)


def _ffn_bwd(x, g, gate, up, dh, wg, wu, wd, name):
    T, D = x.shape
    F = wg.shape[1]
    tm = _pick(T, (256, 128))
    fc = _pick(F, (256, 128))

    def body(x_ref, g_ref, gate_ref, up_ref, dh_ref, wg_ref, wu_ref, wd_ref, dx_ref, dg_ref, act_ref, dgate_ref, dup_ref):
        xv = x_ref[...]
        gv = g_ref[...]
        _, r = _rms_fwd(xv, gv)
        dhv = dh_ref[...]
        dho = (0.5 * dhv).astype(BF16)
        dn = jnp.zeros((tm, D), F32)
        for f0 in range(0, F, fc):
            d_act = _dot_nt(dho, wd_ref[f0:f0 + fc, :])
            gt = gate_ref[:, f0:f0 + fc].astype(F32)
            u = up_ref[:, f0:f0 + fc].astype(F32)
            sg = 1.0 / (1.0 + jnp.exp(-gt))
            silu = gt * sg
            act_ref[:, f0:f0 + fc] = (silu * u).astype(BF16)
            d_up = (d_act * silu).astype(BF16)
            d_gate = (d_act * u * (sg + silu * (1.0 - sg))).astype(BF16)
            dup_ref[:, f0:f0 + fc] = d_up
            dgate_ref[:, f0:f0 + fc] = d_gate
            dn = dn + _dot_nt(d_gate, wg_ref[:, f0:f0 + fc]) + _dot_nt(d_up, wu_ref[:, f0:f0 + fc])
        dxn, dgp = _rms_bwd(xv, gv, r, dn)
        dx_ref[...] = dhv + dxn

        @pl.when(pl.program_id(0) == 0)
        def _():
            dg_ref[...] = jnp.zeros_like(dg_ref)

        dg_ref[...] += dgp

    return pl.pallas_call(
        body, name=name, grid=(T // tm,),
        in_specs=[_row_spec(tm, D), _const_spec((1, D)), _row_spec(tm, F), _row_spec(tm, F), _row_spec(tm, D),
                  _const_spec((D, F)), _const_spec((D, F)), _const_spec((F, D))],
        out_specs=[_row_spec(tm, D), _acc_spec((1, D)), _row_spec(tm, F), _row_spec(tm, F), _row_spec(tm, F)],
        out_shape=[jax.ShapeDtypeStruct((T, D), F32), jax.ShapeDtypeStruct((1, D), F32),
                   jax.ShapeDtypeStruct((T, F), BF16), jax.ShapeDtypeStruct((T, F), BF16), jax.ShapeDtypeStruct((T, F), BF16)],
        compiler_params=_cparams(("arbitrary",)),
    )(x, g, gate, up, dh, wg, wu, wd)


def _matmul_tn(a, b, name, scale=None):
    T, K = a.shape
    N = b.shape[1]
    tk = _pick(K, (1024, 1408, 512, 256, 128))
    tn = _pick(N, (1024, 1408, 1280, 768, 512, 256, 128))
    tt = _pick(T, (512, 256, 128))
    n_t = T // tt

    def body(a_ref, b_ref, o_ref):
        @pl.when(pl.program_id(2) == 0)
        def _():
            o_ref[...] = jnp.zeros_like(o_ref)

        o_ref[...] += _dot_tn(a_ref[...].astype(BF16), b_ref[...].astype(BF16))

        if scale is not None:
            @pl.when(pl.program_id(2) == n_t - 1)
            def _():
                o_ref[...] = o_ref[...] * scale

    return pl.pallas_call(
        body, name=name, grid=(K // tk, N // tn, n_t),
        in_specs=[pl.BlockSpec((tt, tk), lambda i, j, t: (t, i)), pl.BlockSpec((tt, tn), lambda i, j, t: (t, j))],
        out_specs=pl.BlockSpec((tk, tn), lambda i, j, t: (i, j)),
        out_shape=jax.ShapeDtypeStruct((K, N), F32),
        compiler_params=_cparams(("arbitrary", "arbitrary", "arbitrary")),
    )(a, b)


def _rope_spec(tm, S):
    per = S // tm
    return pl.BlockSpec((tm, HEAD_BLOCK), lambda i: (i % per, 0))


def _proj_fwd(h1, g_mix, w_in, g_q, wq, g_kv, wkv, rope_c, rope_s, S):
    T, D = h1.shape
    P = w_in.shape[1]
    tm = _pick(S, (512, 256, 128))
    QW = MLA_HEADS * HEAD_BLOCK
    VW = MLA_HEADS * V_HEAD
    SQ = SWA_HEADS * SWA_HEAD_DIM
    SK = SWA_KV_HEADS * SWA_HEAD_DIM
    o_cq, o_ckv, o_qs, o_ks, o_vs, o_kpe = 0, Q_LORA, Q_LORA + KV_LORA, Q_LORA + KV_LORA + SQ, Q_LORA + KV_LORA + SQ + SK, Q_LORA + KV_LORA + SQ + 2 * SK
    assert P == o_kpe + HEAD_BLOCK

    def body(h_ref, gm_ref, win_ref, gq_ref, wq_ref, gkv_ref, wkv_ref, c_ref, s_ref,
             u_ref, cq_ref, ckv_ref, cqn_ref, ckvn_ref, q_ref, k_ref, v_ref, qs_ref, ks_ref, vs_ref):
        u, _ = _rms_fwd(h_ref[...], gm_ref[...])
        ub = u.astype(BF16)
        u_ref[...] = ub
        proj = _dot(ub, win_ref[...])
        c_q = proj[:, o_cq:o_ckv]
        c_kv = proj[:, o_ckv:o_qs]
        cq_ref[...] = c_q
        ckv_ref[...] = c_kv
        qs_ref[...] = proj[:, o_qs:o_ks].astype(BF16)
        ks_ref[...] = proj[:, o_ks:o_vs].astype(BF16)
        vs_ref[...] = proj[:, o_vs:o_kpe].astype(BF16)
        cb, sb = c_ref[...], s_ref[...]
        kpe = proj[:, o_kpe:P]
        kpe = kpe * cb + _swap16(kpe) * sb
        cqn, _ = _rms_fwd(c_q, gq_ref[...])
        cqn = cqn.astype(BF16)
        cqn_ref[...] = cqn
        q = _dot(cqn, wq_ref[...])
        q = q * jnp.tile(cb, (1, MLA_HEADS)) + _swap16(q) * jnp.tile(sb, (1, MLA_HEADS))
        q_ref[...] = q.astype(BF16)
        ckvn, _ = _rms_fwd(c_kv, gkv_ref[...])
        ckvn = ckvn.astype(BF16)
        ckvn_ref[...] = ckvn
        kv = _dot(ckvn, wkv_ref[...])
        k_ref[...] = (kv[:, :QW] + jnp.tile(kpe, (1, MLA_HEADS))).astype(BF16)
        v_ref[...] = kv[:, QW:].astype(BF16)

    widths = [(D, BF16), (Q_LORA, F32), (KV_LORA, F32), (Q_LORA, BF16), (KV_LORA, BF16), (QW, BF16), (QW, BF16), (VW, BF16),
              (SQ, BF16), (SK, BF16), (SK, BF16)]
    return pl.pallas_call(
        body, name="proj_fwd", grid=(T // tm,),
        in_specs=[_row_spec(tm, D), _const_spec((1, D)), _const_spec((D, P)), _const_spec((1, Q_LORA)),
                  _const_spec((Q_LORA, QW)), _const_spec((1, KV_LORA)), _const_spec((KV_LORA, QW + VW)),
                  _rope_spec(tm, S), _rope_spec(tm, S)],
        out_specs=[_row_spec(tm, w) for w, _ in widths],
        out_shape=[jax.ShapeDtypeStruct((T, w), dt) for w, dt in widths],
        compiler_params=_cparams(("arbitrary",)),
    )(h1, g_mix, w_in, g_q, wq, g_kv, wkv, rope_c, rope_s)


def _proj_bwd(dq, dk, dv, dqs, dks, dvs, c_q, c_kv, h1, dh2, g_mix, w_in, g_q, wq, g_kv, wkv, rope_c, rope_s, S):
    T, D = h1.shape
    P = w_in.shape[1]
    tm = _pick(S, (256, 128))
    QW = MLA_HEADS * HEAD_BLOCK
    VW = MLA_HEADS * V_HEAD
    SQ = SWA_HEADS * SWA_HEAD_DIM
    SK = SWA_KV_HEADS * SWA_HEAD_DIM

    def body(dq_ref, dk_ref, dv_ref, dqs_ref, dks_ref, dvs_ref, cq_ref, ckv_ref, h_ref, dh2_ref, gm_ref, win_ref, gq_ref,
             wq_ref, gkv_ref, wkv_ref, c_ref, s_ref, dh1_ref, dproj_ref, dql_ref, dkv_ref, dgm_ref, dgq_ref, dgkv_ref):
        cb, sb = c_ref[...], s_ref[...]
        dqv = dq_ref[...]
        dql = dqv * jnp.tile(cb, (1, MLA_HEADS)) + _swap16(dqv * jnp.tile(sb, (1, MLA_HEADS)))
        dql = dql.astype(BF16)
        dql_ref[...] = dql
        c_q = cq_ref[...]
        _, rq = _rms_fwd(c_q, gq_ref[...])
        d_cq, dgq = _rms_bwd(c_q, gq_ref[...], rq, _dot_nt(dql, wq_ref[...]))

        dkv_all = dk_ref[...]
        dkpe = dkv_all[:, 0:HEAD_BLOCK]
        for h in range(1, MLA_HEADS):
            dkpe = dkpe + dkv_all[:, h * HEAD_BLOCK:(h + 1) * HEAD_BLOCK]
        lane = lax.broadcasted_iota(jnp.int32, dkpe.shape, 1)
        dkpe = jnp.where((lane >= 64) & (lane < 96), dkpe, 0.0)
        dkpe = dkpe * cb + _swap16(dkpe * sb)
        dkvc = jnp.concatenate([dkv_all.astype(BF16), dv_ref[...].astype(BF16)], axis=1)
        dkv_ref[...] = dkvc
        c_kv = ckv_ref[...]
        _, rkv = _rms_fwd(c_kv, gkv_ref[...])
        d_ckv, dgkv = _rms_bwd(c_kv, gkv_ref[...], rkv, _dot_nt(dkvc, wkv_ref[...]))

        dproj = jnp.concatenate([d_cq.astype(BF16), d_ckv.astype(BF16), dqs_ref[...].astype(BF16),
                                 dks_ref[...].astype(BF16), dvs_ref[...].astype(BF16), dkpe.astype(BF16)], axis=1)
        dproj_ref[...] = dproj
        hv = h_ref[...]
        _, rm = _rms_fwd(hv, gm_ref[...])
        dxn, dgm = _rms_bwd(hv, gm_ref[...], rm, _dot_nt(dproj, win_ref[...]))
        dh1_ref[...] = dh2_ref[...] + dxn

        @pl.when(pl.program_id(0) == 0)
        def _():
            dgm_ref[...] = jnp.zeros_like(dgm_ref)
            dgq_ref[...] = jnp.zeros_like(dgq_ref)
            dgkv_ref[...] = jnp.zeros_like(dgkv_ref)

        dgm_ref[...] += dgm
        dgq_ref[...] += dgq
        dgkv_ref[...] += dgkv

    return pl.pallas_call(
        body, name="proj_bwd", grid=(T // tm,),
        in_specs=[_row_spec(tm, QW), _row_spec(tm, QW), _row_spec(tm, VW), _row_spec(tm, SQ), _row_spec(tm, SK), _row_spec(tm, SK),
                  _row_spec(tm, Q_LORA), _row_spec(tm, KV_LORA), _row_spec(tm, D), _row_spec(tm, D),
                  _const_spec((1, D)), _const_spec((D, P)), _const_spec((1, Q_LORA)), _const_spec((Q_LORA, QW)),
                  _const_spec((1, KV_LORA)), _const_spec((KV_LORA, QW + VW)), _rope_spec(tm, S), _rope_spec(tm, S)],
        out_specs=[_row_spec(tm, D), _row_spec(tm, P), _row_spec(tm, QW), _row_spec(tm, QW + VW),
                   _acc_spec((1, D)), _acc_spec((1, Q_LORA)), _acc_spec((1, KV_LORA))],
        out_shape=[jax.ShapeDtypeStruct((T, D), F32), jax.ShapeDtypeStruct((T, P), BF16), jax.ShapeDtypeStruct((T, QW), BF16),
                   jax.ShapeDtypeStruct((T, QW + VW), BF16), jax.ShapeDtypeStruct((1, D), F32),
                   jax.ShapeDtypeStruct((1, Q_LORA), F32), jax.ShapeDtypeStruct((1, KV_LORA), F32)],
        compiler_params=_cparams(("arbitrary",)),
    )(dq, dk, dv, dqs, dks, dvs, c_q, c_kv, h1, dh2, g_mix, w_in, g_q, wq, g_kv, wkv, rope_c, rope_s)


def _out_fwd(o_mla, o_swa, g_mla, g_swa, w_o, h1):
    T, D = h1.shape
    W = o_mla.shape[1]
    tm = _pick(T, (512, 256, 128))

    def body(om_ref, os_ref, gm_ref, gs_ref, wo_ref, h_ref, h2_ref, oc_ref):
        a, _ = _rms_fwd(om_ref[...], gm_ref[...])
        b, _ = _rms_fwd(os_ref[...], gs_ref[...])
        oc = jnp.concatenate([a.astype(BF16), b.astype(BF16)], axis=1)
        oc_ref[...] = oc
        h2_ref[...] = h_ref[...] + _dot(oc, wo_ref[...])

    return pl.pallas_call(
        body, name="out_fwd", grid=(T // tm,),
        in_specs=[_row_spec(tm, W), _row_spec(tm, W), _const_spec((1, W)), _const_spec((1, W)), _const_spec((2 * W, D)),
                  _row_spec(tm, D)],
        out_specs=[_row_spec(tm, D), _row_spec(tm, 2 * W)],
        out_shape=[jax.ShapeDtypeStruct((T, D), F32), jax.ShapeDtypeStruct((T, 2 * W), BF16)],
        compiler_params=_cparams(("arbitrary",)),
    )(o_mla, o_swa, g_mla, g_swa, w_o, h1)


def _out_bwd(dh2, o_mla, o_swa, g_mla, g_swa, w_o):
    T, D = dh2.shape
    W = o_mla.shape[1]
    tm = _pick(T, (512, 256, 128))

    def body(dh_ref, om_ref, os_ref, gm_ref, gs_ref, wo_ref, dom_ref, dos_ref, dgm_ref, dgs_ref):
        doc = _dot_nt(dh_ref[...].astype(BF16), wo_ref[...])
        om, osw = om_ref[...], os_ref[...]
        _, ra = _rms_fwd(om, gm_ref[...])
        _, rb = _rms_fwd(osw, gs_ref[...])
        da, dga = _rms_bwd(om, gm_ref[...], ra, doc[:, :W])
        db, dgb = _rms_bwd(osw, gs_ref[...], rb, doc[:, W:])
        dom_ref[...] = da
        dos_ref[...] = db

        @pl.when(pl.program_id(0) == 0)
        def _():
            dgm_ref[...] = jnp.zeros_like(dgm_ref)
            dgs_ref[...] = jnp.zeros_like(dgs_ref)

        dgm_ref[...] += dga
        dgs_ref[...] += dgb

    return pl.pallas_call(
        body, name="out_bwd", grid=(T // tm,),
        in_specs=[_row_spec(tm, D), _row_spec(tm, W), _row_spec(tm, W), _const_spec((1, W)), _const_spec((1, W)),
                  _const_spec((2 * W, D))],
        out_specs=[_row_spec(tm, W), _row_spec(tm, W), _acc_spec((1, W)), _acc_spec((1, W))],
        out_shape=[jax.ShapeDtypeStruct((T, W), F32), jax.ShapeDtypeStruct((T, W), F32),
                   jax.ShapeDtypeStruct((1, W), F32), jax.ShapeDtypeStruct((1, W), F32)],
        compiler_params=_cparams(("arbitrary",)),
    )(dh2, o_mla, o_swa, g_mla, g_swa, w_o)


def _loss_head(h3, target, g_final):
    T, D = h3.shape
    tm = _pick(T, (512, 256, 128))

    def body(h_ref, t_ref, g_ref, dh_ref, loss_ref, dg_ref):
        hv, gv = h_ref[...], g_ref[...]
        y, r = _rms_fwd(hv, gv)
        diff = y - t_ref[...]
        dxn, dgp = _rms_bwd(hv, gv, r, diff * (1.0 / D))
        dh_ref[...] = dxn
        part = 0.5 * jnp.sum(jnp.sum(diff * diff, axis=-1, keepdims=True) * (1.0 / D), axis=0, keepdims=True)

        @pl.when(pl.program_id(0) == 0)
        def _():
            loss_ref[...] = jnp.zeros_like(loss_ref)
            dg_ref[...] = jnp.zeros_like(dg_ref)

        loss_ref[...] += jnp.broadcast_to(part, loss_ref.shape)
        dg_ref[...] += dgp

    return pl.pallas_call(
        body, name="loss_head", grid=(T // tm,),
        in_specs=[_row_spec(tm, D), _row_spec(tm, D), _const_spec((1, D))],
        out_specs=[_row_spec(tm, D), _acc_spec((1, 128)), _acc_spec((1, D))],
        out_shape=[jax.ShapeDtypeStruct((T, D), F32), jax.ShapeDtypeStruct((1, 128), F32), jax.ShapeDtypeStruct((1, D), F32)],
        compiler_params=_cparams(("arbitrary",)),
    )(h3, target, g_final)


def _half_mask(shape, half):
    lane = lax.broadcasted_iota(jnp.int32, shape, 1)
    return (lane < 64) if half == 0 else (lane >= 64)


def _mla_fwd(q, k, v, B, S):
    T = B * S
    tq = _pick(S, (256, 128))
    nq = S // tq
    pairs = MLA_HEADS // 2

    def body(q_ref, k_ref, v_ref, o_ref, lse_ref):
        def q_tile(qi, _):
            q0 = pl.multiple_of(qi * tq, tq)
            qpos = q0 + lax.broadcasted_iota(jnp.int32, (tq, tq), 0)
            outs, lses = [], []
            for half in range(2):
                lanes = slice(half * HEAD_BLOCK, (half + 1) * HEAD_BLOCK)
                qv = q_ref[pl.ds(q0, tq), lanes]

                def k_tile(kj, carry):
                    m, l, acc = carry
                    k0 = pl.multiple_of(kj * tq, tq)
                    s = _dot_nt(qv, k_ref[pl.ds(k0, tq), lanes]) * MLA_SCALE
                    kpos = k0 + lax.broadcasted_iota(jnp.int32, (tq, tq), 1)
                    s = jnp.where(kpos <= qpos, s, NEG)
                    m_new = jnp.maximum(m, jnp.max(s, axis=-1, keepdims=True))
                    p = jnp.exp(s - m_new)
                    alpha = jnp.exp(m - m_new)
                    l = alpha * l + jnp.sum(p, axis=-1, keepdims=True)
                    acc = alpha * acc + _dot(p.astype(BF16), v_ref[pl.ds(k0, tq), :])
                    return m_new, l, acc

                m, l, acc = lax.fori_loop(
                    0, qi + 1, k_tile,
                    (jnp.full((tq, 1), NEG, F32), jnp.zeros((tq, 1), F32), jnp.zeros((tq, 2 * V_HEAD), F32)))
                outs.append(acc / l)
                lses.append(jnp.broadcast_to(m + jnp.log(l), (tq, 2 * V_HEAD)))
            low = _half_mask((tq, 2 * V_HEAD), 0)
            o_ref[pl.ds(q0, tq), :] = jnp.where(low, outs[0], outs[1])
            lse_ref[pl.ds(q0, tq), :] = jnp.where(low, lses[0], lses[1])
            return 0

        lax.fori_loop(0, nq, q_tile, 0)

    blk2 = pl.BlockSpec((S, 2 * HEAD_BLOCK), lambda b, j: (b, j))
    blk1 = pl.BlockSpec((S, 2 * V_HEAD), lambda b, j: (b, j))
    return pl.pallas_call(
        body, name="mla_fwd", grid=(B, pairs),
        in_specs=[blk2, blk2, blk1], out_specs=[blk1, blk1],
        out_shape=[jax.ShapeDtypeStruct((T, pairs * 2 * V_HEAD), F32)] * 2,
        compiler_params=_cparams(("arbitrary", "arbitrary")),
    )(q, k, v)


def _mla_bwd(q, k, v, do, o, lse, B, S):
    T = B * S
    tq = _pick(S, (256, 128))
    nq = S // tq
    pairs = MLA_HEADS // 2

    def body(q_ref, k_ref, v_ref, do_ref, o_ref, lse_ref, dq_ref, dk_ref, dv_ref, dq_acc):
        for half in range(2):
            lanes = slice(half * HEAD_BLOCK, (half + 1) * HEAD_BLOCK)
            col = half * V_HEAD
            dq_acc[...] = jnp.zeros_like(dq_acc)

            def k_tile(kj, _):
                k0 = pl.multiple_of(kj * tq, tq)
                kv = k_ref[pl.ds(k0, tq), lanes]
                vv = v_ref[pl.ds(k0, tq), :]
                kpos = k0 + lax.broadcasted_iota(jnp.int32, (tq, tq), 1)

                def q_tile(qi, carry):
                    dk, dv = carry
                    q0 = pl.multiple_of(qi * tq, tq)
                    qv = q_ref[pl.ds(q0, tq), lanes]
                    own = _half_mask((tq, 2 * V_HEAD), half)
                    dov = jnp.where(own, do_ref[pl.ds(q0, tq), :], 0.0)
                    dsum = jnp.sum(dov * o_ref[pl.ds(q0, tq), :], axis=-1, keepdims=True)
                    dob = dov.astype(BF16)
                    lse_col = lse_ref[pl.ds(q0, tq), col:col + 1]
                    s = _dot_nt(qv, kv) * MLA_SCALE
                    qpos = q0 + lax.broadcasted_iota(jnp.int32, (tq, tq), 0)
                    p = jnp.where(kpos <= qpos, jnp.exp(s - lse_col), 0.0)
                    dv = dv + _dot_tn(p.astype(BF16), dob)
                    dp = _dot_nt(dob, vv)
                    ds = (p * (dp - dsum) * MLA_SCALE).astype(BF16)
                    dq_acc[pl.ds(q0, tq), :] += _dot(ds, kv)
                    dk = dk + _dot_tn(ds, qv)
                    return dk, dv

                dk, dv = lax.fori_loop(kj, nq, q_tile,
                                       (jnp.zeros((tq, HEAD_BLOCK), F32), jnp.zeros((tq, 2 * V_HEAD), F32)))
                dk_ref[pl.ds(k0, tq), lanes] = dk
                if half == 0:
                    dv_ref[pl.ds(k0, tq), :] = dv
                else:
                    dv_ref[pl.ds(k0, tq), :] += dv
                return 0

            lax.fori_loop(0, nq, k_tile, 0)
            dq_ref[:, lanes] = dq_acc[...]

    blk2 = pl.BlockSpec((S, 2 * HEAD_BLOCK), lambda b, j: (b, j))
    blk1 = pl.BlockSpec((S, 2 * V_HEAD), lambda b, j: (b, j))
    return pl.pallas_call(
        body, name="mla_bwd", grid=(B, pairs),
        in_specs=[blk2, blk2, blk1, blk1, blk1, blk1], out_specs=[blk2, blk2, blk1],
        out_shape=[jax.ShapeDtypeStruct((T, MLA_HEADS * HEAD_BLOCK), F32), jax.ShapeDtypeStruct((T, MLA_HEADS * HEAD_BLOCK), F32),
                   jax.ShapeDtypeStruct((T, pairs * 2 * V_HEAD), F32)],
        scratch_shapes=[pltpu.VMEM((S, HEAD_BLOCK), F32)],
        compiler_params=_cparams(("arbitrary", "arbitrary")),
    )(q, k, v, do, o, lse)


def _swa_probs(qm, kc, kp, bias_h, sink, n):
    W = WINDOW
    qi = lax.broadcasted_iota(jnp.int32, (W, W), 0)
    kj = lax.broadcasted_iota(jnp.int32, (W, W), 1)
    s_c = jnp.where(kj <= qi, _dot_nt(qm, kc) * SWA_SCALE + bias_h[:, W:], NEG)
    s_p = jnp.where((kj > qi) & (n > 0), _dot_nt(qm, kp) * SWA_SCALE + bias_h[:, :W], NEG)
    m = jnp.maximum(jnp.maximum(jnp.max(s_c, axis=-1, keepdims=True), jnp.max(s_p, axis=-1, keepdims=True)), sink)
    e_c = jnp.exp(s_c - m)
    e_p = jnp.exp(s_p - m)
    e_s = jnp.exp(sink - m)
    inv = 1.0 / (jnp.sum(e_c, axis=-1, keepdims=True) + jnp.sum(e_p, axis=-1, keepdims=True) + e_s)
    return e_c * inv, e_p * inv, e_s * inv


def _swa_fwd(q, k, v, bias, sinks, B, S):
    T = B * S
    W = WINDOW
    nb = S // W
    G = SWA_HEADS // SWA_KV_HEADS

    def body(q_ref, k_ref, v_ref, bias_ref, sink_ref, o_ref):
        def block(n, _):
            r0 = pl.multiple_of(n * W, W)
            p0 = pl.multiple_of(jnp.maximum(n - 1, 0) * W, W)
            kc, kp = k_ref[pl.ds(r0, W), :], k_ref[pl.ds(p0, W), :]
            vc, vp = v_ref[pl.ds(r0, W), :], v_ref[pl.ds(p0, W), :]
            for j in range(G):
                qb = q_ref[pl.ds(r0, W), j * W:(j + 1) * W]
                outs = []
                for half in range(2):
                    h = half * G + j
                    qm = jnp.where(_half_mask((W, W), half), qb, jnp.zeros_like(qb))
                    p_c, p_p, _ = _swa_probs(qm, kc, kp, bias_ref[h], sink_ref[h:h + 1, 0:1], n)
                    outs.append(_dot(p_c.astype(BF16), vc) + _dot(p_p.astype(BF16), vp))
                o_ref[pl.ds(r0, W), j * W:(j + 1) * W] = jnp.where(_half_mask((W, W), 0), outs[0], outs[1])
            return 0

        lax.fori_loop(0, nb, block, 0)

    SQ, SK = SWA_HEADS * SWA_HEAD_DIM, SWA_KV_HEADS * SWA_HEAD_DIM
    return pl.pallas_call(
        body, name="swa_fwd", grid=(B,),
        in_specs=[pl.BlockSpec((S, SQ), lambda b: (b, 0)), pl.BlockSpec((S, SK), lambda b: (b, 0)),
                  pl.BlockSpec((S, SK), lambda b: (b, 0)), _const_spec((SWA_HEADS, W, 2 * W)), _const_spec((SWA_HEADS, 128))],
        out_specs=pl.BlockSpec((S, SQ), lambda b: (b, 0)),
        out_shape=jax.ShapeDtypeStruct((T, SQ), F32),
        compiler_params=_cparams(("arbitrary",)),
    )(q, k, v, bias, sinks)


def _swa_bwd(q, k, v, do, bias, sinks, B, S):
    T = B * S
    W = WINDOW
    nb = S // W
    G = SWA_HEADS // SWA_KV_HEADS

    def body(q_ref, k_ref, v_ref, do_ref, bias_ref, sink_ref, dq_ref, dk_ref, dv_ref, dbias_ref, dsink_ref):
        @pl.when(pl.program_id(0) == 0)
        def _():
            dbias_ref[...] = jnp.zeros_like(dbias_ref)
            dsink_ref[...] = jnp.zeros_like(dsink_ref)

        dk_ref[...] = jnp.zeros_like(dk_ref)
        dv_ref[...] = jnp.zeros_like(dv_ref)

        def block(n, _):
            r0 = pl.multiple_of(n * W, W)
            p0 = pl.multiple_of(jnp.maximum(n - 1, 0) * W, W)
            kc, kp = k_ref[pl.ds(r0, W), :], k_ref[pl.ds(p0, W), :]
            vc, vp = v_ref[pl.ds(r0, W), :], v_ref[pl.ds(p0, W), :]
            dk_c = jnp.zeros((W, W), F32)
            dk_p = jnp.zeros((W, W), F32)
            dv_c = jnp.zeros((W, W), F32)
            dv_p = jnp.zeros((W, W), F32)
            for j in range(G):
                qb = q_ref[pl.ds(r0, W), j * W:(j + 1) * W]
                dob = do_ref[pl.ds(r0, W), j * W:(j + 1) * W]
                dqs = []
                for half in range(2):
                    h = half * G + j
                    own = _half_mask((W, W), half)
                    qm = jnp.where(own, qb, jnp.zeros_like(qb))
                    dom = jnp.where(own, dob, 0.0).astype(BF16)
                    p_c, p_p, p_s = _swa_probs(qm, kc, kp, bias_ref[h], sink_ref[h:h + 1, 0:1], n)
                    dp_c = _dot_nt(dom, vc)
                    dp_p = _dot_nt(dom, vp)
                    dsum = jnp.sum(p_c * dp_c, axis=-1, keepdims=True) + jnp.sum(p_p * dp_p, axis=-1, keepdims=True)
                    ds_c = p_c * (dp_c - dsum)
                    ds_p = p_p * (dp_p - dsum)
                    dbias_ref[h, :, W:] += ds_c
                    dbias_ref[h, :, :W] += ds_p
                    dsink_ref[h:h + 1, :] += jnp.broadcast_to(-jnp.sum(p_s * dsum, axis=0, keepdims=True), (1, 128))
                    dsc = (ds_c * SWA_SCALE).astype(BF16)
                    dsp = (ds_p * SWA_SCALE).astype(BF16)
                    dqs.append(_dot(dsc, kc) + _dot(dsp, kp))
                    dk_c = dk_c + _dot_tn(dsc, qm)
                    dk_p = dk_p + _dot_tn(dsp, qm)
                    dv_c = dv_c + _dot_tn(p_c.astype(BF16), dom)
                    dv_p = dv_p + _dot_tn(p_p.astype(BF16), dom)
                dq_ref[pl.ds(r0, W), j * W:(j + 1) * W] = jnp.where(_half_mask((W, W), 0), dqs[0], dqs[1])
            dk_ref[pl.ds(r0, W), :] += dk_c
            dv_ref[pl.ds(r0, W), :] += dv_c
            dk_ref[pl.ds(p0, W), :] += dk_p
            dv_ref[pl.ds(p0, W), :] += dv_p
            return 0

        lax.fori_loop(0, nb, block, 0)

    SQ, SK = SWA_HEADS * SWA_HEAD_DIM, SWA_KV_HEADS * SWA_HEAD_DIM
    row = lambda w: pl.BlockSpec((S, w), lambda b: (b, 0))
    return pl.pallas_call(
        body, name="swa_bwd", grid=(B,),
        in_specs=[row(SQ), row(SK), row(SK), row(SQ), _const_spec((SWA_HEADS, W, 2 * W)), _const_spec((SWA_HEADS, 128))],
        out_specs=[row(SQ), row(SK), row(SK), _acc_spec((SWA_HEADS, W, 2 * W)), _acc_spec((SWA_HEADS, 128))],
        out_shape=[jax.ShapeDtypeStruct((T, SQ), F32), jax.ShapeDtypeStruct((T, SK), F32), jax.ShapeDtypeStruct((T, SK), F32),
                   jax.ShapeDtypeStruct((SWA_HEADS, W, 2 * W), F32), jax.ShapeDtypeStruct((SWA_HEADS, 128), F32)],
        compiler_params=_cparams(("arbitrary",)),
    )(q, k, v, do, bias, sinks)


def _bias_expand(rel_bias, bucket):
    W = WINDOW

    def body(rb_ref, bucket_ref, out_ref):
        bk = bucket_ref[...]
        for h in range(SWA_HEADS):
            def add(b, acc):
                return jnp.where(bk == b, rb_ref[b, h], acc)

            out_ref[h] = lax.fori_loop(0, REL_BUCKETS, add, jnp.zeros((W, 2 * W), F32))

    return pl.pallas_call(
        body, name="bias_expand",
        in_specs=[pl.BlockSpec(memory_space=pltpu.SMEM), pl.BlockSpec(memory_space=pltpu.VMEM)],
        out_specs=pl.BlockSpec(memory_space=pltpu.VMEM),
        out_shape=jax.ShapeDtypeStruct((SWA_HEADS, W, 2 * W), F32),
        compiler_params=_cparams(),
    )(rel_bias, bucket)


def _bias_reduce(dbias, bucket):
    def body(db_ref, bucket_ref, out_ref):
        bk = bucket_ref[...]
        row = lax.broadcasted_iota(jnp.int32, (REL_BUCKETS, 128), 0)
        lane = lax.broadcasted_iota(jnp.int32, (REL_BUCKETS, 128), 1)
        acc = jnp.zeros((REL_BUCKETS, 128), F32)
        for h in range(SWA_HEADS):
            dbh = db_ref[h]

            def add(b, acc):
                part = jnp.sum(jnp.sum(jnp.where(bk == b, dbh, 0.0), axis=-1, keepdims=True), axis=0, keepdims=True)
                return jnp.where((row == b) & (lane == h), part, acc)

            acc = lax.fori_loop(0, REL_BUCKETS, add, acc)
        out_ref[...] = acc

    return pl.pallas_call(
        body, name="bias_reduce",
        out_shape=jax.ShapeDtypeStruct((REL_BUCKETS, 128), F32),
        compiler_params=_cparams(),
    )(dbias, bucket)


def _t5_bucket(dist):
    n = jnp.maximum(dist, 0)
    max_exact = REL_BUCKETS // 2
    nf = jnp.maximum(n, 1).astype(F32)
    large = max_exact + (jnp.log(nf / max_exact) / math.log(REL_MAX_DIST / max_exact) * (REL_BUCKETS - max_exact)).astype(jnp.int32)
    large = jnp.minimum(large, REL_BUCKETS - 1)
    return jnp.where(n < max_exact, n, large)


def _pair_heads(a, axis, inverse=False):
    shp = a.shape
    a = a.reshape(shp[:axis] + ((4, 2, 64) if inverse else (2, 4, 64)) + shp[axis + 1:])
    a = jnp.swapaxes(a, axis, axis + 1)
    return a.reshape(shp)


def _col_full(g):
    return jnp.transpose(g, (1, 0, 2)).reshape(g.shape[1], N_DEV * g.shape[2])


def _row_full(g):
    return g.reshape(N_DEV * g.shape[1], g.shape[2])


def _col_blocks(w):
    r = w.shape[0]
    return jnp.transpose(w.reshape(r, N_CHIP, 2, -1), (2, 1, 0, 3))


def _row_blocks(w):
    c = w.shape[1]
    return jnp.transpose(w.reshape(N_CHIP, 2, -1, c), (1, 0, 2, 3))


_IN_WIDTHS = (Q_LORA, KV_LORA, QK_ROPE, SWA_HEADS * SWA_HEAD_DIM, SWA_KV_HEADS * SWA_HEAD_DIM, SWA_KV_HEADS * SWA_HEAD_DIM)


def _split_in(w):
    parts, o = [], 0
    for wd in _IN_WIDTHS:
        parts.append(w[:, o:o + wd])
        o += wd
    return parts


def _win_to_kernel(w):
    cq, ckv, kpe, qs, ks, vs = _split_in(w)
    kpe = jnp.pad(kpe, ((0, 0), (64, 32)))
    return jnp.concatenate([cq, ckv, _pair_heads(qs, 1), ks, vs, kpe], axis=1)


def _win_from_kernel(dw):
    o = [0, Q_LORA, Q_LORA + KV_LORA, Q_LORA + KV_LORA + 512, Q_LORA + KV_LORA + 640, Q_LORA + KV_LORA + 768]
    cq, ckv, qs, ks, vs, kpe = [dw[:, a:b] for a, b in zip(o, o[1:] + [dw.shape[1]])]
    return jnp.concatenate([cq, ckv, kpe[:, 64:96], _pair_heads(qs, 1, inverse=True), ks, vs], axis=1)


def _wq_to_kernel(w):
    return jnp.pad(w.reshape(Q_LORA, MLA_HEADS, QK_NOPE + QK_ROPE), ((0, 0), (0, 0), (0, 32))).reshape(Q_LORA, MLA_HEADS * HEAD_BLOCK)


def _wq_from_kernel(dw):
    return dw.reshape(Q_LORA, MLA_HEADS, HEAD_BLOCK)[:, :, :QK_NOPE + QK_ROPE].reshape(Q_LORA, -1)


def _wkv_to_kernel(w):
    w3 = w.reshape(KV_LORA, MLA_HEADS, QK_NOPE + V_HEAD)
    kpart = jnp.pad(w3[:, :, :QK_NOPE], ((0, 0), (0, 0), (0, 64))).reshape(KV_LORA, MLA_HEADS * HEAD_BLOCK)
    return jnp.concatenate([kpart, w3[:, :, QK_NOPE:].reshape(KV_LORA, MLA_HEADS * V_HEAD)], axis=1)


def _wkv_from_kernel(dw):
    dk = dw[:, :MLA_HEADS * HEAD_BLOCK].reshape(KV_LORA, MLA_HEADS, HEAD_BLOCK)[:, :, :QK_NOPE]
    dv = dw[:, MLA_HEADS * HEAD_BLOCK:].reshape(KV_LORA, MLA_HEADS, V_HEAD)
    return jnp.concatenate([dk, dv], axis=2).reshape(KV_LORA, -1)


def _wo_to_kernel(w):
    return jnp.concatenate([w[:512], _pair_heads(w[512:], 0)], axis=0)


def _wo_from_kernel(dw):
    return jnp.concatenate([dw[:512], _pair_heads(dw[512:], 0, inverse=True)], axis=0)


_SMALL = (("g_ffn1", 1024), ("g_mix", 1024), ("g_q_a", 256), ("g_kv_a", 128), ("attn_sinks", 8), ("rel_bias", 256),
          ("g_out_mla", 512), ("g_out_swa", 512), ("g_ffn2", 1024), ("g_final", 1024), ("loss", 1))


def _pack_small(vals):
    rows = []
    for name, size in _SMALL:
        v = vals[name].reshape(-1).astype(F32)
        nrow = -(-size // 128)
        rows.append(jnp.pad(v, (0, nrow * 128 - size)).reshape(nrow, 128))
    packed = jnp.concatenate(rows, axis=0)
    return jnp.pad(packed, ((0, SMALL_ROWS - packed.shape[0]), (0, 0)))


def _unpack_small(packed, shapes):
    out, r = {}, 0
    for name, size in _SMALL:
        nrow = -(-size // 128)
        out[name] = packed[r:r + nrow].reshape(-1)[:size].reshape(shapes[name])
        r += nrow
    return out


def _local_step(x, target, gains, wts):
    B, S, D = x.shape
    T = B * S
    x2 = x.reshape(T, D)
    t2 = target.reshape(T, D)

    pos = jnp.arange(S, dtype=F32)
    inv_freq = ROPE_THETA ** (-jnp.arange(0, QK_ROPE, 2, dtype=F32) / QK_ROPE)
    ang = pos[:, None] * inv_freq[None, :]
    cos, sin = jnp.cos(ang), jnp.sin(ang)
    ones, zeros = jnp.ones((S, 64), F32), jnp.zeros((S, 32), F32)
    rope_c = jnp.concatenate([ones, cos, cos, zeros], axis=1)
    rope_s = jnp.concatenate([0.0 * ones, -sin, sin, zeros], axis=1)

    qi = jnp.arange(WINDOW)[:, None]
    kj = jnp.arange(2 * WINDOW)[None, :]
    bucket = _t5_bucket(qi + WINDOW - kj).astype(jnp.int32)
    bias = _bias_expand(gains["rel_bias"], bucket)
    sinks = jnp.broadcast_to(gains["attn_sinks"].reshape(SWA_HEADS, 1), (SWA_HEADS, 128))
    g_swa = _pair_heads(gains["g_out_swa"], 1)

    h1, n1, gate1, up1 = _ffn_fwd(x2, gains["g_ffn1"], wts["g1"], wts["u1"], wts["d1"], "ffn1_fwd")
    (u, c_q, c_kv, cqn, ckvn, q, k, v, qs, ks, vs) = _proj_fwd(
        h1, gains["g_mix"], wts["in"], gains["g_q_a"], wts["q"], gains["g_kv_a"], wts["kv"], rope_c, rope_s, S)
    o_mla, lse = _mla_fwd(q, k, v, B, S)
    o_swa = _swa_fwd(qs, ks, vs, bias, sinks, B, S)
    h2, oc = _out_fwd(o_mla, o_swa, gains["g_out_mla"], g_swa, wts["o"], h1)
    h3, n2, gate2, up2 = _ffn_fwd(h2, gains["g_ffn2"], wts["g2"], wts["u2"], wts["d2"], "ffn2_fwd")
    dh3, loss, dg_final = _loss_head(h3, t2, gains["g_final"].reshape(1, D))

    dh2, dg_ffn2, act2, dgate2, dup2 = _ffn_bwd(h2, gains["g_ffn2"], gate2, up2, dh3, wts["g2"], wts["u2"], wts["d2"], "ffn2_bwd")
    grads = {"g2": _matmul_tn(n2, dgate2, "dw_gate2"), "u2": _matmul_tn(n2, dup2, "dw_up2"), "d2": _matmul_tn(act2, dh3, "dw_down2", 0.5)}
    do_mla, do_swa, dg_mla, dg_swa = _out_bwd(dh2, o_mla, o_swa, gains["g_out_mla"], g_swa, wts["o"])
    grads["o"] = _matmul_tn(oc, dh2, "dw_o")
    dq, dk, dv = _mla_bwd(q, k, v, do_mla, o_mla, lse, B, S)
    dqs, dks, dvs, dbias, dsink = _swa_bwd(qs, ks, vs, do_swa, bias, sinks, B, S)
    dh1, dproj, dql, dkvc, dg_mix, dg_q, dg_kv = _proj_bwd(
        dq, dk, dv, dqs, dks, dvs, c_q, c_kv, h1, dh2, gains["g_mix"], wts["in"], gains["g_q_a"], wts["q"],
        gains["g_kv_a"], wts["kv"], rope_c, rope_s, S)
    grads["in"] = _matmul_tn(u, dproj, "dw_in")
    grads["q"] = _matmul_tn(cqn, dql, "dw_q")
    grads["kv"] = _matmul_tn(ckvn, dkvc, "dw_kv")
    dx, dg_ffn1, act1, dgate1, dup1 = _ffn_bwd(x2, gains["g_ffn1"], gate1, up1, dh1, wts["g1"], wts["u1"], wts["d1"], "ffn1_bwd")
    grads["g1"] = _matmul_tn(n1, dgate1, "dw_gate1")
    grads["u1"] = _matmul_tn(n1, dup1, "dw_up1")
    grads["d1"] = _matmul_tn(act1, dh1, "dw_down1", 0.5)

    small = {
        "g_ffn1": dg_ffn1, "g_mix": dg_mix, "g_q_a": dg_q, "g_kv_a": dg_kv, "attn_sinks": dsink[:, 0],
        "rel_bias": _bias_reduce(dbias, bucket)[:, :SWA_HEADS], "g_out_mla": dg_mla, "g_out_swa": _pair_heads(dg_swa, 1, inverse=True),
        "g_ffn2": dg_ffn2, "g_final": dg_final, "loss": loss[0, 0:1],
    }
    return dx.reshape(B, S, D), grads, small


_WEIGHTS = ("g_ffn1", "w_ffn1_gate", "w_ffn1_up", "w_ffn1_down", "g_mix", "w_in", "g_q_a", "w_q_b", "g_kv_a", "w_kv_b",
            "attn_sinks", "rel_bias", "g_out_mla", "g_out_swa", "w_o", "g_ffn2", "w_ffn2_gate", "w_ffn2_up", "w_ffn2_down",
            "g_final")
_BIG = (("w_ffn1_gate", "g1", True), ("w_ffn1_up", "u1", True), ("w_ffn1_down", "d1", False), ("w_in", "in", True),
        ("w_q_b", "q", True), ("w_kv_b", "kv", True), ("w_o", "o", False), ("w_ffn2_gate", "g2", True),
        ("w_ffn2_up", "u2", True), ("w_ffn2_down", "d2", False))
_TO_KERNEL = {"in": _win_to_kernel, "q": _wq_to_kernel, "kv": _wkv_to_kernel, "o": _wo_to_kernel}
_FROM_KERNEL = {"in": _win_from_kernel, "q": _wq_from_kernel, "kv": _wkv_from_kernel, "o": _wo_from_kernel}


def kernel(x, g_ffn1, w_ffn1_gate, w_ffn1_up, w_ffn1_down, g_mix, w_in, g_q_a, w_q_b, g_kv_a, w_kv_b, attn_sinks, rel_bias, g_out_mla, g_out_swa, w_o, g_ffn2, w_ffn2_gate, w_ffn2_up, w_ffn2_down, g_final, loss_target, m_g_ffn1, m_w_ffn1_gate, m_w_ffn1_up, m_w_ffn1_down, m_g_mix, m_w_in, m_g_q_a, m_w_q_b, m_g_kv_a, m_w_kv_b, m_attn_sinks, m_rel_bias, m_g_out_mla, m_g_out_swa, m_w_o, m_g_ffn2, m_w_ffn2_gate, m_w_ffn2_up, m_w_ffn2_down, m_g_final, v_g_ffn1, v_w_ffn1_gate, v_w_ffn1_up, v_w_ffn1_down, v_g_mix, v_w_in, v_g_q_a, v_w_q_b, v_g_kv_a, v_w_kv_b, v_attn_sinks, v_rel_bias, v_g_out_mla, v_g_out_swa, v_w_o, v_g_ffn2, v_w_ffn2_gate, v_w_ffn2_up, v_w_ffn2_down, v_g_final):
    w = dict(zip(_WEIGHTS, (g_ffn1, w_ffn1_gate, w_ffn1_up, w_ffn1_down, g_mix, w_in, g_q_a, w_q_b, g_kv_a, w_kv_b, attn_sinks,
                            rel_bias, g_out_mla, g_out_swa, w_o, g_ffn2, w_ffn2_gate, w_ffn2_up, w_ffn2_down, g_final)))
    m = dict(zip(_WEIGHTS, (m_g_ffn1, m_w_ffn1_gate, m_w_ffn1_up, m_w_ffn1_down, m_g_mix, m_w_in, m_g_q_a, m_w_q_b, m_g_kv_a,
                            m_w_kv_b, m_attn_sinks, m_rel_bias, m_g_out_mla, m_g_out_swa, m_w_o, m_g_ffn2, m_w_ffn2_gate,
                            m_w_ffn2_up, m_w_ffn2_down, m_g_final)))
    v = dict(zip(_WEIGHTS, (v_g_ffn1, v_w_ffn1_gate, v_w_ffn1_up, v_w_ffn1_down, v_g_mix, v_w_in, v_g_q_a, v_w_q_b, v_g_kv_a,
                            v_w_kv_b, v_attn_sinks, v_rel_bias, v_g_out_mla, v_g_out_swa, v_w_o, v_g_ffn2, v_w_ffn2_gate,
                            v_w_ffn2_up, v_w_ffn2_down, v_g_final)))
    core = lax.axis_index("c").astype(jnp.int32).reshape(1)
    chip = (2 * lax.axis_index("x") + lax.axis_index("y")).astype(jnp.int32).reshape(1)

    shards = _cast_bf16([w[name][0] for name, _, _ in _BIG])
    gathered = _all_gather(shards, "weights_all_gather")
    wts = {}
    for (name, key, by_col), g in zip(_BIG, gathered):
        full = _col_full(g) if by_col else _row_full(g)
        wts[key] = _TO_KERNEL[key](full) if key in _TO_KERNEL else full

    small_names = [n for n, _ in _SMALL if n != "loss"]
    gains = {n: (w[n] if n in ("rel_bias", "g_final") else w[n].reshape(1, -1)) for n in small_names}
    grad_x, grads, small = _local_step(x, loss_target, gains, wts)

    blocks = []
    for name, key, by_col in _BIG:
        g = grads[key]
        if key in _FROM_KERNEL:
            g = _FROM_KERNEL[key](g)
        blocks.append(_col_blocks(g) if by_col else _row_blocks(g))
    from_sibling = _exchange_cores(blocks)
    sums = [_add_core_halves(b, r, core) for b, r in zip(blocks, from_sibling)]
    from_chips = _exchange_chips([s16 for _, s16 in sums])

    small_gathered = _all_gather([_pack_small(small)], "small_all_gather")[0]

    out_g, out_d, out_m, out_v = {}, {}, {}, {}
    for (name, key, by_col), (s32, _), r16 in zip(_BIG, sums, from_chips):
        g, d, mn, vn = _adam_big(w[name][0], m[name][0], v[name][0], s32, r16, chip)
        out_g[name], out_d[name], out_m[name], out_v[name] = g[None], d[None], mn[None], vn[None]
    shapes = {n: w[n].shape for n in small_names}
    shapes["loss"] = ()
    zero = jnp.zeros((), F32)
    ws, ms, vs = ({**{n: src[n] for n in small_names}, "loss": zero} for src in (w, m, v))
    sg, sd, sm, sv = _adam_small(_pack_small(ws), _pack_small(ms), _pack_small(vs), small_gathered)
    sg, sd, sm, sv = (_unpack_small(p, shapes) for p in (sg, sd, sm, sv))
    for n in small_names:
        out_g[n], out_d[n], out_m[n], out_v[n] = sg[n], sd[n], sm[n], sv[n]

    return (sg["loss"], grad_x, *[out_g[n] for n in _WEIGHTS], *[out_d[n] for n in _WEIGHTS],
            *[out_m[n] for n in _WEIGHTS], *[out_v[n] for n in _WEIGHTS])
```

```python
import functools
import math

import jax
import jax.numpy as jnp
from jax import lax
from jax.experimental import pallas as pl
from jax.experimental.pallas import tpu as pltpu

F32 = jnp.float32
BF16 = jnp.bfloat16
MESH = pl.DeviceIdType.MESH

EPS = 1e-6
NEG = -1e30
N_DEV = 8
N_CHIP = 4

MLA_HEADS = 8
Q_LORA = 256
KV_LORA = 128
QK_NOPE = 64
QK_ROPE = 32
V_HEAD = 64
ROPE_THETA = 10000.0
SWA_HEADS = 8
SWA_KV_HEADS = 2
SWA_HEAD_DIM = 64
WINDOW = 128
REL_BUCKETS = 32
REL_MAX_DIST = 128
HEAD_BLOCK = 128
MLA_Q_TILES = (512, 256, 128)
MLA_SCALE = (QK_NOPE + QK_ROPE) ** -0.5
SWA_SCALE = SWA_HEAD_DIM ** -0.5

ADAM_LR = 0.001
ADAM_B1 = 0.9
ADAM_B2 = 0.999
ADAM_EPS = 1e-08
ADAM_WD = 0.01
ADAM_STEP = 10

VMEM_LIMIT = 56 * 1024 * 1024
SMALL_ROWS = 48


def _cparams(semantics=None):
    return pltpu.CompilerParams(dimension_semantics=semantics, vmem_limit_bytes=VMEM_LIMIT)


def _dot(a, b):
    return jnp.dot(a, b, preferred_element_type=F32)


def _dot_nt(a, b):
    return lax.dot_general(a, b, (((1,), (1,)), ((), ())), preferred_element_type=F32)


def _dot_tn(a, b):
    return lax.dot_general(a, b, (((0,), (0,)), ((), ())), preferred_element_type=F32)


def _rms_fwd(x, g):
    r = lax.rsqrt(jnp.mean(x * x, axis=-1, keepdims=True) + EPS)
    return x * r * g, r


def _rms_bwd(x, g, r, dy):
    xh = x * r
    dyg = dy * g
    dx = r * (dyg - xh * jnp.mean(dyg * xh, axis=-1, keepdims=True))
    return dx, jnp.sum(dy * xh, axis=0, keepdims=True)


def _swap16(x):
    n = x.shape[-1]
    up = pltpu.roll(x, n - 16, 1)
    down = pltpu.roll(x, 16, 1)
    lane = lax.broadcasted_iota(jnp.int32, x.shape, 1) % HEAD_BLOCK
    return jnp.where((lane >= 64) & (lane < 80), up, jnp.where((lane >= 80) & (lane < 96), down, 0.0))


def _const_spec(shape):
    return pl.BlockSpec(shape, lambda *_: (0,) * len(shape), pipeline_mode=pl.Buffered(1))


def _acc_spec(shape):
    return pl.BlockSpec(shape, lambda *_: (0,) * len(shape))


def _row_spec(tm, width):
    return pl.BlockSpec((tm, width), lambda i: (i, 0))


def _cast_bf16(arrays):
    n = len(arrays)

    def body(*refs):
        for src, dst in zip(refs[:n], refs[n:]):
            dst[...] = src[...].astype(BF16)

    return pl.pallas_call(
        body, name="cast_weights",
        out_shape=[jax.ShapeDtypeStruct(a.shape, BF16) for a in arrays],
        compiler_params=_cparams(),
    )(*arrays)


def _all_gather(arrays, name):
    n = len(arrays)

    def body(*refs):
        ins, outs = refs[:n], refs[n:2 * n]
        send_sems, recv_sems, local_sems = refs[2 * n:]
        x, y, c = lax.axis_index("x"), lax.axis_index("y"), lax.axis_index("c")
        me, sibling = (x, y, c), (x, y, 1 - c)
        chips = [(1 - x, y), (x, 1 - y), (1 - x, 1 - y)]

        def slot(a, dev):
            return outs[a].at[4 * dev[0] + 2 * dev[1] + dev[2]]

        def copy(a, k, block, to, src=None):
            return pltpu.make_async_remote_copy(
                src_ref=slot(a, block) if src is None else src, dst_ref=slot(a, block),
                send_sem=send_sems.at[a, k], recv_sem=recv_sems.at[a, k], device_id=to, device_id_type=MESH)

        mine = [pltpu.make_async_copy(ins[a], slot(a, me), local_sems.at[a]) for a in range(n)]
        for cp in mine:
            cp.start()
        first = []
        for a in range(n):
            first.append(copy(a, 0, me, sibling, src=ins[a]))
            first += [copy(a, 1 + j, me, (*chip, c), src=ins[a]) for j, chip in enumerate(chips)]
        for cp in first:
            cp.start()
        passed = []
        for j, chip in enumerate(chips):
            for a in range(n):
                copy(a, 1 + j, (*chip, c), me).wait_recv()
                cp = copy(a, 4 + j, (*chip, c), sibling)
                cp.start()
                passed.append(cp)
        for a in range(n):
            copy(a, 0, sibling, me).wait_recv()
            for j, chip in enumerate(chips):
                copy(a, 4 + j, (*chip, 1 - c), me).wait_recv()
        for cp in first + passed:
            cp.wait_send()
        for cp in mine:
            cp.wait()

    any_spec = pl.BlockSpec(memory_space=pl.ANY)
    return pl.pallas_call(
        body, name=name,
        out_shape=[jax.ShapeDtypeStruct((N_DEV,) + a.shape, a.dtype) for a in arrays],
        in_specs=[any_spec] * n, out_specs=[any_spec] * n,
        scratch_shapes=[pltpu.SemaphoreType.DMA((n, 7)), pltpu.SemaphoreType.DMA((n, 7)), pltpu.SemaphoreType.DMA((n,))],
    )(*arrays)


def _exchange_cores(arrays):
    n = len(arrays)

    def body(*refs):
        ins, outs = refs[:n], refs[n:2 * n]
        send_sems, recv_sems = refs[2 * n:]
        x, y, c = lax.axis_index("x"), lax.axis_index("y"), lax.axis_index("c")
        copies = [pltpu.make_async_remote_copy(
            src_ref=ins[a].at[1 - c], dst_ref=outs[a], send_sem=send_sems.at[a], recv_sem=recv_sems.at[a],
            device_id=(x, y, 1 - c), device_id_type=MESH) for a in range(n)]
        for cp in copies:
            cp.start()
        for cp in copies:
            cp.wait()

    any_spec = pl.BlockSpec(memory_space=pl.ANY)
    return pl.pallas_call(
        body, name="grad_exchange_cores",
        out_shape=[jax.ShapeDtypeStruct(a.shape[1:], a.dtype) for a in arrays],
        in_specs=[any_spec] * n, out_specs=[any_spec] * n,
        scratch_shapes=[pltpu.SemaphoreType.DMA((n,)), pltpu.SemaphoreType.DMA((n,))],
    )(*arrays)


def _exchange_chips(arrays):
    n = len(arrays)

    def body(*refs):
        ins, outs = refs[:n], refs[n:2 * n]
        send_sems, recv_sems, local_sems = refs[2 * n:]
        x, y, c = lax.axis_index("x"), lax.axis_index("y"), lax.axis_index("c")
        my_chip = 2 * x + y
        chips = [(1 - x, y), (x, 1 - y), (1 - x, 1 - y)]
        mine = [pltpu.make_async_copy(ins[a].at[my_chip], outs[a].at[my_chip], local_sems.at[a]) for a in range(n)]
        for cp in mine:
            cp.start()
        def copy(a, j, src_chip, dst_chip, to):
            return pltpu.make_async_remote_copy(
                src_ref=ins[a].at[dst_chip], dst_ref=outs[a].at[src_chip],
                send_sem=send_sems.at[a, j], recv_sem=recv_sems.at[a, j], device_id=to, device_id_type=MESH)

        sends = [copy(a, j, my_chip, 2 * chip[0] + chip[1], (*chip, c)) for a in range(n) for j, chip in enumerate(chips)]
        for cp in sends:
            cp.start()
        for a in range(n):
            for j, chip in enumerate(chips):
                copy(a, j, 2 * chip[0] + chip[1], my_chip, (x, y, c)).wait_recv()
        for cp in sends:
            cp.wait_send()
        for cp in mine:
            cp.wait()

    any_spec = pl.BlockSpec(memory_space=pl.ANY)
    return pl.pallas_call(
        body, name="grad_exchange_chips",
        out_shape=[jax.ShapeDtypeStruct(a.shape, a.dtype) for a in arrays],
        in_specs=[any_spec] * n, out_specs=[any_spec] * n,
        scratch_shapes=[pltpu.SemaphoreType.DMA((n, 3)), pltpu.SemaphoreType.DMA((n, 3)), pltpu.SemaphoreType.DMA((n,))],
    )(*arrays)


def _add_core_halves(mine_all, recv, core):
    _, nchip, r, c = mine_all.shape
    tr = r if r <= 512 else (512 if r % 512 == 0 else 352)
    assert r % tr == 0

    def body(core_ref, a_ref, b_ref, o32_ref, o16_ref):
        s = a_ref[...] + b_ref[...]
        o32_ref[...] = s
        o16_ref[...] = s.astype(BF16)

    grid_spec = pltpu.PrefetchScalarGridSpec(
        num_scalar_prefetch=1, grid=(nchip, r // tr),
        in_specs=[pl.BlockSpec((None, None, tr, c), lambda q, i, s: (s[0], q, i, 0)),
                  pl.BlockSpec((None, tr, c), lambda q, i, s: (q, i, 0))],
        out_specs=[pl.BlockSpec((None, tr, c), lambda q, i, s: (q, i, 0)),
                   pl.BlockSpec((None, tr, c), lambda q, i, s: (q, i, 0))])
    return pl.pallas_call(
        body, name="grad_add_cores", grid_spec=grid_spec,
        out_shape=[jax.ShapeDtypeStruct((nchip, r, c), F32), jax.ShapeDtypeStruct((nchip, r, c), BF16)],
        compiler_params=_cparams(("arbitrary", "arbitrary")),
    )(core, mine_all, recv)


def _adam_math(w, g, m, v):
    m = ADAM_B1 * m + (1.0 - ADAM_B1) * g
    v = ADAM_B2 * v + (1.0 - ADAM_B2) * (g * g)
    m_hat = m / (1.0 - ADAM_B1 ** ADAM_STEP)
    v_hat = v / (1.0 - ADAM_B2 ** ADAM_STEP)
    delta = -ADAM_LR * (m_hat / (jnp.sqrt(v_hat) + ADAM_EPS) + ADAM_WD * w)
    return delta, m, v


def _adam_big(w, m, v, sum32, recv16, chip):
    r, c = w.shape
    tr = r if r <= 512 else (512 if r % 512 == 0 else 352)
    assert r % tr == 0

    def body(chip_ref, w_ref, m_ref, v_ref, s_ref, r_ref, g_out, d_out, m_out, v_out):
        me = chip_ref[0]
        g = None
        for q in range(N_CHIP):
            term = jnp.where(q == me, s_ref[...], r_ref[q].astype(F32))
            g = term if g is None else g + term
        d, mn, vn = _adam_math(w_ref[...], g, m_ref[...], v_ref[...])
        g_out[...] = g
        d_out[...] = d
        m_out[...] = mn
        v_out[...] = vn

    blk = pl.BlockSpec((tr, c), lambda i, s: (i, 0))
    grid_spec = pltpu.PrefetchScalarGridSpec(
        num_scalar_prefetch=1, grid=(r // tr,),
        in_specs=[blk, blk, blk, pl.BlockSpec((None, tr, c), lambda i, s: (s[0], i, 0)),
                  pl.BlockSpec((N_CHIP, tr, c), lambda i, s: (0, i, 0))],
        out_specs=[blk, blk, blk, blk])
    return pl.pallas_call(
        body, name="adamw_big", grid_spec=grid_spec,
        out_shape=[jax.ShapeDtypeStruct((r, c), F32)] * 4,
        compiler_params=_cparams(("arbitrary",)),
    )(chip, w, m, v, sum32, recv16)


def _adam_small(w, m, v, gathered):
    def body(w_ref, m_ref, v_ref, p_ref, g_out, d_out, m_out, v_out):
        g = p_ref[0]
        for dev in range(1, N_DEV):
            g = g + p_ref[dev]
        d, mn, vn = _adam_math(w_ref[...], g, m_ref[...], v_ref[...])
        g_out[...] = g
        d_out[...] = d
        m_out[...] = mn
        v_out[...] = vn

    return pl.pallas_call(
        body, name="adamw_small",
        out_shape=[jax.ShapeDtypeStruct(w.shape, F32)] * 4,
        compiler_params=_cparams(),
    )(w, m, v, gathered)


def _pick(n, prefer):
    for t in prefer:
        if n % t == 0:
            return t
    return n


def _ffn_fwd(x, g, wg, wu, wd, name):
    T, D = x.shape
    F = wg.shape[1]
    tm = _pick(T, (512, 256, 128))
    fc = _pick(F, (256, 128))

    def body(x_ref, g_ref, wg_ref, wu_ref, wd_ref, h_ref, n_ref, gate_ref, up_ref):
        xv = x_ref[...]
        n, _ = _rms_fwd(xv, g_ref[...])
        nb = n.astype(BF16)
        n_ref[...] = nb
        acc = jnp.zeros((tm, D), F32)
        for f0 in range(0, F, fc):
            gate = _dot(nb, wg_ref[:, f0:f0 + fc])
            up = _dot(nb, wu_ref[:, f0:f0 + fc])
            gate_ref[:, f0:f0 + fc] = gate.astype(BF16)
            up_ref[:, f0:f0 + fc] = up.astype(BF16)
            act = gate * (1.0 / (1.0 + jnp.exp(-gate))) * up
            acc = acc + _dot(act.astype(BF16), wd_ref[f0:f0 + fc, :])
        h_ref[...] = xv + 0.5 * acc

    return pl.pallas_call(
        body, name=name, grid=(T // tm,),
        in_specs=[_row_spec(tm, D), _const_spec((1, D)), _const_spec((D, F)), _const_spec((D, F)), _const_spec((F, D))],
        out_specs=[_row_spec(tm, D), _row_spec(tm, D), _row_spec(tm, F), _row_spec(tm, F)],
        out_shape=[jax.ShapeDtypeStruct((T, D), F32), jax.ShapeDtypeStruct((T, D), BF16),
                   jax.ShapeDtypeStruct((T, F), BF16), jax.ShapeDtypeStruct((T, F), BF16)],
        compiler_params=_cparams(("arbitrary",)),
    )(x, g, wg, wu, wd)


def _ffn_bwd(x, g, gate, up, dh, wg, wu, wd, name):
    T, D = x.shape
    F = wg.shape[1]
    tm = _pick(T, (256, 128))
    fc = _pick(F, (256, 128))

    def body(x_ref, g_ref, gate_ref, up_ref, dh_ref, wg_ref, wu_ref, wd_ref, dx_ref, dg_ref, act_ref, dgate_ref, dup_ref):
        xv = x_ref[...]
        gv = g_ref[...]
        _, r = _rms_fwd(xv, gv)
        dhv = dh_ref[...]
        dho = (0.5 * dhv).astype(BF16)
        dn = jnp.zeros((tm, D), F32)
        for f0 in range(0, F, fc):
            d_act = _dot_nt(dho, wd_ref[f0:f0 + fc, :])
            gt = gate_ref[:, f0:f0 + fc].astype(F32)
            u = up_ref[:, f0:f0 + fc].astype(F32)
            sg = 1.0 / (1.0 + jnp.exp(-gt))
            silu = gt * sg
            act_ref[:, f0:f0 + fc] = (silu * u).astype(BF16)
            d_up = (d_act * silu).astype(BF16)
            d_gate = (d_act * u * (sg + silu * (1.0 - sg))).astype(BF16)
            dup_ref[:, f0:f0 + fc] = d_up
            dgate_ref[:, f0:f0 + fc] = d_gate
            dn = dn + _dot_nt(d_gate, wg_ref[:, f0:f0 + fc]) + _dot_nt(d_up, wu_ref[:, f0:f0 + fc])
        dxn, dgp = _rms_bwd(xv, gv, r, dn)
        dx_ref[...] = dhv + dxn

        @pl.when(pl.program_id(0) == 0)
        def _():
            dg_ref[...] = jnp.zeros_like(dg_ref)

        dg_ref[...] += dgp

    return pl.pallas_call(
        body, name=name, grid=(T // tm,),
        in_specs=[_row_spec(tm, D), _const_spec((1, D)), _row_spec(tm, F), _row_spec(tm, F), _row_spec(tm, D),
                  _const_spec((D, F)), _const_spec((D, F)), _const_spec((F, D))],
        out_specs=[_row_spec(tm, D), _acc_spec((1, D)), _row_spec(tm, F), _row_spec(tm, F), _row_spec(tm, F)],
        out_shape=[jax.ShapeDtypeStruct((T, D), F32), jax.ShapeDtypeStruct((1, D), F32),
                   jax.ShapeDtypeStruct((T, F), BF16), jax.ShapeDtypeStruct((T, F), BF16), jax.ShapeDtypeStruct((T, F), BF16)],
        compiler_params=_cparams(("arbitrary",)),
    )(x, g, gate, up, dh, wg, wu, wd)


def _matmul_tn(a, b, name, scale=None):
    T, K = a.shape
    N = b.shape[1]
    tk = _pick(K, (1024, 1408, 512, 256, 128))
    tn = _pick(N, (1024, 1408, 1280, 768, 512, 256, 128))
    tt = _pick(T, (512, 256, 128))
    n_t = T // tt

    def body(a_ref, b_ref, o_ref):
        @pl.when(pl.program_id(2) == 0)
        def _():
            o_ref[...] = jnp.zeros_like(o_ref)

        o_ref[...] += _dot_tn(a_ref[...].astype(BF16), b_ref[...].astype(BF16))

        if scale is not None:
            @pl.when(pl.program_id(2) == n_t - 1)
            def _():
                o_ref[...] = o_ref[...] * scale

    return pl.pallas_call(
        body, name=name, grid=(K // tk, N // tn, n_t),
        in_specs=[pl.BlockSpec((tt, tk), lambda i, j, t: (t, i)), pl.BlockSpec((tt, tn), lambda i, j, t: (t, j))],
        out_specs=pl.BlockSpec((tk, tn), lambda i, j, t: (i, j)),
        out_shape=jax.ShapeDtypeStruct((K, N), F32),
        compiler_params=_cparams(("arbitrary", "arbitrary", "arbitrary")),
    )(a, b)


def _rope_spec(tm, S):
    per = S // tm
    return pl.BlockSpec((tm, HEAD_BLOCK), lambda i: (i % per, 0))


def _proj_fwd(h1, g_mix, w_in, g_q, wq, g_kv, wkv, rope_c, rope_s, S):
    T, D = h1.shape
    P = w_in.shape[1]
    tm = _pick(S, (512, 256, 128))
    QW = MLA_HEADS * HEAD_BLOCK
    VW = MLA_HEADS * V_HEAD
    SQ = SWA_HEADS * SWA_HEAD_DIM
    SK = SWA_KV_HEADS * SWA_HEAD_DIM
    o_cq, o_ckv, o_qs, o_ks, o_vs, o_kpe = 0, Q_LORA, Q_LORA + KV_LORA, Q_LORA + KV_LORA + SQ, Q_LORA + KV_LORA + SQ + SK, Q_LORA + KV_LORA + SQ + 2 * SK
    assert P == o_kpe + HEAD_BLOCK

    def body(h_ref, gm_ref, win_ref, gq_ref, wq_ref, gkv_ref, wkv_ref, c_ref, s_ref,
             u_ref, cq_ref, ckv_ref, cqn_ref, ckvn_ref, q_ref, k_ref, v_ref, qs_ref, ks_ref, vs_ref):
        u, _ = _rms_fwd(h_ref[...], gm_ref[...])
        ub = u.astype(BF16)
        u_ref[...] = ub
        proj = _dot(ub, win_ref[...])
        c_q = proj[:, o_cq:o_ckv]
        c_kv = proj[:, o_ckv:o_qs]
        cq_ref[...] = c_q
        ckv_ref[...] = c_kv
        qs_ref[...] = proj[:, o_qs:o_ks].astype(BF16)
        ks_ref[...] = proj[:, o_ks:o_vs].astype(BF16)
        vs_ref[...] = proj[:, o_vs:o_kpe].astype(BF16)
        cb, sb = c_ref[...], s_ref[...]
        kpe = proj[:, o_kpe:P]
        kpe = kpe * cb + _swap16(kpe) * sb
        cqn, _ = _rms_fwd(c_q, gq_ref[...])
        cqn = cqn.astype(BF16)
        cqn_ref[...] = cqn
        q = _dot(cqn, wq_ref[...])
        q = q * jnp.tile(cb, (1, MLA_HEADS)) + _swap16(q) * jnp.tile(sb, (1, MLA_HEADS))
        q_ref[...] = q.astype(BF16)
        ckvn, _ = _rms_fwd(c_kv, gkv_ref[...])
        ckvn = ckvn.astype(BF16)
        ckvn_ref[...] = ckvn
        kv = _dot(ckvn, wkv_ref[...])
        k_ref[...] = (kv[:, :QW] + jnp.tile(kpe, (1, MLA_HEADS))).astype(BF16)
        v_ref[...] = kv[:, QW:].astype(BF16)

    widths = [(D, BF16), (Q_LORA, F32), (KV_LORA, F32), (Q_LORA, BF16), (KV_LORA, BF16), (QW, BF16), (QW, BF16), (VW, BF16),
              (SQ, BF16), (SK, BF16), (SK, BF16)]
    return pl.pallas_call(
        body, name="proj_fwd", grid=(T // tm,),
        in_specs=[_row_spec(tm, D), _const_spec((1, D)), _const_spec((D, P)), _const_spec((1, Q_LORA)),
                  _const_spec((Q_LORA, QW)), _const_spec((1, KV_LORA)), _const_spec((KV_LORA, QW + VW)),
                  _rope_spec(tm, S), _rope_spec(tm, S)],
        out_specs=[_row_spec(tm, w) for w, _ in widths],
        out_shape=[jax.ShapeDtypeStruct((T, w), dt) for w, dt in widths],
        compiler_params=_cparams(("arbitrary",)),
    )(h1, g_mix, w_in, g_q, wq, g_kv, wkv, rope_c, rope_s)


def _proj_bwd(dq, dk, dv, dqs, dks, dvs, c_q, c_kv, h1, dh2, g_mix, w_in, g_q, wq, g_kv, wkv, rope_c, rope_s, S):
    T, D = h1.shape
    P = w_in.shape[1]
    tm = _pick(S, (256, 128))
    QW = MLA_HEADS * HEAD_BLOCK
    VW = MLA_HEADS * V_HEAD
    SQ = SWA_HEADS * SWA_HEAD_DIM
    SK = SWA_KV_HEADS * SWA_HEAD_DIM

    def body(dq_ref, dk_ref, dv_ref, dqs_ref, dks_ref, dvs_ref, cq_ref, ckv_ref, h_ref, dh2_ref, gm_ref, win_ref, gq_ref,
             wq_ref, gkv_ref, wkv_ref, c_ref, s_ref, dh1_ref, dproj_ref, dql_ref, dkv_ref, dgm_ref, dgq_ref, dgkv_ref):
        cb, sb = c_ref[...], s_ref[...]
        dqv = dq_ref[...]
        dql = dqv * jnp.tile(cb, (1, MLA_HEADS)) + _swap16(dqv * jnp.tile(sb, (1, MLA_HEADS)))
        dql = dql.astype(BF16)
        dql_ref[...] = dql
        c_q = cq_ref[...]
        _, rq = _rms_fwd(c_q, gq_ref[...])
        d_cq, dgq = _rms_bwd(c_q, gq_ref[...], rq, _dot_nt(dql, wq_ref[...]))

        dkv_all = dk_ref[...]
        dkpe = dkv_all[:, 0:HEAD_BLOCK]
        for h in range(1, MLA_HEADS):
            dkpe = dkpe + dkv_all[:, h * HEAD_BLOCK:(h + 1) * HEAD_BLOCK]
        lane = lax.broadcasted_iota(jnp.int32, dkpe.shape, 1)
        dkpe = jnp.where((lane >= 64) & (lane < 96), dkpe, 0.0)
        dkpe = dkpe * cb + _swap16(dkpe * sb)
        dkvc = jnp.concatenate([dkv_all.astype(BF16), dv_ref[...].astype(BF16)], axis=1)
        dkv_ref[...] = dkvc
        c_kv = ckv_ref[...]
        _, rkv = _rms_fwd(c_kv, gkv_ref[...])
        d_ckv, dgkv = _rms_bwd(c_kv, gkv_ref[...], rkv, _dot_nt(dkvc, wkv_ref[...]))

        dproj = jnp.concatenate([d_cq.astype(BF16), d_ckv.astype(BF16), dqs_ref[...].astype(BF16),
                                 dks_ref[...].astype(BF16), dvs_ref[...].astype(BF16), dkpe.astype(BF16)], axis=1)
        dproj_ref[...] = dproj
        hv = h_ref[...]
        _, rm = _rms_fwd(hv, gm_ref[...])
        dxn, dgm = _rms_bwd(hv, gm_ref[...], rm, _dot_nt(dproj, win_ref[...]))
        dh1_ref[...] = dh2_ref[...] + dxn

        @pl.when(pl.program_id(0) == 0)
        def _():
            dgm_ref[...] = jnp.zeros_like(dgm_ref)
            dgq_ref[...] = jnp.zeros_like(dgq_ref)
            dgkv_ref[...] = jnp.zeros_like(dgkv_ref)

        dgm_ref[...] += dgm
        dgq_ref[...] += dgq
        dgkv_ref[...] += dgkv

    return pl.pallas_call(
        body, name="proj_bwd", grid=(T // tm,),
        in_specs=[_row_spec(tm, QW), _row_spec(tm, QW), _row_spec(tm, VW), _row_spec(tm, SQ), _row_spec(tm, SK), _row_spec(tm, SK),
                  _row_spec(tm, Q_LORA), _row_spec(tm, KV_LORA), _row_spec(tm, D), _row_spec(tm, D),
                  _const_spec((1, D)), _const_spec((D, P)), _const_spec((1, Q_LORA)), _const_spec((Q_LORA, QW)),
                  _const_spec((1, KV_LORA)), _const_spec((KV_LORA, QW + VW)), _rope_spec(tm, S), _rope_spec(tm, S)],
        out_specs=[_row_spec(tm, D), _row_spec(tm, P), _row_spec(tm, QW), _row_spec(tm, QW + VW),
                   _acc_spec((1, D)), _acc_spec((1, Q_LORA)), _acc_spec((1, KV_LORA))],
        out_shape=[jax.ShapeDtypeStruct((T, D), F32), jax.ShapeDtypeStruct((T, P), BF16), jax.ShapeDtypeStruct((T, QW), BF16),
                   jax.ShapeDtypeStruct((T, QW + VW), BF16), jax.ShapeDtypeStruct((1, D), F32),
                   jax.ShapeDtypeStruct((1, Q_LORA), F32), jax.ShapeDtypeStruct((1, KV_LORA), F32)],
        compiler_params=_cparams(("arbitrary",)),
    )(dq, dk, dv, dqs, dks, dvs, c_q, c_kv, h1, dh2, g_mix, w_in, g_q, wq, g_kv, wkv, rope_c, rope_s)


def _out_fwd(o_mla, o_swa, g_mla, g_swa, w_o, h1):
    T, D = h1.shape
    W = o_mla.shape[1]
    tm = _pick(T, (512, 256, 128))

    def body(om_ref, os_ref, gm_ref, gs_ref, wo_ref, h_ref, h2_ref, oc_ref):
        a, _ = _rms_fwd(om_ref[...], gm_ref[...])
        b, _ = _rms_fwd(os_ref[...], gs_ref[...])
        oc = jnp.concatenate([a.astype(BF16), b.astype(BF16)], axis=1)
        oc_ref[...] = oc
        h2_ref[...] = h_ref[...] + _dot(oc, wo_ref[...])

    return pl.pallas_call(
        body, name="out_fwd", grid=(T // tm,),
        in_specs=[_row_spec(tm, W), _row_spec(tm, W), _const_spec((1, W)), _const_spec((1, W)), _const_spec((2 * W, D)),
                  _row_spec(tm, D)],
        out_specs=[_row_spec(tm, D), _row_spec(tm, 2 * W)],
        out_shape=[jax.ShapeDtypeStruct((T, D), F32), jax.ShapeDtypeStruct((T, 2 * W), BF16)],
        compiler_params=_cparams(("arbitrary",)),
    )(o_mla, o_swa, g_mla, g_swa, w_o, h1)


def _out_bwd(dh2, o_mla, o_swa, g_mla, g_swa, w_o):
    T, D = dh2.shape
    W = o_mla.shape[1]
    tm = _pick(T, (512, 256, 128))

    def body(dh_ref, om_ref, os_ref, gm_ref, gs_ref, wo_ref, dom_ref, dos_ref, dgm_ref, dgs_ref):
        doc = _dot_nt(dh_ref[...].astype(BF16), wo_ref[...])
        om, osw = om_ref[...], os_ref[...]
        _, ra = _rms_fwd(om, gm_ref[...])
        _, rb = _rms_fwd(osw, gs_ref[...])
        da, dga = _rms_bwd(om, gm_ref[...], ra, doc[:, :W])
        db, dgb = _rms_bwd(osw, gs_ref[...], rb, doc[:, W:])
        dom_ref[...] = da
        dos_ref[...] = db

        @pl.when(pl.program_id(0) == 0)
        def _():
            dgm_ref[...] = jnp.zeros_like(dgm_ref)
            dgs_ref[...] = jnp.zeros_like(dgs_ref)

        dgm_ref[...] += dga
        dgs_ref[...] += dgb

    return pl.pallas_call(
        body, name="out_bwd", grid=(T // tm,),
        in_specs=[_row_spec(tm, D), _row_spec(tm, W), _row_spec(tm, W), _const_spec((1, W)), _const_spec((1, W)),
                  _const_spec((2 * W, D))],
        out_specs=[_row_spec(tm, W), _row_spec(tm, W), _acc_spec((1, W)), _acc_spec((1, W))],
        out_shape=[jax.ShapeDtypeStruct((T, W), F32), jax.ShapeDtypeStruct((T, W), F32),
                   jax.ShapeDtypeStruct((1, W), F32), jax.ShapeDtypeStruct((1, W), F32)],
        compiler_params=_cparams(("arbitrary",)),
    )(dh2, o_mla, o_swa, g_mla, g_swa, w_o)


def _loss_head(h3, target, g_final):
    T, D = h3.shape
    tm = _pick(T, (512, 256, 128))

    def body(h_ref, t_ref, g_ref, dh_ref, loss_ref, dg_ref):
        hv, gv = h_ref[...], g_ref[...]
        y, r = _rms_fwd(hv, gv)
        diff = y - t_ref[...]
        dxn, dgp = _rms_bwd(hv, gv, r, diff * (1.0 / D))
        dh_ref[...] = dxn
        part = 0.5 * jnp.sum(jnp.sum(diff * diff, axis=-1, keepdims=True) * (1.0 / D), axis=0, keepdims=True)

        @pl.when(pl.program_id(0) == 0)
        def _():
            loss_ref[...] = jnp.zeros_like(loss_ref)
            dg_ref[...] = jnp.zeros_like(dg_ref)

        loss_ref[...] += jnp.broadcast_to(part, loss_ref.shape)
        dg_ref[...] += dgp

    return pl.pallas_call(
        body, name="loss_head", grid=(T // tm,),
        in_specs=[_row_spec(tm, D), _row_spec(tm, D), _const_spec((1, D))],
        out_specs=[_row_spec(tm, D), _acc_spec((1, 128)), _acc_spec((1, D))],
        out_shape=[jax.ShapeDtypeStruct((T, D), F32), jax.ShapeDtypeStruct((1, 128), F32), jax.ShapeDtypeStruct((1, D), F32)],
        compiler_params=_cparams(("arbitrary",)),
    )(h3, target, g_final)


def _half_mask(shape, half):
    lane = lax.broadcasted_iota(jnp.int32, shape, 1)
    return (lane < 64) if half == 0 else (lane >= 64)


def _mla_fwd(q, k, v, B, S):
    T = B * S
    tq = _pick(S, MLA_Q_TILES)
    nq = S // tq
    pairs = MLA_HEADS // 2

    def body(q_ref, k_ref, v_ref, o_ref, lse_ref):
        causal = lax.broadcasted_iota(jnp.int32, (tq, tq), 1) <= lax.broadcasted_iota(jnp.int32, (tq, tq), 0)
        low = _half_mask((tq, 2 * V_HEAD), 0)
        for qi in range(nq):
            rows = slice(qi * tq, (qi + 1) * tq)
            past = slice(0, qi * tq)
            outs, lses = [], []
            for half in range(2):
                lanes = slice(half * HEAD_BLOCK, (half + 1) * HEAD_BLOCK)
                qv = q_ref[rows, lanes]
                s_d = jnp.where(causal, _dot_nt(qv, k_ref[rows, lanes]) * MLA_SCALE, NEG)
                m = jnp.max(s_d, axis=-1, keepdims=True)
                if qi:
                    s_p = _dot_nt(qv, k_ref[past, lanes]) * MLA_SCALE
                    m = jnp.maximum(m, jnp.max(s_p, axis=-1, keepdims=True))
                p_d = jnp.exp(s_d - m)
                l = jnp.sum(p_d, axis=-1, keepdims=True)
                acc = _dot(p_d.astype(BF16), v_ref[rows, :])
                if qi:
                    p_p = jnp.exp(s_p - m)
                    l = l + jnp.sum(p_p, axis=-1, keepdims=True)
                    acc = acc + _dot(p_p.astype(BF16), v_ref[past, :])
                outs.append(acc * (1.0 / l))
                lses.append(jnp.broadcast_to(m + jnp.log(l), (tq, 2 * V_HEAD)))
            o_ref[rows, :] = jnp.where(low, outs[0], outs[1])
            lse_ref[rows, :] = jnp.where(low, lses[0], lses[1])

    blk2 = pl.BlockSpec((S, 2 * HEAD_BLOCK), lambda b, j: (b, j))
    blk1 = pl.BlockSpec((S, 2 * V_HEAD), lambda b, j: (b, j))
    return pl.pallas_call(
        body, name="mla_fwd", grid=(B, pairs),
        in_specs=[blk2, blk2, blk1], out_specs=[blk1, blk1],
        out_shape=[jax.ShapeDtypeStruct((T, pairs * 2 * V_HEAD), F32)] * 2,
        compiler_params=_cparams(("arbitrary", "arbitrary")),
    )(q, k, v)


def _mla_bwd(q, k, v, do, o, lse, B, S):
    T = B * S
    tq = _pick(S, MLA_Q_TILES)
    nq = S // tq
    pairs = MLA_HEADS // 2

    def body(q_ref, k_ref, v_ref, do_ref, o_ref, lse_ref, dq_ref, dk_ref, dv_ref, dk_acc):
        causal = lax.broadcasted_iota(jnp.int32, (tq, tq), 1) <= lax.broadcasted_iota(jnp.int32, (tq, tq), 0)
        for half in range(2):
            lanes = slice(half * HEAD_BLOCK, (half + 1) * HEAD_BLOCK)
            col = half * V_HEAD
            own = _half_mask((tq, 2 * V_HEAD), half)
            for qi in reversed(range(nq)):
                rows = slice(qi * tq, (qi + 1) * tq)
                past = slice(0, qi * tq)
                first = qi == nq - 1
                qv = q_ref[rows, lanes]
                dov = jnp.where(own, do_ref[rows, :], 0.0)
                dsum = jnp.sum(dov * o_ref[rows, :], axis=-1, keepdims=True)
                dob = dov.astype(BF16)
                lse_col = lse_ref[rows, col:col + 1]

                def grads(keys, mask):
                    kv, vv = k_ref[keys, lanes], v_ref[keys, :]
                    p = jnp.exp(_dot_nt(qv, kv) * MLA_SCALE - lse_col)
                    if mask is not None:
                        p = jnp.where(mask, p, 0.0)
                    ds = (p * (_dot_nt(dob, vv) - dsum) * MLA_SCALE).astype(BF16)
                    dk, dv = _dot_tn(ds, qv), _dot_tn(p.astype(BF16), dob)
                    if first:
                        dk_acc[keys, :] = dk
                    else:
                        dk_acc[keys, :] += dk
                    if first and half == 0:
                        dv_ref[keys, :] = dv
                    else:
                        dv_ref[keys, :] += dv
                    return _dot(ds, kv)

                dq = grads(rows, causal)
                if qi:
                    dq = dq + grads(past, None)
                dq_ref[rows, lanes] = dq
            dk_ref[:, lanes] = dk_acc[...]

    blk2 = pl.BlockSpec((S, 2 * HEAD_BLOCK), lambda b, j: (b, j))
    blk1 = pl.BlockSpec((S, 2 * V_HEAD), lambda b, j: (b, j))
    return pl.pallas_call(
        body, name="mla_bwd", grid=(B, pairs),
        in_specs=[blk2, blk2, blk1, blk1, blk1, blk1], out_specs=[blk2, blk2, blk1],
        out_shape=[jax.ShapeDtypeStruct((T, MLA_HEADS * HEAD_BLOCK), F32), jax.ShapeDtypeStruct((T, MLA_HEADS * HEAD_BLOCK), F32),
                   jax.ShapeDtypeStruct((T, pairs * 2 * V_HEAD), F32)],
        scratch_shapes=[pltpu.VMEM((S, HEAD_BLOCK), F32)],
        compiler_params=_cparams(("arbitrary", "arbitrary")),
    )(q, k, v, do, o, lse)


SWA_GROUP = SWA_HEADS // SWA_KV_HEADS
SWA_ROWS = SWA_GROUP * WINDOW


def _swa_stack(ref, r0):
    return jnp.concatenate([ref[pl.ds(r0, WINDOW), j * WINDOW:(j + 1) * WINDOW] for j in range(SWA_GROUP)], axis=0)


def _swa_unstack(low, high):
    sel = _half_mask((WINDOW, WINDOW), 0)
    return jnp.concatenate([jnp.where(sel, low[j * WINDOW:(j + 1) * WINDOW], high[j * WINDOW:(j + 1) * WINDOW])
                            for j in range(SWA_GROUP)], axis=1)


def _swa_sink_col(sink_ref, g):
    return jnp.concatenate([jnp.broadcast_to(sink_ref[g * SWA_GROUP + j:g * SWA_GROUP + j + 1, 0:1], (WINDOW, 1))
                            for j in range(SWA_GROUP)], axis=0)


def _swa_probs(qm, kcat, bias_g, sink, n):
    qi = lax.broadcasted_iota(jnp.int32, (SWA_ROWS, 2 * WINDOW), 0) % WINDOW
    col = lax.broadcasted_iota(jnp.int32, (SWA_ROWS, 2 * WINDOW), 1)
    valid = (col > qi) & (col <= qi + WINDOW) & ((col >= WINDOW) | (n > 0))
    s = jnp.where(valid, _dot_nt(qm, kcat) * SWA_SCALE + bias_g, NEG)
    m = jnp.maximum(jnp.max(s, axis=-1, keepdims=True), sink)
    e = jnp.exp(s - m)
    e_s = jnp.exp(sink - m)
    inv = 1.0 / (jnp.sum(e, axis=-1, keepdims=True) + e_s)
    return e * inv, e_s * inv


def _swa_fwd(q, k, v, bias, sinks, B, S):
    T = B * S
    W = WINDOW
    nb = S // W

    def body(q_ref, k_ref, v_ref, bias_ref, sink_ref, o_ref):
        sink_cols = [_swa_sink_col(sink_ref, g) for g in range(SWA_KV_HEADS)]

        def block(n, _):
            r0 = pl.multiple_of(n * W, W)
            p0 = pl.multiple_of(jnp.maximum(n - 1, 0) * W, W)
            kcat = jnp.concatenate([k_ref[pl.ds(p0, W), :], k_ref[pl.ds(r0, W), :]], axis=0)
            vcat = jnp.concatenate([v_ref[pl.ds(p0, W), :], v_ref[pl.ds(r0, W), :]], axis=0)
            qcat = _swa_stack(q_ref, r0)
            outs = []
            for g in range(SWA_KV_HEADS):
                qm = jnp.where(_half_mask(qcat.shape, g), qcat, jnp.zeros_like(qcat))
                p, _ = _swa_probs(qm, kcat, bias_ref[g], sink_cols[g], n)
                outs.append(_dot(p.astype(BF16), vcat))
            o_ref[pl.ds(r0, W), :] = _swa_unstack(outs[0], outs[1])
            return 0

        lax.fori_loop(0, nb, block, 0)

    SQ, SK = SWA_HEADS * SWA_HEAD_DIM, SWA_KV_HEADS * SWA_HEAD_DIM
    return pl.pallas_call(
        body, name="swa_fwd", grid=(B,),
        in_specs=[pl.BlockSpec((S, SQ), lambda b: (b, 0)), pl.BlockSpec((S, SK), lambda b: (b, 0)),
                  pl.BlockSpec((S, SK), lambda b: (b, 0)), _const_spec((SWA_KV_HEADS, SWA_ROWS, 2 * W)),
                  _const_spec((SWA_HEADS, 128))],
        out_specs=pl.BlockSpec((S, SQ), lambda b: (b, 0)),
        out_shape=jax.ShapeDtypeStruct((T, SQ), F32),
        compiler_params=_cparams(("arbitrary",)),
    )(q, k, v, bias, sinks)


def _swa_bwd(q, k, v, do, bias, sinks, B, S):
    T = B * S
    W = WINDOW
    nb = S // W

    def body(q_ref, k_ref, v_ref, do_ref, bias_ref, sink_ref, dq_ref, dk_ref, dv_ref, dbias_ref, dsink_ref):
        @pl.when(pl.program_id(0) == 0)
        def _():
            dbias_ref[...] = jnp.zeros_like(dbias_ref)
            dsink_ref[...] = jnp.zeros_like(dsink_ref)

        dk_ref[...] = jnp.zeros_like(dk_ref)
        dv_ref[...] = jnp.zeros_like(dv_ref)
        sink_cols = [_swa_sink_col(sink_ref, g) for g in range(SWA_KV_HEADS)]

        def block(n, _):
            r0 = pl.multiple_of(n * W, W)
            p0 = pl.multiple_of(jnp.maximum(n - 1, 0) * W, W)
            kcat = jnp.concatenate([k_ref[pl.ds(p0, W), :], k_ref[pl.ds(r0, W), :]], axis=0)
            vcat = jnp.concatenate([v_ref[pl.ds(p0, W), :], v_ref[pl.ds(r0, W), :]], axis=0)
            qcat = _swa_stack(q_ref, r0)
            docat = _swa_stack(do_ref, r0)
            dk = jnp.zeros((2 * W, W), F32)
            dv = jnp.zeros((2 * W, W), F32)
            dqs = []
            for g in range(SWA_KV_HEADS):
                own = _half_mask(qcat.shape, g)
                qm = jnp.where(own, qcat, jnp.zeros_like(qcat))
                dom = jnp.where(own, docat, 0.0).astype(BF16)
                p, p_s = _swa_probs(qm, kcat, bias_ref[g], sink_cols[g], n)
                dp = _dot_nt(dom, vcat)
                dsum = jnp.sum(p * dp, axis=-1, keepdims=True)
                ds = p * (dp - dsum)
                dbias_ref[g] += ds
                dsink_ref[g] += jnp.broadcast_to(-p_s * dsum, (SWA_ROWS, 128))
                dsb = (ds * SWA_SCALE).astype(BF16)
                dqs.append(_dot(dsb, kcat))
                dk = dk + _dot_tn(dsb, qm)
                dv = dv + _dot_tn(p.astype(BF16), dom)
            dq_ref[pl.ds(r0, W), :] = _swa_unstack(dqs[0], dqs[1])
            dk_ref[pl.ds(p0, W), :] += dk[:W]
            dv_ref[pl.ds(p0, W), :] += dv[:W]
            dk_ref[pl.ds(r0, W), :] += dk[W:]
            dv_ref[pl.ds(r0, W), :] += dv[W:]
            return 0

        lax.fori_loop(0, nb, block, 0)

    SQ, SK = SWA_HEADS * SWA_HEAD_DIM, SWA_KV_HEADS * SWA_HEAD_DIM
    row = lambda w: pl.BlockSpec((S, w), lambda b: (b, 0))
    return pl.pallas_call(
        body, name="swa_bwd", grid=(B,),
        in_specs=[row(SQ), row(SK), row(SK), row(SQ), _const_spec((SWA_KV_HEADS, SWA_ROWS, 2 * W)), _const_spec((SWA_HEADS, 128))],
        out_specs=[row(SQ), row(SK), row(SK), _acc_spec((SWA_KV_HEADS, SWA_ROWS, 2 * W)), _acc_spec((SWA_KV_HEADS, SWA_ROWS, 128))],
        out_shape=[jax.ShapeDtypeStruct((T, SQ), F32), jax.ShapeDtypeStruct((T, SK), F32), jax.ShapeDtypeStruct((T, SK), F32),
                   jax.ShapeDtypeStruct((SWA_KV_HEADS, SWA_ROWS, 2 * W), F32), jax.ShapeDtypeStruct((SWA_KV_HEADS, SWA_ROWS, 128), F32)],
        compiler_params=_cparams(("arbitrary",)),
    )(q, k, v, do, bias, sinks)


def _bias_expand(rel_bias, bucket):
    W = WINDOW

    def body(rb_ref, bucket_ref, out_ref):
        bk = bucket_ref[...]
        for h in range(SWA_HEADS):
            def add(b, acc):
                return jnp.where(bk == b, rb_ref[b, h], acc)

            out_ref[h] = lax.fori_loop(0, REL_BUCKETS, add, jnp.zeros((W, 2 * W), F32))

    return pl.pallas_call(
        body, name="bias_expand",
        in_specs=[pl.BlockSpec(memory_space=pltpu.SMEM), pl.BlockSpec(memory_space=pltpu.VMEM)],
        out_specs=pl.BlockSpec(memory_space=pltpu.VMEM),
        out_shape=jax.ShapeDtypeStruct((SWA_HEADS, W, 2 * W), F32),
        compiler_params=_cparams(),
    )(rel_bias, bucket)


def _bias_reduce(dbias, dsink_rows, bucket):
    def body(db_ref, ds_ref, bucket_ref, out_ref, sink_out_ref):
        for h in range(SWA_HEADS):
            sink_out_ref[h:h + 1, :] = jnp.sum(ds_ref[h], axis=0, keepdims=True)
        bk = bucket_ref[...]
        row = lax.broadcasted_iota(jnp.int32, (REL_BUCKETS, 128), 0)
        lane = lax.broadcasted_iota(jnp.int32, (REL_BUCKETS, 128), 1)
        acc = jnp.zeros((REL_BUCKETS, 128), F32)
        for h in range(SWA_HEADS):
            dbh = db_ref[h]

            def add(b, acc):
                part = jnp.sum(jnp.sum(jnp.where(bk == b, dbh, 0.0), axis=-1, keepdims=True), axis=0, keepdims=True)
                return jnp.where((row == b) & (lane == h), part, acc)

            acc = lax.fori_loop(0, REL_BUCKETS, add, acc)
        out_ref[...] = acc

    return pl.pallas_call(
        body, name="bias_reduce",
        out_shape=[jax.ShapeDtypeStruct((REL_BUCKETS, 128), F32), jax.ShapeDtypeStruct((SWA_HEADS, 128), F32)],
        compiler_params=_cparams(),
    )(dbias, dsink_rows, bucket)


def _t5_bucket(dist):
    n = jnp.maximum(dist, 0)
    max_exact = REL_BUCKETS // 2
    nf = jnp.maximum(n, 1).astype(F32)
    large = max_exact + (jnp.log(nf / max_exact) / math.log(REL_MAX_DIST / max_exact) * (REL_BUCKETS - max_exact)).astype(jnp.int32)
    large = jnp.minimum(large, REL_BUCKETS - 1)
    return jnp.where(n < max_exact, n, large)


def _pair_heads(a, axis, inverse=False):
    shp = a.shape
    a = a.reshape(shp[:axis] + ((4, 2, 64) if inverse else (2, 4, 64)) + shp[axis + 1:])
    a = jnp.swapaxes(a, axis, axis + 1)
    return a.reshape(shp)


def _col_full(g):
    return jnp.transpose(g, (1, 0, 2)).reshape(g.shape[1], N_DEV * g.shape[2])


def _row_full(g):
    return g.reshape(N_DEV * g.shape[1], g.shape[2])


def _col_blocks(w):
    r = w.shape[0]
    return jnp.transpose(w.reshape(r, N_CHIP, 2, -1), (2, 1, 0, 3))


def _row_blocks(w):
    c = w.shape[1]
    return jnp.transpose(w.reshape(N_CHIP, 2, -1, c), (1, 0, 2, 3))


_IN_WIDTHS = (Q_LORA, KV_LORA, QK_ROPE, SWA_HEADS * SWA_HEAD_DIM, SWA_KV_HEADS * SWA_HEAD_DIM, SWA_KV_HEADS * SWA_HEAD_DIM)


def _split_in(w):
    parts, o = [], 0
    for wd in _IN_WIDTHS:
        parts.append(w[:, o:o + wd])
        o += wd
    return parts


def _win_to_kernel(w):
    cq, ckv, kpe, qs, ks, vs = _split_in(w)
    kpe = jnp.pad(kpe, ((0, 0), (64, 32)))
    return jnp.concatenate([cq, ckv, _pair_heads(qs, 1), ks, vs, kpe], axis=1)


def _win_from_kernel(dw):
    o = [0, Q_LORA, Q_LORA + KV_LORA, Q_LORA + KV_LORA + 512, Q_LORA + KV_LORA + 640, Q_LORA + KV_LORA + 768]
    cq, ckv, qs, ks, vs, kpe = [dw[:, a:b] for a, b in zip(o, o[1:] + [dw.shape[1]])]
    return jnp.concatenate([cq, ckv, kpe[:, 64:96], _pair_heads(qs, 1, inverse=True), ks, vs], axis=1)


def _wq_to_kernel(w):
    return jnp.pad(w.reshape(Q_LORA, MLA_HEADS, QK_NOPE + QK_ROPE), ((0, 0), (0, 0), (0, 32))).reshape(Q_LORA, MLA_HEADS * HEAD_BLOCK)


def _wq_from_kernel(dw):
    return dw.reshape(Q_LORA, MLA_HEADS, HEAD_BLOCK)[:, :, :QK_NOPE + QK_ROPE].reshape(Q_LORA, -1)


def _wkv_to_kernel(w):
    w3 = w.reshape(KV_LORA, MLA_HEADS, QK_NOPE + V_HEAD)
    kpart = jnp.pad(w3[:, :, :QK_NOPE], ((0, 0), (0, 0), (0, 64))).reshape(KV_LORA, MLA_HEADS * HEAD_BLOCK)
    return jnp.concatenate([kpart, w3[:, :, QK_NOPE:].reshape(KV_LORA, MLA_HEADS * V_HEAD)], axis=1)


def _wkv_from_kernel(dw):
    dk = dw[:, :MLA_HEADS * HEAD_BLOCK].reshape(KV_LORA, MLA_HEADS, HEAD_BLOCK)[:, :, :QK_NOPE]
    dv = dw[:, MLA_HEADS * HEAD_BLOCK:].reshape(KV_LORA, MLA_HEADS, V_HEAD)
    return jnp.concatenate([dk, dv], axis=2).reshape(KV_LORA, -1)


def _wo_to_kernel(w):
    return jnp.concatenate([w[:512], _pair_heads(w[512:], 0)], axis=0)


def _wo_from_kernel(dw):
    return jnp.concatenate([dw[:512], _pair_heads(dw[512:], 0, inverse=True)], axis=0)


_SMALL = (("g_ffn1", 1024), ("g_mix", 1024), ("g_q_a", 256), ("g_kv_a", 128), ("attn_sinks", 8), ("rel_bias", 256),
          ("g_out_mla", 512), ("g_out_swa", 512), ("g_ffn2", 1024), ("g_final", 1024), ("loss", 1))


def _pack_small(vals):
    rows = []
    for name, size in _SMALL:
        v = vals[name].reshape(-1).astype(F32)
        nrow = -(-size // 128)
        rows.append(jnp.pad(v, (0, nrow * 128 - size)).reshape(nrow, 128))
    packed = jnp.concatenate(rows, axis=0)
    return jnp.pad(packed, ((0, SMALL_ROWS - packed.shape[0]), (0, 0)))


def _unpack_small(packed, shapes):
    out, r = {}, 0
    for name, size in _SMALL:
        nrow = -(-size // 128)
        out[name] = packed[r:r + nrow].reshape(-1)[:size].reshape(shapes[name])
        r += nrow
    return out


def _local_step(x, target, gains, wts):
    B, S, D = x.shape
    T = B * S
    x2 = x.reshape(T, D)
    t2 = target.reshape(T, D)

    pos = jnp.arange(S, dtype=F32)
    inv_freq = ROPE_THETA ** (-jnp.arange(0, QK_ROPE, 2, dtype=F32) / QK_ROPE)
    ang = pos[:, None] * inv_freq[None, :]
    cos, sin = jnp.cos(ang), jnp.sin(ang)
    ones, zeros = jnp.ones((S, 64), F32), jnp.zeros((S, 32), F32)
    rope_c = jnp.concatenate([ones, cos, cos, zeros], axis=1)
    rope_s = jnp.concatenate([0.0 * ones, -sin, sin, zeros], axis=1)

    qi = jnp.arange(WINDOW)[:, None]
    kj = jnp.arange(2 * WINDOW)[None, :]
    bucket = _t5_bucket(qi + WINDOW - kj).astype(jnp.int32)
    bias = _bias_expand(gains["rel_bias"], bucket).reshape(SWA_KV_HEADS, SWA_ROWS, 2 * WINDOW)
    sinks = jnp.broadcast_to(gains["attn_sinks"].reshape(SWA_HEADS, 1), (SWA_HEADS, 128))
    g_swa = _pair_heads(gains["g_out_swa"], 1)

    h1, n1, gate1, up1 = _ffn_fwd(x2, gains["g_ffn1"], wts["g1"], wts["u1"], wts["d1"], "ffn1_fwd")
    (u, c_q, c_kv, cqn, ckvn, q, k, v, qs, ks, vs) = _proj_fwd(
        h1, gains["g_mix"], wts["in"], gains["g_q_a"], wts["q"], gains["g_kv_a"], wts["kv"], rope_c, rope_s, S)
    o_mla, lse = _mla_fwd(q, k, v, B, S)
    o_swa = _swa_fwd(qs, ks, vs, bias, sinks, B, S)
    h2, oc = _out_fwd(o_mla, o_swa, gains["g_out_mla"], g_swa, wts["o"], h1)
    h3, n2, gate2, up2 = _ffn_fwd(h2, gains["g_ffn2"], wts["g2"], wts["u2"], wts["d2"], "ffn2_fwd")
    dh3, loss, dg_final = _loss_head(h3, t2, gains["g_final"].reshape(1, D))

    dh2, dg_ffn2, act2, dgate2, dup2 = _ffn_bwd(h2, gains["g_ffn2"], gate2, up2, dh3, wts["g2"], wts["u2"], wts["d2"], "ffn2_bwd")
    grads = {"g2": _matmul_tn(n2, dgate2, "dw_gate2"), "u2": _matmul_tn(n2, dup2, "dw_up2"), "d2": _matmul_tn(act2, dh3, "dw_down2", 0.5)}
    do_mla, do_swa, dg_mla, dg_swa = _out_bwd(dh2, o_mla, o_swa, gains["g_out_mla"], g_swa, wts["o"])
    grads["o"] = _matmul_tn(oc, dh2, "dw_o")
    dq, dk, dv = _mla_bwd(q, k, v, do_mla, o_mla, lse, B, S)
    dqs, dks, dvs, dbias, dsink = _swa_bwd(qs, ks, vs, do_swa, bias, sinks, B, S)
    dh1, dproj, dql, dkvc, dg_mix, dg_q, dg_kv = _proj_bwd(
        dq, dk, dv, dqs, dks, dvs, c_q, c_kv, h1, dh2, gains["g_mix"], wts["in"], gains["g_q_a"], wts["q"],
        gains["g_kv_a"], wts["kv"], rope_c, rope_s, S)
    grads["in"] = _matmul_tn(u, dproj, "dw_in")
    grads["q"] = _matmul_tn(cqn, dql, "dw_q")
    grads["kv"] = _matmul_tn(ckvn, dkvc, "dw_kv")
    dx, dg_ffn1, act1, dgate1, dup1 = _ffn_bwd(x2, gains["g_ffn1"], gate1, up1, dh1, wts["g1"], wts["u1"], wts["d1"], "ffn1_bwd")
    grads["g1"] = _matmul_tn(n1, dgate1, "dw_gate1")
    grads["u1"] = _matmul_tn(n1, dup1, "dw_up1")
    grads["d1"] = _matmul_tn(act1, dh1, "dw_down1", 0.5)

    d_rel_bias, d_sinks = _bias_reduce(dbias.reshape(SWA_HEADS, WINDOW, 2 * WINDOW), dsink.reshape(SWA_HEADS, WINDOW, 128), bucket)
    small = {
        "g_ffn1": dg_ffn1, "g_mix": dg_mix, "g_q_a": dg_q, "g_kv_a": dg_kv, "attn_sinks": d_sinks[:, 0],
        "rel_bias": d_rel_bias[:, :SWA_HEADS], "g_out_mla": dg_mla, "g_out_swa": _pair_heads(dg_swa, 1, inverse=True),
        "g_ffn2": dg_ffn2, "g_final": dg_final, "loss": loss[0, 0:1],
    }
    return dx.reshape(B, S, D), grads, small


_WEIGHTS = ("g_ffn1", "w_ffn1_gate", "w_ffn1_up", "w_ffn1_down", "g_mix", "w_in", "g_q_a", "w_q_b", "g_kv_a", "w_kv_b",
            "attn_sinks", "rel_bias", "g_out_mla", "g_out_swa", "w_o", "g_ffn2", "w_ffn2_gate", "w_ffn2_up", "w_ffn2_down",
            "g_final")
_BIG = (("w_ffn1_gate", "g1", True), ("w_ffn1_up", "u1", True), ("w_ffn1_down", "d1", False), ("w_in", "in", True),
        ("w_q_b", "q", True), ("w_kv_b", "kv", True), ("w_o", "o", False), ("w_ffn2_gate", "g2", True),
        ("w_ffn2_up", "u2", True), ("w_ffn2_down", "d2", False))
_TO_KERNEL = {"in": _win_to_kernel, "q": _wq_to_kernel, "kv": _wkv_to_kernel, "o": _wo_to_kernel}
_FROM_KERNEL = {"in": _win_from_kernel, "q": _wq_from_kernel, "kv": _wkv_from_kernel, "o": _wo_from_kernel}


def kernel(x, g_ffn1, w_ffn1_gate, w_ffn1_up, w_ffn1_down, g_mix, w_in, g_q_a, w_q_b, g_kv_a, w_kv_b, attn_sinks, rel_bias, g_out_mla, g_out_swa, w_o, g_ffn2, w_ffn2_gate, w_ffn2_up, w_ffn2_down, g_final, loss_target, m_g_ffn1, m_w_ffn1_gate, m_w_ffn1_up, m_w_ffn1_down, m_g_mix, m_w_in, m_g_q_a, m_w_q_b, m_g_kv_a, m_w_kv_b, m_attn_sinks, m_rel_bias, m_g_out_mla, m_g_out_swa, m_w_o, m_g_ffn2, m_w_ffn2_gate, m_w_ffn2_up, m_w_ffn2_down, m_g_final, v_g_ffn1, v_w_ffn1_gate, v_w_ffn1_up, v_w_ffn1_down, v_g_mix, v_w_in, v_g_q_a, v_w_q_b, v_g_kv_a, v_w_kv_b, v_attn_sinks, v_rel_bias, v_g_out_mla, v_g_out_swa, v_w_o, v_g_ffn2, v_w_ffn2_gate, v_w_ffn2_up, v_w_ffn2_down, v_g_final):
    w = dict(zip(_WEIGHTS, (g_ffn1, w_ffn1_gate, w_ffn1_up, w_ffn1_down, g_mix, w_in, g_q_a, w_q_b, g_kv_a, w_kv_b, attn_sinks,
                            rel_bias, g_out_mla, g_out_swa, w_o, g_ffn2, w_ffn2_gate, w_ffn2_up, w_ffn2_down, g_final)))
    m = dict(zip(_WEIGHTS, (m_g_ffn1, m_w_ffn1_gate, m_w_ffn1_up, m_w_ffn1_down, m_g_mix, m_w_in, m_g_q_a, m_w_q_b, m_g_kv_a,
                            m_w_kv_b, m_attn_sinks, m_rel_bias, m_g_out_mla, m_g_out_swa, m_w_o, m_g_ffn2, m_w_ffn2_gate,
                            m_w_ffn2_up, m_w_ffn2_down, m_g_final)))
    v = dict(zip(_WEIGHTS, (v_g_ffn1, v_w_ffn1_gate, v_w_ffn1_up, v_w_ffn1_down, v_g_mix, v_w_in, v_g_q_a, v_w_q_b, v_g_kv_a,
                            v_w_kv_b, v_attn_sinks, v_rel_bias, v_g_out_mla, v_g_out_swa, v_w_o, v_g_ffn2, v_w_ffn2_gate,
                            v_w_ffn2_up, v_w_ffn2_down, v_g_final)))
    core = lax.axis_index("c").astype(jnp.int32).reshape(1)
    chip = (2 * lax.axis_index("x") + lax.axis_index("y")).astype(jnp.int32).reshape(1)

    shards = _cast_bf16([w[name][0] for name, _, _ in _BIG])
    gathered = _all_gather(shards, "weights_all_gather")
    wts = {}
    for (name, key, by_col), g in zip(_BIG, gathered):
        full = _col_full(g) if by_col else _row_full(g)
        wts[key] = _TO_KERNEL[key](full) if key in _TO_KERNEL else full

    small_names = [n for n, _ in _SMALL if n != "loss"]
    gains = {n: (w[n] if n in ("rel_bias", "g_final") else w[n].reshape(1, -1)) for n in small_names}
    grad_x, grads, small = _local_step(x, loss_target, gains, wts)

    blocks = []
    for name, key, by_col in _BIG:
        g = grads[key]
        if key in _FROM_KERNEL:
            g = _FROM_KERNEL[key](g)
        blocks.append(_col_blocks(g) if by_col else _row_blocks(g))
    from_sibling = _exchange_cores(blocks)
    sums = [_add_core_halves(b, r, core) for b, r in zip(blocks, from_sibling)]
    from_chips = _exchange_chips([s16 for _, s16 in sums])

    small_gathered = _all_gather([_pack_small(small)], "small_all_gather")[0]

    out_g, out_d, out_m, out_v = {}, {}, {}, {}
    for (name, key, by_col), (s32, _), r16 in zip(_BIG, sums, from_chips):
        g, d, mn, vn = _adam_big(w[name][0], m[name][0], v[name][0], s32, r16, chip)
        out_g[name], out_d[name], out_m[name], out_v[name] = g[None], d[None], mn[None], vn[None]
    shapes = {n: w[n].shape for n in small_names}
    shapes["loss"] = ()
    zero = jnp.zeros((), F32)
    ws, ms, vs = ({**{n: src[n] for n in small_names}, "loss": zero} for src in (w, m, v))
    sg, sd, sm, sv = _adam_small(_pack_small(ws), _pack_small(ms), _pack_small(vs), small_gathered)
    sg, sd, sm, sv = (_unpack_small(p, shapes) for p in (sg, sd, sm, sv))
    for n in small_names:
        out_g[n], out_d[n], out_m[n], out_v[n] = sg[n], sd[n], sm[n], sv[n]

    return (sg["loss"], grad_x, *[out_g[n] for n in _WEIGHTS], *[out_d[n] for n in _WEIGHTS],
            *[out_m[n] for n in _WEIGHTS], *[out_v[n] for n in _WEIGHTS])
```

```python
import functools
import math

import jax
import jax.numpy as jnp
from jax import lax
from jax.experimental import pallas as pl
from jax.experimental.pallas import tpu as pltpu

F32 = jnp.float32
BF16 = jnp.bfloat16
MESH = pl.DeviceIdType.MESH

EPS = 1e-6
NEG = -1e30
N_DEV = 8
N_CHIP = 4

MLA_HEADS = 8
Q_LORA = 256
KV_LORA = 128
QK_NOPE = 64
QK_ROPE = 32
V_HEAD = 64
ROPE_THETA = 10000.0
SWA_HEADS = 8
SWA_KV_HEADS = 2
SWA_HEAD_DIM = 64
WINDOW = 128
REL_BUCKETS = 32
REL_MAX_DIST = 128
HEAD_BLOCK = 128
MLA_Q_TILES = (512, 256, 128)
MLA_SCALE = (QK_NOPE + QK_ROPE) ** -0.5
SWA_SCALE = SWA_HEAD_DIM ** -0.5

ADAM_LR = 0.001
ADAM_B1 = 0.9
ADAM_B2 = 0.999
ADAM_EPS = 1e-08
ADAM_WD = 0.01
ADAM_STEP = 10

VMEM_LIMIT = 56 * 1024 * 1024
SMALL_ROWS = 48


def _cparams(semantics=None):
    return pltpu.CompilerParams(dimension_semantics=semantics, vmem_limit_bytes=VMEM_LIMIT)


def _dot(a, b):
    return jnp.dot(a, b, preferred_element_type=F32)


def _dot_nt(a, b):
    return lax.dot_general(a, b, (((1,), (1,)), ((), ())), preferred_element_type=F32)


def _dot_tn(a, b):
    return lax.dot_general(a, b, (((0,), (0,)), ((), ())), preferred_element_type=F32)


def _rms_fwd(x, g):
    r = lax.rsqrt(jnp.mean(x * x, axis=-1, keepdims=True) + EPS)
    return x * r * g, r


def _rms_bwd(x, g, r, dy):
    xh = x * r
    dyg = dy * g
    dx = r * (dyg - xh * jnp.mean(dyg * xh, axis=-1, keepdims=True))
    return dx, jnp.sum(dy * xh, axis=0, keepdims=True)


def _swap16(x):
    n = x.shape[-1]
    up = pltpu.roll(x, n - 16, 1)
    down = pltpu.roll(x, 16, 1)
    lane = lax.broadcasted_iota(jnp.int32, x.shape, 1) % HEAD_BLOCK
    return jnp.where((lane >= 64) & (lane < 80), up, jnp.where((lane >= 80) & (lane < 96), down, 0.0))


def _const_spec(shape):
    return pl.BlockSpec(shape, lambda *_: (0,) * len(shape), pipeline_mode=pl.Buffered(1))


def _acc_spec(shape):
    return pl.BlockSpec(shape, lambda *_: (0,) * len(shape))


def _row_spec(tm, width):
    return pl.BlockSpec((tm, width), lambda i: (i, 0))


def _cast_bf16(arrays):
    n = len(arrays)

    def body(*refs):
        for src, dst in zip(refs[:n], refs[n:]):
            dst[...] = src[...].astype(BF16)

    return pl.pallas_call(
        body, name="cast_weights",
        out_shape=[jax.ShapeDtypeStruct(a.shape, BF16) for a in arrays],
        compiler_params=_cparams(),
    )(*arrays)


def _all_gather(arrays, name):
    n = len(arrays)

    def body(*refs):
        ins, outs = refs[:n], refs[n:2 * n]
        send_sems, recv_sems, local_sems = refs[2 * n:]
        x, y, c = lax.axis_index("x"), lax.axis_index("y"), lax.axis_index("c")
        me, sibling = (x, y, c), (x, y, 1 - c)
        chips = [(1 - x, y), (x, 1 - y), (1 - x, 1 - y)]

        def slot(a, dev):
            return outs[a].at[4 * dev[0] + 2 * dev[1] + dev[2]]

        def copy(a, k, block, to, src=None):
            return pltpu.make_async_remote_copy(
                src_ref=slot(a, block) if src is None else src, dst_ref=slot(a, block),
                send_sem=send_sems.at[a, k], recv_sem=recv_sems.at[a, k], device_id=to, device_id_type=MESH)

        mine = [pltpu.make_async_copy(ins[a], slot(a, me), local_sems.at[a]) for a in range(n)]
        for cp in mine:
            cp.start()
        first = []
        for a in range(n):
            first.append(copy(a, 0, me, sibling, src=ins[a]))
            first += [copy(a, 1 + j, me, (*chip, c), src=ins[a]) for j, chip in enumerate(chips)]
        for cp in first:
            cp.start()
        passed = []
        for j, chip in enumerate(chips):
            for a in range(n):
                copy(a, 1 + j, (*chip, c), me).wait_recv()
                cp = copy(a, 4 + j, (*chip, c), sibling)
                cp.start()
                passed.append(cp)
        for a in range(n):
            copy(a, 0, sibling, me).wait_recv()
            for j, chip in enumerate(chips):
                copy(a, 4 + j, (*chip, 1 - c), me).wait_recv()
        for cp in first + passed:
            cp.wait_send()
        for cp in mine:
            cp.wait()

    any_spec = pl.BlockSpec(memory_space=pl.ANY)
    return pl.pallas_call(
        body, name=name,
        out_shape=[jax.ShapeDtypeStruct((N_DEV,) + a.shape, a.dtype) for a in arrays],
        in_specs=[any_spec] * n, out_specs=[any_spec] * n,
        scratch_shapes=[pltpu.SemaphoreType.DMA((n, 7)), pltpu.SemaphoreType.DMA((n, 7)), pltpu.SemaphoreType.DMA((n,))],
    )(*arrays)


def _exchange_cores(arrays):
    n = len(arrays)

    def body(*refs):
        ins, outs = refs[:n], refs[n:2 * n]
        send_sems, recv_sems = refs[2 * n:]
        x, y, c = lax.axis_index("x"), lax.axis_index("y"), lax.axis_index("c")
        copies = [pltpu.make_async_remote_copy(
            src_ref=ins[a].at[1 - c], dst_ref=outs[a], send_sem=send_sems.at[a], recv_sem=recv_sems.at[a],
            device_id=(x, y, 1 - c), device_id_type=MESH) for a in range(n)]
        for cp in copies:
            cp.start()
        for cp in copies:
            cp.wait()

    any_spec = pl.BlockSpec(memory_space=pl.ANY)
    return pl.pallas_call(
        body, name="grad_exchange_cores",
        out_shape=[jax.ShapeDtypeStruct(a.shape[1:], a.dtype) for a in arrays],
        in_specs=[any_spec] * n, out_specs=[any_spec] * n,
        scratch_shapes=[pltpu.SemaphoreType.DMA((n,)), pltpu.SemaphoreType.DMA((n,))],
    )(*arrays)


def _exchange_chips(arrays):
    n = len(arrays)

    def body(*refs):
        ins, outs = refs[:n], refs[n:2 * n]
        send_sems, recv_sems, local_sems = refs[2 * n:]
        x, y, c = lax.axis_index("x"), lax.axis_index("y"), lax.axis_index("c")
        my_chip = 2 * x + y
        chips = [(1 - x, y), (x, 1 - y), (1 - x, 1 - y)]
        mine = [pltpu.make_async_copy(ins[a].at[my_chip], outs[a].at[my_chip], local_sems.at[a]) for a in range(n)]
        for cp in mine:
            cp.start()
        def copy(a, j, src_chip, dst_chip, to):
            return pltpu.make_async_remote_copy(
                src_ref=ins[a].at[dst_chip], dst_ref=outs[a].at[src_chip],
                send_sem=send_sems.at[a, j], recv_sem=recv_sems.at[a, j], device_id=to, device_id_type=MESH)

        sends = [copy(a, j, my_chip, 2 * chip[0] + chip[1], (*chip, c)) for a in range(n) for j, chip in enumerate(chips)]
        for cp in sends:
            cp.start()
        for a in range(n):
            for j, chip in enumerate(chips):
                copy(a, j, 2 * chip[0] + chip[1], my_chip, (x, y, c)).wait_recv()
        for cp in sends:
            cp.wait_send()
        for cp in mine:
            cp.wait()

    any_spec = pl.BlockSpec(memory_space=pl.ANY)
    return pl.pallas_call(
        body, name="grad_exchange_chips",
        out_shape=[jax.ShapeDtypeStruct(a.shape, a.dtype) for a in arrays],
        in_specs=[any_spec] * n, out_specs=[any_spec] * n,
        scratch_shapes=[pltpu.SemaphoreType.DMA((n, 3)), pltpu.SemaphoreType.DMA((n, 3)), pltpu.SemaphoreType.DMA((n,))],
    )(*arrays)


def _add_core_halves(mine_all, recv, core):
    _, nchip, r, c = mine_all.shape
    tr = r if r <= 512 else (512 if r % 512 == 0 else 352)
    assert r % tr == 0

    def body(core_ref, a_ref, b_ref, o32_ref, o16_ref):
        s = a_ref[...] + b_ref[...]
        o32_ref[...] = s
        o16_ref[...] = s.astype(BF16)

    grid_spec = pltpu.PrefetchScalarGridSpec(
        num_scalar_prefetch=1, grid=(nchip, r // tr),
        in_specs=[pl.BlockSpec((None, None, tr, c), lambda q, i, s: (s[0], q, i, 0)),
                  pl.BlockSpec((None, tr, c), lambda q, i, s: (q, i, 0))],
        out_specs=[pl.BlockSpec((None, tr, c), lambda q, i, s: (q, i, 0)),
                   pl.BlockSpec((None, tr, c), lambda q, i, s: (q, i, 0))])
    return pl.pallas_call(
        body, name="grad_add_cores", grid_spec=grid_spec,
        out_shape=[jax.ShapeDtypeStruct((nchip, r, c), F32), jax.ShapeDtypeStruct((nchip, r, c), BF16)],
        compiler_params=_cparams(("arbitrary", "arbitrary")),
    )(core, mine_all, recv)


def _adam_math(w, g, m, v):
    m = ADAM_B1 * m + (1.0 - ADAM_B1) * g
    v = ADAM_B2 * v + (1.0 - ADAM_B2) * (g * g)
    m_hat = m / (1.0 - ADAM_B1 ** ADAM_STEP)
    v_hat = v / (1.0 - ADAM_B2 ** ADAM_STEP)
    delta = -ADAM_LR * (m_hat / (jnp.sqrt(v_hat) + ADAM_EPS) + ADAM_WD * w)
    return delta, m, v


def _adam_big(w, m, v, sum32, recv16, chip):
    r, c = w.shape
    tr = r if r <= 512 else (512 if r % 512 == 0 else 352)
    assert r % tr == 0

    def body(chip_ref, w_ref, m_ref, v_ref, s_ref, r_ref, g_out, d_out, m_out, v_out):
        me = chip_ref[0]
        g = None
        for q in range(N_CHIP):
            term = jnp.where(q == me, s_ref[...], r_ref[q].astype(F32))
            g = term if g is None else g + term
        d, mn, vn = _adam_math(w_ref[...], g, m_ref[...], v_ref[...])
        g_out[...] = g
        d_out[...] = d
        m_out[...] = mn
        v_out[...] = vn

    blk = pl.BlockSpec((tr, c), lambda i, s: (i, 0))
    grid_spec = pltpu.PrefetchScalarGridSpec(
        num_scalar_prefetch=1, grid=(r // tr,),
        in_specs=[blk, blk, blk, pl.BlockSpec((None, tr, c), lambda i, s: (s[0], i, 0)),
                  pl.BlockSpec((N_CHIP, tr, c), lambda i, s: (0, i, 0))],
        out_specs=[blk, blk, blk, blk])
    return pl.pallas_call(
        body, name="adamw_big", grid_spec=grid_spec,
        out_shape=[jax.ShapeDtypeStruct((r, c), F32)] * 4,
        compiler_params=_cparams(("arbitrary",)),
    )(chip, w, m, v, sum32, recv16)


def _adam_small(w, m, v, gathered):
    def body(w_ref, m_ref, v_ref, p_ref, g_out, d_out, m_out, v_out):
        g = p_ref[0]
        for dev in range(1, N_DEV):
            g = g + p_ref[dev]
        d, mn, vn = _adam_math(w_ref[...], g, m_ref[...], v_ref[...])
        g_out[...] = g
        d_out[...] = d
        m_out[...] = mn
        v_out[...] = vn

    return pl.pallas_call(
        body, name="adamw_small",
        out_shape=[jax.ShapeDtypeStruct(w.shape, F32)] * 4,
        compiler_params=_cparams(),
    )(w, m, v, gathered)


def _pick(n, prefer):
    for t in prefer:
        if n % t == 0:
            return t
    return n


def _ffn_fwd(x, g, wg, wu, wd, name):
    T, D = x.shape
    F = wg.shape[1]
    tm = _pick(T, (512, 256, 128))
    fc = _pick(F, (256, 128))

    def body(x_ref, g_ref, wg_ref, wu_ref, wd_ref, h_ref, n_ref, gate_ref, up_ref):
        xv = x_ref[...]
        n, _ = _rms_fwd(xv, g_ref[...])
        nb = n.astype(BF16)
        n_ref[...] = nb
        acc = jnp.zeros((tm, D), F32)
        for f0 in range(0, F, fc):
            gate = _dot(nb, wg_ref[:, f0:f0 + fc])
            up = _dot(nb, wu_ref[:, f0:f0 + fc])
            gate_ref[:, f0:f0 + fc] = gate.astype(BF16)
            up_ref[:, f0:f0 + fc] = up.astype(BF16)
            act = gate * (1.0 / (1.0 + jnp.exp(-gate))) * up
            acc = acc + _dot(act.astype(BF16), wd_ref[f0:f0 + fc, :])
        h_ref[...] = xv + 0.5 * acc

    return pl.pallas_call(
        body, name=name, grid=(T // tm,),
        in_specs=[_row_spec(tm, D), _const_spec((1, D)), _const_spec((D, F)), _const_spec((D, F)), _const_spec((F, D))],
        out_specs=[_row_spec(tm, D), _row_spec(tm, D), _row_spec(tm, F), _row_spec(tm, F)],
        out_shape=[jax.ShapeDtypeStruct((T, D), F32), jax.ShapeDtypeStruct((T, D), BF16),
                   jax.ShapeDtypeStruct((T, F), BF16), jax.ShapeDtypeStruct((T, F), BF16)],
        compiler_params=_cparams(("arbitrary",)),
    )(x, g, wg, wu, wd)


def _ffn_bwd(x, g, gate, up, dh, wg, wu, wd, name):
    T, D = x.shape
    F = wg.shape[1]
    tm = _pick(T, (256, 128))
    fc = _pick(F, (256, 128))

    def body(x_ref, g_ref, gate_ref, up_ref, dh_ref, wg_ref, wu_ref, wd_ref, dx_ref, dg_ref, act_ref, dgate_ref, dup_ref):
        xv = x_ref[...]
        gv = g_ref[...]
        _, r = _rms_fwd(xv, gv)
        dhv = dh_ref[...]
        dho = (0.5 * dhv).astype(BF16)
        dn = jnp.zeros((tm, D), F32)
        for f0 in range(0, F, fc):
            d_act = _dot_nt(dho, wd_ref[f0:f0 + fc, :])
            gt = gate_ref[:, f0:f0 + fc].astype(F32)
            u = up_ref[:, f0:f0 + fc].astype(F32)
            sg = 1.0 / (1.0 + jnp.exp(-gt))
            silu = gt * sg
            act_ref[:, f0:f0 + fc] = (silu * u).astype(BF16)
            d_up = (d_act * silu).astype(BF16)
            d_gate = (d_act * u * (sg + silu * (1.0 - sg))).astype(BF16)
            dup_ref[:, f0:f0 + fc] = d_up
            dgate_ref[:, f0:f0 + fc] = d_gate
            dn = dn + _dot_nt(d_gate, wg_ref[:, f0:f0 + fc]) + _dot_nt(d_up, wu_ref[:, f0:f0 + fc])
        dxn, dgp = _rms_bwd(xv, gv, r, dn)
        dx_ref[...] = dhv + dxn

        @pl.when(pl.program_id(0) == 0)
        def _():
            dg_ref[...] = jnp.zeros_like(dg_ref)

        dg_ref[...] += dgp

    return pl.pallas_call(
        body, name=name, grid=(T // tm,),
        in_specs=[_row_spec(tm, D), _const_spec((1, D)), _row_spec(tm, F), _row_spec(tm, F), _row_spec(tm, D),
                  _const_spec((D, F)), _const_spec((D, F)), _const_spec((F, D))],
        out_specs=[_row_spec(tm, D), _acc_spec((1, D)), _row_spec(tm, F), _row_spec(tm, F), _row_spec(tm, F)],
        out_shape=[jax.ShapeDtypeStruct((T, D), F32), jax.ShapeDtypeStruct((1, D), F32),
                   jax.ShapeDtypeStruct((T, F), BF16), jax.ShapeDtypeStruct((T, F), BF16), jax.ShapeDtypeStruct((T, F), BF16)],
        compiler_params=_cparams(("arbitrary",)),
    )(x, g, gate, up, dh, wg, wu, wd)


def _matmul_tn(a, b, name, scale=None):
    T, K = a.shape
    N = b.shape[1]
    tk = _pick(K, (1024, 1408, 512, 256, 128))
    tn = _pick(N, (1024, 1408, 1280, 768, 512, 256, 128))
    tt = _pick(T, (512, 256, 128))
    n_t = T // tt

    def body(a_ref, b_ref, o_ref):
        @pl.when(pl.program_id(2) == 0)
        def _():
            o_ref[...] = jnp.zeros_like(o_ref)

        o_ref[...] += _dot_tn(a_ref[...].astype(BF16), b_ref[...].astype(BF16))

        if scale is not None:
            @pl.when(pl.program_id(2) == n_t - 1)
            def _():
                o_ref[...] = o_ref[...] * scale

    return pl.pallas_call(
        body, name=name, grid=(K // tk, N // tn, n_t),
        in_specs=[pl.BlockSpec((tt, tk), lambda i, j, t: (t, i)), pl.BlockSpec((tt, tn), lambda i, j, t: (t, j))],
        out_specs=pl.BlockSpec((tk, tn), lambda i, j, t: (i, j)),
        out_shape=jax.ShapeDtypeStruct((K, N), F32),
        compiler_params=_cparams(("arbitrary", "arbitrary", "arbitrary")),
    )(a, b)


def _rope_spec(tm, S):
    per = S // tm
    return pl.BlockSpec((tm, HEAD_BLOCK), lambda i: (i % per, 0))


def _proj_fwd(h1, g_mix, w_in, g_q, wq, g_kv, wkv, rope_c, rope_s, S):
    T, D = h1.shape
    P = w_in.shape[1]
    tm = _pick(S, (512, 256, 128))
    QW = MLA_HEADS * HEAD_BLOCK
    VW = MLA_HEADS * V_HEAD
    SQ = SWA_HEADS * SWA_HEAD_DIM
    SK = SWA_KV_HEADS * SWA_HEAD_DIM
    o_cq, o_ckv, o_qs, o_ks, o_vs, o_kpe = 0, Q_LORA, Q_LORA + KV_LORA, Q_LORA + KV_LORA + SQ, Q_LORA + KV_LORA + SQ + SK, Q_LORA + KV_LORA + SQ + 2 * SK
    assert P == o_kpe + HEAD_BLOCK

    def body(h_ref, gm_ref, win_ref, gq_ref, wq_ref, gkv_ref, wkv_ref, c_ref, s_ref,
             u_ref, cq_ref, ckv_ref, cqn_ref, ckvn_ref, q_ref, k_ref, v_ref, qs_ref, ks_ref, vs_ref):
        u, _ = _rms_fwd(h_ref[...], gm_ref[...])
        ub = u.astype(BF16)
        u_ref[...] = ub
        proj = _dot(ub, win_ref[...])
        c_q = proj[:, o_cq:o_ckv]
        c_kv = proj[:, o_ckv:o_qs]
        cq_ref[...] = c_q
        ckv_ref[...] = c_kv
        qs_ref[...] = proj[:, o_qs:o_ks].astype(BF16)
        ks_ref[...] = proj[:, o_ks:o_vs].astype(BF16)
        vs_ref[...] = proj[:, o_vs:o_kpe].astype(BF16)
        cb, sb = c_ref[...], s_ref[...]
        kpe = proj[:, o_kpe:P]
        kpe = kpe * cb + _swap16(kpe) * sb
        cqn, _ = _rms_fwd(c_q, gq_ref[...])
        cqn = cqn.astype(BF16)
        cqn_ref[...] = cqn
        q = _dot(cqn, wq_ref[...])
        q = q * jnp.tile(cb, (1, MLA_HEADS)) + _swap16(q) * jnp.tile(sb, (1, MLA_HEADS))
        q_ref[...] = q.astype(BF16)
        ckvn, _ = _rms_fwd(c_kv, gkv_ref[...])
        ckvn = ckvn.astype(BF16)
        ckvn_ref[...] = ckvn
        kv = _dot(ckvn, wkv_ref[...])
        k_ref[...] = (kv[:, :QW] + jnp.tile(kpe, (1, MLA_HEADS))).astype(BF16)
        v_ref[...] = kv[:, QW:].astype(BF16)

    widths = [(D, BF16), (Q_LORA, F32), (KV_LORA, F32), (Q_LORA, BF16), (KV_LORA, BF16), (QW, BF16), (QW, BF16), (VW, BF16),
              (SQ, BF16), (SK, BF16), (SK, BF16)]
    return pl.pallas_call(
        body, name="proj_fwd", grid=(T // tm,),
        in_specs=[_row_spec(tm, D), _const_spec((1, D)), _const_spec((D, P)), _const_spec((1, Q_LORA)),
                  _const_spec((Q_LORA, QW)), _const_spec((1, KV_LORA)), _const_spec((KV_LORA, QW + VW)),
                  _rope_spec(tm, S), _rope_spec(tm, S)],
        out_specs=[_row_spec(tm, w) for w, _ in widths],
        out_shape=[jax.ShapeDtypeStruct((T, w), dt) for w, dt in widths],
        compiler_params=_cparams(("arbitrary",)),
    )(h1, g_mix, w_in, g_q, wq, g_kv, wkv, rope_c, rope_s)


def _proj_bwd(dq, dk, dv, dqs, dks, dvs, c_q, c_kv, h1, dh2, g_mix, w_in, g_q, wq, g_kv, wkv, rope_c, rope_s, S):
    T, D = h1.shape
    P = w_in.shape[1]
    tm = _pick(S, (256, 128))
    QW = MLA_HEADS * HEAD_BLOCK
    VW = MLA_HEADS * V_HEAD
    SQ = SWA_HEADS * SWA_HEAD_DIM
    SK = SWA_KV_HEADS * SWA_HEAD_DIM

    def body(dq_ref, dk_ref, dv_ref, dqs_ref, dks_ref, dvs_ref, cq_ref, ckv_ref, h_ref, dh2_ref, gm_ref, win_ref, gq_ref,
             wq_ref, gkv_ref, wkv_ref, c_ref, s_ref, dh1_ref, dproj_ref, dql_ref, dkv_ref, dgm_ref, dgq_ref, dgkv_ref):
        cb, sb = c_ref[...], s_ref[...]
        dqv = dq_ref[...]
        dql = dqv * jnp.tile(cb, (1, MLA_HEADS)) + _swap16(dqv * jnp.tile(sb, (1, MLA_HEADS)))
        dql = dql.astype(BF16)
        dql_ref[...] = dql
        c_q = cq_ref[...]
        _, rq = _rms_fwd(c_q, gq_ref[...])
        d_cq, dgq = _rms_bwd(c_q, gq_ref[...], rq, _dot_nt(dql, wq_ref[...]))

        dkv_all = dk_ref[...]
        dkpe = dkv_all[:, 0:HEAD_BLOCK]
        for h in range(1, MLA_HEADS):
            dkpe = dkpe + dkv_all[:, h * HEAD_BLOCK:(h + 1) * HEAD_BLOCK]
        lane = lax.broadcasted_iota(jnp.int32, dkpe.shape, 1)
        dkpe = jnp.where((lane >= 64) & (lane < 96), dkpe, 0.0)
        dkpe = dkpe * cb + _swap16(dkpe * sb)
        dkvc = jnp.concatenate([dkv_all.astype(BF16), dv_ref[...].astype(BF16)], axis=1)
        dkv_ref[...] = dkvc
        c_kv = ckv_ref[...]
        _, rkv = _rms_fwd(c_kv, gkv_ref[...])
        d_ckv, dgkv = _rms_bwd(c_kv, gkv_ref[...], rkv, _dot_nt(dkvc, wkv_ref[...]))

        dproj = jnp.concatenate([d_cq.astype(BF16), d_ckv.astype(BF16), dqs_ref[...].astype(BF16),
                                 dks_ref[...].astype(BF16), dvs_ref[...].astype(BF16), dkpe.astype(BF16)], axis=1)
        dproj_ref[...] = dproj
        hv = h_ref[...]
        _, rm = _rms_fwd(hv, gm_ref[...])
        dxn, dgm = _rms_bwd(hv, gm_ref[...], rm, _dot_nt(dproj, win_ref[...]))
        dh1_ref[...] = dh2_ref[...] + dxn

        @pl.when(pl.program_id(0) == 0)
        def _():
            dgm_ref[...] = jnp.zeros_like(dgm_ref)
            dgq_ref[...] = jnp.zeros_like(dgq_ref)
            dgkv_ref[...] = jnp.zeros_like(dgkv_ref)

        dgm_ref[...] += dgm
        dgq_ref[...] += dgq
        dgkv_ref[...] += dgkv

    return pl.pallas_call(
        body, name="proj_bwd", grid=(T // tm,),
        in_specs=[_row_spec(tm, QW), _row_spec(tm, QW), _row_spec(tm, VW), _row_spec(tm, SQ), _row_spec(tm, SK), _row_spec(tm, SK),
                  _row_spec(tm, Q_LORA), _row_spec(tm, KV_LORA), _row_spec(tm, D), _row_spec(tm, D),
                  _const_spec((1, D)), _const_spec((D, P)), _const_spec((1, Q_LORA)), _const_spec((Q_LORA, QW)),
                  _const_spec((1, KV_LORA)), _const_spec((KV_LORA, QW + VW)), _rope_spec(tm, S), _rope_spec(tm, S)],
        out_specs=[_row_spec(tm, D), _row_spec(tm, P), _row_spec(tm, QW), _row_spec(tm, QW + VW),
                   _acc_spec((1, D)), _acc_spec((1, Q_LORA)), _acc_spec((1, KV_LORA))],
        out_shape=[jax.ShapeDtypeStruct((T, D), F32), jax.ShapeDtypeStruct((T, P), BF16), jax.ShapeDtypeStruct((T, QW), BF16),
                   jax.ShapeDtypeStruct((T, QW + VW), BF16), jax.ShapeDtypeStruct((1, D), F32),
                   jax.ShapeDtypeStruct((1, Q_LORA), F32), jax.ShapeDtypeStruct((1, KV_LORA), F32)],
        compiler_params=_cparams(("arbitrary",)),
    )(dq, dk, dv, dqs, dks, dvs, c_q, c_kv, h1, dh2, g_mix, w_in, g_q, wq, g_kv, wkv, rope_c, rope_s)


def _out_fwd(o_mla, o_swa, g_mla, g_swa, w_o, h1):
    T, D = h1.shape
    W = o_mla.shape[1]
    tm = _pick(T, (512, 256, 128))

    def body(om_ref, os_ref, gm_ref, gs_ref, wo_ref, h_ref, h2_ref, oc_ref):
        a, _ = _rms_fwd(om_ref[...], gm_ref[...])
        b, _ = _rms_fwd(os_ref[...], gs_ref[...])
        oc = jnp.concatenate([a.astype(BF16), b.astype(BF16)], axis=1)
        oc_ref[...] = oc
        h2_ref[...] = h_ref[...] + _dot(oc, wo_ref[...])

    return pl.pallas_call(
        body, name="out_fwd", grid=(T // tm,),
        in_specs=[_row_spec(tm, W), _row_spec(tm, W), _const_spec((1, W)), _const_spec((1, W)), _const_spec((2 * W, D)),
                  _row_spec(tm, D)],
        out_specs=[_row_spec(tm, D), _row_spec(tm, 2 * W)],
        out_shape=[jax.ShapeDtypeStruct((T, D), F32), jax.ShapeDtypeStruct((T, 2 * W), BF16)],
        compiler_params=_cparams(("arbitrary",)),
    )(o_mla, o_swa, g_mla, g_swa, w_o, h1)


def _out_bwd(dh2, o_mla, o_swa, g_mla, g_swa, w_o):
    T, D = dh2.shape
    W = o_mla.shape[1]
    tm = _pick(T, (512, 256, 128))

    def body(dh_ref, om_ref, os_ref, gm_ref, gs_ref, wo_ref, dom_ref, dos_ref, dgm_ref, dgs_ref):
        doc = _dot_nt(dh_ref[...].astype(BF16), wo_ref[...])
        om, osw = om_ref[...], os_ref[...]
        _, ra = _rms_fwd(om, gm_ref[...])
        _, rb = _rms_fwd(osw, gs_ref[...])
        da, dga = _rms_bwd(om, gm_ref[...], ra, doc[:, :W])
        db, dgb = _rms_bwd(osw, gs_ref[...], rb, doc[:, W:])
        dom_ref[...] = da
        dos_ref[...] = db

        @pl.when(pl.program_id(0) == 0)
        def _():
            dgm_ref[...] = jnp.zeros_like(dgm_ref)
            dgs_ref[...] = jnp.zeros_like(dgs_ref)

        dgm_ref[...] += dga
        dgs_ref[...] += dgb

    return pl.pallas_call(
        body, name="out_bwd", grid=(T // tm,),
        in_specs=[_row_spec(tm, D), _row_spec(tm, W), _row_spec(tm, W), _const_spec((1, W)), _const_spec((1, W)),
                  _const_spec((2 * W, D))],
        out_specs=[_row_spec(tm, W), _row_spec(tm, W), _acc_spec((1, W)), _acc_spec((1, W))],
        out_shape=[jax.ShapeDtypeStruct((T, W), F32), jax.ShapeDtypeStruct((T, W), F32),
                   jax.ShapeDtypeStruct((1, W), F32), jax.ShapeDtypeStruct((1, W), F32)],
        compiler_params=_cparams(("arbitrary",)),
    )(dh2, o_mla, o_swa, g_mla, g_swa, w_o)


def _loss_head(h3, target, g_final):
    T, D = h3.shape
    tm = _pick(T, (512, 256, 128))

    def body(h_ref, t_ref, g_ref, dh_ref, loss_ref, dg_ref):
        hv, gv = h_ref[...], g_ref[...]
        y, r = _rms_fwd(hv, gv)
        diff = y - t_ref[...]
        dxn, dgp = _rms_bwd(hv, gv, r, diff * (1.0 / D))
        dh_ref[...] = dxn
        part = 0.5 * jnp.sum(jnp.sum(diff * diff, axis=-1, keepdims=True) * (1.0 / D), axis=0, keepdims=True)

        @pl.when(pl.program_id(0) == 0)
        def _():
            loss_ref[...] = jnp.zeros_like(loss_ref)
            dg_ref[...] = jnp.zeros_like(dg_ref)

        loss_ref[...] += jnp.broadcast_to(part, loss_ref.shape)
        dg_ref[...] += dgp

    return pl.pallas_call(
        body, name="loss_head", grid=(T // tm,),
        in_specs=[_row_spec(tm, D), _row_spec(tm, D), _const_spec((1, D))],
        out_specs=[_row_spec(tm, D), _acc_spec((1, 128)), _acc_spec((1, D))],
        out_shape=[jax.ShapeDtypeStruct((T, D), F32), jax.ShapeDtypeStruct((1, 128), F32), jax.ShapeDtypeStruct((1, D), F32)],
        compiler_params=_cparams(("arbitrary",)),
    )(h3, target, g_final)


def _half_mask(shape, half):
    lane = lax.broadcasted_iota(jnp.int32, shape, 1)
    return (lane < 64) if half == 0 else (lane >= 64)


def _mla_fwd(q, k, v, B, S):
    T = B * S
    tq = _pick(S, MLA_Q_TILES)
    nq = S // tq
    pairs = MLA_HEADS // 2

    def body(q_ref, k_ref, v_ref, o_ref, lse_ref):
        causal = lax.broadcasted_iota(jnp.int32, (tq, tq), 1) <= lax.broadcasted_iota(jnp.int32, (tq, tq), 0)
        low = _half_mask((tq, 2 * V_HEAD), 0)
        for qi in range(nq):
            rows = slice(qi * tq, (qi + 1) * tq)
            past = slice(0, qi * tq)
            outs, lses = [], []
            for half in range(2):
                lanes = slice(half * HEAD_BLOCK, (half + 1) * HEAD_BLOCK)
                qv = q_ref[rows, lanes]
                s_d = jnp.where(causal, _dot_nt(qv, k_ref[rows, lanes]) * MLA_SCALE, NEG)
                m = jnp.max(s_d, axis=-1, keepdims=True)
                if qi:
                    s_p = _dot_nt(qv, k_ref[past, lanes]) * MLA_SCALE
                    m = jnp.maximum(m, jnp.max(s_p, axis=-1, keepdims=True))
                p_d = jnp.exp(s_d - m)
                l = jnp.sum(p_d, axis=-1, keepdims=True)
                acc = _dot(p_d.astype(BF16), v_ref[rows, :])
                if qi:
                    p_p = jnp.exp(s_p - m)
                    l = l + jnp.sum(p_p, axis=-1, keepdims=True)
                    acc = acc + _dot(p_p.astype(BF16), v_ref[past, :])
                outs.append(acc * (1.0 / l))
                lses.append(jnp.broadcast_to(m + jnp.log(l), (tq, 2 * V_HEAD)))
            o_ref[rows, :] = jnp.where(low, outs[0], outs[1])
            lse_ref[rows, :] = jnp.where(low, lses[0], lses[1])

    blk2 = pl.BlockSpec((S, 2 * HEAD_BLOCK), lambda b, j: (b, j))
    blk1 = pl.BlockSpec((S, 2 * V_HEAD), lambda b, j: (b, j))
    return pl.pallas_call(
        body, name="mla_fwd", grid=(B, pairs),
        in_specs=[blk2, blk2, blk1], out_specs=[blk1, blk1],
        out_shape=[jax.ShapeDtypeStruct((T, pairs * 2 * V_HEAD), F32)] * 2,
        compiler_params=_cparams(("arbitrary", "arbitrary")),
    )(q, k, v)


def _mla_bwd(q, k, v, do, o, lse, B, S):
    T = B * S
    tq = _pick(S, MLA_Q_TILES)
    nq = S // tq
    pairs = MLA_HEADS // 2

    def body(q_ref, k_ref, v_ref, do_ref, o_ref, lse_ref, dq_ref, dk_ref, dv_ref, dk_acc):
        causal = lax.broadcasted_iota(jnp.int32, (tq, tq), 1) <= lax.broadcasted_iota(jnp.int32, (tq, tq), 0)
        for half in range(2):
            lanes = slice(half * HEAD_BLOCK, (half + 1) * HEAD_BLOCK)
            col = half * V_HEAD
            own = _half_mask((tq, 2 * V_HEAD), half)
            for qi in reversed(range(nq)):
                rows = slice(qi * tq, (qi + 1) * tq)
                past = slice(0, qi * tq)
                first = qi == nq - 1
                qv = q_ref[rows, lanes]
                dov = jnp.where(own, do_ref[rows, :], 0.0)
                dsum = jnp.sum(dov * o_ref[rows, :], axis=-1, keepdims=True)
                dob = dov.astype(BF16)
                lse_col = lse_ref[rows, col:col + 1]

                def grads(keys, mask):
                    kv, vv = k_ref[keys, lanes], v_ref[keys, :]
                    p = jnp.exp(_dot_nt(qv, kv) * MLA_SCALE - lse_col)
                    if mask is not None:
                        p = jnp.where(mask, p, 0.0)
                    ds = (p * (_dot_nt(dob, vv) - dsum) * MLA_SCALE).astype(BF16)
                    dk, dv = _dot_tn(ds, qv), _dot_tn(p.astype(BF16), dob)
                    if first:
                        dk_acc[keys, :] = dk
                    else:
                        dk_acc[keys, :] += dk
                    if first and half == 0:
                        dv_ref[keys, :] = dv
                    else:
                        dv_ref[keys, :] += dv
                    return _dot(ds, kv)

                dq = grads(rows, causal)
                if qi:
                    dq = dq + grads(past, None)
                dq_ref[rows, lanes] = dq
            dk_ref[:, lanes] = dk_acc[...]

    blk2 = pl.BlockSpec((S, 2 * HEAD_BLOCK), lambda b, j: (b, j))
    blk1 = pl.BlockSpec((S, 2 * V_HEAD), lambda b, j: (b, j))
    return pl.pallas_call(
        body, name="mla_bwd", grid=(B, pairs),
        in_specs=[blk2, blk2, blk1, blk1, blk1, blk1], out_specs=[blk2, blk2, blk1],
        out_shape=[jax.ShapeDtypeStruct((T, MLA_HEADS * HEAD_BLOCK), F32), jax.ShapeDtypeStruct((T, MLA_HEADS * HEAD_BLOCK), F32),
                   jax.ShapeDtypeStruct((T, pairs * 2 * V_HEAD), F32)],
        scratch_shapes=[pltpu.VMEM((S, HEAD_BLOCK), F32)],
        compiler_params=_cparams(("arbitrary", "arbitrary")),
    )(q, k, v, do, o, lse)


SWA_GROUP = SWA_HEADS // SWA_KV_HEADS
SWA_ROWS = SWA_GROUP * WINDOW


SWA_UNROLL = 4


def _loop_unrolled(n, unroll, body):
    u = next(c for c in (unroll, 2, 1) if n % c == 0)

    def trip(t, carry):
        for i in range(u):
            body(t * u + i, 0)
        return carry

    lax.fori_loop(0, n // u, trip, 0)


def _swa_stack(ref, r0):
    return jnp.concatenate([ref[pl.ds(r0, WINDOW), j * WINDOW:(j + 1) * WINDOW] for j in range(SWA_GROUP)], axis=0)


def _swa_unstack(low, high):
    sel = _half_mask((WINDOW, WINDOW), 0)
    return jnp.concatenate([jnp.where(sel, low[j * WINDOW:(j + 1) * WINDOW], high[j * WINDOW:(j + 1) * WINDOW])
                            for j in range(SWA_GROUP)], axis=1)


def _swa_sink_col(sink_ref, g):
    return jnp.concatenate([jnp.broadcast_to(sink_ref[g * SWA_GROUP + j:g * SWA_GROUP + j + 1, 0:1], (WINDOW, 1))
                            for j in range(SWA_GROUP)], axis=0)


def _swa_probs(qm, kcat, bias_g, sink, n):
    qi = lax.broadcasted_iota(jnp.int32, (SWA_ROWS, 2 * WINDOW), 0) % WINDOW
    col = lax.broadcasted_iota(jnp.int32, (SWA_ROWS, 2 * WINDOW), 1)
    valid = (col > qi) & (col <= qi + WINDOW) & ((col >= WINDOW) | (n > 0))
    s = jnp.where(valid, _dot_nt(qm, kcat) * SWA_SCALE + bias_g, NEG)
    m = jnp.maximum(jnp.max(s, axis=-1, keepdims=True), sink)
    e = jnp.exp(s - m)
    e_s = jnp.exp(sink - m)
    inv = 1.0 / (jnp.sum(e, axis=-1, keepdims=True) + e_s)
    return e * inv, e_s * inv


def _swa_fwd(q, k, v, bias, sinks, B, S):
    T = B * S
    W = WINDOW
    nb = S // W

    def body(q_ref, k_ref, v_ref, bias_ref, sink_ref, o_ref):
        sink_cols = [_swa_sink_col(sink_ref, g) for g in range(SWA_KV_HEADS)]

        def block(n, _):
            r0 = pl.multiple_of(n * W, W)
            p0 = pl.multiple_of(jnp.maximum(n - 1, 0) * W, W)
            kcat = jnp.concatenate([k_ref[pl.ds(p0, W), :], k_ref[pl.ds(r0, W), :]], axis=0)
            vcat = jnp.concatenate([v_ref[pl.ds(p0, W), :], v_ref[pl.ds(r0, W), :]], axis=0)
            qcat = _swa_stack(q_ref, r0)
            outs = []
            for g in range(SWA_KV_HEADS):
                qm = jnp.where(_half_mask(qcat.shape, g), qcat, jnp.zeros_like(qcat))
                p, _ = _swa_probs(qm, kcat, bias_ref[g], sink_cols[g], n)
                outs.append(_dot(p.astype(BF16), vcat))
            o_ref[pl.ds(r0, W), :] = _swa_unstack(outs[0], outs[1])
            return 0

        _loop_unrolled(nb, SWA_UNROLL, block)

    SQ, SK = SWA_HEADS * SWA_HEAD_DIM, SWA_KV_HEADS * SWA_HEAD_DIM
    return pl.pallas_call(
        body, name="swa_fwd", grid=(B,),
        in_specs=[pl.BlockSpec((S, SQ), lambda b: (b, 0)), pl.BlockSpec((S, SK), lambda b: (b, 0)),
                  pl.BlockSpec((S, SK), lambda b: (b, 0)), _const_spec((SWA_KV_HEADS, SWA_ROWS, 2 * W)),
                  _const_spec((SWA_HEADS, 128))],
        out_specs=pl.BlockSpec((S, SQ), lambda b: (b, 0)),
        out_shape=jax.ShapeDtypeStruct((T, SQ), F32),
        compiler_params=_cparams(("arbitrary",)),
    )(q, k, v, bias, sinks)


def _swa_bwd(q, k, v, do, bias, sinks, B, S):
    T = B * S
    W = WINDOW
    nb = S // W

    def body(q_ref, k_ref, v_ref, do_ref, bias_ref, sink_ref, dq_ref, dk_ref, dv_ref, dbias_ref, dsink_ref):
        @pl.when(pl.program_id(0) == 0)
        def _():
            dbias_ref[...] = jnp.zeros_like(dbias_ref)
            dsink_ref[...] = jnp.zeros_like(dsink_ref)

        dk_ref[...] = jnp.zeros_like(dk_ref)
        dv_ref[...] = jnp.zeros_like(dv_ref)
        sink_cols = [_swa_sink_col(sink_ref, g) for g in range(SWA_KV_HEADS)]

        def block(n, _):
            r0 = pl.multiple_of(n * W, W)
            p0 = pl.multiple_of(jnp.maximum(n - 1, 0) * W, W)
            kcat = jnp.concatenate([k_ref[pl.ds(p0, W), :], k_ref[pl.ds(r0, W), :]], axis=0)
            vcat = jnp.concatenate([v_ref[pl.ds(p0, W), :], v_ref[pl.ds(r0, W), :]], axis=0)
            qcat = _swa_stack(q_ref, r0)
            docat = _swa_stack(do_ref, r0)
            dk = jnp.zeros((2 * W, W), F32)
            dv = jnp.zeros((2 * W, W), F32)
            dqs = []
            for g in range(SWA_KV_HEADS):
                own = _half_mask(qcat.shape, g)
                qm = jnp.where(own, qcat, jnp.zeros_like(qcat))
                dom = jnp.where(own, docat, 0.0).astype(BF16)
                p, p_s = _swa_probs(qm, kcat, bias_ref[g], sink_cols[g], n)
                dp = _dot_nt(dom, vcat)
                dsum = jnp.sum(p * dp, axis=-1, keepdims=True)
                ds = p * (dp - dsum)
                dbias_ref[g] += ds
                dsink_ref[g] += jnp.broadcast_to(-p_s * dsum, (SWA_ROWS, 128))
                dsb = (ds * SWA_SCALE).astype(BF16)
                dqs.append(_dot(dsb, kcat))
                dk = dk + _dot_tn(dsb, qm)
                dv = dv + _dot_tn(p.astype(BF16), dom)
            dq_ref[pl.ds(r0, W), :] = _swa_unstack(dqs[0], dqs[1])
            dk_ref[pl.ds(p0, W), :] += dk[:W]
            dv_ref[pl.ds(p0, W), :] += dv[:W]
            dk_ref[pl.ds(r0, W), :] += dk[W:]
            dv_ref[pl.ds(r0, W), :] += dv[W:]
            return 0

        _loop_unrolled(nb, SWA_UNROLL, block)

    SQ, SK = SWA_HEADS * SWA_HEAD_DIM, SWA_KV_HEADS * SWA_HEAD_DIM
    row = lambda w: pl.BlockSpec((S, w), lambda b: (b, 0))
    return pl.pallas_call(
        body, name="swa_bwd", grid=(B,),
        in_specs=[row(SQ), row(SK), row(SK), row(SQ), _const_spec((SWA_KV_HEADS, SWA_ROWS, 2 * W)), _const_spec((SWA_HEADS, 128))],
        out_specs=[row(SQ), row(SK), row(SK), _acc_spec((SWA_KV_HEADS, SWA_ROWS, 2 * W)), _acc_spec((SWA_KV_HEADS, SWA_ROWS, 128))],
        out_shape=[jax.ShapeDtypeStruct((T, SQ), F32), jax.ShapeDtypeStruct((T, SK), F32), jax.ShapeDtypeStruct((T, SK), F32),
                   jax.ShapeDtypeStruct((SWA_KV_HEADS, SWA_ROWS, 2 * W), F32), jax.ShapeDtypeStruct((SWA_KV_HEADS, SWA_ROWS, 128), F32)],
        compiler_params=_cparams(("arbitrary",)),
    )(q, k, v, do, bias, sinks)


def _bias_expand(rel_bias, bucket):
    W = WINDOW

    def body(rb_ref, bucket_ref, out_ref):
        bk = bucket_ref[...]
        for h in range(SWA_HEADS):
            def add(b, acc):
                return jnp.where(bk == b, rb_ref[b, h], acc)

            out_ref[h] = lax.fori_loop(0, REL_BUCKETS, add, jnp.zeros((W, 2 * W), F32))

    return pl.pallas_call(
        body, name="bias_expand",
        in_specs=[pl.BlockSpec(memory_space=pltpu.SMEM), pl.BlockSpec(memory_space=pltpu.VMEM)],
        out_specs=pl.BlockSpec(memory_space=pltpu.VMEM),
        out_shape=jax.ShapeDtypeStruct((SWA_HEADS, W, 2 * W), F32),
        compiler_params=_cparams(),
    )(rel_bias, bucket)


def _bias_reduce(dbias, dsink_rows, bucket):
    def body(db_ref, ds_ref, bucket_ref, out_ref, sink_out_ref):
        for h in range(SWA_HEADS):
            sink_out_ref[h:h + 1, :] = jnp.sum(ds_ref[h], axis=0, keepdims=True)
        bk = bucket_ref[...]
        row = lax.broadcasted_iota(jnp.int32, (REL_BUCKETS, 128), 0)
        lane = lax.broadcasted_iota(jnp.int32, (REL_BUCKETS, 128), 1)
        acc = jnp.zeros((REL_BUCKETS, 128), F32)
        for h in range(SWA_HEADS):
            dbh = db_ref[h]

            def add(b, acc):
                part = jnp.sum(jnp.sum(jnp.where(bk == b, dbh, 0.0), axis=-1, keepdims=True), axis=0, keepdims=True)
                return jnp.where((row == b) & (lane == h), part, acc)

            acc = lax.fori_loop(0, REL_BUCKETS, add, acc)
        out_ref[...] = acc

    return pl.pallas_call(
        body, name="bias_reduce",
        out_shape=[jax.ShapeDtypeStruct((REL_BUCKETS, 128), F32), jax.ShapeDtypeStruct((SWA_HEADS, 128), F32)],
        compiler_params=_cparams(),
    )(dbias, dsink_rows, bucket)


def _t5_bucket(dist):
    n = jnp.maximum(dist, 0)
    max_exact = REL_BUCKETS // 2
    nf = jnp.maximum(n, 1).astype(F32)
    large = max_exact + (jnp.log(nf / max_exact) / math.log(REL_MAX_DIST / max_exact) * (REL_BUCKETS - max_exact)).astype(jnp.int32)
    large = jnp.minimum(large, REL_BUCKETS - 1)
    return jnp.where(n < max_exact, n, large)


def _pair_heads(a, axis, inverse=False):
    shp = a.shape
    a = a.reshape(shp[:axis] + ((4, 2, 64) if inverse else (2, 4, 64)) + shp[axis + 1:])
    a = jnp.swapaxes(a, axis, axis + 1)
    return a.reshape(shp)


def _col_full(g):
    return jnp.transpose(g, (1, 0, 2)).reshape(g.shape[1], N_DEV * g.shape[2])


def _row_full(g):
    return g.reshape(N_DEV * g.shape[1], g.shape[2])


def _col_blocks(w):
    r = w.shape[0]
    return jnp.transpose(w.reshape(r, N_CHIP, 2, -1), (2, 1, 0, 3))


def _row_blocks(w):
    c = w.shape[1]
    return jnp.transpose(w.reshape(N_CHIP, 2, -1, c), (1, 0, 2, 3))


_IN_WIDTHS = (Q_LORA, KV_LORA, QK_ROPE, SWA_HEADS * SWA_HEAD_DIM, SWA_KV_HEADS * SWA_HEAD_DIM, SWA_KV_HEADS * SWA_HEAD_DIM)


def _split_in(w):
    parts, o = [], 0
    for wd in _IN_WIDTHS:
        parts.append(w[:, o:o + wd])
        o += wd
    return parts


def _win_to_kernel(w):
    cq, ckv, kpe, qs, ks, vs = _split_in(w)
    kpe = jnp.pad(kpe, ((0, 0), (64, 32)))
    return jnp.concatenate([cq, ckv, _pair_heads(qs, 1), ks, vs, kpe], axis=1)


def _win_from_kernel(dw):
    o = [0, Q_LORA, Q_LORA + KV_LORA, Q_LORA + KV_LORA + 512, Q_LORA + KV_LORA + 640, Q_LORA + KV_LORA + 768]
    cq, ckv, qs, ks, vs, kpe = [dw[:, a:b] for a, b in zip(o, o[1:] + [dw.shape[1]])]
    return jnp.concatenate([cq, ckv, kpe[:, 64:96], _pair_heads(qs, 1, inverse=True), ks, vs], axis=1)


def _wq_to_kernel(w):
    return jnp.pad(w.reshape(Q_LORA, MLA_HEADS, QK_NOPE + QK_ROPE), ((0, 0), (0, 0), (0, 32))).reshape(Q_LORA, MLA_HEADS * HEAD_BLOCK)


def _wq_from_kernel(dw):
    return dw.reshape(Q_LORA, MLA_HEADS, HEAD_BLOCK)[:, :, :QK_NOPE + QK_ROPE].reshape(Q_LORA, -1)


def _wkv_to_kernel(w):
    w3 = w.reshape(KV_LORA, MLA_HEADS, QK_NOPE + V_HEAD)
    kpart = jnp.pad(w3[:, :, :QK_NOPE], ((0, 0), (0, 0), (0, 64))).reshape(KV_LORA, MLA_HEADS * HEAD_BLOCK)
    return jnp.concatenate([kpart, w3[:, :, QK_NOPE:].reshape(KV_LORA, MLA_HEADS * V_HEAD)], axis=1)


def _wkv_from_kernel(dw):
    dk = dw[:, :MLA_HEADS * HEAD_BLOCK].reshape(KV_LORA, MLA_HEADS, HEAD_BLOCK)[:, :, :QK_NOPE]
    dv = dw[:, MLA_HEADS * HEAD_BLOCK:].reshape(KV_LORA, MLA_HEADS, V_HEAD)
    return jnp.concatenate([dk, dv], axis=2).reshape(KV_LORA, -1)


def _wo_to_kernel(w):
    return jnp.concatenate([w[:512], _pair_heads(w[512:], 0)], axis=0)


def _wo_from_kernel(dw):
    return jnp.concatenate([dw[:512], _pair_heads(dw[512:], 0, inverse=True)], axis=0)


_SMALL = (("g_ffn1", 1024), ("g_mix", 1024), ("g_q_a", 256), ("g_kv_a", 128), ("attn_sinks", 8), ("rel_bias", 256),
          ("g_out_mla", 512), ("g_out_swa", 512), ("g_ffn2", 1024), ("g_final", 1024), ("loss", 1))


def _pack_small(vals):
    rows = []
    for name, size in _SMALL:
        v = vals[name].reshape(-1).astype(F32)
        nrow = -(-size // 128)
        rows.append(jnp.pad(v, (0, nrow * 128 - size)).reshape(nrow, 128))
    packed = jnp.concatenate(rows, axis=0)
    return jnp.pad(packed, ((0, SMALL_ROWS - packed.shape[0]), (0, 0)))


def _unpack_small(packed, shapes):
    out, r = {}, 0
    for name, size in _SMALL:
        nrow = -(-size // 128)
        out[name] = packed[r:r + nrow].reshape(-1)[:size].reshape(shapes[name])
        r += nrow
    return out


def _local_step(x, target, gains, wts):
    B, S, D = x.shape
    T = B * S
    x2 = x.reshape(T, D)
    t2 = target.reshape(T, D)

    pos = jnp.arange(S, dtype=F32)
    inv_freq = ROPE_THETA ** (-jnp.arange(0, QK_ROPE, 2, dtype=F32) / QK_ROPE)
    ang = pos[:, None] * inv_freq[None, :]
    cos, sin = jnp.cos(ang), jnp.sin(ang)
    ones, zeros = jnp.ones((S, 64), F32), jnp.zeros((S, 32), F32)
    rope_c = jnp.concatenate([ones, cos, cos, zeros], axis=1)
    rope_s = jnp.concatenate([0.0 * ones, -sin, sin, zeros], axis=1)

    qi = jnp.arange(WINDOW)[:, None]
    kj = jnp.arange(2 * WINDOW)[None, :]
    bucket = _t5_bucket(qi + WINDOW - kj).astype(jnp.int32)
    bias = _bias_expand(gains["rel_bias"], bucket).reshape(SWA_KV_HEADS, SWA_ROWS, 2 * WINDOW)
    sinks = jnp.broadcast_to(gains["attn_sinks"].reshape(SWA_HEADS, 1), (SWA_HEADS, 128))
    g_swa = _pair_heads(gains["g_out_swa"], 1)

    h1, n1, gate1, up1 = _ffn_fwd(x2, gains["g_ffn1"], wts["g1"], wts["u1"], wts["d1"], "ffn1_fwd")
    (u, c_q, c_kv, cqn, ckvn, q, k, v, qs, ks, vs) = _proj_fwd(
        h1, gains["g_mix"], wts["in"], gains["g_q_a"], wts["q"], gains["g_kv_a"], wts["kv"], rope_c, rope_s, S)
    o_mla, lse = _mla_fwd(q, k, v, B, S)
    o_swa = _swa_fwd(qs, ks, vs, bias, sinks, B, S)
    h2, oc = _out_fwd(o_mla, o_swa, gains["g_out_mla"], g_swa, wts["o"], h1)
    h3, n2, gate2, up2 = _ffn_fwd(h2, gains["g_ffn2"], wts["g2"], wts["u2"], wts["d2"], "ffn2_fwd")
    dh3, loss, dg_final = _loss_head(h3, t2, gains["g_final"].reshape(1, D))

    dh2, dg_ffn2, act2, dgate2, dup2 = _ffn_bwd(h2, gains["g_ffn2"], gate2, up2, dh3, wts["g2"], wts["u2"], wts["d2"], "ffn2_bwd")
    grads = {"g2": _matmul_tn(n2, dgate2, "dw_gate2"), "u2": _matmul_tn(n2, dup2, "dw_up2"), "d2": _matmul_tn(act2, dh3, "dw_down2", 0.5)}
    do_mla, do_swa, dg_mla, dg_swa = _out_bwd(dh2, o_mla, o_swa, gains["g_out_mla"], g_swa, wts["o"])
    grads["o"] = _matmul_tn(oc, dh2, "dw_o")
    dq, dk, dv = _mla_bwd(q, k, v, do_mla, o_mla, lse, B, S)
    dqs, dks, dvs, dbias, dsink = _swa_bwd(qs, ks, vs, do_swa, bias, sinks, B, S)
    dh1, dproj, dql, dkvc, dg_mix, dg_q, dg_kv = _proj_bwd(
        dq, dk, dv, dqs, dks, dvs, c_q, c_kv, h1, dh2, gains["g_mix"], wts["in"], gains["g_q_a"], wts["q"],
        gains["g_kv_a"], wts["kv"], rope_c, rope_s, S)
    grads["in"] = _matmul_tn(u, dproj, "dw_in")
    grads["q"] = _matmul_tn(cqn, dql, "dw_q")
    grads["kv"] = _matmul_tn(ckvn, dkvc, "dw_kv")
    dx, dg_ffn1, act1, dgate1, dup1 = _ffn_bwd(x2, gains["g_ffn1"], gate1, up1, dh1, wts["g1"], wts["u1"], wts["d1"], "ffn1_bwd")
    grads["g1"] = _matmul_tn(n1, dgate1, "dw_gate1")
    grads["u1"] = _matmul_tn(n1, dup1, "dw_up1")
    grads["d1"] = _matmul_tn(act1, dh1, "dw_down1", 0.5)

    d_rel_bias, d_sinks = _bias_reduce(dbias.reshape(SWA_HEADS, WINDOW, 2 * WINDOW), dsink.reshape(SWA_HEADS, WINDOW, 128), bucket)
    small = {
        "g_ffn1": dg_ffn1, "g_mix": dg_mix, "g_q_a": dg_q, "g_kv_a": dg_kv, "attn_sinks": d_sinks[:, 0],
        "rel_bias": d_rel_bias[:, :SWA_HEADS], "g_out_mla": dg_mla, "g_out_swa": _pair_heads(dg_swa, 1, inverse=True),
        "g_ffn2": dg_ffn2, "g_final": dg_final, "loss": loss[0, 0:1],
    }
    return dx.reshape(B, S, D), grads, small


_WEIGHTS = ("g_ffn1", "w_ffn1_gate", "w_ffn1_up", "w_ffn1_down", "g_mix", "w_in", "g_q_a", "w_q_b", "g_kv_a", "w_kv_b",
            "attn_sinks", "rel_bias", "g_out_mla", "g_out_swa", "w_o", "g_ffn2", "w_ffn2_gate", "w_ffn2_up", "w_ffn2_down",
            "g_final")
_BIG = (("w_ffn1_gate", "g1", True), ("w_ffn1_up", "u1", True), ("w_ffn1_down", "d1", False), ("w_in", "in", True),
        ("w_q_b", "q", True), ("w_kv_b", "kv", True), ("w_o", "o", False), ("w_ffn2_gate", "g2", True),
        ("w_ffn2_up", "u2", True), ("w_ffn2_down", "d2", False))
_TO_KERNEL = {"in": _win_to_kernel, "q": _wq_to_kernel, "kv": _wkv_to_kernel, "o": _wo_to_kernel}
_FROM_KERNEL = {"in": _win_from_kernel, "q": _wq_from_kernel, "kv": _wkv_from_kernel, "o": _wo_from_kernel}


def kernel(x, g_ffn1, w_ffn1_gate, w_ffn1_up, w_ffn1_down, g_mix, w_in, g_q_a, w_q_b, g_kv_a, w_kv_b, attn_sinks, rel_bias, g_out_mla, g_out_swa, w_o, g_ffn2, w_ffn2_gate, w_ffn2_up, w_ffn2_down, g_final, loss_target, m_g_ffn1, m_w_ffn1_gate, m_w_ffn1_up, m_w_ffn1_down, m_g_mix, m_w_in, m_g_q_a, m_w_q_b, m_g_kv_a, m_w_kv_b, m_attn_sinks, m_rel_bias, m_g_out_mla, m_g_out_swa, m_w_o, m_g_ffn2, m_w_ffn2_gate, m_w_ffn2_up, m_w_ffn2_down, m_g_final, v_g_ffn1, v_w_ffn1_gate, v_w_ffn1_up, v_w_ffn1_down, v_g_mix, v_w_in, v_g_q_a, v_w_q_b, v_g_kv_a, v_w_kv_b, v_attn_sinks, v_rel_bias, v_g_out_mla, v_g_out_swa, v_w_o, v_g_ffn2, v_w_ffn2_gate, v_w_ffn2_up, v_w_ffn2_down, v_g_final):
    w = dict(zip(_WEIGHTS, (g_ffn1, w_ffn1_gate, w_ffn1_up, w_ffn1_down, g_mix, w_in, g_q_a, w_q_b, g_kv_a, w_kv_b, attn_sinks,
                            rel_bias, g_out_mla, g_out_swa, w_o, g_ffn2, w_ffn2_gate, w_ffn2_up, w_ffn2_down, g_final)))
    m = dict(zip(_WEIGHTS, (m_g_ffn1, m_w_ffn1_gate, m_w_ffn1_up, m_w_ffn1_down, m_g_mix, m_w_in, m_g_q_a, m_w_q_b, m_g_kv_a,
                            m_w_kv_b, m_attn_sinks, m_rel_bias, m_g_out_mla, m_g_out_swa, m_w_o, m_g_ffn2, m_w_ffn2_gate,
                            m_w_ffn2_up, m_w_ffn2_down, m_g_final)))
    v = dict(zip(_WEIGHTS, (v_g_ffn1, v_w_ffn1_gate, v_w_ffn1_up, v_w_ffn1_down, v_g_mix, v_w_in, v_g_q_a, v_w_q_b, v_g_kv_a,
                            v_w_kv_b, v_attn_sinks, v_rel_bias, v_g_out_mla, v_g_out_swa, v_w_o, v_g_ffn2, v_w_ffn2_gate,
                            v_w_ffn2_up, v_w_ffn2_down, v_g_final)))
    core = lax.axis_index("c").astype(jnp.int32).reshape(1)
    chip = (2 * lax.axis_index("x") + lax.axis_index("y")).astype(jnp.int32).reshape(1)

    shards = _cast_bf16([w[name][0] for name, _, _ in _BIG])
    gathered = _all_gather(shards, "weights_all_gather")
    wts = {}
    for (name, key, by_col), g in zip(_BIG, gathered):
        full = _col_full(g) if by_col else _row_full(g)
        wts[key] = _TO_KERNEL[key](full) if key in _TO_KERNEL else full

    small_names = [n for n, _ in _SMALL if n != "loss"]
    gains = {n: (w[n] if n in ("rel_bias", "g_final") else w[n].reshape(1, -1)) for n in small_names}
    grad_x, grads, small = _local_step(x, loss_target, gains, wts)

    blocks = []
    for name, key, by_col in _BIG:
        g = grads[key]
        if key in _FROM_KERNEL:
            g = _FROM_KERNEL[key](g)
        blocks.append(_col_blocks(g) if by_col else _row_blocks(g))
    from_sibling = _exchange_cores(blocks)
    sums = [_add_core_halves(b, r, core) for b, r in zip(blocks, from_sibling)]
    from_chips = _exchange_chips([s16 for _, s16 in sums])

    small_gathered = _all_gather([_pack_small(small)], "small_all_gather")[0]

    out_g, out_d, out_m, out_v = {}, {}, {}, {}
    for (name, key, by_col), (s32, _), r16 in zip(_BIG, sums, from_chips):
        g, d, mn, vn = _adam_big(w[name][0], m[name][0], v[name][0], s32, r16, chip)
        out_g[name], out_d[name], out_m[name], out_v[name] = g[None], d[None], mn[None], vn[None]
    shapes = {n: w[n].shape for n in small_names}
    shapes["loss"] = ()
    zero = jnp.zeros((), F32)
    ws, ms, vs = ({**{n: src[n] for n in small_names}, "loss": zero} for src in (w, m, v))
    sg, sd, sm, sv = _adam_small(_pack_small(ws), _pack_small(ms), _pack_small(vs), small_gathered)
    sg, sd, sm, sv = (_unpack_small(p, shapes) for p in (sg, sd, sm, sv))
    for n in small_names:
        out_g[n], out_d[n], out_m[n], out_v[n] = sg[n], sd[n], sm[n], sv[n]

    return (sg["loss"], grad_x, *[out_g[n] for n in _WEIGHTS], *[out_d[n] for n in _WEIGHTS],
            *[out_m[n] for n in _WEIGHTS], *[out_v[n] for n in _WEIGHTS])
```

```python
import functools
import math

import jax
import jax.numpy as jnp
from jax import lax
from jax.experimental import pallas as pl
from jax.experimental.pallas import tpu as pltpu

F32 = jnp.float32
BF16 = jnp.bfloat16
MESH = pl.DeviceIdType.MESH

EPS = 1e-6
NEG = -1e30
N_DEV = 8
N_CHIP = 4

MLA_HEADS = 8
Q_LORA = 256
KV_LORA = 128
QK_NOPE = 64
QK_ROPE = 32
V_HEAD = 64
ROPE_THETA = 10000.0
SWA_HEADS = 8
SWA_KV_HEADS = 2
SWA_HEAD_DIM = 64
WINDOW = 128
REL_BUCKETS = 32
REL_MAX_DIST = 128
HEAD_BLOCK = 128
MLA_Q_TILES = (512, 256, 128)
MLA_SCALE = (QK_NOPE + QK_ROPE) ** -0.5
SWA_SCALE = SWA_HEAD_DIM ** -0.5

ADAM_LR = 0.001
ADAM_B1 = 0.9
ADAM_B2 = 0.999
ADAM_EPS = 1e-08
ADAM_WD = 0.01
ADAM_STEP = 10

VMEM_LIMIT = 56 * 1024 * 1024
SMALL_ROWS = 48


def _cparams(semantics=None):
    return pltpu.CompilerParams(dimension_semantics=semantics, vmem_limit_bytes=VMEM_LIMIT)


def _dot(a, b):
    return jnp.dot(a, b, preferred_element_type=F32)


def _dot_nt(a, b):
    return lax.dot_general(a, b, (((1,), (1,)), ((), ())), preferred_element_type=F32)


def _dot_tn(a, b):
    return lax.dot_general(a, b, (((0,), (0,)), ((), ())), preferred_element_type=F32)


def _rms_fwd(x, g):
    r = lax.rsqrt(jnp.mean(x * x, axis=-1, keepdims=True) + EPS)
    return x * r * g, r


def _rms_bwd(x, g, r, dy):
    xh = x * r
    dyg = dy * g
    dx = r * (dyg - xh * jnp.mean(dyg * xh, axis=-1, keepdims=True))
    return dx, jnp.sum(dy * xh, axis=0, keepdims=True)


def _swap16(x):
    n = x.shape[-1]
    up = pltpu.roll(x, n - 16, 1)
    down = pltpu.roll(x, 16, 1)
    lane = lax.broadcasted_iota(jnp.int32, x.shape, 1) % HEAD_BLOCK
    return jnp.where((lane >= 64) & (lane < 80), up, jnp.where((lane >= 80) & (lane < 96), down, 0.0))


def _const_spec(shape):
    return pl.BlockSpec(shape, lambda *_: (0,) * len(shape), pipeline_mode=pl.Buffered(1))


def _acc_spec(shape):
    return pl.BlockSpec(shape, lambda *_: (0,) * len(shape))


def _row_spec(tm, width):
    return pl.BlockSpec((tm, width), lambda i: (i, 0))


def _cast_bf16(arrays):
    n = len(arrays)

    def body(*refs):
        for src, dst in zip(refs[:n], refs[n:]):
            dst[...] = src[...].astype(BF16)

    return pl.pallas_call(
        body, name="cast_weights",
        out_shape=[jax.ShapeDtypeStruct(a.shape, BF16) for a in arrays],
        compiler_params=_cparams(),
    )(*arrays)


def _all_gather(arrays, name):
    n = len(arrays)

    def body(*refs):
        ins, outs = refs[:n], refs[n:2 * n]
        send_sems, recv_sems, local_sems = refs[2 * n:]
        x, y, c = lax.axis_index("x"), lax.axis_index("y"), lax.axis_index("c")
        me, sibling = (x, y, c), (x, y, 1 - c)
        chips = [(1 - x, y), (x, 1 - y), (1 - x, 1 - y)]

        def slot(a, dev):
            return outs[a].at[4 * dev[0] + 2 * dev[1] + dev[2]]

        def copy(a, k, block, to, src=None):
            return pltpu.make_async_remote_copy(
                src_ref=slot(a, block) if src is None else src, dst_ref=slot(a, block),
                send_sem=send_sems.at[a, k], recv_sem=recv_sems.at[a, k], device_id=to, device_id_type=MESH)

        mine = [pltpu.make_async_copy(ins[a], slot(a, me), local_sems.at[a]) for a in range(n)]
        for cp in mine:
            cp.start()
        first = []
        for a in range(n):
            first.append(copy(a, 0, me, sibling, src=ins[a]))
            first += [copy(a, 1 + j, me, (*chip, c), src=ins[a]) for j, chip in enumerate(chips)]
        for cp in first:
            cp.start()
        passed = []
        for j, chip in enumerate(chips):
            for a in range(n):
                copy(a, 1 + j, (*chip, c), me).wait_recv()
                cp = copy(a, 4 + j, (*chip, c), sibling)
                cp.start()
                passed.append(cp)
        for a in range(n):
            copy(a, 0, sibling, me).wait_recv()
            for j, chip in enumerate(chips):
                copy(a, 4 + j, (*chip, 1 - c), me).wait_recv()
        for cp in first + passed:
            cp.wait_send()
        for cp in mine:
            cp.wait()

    any_spec = pl.BlockSpec(memory_space=pl.ANY)
    return pl.pallas_call(
        body, name=name,
        out_shape=[jax.ShapeDtypeStruct((N_DEV,) + a.shape, a.dtype) for a in arrays],
        in_specs=[any_spec] * n, out_specs=[any_spec] * n,
        scratch_shapes=[pltpu.SemaphoreType.DMA((n, 7)), pltpu.SemaphoreType.DMA((n, 7)), pltpu.SemaphoreType.DMA((n,))],
    )(*arrays)


def _exchange_cores(arrays):
    n = len(arrays)

    def body(*refs):
        ins, outs = refs[:n], refs[n:2 * n]
        send_sems, recv_sems = refs[2 * n:]
        x, y, c = lax.axis_index("x"), lax.axis_index("y"), lax.axis_index("c")
        copies = [pltpu.make_async_remote_copy(
            src_ref=ins[a].at[1 - c], dst_ref=outs[a], send_sem=send_sems.at[a], recv_sem=recv_sems.at[a],
            device_id=(x, y, 1 - c), device_id_type=MESH) for a in range(n)]
        for cp in copies:
            cp.start()
        for cp in copies:
            cp.wait()

    any_spec = pl.BlockSpec(memory_space=pl.ANY)
    return pl.pallas_call(
        body, name="grad_exchange_cores",
        out_shape=[jax.ShapeDtypeStruct(a.shape[1:], a.dtype) for a in arrays],
        in_specs=[any_spec] * n, out_specs=[any_spec] * n,
        scratch_shapes=[pltpu.SemaphoreType.DMA((n,)), pltpu.SemaphoreType.DMA((n,))],
    )(*arrays)


def _exchange_chips(arrays):
    n = len(arrays)

    def body(*refs):
        ins, outs = refs[:n], refs[n:2 * n]
        send_sems, recv_sems, local_sems = refs[2 * n:]
        x, y, c = lax.axis_index("x"), lax.axis_index("y"), lax.axis_index("c")
        my_chip = 2 * x + y
        chips = [(1 - x, y), (x, 1 - y), (1 - x, 1 - y)]
        mine = [pltpu.make_async_copy(ins[a].at[my_chip], outs[a].at[my_chip], local_sems.at[a]) for a in range(n)]
        for cp in mine:
            cp.start()
        def copy(a, j, src_chip, dst_chip, to):
            return pltpu.make_async_remote_copy(
                src_ref=ins[a].at[dst_chip], dst_ref=outs[a].at[src_chip],
                send_sem=send_sems.at[a, j], recv_sem=recv_sems.at[a, j], device_id=to, device_id_type=MESH)

        sends = [copy(a, j, my_chip, 2 * chip[0] + chip[1], (*chip, c)) for a in range(n) for j, chip in enumerate(chips)]
        for cp in sends:
            cp.start()
        for a in range(n):
            for j, chip in enumerate(chips):
                copy(a, j, 2 * chip[0] + chip[1], my_chip, (x, y, c)).wait_recv()
        for cp in sends:
            cp.wait_send()
        for cp in mine:
            cp.wait()

    any_spec = pl.BlockSpec(memory_space=pl.ANY)
    return pl.pallas_call(
        body, name="grad_exchange_chips",
        out_shape=[jax.ShapeDtypeStruct(a.shape, a.dtype) for a in arrays],
        in_specs=[any_spec] * n, out_specs=[any_spec] * n,
        scratch_shapes=[pltpu.SemaphoreType.DMA((n, 3)), pltpu.SemaphoreType.DMA((n, 3)), pltpu.SemaphoreType.DMA((n,))],
    )(*arrays)


def _add_core_halves(mine_all, recv, core):
    _, nchip, r, c = mine_all.shape
    tr = r if r <= 512 else (512 if r % 512 == 0 else 352)
    assert r % tr == 0

    def body(core_ref, a_ref, b_ref, o32_ref, o16_ref):
        s = a_ref[...] + b_ref[...]
        o32_ref[...] = s
        o16_ref[...] = s.astype(BF16)

    grid_spec = pltpu.PrefetchScalarGridSpec(
        num_scalar_prefetch=1, grid=(nchip, r // tr),
        in_specs=[pl.BlockSpec((None, None, tr, c), lambda q, i, s: (s[0], q, i, 0)),
                  pl.BlockSpec((None, tr, c), lambda q, i, s: (q, i, 0))],
        out_specs=[pl.BlockSpec((None, tr, c), lambda q, i, s: (q, i, 0)),
                   pl.BlockSpec((None, tr, c), lambda q, i, s: (q, i, 0))])
    return pl.pallas_call(
        body, name="grad_add_cores", grid_spec=grid_spec,
        out_shape=[jax.ShapeDtypeStruct((nchip, r, c), F32), jax.ShapeDtypeStruct((nchip, r, c), BF16)],
        compiler_params=_cparams(("arbitrary", "arbitrary")),
    )(core, mine_all, recv)


def _adam_math(w, g, m, v):
    m = ADAM_B1 * m + (1.0 - ADAM_B1) * g
    v = ADAM_B2 * v + (1.0 - ADAM_B2) * (g * g)
    m_hat = m / (1.0 - ADAM_B1 ** ADAM_STEP)
    v_hat = v / (1.0 - ADAM_B2 ** ADAM_STEP)
    delta = -ADAM_LR * (m_hat / (jnp.sqrt(v_hat) + ADAM_EPS) + ADAM_WD * w)
    return delta, m, v


def _adam_big(w, m, v, sum32, recv16, chip):
    r, c = w.shape
    tr = r if r <= 512 else (512 if r % 512 == 0 else 352)
    assert r % tr == 0

    def body(chip_ref, w_ref, m_ref, v_ref, s_ref, r_ref, g_out, d_out, m_out, v_out):
        me = chip_ref[0]
        g = None
        for q in range(N_CHIP):
            term = jnp.where(q == me, s_ref[...], r_ref[q].astype(F32))
            g = term if g is None else g + term
        d, mn, vn = _adam_math(w_ref[...], g, m_ref[...], v_ref[...])
        g_out[...] = g
        d_out[...] = d
        m_out[...] = mn
        v_out[...] = vn

    blk = pl.BlockSpec((tr, c), lambda i, s: (i, 0))
    grid_spec = pltpu.PrefetchScalarGridSpec(
        num_scalar_prefetch=1, grid=(r // tr,),
        in_specs=[blk, blk, blk, pl.BlockSpec((None, tr, c), lambda i, s: (s[0], i, 0)),
                  pl.BlockSpec((N_CHIP, tr, c), lambda i, s: (0, i, 0))],
        out_specs=[blk, blk, blk, blk])
    return pl.pallas_call(
        body, name="adamw_big", grid_spec=grid_spec,
        out_shape=[jax.ShapeDtypeStruct((r, c), F32)] * 4,
        compiler_params=_cparams(("arbitrary",)),
    )(chip, w, m, v, sum32, recv16)


def _adam_small(w, m, v, gathered):
    def body(w_ref, m_ref, v_ref, p_ref, g_out, d_out, m_out, v_out):
        g = p_ref[0]
        for dev in range(1, N_DEV):
            g = g + p_ref[dev]
        d, mn, vn = _adam_math(w_ref[...], g, m_ref[...], v_ref[...])
        g_out[...] = g
        d_out[...] = d
        m_out[...] = mn
        v_out[...] = vn

    return pl.pallas_call(
        body, name="adamw_small",
        out_shape=[jax.ShapeDtypeStruct(w.shape, F32)] * 4,
        compiler_params=_cparams(),
    )(w, m, v, gathered)


def _pick(n, prefer):
    for t in prefer:
        if n % t == 0:
            return t
    return n


def _ffn_fwd(x, g, wg, wu, wd, name):
    T, D = x.shape
    F = wg.shape[1]
    tm = _pick(T, (512, 256, 128))
    fc = _pick(F, (256, 128))

    def body(x_ref, g_ref, wg_ref, wu_ref, wd_ref, h_ref, n_ref, gate_ref, up_ref):
        xv = x_ref[...]
        n, _ = _rms_fwd(xv, g_ref[...])
        nb = n.astype(BF16)
        n_ref[...] = nb
        acc = jnp.zeros((tm, D), F32)
        for f0 in range(0, F, fc):
            gate = _dot(nb, wg_ref[:, f0:f0 + fc])
            up = _dot(nb, wu_ref[:, f0:f0 + fc])
            gate_ref[:, f0:f0 + fc] = gate.astype(BF16)
            up_ref[:, f0:f0 + fc] = up.astype(BF16)
            act = gate * (1.0 / (1.0 + jnp.exp(-gate))) * up
            acc = acc + _dot(act.astype(BF16), wd_ref[f0:f0 + fc, :])
        h_ref[...] = xv + 0.5 * acc

    return pl.pallas_call(
        body, name=name, grid=(T // tm,),
        in_specs=[_row_spec(tm, D), _const_spec((1, D)), _const_spec((D, F)), _const_spec((D, F)), _const_spec((F, D))],
        out_specs=[_row_spec(tm, D), _row_spec(tm, D), _row_spec(tm, F), _row_spec(tm, F)],
        out_shape=[jax.ShapeDtypeStruct((T, D), F32), jax.ShapeDtypeStruct((T, D), BF16),
                   jax.ShapeDtypeStruct((T, F), BF16), jax.ShapeDtypeStruct((T, F), BF16)],
        compiler_params=_cparams(("arbitrary",)),
    )(x, g, wg, wu, wd)


def _ffn_bwd(x, g, gate, up, dh, wg, wu, wd, name):
    T, D = x.shape
    F = wg.shape[1]
    tm = _pick(T, (512, 256, 128))
    halves = 2
    Fh = F // halves
    assert Fh * halves == F and Fh % 128 == 0
    chunks = [(f0, min(f0 + 512, Fh)) for f0 in range(0, Fh, 512)]

    def body(x_ref, g_ref, gate_ref, up_ref, dh_ref, wg_ref, wu_ref, wd_ref, dx_ref, dg_ref, act_ref, dgate_ref, dup_ref, dn_ref):
        i, h = pl.program_id(0), pl.program_id(1)
        dhv = dh_ref[...]
        dho = (0.5 * dhv).astype(BF16)
        dn = jnp.zeros((tm, D), F32)
        for f0, f1 in chunks:
            d_act = _dot_nt(dho, wd_ref[f0:f1, :])
            gt = gate_ref[:, f0:f1].astype(F32)
            u = up_ref[:, f0:f1].astype(F32)
            sg = 1.0 / (1.0 + jnp.exp(-gt))
            silu = gt * sg
            act_ref[:, f0:f1] = (silu * u).astype(BF16)
            d_up = (d_act * silu).astype(BF16)
            d_gate = (d_act * u * (sg + silu * (1.0 - sg))).astype(BF16)
            dup_ref[:, f0:f1] = d_up
            dgate_ref[:, f0:f1] = d_gate
            dn = dn + _dot_nt(d_gate, wg_ref[:, f0:f1]) + _dot_nt(d_up, wu_ref[:, f0:f1])

        @pl.when(h == 0)
        def _():
            dn_ref[...] = dn

        @pl.when(h > 0)
        def _():
            dn_ref[...] += dn

        @pl.when((i == 0) & (h == 0))
        def _():
            dg_ref[...] = jnp.zeros_like(dg_ref)

        @pl.when(h == halves - 1)
        def _():
            xv, gv = x_ref[...], g_ref[...]
            _, r = _rms_fwd(xv, gv)
            dxn, dgp = _rms_bwd(xv, gv, r, dn_ref[...])
            dx_ref[...] = dhv + dxn
            dg_ref[...] += dgp

    tile = pl.BlockSpec((tm, D), lambda i, h: (i, 0))
    wide = pl.BlockSpec((tm, Fh), lambda i, h: (i, h))
    return pl.pallas_call(
        body, name=name, grid=(T // tm, halves),
        in_specs=[tile, _const_spec((1, D)), wide, wide, tile,
                  pl.BlockSpec((D, Fh), lambda i, h: (0, h)), pl.BlockSpec((D, Fh), lambda i, h: (0, h)),
                  pl.BlockSpec((Fh, D), lambda i, h: (h, 0))],
        out_specs=[tile, _acc_spec((1, D)), wide, wide, wide],
        out_shape=[jax.ShapeDtypeStruct((T, D), F32), jax.ShapeDtypeStruct((1, D), F32),
                   jax.ShapeDtypeStruct((T, F), BF16), jax.ShapeDtypeStruct((T, F), BF16), jax.ShapeDtypeStruct((T, F), BF16)],
        scratch_shapes=[pltpu.VMEM((tm, D), F32)],
        compiler_params=_cparams(("arbitrary", "arbitrary")),
    )(x, g, gate, up, dh, wg, wu, wd)


def _matmul_tn(a, b, name, scale=None):
    T, K = a.shape
    N = b.shape[1]
    tk = _pick(K, (1024, 1408, 512, 256, 128))
    tn = _pick(N, (1024, 1408, 1280, 768, 512, 256, 128))
    tt = _pick(T, (2048, 1024, 512, 256, 128))
    n_t = T // tt

    def body(a_ref, b_ref, o_ref):
        @pl.when(pl.program_id(2) == 0)
        def _():
            o_ref[...] = jnp.zeros_like(o_ref)

        o_ref[...] += _dot_tn(a_ref[...].astype(BF16), b_ref[...].astype(BF16))

        if scale is not None:
            @pl.when(pl.program_id(2) == n_t - 1)
            def _():
                o_ref[...] = o_ref[...] * scale

    return pl.pallas_call(
        body, name=name, grid=(K // tk, N // tn, n_t),
        in_specs=[pl.BlockSpec((tt, tk), lambda i, j, t: (t, i)), pl.BlockSpec((tt, tn), lambda i, j, t: (t, j))],
        out_specs=pl.BlockSpec((tk, tn), lambda i, j, t: (i, j)),
        out_shape=jax.ShapeDtypeStruct((K, N), F32),
        compiler_params=_cparams(("arbitrary", "arbitrary", "arbitrary")),
    )(a, b)


def _rope_spec(tm, S):
    per = S // tm
    return pl.BlockSpec((tm, HEAD_BLOCK), lambda i: (i % per, 0))


def _proj_fwd(h1, g_mix, w_in, g_q, wq, g_kv, wkv, rope_c, rope_s, S):
    T, D = h1.shape
    P = w_in.shape[1]
    tm = _pick(S, (512, 256, 128))
    QW = MLA_HEADS * HEAD_BLOCK
    VW = MLA_HEADS * V_HEAD
    SQ = SWA_HEADS * SWA_HEAD_DIM
    SK = SWA_KV_HEADS * SWA_HEAD_DIM
    o_cq, o_ckv, o_qs, o_ks, o_vs, o_kpe = 0, Q_LORA, Q_LORA + KV_LORA, Q_LORA + KV_LORA + SQ, Q_LORA + KV_LORA + SQ + SK, Q_LORA + KV_LORA + SQ + 2 * SK
    assert P == o_kpe + HEAD_BLOCK

    def body(h_ref, gm_ref, win_ref, gq_ref, wq_ref, gkv_ref, wkv_ref, c_ref, s_ref,
             u_ref, cq_ref, ckv_ref, cqn_ref, ckvn_ref, q_ref, k_ref, v_ref, qs_ref, ks_ref, vs_ref):
        u, _ = _rms_fwd(h_ref[...], gm_ref[...])
        ub = u.astype(BF16)
        u_ref[...] = ub
        proj = _dot(ub, win_ref[...])
        c_q = proj[:, o_cq:o_ckv]
        c_kv = proj[:, o_ckv:o_qs]
        cq_ref[...] = c_q
        ckv_ref[...] = c_kv
        qs_ref[...] = proj[:, o_qs:o_ks].astype(BF16)
        ks_ref[...] = proj[:, o_ks:o_vs].astype(BF16)
        vs_ref[...] = proj[:, o_vs:o_kpe].astype(BF16)
        cb, sb = c_ref[...], s_ref[...]
        kpe = proj[:, o_kpe:P]
        kpe = kpe * cb + _swap16(kpe) * sb
        cqn, _ = _rms_fwd(c_q, gq_ref[...])
        cqn = cqn.astype(BF16)
        cqn_ref[...] = cqn
        q = _dot(cqn, wq_ref[...])
        q = q * jnp.tile(cb, (1, MLA_HEADS)) + _swap16(q) * jnp.tile(sb, (1, MLA_HEADS))
        q_ref[...] = q.astype(BF16)
        ckvn, _ = _rms_fwd(c_kv, gkv_ref[...])
        ckvn = ckvn.astype(BF16)
        ckvn_ref[...] = ckvn
        kv = _dot(ckvn, wkv_ref[...])
        k_ref[...] = (kv[:, :QW] + jnp.tile(kpe, (1, MLA_HEADS))).astype(BF16)
        v_ref[...] = kv[:, QW:].astype(BF16)

    widths = [(D, BF16), (Q_LORA, F32), (KV_LORA, F32), (Q_LORA, BF16), (KV_LORA, BF16), (QW, BF16), (QW, BF16), (VW, BF16),
              (SQ, BF16), (SK, BF16), (SK, BF16)]
    return pl.pallas_call(
        body, name="proj_fwd", grid=(T // tm,),
        in_specs=[_row_spec(tm, D), _const_spec((1, D)), _const_spec((D, P)), _const_spec((1, Q_LORA)),
                  _const_spec((Q_LORA, QW)), _const_spec((1, KV_LORA)), _const_spec((KV_LORA, QW + VW)),
                  _rope_spec(tm, S), _rope_spec(tm, S)],
        out_specs=[_row_spec(tm, w) for w, _ in widths],
        out_shape=[jax.ShapeDtypeStruct((T, w), dt) for w, dt in widths],
        compiler_params=_cparams(("arbitrary",)),
    )(h1, g_mix, w_in, g_q, wq, g_kv, wkv, rope_c, rope_s)


def _proj_bwd(dq, dk, dv, dqs, dks, dvs, c_q, c_kv, h1, dh2, g_mix, w_in, g_q, wq, g_kv, wkv, rope_c, rope_s, S):
    T, D = h1.shape
    P = w_in.shape[1]
    tm = _pick(S, (256, 128))
    QW = MLA_HEADS * HEAD_BLOCK
    VW = MLA_HEADS * V_HEAD
    SQ = SWA_HEADS * SWA_HEAD_DIM
    SK = SWA_KV_HEADS * SWA_HEAD_DIM

    def body(dq_ref, dk_ref, dv_ref, dqs_ref, dks_ref, dvs_ref, cq_ref, ckv_ref, h_ref, dh2_ref, gm_ref, win_ref, gq_ref,
             wq_ref, gkv_ref, wkv_ref, c_ref, s_ref, dh1_ref, dproj_ref, dql_ref, dkv_ref, dgm_ref, dgq_ref, dgkv_ref):
        cb, sb = c_ref[...], s_ref[...]
        dqv = dq_ref[...]
        dql = dqv * jnp.tile(cb, (1, MLA_HEADS)) + _swap16(dqv * jnp.tile(sb, (1, MLA_HEADS)))
        dql = dql.astype(BF16)
        dql_ref[...] = dql
        c_q = cq_ref[...]
        _, rq = _rms_fwd(c_q, gq_ref[...])
        d_cq, dgq = _rms_bwd(c_q, gq_ref[...], rq, _dot_nt(dql, wq_ref[...]))

        dkv_all = dk_ref[...]
        dkpe = dkv_all[:, 0:HEAD_BLOCK]
        for h in range(1, MLA_HEADS):
            dkpe = dkpe + dkv_all[:, h * HEAD_BLOCK:(h + 1) * HEAD_BLOCK]
        lane = lax.broadcasted_iota(jnp.int32, dkpe.shape, 1)
        dkpe = jnp.where((lane >= 64) & (lane < 96), dkpe, 0.0)
        dkpe = dkpe * cb + _swap16(dkpe * sb)
        dkvc = jnp.concatenate([dkv_all.astype(BF16), dv_ref[...].astype(BF16)], axis=1)
        dkv_ref[...] = dkvc
        c_kv = ckv_ref[...]
        _, rkv = _rms_fwd(c_kv, gkv_ref[...])
        d_ckv, dgkv = _rms_bwd(c_kv, gkv_ref[...], rkv, _dot_nt(dkvc, wkv_ref[...]))

        dproj = jnp.concatenate([d_cq.astype(BF16), d_ckv.astype(BF16), dqs_ref[...].astype(BF16),
                                 dks_ref[...].astype(BF16), dvs_ref[...].astype(BF16), dkpe.astype(BF16)], axis=1)
        dproj_ref[...] = dproj
        hv = h_ref[...]
        _, rm = _rms_fwd(hv, gm_ref[...])
        dxn, dgm = _rms_bwd(hv, gm_ref[...], rm, _dot_nt(dproj, win_ref[...]))
        dh1_ref[...] = dh2_ref[...] + dxn

        @pl.when(pl.program_id(0) == 0)
        def _():
            dgm_ref[...] = jnp.zeros_like(dgm_ref)
            dgq_ref[...] = jnp.zeros_like(dgq_ref)
            dgkv_ref[...] = jnp.zeros_like(dgkv_ref)

        dgm_ref[...] += dgm
        dgq_ref[...] += dgq
        dgkv_ref[...] += dgkv

    return pl.pallas_call(
        body, name="proj_bwd", grid=(T // tm,),
        in_specs=[_row_spec(tm, QW), _row_spec(tm, QW), _row_spec(tm, VW), _row_spec(tm, SQ), _row_spec(tm, SK), _row_spec(tm, SK),
                  _row_spec(tm, Q_LORA), _row_spec(tm, KV_LORA), _row_spec(tm, D), _row_spec(tm, D),
                  _const_spec((1, D)), _const_spec((D, P)), _const_spec((1, Q_LORA)), _const_spec((Q_LORA, QW)),
                  _const_spec((1, KV_LORA)), _const_spec((KV_LORA, QW + VW)), _rope_spec(tm, S), _rope_spec(tm, S)],
        out_specs=[_row_spec(tm, D), _row_spec(tm, P), _row_spec(tm, QW), _row_spec(tm, QW + VW),
                   _acc_spec((1, D)), _acc_spec((1, Q_LORA)), _acc_spec((1, KV_LORA))],
        out_shape=[jax.ShapeDtypeStruct((T, D), F32), jax.ShapeDtypeStruct((T, P), BF16), jax.ShapeDtypeStruct((T, QW), BF16),
                   jax.ShapeDtypeStruct((T, QW + VW), BF16), jax.ShapeDtypeStruct((1, D), F32),
                   jax.ShapeDtypeStruct((1, Q_LORA), F32), jax.ShapeDtypeStruct((1, KV_LORA), F32)],
        compiler_params=_cparams(("arbitrary",)),
    )(dq, dk, dv, dqs, dks, dvs, c_q, c_kv, h1, dh2, g_mix, w_in, g_q, wq, g_kv, wkv, rope_c, rope_s)


def _out_fwd(o_mla, o_swa, g_mla, g_swa, w_o, h1):
    T, D = h1.shape
    W = o_mla.shape[1]
    tm = _pick(T, (512, 256, 128))

    def body(om_ref, os_ref, gm_ref, gs_ref, wo_ref, h_ref, h2_ref, oc_ref):
        a, _ = _rms_fwd(om_ref[...], gm_ref[...])
        b, _ = _rms_fwd(os_ref[...], gs_ref[...])
        oc = jnp.concatenate([a.astype(BF16), b.astype(BF16)], axis=1)
        oc_ref[...] = oc
        h2_ref[...] = h_ref[...] + _dot(oc, wo_ref[...])

    return pl.pallas_call(
        body, name="out_fwd", grid=(T // tm,),
        in_specs=[_row_spec(tm, W), _row_spec(tm, W), _const_spec((1, W)), _const_spec((1, W)), _const_spec((2 * W, D)),
                  _row_spec(tm, D)],
        out_specs=[_row_spec(tm, D), _row_spec(tm, 2 * W)],
        out_shape=[jax.ShapeDtypeStruct((T, D), F32), jax.ShapeDtypeStruct((T, 2 * W), BF16)],
        compiler_params=_cparams(("arbitrary",)),
    )(o_mla, o_swa, g_mla, g_swa, w_o, h1)


def _out_bwd(dh2, o_mla, o_swa, g_mla, g_swa, w_o):
    T, D = dh2.shape
    W = o_mla.shape[1]
    tm = _pick(T, (512, 256, 128))

    def body(dh_ref, om_ref, os_ref, gm_ref, gs_ref, wo_ref, dom_ref, dos_ref, dgm_ref, dgs_ref):
        doc = _dot_nt(dh_ref[...].astype(BF16), wo_ref[...])
        om, osw = om_ref[...], os_ref[...]
        _, ra = _rms_fwd(om, gm_ref[...])
        _, rb = _rms_fwd(osw, gs_ref[...])
        da, dga = _rms_bwd(om, gm_ref[...], ra, doc[:, :W])
        db, dgb = _rms_bwd(osw, gs_ref[...], rb, doc[:, W:])
        dom_ref[...] = da
        dos_ref[...] = db

        @pl.when(pl.program_id(0) == 0)
        def _():
            dgm_ref[...] = jnp.zeros_like(dgm_ref)
            dgs_ref[...] = jnp.zeros_like(dgs_ref)

        dgm_ref[...] += dga
        dgs_ref[...] += dgb

    return pl.pallas_call(
        body, name="out_bwd", grid=(T // tm,),
        in_specs=[_row_spec(tm, D), _row_spec(tm, W), _row_spec(tm, W), _const_spec((1, W)), _const_spec((1, W)),
                  _const_spec((2 * W, D))],
        out_specs=[_row_spec(tm, W), _row_spec(tm, W), _acc_spec((1, W)), _acc_spec((1, W))],
        out_shape=[jax.ShapeDtypeStruct((T, W), F32), jax.ShapeDtypeStruct((T, W), F32),
                   jax.ShapeDtypeStruct((1, W), F32), jax.ShapeDtypeStruct((1, W), F32)],
        compiler_params=_cparams(("arbitrary",)),
    )(dh2, o_mla, o_swa, g_mla, g_swa, w_o)


def _loss_head(h3, target, g_final):
    T, D = h3.shape
    tm = _pick(T, (512, 256, 128))

    def body(h_ref, t_ref, g_ref, dh_ref, loss_ref, dg_ref):
        hv, gv = h_ref[...], g_ref[...]
        y, r = _rms_fwd(hv, gv)
        diff = y - t_ref[...]
        dxn, dgp = _rms_bwd(hv, gv, r, diff * (1.0 / D))
        dh_ref[...] = dxn
        part = 0.5 * jnp.sum(jnp.sum(diff * diff, axis=-1, keepdims=True) * (1.0 / D), axis=0, keepdims=True)

        @pl.when(pl.program_id(0) == 0)
        def _():
            loss_ref[...] = jnp.zeros_like(loss_ref)
            dg_ref[...] = jnp.zeros_like(dg_ref)

        loss_ref[...] += jnp.broadcast_to(part, loss_ref.shape)
        dg_ref[...] += dgp

    return pl.pallas_call(
        body, name="loss_head", grid=(T // tm,),
        in_specs=[_row_spec(tm, D), _row_spec(tm, D), _const_spec((1, D))],
        out_specs=[_row_spec(tm, D), _acc_spec((1, 128)), _acc_spec((1, D))],
        out_shape=[jax.ShapeDtypeStruct((T, D), F32), jax.ShapeDtypeStruct((1, 128), F32), jax.ShapeDtypeStruct((1, D), F32)],
        compiler_params=_cparams(("arbitrary",)),
    )(h3, target, g_final)


def _half_mask(shape, half):
    lane = lax.broadcasted_iota(jnp.int32, shape, 1)
    return (lane < 64) if half == 0 else (lane >= 64)


def _mla_fwd(q, k, v, B, S):
    T = B * S
    tq = _pick(S, MLA_Q_TILES)
    nq = S // tq
    pairs = MLA_HEADS // 2

    def body(q_ref, k_ref, v_ref, o_ref, lse_ref):
        causal = lax.broadcasted_iota(jnp.int32, (tq, tq), 1) <= lax.broadcasted_iota(jnp.int32, (tq, tq), 0)
        low = _half_mask((tq, 2 * V_HEAD), 0)
        for qi in range(nq):
            rows = slice(qi * tq, (qi + 1) * tq)
            past = slice(0, qi * tq)
            outs, lses = [], []
            for half in range(2):
                lanes = slice(half * HEAD_BLOCK, (half + 1) * HEAD_BLOCK)
                qv = q_ref[rows, lanes]
                s_d = jnp.where(causal, _dot_nt(qv, k_ref[rows, lanes]) * MLA_SCALE, NEG)
                m = jnp.max(s_d, axis=-1, keepdims=True)
                if qi:
                    s_p = _dot_nt(qv, k_ref[past, lanes]) * MLA_SCALE
                    m = jnp.maximum(m, jnp.max(s_p, axis=-1, keepdims=True))
                p_d = jnp.exp(s_d - m)
                l = jnp.sum(p_d, axis=-1, keepdims=True)
                acc = _dot(p_d.astype(BF16), v_ref[rows, :])
                if qi:
                    p_p = jnp.exp(s_p - m)
                    l = l + jnp.sum(p_p, axis=-1, keepdims=True)
                    acc = acc + _dot(p_p.astype(BF16), v_ref[past, :])
                outs.append(acc * (1.0 / l))
                lses.append(jnp.broadcast_to(m + jnp.log(l), (tq, 2 * V_HEAD)))
            o_ref[rows, :] = jnp.where(low, outs[0], outs[1])
            lse_ref[rows, :] = jnp.where(low, lses[0], lses[1])

    blk2 = pl.BlockSpec((S, 2 * HEAD_BLOCK), lambda b, j: (b, j))
    blk1 = pl.BlockSpec((S, 2 * V_HEAD), lambda b, j: (b, j))
    return pl.pallas_call(
        body, name="mla_fwd", grid=(B, pairs),
        in_specs=[blk2, blk2, blk1], out_specs=[blk1, blk1],
        out_shape=[jax.ShapeDtypeStruct((T, pairs * 2 * V_HEAD), F32)] * 2,
        compiler_params=_cparams(("arbitrary", "arbitrary")),
    )(q, k, v)


def _mla_bwd(q, k, v, do, o, lse, B, S):
    T = B * S
    tq = _pick(S, MLA_Q_TILES)
    nq = S // tq
    pairs = MLA_HEADS // 2

    def body(q_ref, k_ref, v_ref, do_ref, o_ref, lse_ref, dq_ref, dk_ref, dv_ref, dk_acc):
        causal = lax.broadcasted_iota(jnp.int32, (tq, tq), 1) <= lax.broadcasted_iota(jnp.int32, (tq, tq), 0)
        for half in range(2):
            lanes = slice(half * HEAD_BLOCK, (half + 1) * HEAD_BLOCK)
            col = half * V_HEAD
            own = _half_mask((tq, 2 * V_HEAD), half)
            for qi in reversed(range(nq)):
                rows = slice(qi * tq, (qi + 1) * tq)
                past = slice(0, qi * tq)
                first = qi == nq - 1
                qv = q_ref[rows, lanes]
                dov = jnp.where(own, do_ref[rows, :], 0.0)
                dsum = jnp.sum(dov * o_ref[rows, :], axis=-1, keepdims=True)
                dob = dov.astype(BF16)
                lse_col = lse_ref[rows, col:col + 1]

                def grads(keys, mask):
                    kv, vv = k_ref[keys, lanes], v_ref[keys, :]
                    p = jnp.exp(_dot_nt(qv, kv) * MLA_SCALE - lse_col)
                    if mask is not None:
                        p = jnp.where(mask, p, 0.0)
                    ds = (p * (_dot_nt(dob, vv) - dsum) * MLA_SCALE).astype(BF16)
                    dk, dv = _dot_tn(ds, qv), _dot_tn(p.astype(BF16), dob)
                    if first:
                        dk_acc[keys, :] = dk
                    else:
                        dk_acc[keys, :] += dk
                    if first and half == 0:
                        dv_ref[keys, :] = dv
                    else:
                        dv_ref[keys, :] += dv
                    return _dot(ds, kv)

                dq = grads(rows, causal)
                if qi:
                    dq = dq + grads(past, None)
                dq_ref[rows, lanes] = dq
            dk_ref[:, lanes] = dk_acc[...]

    blk2 = pl.BlockSpec((S, 2 * HEAD_BLOCK), lambda b, j: (b, j))
    blk1 = pl.BlockSpec((S, 2 * V_HEAD), lambda b, j: (b, j))
    return pl.pallas_call(
        body, name="mla_bwd", grid=(B, pairs),
        in_specs=[blk2, blk2, blk1, blk1, blk1, blk1], out_specs=[blk2, blk2, blk1],
        out_shape=[jax.ShapeDtypeStruct((T, MLA_HEADS * HEAD_BLOCK), F32), jax.ShapeDtypeStruct((T, MLA_HEADS * HEAD_BLOCK), F32),
                   jax.ShapeDtypeStruct((T, pairs * 2 * V_HEAD), F32)],
        scratch_shapes=[pltpu.VMEM((S, HEAD_BLOCK), F32)],
        compiler_params=_cparams(("arbitrary", "arbitrary")),
    )(q, k, v, do, o, lse)


SWA_GROUP = SWA_HEADS // SWA_KV_HEADS
SWA_ROWS = SWA_GROUP * WINDOW


SWA_UNROLL = 4


def _loop_unrolled(n, unroll, body):
    u = next(c for c in (unroll, 2, 1) if n % c == 0)

    def trip(t, carry):
        for i in range(u):
            body(t * u + i, 0)
        return carry

    lax.fori_loop(0, n // u, trip, 0)


def _swa_stack(ref, r0):
    return jnp.concatenate([ref[pl.ds(r0, WINDOW), j * WINDOW:(j + 1) * WINDOW] for j in range(SWA_GROUP)], axis=0)


def _swa_unstack(low, high):
    sel = _half_mask((WINDOW, WINDOW), 0)
    return jnp.concatenate([jnp.where(sel, low[j * WINDOW:(j + 1) * WINDOW], high[j * WINDOW:(j + 1) * WINDOW])
                            for j in range(SWA_GROUP)], axis=1)


def _swa_sink_col(sink_ref, g):
    return jnp.concatenate([jnp.broadcast_to(sink_ref[g * SWA_GROUP + j:g * SWA_GROUP + j + 1, 0:1], (WINDOW, 1))
                            for j in range(SWA_GROUP)], axis=0)


def _swa_probs(qm, kcat, bias_g, sink, n):
    qi = lax.broadcasted_iota(jnp.int32, (SWA_ROWS, 2 * WINDOW), 0) % WINDOW
    col = lax.broadcasted_iota(jnp.int32, (SWA_ROWS, 2 * WINDOW), 1)
    valid = (col > qi) & (col <= qi + WINDOW) & ((col >= WINDOW) | (n > 0))
    s = jnp.where(valid, _dot_nt(qm, kcat) * SWA_SCALE + bias_g, NEG)
    m = jnp.maximum(jnp.max(s, axis=-1, keepdims=True), sink)
    e = jnp.exp(s - m)
    e_s = jnp.exp(sink - m)
    inv = 1.0 / (jnp.sum(e, axis=-1, keepdims=True) + e_s)
    return e * inv, e_s * inv


def _swa_fwd(q, k, v, bias, sinks, B, S):
    T = B * S
    W = WINDOW
    nb = S // W

    def body(q_ref, k_ref, v_ref, bias_ref, sink_ref, o_ref):
        sink_cols = [_swa_sink_col(sink_ref, g) for g in range(SWA_KV_HEADS)]

        def block(n, _):
            r0 = pl.multiple_of(n * W, W)
            p0 = pl.multiple_of(jnp.maximum(n - 1, 0) * W, W)
            kcat = jnp.concatenate([k_ref[pl.ds(p0, W), :], k_ref[pl.ds(r0, W), :]], axis=0)
            vcat = jnp.concatenate([v_ref[pl.ds(p0, W), :], v_ref[pl.ds(r0, W), :]], axis=0)
            qcat = _swa_stack(q_ref, r0)
            outs = []
            for g in range(SWA_KV_HEADS):
                qm = jnp.where(_half_mask(qcat.shape, g), qcat, jnp.zeros_like(qcat))
                p, _ = _swa_probs(qm, kcat, bias_ref[g], sink_cols[g], n)
                outs.append(_dot(p.astype(BF16), vcat))
            o_ref[pl.ds(r0, W), :] = _swa_unstack(outs[0], outs[1])
            return 0

        _loop_unrolled(nb, SWA_UNROLL, block)

    SQ, SK = SWA_HEADS * SWA_HEAD_DIM, SWA_KV_HEADS * SWA_HEAD_DIM
    return pl.pallas_call(
        body, name="swa_fwd", grid=(B,),
        in_specs=[pl.BlockSpec((S, SQ), lambda b: (b, 0)), pl.BlockSpec((S, SK), lambda b: (b, 0)),
                  pl.BlockSpec((S, SK), lambda b: (b, 0)), _const_spec((SWA_KV_HEADS, SWA_ROWS, 2 * W)),
                  _const_spec((SWA_HEADS, 128))],
        out_specs=pl.BlockSpec((S, SQ), lambda b: (b, 0)),
        out_shape=jax.ShapeDtypeStruct((T, SQ), F32),
        compiler_params=_cparams(("arbitrary",)),
    )(q, k, v, bias, sinks)


def _swa_bwd(q, k, v, do, bias, sinks, B, S):
    T = B * S
    W = WINDOW
    nb = S // W

    def body(q_ref, k_ref, v_ref, do_ref, bias_ref, sink_ref, dq_ref, dk_ref, dv_ref, dbias_ref, dsink_ref):
        @pl.when(pl.program_id(0) == 0)
        def _():
            dbias_ref[...] = jnp.zeros_like(dbias_ref)
            dsink_ref[...] = jnp.zeros_like(dsink_ref)

        dk_ref[...] = jnp.zeros_like(dk_ref)
        dv_ref[...] = jnp.zeros_like(dv_ref)
        sink_cols = [_swa_sink_col(sink_ref, g) for g in range(SWA_KV_HEADS)]

        def block(n, _):
            r0 = pl.multiple_of(n * W, W)
            p0 = pl.multiple_of(jnp.maximum(n - 1, 0) * W, W)
            kcat = jnp.concatenate([k_ref[pl.ds(p0, W), :], k_ref[pl.ds(r0, W), :]], axis=0)
            vcat = jnp.concatenate([v_ref[pl.ds(p0, W), :], v_ref[pl.ds(r0, W), :]], axis=0)
            qcat = _swa_stack(q_ref, r0)
            docat = _swa_stack(do_ref, r0)
            dk = jnp.zeros((2 * W, W), F32)
            dv = jnp.zeros((2 * W, W), F32)
            dqs = []
            for g in range(SWA_KV_HEADS):
                own = _half_mask(qcat.shape, g)
                qm = jnp.where(own, qcat, jnp.zeros_like(qcat))
                dom = jnp.where(own, docat, 0.0).astype(BF16)
                p, p_s = _swa_probs(qm, kcat, bias_ref[g], sink_cols[g], n)
                dp = _dot_nt(dom, vcat)
                dsum = jnp.sum(p * dp, axis=-1, keepdims=True)
                ds = p * (dp - dsum)
                dbias_ref[g] += ds
                dsink_ref[g] += jnp.broadcast_to(-p_s * dsum, (SWA_ROWS, 128))
                dsb = (ds * SWA_SCALE).astype(BF16)
                dqs.append(_dot(dsb, kcat))
                dk = dk + _dot_tn(dsb, qm)
                dv = dv + _dot_tn(p.astype(BF16), dom)
            dq_ref[pl.ds(r0, W), :] = _swa_unstack(dqs[0], dqs[1])
            dk_ref[pl.ds(p0, W), :] += dk[:W]
            dv_ref[pl.ds(p0, W), :] += dv[:W]
            dk_ref[pl.ds(r0, W), :] += dk[W:]
            dv_ref[pl.ds(r0, W), :] += dv[W:]
            return 0

        _loop_unrolled(nb, SWA_UNROLL, block)

    SQ, SK = SWA_HEADS * SWA_HEAD_DIM, SWA_KV_HEADS * SWA_HEAD_DIM
    row = lambda w: pl.BlockSpec((S, w), lambda b: (b, 0))
    return pl.pallas_call(
        body, name="swa_bwd", grid=(B,),
        in_specs=[row(SQ), row(SK), row(SK), row(SQ), _const_spec((SWA_KV_HEADS, SWA_ROWS, 2 * W)), _const_spec((SWA_HEADS, 128))],
        out_specs=[row(SQ), row(SK), row(SK), _acc_spec((SWA_KV_HEADS, SWA_ROWS, 2 * W)), _acc_spec((SWA_KV_HEADS, SWA_ROWS, 128))],
        out_shape=[jax.ShapeDtypeStruct((T, SQ), F32), jax.ShapeDtypeStruct((T, SK), F32), jax.ShapeDtypeStruct((T, SK), F32),
                   jax.ShapeDtypeStruct((SWA_KV_HEADS, SWA_ROWS, 2 * W), F32), jax.ShapeDtypeStruct((SWA_KV_HEADS, SWA_ROWS, 128), F32)],
        compiler_params=_cparams(("arbitrary",)),
    )(q, k, v, do, bias, sinks)


def _bias_expand(rel_bias, bucket):
    W = WINDOW

    def body(rb_ref, bucket_ref, out_ref):
        bk = bucket_ref[...]
        for h in range(SWA_HEADS):
            def add(b, acc):
                return jnp.where(bk == b, rb_ref[b, h], acc)

            out_ref[h] = lax.fori_loop(0, REL_BUCKETS, add, jnp.zeros((W, 2 * W), F32))

    return pl.pallas_call(
        body, name="bias_expand",
        in_specs=[pl.BlockSpec(memory_space=pltpu.SMEM), pl.BlockSpec(memory_space=pltpu.VMEM)],
        out_specs=pl.BlockSpec(memory_space=pltpu.VMEM),
        out_shape=jax.ShapeDtypeStruct((SWA_HEADS, W, 2 * W), F32),
        compiler_params=_cparams(),
    )(rel_bias, bucket)


def _bias_reduce(dbias, dsink_rows, bucket):
    def body(db_ref, ds_ref, bucket_ref, out_ref, sink_out_ref):
        for h in range(SWA_HEADS):
            sink_out_ref[h:h + 1, :] = jnp.sum(ds_ref[h], axis=0, keepdims=True)
        bk = bucket_ref[...]
        row = lax.broadcasted_iota(jnp.int32, (REL_BUCKETS, 128), 0)
        lane = lax.broadcasted_iota(jnp.int32, (REL_BUCKETS, 128), 1)
        acc = jnp.zeros((REL_BUCKETS, 128), F32)
        for h in range(SWA_HEADS):
            dbh = db_ref[h]

            def add(b, acc):
                part = jnp.sum(jnp.sum(jnp.where(bk == b, dbh, 0.0), axis=-1, keepdims=True), axis=0, keepdims=True)
                return jnp.where((row == b) & (lane == h), part, acc)

            acc = lax.fori_loop(0, REL_BUCKETS, add, acc)
        out_ref[...] = acc

    return pl.pallas_call(
        body, name="bias_reduce",
        out_shape=[jax.ShapeDtypeStruct((REL_BUCKETS, 128), F32), jax.ShapeDtypeStruct((SWA_HEADS, 128), F32)],
        compiler_params=_cparams(),
    )(dbias, dsink_rows, bucket)


def _t5_bucket(dist):
    n = jnp.maximum(dist, 0)
    max_exact = REL_BUCKETS // 2
    nf = jnp.maximum(n, 1).astype(F32)
    large = max_exact + (jnp.log(nf / max_exact) / math.log(REL_MAX_DIST / max_exact) * (REL_BUCKETS - max_exact)).astype(jnp.int32)
    large = jnp.minimum(large, REL_BUCKETS - 1)
    return jnp.where(n < max_exact, n, large)


def _pair_heads(a, axis, inverse=False):
    shp = a.shape
    a = a.reshape(shp[:axis] + ((4, 2, 64) if inverse else (2, 4, 64)) + shp[axis + 1:])
    a = jnp.swapaxes(a, axis, axis + 1)
    return a.reshape(shp)


def _col_full(g):
    return jnp.transpose(g, (1, 0, 2)).reshape(g.shape[1], N_DEV * g.shape[2])


def _row_full(g):
    return g.reshape(N_DEV * g.shape[1], g.shape[2])


def _col_blocks(w):
    r = w.shape[0]
    return jnp.transpose(w.reshape(r, N_CHIP, 2, -1), (2, 1, 0, 3))


def _row_blocks(w):
    c = w.shape[1]
    return jnp.transpose(w.reshape(N_CHIP, 2, -1, c), (1, 0, 2, 3))


_IN_WIDTHS = (Q_LORA, KV_LORA, QK_ROPE, SWA_HEADS * SWA_HEAD_DIM, SWA_KV_HEADS * SWA_HEAD_DIM, SWA_KV_HEADS * SWA_HEAD_DIM)


def _split_in(w):
    parts, o = [], 0
    for wd in _IN_WIDTHS:
        parts.append(w[:, o:o + wd])
        o += wd
    return parts


def _win_to_kernel(w):
    cq, ckv, kpe, qs, ks, vs = _split_in(w)
    kpe = jnp.pad(kpe, ((0, 0), (64, 32)))
    return jnp.concatenate([cq, ckv, _pair_heads(qs, 1), ks, vs, kpe], axis=1)


def _win_from_kernel(dw):
    o = [0, Q_LORA, Q_LORA + KV_LORA, Q_LORA + KV_LORA + 512, Q_LORA + KV_LORA + 640, Q_LORA + KV_LORA + 768]
    cq, ckv, qs, ks, vs, kpe = [dw[:, a:b] for a, b in zip(o, o[1:] + [dw.shape[1]])]
    return jnp.concatenate([cq, ckv, kpe[:, 64:96], _pair_heads(qs, 1, inverse=True), ks, vs], axis=1)


def _wq_to_kernel(w):
    return jnp.pad(w.reshape(Q_LORA, MLA_HEADS, QK_NOPE + QK_ROPE), ((0, 0), (0, 0), (0, 32))).reshape(Q_LORA, MLA_HEADS * HEAD_BLOCK)


def _wq_from_kernel(dw):
    return dw.reshape(Q_LORA, MLA_HEADS, HEAD_BLOCK)[:, :, :QK_NOPE + QK_ROPE].reshape(Q_LORA, -1)


def _wkv_to_kernel(w):
    w3 = w.reshape(KV_LORA, MLA_HEADS, QK_NOPE + V_HEAD)
    kpart = jnp.pad(w3[:, :, :QK_NOPE], ((0, 0), (0, 0), (0, 64))).reshape(KV_LORA, MLA_HEADS * HEAD_BLOCK)
    return jnp.concatenate([kpart, w3[:, :, QK_NOPE:].reshape(KV_LORA, MLA_HEADS * V_HEAD)], axis=1)


def _wkv_from_kernel(dw):
    dk = dw[:, :MLA_HEADS * HEAD_BLOCK].reshape(KV_LORA, MLA_HEADS, HEAD_BLOCK)[:, :, :QK_NOPE]
    dv = dw[:, MLA_HEADS * HEAD_BLOCK:].reshape(KV_LORA, MLA_HEADS, V_HEAD)
    return jnp.concatenate([dk, dv], axis=2).reshape(KV_LORA, -1)


def _wo_to_kernel(w):
    return jnp.concatenate([w[:512], _pair_heads(w[512:], 0)], axis=0)


def _wo_from_kernel(dw):
    return jnp.concatenate([dw[:512], _pair_heads(dw[512:], 0, inverse=True)], axis=0)


_SMALL = (("g_ffn1", 1024), ("g_mix", 1024), ("g_q_a", 256), ("g_kv_a", 128), ("attn_sinks", 8), ("rel_bias", 256),
          ("g_out_mla", 512), ("g_out_swa", 512), ("g_ffn2", 1024), ("g_final", 1024), ("loss", 1))


def _pack_small(vals):
    rows = []
    for name, size in _SMALL:
        v = vals[name].reshape(-1).astype(F32)
        nrow = -(-size // 128)
        rows.append(jnp.pad(v, (0, nrow * 128 - size)).reshape(nrow, 128))
    packed = jnp.concatenate(rows, axis=0)
    return jnp.pad(packed, ((0, SMALL_ROWS - packed.shape[0]), (0, 0)))


def _unpack_small(packed, shapes):
    out, r = {}, 0
    for name, size in _SMALL:
        nrow = -(-size // 128)
        out[name] = packed[r:r + nrow].reshape(-1)[:size].reshape(shapes[name])
        r += nrow
    return out


def _local_step(x, target, gains, wts):
    B, S, D = x.shape
    T = B * S
    x2 = x.reshape(T, D)
    t2 = target.reshape(T, D)

    pos = jnp.arange(S, dtype=F32)
    inv_freq = ROPE_THETA ** (-jnp.arange(0, QK_ROPE, 2, dtype=F32) / QK_ROPE)
    ang = pos[:, None] * inv_freq[None, :]
    cos, sin = jnp.cos(ang), jnp.sin(ang)
    ones, zeros = jnp.ones((S, 64), F32), jnp.zeros((S, 32), F32)
    rope_c = jnp.concatenate([ones, cos, cos, zeros], axis=1)
    rope_s = jnp.concatenate([0.0 * ones, -sin, sin, zeros], axis=1)

    qi = jnp.arange(WINDOW)[:, None]
    kj = jnp.arange(2 * WINDOW)[None, :]
    bucket = _t5_bucket(qi + WINDOW - kj).astype(jnp.int32)
    bias = _bias_expand(gains["rel_bias"], bucket).reshape(SWA_KV_HEADS, SWA_ROWS, 2 * WINDOW)
    sinks = jnp.broadcast_to(gains["attn_sinks"].reshape(SWA_HEADS, 1), (SWA_HEADS, 128))
    g_swa = _pair_heads(gains["g_out_swa"], 1)

    h1, n1, gate1, up1 = _ffn_fwd(x2, gains["g_ffn1"], wts["g1"], wts["u1"], wts["d1"], "ffn1_fwd")
    (u, c_q, c_kv, cqn, ckvn, q, k, v, qs, ks, vs) = _proj_fwd(
        h1, gains["g_mix"], wts["in"], gains["g_q_a"], wts["q"], gains["g_kv_a"], wts["kv"], rope_c, rope_s, S)
    o_mla, lse = _mla_fwd(q, k, v, B, S)
    o_swa = _swa_fwd(qs, ks, vs, bias, sinks, B, S)
    h2, oc = _out_fwd(o_mla, o_swa, gains["g_out_mla"], g_swa, wts["o"], h1)
    h3, n2, gate2, up2 = _ffn_fwd(h2, gains["g_ffn2"], wts["g2"], wts["u2"], wts["d2"], "ffn2_fwd")
    dh3, loss, dg_final = _loss_head(h3, t2, gains["g_final"].reshape(1, D))

    dh2, dg_ffn2, act2, dgate2, dup2 = _ffn_bwd(h2, gains["g_ffn2"], gate2, up2, dh3, wts["g2"], wts["u2"], wts["d2"], "ffn2_bwd")
    grads = {"g2": _matmul_tn(n2, dgate2, "dw_gate2"), "u2": _matmul_tn(n2, dup2, "dw_up2"), "d2": _matmul_tn(act2, dh3, "dw_down2", 0.5)}
    do_mla, do_swa, dg_mla, dg_swa = _out_bwd(dh2, o_mla, o_swa, gains["g_out_mla"], g_swa, wts["o"])
    grads["o"] = _matmul_tn(oc, dh2, "dw_o")
    dq, dk, dv = _mla_bwd(q, k, v, do_mla, o_mla, lse, B, S)
    dqs, dks, dvs, dbias, dsink = _swa_bwd(qs, ks, vs, do_swa, bias, sinks, B, S)
    dh1, dproj, dql, dkvc, dg_mix, dg_q, dg_kv = _proj_bwd(
        dq, dk, dv, dqs, dks, dvs, c_q, c_kv, h1, dh2, gains["g_mix"], wts["in"], gains["g_q_a"], wts["q"],
        gains["g_kv_a"], wts["kv"], rope_c, rope_s, S)
    grads["in"] = _matmul_tn(u, dproj, "dw_in")
    grads["q"] = _matmul_tn(cqn, dql, "dw_q")
    grads["kv"] = _matmul_tn(ckvn, dkvc, "dw_kv")
    dx, dg_ffn1, act1, dgate1, dup1 = _ffn_bwd(x2, gains["g_ffn1"], gate1, up1, dh1, wts["g1"], wts["u1"], wts["d1"], "ffn1_bwd")
    grads["g1"] = _matmul_tn(n1, dgate1, "dw_gate1")
    grads["u1"] = _matmul_tn(n1, dup1, "dw_up1")
    grads["d1"] = _matmul_tn(act1, dh1, "dw_down1", 0.5)

    d_rel_bias, d_sinks = _bias_reduce(dbias.reshape(SWA_HEADS, WINDOW, 2 * WINDOW), dsink.reshape(SWA_HEADS, WINDOW, 128), bucket)
    small = {
        "g_ffn1": dg_ffn1, "g_mix": dg_mix, "g_q_a": dg_q, "g_kv_a": dg_kv, "attn_sinks": d_sinks[:, 0],
        "rel_bias": d_rel_bias[:, :SWA_HEADS], "g_out_mla": dg_mla, "g_out_swa": _pair_heads(dg_swa, 1, inverse=True),
        "g_ffn2": dg_ffn2, "g_final": dg_final, "loss": loss[0, 0:1],
    }
    return dx.reshape(B, S, D), grads, small


_WEIGHTS = ("g_ffn1", "w_ffn1_gate", "w_ffn1_up", "w_ffn1_down", "g_mix", "w_in", "g_q_a", "w_q_b", "g_kv_a", "w_kv_b",
            "attn_sinks", "rel_bias", "g_out_mla", "g_out_swa", "w_o", "g_ffn2", "w_ffn2_gate", "w_ffn2_up", "w_ffn2_down",
            "g_final")
_BIG = (("w_ffn1_gate", "g1", True), ("w_ffn1_up", "u1", True), ("w_ffn1_down", "d1", False), ("w_in", "in", True),
        ("w_q_b", "q", True), ("w_kv_b", "kv", True), ("w_o", "o", False), ("w_ffn2_gate", "g2", True),
        ("w_ffn2_up", "u2", True), ("w_ffn2_down", "d2", False))
_TO_KERNEL = {"in": _win_to_kernel, "q": _wq_to_kernel, "kv": _wkv_to_kernel, "o": _wo_to_kernel}
_FROM_KERNEL = {"in": _win_from_kernel, "q": _wq_from_kernel, "kv": _wkv_from_kernel, "o": _wo_from_kernel}


def kernel(x, g_ffn1, w_ffn1_gate, w_ffn1_up, w_ffn1_down, g_mix, w_in, g_q_a, w_q_b, g_kv_a, w_kv_b, attn_sinks, rel_bias, g_out_mla, g_out_swa, w_o, g_ffn2, w_ffn2_gate, w_ffn2_up, w_ffn2_down, g_final, loss_target, m_g_ffn1, m_w_ffn1_gate, m_w_ffn1_up, m_w_ffn1_down, m_g_mix, m_w_in, m_g_q_a, m_w_q_b, m_g_kv_a, m_w_kv_b, m_attn_sinks, m_rel_bias, m_g_out_mla, m_g_out_swa, m_w_o, m_g_ffn2, m_w_ffn2_gate, m_w_ffn2_up, m_w_ffn2_down, m_g_final, v_g_ffn1, v_w_ffn1_gate, v_w_ffn1_up, v_w_ffn1_down, v_g_mix, v_w_in, v_g_q_a, v_w_q_b, v_g_kv_a, v_w_kv_b, v_attn_sinks, v_rel_bias, v_g_out_mla, v_g_out_swa, v_w_o, v_g_ffn2, v_w_ffn2_gate, v_w_ffn2_up, v_w_ffn2_down, v_g_final):
    w = dict(zip(_WEIGHTS, (g_ffn1, w_ffn1_gate, w_ffn1_up, w_ffn1_down, g_mix, w_in, g_q_a, w_q_b, g_kv_a, w_kv_b, attn_sinks,
                            rel_bias, g_out_mla, g_out_swa, w_o, g_ffn2, w_ffn2_gate, w_ffn2_up, w_ffn2_down, g_final)))
    m = dict(zip(_WEIGHTS, (m_g_ffn1, m_w_ffn1_gate, m_w_ffn1_up, m_w_ffn1_down, m_g_mix, m_w_in, m_g_q_a, m_w_q_b, m_g_kv_a,
                            m_w_kv_b, m_attn_sinks, m_rel_bias, m_g_out_mla, m_g_out_swa, m_w_o, m_g_ffn2, m_w_ffn2_gate,
                            m_w_ffn2_up, m_w_ffn2_down, m_g_final)))
    v = dict(zip(_WEIGHTS, (v_g_ffn1, v_w_ffn1_gate, v_w_ffn1_up, v_w_ffn1_down, v_g_mix, v_w_in, v_g_q_a, v_w_q_b, v_g_kv_a,
                            v_w_kv_b, v_attn_sinks, v_rel_bias, v_g_out_mla, v_g_out_swa, v_w_o, v_g_ffn2, v_w_ffn2_gate,
                            v_w_ffn2_up, v_w_ffn2_down, v_g_final)))
    core = lax.axis_index("c").astype(jnp.int32).reshape(1)
    chip = (2 * lax.axis_index("x") + lax.axis_index("y")).astype(jnp.int32).reshape(1)

    shards = _cast_bf16([w[name][0] for name, _, _ in _BIG])
    gathered = _all_gather(shards, "weights_all_gather")
    wts = {}
    for (name, key, by_col), g in zip(_BIG, gathered):
        full = _col_full(g) if by_col else _row_full(g)
        wts[key] = _TO_KERNEL[key](full) if key in _TO_KERNEL else full

    small_names = [n for n, _ in _SMALL if n != "loss"]
    gains = {n: (w[n] if n in ("rel_bias", "g_final") else w[n].reshape(1, -1)) for n in small_names}
    grad_x, grads, small = _local_step(x, loss_target, gains, wts)

    blocks = []
    for name, key, by_col in _BIG:
        g = grads[key]
        if key in _FROM_KERNEL:
            g = _FROM_KERNEL[key](g)
        blocks.append(_col_blocks(g) if by_col else _row_blocks(g))
    from_sibling = _exchange_cores(blocks)
    sums = [_add_core_halves(b, r, core) for b, r in zip(blocks, from_sibling)]
    from_chips = _exchange_chips([s16 for _, s16 in sums])

    small_gathered = _all_gather([_pack_small(small)], "small_all_gather")[0]

    out_g, out_d, out_m, out_v = {}, {}, {}, {}
    for (name, key, by_col), (s32, _), r16 in zip(_BIG, sums, from_chips):
        g, d, mn, vn = _adam_big(w[name][0], m[name][0], v[name][0], s32, r16, chip)
        out_g[name], out_d[name], out_m[name], out_v[name] = g[None], d[None], mn[None], vn[None]
    shapes = {n: w[n].shape for n in small_names}
    shapes["loss"] = ()
    zero = jnp.zeros((), F32)
    ws, ms, vs = ({**{n: src[n] for n in small_names}, "loss": zero} for src in (w, m, v))
    sg, sd, sm, sv = _adam_small(_pack_small(ws), _pack_small(ms), _pack_small(vs), small_gathered)
    sg, sd, sm, sv = (_unpack_small(p, shapes) for p in (sg, sd, sm, sv))
    for n in small_names:
        out_g[n], out_d[n], out_m[n], out_v[n] = sg[n], sd[n], sm[n], sv[n]

    return (sg["loss"], grad_x, *[out_g[n] for n in _WEIGHTS], *[out_d[n] for n in _WEIGHTS],
            *[out_m[n] for n in _WEIGHTS], *[out_v[n] for n in _WEIGHTS])
```

```python
import functools
import math

import jax
import jax.numpy as jnp
from jax import lax
from jax.experimental import pallas as pl
from jax.experimental.pallas import tpu as pltpu

F32 = jnp.float32
BF16 = jnp.bfloat16
MESH = pl.DeviceIdType.MESH

EPS = 1e-6
NEG = -1e30
N_DEV = 8

MLA_HEADS = 8
Q_LORA = 256
KV_LORA = 128
QK_NOPE = 64
QK_ROPE = 32
V_HEAD = 64
ROPE_THETA = 10000.0
SWA_HEADS = 8
SWA_KV_HEADS = 2
SWA_HEAD_DIM = 64
WINDOW = 128
REL_BUCKETS = 32
REL_MAX_DIST = 128
HEAD_BLOCK = 128
MLA_Q_TILES = (512, 256, 128)
MLA_SCALE = (QK_NOPE + QK_ROPE) ** -0.5
SWA_SCALE = SWA_HEAD_DIM ** -0.5

ADAM_LR = 0.001
ADAM_B1 = 0.9
ADAM_B2 = 0.999
ADAM_EPS = 1e-08
ADAM_WD = 0.01
ADAM_STEP = 10

VMEM_LIMIT = 56 * 1024 * 1024
SMALL_ROWS = 48


def _cparams(semantics=None):
    return pltpu.CompilerParams(dimension_semantics=semantics, vmem_limit_bytes=VMEM_LIMIT)


def _dot(a, b):
    return jnp.dot(a, b, preferred_element_type=F32)


def _dot_nt(a, b):
    return lax.dot_general(a, b, (((1,), (1,)), ((), ())), preferred_element_type=F32)


def _dot_tn(a, b):
    return lax.dot_general(a, b, (((0,), (0,)), ((), ())), preferred_element_type=F32)


def _rms_fwd(x, g):
    r = lax.rsqrt(jnp.mean(x * x, axis=-1, keepdims=True) + EPS)
    return x * r * g, r


def _rms_bwd(x, g, r, dy):
    xh = x * r
    dyg = dy * g
    dx = r * (dyg - xh * jnp.mean(dyg * xh, axis=-1, keepdims=True))
    return dx, jnp.sum(dy * xh, axis=0, keepdims=True)


def _swap16(x):
    n = x.shape[-1]
    up = pltpu.roll(x, n - 16, 1)
    down = pltpu.roll(x, 16, 1)
    lane = lax.broadcasted_iota(jnp.int32, x.shape, 1) % HEAD_BLOCK
    return jnp.where((lane >= 64) & (lane < 80), up, jnp.where((lane >= 80) & (lane < 96), down, 0.0))


def _const_spec(shape):
    return pl.BlockSpec(shape, lambda *_: (0,) * len(shape), pipeline_mode=pl.Buffered(1))


def _acc_spec(shape):
    return pl.BlockSpec(shape, lambda *_: (0,) * len(shape))


def _row_spec(tm, width):
    return pl.BlockSpec((tm, width), lambda i: (i, 0))


def _cast_bf16(arrays):
    n = len(arrays)

    def body(*refs):
        for src, dst in zip(refs[:n], refs[n:]):
            dst[...] = src[...].astype(BF16)

    return pl.pallas_call(
        body, name="cast_weights",
        out_shape=[jax.ShapeDtypeStruct(a.shape, BF16) for a in arrays],
        compiler_params=_cparams(),
    )(*arrays)


def _all_gather(arrays, name):
    n = len(arrays)

    def body(*refs):
        ins, outs = refs[:n], refs[n:2 * n]
        send_sems, recv_sems, local_sems = refs[2 * n:]
        x, y, c = lax.axis_index("x"), lax.axis_index("y"), lax.axis_index("c")
        me, sibling = (x, y, c), (x, y, 1 - c)
        chips = [(1 - x, y), (x, 1 - y), (1 - x, 1 - y)]

        def slot(a, dev):
            return outs[a].at[4 * dev[0] + 2 * dev[1] + dev[2]]

        def copy(a, k, block, to, src=None):
            return pltpu.make_async_remote_copy(
                src_ref=slot(a, block) if src is None else src, dst_ref=slot(a, block),
                send_sem=send_sems.at[a, k], recv_sem=recv_sems.at[a, k], device_id=to, device_id_type=MESH)

        mine = [pltpu.make_async_copy(ins[a], slot(a, me), local_sems.at[a]) for a in range(n)]
        for cp in mine:
            cp.start()
        first = []
        for a in range(n):
            first.append(copy(a, 0, me, sibling, src=ins[a]))
            first += [copy(a, 1 + j, me, (*chip, c), src=ins[a]) for j, chip in enumerate(chips)]
        for cp in first:
            cp.start()
        passed = []
        for j, chip in enumerate(chips):
            for a in range(n):
                copy(a, 1 + j, (*chip, c), me).wait_recv()
                cp = copy(a, 4 + j, (*chip, c), sibling)
                cp.start()
                passed.append(cp)
        for a in range(n):
            copy(a, 0, sibling, me).wait_recv()
            for j, chip in enumerate(chips):
                copy(a, 4 + j, (*chip, 1 - c), me).wait_recv()
        for cp in first + passed:
            cp.wait_send()
        for cp in mine:
            cp.wait()

    any_spec = pl.BlockSpec(memory_space=pl.ANY)
    return pl.pallas_call(
        body, name=name,
        out_shape=[jax.ShapeDtypeStruct((N_DEV,) + a.shape, a.dtype) for a in arrays],
        in_specs=[any_spec] * n, out_specs=[any_spec] * n,
        scratch_shapes=[pltpu.SemaphoreType.DMA((n, 7)), pltpu.SemaphoreType.DMA((n, 7)), pltpu.SemaphoreType.DMA((n,))],
    )(*arrays)


def _exchange_shapes(jobs):
    dsts = [jax.ShapeDtypeStruct(((N_DEV,) + a.shape) if kind == "gather" else a.shape, a.dtype) for kind, a in jobs]
    n = len(jobs)
    sems = [pltpu.SemaphoreType.DMA((n, N_DEV - 1)), pltpu.SemaphoreType.DMA((n, N_DEV - 1)), pltpu.SemaphoreType.DMA((n,))]
    return dsts, sems


def _exchange_copies(kinds, srcs, dsts, send_sems, recv_sems, local_sems):
    x, y, c = lax.axis_index("x"), lax.axis_index("y"), lax.axis_index("c")
    me = 4 * x + 2 * y + c
    remote, local = [], []
    for a, kind in enumerate(kinds):
        for k in range(1, N_DEV):
            peer = (1 - x if k & 4 else x, 1 - y if k & 2 else y, 1 - c if k & 1 else c)
            src = srcs[a] if kind == "gather" else srcs[a].at[4 * peer[0] + 2 * peer[1] + peer[2]]
            remote.append(pltpu.make_async_remote_copy(
                src_ref=src, dst_ref=dsts[a].at[me], send_sem=send_sems.at[a, k - 1], recv_sem=recv_sems.at[a, k - 1],
                device_id=peer, device_id_type=MESH))
        src = srcs[a] if kind == "gather" else srcs[a].at[me]
        local.append(pltpu.make_async_copy(src, dsts[a].at[me], local_sems.at[a]))
    return remote, local


def _exchange(jobs, name):
    n = len(jobs)
    kinds = [kind for kind, _ in jobs]
    dst_shapes, sems = _exchange_shapes(jobs)

    def body(*refs):
        remote, local = _exchange_copies(kinds, refs[:n], refs[n:2 * n], *refs[2 * n:])
        for cp in local + remote:
            cp.start()
        for cp in remote + local:
            cp.wait()

    any_spec = pl.BlockSpec(memory_space=pl.ANY)
    return pl.pallas_call(
        body, name=name, out_shape=dst_shapes, in_specs=[any_spec] * n, out_specs=[any_spec] * n, scratch_shapes=sems,
    )(*[a for _, a in jobs])


def _pcall(body, *, name, grid, in_specs, out_specs, out_shape, args, scratch_shapes=(), exchange=None):
    semantics = ("arbitrary",) * len(grid)
    if not exchange:
        outs = pl.pallas_call(body, name=name, grid=grid, in_specs=list(in_specs), out_specs=list(out_specs),
                              out_shape=list(out_shape), scratch_shapes=list(scratch_shapes),
                              compiler_params=_cparams(semantics))(*args)
        return list(outs), []
    kinds = [kind for kind, _ in exchange]
    n_in, n_out, n_job = len(in_specs), len(out_specs), len(exchange)
    dst_shapes, sems = _exchange_shapes(exchange)

    def with_exchange(*refs):
        ins, srcs = refs[:n_in], refs[n_in:n_in + n_job]
        outs, dsts = refs[n_in + n_job:n_in + n_job + n_out], refs[n_in + n_job + n_out:n_in + 2 * n_job + n_out]
        scratch, sem_refs = refs[n_in + 2 * n_job + n_out:-3], refs[-3:]
        first = functools.reduce(jnp.logical_and, [pl.program_id(d) == 0 for d in range(len(grid))])
        last = functools.reduce(jnp.logical_and, [pl.program_id(d) == grid[d] - 1 for d in range(len(grid))])

        @pl.when(first)
        def _():
            remote, local = _exchange_copies(kinds, srcs, dsts, *sem_refs)
            for cp in local + remote:
                cp.start()

        body(*ins, *outs, *scratch)

        @pl.when(last)
        def _():
            remote, local = _exchange_copies(kinds, srcs, dsts, *sem_refs)
            for cp in remote + local:
                cp.wait()

    any_spec = pl.BlockSpec(memory_space=pl.ANY)
    outs = pl.pallas_call(
        with_exchange, name=name, grid=grid, in_specs=list(in_specs) + [any_spec] * n_job,
        out_specs=list(out_specs) + [any_spec] * n_job, out_shape=list(out_shape) + dst_shapes,
        scratch_shapes=list(scratch_shapes) + sems, compiler_params=_cparams(semantics),
    )(*args, *[a for _, a in exchange])
    return list(outs[:n_out]), list(outs[n_out:])


def _adam_math(w, g, m, v):
    m = ADAM_B1 * m + (1.0 - ADAM_B1) * g
    v = ADAM_B2 * v + (1.0 - ADAM_B2) * (g * g)
    m_hat = m / (1.0 - ADAM_B1 ** ADAM_STEP)
    v_hat = v / (1.0 - ADAM_B2 ** ADAM_STEP)
    delta = -ADAM_LR * (m_hat / (jnp.sqrt(v_hat) + ADAM_EPS) + ADAM_WD * w)
    return delta, m, v


def _adam_big(w, m, v, own32, recv16, me):
    r, c = w.shape
    tr = r if r <= 512 else (512 if r % 512 == 0 else 352)
    assert r % tr == 0

    def body(me_ref, w_ref, m_ref, v_ref, s_ref, r_ref, g_out, d_out, m_out, v_out):
        g = s_ref[...]
        for k in range(1, N_DEV):
            g = g + r_ref[jnp.bitwise_xor(me_ref[0], k)].astype(F32)
        d, mn, vn = _adam_math(w_ref[...], g, m_ref[...], v_ref[...])
        g_out[...] = g
        d_out[...] = d
        m_out[...] = mn
        v_out[...] = vn

    blk = pl.BlockSpec((tr, c), lambda i, s: (i, 0))
    grid_spec = pltpu.PrefetchScalarGridSpec(
        num_scalar_prefetch=1, grid=(r // tr,),
        in_specs=[blk, blk, blk, blk, pl.BlockSpec((N_DEV, tr, c), lambda i, s: (0, i, 0))],
        out_specs=[blk, blk, blk, blk])
    return pl.pallas_call(
        body, name="adamw_big", grid_spec=grid_spec,
        out_shape=[jax.ShapeDtypeStruct((r, c), F32)] * 4,
        compiler_params=_cparams(("arbitrary",)),
    )(me, w, m, v, own32, recv16)


def _adam_small(w, m, v, gathered):
    def body(w_ref, m_ref, v_ref, p_ref, g_out, d_out, m_out, v_out):
        g = p_ref[0]
        for dev in range(1, N_DEV):
            g = g + p_ref[dev]
        d, mn, vn = _adam_math(w_ref[...], g, m_ref[...], v_ref[...])
        g_out[...] = g
        d_out[...] = d
        m_out[...] = mn
        v_out[...] = vn

    return pl.pallas_call(
        body, name="adamw_small",
        out_shape=[jax.ShapeDtypeStruct(w.shape, F32)] * 4,
        compiler_params=_cparams(),
    )(w, m, v, gathered)


def _pick(n, prefer):
    for t in prefer:
        if n % t == 0:
            return t
    return n


def _ffn_fwd(x, g, wg, wu, wd, name, exchange=None):
    T, D = x.shape
    F = wg.shape[1]
    tm = _pick(T, (512, 256, 128))
    fc = _pick(F, (256, 128))

    def body(x_ref, g_ref, wg_ref, wu_ref, wd_ref, h_ref, n_ref, gate_ref, up_ref):
        xv = x_ref[...]
        n, _ = _rms_fwd(xv, g_ref[...])
        nb = n.astype(BF16)
        n_ref[...] = nb
        acc = jnp.zeros((tm, D), F32)
        for f0 in range(0, F, fc):
            gate = _dot(nb, wg_ref[:, f0:f0 + fc])
            up = _dot(nb, wu_ref[:, f0:f0 + fc])
            gate_ref[:, f0:f0 + fc] = gate.astype(BF16)
            up_ref[:, f0:f0 + fc] = up.astype(BF16)
            act = gate * (1.0 / (1.0 + jnp.exp(-gate))) * up
            acc = acc + _dot(act.astype(BF16), wd_ref[f0:f0 + fc, :])
        h_ref[...] = xv + 0.5 * acc

    return _pcall(
        body, name=name, grid=(T // tm,),
        in_specs=[_row_spec(tm, D), _const_spec((1, D)), _const_spec((D, F)), _const_spec((D, F)), _const_spec((F, D))],
        out_specs=[_row_spec(tm, D), _row_spec(tm, D), _row_spec(tm, F), _row_spec(tm, F)],
        out_shape=[jax.ShapeDtypeStruct((T, D), F32), jax.ShapeDtypeStruct((T, D), BF16),
                   jax.ShapeDtypeStruct((T, F), BF16), jax.ShapeDtypeStruct((T, F), BF16)],
        args=(x, g, wg, wu, wd), exchange=exchange)


def _ffn_bwd(x, g, gate, up, dh, wg, wu, wd, name, exchange=None):
    T, D = x.shape
    F = wg.shape[1]
    tm = _pick(T, (512, 256, 128))
    halves = 2
    Fh = F // halves
    assert Fh * halves == F and Fh % 128 == 0
    chunks = [(f0, min(f0 + 512, Fh)) for f0 in range(0, Fh, 512)]

    def body(x_ref, g_ref, gate_ref, up_ref, dh_ref, wg_ref, wu_ref, wd_ref, dx_ref, dg_ref, act_ref, dgate_ref, dup_ref, dn_ref):
        i, h = pl.program_id(0), pl.program_id(1)
        dhv = dh_ref[...]
        dho = (0.5 * dhv).astype(BF16)
        dn = jnp.zeros((tm, D), F32)
        for f0, f1 in chunks:
            d_act = _dot_nt(dho, wd_ref[f0:f1, :])
            gt = gate_ref[:, f0:f1].astype(F32)
            u = up_ref[:, f0:f1].astype(F32)
            sg = 1.0 / (1.0 + jnp.exp(-gt))
            silu = gt * sg
            act_ref[:, f0:f1] = (silu * u).astype(BF16)
            d_up = (d_act * silu).astype(BF16)
            d_gate = (d_act * u * (sg + silu * (1.0 - sg))).astype(BF16)
            dup_ref[:, f0:f1] = d_up
            dgate_ref[:, f0:f1] = d_gate
            dn = dn + _dot_nt(d_gate, wg_ref[:, f0:f1]) + _dot_nt(d_up, wu_ref[:, f0:f1])

        @pl.when(h == 0)
        def _():
            dn_ref[...] = dn

        @pl.when(h > 0)
        def _():
            dn_ref[...] += dn

        @pl.when((i == 0) & (h == 0))
        def _():
            dg_ref[...] = jnp.zeros_like(dg_ref)

        @pl.when(h == halves - 1)
        def _():
            xv, gv = x_ref[...], g_ref[...]
            _, r = _rms_fwd(xv, gv)
            dxn, dgp = _rms_bwd(xv, gv, r, dn_ref[...])
            dx_ref[...] = dhv + dxn
            dg_ref[...] += dgp

    tile = pl.BlockSpec((tm, D), lambda i, h: (i, 0))
    wide = pl.BlockSpec((tm, Fh), lambda i, h: (i, h))
    return _pcall(
        body, name=name, grid=(T // tm, halves),
        in_specs=[tile, _const_spec((1, D)), wide, wide, tile,
                  pl.BlockSpec((D, Fh), lambda i, h: (0, h)), pl.BlockSpec((D, Fh), lambda i, h: (0, h)),
                  pl.BlockSpec((Fh, D), lambda i, h: (h, 0))],
        out_specs=[tile, _acc_spec((1, D)), wide, wide, wide],
        out_shape=[jax.ShapeDtypeStruct((T, D), F32), jax.ShapeDtypeStruct((1, D), F32),
                   jax.ShapeDtypeStruct((T, F), BF16), jax.ShapeDtypeStruct((T, F), BF16), jax.ShapeDtypeStruct((T, F), BF16)],
        scratch_shapes=[pltpu.VMEM((tm, D), F32)],
        args=(x, g, gate, up, dh, wg, wu, wd), exchange=exchange)


def _matmul_tn(a, b, name, scale=None, exchange=None):
    T, K = a.shape
    N = b.shape[1]
    tk = _pick(K, (1024, 1408, 512, 256, 128))
    tn = _pick(N, (1024, 1408, 1280, 768, 512, 256, 128))
    tt = _pick(T, (2048, 1024, 512, 256, 128))
    n_t = T // tt

    def body(a_ref, b_ref, o_ref):
        @pl.when(pl.program_id(2) == 0)
        def _():
            o_ref[...] = jnp.zeros_like(o_ref)

        o_ref[...] += _dot_tn(a_ref[...].astype(BF16), b_ref[...].astype(BF16))

        if scale is not None:
            @pl.when(pl.program_id(2) == n_t - 1)
            def _():
                o_ref[...] = o_ref[...] * scale

    outs, got = _pcall(
        body, name=name, grid=(K // tk, N // tn, n_t),
        in_specs=[pl.BlockSpec((tt, tk), lambda i, j, t: (t, i)), pl.BlockSpec((tt, tn), lambda i, j, t: (t, j))],
        out_specs=[pl.BlockSpec((tk, tn), lambda i, j, t: (i, j))],
        out_shape=[jax.ShapeDtypeStruct((K, N), F32)],
        args=(a, b), exchange=exchange)
    return outs[0], got


def _rope_spec(tm, S):
    per = S // tm
    return pl.BlockSpec((tm, HEAD_BLOCK), lambda i: (i % per, 0))


def _proj_fwd(h1, g_mix, w_in, g_q, wq, g_kv, wkv, rope_c, rope_s, S):
    T, D = h1.shape
    P = w_in.shape[1]
    tm = _pick(S, (512, 256, 128))
    QW = MLA_HEADS * HEAD_BLOCK
    VW = MLA_HEADS * V_HEAD
    SQ = SWA_HEADS * SWA_HEAD_DIM
    SK = SWA_KV_HEADS * SWA_HEAD_DIM
    o_cq, o_ckv, o_qs, o_ks, o_vs, o_kpe = 0, Q_LORA, Q_LORA + KV_LORA, Q_LORA + KV_LORA + SQ, Q_LORA + KV_LORA + SQ + SK, Q_LORA + KV_LORA + SQ + 2 * SK
    assert P == o_kpe + HEAD_BLOCK

    def body(h_ref, gm_ref, win_ref, gq_ref, wq_ref, gkv_ref, wkv_ref, c_ref, s_ref,
             u_ref, cq_ref, ckv_ref, cqn_ref, ckvn_ref, q_ref, k_ref, v_ref, qs_ref, ks_ref, vs_ref):
        u, _ = _rms_fwd(h_ref[...], gm_ref[...])
        ub = u.astype(BF16)
        u_ref[...] = ub
        proj = _dot(ub, win_ref[...])
        c_q = proj[:, o_cq:o_ckv]
        c_kv = proj[:, o_ckv:o_qs]
        cq_ref[...] = c_q
        ckv_ref[...] = c_kv
        qs_ref[...] = proj[:, o_qs:o_ks].astype(BF16)
        ks_ref[...] = proj[:, o_ks:o_vs].astype(BF16)
        vs_ref[...] = proj[:, o_vs:o_kpe].astype(BF16)
        cb, sb = c_ref[...], s_ref[...]
        kpe = proj[:, o_kpe:P]
        kpe = kpe * cb + _swap16(kpe) * sb
        cqn, _ = _rms_fwd(c_q, gq_ref[...])
        cqn = cqn.astype(BF16)
        cqn_ref[...] = cqn
        q = _dot(cqn, wq_ref[...])
        q = q * jnp.tile(cb, (1, MLA_HEADS)) + _swap16(q) * jnp.tile(sb, (1, MLA_HEADS))
        q_ref[...] = q.astype(BF16)
        ckvn, _ = _rms_fwd(c_kv, gkv_ref[...])
        ckvn = ckvn.astype(BF16)
        ckvn_ref[...] = ckvn
        kv = _dot(ckvn, wkv_ref[...])
        k_ref[...] = (kv[:, :QW] + jnp.tile(kpe, (1, MLA_HEADS))).astype(BF16)
        v_ref[...] = kv[:, QW:].astype(BF16)

    widths = [(D, BF16), (Q_LORA, F32), (KV_LORA, F32), (Q_LORA, BF16), (KV_LORA, BF16), (QW, BF16), (QW, BF16), (VW, BF16),
              (SQ, BF16), (SK, BF16), (SK, BF16)]
    return pl.pallas_call(
        body, name="proj_fwd", grid=(T // tm,),
        in_specs=[_row_spec(tm, D), _const_spec((1, D)), _const_spec((D, P)), _const_spec((1, Q_LORA)),
                  _const_spec((Q_LORA, QW)), _const_spec((1, KV_LORA)), _const_spec((KV_LORA, QW + VW)),
                  _rope_spec(tm, S), _rope_spec(tm, S)],
        out_specs=[_row_spec(tm, w) for w, _ in widths],
        out_shape=[jax.ShapeDtypeStruct((T, w), dt) for w, dt in widths],
        compiler_params=_cparams(("arbitrary",)),
    )(h1, g_mix, w_in, g_q, wq, g_kv, wkv, rope_c, rope_s)


def _proj_bwd(dq, dk, dv, dqs, dks, dvs, c_q, c_kv, h1, dh2, g_mix, w_in, g_q, wq, g_kv, wkv, rope_c, rope_s, S):
    T, D = h1.shape
    P = w_in.shape[1]
    tm = _pick(S, (256, 128))
    QW = MLA_HEADS * HEAD_BLOCK
    VW = MLA_HEADS * V_HEAD
    SQ = SWA_HEADS * SWA_HEAD_DIM
    SK = SWA_KV_HEADS * SWA_HEAD_DIM

    def body(dq_ref, dk_ref, dv_ref, dqs_ref, dks_ref, dvs_ref, cq_ref, ckv_ref, h_ref, dh2_ref, gm_ref, win_ref, gq_ref,
             wq_ref, gkv_ref, wkv_ref, c_ref, s_ref, dh1_ref, dproj_ref, dql_ref, dkv_ref, dgm_ref, dgq_ref, dgkv_ref):
        cb, sb = c_ref[...], s_ref[...]
        dqv = dq_ref[...]
        dql = dqv * jnp.tile(cb, (1, MLA_HEADS)) + _swap16(dqv * jnp.tile(sb, (1, MLA_HEADS)))
        dql = dql.astype(BF16)
        dql_ref[...] = dql
        c_q = cq_ref[...]
        _, rq = _rms_fwd(c_q, gq_ref[...])
        d_cq, dgq = _rms_bwd(c_q, gq_ref[...], rq, _dot_nt(dql, wq_ref[...]))

        dkv_all = dk_ref[...]
        dkpe = dkv_all[:, 0:HEAD_BLOCK]
        for h in range(1, MLA_HEADS):
            dkpe = dkpe + dkv_all[:, h * HEAD_BLOCK:(h + 1) * HEAD_BLOCK]
        lane = lax.broadcasted_iota(jnp.int32, dkpe.shape, 1)
        dkpe = jnp.where((lane >= 64) & (lane < 96), dkpe, 0.0)
        dkpe = dkpe * cb + _swap16(dkpe * sb)
        dkvc = jnp.concatenate([dkv_all.astype(BF16), dv_ref[...].astype(BF16)], axis=1)
        dkv_ref[...] = dkvc
        c_kv = ckv_ref[...]
        _, rkv = _rms_fwd(c_kv, gkv_ref[...])
        d_ckv, dgkv = _rms_bwd(c_kv, gkv_ref[...], rkv, _dot_nt(dkvc, wkv_ref[...]))

        dproj = jnp.concatenate([d_cq.astype(BF16), d_ckv.astype(BF16), dqs_ref[...].astype(BF16),
                                 dks_ref[...].astype(BF16), dvs_ref[...].astype(BF16), dkpe.astype(BF16)], axis=1)
        dproj_ref[...] = dproj
        hv = h_ref[...]
        _, rm = _rms_fwd(hv, gm_ref[...])
        dxn, dgm = _rms_bwd(hv, gm_ref[...], rm, _dot_nt(dproj, win_ref[...]))
        dh1_ref[...] = dh2_ref[...] + dxn

        @pl.when(pl.program_id(0) == 0)
        def _():
            dgm_ref[...] = jnp.zeros_like(dgm_ref)
            dgq_ref[...] = jnp.zeros_like(dgq_ref)
            dgkv_ref[...] = jnp.zeros_like(dgkv_ref)

        dgm_ref[...] += dgm
        dgq_ref[...] += dgq
        dgkv_ref[...] += dgkv

    return pl.pallas_call(
        body, name="proj_bwd", grid=(T // tm,),
        in_specs=[_row_spec(tm, QW), _row_spec(tm, QW), _row_spec(tm, VW), _row_spec(tm, SQ), _row_spec(tm, SK), _row_spec(tm, SK),
                  _row_spec(tm, Q_LORA), _row_spec(tm, KV_LORA), _row_spec(tm, D), _row_spec(tm, D),
                  _const_spec((1, D)), _const_spec((D, P)), _const_spec((1, Q_LORA)), _const_spec((Q_LORA, QW)),
                  _const_spec((1, KV_LORA)), _const_spec((KV_LORA, QW + VW)), _rope_spec(tm, S), _rope_spec(tm, S)],
        out_specs=[_row_spec(tm, D), _row_spec(tm, P), _row_spec(tm, QW), _row_spec(tm, QW + VW),
                   _acc_spec((1, D)), _acc_spec((1, Q_LORA)), _acc_spec((1, KV_LORA))],
        out_shape=[jax.ShapeDtypeStruct((T, D), F32), jax.ShapeDtypeStruct((T, P), BF16), jax.ShapeDtypeStruct((T, QW), BF16),
                   jax.ShapeDtypeStruct((T, QW + VW), BF16), jax.ShapeDtypeStruct((1, D), F32),
                   jax.ShapeDtypeStruct((1, Q_LORA), F32), jax.ShapeDtypeStruct((1, KV_LORA), F32)],
        compiler_params=_cparams(("arbitrary",)),
    )(dq, dk, dv, dqs, dks, dvs, c_q, c_kv, h1, dh2, g_mix, w_in, g_q, wq, g_kv, wkv, rope_c, rope_s)


def _out_fwd(o_mla, o_swa, g_mla, g_swa, w_o, h1):
    T, D = h1.shape
    W = o_mla.shape[1]
    tm = _pick(T, (512, 256, 128))

    def body(om_ref, os_ref, gm_ref, gs_ref, wo_ref, h_ref, h2_ref, oc_ref):
        a, _ = _rms_fwd(om_ref[...], gm_ref[...])
        b, _ = _rms_fwd(os_ref[...], gs_ref[...])
        oc = jnp.concatenate([a.astype(BF16), b.astype(BF16)], axis=1)
        oc_ref[...] = oc
        h2_ref[...] = h_ref[...] + _dot(oc, wo_ref[...])

    return pl.pallas_call(
        body, name="out_fwd", grid=(T // tm,),
        in_specs=[_row_spec(tm, W), _row_spec(tm, W), _const_spec((1, W)), _const_spec((1, W)), _const_spec((2 * W, D)),
                  _row_spec(tm, D)],
        out_specs=[_row_spec(tm, D), _row_spec(tm, 2 * W)],
        out_shape=[jax.ShapeDtypeStruct((T, D), F32), jax.ShapeDtypeStruct((T, 2 * W), BF16)],
        compiler_params=_cparams(("arbitrary",)),
    )(o_mla, o_swa, g_mla, g_swa, w_o, h1)


def _out_bwd(dh2, o_mla, o_swa, g_mla, g_swa, w_o):
    T, D = dh2.shape
    W = o_mla.shape[1]
    tm = _pick(T, (512, 256, 128))

    def body(dh_ref, om_ref, os_ref, gm_ref, gs_ref, wo_ref, dom_ref, dos_ref, dgm_ref, dgs_ref):
        doc = _dot_nt(dh_ref[...].astype(BF16), wo_ref[...])
        om, osw = om_ref[...], os_ref[...]
        _, ra = _rms_fwd(om, gm_ref[...])
        _, rb = _rms_fwd(osw, gs_ref[...])
        da, dga = _rms_bwd(om, gm_ref[...], ra, doc[:, :W])
        db, dgb = _rms_bwd(osw, gs_ref[...], rb, doc[:, W:])
        dom_ref[...] = da
        dos_ref[...] = db

        @pl.when(pl.program_id(0) == 0)
        def _():
            dgm_ref[...] = jnp.zeros_like(dgm_ref)
            dgs_ref[...] = jnp.zeros_like(dgs_ref)

        dgm_ref[...] += dga
        dgs_ref[...] += dgb

    return pl.pallas_call(
        body, name="out_bwd", grid=(T // tm,),
        in_specs=[_row_spec(tm, D), _row_spec(tm, W), _row_spec(tm, W), _const_spec((1, W)), _const_spec((1, W)),
                  _const_spec((2 * W, D))],
        out_specs=[_row_spec(tm, W), _row_spec(tm, W), _acc_spec((1, W)), _acc_spec((1, W))],
        out_shape=[jax.ShapeDtypeStruct((T, W), F32), jax.ShapeDtypeStruct((T, W), F32),
                   jax.ShapeDtypeStruct((1, W), F32), jax.ShapeDtypeStruct((1, W), F32)],
        compiler_params=_cparams(("arbitrary",)),
    )(dh2, o_mla, o_swa, g_mla, g_swa, w_o)


def _loss_head(h3, target, g_final):
    T, D = h3.shape
    tm = _pick(T, (512, 256, 128))

    def body(h_ref, t_ref, g_ref, dh_ref, loss_ref, dg_ref):
        hv, gv = h_ref[...], g_ref[...]
        y, r = _rms_fwd(hv, gv)
        diff = y - t_ref[...]
        dxn, dgp = _rms_bwd(hv, gv, r, diff * (1.0 / D))
        dh_ref[...] = dxn
        part = 0.5 * jnp.sum(jnp.sum(diff * diff, axis=-1, keepdims=True) * (1.0 / D), axis=0, keepdims=True)

        @pl.when(pl.program_id(0) == 0)
        def _():
            loss_ref[...] = jnp.zeros_like(loss_ref)
            dg_ref[...] = jnp.zeros_like(dg_ref)

        loss_ref[...] += jnp.broadcast_to(part, loss_ref.shape)
        dg_ref[...] += dgp

    return pl.pallas_call(
        body, name="loss_head", grid=(T // tm,),
        in_specs=[_row_spec(tm, D), _row_spec(tm, D), _const_spec((1, D))],
        out_specs=[_row_spec(tm, D), _acc_spec((1, 128)), _acc_spec((1, D))],
        out_shape=[jax.ShapeDtypeStruct((T, D), F32), jax.ShapeDtypeStruct((1, 128), F32), jax.ShapeDtypeStruct((1, D), F32)],
        compiler_params=_cparams(("arbitrary",)),
    )(h3, target, g_final)


def _half_mask(shape, half):
    lane = lax.broadcasted_iota(jnp.int32, shape, 1)
    return (lane < 64) if half == 0 else (lane >= 64)


def _mla_fwd(q, k, v, B, S, exchange=None):
    T = B * S
    tq = _pick(S, MLA_Q_TILES)
    nq = S // tq
    pairs = MLA_HEADS // 2

    def body(q_ref, k_ref, v_ref, o_ref, lse_ref):
        causal = lax.broadcasted_iota(jnp.int32, (tq, tq), 1) <= lax.broadcasted_iota(jnp.int32, (tq, tq), 0)
        low = _half_mask((tq, 2 * V_HEAD), 0)
        for qi in range(nq):
            rows = slice(qi * tq, (qi + 1) * tq)
            past = slice(0, qi * tq)
            outs, lses = [], []
            for half in range(2):
                lanes = slice(half * HEAD_BLOCK, (half + 1) * HEAD_BLOCK)
                qv = q_ref[rows, lanes]
                s_d = jnp.where(causal, _dot_nt(qv, k_ref[rows, lanes]) * MLA_SCALE, NEG)
                m = jnp.max(s_d, axis=-1, keepdims=True)
                if qi:
                    s_p = _dot_nt(qv, k_ref[past, lanes]) * MLA_SCALE
                    m = jnp.maximum(m, jnp.max(s_p, axis=-1, keepdims=True))
                p_d = jnp.exp(s_d - m)
                l = jnp.sum(p_d, axis=-1, keepdims=True)
                acc = _dot(p_d.astype(BF16), v_ref[rows, :])
                if qi:
                    p_p = jnp.exp(s_p - m)
                    l = l + jnp.sum(p_p, axis=-1, keepdims=True)
                    acc = acc + _dot(p_p.astype(BF16), v_ref[past, :])
                outs.append(acc * (1.0 / l))
                lses.append(jnp.broadcast_to(m + jnp.log(l), (tq, 2 * V_HEAD)))
            o_ref[rows, :] = jnp.where(low, outs[0], outs[1])
            lse_ref[rows, :] = jnp.where(low, lses[0], lses[1])

    blk2 = pl.BlockSpec((S, 2 * HEAD_BLOCK), lambda b, j: (b, j))
    blk1 = pl.BlockSpec((S, 2 * V_HEAD), lambda b, j: (b, j))
    return _pcall(
        body, name="mla_fwd", grid=(B, pairs),
        in_specs=[blk2, blk2, blk1], out_specs=[blk1, blk1],
        out_shape=[jax.ShapeDtypeStruct((T, pairs * 2 * V_HEAD), F32)] * 2,
        args=(q, k, v), exchange=exchange)


def _mla_bwd(q, k, v, do, o, lse, B, S, exchange=None):
    T = B * S
    tq = _pick(S, MLA_Q_TILES)
    nq = S // tq
    pairs = MLA_HEADS // 2

    def body(q_ref, k_ref, v_ref, do_ref, o_ref, lse_ref, dq_ref, dk_ref, dv_ref, dk_acc):
        causal = lax.broadcasted_iota(jnp.int32, (tq, tq), 1) <= lax.broadcasted_iota(jnp.int32, (tq, tq), 0)
        for half in range(2):
            lanes = slice(half * HEAD_BLOCK, (half + 1) * HEAD_BLOCK)
            col = half * V_HEAD
            own = _half_mask((tq, 2 * V_HEAD), half)
            for qi in reversed(range(nq)):
                rows = slice(qi * tq, (qi + 1) * tq)
                past = slice(0, qi * tq)
                first = qi == nq - 1
                qv = q_ref[rows, lanes]
                dov = jnp.where(own, do_ref[rows, :], 0.0)
                dsum = jnp.sum(dov * o_ref[rows, :], axis=-1, keepdims=True)
                dob = dov.astype(BF16)
                lse_col = lse_ref[rows, col:col + 1]

                def grads(keys, mask):
                    kv, vv = k_ref[keys, lanes], v_ref[keys, :]
                    p = jnp.exp(_dot_nt(qv, kv) * MLA_SCALE - lse_col)
                    if mask is not None:
                        p = jnp.where(mask, p, 0.0)
                    ds = (p * (_dot_nt(dob, vv) - dsum) * MLA_SCALE).astype(BF16)
                    dk, dv = _dot_tn(ds, qv), _dot_tn(p.astype(BF16), dob)
                    if first:
                        dk_acc[keys, :] = dk
                    else:
                        dk_acc[keys, :] += dk
                    if first and half == 0:
                        dv_ref[keys, :] = dv
                    else:
                        dv_ref[keys, :] += dv
                    return _dot(ds, kv)

                dq = grads(rows, causal)
                if qi:
                    dq = dq + grads(past, None)
                dq_ref[rows, lanes] = dq
            dk_ref[:, lanes] = dk_acc[...]

    blk2 = pl.BlockSpec((S, 2 * HEAD_BLOCK), lambda b, j: (b, j))
    blk1 = pl.BlockSpec((S, 2 * V_HEAD), lambda b, j: (b, j))
    return _pcall(
        body, name="mla_bwd", grid=(B, pairs),
        in_specs=[blk2, blk2, blk1, blk1, blk1, blk1], out_specs=[blk2, blk2, blk1],
        out_shape=[jax.ShapeDtypeStruct((T, MLA_HEADS * HEAD_BLOCK), F32), jax.ShapeDtypeStruct((T, MLA_HEADS * HEAD_BLOCK), F32),
                   jax.ShapeDtypeStruct((T, pairs * 2 * V_HEAD), F32)],
        scratch_shapes=[pltpu.VMEM((S, HEAD_BLOCK), F32)],
        args=(q, k, v, do, o, lse), exchange=exchange)


SWA_GROUP = SWA_HEADS // SWA_KV_HEADS
SWA_ROWS = SWA_GROUP * WINDOW


SWA_UNROLL = 4


def _loop_unrolled(n, unroll, body):
    u = next(c for c in (unroll, 2, 1) if n % c == 0)

    def trip(t, carry):
        for i in range(u):
            body(t * u + i, 0)
        return carry

    lax.fori_loop(0, n // u, trip, 0)


def _swa_stack(ref, r0):
    return jnp.concatenate([ref[pl.ds(r0, WINDOW), j * WINDOW:(j + 1) * WINDOW] for j in range(SWA_GROUP)], axis=0)


def _swa_unstack(low, high):
    sel = _half_mask((WINDOW, WINDOW), 0)
    return jnp.concatenate([jnp.where(sel, low[j * WINDOW:(j + 1) * WINDOW], high[j * WINDOW:(j + 1) * WINDOW])
                            for j in range(SWA_GROUP)], axis=1)


def _swa_sink_col(sink_ref, g):
    return jnp.concatenate([jnp.broadcast_to(sink_ref[g * SWA_GROUP + j:g * SWA_GROUP + j + 1, 0:1], (WINDOW, 1))
                            for j in range(SWA_GROUP)], axis=0)


def _swa_probs(qm, kcat, bias_g, sink, n):
    qi = lax.broadcasted_iota(jnp.int32, (SWA_ROWS, 2 * WINDOW), 0) % WINDOW
    col = lax.broadcasted_iota(jnp.int32, (SWA_ROWS, 2 * WINDOW), 1)
    valid = (col > qi) & (col <= qi + WINDOW) & ((col >= WINDOW) | (n > 0))
    s = jnp.where(valid, _dot_nt(qm, kcat) * SWA_SCALE + bias_g, NEG)
    m = jnp.maximum(jnp.max(s, axis=-1, keepdims=True), sink)
    e = jnp.exp(s - m)
    e_s = jnp.exp(sink - m)
    inv = 1.0 / (jnp.sum(e, axis=-1, keepdims=True) + e_s)
    return e * inv, e_s * inv


def _swa_fwd(q, k, v, bias, sinks, B, S):
    T = B * S
    W = WINDOW
    nb = S // W

    def body(q_ref, k_ref, v_ref, bias_ref, sink_ref, o_ref):
        sink_cols = [_swa_sink_col(sink_ref, g) for g in range(SWA_KV_HEADS)]

        def block(n, _):
            r0 = pl.multiple_of(n * W, W)
            p0 = pl.multiple_of(jnp.maximum(n - 1, 0) * W, W)
            kcat = jnp.concatenate([k_ref[pl.ds(p0, W), :], k_ref[pl.ds(r0, W), :]], axis=0)
            vcat = jnp.concatenate([v_ref[pl.ds(p0, W), :], v_ref[pl.ds(r0, W), :]], axis=0)
            qcat = _swa_stack(q_ref, r0)
            outs = []
            for g in range(SWA_KV_HEADS):
                qm = jnp.where(_half_mask(qcat.shape, g), qcat, jnp.zeros_like(qcat))
                p, _ = _swa_probs(qm, kcat, bias_ref[g], sink_cols[g], n)
                outs.append(_dot(p.astype(BF16), vcat))
            o_ref[pl.ds(r0, W), :] = _swa_unstack(outs[0], outs[1])
            return 0

        _loop_unrolled(nb, SWA_UNROLL, block)

    SQ, SK = SWA_HEADS * SWA_HEAD_DIM, SWA_KV_HEADS * SWA_HEAD_DIM
    return pl.pallas_call(
        body, name="swa_fwd", grid=(B,),
        in_specs=[pl.BlockSpec((S, SQ), lambda b: (b, 0)), pl.BlockSpec((S, SK), lambda b: (b, 0)),
                  pl.BlockSpec((S, SK), lambda b: (b, 0)), _const_spec((SWA_KV_HEADS, SWA_ROWS, 2 * W)),
                  _const_spec((SWA_HEADS, 128))],
        out_specs=pl.BlockSpec((S, SQ), lambda b: (b, 0)),
        out_shape=jax.ShapeDtypeStruct((T, SQ), F32),
        compiler_params=_cparams(("arbitrary",)),
    )(q, k, v, bias, sinks)


def _swa_bwd(q, k, v, do, bias, sinks, B, S):
    T = B * S
    W = WINDOW
    nb = S // W

    def body(q_ref, k_ref, v_ref, do_ref, bias_ref, sink_ref, dq_ref, dk_ref, dv_ref, dbias_ref, dsink_ref):
        @pl.when(pl.program_id(0) == 0)
        def _():
            dbias_ref[...] = jnp.zeros_like(dbias_ref)
            dsink_ref[...] = jnp.zeros_like(dsink_ref)

        dk_ref[...] = jnp.zeros_like(dk_ref)
        dv_ref[...] = jnp.zeros_like(dv_ref)
        sink_cols = [_swa_sink_col(sink_ref, g) for g in range(SWA_KV_HEADS)]

        def block(n, _):
            r0 = pl.multiple_of(n * W, W)
            p0 = pl.multiple_of(jnp.maximum(n - 1, 0) * W, W)
            kcat = jnp.concatenate([k_ref[pl.ds(p0, W), :], k_ref[pl.ds(r0, W), :]], axis=0)
            vcat = jnp.concatenate([v_ref[pl.ds(p0, W), :], v_ref[pl.ds(r0, W), :]], axis=0)
            qcat = _swa_stack(q_ref, r0)
            docat = _swa_stack(do_ref, r0)
            dk = jnp.zeros((2 * W, W), F32)
            dv = jnp.zeros((2 * W, W), F32)
            dqs = []
            for g in range(SWA_KV_HEADS):
                own = _half_mask(qcat.shape, g)
                qm = jnp.where(own, qcat, jnp.zeros_like(qcat))
                dom = jnp.where(own, docat, 0.0).astype(BF16)
                p, p_s = _swa_probs(qm, kcat, bias_ref[g], sink_cols[g], n)
                dp = _dot_nt(dom, vcat)
                dsum = jnp.sum(p * dp, axis=-1, keepdims=True)
                ds = p * (dp - dsum)
                dbias_ref[g] += ds
                dsink_ref[g] += jnp.broadcast_to(-p_s * dsum, (SWA_ROWS, 128))
                dsb = (ds * SWA_SCALE).astype(BF16)
                dqs.append(_dot(dsb, kcat))
                dk = dk + _dot_tn(dsb, qm)
                dv = dv + _dot_tn(p.astype(BF16), dom)
            dq_ref[pl.ds(r0, W), :] = _swa_unstack(dqs[0], dqs[1])
            dk_ref[pl.ds(p0, W), :] += dk[:W]
            dv_ref[pl.ds(p0, W), :] += dv[:W]
            dk_ref[pl.ds(r0, W), :] += dk[W:]
            dv_ref[pl.ds(r0, W), :] += dv[W:]
            return 0

        _loop_unrolled(nb, SWA_UNROLL, block)

    SQ, SK = SWA_HEADS * SWA_HEAD_DIM, SWA_KV_HEADS * SWA_HEAD_DIM
    row = lambda w: pl.BlockSpec((S, w), lambda b: (b, 0))
    return pl.pallas_call(
        body, name="swa_bwd", grid=(B,),
        in_specs=[row(SQ), row(SK), row(SK), row(SQ), _const_spec((SWA_KV_HEADS, SWA_ROWS, 2 * W)), _const_spec((SWA_HEADS, 128))],
        out_specs=[row(SQ), row(SK), row(SK), _acc_spec((SWA_KV_HEADS, SWA_ROWS, 2 * W)), _acc_spec((SWA_KV_HEADS, SWA_ROWS, 128))],
        out_shape=[jax.ShapeDtypeStruct((T, SQ), F32), jax.ShapeDtypeStruct((T, SK), F32), jax.ShapeDtypeStruct((T, SK), F32),
                   jax.ShapeDtypeStruct((SWA_KV_HEADS, SWA_ROWS, 2 * W), F32), jax.ShapeDtypeStruct((SWA_KV_HEADS, SWA_ROWS, 128), F32)],
        compiler_params=_cparams(("arbitrary",)),
    )(q, k, v, do, bias, sinks)


def _bias_expand(rel_bias, bucket):
    W = WINDOW

    def body(rb_ref, bucket_ref, out_ref):
        bk = bucket_ref[...]
        for h in range(SWA_HEADS):
            def add(b, acc):
                return jnp.where(bk == b, rb_ref[b, h], acc)

            out_ref[h] = lax.fori_loop(0, REL_BUCKETS, add, jnp.zeros((W, 2 * W), F32))

    return pl.pallas_call(
        body, name="bias_expand",
        in_specs=[pl.BlockSpec(memory_space=pltpu.SMEM), pl.BlockSpec(memory_space=pltpu.VMEM)],
        out_specs=pl.BlockSpec(memory_space=pltpu.VMEM),
        out_shape=jax.ShapeDtypeStruct((SWA_HEADS, W, 2 * W), F32),
        compiler_params=_cparams(),
    )(rel_bias, bucket)


def _bias_reduce(dbias, dsink_rows, bucket):
    def body(db_ref, ds_ref, bucket_ref, out_ref, sink_out_ref):
        for h in range(SWA_HEADS):
            sink_out_ref[h:h + 1, :] = jnp.sum(ds_ref[h], axis=0, keepdims=True)
        bk = bucket_ref[...]
        row = lax.broadcasted_iota(jnp.int32, (REL_BUCKETS, 128), 0)
        lane = lax.broadcasted_iota(jnp.int32, (REL_BUCKETS, 128), 1)
        acc = jnp.zeros((REL_BUCKETS, 128), F32)
        for h in range(SWA_HEADS):
            dbh = db_ref[h]

            def add(b, acc):
                part = jnp.sum(jnp.sum(jnp.where(bk == b, dbh, 0.0), axis=-1, keepdims=True), axis=0, keepdims=True)
                return jnp.where((row == b) & (lane == h), part, acc)

            acc = lax.fori_loop(0, REL_BUCKETS, add, acc)
        out_ref[...] = acc

    return pl.pallas_call(
        body, name="bias_reduce",
        out_shape=[jax.ShapeDtypeStruct((REL_BUCKETS, 128), F32), jax.ShapeDtypeStruct((SWA_HEADS, 128), F32)],
        compiler_params=_cparams(),
    )(dbias, dsink_rows, bucket)


def _t5_bucket(dist):
    n = jnp.maximum(dist, 0)
    max_exact = REL_BUCKETS // 2
    nf = jnp.maximum(n, 1).astype(F32)
    large = max_exact + (jnp.log(nf / max_exact) / math.log(REL_MAX_DIST / max_exact) * (REL_BUCKETS - max_exact)).astype(jnp.int32)
    large = jnp.minimum(large, REL_BUCKETS - 1)
    return jnp.where(n < max_exact, n, large)


def _pair_heads(a, axis, inverse=False):
    shp = a.shape
    a = a.reshape(shp[:axis] + ((4, 2, 64) if inverse else (2, 4, 64)) + shp[axis + 1:])
    a = jnp.swapaxes(a, axis, axis + 1)
    return a.reshape(shp)


def _col_full(g):
    return jnp.transpose(g, (1, 0, 2)).reshape(g.shape[1], N_DEV * g.shape[2])


def _row_full(g):
    return g.reshape(N_DEV * g.shape[1], g.shape[2])


def _owner_blocks(w, by_col, me):
    if by_col:
        r, c = w.shape[0], w.shape[1] // N_DEV
        own = lax.dynamic_slice(w, (0, me * c), (r, c))
        blocks = jnp.transpose(w.astype(BF16).reshape(r, N_DEV, c), (1, 0, 2))
    else:
        r, c = w.shape[0] // N_DEV, w.shape[1]
        own = lax.dynamic_slice(w, (me * r, 0), (r, c))
        blocks = w.astype(BF16).reshape(N_DEV, r, c)
    return own, blocks


_IN_WIDTHS = (Q_LORA, KV_LORA, QK_ROPE, SWA_HEADS * SWA_HEAD_DIM, SWA_KV_HEADS * SWA_HEAD_DIM, SWA_KV_HEADS * SWA_HEAD_DIM)


def _split_in(w):
    parts, o = [], 0
    for wd in _IN_WIDTHS:
        parts.append(w[:, o:o + wd])
        o += wd
    return parts


def _win_to_kernel(w):
    cq, ckv, kpe, qs, ks, vs = _split_in(w)
    kpe = jnp.pad(kpe, ((0, 0), (64, 32)))
    return jnp.concatenate([cq, ckv, _pair_heads(qs, 1), ks, vs, kpe], axis=1)


def _win_from_kernel(dw):
    o = [0, Q_LORA, Q_LORA + KV_LORA, Q_LORA + KV_LORA + 512, Q_LORA + KV_LORA + 640, Q_LORA + KV_LORA + 768]
    cq, ckv, qs, ks, vs, kpe = [dw[:, a:b] for a, b in zip(o, o[1:] + [dw.shape[1]])]
    return jnp.concatenate([cq, ckv, kpe[:, 64:96], _pair_heads(qs, 1, inverse=True), ks, vs], axis=1)


def _wq_to_kernel(w):
    return jnp.pad(w.reshape(Q_LORA, MLA_HEADS, QK_NOPE + QK_ROPE), ((0, 0), (0, 0), (0, 32))).reshape(Q_LORA, MLA_HEADS * HEAD_BLOCK)


def _wq_from_kernel(dw):
    return dw.reshape(Q_LORA, MLA_HEADS, HEAD_BLOCK)[:, :, :QK_NOPE + QK_ROPE].reshape(Q_LORA, -1)


def _wkv_to_kernel(w):
    w3 = w.reshape(KV_LORA, MLA_HEADS, QK_NOPE + V_HEAD)
    kpart = jnp.pad(w3[:, :, :QK_NOPE], ((0, 0), (0, 0), (0, 64))).reshape(KV_LORA, MLA_HEADS * HEAD_BLOCK)
    return jnp.concatenate([kpart, w3[:, :, QK_NOPE:].reshape(KV_LORA, MLA_HEADS * V_HEAD)], axis=1)


def _wkv_from_kernel(dw):
    dk = dw[:, :MLA_HEADS * HEAD_BLOCK].reshape(KV_LORA, MLA_HEADS, HEAD_BLOCK)[:, :, :QK_NOPE]
    dv = dw[:, MLA_HEADS * HEAD_BLOCK:].reshape(KV_LORA, MLA_HEADS, V_HEAD)
    return jnp.concatenate([dk, dv], axis=2).reshape(KV_LORA, -1)


def _wo_to_kernel(w):
    return jnp.concatenate([w[:512], _pair_heads(w[512:], 0)], axis=0)


def _wo_from_kernel(dw):
    return jnp.concatenate([dw[:512], _pair_heads(dw[512:], 0, inverse=True)], axis=0)


_SMALL = (("g_ffn1", 1024), ("g_mix", 1024), ("g_q_a", 256), ("g_kv_a", 128), ("attn_sinks", 8), ("rel_bias", 256),
          ("g_out_mla", 512), ("g_out_swa", 512), ("g_ffn2", 1024), ("g_final", 1024), ("loss", 1))


def _pack_small(vals):
    rows = []
    for name, size in _SMALL:
        v = vals[name].reshape(-1).astype(F32)
        nrow = -(-size // 128)
        rows.append(jnp.pad(v, (0, nrow * 128 - size)).reshape(nrow, 128))
    packed = jnp.concatenate(rows, axis=0)
    return jnp.pad(packed, ((0, SMALL_ROWS - packed.shape[0]), (0, 0)))


def _unpack_small(packed, shapes):
    out, r = {}, 0
    for name, size in _SMALL:
        nrow = -(-size // 128)
        out[name] = packed[r:r + nrow].reshape(-1)[:size].reshape(shapes[name])
        r += nrow
    return out


class _Comm:
    def __init__(self, shards, by_col, me):
        self.shards, self.by_col, self.me = shards, by_col, me
        self.own32, self.recv16 = {}, {}

    def gather(self, keys):
        return [("gather", self.shards[k]) for k in keys]

    def gathered(self, keys, results, wts):
        for k, g in zip(keys, results):
            full = _col_full(g) if self.by_col[k] else _row_full(g)
            wts[k] = _TO_KERNEL[k](full) if k in _TO_KERNEL else full

    def scatter(self, keys, grads):
        jobs = []
        for k in keys:
            g = _FROM_KERNEL[k](grads[k]) if k in _FROM_KERNEL else grads[k]
            self.own32[k], blocks = _owner_blocks(g, self.by_col[k], self.me)
            jobs.append(("scatter", blocks))
        return jobs

    def scattered(self, keys, results):
        self.recv16.update(zip(keys, results))


def _step(x, target, gains, wts, comm):
    def jobs_in(keys):
        return comm.gather(keys) if comm else None

    def jobs_out(keys):
        return comm.scatter(keys, grads) if comm else None

    B, S, D = x.shape
    T = B * S
    x2 = x.reshape(T, D)
    t2 = target.reshape(T, D)

    pos = jnp.arange(S, dtype=F32)
    inv_freq = ROPE_THETA ** (-jnp.arange(0, QK_ROPE, 2, dtype=F32) / QK_ROPE)
    ang = pos[:, None] * inv_freq[None, :]
    cos, sin = jnp.cos(ang), jnp.sin(ang)
    ones, zeros = jnp.ones((S, 64), F32), jnp.zeros((S, 32), F32)
    rope_c = jnp.concatenate([ones, cos, cos, zeros], axis=1)
    rope_s = jnp.concatenate([0.0 * ones, -sin, sin, zeros], axis=1)

    qi = jnp.arange(WINDOW)[:, None]
    kj = jnp.arange(2 * WINDOW)[None, :]
    bucket = _t5_bucket(qi + WINDOW - kj).astype(jnp.int32)
    bias = _bias_expand(gains["rel_bias"], bucket).reshape(SWA_KV_HEADS, SWA_ROWS, 2 * WINDOW)
    sinks = jnp.broadcast_to(gains["attn_sinks"].reshape(SWA_HEADS, 1), (SWA_HEADS, 128))
    g_swa = _pair_heads(gains["g_out_swa"], 1)

    keys = ("in", "q", "kv", "o", "g2")
    (h1, n1, gate1, up1), got = _ffn_fwd(x2, gains["g_ffn1"], wts["g1"], wts["u1"], wts["d1"], "ffn1_fwd", jobs_in(keys))
    if comm:
        comm.gathered(keys, got, wts)
    (u, c_q, c_kv, cqn, ckvn, q, k, v, qs, ks, vs) = _proj_fwd(
        h1, gains["g_mix"], wts["in"], gains["g_q_a"], wts["q"], gains["g_kv_a"], wts["kv"], rope_c, rope_s, S)
    keys = ("u2", "d2")
    (o_mla, lse), got = _mla_fwd(q, k, v, B, S, jobs_in(keys))
    if comm:
        comm.gathered(keys, got, wts)
    o_swa = _swa_fwd(qs, ks, vs, bias, sinks, B, S)
    h2, oc = _out_fwd(o_mla, o_swa, gains["g_out_mla"], g_swa, wts["o"], h1)
    (h3, n2, gate2, up2), _ = _ffn_fwd(h2, gains["g_ffn2"], wts["g2"], wts["u2"], wts["d2"], "ffn2_fwd")
    dh3, loss, dg_final = _loss_head(h3, t2, gains["g_final"].reshape(1, D))

    grads = {}
    (dh2, dg_ffn2, act2, dgate2, dup2), _ = _ffn_bwd(h2, gains["g_ffn2"], gate2, up2, dh3, wts["g2"], wts["u2"], wts["d2"], "ffn2_bwd")
    grads["g2"], _ = _matmul_tn(n2, dgate2, "dw_gate2")
    grads["u2"], _ = _matmul_tn(n2, dup2, "dw_up2")
    grads["d2"], _ = _matmul_tn(act2, dh3, "dw_down2", 0.5)
    do_mla, do_swa, dg_mla, dg_swa = _out_bwd(dh2, o_mla, o_swa, gains["g_out_mla"], g_swa, wts["o"])
    grads["o"], _ = _matmul_tn(oc, dh2, "dw_o")
    keys = ("g2", "u2", "d2")
    (dq, dk, dv), got = _mla_bwd(q, k, v, do_mla, o_mla, lse, B, S, jobs_out(keys))
    if comm:
        comm.scattered(keys, got)
    dqs, dks, dvs, dbias, dsink = _swa_bwd(qs, ks, vs, do_swa, bias, sinks, B, S)
    dh1, dproj, dql, dkvc, dg_mix, dg_q, dg_kv = _proj_bwd(
        dq, dk, dv, dqs, dks, dvs, c_q, c_kv, h1, dh2, gains["g_mix"], wts["in"], gains["g_q_a"], wts["q"],
        gains["g_kv_a"], wts["kv"], rope_c, rope_s, S)
    grads["in"], _ = _matmul_tn(u, dproj, "dw_in")
    grads["q"], _ = _matmul_tn(cqn, dql, "dw_q")
    grads["kv"], _ = _matmul_tn(ckvn, dkvc, "dw_kv")
    keys = ("o", "in", "q", "kv")
    (dx, dg_ffn1, act1, dgate1, dup1), got = _ffn_bwd(
        x2, gains["g_ffn1"], gate1, up1, dh1, wts["g1"], wts["u1"], wts["d1"], "ffn1_bwd", jobs_out(keys))
    if comm:
        comm.scattered(keys, got)
    grads["g1"], _ = _matmul_tn(n1, dgate1, "dw_gate1")
    grads["u1"], got = _matmul_tn(n1, dup1, "dw_up1", exchange=jobs_out(("g1",)))
    if comm:
        comm.scattered(("g1",), got)
    grads["d1"], got = _matmul_tn(act1, dh1, "dw_down1", 0.5, exchange=jobs_out(("u1",)))
    if comm:
        comm.scattered(("u1",), got)
        comm.scattered(("d1",), _exchange(jobs_out(("d1",)), "grad_exchange_last"))

    d_rel_bias, d_sinks = _bias_reduce(dbias.reshape(SWA_HEADS, WINDOW, 2 * WINDOW), dsink.reshape(SWA_HEADS, WINDOW, 128), bucket)
    small = {
        "g_ffn1": dg_ffn1, "g_mix": dg_mix, "g_q_a": dg_q, "g_kv_a": dg_kv, "attn_sinks": d_sinks[:, 0],
        "rel_bias": d_rel_bias[:, :SWA_HEADS], "g_out_mla": dg_mla, "g_out_swa": _pair_heads(dg_swa, 1, inverse=True),
        "g_ffn2": dg_ffn2, "g_final": dg_final, "loss": loss[0, 0:1],
    }
    return dx.reshape(B, S, D), grads, small


_WEIGHTS = ("g_ffn1", "w_ffn1_gate", "w_ffn1_up", "w_ffn1_down", "g_mix", "w_in", "g_q_a", "w_q_b", "g_kv_a", "w_kv_b",
            "attn_sinks", "rel_bias", "g_out_mla", "g_out_swa", "w_o", "g_ffn2", "w_ffn2_gate", "w_ffn2_up", "w_ffn2_down",
            "g_final")
_BIG = (("w_ffn1_gate", "g1", True), ("w_ffn1_up", "u1", True), ("w_ffn1_down", "d1", False), ("w_in", "in", True),
        ("w_q_b", "q", True), ("w_kv_b", "kv", True), ("w_o", "o", False), ("w_ffn2_gate", "g2", True),
        ("w_ffn2_up", "u2", True), ("w_ffn2_down", "d2", False))
_TO_KERNEL = {"in": _win_to_kernel, "q": _wq_to_kernel, "kv": _wkv_to_kernel, "o": _wo_to_kernel}
_FROM_KERNEL = {"in": _win_from_kernel, "q": _wq_from_kernel, "kv": _wkv_from_kernel, "o": _wo_from_kernel}


def kernel(x, g_ffn1, w_ffn1_gate, w_ffn1_up, w_ffn1_down, g_mix, w_in, g_q_a, w_q_b, g_kv_a, w_kv_b, attn_sinks, rel_bias, g_out_mla, g_out_swa, w_o, g_ffn2, w_ffn2_gate, w_ffn2_up, w_ffn2_down, g_final, loss_target, m_g_ffn1, m_w_ffn1_gate, m_w_ffn1_up, m_w_ffn1_down, m_g_mix, m_w_in, m_g_q_a, m_w_q_b, m_g_kv_a, m_w_kv_b, m_attn_sinks, m_rel_bias, m_g_out_mla, m_g_out_swa, m_w_o, m_g_ffn2, m_w_ffn2_gate, m_w_ffn2_up, m_w_ffn2_down, m_g_final, v_g_ffn1, v_w_ffn1_gate, v_w_ffn1_up, v_w_ffn1_down, v_g_mix, v_w_in, v_g_q_a, v_w_q_b, v_g_kv_a, v_w_kv_b, v_attn_sinks, v_rel_bias, v_g_out_mla, v_g_out_swa, v_w_o, v_g_ffn2, v_w_ffn2_gate, v_w_ffn2_up, v_w_ffn2_down, v_g_final):
    w = dict(zip(_WEIGHTS, (g_ffn1, w_ffn1_gate, w_ffn1_up, w_ffn1_down, g_mix, w_in, g_q_a, w_q_b, g_kv_a, w_kv_b, attn_sinks,
                            rel_bias, g_out_mla, g_out_swa, w_o, g_ffn2, w_ffn2_gate, w_ffn2_up, w_ffn2_down, g_final)))
    m = dict(zip(_WEIGHTS, (m_g_ffn1, m_w_ffn1_gate, m_w_ffn1_up, m_w_ffn1_down, m_g_mix, m_w_in, m_g_q_a, m_w_q_b, m_g_kv_a,
                            m_w_kv_b, m_attn_sinks, m_rel_bias, m_g_out_mla, m_g_out_swa, m_w_o, m_g_ffn2, m_w_ffn2_gate,
                            m_w_ffn2_up, m_w_ffn2_down, m_g_final)))
    v = dict(zip(_WEIGHTS, (v_g_ffn1, v_w_ffn1_gate, v_w_ffn1_up, v_w_ffn1_down, v_g_mix, v_w_in, v_g_q_a, v_w_q_b, v_g_kv_a,
                            v_w_kv_b, v_attn_sinks, v_rel_bias, v_g_out_mla, v_g_out_swa, v_w_o, v_g_ffn2, v_w_ffn2_gate,
                            v_w_ffn2_up, v_w_ffn2_down, v_g_final)))
    me = 4 * lax.axis_index("x") + 2 * lax.axis_index("y") + lax.axis_index("c")

    shards = dict(zip([key for _, key, _ in _BIG], _cast_bf16([w[name][0] for name, _, _ in _BIG])))
    comm = _Comm(shards, {key: by_col for _, key, by_col in _BIG}, me)
    first = ("g1", "u1", "d1")
    wts = {}
    comm.gathered(first, _all_gather([shards[k] for k in first], "weights_all_gather"), wts)

    small_names = [n for n, _ in _SMALL if n != "loss"]
    gains = {n: (w[n] if n in ("rel_bias", "g_final") else w[n].reshape(1, -1)) for n in small_names}
    grad_x, _, small = _step(x, loss_target, gains, wts, comm)

    small_gathered = _all_gather([_pack_small(small)], "small_all_gather")[0]

    out_g, out_d, out_m, out_v = {}, {}, {}, {}
    me1 = me.astype(jnp.int32).reshape(1)
    for name, key, _ in _BIG:
        g, d, mn, vn = _adam_big(w[name][0], m[name][0], v[name][0], comm.own32[key], comm.recv16[key], me1)
        out_g[name], out_d[name], out_m[name], out_v[name] = g[None], d[None], mn[None], vn[None]
    shapes = {n: w[n].shape for n in small_names}
    shapes["loss"] = ()
    zero = jnp.zeros((), F32)
    ws, ms, vs = ({**{n: src[n] for n in small_names}, "loss": zero} for src in (w, m, v))
    sg, sd, sm, sv = _adam_small(_pack_small(ws), _pack_small(ms), _pack_small(vs), small_gathered)
    sg, sd, sm, sv = (_unpack_small(p, shapes) for p in (sg, sd, sm, sv))
    for n in small_names:
        out_g[n], out_d[n], out_m[n], out_v[n] = sg[n], sd[n], sm[n], sv[n]

    return (sg["loss"], grad_x, *[out_g[n] for n in _WEIGHTS], *[out_d[n] for n in _WEIGHTS],
            *[out_m[n] for n in _WEIGHTS], *[out_v[n] for n in _WEIGHTS])
```

```python
import functools
import math

import jax
import jax.numpy as jnp
from jax import lax
from jax.experimental import pallas as pl
from jax.experimental.pallas import tpu as pltpu

F32 = jnp.float32
BF16 = jnp.bfloat16
MESH = pl.DeviceIdType.MESH

EPS = 1e-6
NEG = -1e30
N_DEV = 8

MLA_HEADS = 8
Q_LORA = 256
KV_LORA = 128
QK_NOPE = 64
QK_ROPE = 32
V_HEAD = 64
ROPE_THETA = 10000.0
SWA_HEADS = 8
SWA_KV_HEADS = 2
SWA_HEAD_DIM = 64
WINDOW = 128
REL_BUCKETS = 32
REL_MAX_DIST = 128
HEAD_BLOCK = 128
MLA_Q_TILES = (512, 256, 128)
MLA_SCALE = (QK_NOPE + QK_ROPE) ** -0.5
SWA_SCALE = SWA_HEAD_DIM ** -0.5

ADAM_LR = 0.001
ADAM_B1 = 0.9
ADAM_B2 = 0.999
ADAM_EPS = 1e-08
ADAM_WD = 0.01
ADAM_STEP = 10

VMEM_LIMIT = 56 * 1024 * 1024


def _cparams(semantics=None):
    return pltpu.CompilerParams(dimension_semantics=semantics, vmem_limit_bytes=VMEM_LIMIT)


def _dot(a, b):
    return jnp.dot(a, b, preferred_element_type=F32)


def _dot_nt(a, b):
    return lax.dot_general(a, b, (((1,), (1,)), ((), ())), preferred_element_type=F32)


def _dot_tn(a, b):
    return lax.dot_general(a, b, (((0,), (0,)), ((), ())), preferred_element_type=F32)


def _rms_fwd(x, g):
    r = lax.rsqrt(jnp.mean(x * x, axis=-1, keepdims=True) + EPS)
    return x * r * g, r


def _rms_bwd(x, g, r, dy):
    xh = x * r
    dyg = dy * g
    dx = r * (dyg - xh * jnp.mean(dyg * xh, axis=-1, keepdims=True))
    return dx, jnp.sum(dy * xh, axis=0, keepdims=True)


def _swap16(x):
    n = x.shape[-1]
    up = pltpu.roll(x, n - 16, 1)
    down = pltpu.roll(x, 16, 1)
    lane = lax.broadcasted_iota(jnp.int32, x.shape, 1) % HEAD_BLOCK
    return jnp.where((lane >= 64) & (lane < 80), up, jnp.where((lane >= 80) & (lane < 96), down, 0.0))


def _const_spec(shape):
    return pl.BlockSpec(shape, lambda *_: (0,) * len(shape), pipeline_mode=pl.Buffered(1))


def _acc_spec(shape):
    return pl.BlockSpec(shape, lambda *_: (0,) * len(shape))


def _row_spec(tm, width):
    return pl.BlockSpec((tm, width), lambda i: (i, 0))


def _cast_bf16(arrays):
    n = len(arrays)

    def body(*refs):
        for src, dst in zip(refs[:n], refs[n:]):
            dst[...] = src[...].astype(BF16)

    return pl.pallas_call(
        body, name="cast_weights",
        out_shape=[jax.ShapeDtypeStruct(a.shape, BF16) for a in arrays],
        compiler_params=_cparams(),
    )(*arrays)


def _all_gather(arrays, name):
    n = len(arrays)

    def body(*refs):
        ins, outs = refs[:n], refs[n:2 * n]
        send_sems, recv_sems, local_sems = refs[2 * n:]
        x, y, c = lax.axis_index("x"), lax.axis_index("y"), lax.axis_index("c")
        me, sibling = (x, y, c), (x, y, 1 - c)
        chips = [(1 - x, y), (x, 1 - y), (1 - x, 1 - y)]

        def slot(a, dev):
            return outs[a].at[4 * dev[0] + 2 * dev[1] + dev[2]]

        def copy(a, k, block, to, src=None):
            return pltpu.make_async_remote_copy(
                src_ref=slot(a, block) if src is None else src, dst_ref=slot(a, block),
                send_sem=send_sems.at[a, k], recv_sem=recv_sems.at[a, k], device_id=to, device_id_type=MESH)

        mine = [pltpu.make_async_copy(ins[a], slot(a, me), local_sems.at[a]) for a in range(n)]
        for cp in mine:
            cp.start()
        first = []
        for a in range(n):
            first.append(copy(a, 0, me, sibling, src=ins[a]))
            first += [copy(a, 1 + j, me, (*chip, c), src=ins[a]) for j, chip in enumerate(chips)]
        for cp in first:
            cp.start()
        passed = []
        for j, chip in enumerate(chips):
            for a in range(n):
                copy(a, 1 + j, (*chip, c), me).wait_recv()
                cp = copy(a, 4 + j, (*chip, c), sibling)
                cp.start()
                passed.append(cp)
        for a in range(n):
            copy(a, 0, sibling, me).wait_recv()
            for j, chip in enumerate(chips):
                copy(a, 4 + j, (*chip, 1 - c), me).wait_recv()
        for cp in first + passed:
            cp.wait_send()
        for cp in mine:
            cp.wait()

    any_spec = pl.BlockSpec(memory_space=pl.ANY)
    return pl.pallas_call(
        body, name=name,
        out_shape=[jax.ShapeDtypeStruct((N_DEV,) + a.shape, a.dtype) for a in arrays],
        in_specs=[any_spec] * n, out_specs=[any_spec] * n,
        scratch_shapes=[pltpu.SemaphoreType.DMA((n, 7)), pltpu.SemaphoreType.DMA((n, 7)), pltpu.SemaphoreType.DMA((n,))],
    )(*arrays)


def _exchange_shapes(jobs):
    dsts = [jax.ShapeDtypeStruct(((N_DEV,) + a.shape) if kind == "gather" else a.shape, a.dtype) for kind, a in jobs]
    n = len(jobs)
    sems = [pltpu.SemaphoreType.DMA((n, N_DEV - 1)), pltpu.SemaphoreType.DMA((n, N_DEV - 1)), pltpu.SemaphoreType.DMA((n,))]
    return dsts, sems


def _exchange_copies(kinds, srcs, dsts, send_sems, recv_sems, local_sems):
    x, y, c = lax.axis_index("x"), lax.axis_index("y"), lax.axis_index("c")
    me = 4 * x + 2 * y + c
    remote, local = [], []
    for a, kind in enumerate(kinds):
        for k in range(1, N_DEV):
            peer = (1 - x if k & 4 else x, 1 - y if k & 2 else y, 1 - c if k & 1 else c)
            src = srcs[a] if kind == "gather" else srcs[a].at[4 * peer[0] + 2 * peer[1] + peer[2]]
            remote.append(pltpu.make_async_remote_copy(
                src_ref=src, dst_ref=dsts[a].at[me], send_sem=send_sems.at[a, k - 1], recv_sem=recv_sems.at[a, k - 1],
                device_id=peer, device_id_type=MESH))
        src = srcs[a] if kind == "gather" else srcs[a].at[me]
        local.append(pltpu.make_async_copy(src, dsts[a].at[me], local_sems.at[a]))
    return remote, local


def _exchange(jobs, name):
    n = len(jobs)
    kinds = [kind for kind, _ in jobs]
    dst_shapes, sems = _exchange_shapes(jobs)

    def body(*refs):
        remote, local = _exchange_copies(kinds, refs[:n], refs[n:2 * n], *refs[2 * n:])
        for cp in local + remote:
            cp.start()
        for cp in remote + local:
            cp.wait()

    any_spec = pl.BlockSpec(memory_space=pl.ANY)
    return pl.pallas_call(
        body, name=name, out_shape=dst_shapes, in_specs=[any_spec] * n, out_specs=[any_spec] * n, scratch_shapes=sems,
    )(*[a for _, a in jobs])


def _pcall(body, *, name, grid, in_specs, out_specs, out_shape, args, scratch_shapes=(), exchange=None):
    semantics = ("arbitrary",) * len(grid)
    if not exchange:
        outs = pl.pallas_call(body, name=name, grid=grid, in_specs=list(in_specs), out_specs=list(out_specs),
                              out_shape=list(out_shape), scratch_shapes=list(scratch_shapes),
                              compiler_params=_cparams(semantics))(*args)
        return list(outs), []
    kinds = [kind for kind, _ in exchange]
    n_in, n_out, n_job = len(in_specs), len(out_specs), len(exchange)
    dst_shapes, sems = _exchange_shapes(exchange)

    def with_exchange(*refs):
        ins, srcs = refs[:n_in], refs[n_in:n_in + n_job]
        outs, dsts = refs[n_in + n_job:n_in + n_job + n_out], refs[n_in + n_job + n_out:n_in + 2 * n_job + n_out]
        scratch, sem_refs = refs[n_in + 2 * n_job + n_out:-3], refs[-3:]
        first = functools.reduce(jnp.logical_and, [pl.program_id(d) == 0 for d in range(len(grid))])
        last = functools.reduce(jnp.logical_and, [pl.program_id(d) == grid[d] - 1 for d in range(len(grid))])

        @pl.when(first)
        def _():
            remote, local = _exchange_copies(kinds, srcs, dsts, *sem_refs)
            for cp in local + remote:
                cp.start()

        body(*ins, *outs, *scratch)

        @pl.when(last)
        def _():
            remote, local = _exchange_copies(kinds, srcs, dsts, *sem_refs)
            for cp in remote + local:
                cp.wait()

    any_spec = pl.BlockSpec(memory_space=pl.ANY)
    outs = pl.pallas_call(
        with_exchange, name=name, grid=grid, in_specs=list(in_specs) + [any_spec] * n_job,
        out_specs=list(out_specs) + [any_spec] * n_job, out_shape=list(out_shape) + dst_shapes,
        scratch_shapes=list(scratch_shapes) + sems, compiler_params=_cparams(semantics),
    )(*args, *[a for _, a in exchange])
    return list(outs[:n_out]), list(outs[n_out:])


def _adam_math(w, g, m, v):
    m = ADAM_B1 * m + (1.0 - ADAM_B1) * g
    v = ADAM_B2 * v + (1.0 - ADAM_B2) * (g * g)
    m_hat = m / (1.0 - ADAM_B1 ** ADAM_STEP)
    v_hat = v / (1.0 - ADAM_B2 ** ADAM_STEP)
    delta = -ADAM_LR * (m_hat / (jnp.sqrt(v_hat) + ADAM_EPS) + ADAM_WD * w)
    return delta, m, v


def _adam_big(w, m, v, own32, recv16, me):
    r, c = w.shape
    tr = r if r <= 512 else (512 if r % 512 == 0 else 352)
    assert r % tr == 0

    def body(me_ref, w_ref, m_ref, v_ref, s_ref, r_ref, g_out, d_out, m_out, v_out):
        g = s_ref[...]
        for k in range(1, N_DEV):
            g = g + r_ref[jnp.bitwise_xor(me_ref[0], k)].astype(F32)
        d, mn, vn = _adam_math(w_ref[...], g, m_ref[...], v_ref[...])
        g_out[...] = g
        d_out[...] = d
        m_out[...] = mn
        v_out[...] = vn

    blk = pl.BlockSpec((tr, c), lambda i, s: (i, 0))
    grid_spec = pltpu.PrefetchScalarGridSpec(
        num_scalar_prefetch=1, grid=(r // tr,),
        in_specs=[blk, blk, blk, blk, pl.BlockSpec((N_DEV, tr, c), lambda i, s: (0, i, 0))],
        out_specs=[blk, blk, blk, blk])
    return pl.pallas_call(
        body, name="adamw_big", grid_spec=grid_spec,
        out_shape=[jax.ShapeDtypeStruct((r, c), F32)] * 4,
        compiler_params=_cparams(("arbitrary",)),
    )(me, w, m, v, own32, recv16)


def _adam_small(ws, ms, vs, partials, loss_parts):
    n = len(ws)

    def body(*refs):
        w_refs, m_refs, v_refs, p_refs = (refs[i * n:(i + 1) * n] for i in range(4))
        l_ref, loss_out = refs[4 * n], refs[4 * n + 1]
        outs = refs[4 * n + 2:]

        def total(ref):
            s = ref[0]
            for dev in range(1, N_DEV):
                s = s + ref[dev]
            return s

        loss_out[...] = total(l_ref)
        for i in range(n):
            r, c = w_refs[i].shape
            g = total(p_refs[i])[:, :c]
            d, mn, vn = _adam_math(w_refs[i][...], g, m_refs[i][...], v_refs[i][...])
            for j, val in enumerate((g, d, mn, vn)):
                outs[4 * i + j][...] = val

    shapes = [jax.ShapeDtypeStruct((1, 128), F32)]
    for w in ws:
        shapes += [jax.ShapeDtypeStruct(w.shape, F32)] * 4
    outs = pl.pallas_call(body, name="adamw_small", out_shape=shapes, compiler_params=_cparams())(*ws, *ms, *vs, *partials, loss_parts)
    return outs[0], [outs[1 + 4 * i:5 + 4 * i] for i in range(n)]


def _pick(n, prefer):
    for t in prefer:
        if n % t == 0:
            return t
    return n


def _ffn_fwd(x, g, wg, wu, wd, name, exchange=None):
    T, D = x.shape
    F = wg.shape[1]
    tm = _pick(T, (512, 256, 128))
    fc = _pick(F, (256, 128))

    def body(x_ref, g_ref, wg_ref, wu_ref, wd_ref, h_ref, n_ref, gate_ref, up_ref, act_ref):
        xv = x_ref[...]
        n, _ = _rms_fwd(xv, g_ref[...])
        nb = n.astype(BF16)
        n_ref[...] = nb
        acc = jnp.zeros((tm, D), F32)
        for f0 in range(0, F, fc):
            gate = _dot(nb, wg_ref[:, f0:f0 + fc])
            up = _dot(nb, wu_ref[:, f0:f0 + fc])
            gate_ref[:, f0:f0 + fc] = gate.astype(BF16)
            up_ref[:, f0:f0 + fc] = up.astype(BF16)
            act = (gate * (1.0 / (1.0 + jnp.exp(-gate))) * up).astype(BF16)
            act_ref[:, f0:f0 + fc] = act
            acc = acc + _dot(act, wd_ref[f0:f0 + fc, :])
        h_ref[...] = xv + 0.5 * acc

    return _pcall(
        body, name=name, grid=(T // tm,),
        in_specs=[_row_spec(tm, D), _const_spec((1, D)), _const_spec((D, F)), _const_spec((D, F)), _const_spec((F, D))],
        out_specs=[_row_spec(tm, D), _row_spec(tm, D), _row_spec(tm, F), _row_spec(tm, F), _row_spec(tm, F)],
        out_shape=[jax.ShapeDtypeStruct((T, D), F32), jax.ShapeDtypeStruct((T, D), BF16),
                   jax.ShapeDtypeStruct((T, F), BF16), jax.ShapeDtypeStruct((T, F), BF16), jax.ShapeDtypeStruct((T, F), BF16)],
        args=(x, g, wg, wu, wd), exchange=exchange)


def _ffn_bwd(x, g, gate, up, dh, wg, wu, wd, name, exchange=None):
    T, D = x.shape
    F = wg.shape[1]
    tm = _pick(T, (512, 256, 128))
    halves = 2
    Fh = F // halves
    assert Fh * halves == F and Fh % 128 == 0
    chunks = [(f0, min(f0 + 512, Fh)) for f0 in range(0, Fh, 512)]

    def body(x_ref, g_ref, gate_ref, up_ref, dh_ref, wg_ref, wu_ref, wd_ref, dx_ref, dg_ref, dgate_ref, dup_ref, dn_ref):
        i, h = pl.program_id(0), pl.program_id(1)
        dhv = dh_ref[...]
        dho = (0.5 * dhv).astype(BF16)
        dn = jnp.zeros((tm, D), F32)
        for f0, f1 in chunks:
            d_act = _dot_nt(dho, wd_ref[f0:f1, :])
            gt = gate_ref[:, f0:f1].astype(F32)
            u = up_ref[:, f0:f1].astype(F32)
            sg = 1.0 / (1.0 + jnp.exp(-gt))
            silu = gt * sg
            d_up = (d_act * silu).astype(BF16)
            d_gate = (d_act * u * (sg + silu * (1.0 - sg))).astype(BF16)
            dup_ref[:, f0:f1] = d_up
            dgate_ref[:, f0:f1] = d_gate
            dn = dn + _dot_nt(d_gate, wg_ref[:, f0:f1]) + _dot_nt(d_up, wu_ref[:, f0:f1])

        @pl.when(h == 0)
        def _():
            dn_ref[...] = dn

        @pl.when(h > 0)
        def _():
            dn_ref[...] += dn

        @pl.when((i == 0) & (h == 0))
        def _():
            dg_ref[...] = jnp.zeros_like(dg_ref)

        @pl.when(h == halves - 1)
        def _():
            xv, gv = x_ref[...], g_ref[...]
            _, r = _rms_fwd(xv, gv)
            dxn, dgp = _rms_bwd(xv, gv, r, dn_ref[...])
            dx_ref[...] = dhv + dxn
            dg_ref[...] += dgp

    tile = pl.BlockSpec((tm, D), lambda i, h: (i, 0))
    wide = pl.BlockSpec((tm, Fh), lambda i, h: (i, h))
    return _pcall(
        body, name=name, grid=(T // tm, halves),
        in_specs=[tile, _const_spec((1, D)), wide, wide, tile,
                  pl.BlockSpec((D, Fh), lambda i, h: (0, h)), pl.BlockSpec((D, Fh), lambda i, h: (0, h)),
                  pl.BlockSpec((Fh, D), lambda i, h: (h, 0))],
        out_specs=[tile, _acc_spec((1, D)), wide, wide],
        out_shape=[jax.ShapeDtypeStruct((T, D), F32), jax.ShapeDtypeStruct((1, D), F32),
                   jax.ShapeDtypeStruct((T, F), BF16), jax.ShapeDtypeStruct((T, F), BF16)],
        scratch_shapes=[pltpu.VMEM((tm, D), F32)],
        args=(x, g, gate, up, dh, wg, wu, wd), exchange=exchange)


def _matmul_tn(a, b, name, scale=None, exchange=None):
    T, K = a.shape
    N = b.shape[1]
    tk = _pick(K, (1024, 1408, 512, 256, 128))
    tn = _pick(N, (1024, 1408, 1280, 768, 512, 256, 128))
    tt = _pick(T, (2048, 1024, 512, 256, 128))
    n_t = T // tt

    def body(a_ref, b_ref, o_ref):
        @pl.when(pl.program_id(2) == 0)
        def _():
            o_ref[...] = jnp.zeros_like(o_ref)

        o_ref[...] += _dot_tn(a_ref[...].astype(BF16), b_ref[...].astype(BF16))

        if scale is not None:
            @pl.when(pl.program_id(2) == n_t - 1)
            def _():
                o_ref[...] = o_ref[...] * scale

    outs, got = _pcall(
        body, name=name, grid=(K // tk, N // tn, n_t),
        in_specs=[pl.BlockSpec((tt, tk), lambda i, j, t: (t, i)), pl.BlockSpec((tt, tn), lambda i, j, t: (t, j))],
        out_specs=[pl.BlockSpec((tk, tn), lambda i, j, t: (i, j))],
        out_shape=[jax.ShapeDtypeStruct((K, N), F32)],
        args=(a, b), exchange=exchange)
    return outs[0], got


def _rope_spec(tm, S):
    per = S // tm
    return pl.BlockSpec((tm, HEAD_BLOCK), lambda i: (i % per, 0))


def _proj_fwd(h1, g_mix, w_in, g_q, wq, g_kv, wkv, rope_c, rope_s, S):
    T, D = h1.shape
    P = w_in.shape[1]
    tm = _pick(S, (512, 256, 128))
    QW = MLA_HEADS * HEAD_BLOCK
    VW = MLA_HEADS * V_HEAD
    SQ = SWA_HEADS * SWA_HEAD_DIM
    SK = SWA_KV_HEADS * SWA_HEAD_DIM
    o_cq, o_ckv, o_qs, o_ks, o_vs, o_kpe = 0, Q_LORA, Q_LORA + KV_LORA, Q_LORA + KV_LORA + SQ, Q_LORA + KV_LORA + SQ + SK, Q_LORA + KV_LORA + SQ + 2 * SK
    assert P == o_kpe + HEAD_BLOCK

    def body(h_ref, gm_ref, win_ref, gq_ref, wq_ref, gkv_ref, wkv_ref, c_ref, s_ref,
             u_ref, cq_ref, ckv_ref, cqn_ref, ckvn_ref, q_ref, k_ref, v_ref, qs_ref, ks_ref, vs_ref):
        u, _ = _rms_fwd(h_ref[...], gm_ref[...])
        ub = u.astype(BF16)
        u_ref[...] = ub
        proj = _dot(ub, win_ref[...])
        c_q = proj[:, o_cq:o_ckv]
        c_kv = proj[:, o_ckv:o_qs]
        cq_ref[...] = c_q
        ckv_ref[...] = c_kv
        qs_ref[...] = proj[:, o_qs:o_ks].astype(BF16)
        ks_ref[...] = proj[:, o_ks:o_vs].astype(BF16)
        vs_ref[...] = proj[:, o_vs:o_kpe].astype(BF16)
        cb, sb = c_ref[...], s_ref[...]
        kpe = proj[:, o_kpe:P]
        kpe = kpe * cb + _swap16(kpe) * sb
        cqn, _ = _rms_fwd(c_q, gq_ref[...])
        cqn = cqn.astype(BF16)
        cqn_ref[...] = cqn
        q = _dot(cqn, wq_ref[...])
        q = q * jnp.tile(cb, (1, MLA_HEADS)) + _swap16(q) * jnp.tile(sb, (1, MLA_HEADS))
        q_ref[...] = q.astype(BF16)
        ckvn, _ = _rms_fwd(c_kv, gkv_ref[...])
        ckvn = ckvn.astype(BF16)
        ckvn_ref[...] = ckvn
        kv = _dot(ckvn, wkv_ref[...])
        k_ref[...] = (kv[:, :QW] + jnp.tile(kpe, (1, MLA_HEADS))).astype(BF16)
        v_ref[...] = kv[:, QW:].astype(BF16)

    widths = [(D, BF16), (Q_LORA, F32), (KV_LORA, F32), (Q_LORA, BF16), (KV_LORA, BF16), (QW, BF16), (QW, BF16), (VW, BF16),
              (SQ, BF16), (SK, BF16), (SK, BF16)]
    return pl.pallas_call(
        body, name="proj_fwd", grid=(T // tm,),
        in_specs=[_row_spec(tm, D), _const_spec((1, D)), _const_spec((D, P)), _const_spec((1, Q_LORA)),
                  _const_spec((Q_LORA, QW)), _const_spec((1, KV_LORA)), _const_spec((KV_LORA, QW + VW)),
                  _rope_spec(tm, S), _rope_spec(tm, S)],
        out_specs=[_row_spec(tm, w) for w, _ in widths],
        out_shape=[jax.ShapeDtypeStruct((T, w), dt) for w, dt in widths],
        compiler_params=_cparams(("arbitrary",)),
    )(h1, g_mix, w_in, g_q, wq, g_kv, wkv, rope_c, rope_s)


def _proj_bwd(dq, dk, dv, dqs, dks, dvs, c_q, c_kv, h1, dh2, g_mix, w_in, g_q, wq, g_kv, wkv, rope_c, rope_s, S):
    T, D = h1.shape
    P = w_in.shape[1]
    tm = _pick(S, (256, 128))
    QW = MLA_HEADS * HEAD_BLOCK
    VW = MLA_HEADS * V_HEAD
    SQ = SWA_HEADS * SWA_HEAD_DIM
    SK = SWA_KV_HEADS * SWA_HEAD_DIM

    def body(dq_ref, dk_ref, dv_ref, dqs_ref, dks_ref, dvs_ref, cq_ref, ckv_ref, h_ref, dh2_ref, gm_ref, win_ref, gq_ref,
             wq_ref, gkv_ref, wkv_ref, c_ref, s_ref, dh1_ref, dproj_ref, dql_ref, dkv_ref, dgm_ref, dgq_ref, dgkv_ref):
        cb, sb = c_ref[...], s_ref[...]
        dqv = dq_ref[...]
        dql = dqv * jnp.tile(cb, (1, MLA_HEADS)) + _swap16(dqv * jnp.tile(sb, (1, MLA_HEADS)))
        dql = dql.astype(BF16)
        dql_ref[...] = dql
        c_q = cq_ref[...]
        _, rq = _rms_fwd(c_q, gq_ref[...])
        d_cq, dgq = _rms_bwd(c_q, gq_ref[...], rq, _dot_nt(dql, wq_ref[...]))

        dkv_all = dk_ref[...]
        dkpe = dkv_all[:, 0:HEAD_BLOCK]
        for h in range(1, MLA_HEADS):
            dkpe = dkpe + dkv_all[:, h * HEAD_BLOCK:(h + 1) * HEAD_BLOCK]
        lane = lax.broadcasted_iota(jnp.int32, dkpe.shape, 1)
        dkpe = jnp.where((lane >= 64) & (lane < 96), dkpe, 0.0)
        dkpe = dkpe * cb + _swap16(dkpe * sb)
        dkvc = jnp.concatenate([dkv_all.astype(BF16), dv_ref[...].astype(BF16)], axis=1)
        dkv_ref[...] = dkvc
        c_kv = ckv_ref[...]
        _, rkv = _rms_fwd(c_kv, gkv_ref[...])
        d_ckv, dgkv = _rms_bwd(c_kv, gkv_ref[...], rkv, _dot_nt(dkvc, wkv_ref[...]))

        dproj = jnp.concatenate([d_cq.astype(BF16), d_ckv.astype(BF16), dqs_ref[...].astype(BF16),
                                 dks_ref[...].astype(BF16), dvs_ref[...].astype(BF16), dkpe.astype(BF16)], axis=1)
        dproj_ref[...] = dproj
        hv = h_ref[...]
        _, rm = _rms_fwd(hv, gm_ref[...])
        dxn, dgm = _rms_bwd(hv, gm_ref[...], rm, _dot_nt(dproj, win_ref[...]))
        dh1_ref[...] = dh2_ref[...] + dxn

        @pl.when(pl.program_id(0) == 0)
        def _():
            dgm_ref[...] = jnp.zeros_like(dgm_ref)
            dgq_ref[...] = jnp.zeros_like(dgq_ref)
            dgkv_ref[...] = jnp.zeros_like(dgkv_ref)

        dgm_ref[...] += dgm
        dgq_ref[...] += dgq
        dgkv_ref[...] += dgkv

    return pl.pallas_call(
        body, name="proj_bwd", grid=(T // tm,),
        in_specs=[_row_spec(tm, QW), _row_spec(tm, QW), _row_spec(tm, VW), _row_spec(tm, SQ), _row_spec(tm, SK), _row_spec(tm, SK),
                  _row_spec(tm, Q_LORA), _row_spec(tm, KV_LORA), _row_spec(tm, D), _row_spec(tm, D),
                  _const_spec((1, D)), _const_spec((D, P)), _const_spec((1, Q_LORA)), _const_spec((Q_LORA, QW)),
                  _const_spec((1, KV_LORA)), _const_spec((KV_LORA, QW + VW)), _rope_spec(tm, S), _rope_spec(tm, S)],
        out_specs=[_row_spec(tm, D), _row_spec(tm, P), _row_spec(tm, QW), _row_spec(tm, QW + VW),
                   _acc_spec((1, D)), _acc_spec((1, Q_LORA)), _acc_spec((1, KV_LORA))],
        out_shape=[jax.ShapeDtypeStruct((T, D), F32), jax.ShapeDtypeStruct((T, P), BF16), jax.ShapeDtypeStruct((T, QW), BF16),
                   jax.ShapeDtypeStruct((T, QW + VW), BF16), jax.ShapeDtypeStruct((1, D), F32),
                   jax.ShapeDtypeStruct((1, Q_LORA), F32), jax.ShapeDtypeStruct((1, KV_LORA), F32)],
        compiler_params=_cparams(("arbitrary",)),
    )(dq, dk, dv, dqs, dks, dvs, c_q, c_kv, h1, dh2, g_mix, w_in, g_q, wq, g_kv, wkv, rope_c, rope_s)


def _out_fwd(o_mla, o_swa, g_mla, g_swa, w_o, h1):
    T, D = h1.shape
    W = o_mla.shape[1]
    tm = _pick(T, (512, 256, 128))

    def body(om_ref, os_ref, gm_ref, gs_ref, wo_ref, h_ref, h2_ref, oc_ref):
        a, _ = _rms_fwd(om_ref[...], gm_ref[...])
        b, _ = _rms_fwd(os_ref[...], gs_ref[...])
        oc = jnp.concatenate([a.astype(BF16), b.astype(BF16)], axis=1)
        oc_ref[...] = oc
        h2_ref[...] = h_ref[...] + _dot(oc, wo_ref[...])

    return pl.pallas_call(
        body, name="out_fwd", grid=(T // tm,),
        in_specs=[_row_spec(tm, W), _row_spec(tm, W), _const_spec((1, W)), _const_spec((1, W)), _const_spec((2 * W, D)),
                  _row_spec(tm, D)],
        out_specs=[_row_spec(tm, D), _row_spec(tm, 2 * W)],
        out_shape=[jax.ShapeDtypeStruct((T, D), F32), jax.ShapeDtypeStruct((T, 2 * W), BF16)],
        compiler_params=_cparams(("arbitrary",)),
    )(o_mla, o_swa, g_mla, g_swa, w_o, h1)


def _out_bwd(dh2, o_mla, o_swa, g_mla, g_swa, w_o):
    T, D = dh2.shape
    W = o_mla.shape[1]
    tm = _pick(T, (512, 256, 128))

    def body(dh_ref, om_ref, os_ref, gm_ref, gs_ref, wo_ref, dom_ref, dos_ref, dgm_ref, dgs_ref):
        doc = _dot_nt(dh_ref[...].astype(BF16), wo_ref[...])
        om, osw = om_ref[...], os_ref[...]
        _, ra = _rms_fwd(om, gm_ref[...])
        _, rb = _rms_fwd(osw, gs_ref[...])
        da, dga = _rms_bwd(om, gm_ref[...], ra, doc[:, :W])
        db, dgb = _rms_bwd(osw, gs_ref[...], rb, doc[:, W:])
        dom_ref[...] = da
        dos_ref[...] = db

        @pl.when(pl.program_id(0) == 0)
        def _():
            dgm_ref[...] = jnp.zeros_like(dgm_ref)
            dgs_ref[...] = jnp.zeros_like(dgs_ref)

        dgm_ref[...] += dga
        dgs_ref[...] += dgb

    return pl.pallas_call(
        body, name="out_bwd", grid=(T // tm,),
        in_specs=[_row_spec(tm, D), _row_spec(tm, W), _row_spec(tm, W), _const_spec((1, W)), _const_spec((1, W)),
                  _const_spec((2 * W, D))],
        out_specs=[_row_spec(tm, W), _row_spec(tm, W), _acc_spec((1, W)), _acc_spec((1, W))],
        out_shape=[jax.ShapeDtypeStruct((T, W), F32), jax.ShapeDtypeStruct((T, W), F32),
                   jax.ShapeDtypeStruct((1, W), F32), jax.ShapeDtypeStruct((1, W), F32)],
        compiler_params=_cparams(("arbitrary",)),
    )(dh2, o_mla, o_swa, g_mla, g_swa, w_o)


def _loss_head(h3, target, g_final):
    T, D = h3.shape
    tm = _pick(T, (512, 256, 128))

    def body(h_ref, t_ref, g_ref, dh_ref, loss_ref, dg_ref):
        hv, gv = h_ref[...], g_ref[...]
        y, r = _rms_fwd(hv, gv)
        diff = y - t_ref[...]
        dxn, dgp = _rms_bwd(hv, gv, r, diff * (1.0 / D))
        dh_ref[...] = dxn
        part = 0.5 * jnp.sum(jnp.sum(diff * diff, axis=-1, keepdims=True) * (1.0 / D), axis=0, keepdims=True)

        @pl.when(pl.program_id(0) == 0)
        def _():
            loss_ref[...] = jnp.zeros_like(loss_ref)
            dg_ref[...] = jnp.zeros_like(dg_ref)

        loss_ref[...] += jnp.broadcast_to(part, loss_ref.shape)
        dg_ref[...] += dgp

    return pl.pallas_call(
        body, name="loss_head", grid=(T // tm,),
        in_specs=[_row_spec(tm, D), _row_spec(tm, D), _const_spec((1, D))],
        out_specs=[_row_spec(tm, D), _acc_spec((1, 128)), _acc_spec((1, D))],
        out_shape=[jax.ShapeDtypeStruct((T, D), F32), jax.ShapeDtypeStruct((1, 128), F32), jax.ShapeDtypeStruct((1, D), F32)],
        compiler_params=_cparams(("arbitrary",)),
    )(h3, target, g_final)


def _half_mask(shape, half):
    lane = lax.broadcasted_iota(jnp.int32, shape, 1)
    return (lane < 64) if half == 0 else (lane >= 64)


def _mla_fwd(q, k, v, B, S, exchange=None):
    T = B * S
    tq = _pick(S, MLA_Q_TILES)
    nq = S // tq
    pairs = MLA_HEADS // 2

    def body(q_ref, k_ref, v_ref, o_ref, lse_ref):
        causal = lax.broadcasted_iota(jnp.int32, (tq, tq), 1) <= lax.broadcasted_iota(jnp.int32, (tq, tq), 0)
        low = _half_mask((tq, 2 * V_HEAD), 0)
        for qi in range(nq):
            rows = slice(qi * tq, (qi + 1) * tq)
            past = slice(0, qi * tq)
            outs, lses = [], []
            for half in range(2):
                lanes = slice(half * HEAD_BLOCK, (half + 1) * HEAD_BLOCK)
                qv = q_ref[rows, lanes]
                s_d = jnp.where(causal, _dot_nt(qv, k_ref[rows, lanes]) * MLA_SCALE, NEG)
                m = jnp.max(s_d, axis=-1, keepdims=True)
                if qi:
                    s_p = _dot_nt(qv, k_ref[past, lanes]) * MLA_SCALE
                    m = jnp.maximum(m, jnp.max(s_p, axis=-1, keepdims=True))
                p_d = jnp.exp(s_d - m)
                l = jnp.sum(p_d, axis=-1, keepdims=True)
                acc = _dot(p_d.astype(BF16), v_ref[rows, :])
                if qi:
                    p_p = jnp.exp(s_p - m)
                    l = l + jnp.sum(p_p, axis=-1, keepdims=True)
                    acc = acc + _dot(p_p.astype(BF16), v_ref[past, :])
                outs.append(acc * (1.0 / l))
                lses.append(jnp.broadcast_to(m + jnp.log(l), (tq, 2 * V_HEAD)))
            o_ref[rows, :] = jnp.where(low, outs[0], outs[1])
            lse_ref[rows, :] = jnp.where(low, lses[0], lses[1])

    blk2 = pl.BlockSpec((S, 2 * HEAD_BLOCK), lambda b, j: (b, j))
    blk1 = pl.BlockSpec((S, 2 * V_HEAD), lambda b, j: (b, j))
    return _pcall(
        body, name="mla_fwd", grid=(B, pairs),
        in_specs=[blk2, blk2, blk1], out_specs=[blk1, blk1],
        out_shape=[jax.ShapeDtypeStruct((T, pairs * 2 * V_HEAD), F32)] * 2,
        args=(q, k, v), exchange=exchange)


def _mla_bwd(q, k, v, do, o, lse, B, S, exchange=None):
    T = B * S
    tq = _pick(S, MLA_Q_TILES)
    nq = S // tq
    pairs = MLA_HEADS // 2

    def body(q_ref, k_ref, v_ref, do_ref, o_ref, lse_ref, dq_ref, dk_ref, dv_ref, dk_acc):
        causal = lax.broadcasted_iota(jnp.int32, (tq, tq), 1) <= lax.broadcasted_iota(jnp.int32, (tq, tq), 0)
        for half in range(2):
            lanes = slice(half * HEAD_BLOCK, (half + 1) * HEAD_BLOCK)
            col = half * V_HEAD
            own = _half_mask((tq, 2 * V_HEAD), half)
            for qi in reversed(range(nq)):
                rows = slice(qi * tq, (qi + 1) * tq)
                past = slice(0, qi * tq)
                first = qi == nq - 1
                qv = q_ref[rows, lanes]
                dov = jnp.where(own, do_ref[rows, :], 0.0)
                dsum = jnp.sum(dov * o_ref[rows, :], axis=-1, keepdims=True)
                dob = dov.astype(BF16)
                lse_col = lse_ref[rows, col:col + 1]

                def grads(keys, mask):
                    kv, vv = k_ref[keys, lanes], v_ref[keys, :]
                    p = jnp.exp(_dot_nt(qv, kv) * MLA_SCALE - lse_col)
                    if mask is not None:
                        p = jnp.where(mask, p, 0.0)
                    ds = (p * (_dot_nt(dob, vv) - dsum) * MLA_SCALE).astype(BF16)
                    dk, dv = _dot_tn(ds, qv), _dot_tn(p.astype(BF16), dob)
                    if first:
                        dk_acc[keys, :] = dk
                    else:
                        dk_acc[keys, :] += dk
                    if first and half == 0:
                        dv_ref[keys, :] = dv
                    else:
                        dv_ref[keys, :] += dv
                    return _dot(ds, kv)

                dq = grads(rows, causal)
                if qi:
                    dq = dq + grads(past, None)
                dq_ref[rows, lanes] = dq
            dk_ref[:, lanes] = dk_acc[...]

    blk2 = pl.BlockSpec((S, 2 * HEAD_BLOCK), lambda b, j: (b, j))
    blk1 = pl.BlockSpec((S, 2 * V_HEAD), lambda b, j: (b, j))
    return _pcall(
        body, name="mla_bwd", grid=(B, pairs),
        in_specs=[blk2, blk2, blk1, blk1, blk1, blk1], out_specs=[blk2, blk2, blk1],
        out_shape=[jax.ShapeDtypeStruct((T, MLA_HEADS * HEAD_BLOCK), F32), jax.ShapeDtypeStruct((T, MLA_HEADS * HEAD_BLOCK), F32),
                   jax.ShapeDtypeStruct((T, pairs * 2 * V_HEAD), F32)],
        scratch_shapes=[pltpu.VMEM((S, HEAD_BLOCK), F32)],
        args=(q, k, v, do, o, lse), exchange=exchange)


SWA_GROUP = SWA_HEADS // SWA_KV_HEADS
SWA_ROWS = SWA_GROUP * WINDOW


SWA_UNROLL = 4


def _loop_unrolled(n, unroll, body):
    u = next(c for c in (unroll, 2, 1) if n % c == 0)

    def trip(t, carry):
        for i in range(u):
            body(t * u + i, 0)
        return carry

    lax.fori_loop(0, n // u, trip, 0)


def _swa_stack(ref, r0):
    return jnp.concatenate([ref[pl.ds(r0, WINDOW), j * WINDOW:(j + 1) * WINDOW] for j in range(SWA_GROUP)], axis=0)


def _swa_unstack(low, high):
    sel = _half_mask((WINDOW, WINDOW), 0)
    return jnp.concatenate([jnp.where(sel, low[j * WINDOW:(j + 1) * WINDOW], high[j * WINDOW:(j + 1) * WINDOW])
                            for j in range(SWA_GROUP)], axis=1)


def _swa_sink_col(sink_ref, g):
    return jnp.concatenate([jnp.broadcast_to(sink_ref[g * SWA_GROUP + j:g * SWA_GROUP + j + 1, 0:1], (WINDOW, 1))
                            for j in range(SWA_GROUP)], axis=0)


def _swa_probs(qm, kcat, bias_g, sink, n):
    qi = lax.broadcasted_iota(jnp.int32, (SWA_ROWS, 2 * WINDOW), 0) % WINDOW
    col = lax.broadcasted_iota(jnp.int32, (SWA_ROWS, 2 * WINDOW), 1)
    valid = (col > qi) & (col <= qi + WINDOW) & ((col >= WINDOW) | (n > 0))
    s = jnp.where(valid, _dot_nt(qm, kcat) * SWA_SCALE + bias_g, NEG)
    m = jnp.maximum(jnp.max(s, axis=-1, keepdims=True), sink)
    e = jnp.exp(s - m)
    e_s = jnp.exp(sink - m)
    inv = 1.0 / (jnp.sum(e, axis=-1, keepdims=True) + e_s)
    return e * inv, e_s * inv


def _swa_fwd(q, k, v, bias, sinks, B, S):
    T = B * S
    W = WINDOW
    nb = S // W

    def body(q_ref, k_ref, v_ref, bias_ref, sink_ref, o_ref):
        sink_cols = [_swa_sink_col(sink_ref, g) for g in range(SWA_KV_HEADS)]

        def block(n, _):
            r0 = pl.multiple_of(n * W, W)
            p0 = pl.multiple_of(jnp.maximum(n - 1, 0) * W, W)
            kcat = jnp.concatenate([k_ref[pl.ds(p0, W), :], k_ref[pl.ds(r0, W), :]], axis=0)
            vcat = jnp.concatenate([v_ref[pl.ds(p0, W), :], v_ref[pl.ds(r0, W), :]], axis=0)
            qcat = _swa_stack(q_ref, r0)
            outs = []
            for g in range(SWA_KV_HEADS):
                qm = jnp.where(_half_mask(qcat.shape, g), qcat, jnp.zeros_like(qcat))
                p, _ = _swa_probs(qm, kcat, bias_ref[g], sink_cols[g], n)
                outs.append(_dot(p.astype(BF16), vcat))
            o_ref[pl.ds(r0, W), :] = _swa_unstack(outs[0], outs[1])
            return 0

        _loop_unrolled(nb, SWA_UNROLL, block)

    SQ, SK = SWA_HEADS * SWA_HEAD_DIM, SWA_KV_HEADS * SWA_HEAD_DIM
    return pl.pallas_call(
        body, name="swa_fwd", grid=(B,),
        in_specs=[pl.BlockSpec((S, SQ), lambda b: (b, 0)), pl.BlockSpec((S, SK), lambda b: (b, 0)),
                  pl.BlockSpec((S, SK), lambda b: (b, 0)), _const_spec((SWA_KV_HEADS, SWA_ROWS, 2 * W)),
                  _const_spec((SWA_HEADS, 128))],
        out_specs=pl.BlockSpec((S, SQ), lambda b: (b, 0)),
        out_shape=jax.ShapeDtypeStruct((T, SQ), F32),
        compiler_params=_cparams(("arbitrary",)),
    )(q, k, v, bias, sinks)


def _swa_bwd(q, k, v, do, bias, sinks, B, S):
    T = B * S
    W = WINDOW
    nb = S // W

    def body(q_ref, k_ref, v_ref, do_ref, bias_ref, sink_ref, dq_ref, dk_ref, dv_ref, dbias_ref, dsink_ref):
        @pl.when(pl.program_id(0) == 0)
        def _():
            dbias_ref[...] = jnp.zeros_like(dbias_ref)
            dsink_ref[...] = jnp.zeros_like(dsink_ref)

        dk_ref[...] = jnp.zeros_like(dk_ref)
        dv_ref[...] = jnp.zeros_like(dv_ref)
        sink_cols = [_swa_sink_col(sink_ref, g) for g in range(SWA_KV_HEADS)]

        def block(n, _):
            r0 = pl.multiple_of(n * W, W)
            p0 = pl.multiple_of(jnp.maximum(n - 1, 0) * W, W)
            kcat = jnp.concatenate([k_ref[pl.ds(p0, W), :], k_ref[pl.ds(r0, W), :]], axis=0)
            vcat = jnp.concatenate([v_ref[pl.ds(p0, W), :], v_ref[pl.ds(r0, W), :]], axis=0)
            qcat = _swa_stack(q_ref, r0)
            docat = _swa_stack(do_ref, r0)
            dk = jnp.zeros((2 * W, W), F32)
            dv = jnp.zeros((2 * W, W), F32)
            dqs = []
            for g in range(SWA_KV_HEADS):
                own = _half_mask(qcat.shape, g)
                qm = jnp.where(own, qcat, jnp.zeros_like(qcat))
                dom = jnp.where(own, docat, 0.0).astype(BF16)
                p, p_s = _swa_probs(qm, kcat, bias_ref[g], sink_cols[g], n)
                dp = _dot_nt(dom, vcat)
                dsum = jnp.sum(p * dp, axis=-1, keepdims=True)
                ds = p * (dp - dsum)
                dbias_ref[g] += ds
                dsink_ref[g] += jnp.broadcast_to(-p_s * dsum, (SWA_ROWS, 128))
                dsb = (ds * SWA_SCALE).astype(BF16)
                dqs.append(_dot(dsb, kcat))
                dk = dk + _dot_tn(dsb, qm)
                dv = dv + _dot_tn(p.astype(BF16), dom)
            dq_ref[pl.ds(r0, W), :] = _swa_unstack(dqs[0], dqs[1])
            dk_ref[pl.ds(p0, W), :] += dk[:W]
            dv_ref[pl.ds(p0, W), :] += dv[:W]
            dk_ref[pl.ds(r0, W), :] += dk[W:]
            dv_ref[pl.ds(r0, W), :] += dv[W:]
            return 0

        _loop_unrolled(nb, SWA_UNROLL, block)

    SQ, SK = SWA_HEADS * SWA_HEAD_DIM, SWA_KV_HEADS * SWA_HEAD_DIM
    row = lambda w: pl.BlockSpec((S, w), lambda b: (b, 0))
    return pl.pallas_call(
        body, name="swa_bwd", grid=(B,),
        in_specs=[row(SQ), row(SK), row(SK), row(SQ), _const_spec((SWA_KV_HEADS, SWA_ROWS, 2 * W)), _const_spec((SWA_HEADS, 128))],
        out_specs=[row(SQ), row(SK), row(SK), _acc_spec((SWA_KV_HEADS, SWA_ROWS, 2 * W)), _acc_spec((SWA_KV_HEADS, SWA_ROWS, 128))],
        out_shape=[jax.ShapeDtypeStruct((T, SQ), F32), jax.ShapeDtypeStruct((T, SK), F32), jax.ShapeDtypeStruct((T, SK), F32),
                   jax.ShapeDtypeStruct((SWA_KV_HEADS, SWA_ROWS, 2 * W), F32), jax.ShapeDtypeStruct((SWA_KV_HEADS, SWA_ROWS, 128), F32)],
        compiler_params=_cparams(("arbitrary",)),
    )(q, k, v, do, bias, sinks)


def _bias_expand(rel_bias, bucket):
    W = WINDOW

    def body(rb_ref, bucket_ref, out_ref):
        bk = bucket_ref[...]
        for h in range(SWA_HEADS):
            def add(b, acc):
                return jnp.where(bk == b, rb_ref[b, h], acc)

            out_ref[h] = lax.fori_loop(0, REL_BUCKETS, add, jnp.zeros((W, 2 * W), F32))

    return pl.pallas_call(
        body, name="bias_expand",
        in_specs=[pl.BlockSpec(memory_space=pltpu.SMEM), pl.BlockSpec(memory_space=pltpu.VMEM)],
        out_specs=pl.BlockSpec(memory_space=pltpu.VMEM),
        out_shape=jax.ShapeDtypeStruct((SWA_HEADS, W, 2 * W), F32),
        compiler_params=_cparams(),
    )(rel_bias, bucket)


def _bias_reduce(dbias, dsink_rows, bucket):
    def body(db_ref, ds_ref, bucket_ref, out_ref, sink_out_ref):
        sink_lane = lax.broadcasted_iota(jnp.int32, (1, 128), 1)
        sink_acc = jnp.zeros((1, 128), F32)
        for h in range(SWA_HEADS):
            sink_acc = jnp.where(sink_lane == h, jnp.sum(ds_ref[h], axis=0, keepdims=True), sink_acc)
        sink_out_ref[...] = sink_acc
        bk = bucket_ref[...]
        row = lax.broadcasted_iota(jnp.int32, (REL_BUCKETS, 128), 0)
        lane = lax.broadcasted_iota(jnp.int32, (REL_BUCKETS, 128), 1)
        acc = jnp.zeros((REL_BUCKETS, 128), F32)
        for h in range(SWA_HEADS):
            dbh = db_ref[h]

            def add(b, acc):
                part = jnp.sum(jnp.sum(jnp.where(bk == b, dbh, 0.0), axis=-1, keepdims=True), axis=0, keepdims=True)
                return jnp.where((row == b) & (lane == h), part, acc)

            acc = lax.fori_loop(0, REL_BUCKETS, add, acc)
        out_ref[...] = acc

    return pl.pallas_call(
        body, name="bias_reduce",
        out_shape=[jax.ShapeDtypeStruct((REL_BUCKETS, 128), F32), jax.ShapeDtypeStruct((1, 128), F32)],
        compiler_params=_cparams(),
    )(dbias, dsink_rows, bucket)


def _t5_bucket(dist):
    n = jnp.maximum(dist, 0)
    max_exact = REL_BUCKETS // 2
    nf = jnp.maximum(n, 1).astype(F32)
    large = max_exact + (jnp.log(nf / max_exact) / math.log(REL_MAX_DIST / max_exact) * (REL_BUCKETS - max_exact)).astype(jnp.int32)
    large = jnp.minimum(large, REL_BUCKETS - 1)
    return jnp.where(n < max_exact, n, large)


def _pair_heads(a, axis, inverse=False):
    shp = a.shape
    a = a.reshape(shp[:axis] + ((4, 2, 64) if inverse else (2, 4, 64)) + shp[axis + 1:])
    a = jnp.swapaxes(a, axis, axis + 1)
    return a.reshape(shp)


def _col_full(g):
    return jnp.transpose(g, (1, 0, 2)).reshape(g.shape[1], N_DEV * g.shape[2])


def _row_full(g):
    return g.reshape(N_DEV * g.shape[1], g.shape[2])


def _owner_blocks(w, by_col, me):
    if by_col:
        r, c = w.shape[0], w.shape[1] // N_DEV
        own = lax.dynamic_slice(w, (0, me * c), (r, c))
        blocks = jnp.transpose(w.astype(BF16).reshape(r, N_DEV, c), (1, 0, 2))
    else:
        r, c = w.shape[0] // N_DEV, w.shape[1]
        own = lax.dynamic_slice(w, (me * r, 0), (r, c))
        blocks = w.astype(BF16).reshape(N_DEV, r, c)
    return own, blocks


_IN_WIDTHS = (Q_LORA, KV_LORA, QK_ROPE, SWA_HEADS * SWA_HEAD_DIM, SWA_KV_HEADS * SWA_HEAD_DIM, SWA_KV_HEADS * SWA_HEAD_DIM)


def _split_in(w):
    parts, o = [], 0
    for wd in _IN_WIDTHS:
        parts.append(w[:, o:o + wd])
        o += wd
    return parts


def _win_to_kernel(w):
    cq, ckv, kpe, qs, ks, vs = _split_in(w)
    kpe = jnp.pad(kpe, ((0, 0), (64, 32)))
    return jnp.concatenate([cq, ckv, _pair_heads(qs, 1), ks, vs, kpe], axis=1)


def _win_from_kernel(dw):
    o = [0, Q_LORA, Q_LORA + KV_LORA, Q_LORA + KV_LORA + 512, Q_LORA + KV_LORA + 640, Q_LORA + KV_LORA + 768]
    cq, ckv, qs, ks, vs, kpe = [dw[:, a:b] for a, b in zip(o, o[1:] + [dw.shape[1]])]
    return jnp.concatenate([cq, ckv, kpe[:, 64:96], _pair_heads(qs, 1, inverse=True), ks, vs], axis=1)


def _wq_to_kernel(w):
    return jnp.pad(w.reshape(Q_LORA, MLA_HEADS, QK_NOPE + QK_ROPE), ((0, 0), (0, 0), (0, 32))).reshape(Q_LORA, MLA_HEADS * HEAD_BLOCK)


def _wq_from_kernel(dw):
    return dw.reshape(Q_LORA, MLA_HEADS, HEAD_BLOCK)[:, :, :QK_NOPE + QK_ROPE].reshape(Q_LORA, -1)


def _wkv_to_kernel(w):
    w3 = w.reshape(KV_LORA, MLA_HEADS, QK_NOPE + V_HEAD)
    kpart = jnp.pad(w3[:, :, :QK_NOPE], ((0, 0), (0, 0), (0, 64))).reshape(KV_LORA, MLA_HEADS * HEAD_BLOCK)
    return jnp.concatenate([kpart, w3[:, :, QK_NOPE:].reshape(KV_LORA, MLA_HEADS * V_HEAD)], axis=1)


def _wkv_from_kernel(dw):
    dk = dw[:, :MLA_HEADS * HEAD_BLOCK].reshape(KV_LORA, MLA_HEADS, HEAD_BLOCK)[:, :, :QK_NOPE]
    dv = dw[:, MLA_HEADS * HEAD_BLOCK:].reshape(KV_LORA, MLA_HEADS, V_HEAD)
    return jnp.concatenate([dk, dv], axis=2).reshape(KV_LORA, -1)


def _wo_to_kernel(w):
    return jnp.concatenate([w[:512], _pair_heads(w[512:], 0)], axis=0)


def _wo_from_kernel(dw):
    return jnp.concatenate([dw[:512], _pair_heads(dw[512:], 0, inverse=True)], axis=0)


_SMALL = ("g_ffn1", "g_mix", "g_q_a", "g_kv_a", "attn_sinks", "rel_bias", "g_out_mla", "g_out_swa", "g_ffn2", "g_final")


class _Comm:
    def __init__(self, shards, by_col, me):
        self.shards, self.by_col, self.me = shards, by_col, me
        self.own32, self.recv16 = {}, {}

    def gather(self, keys):
        return [("gather", self.shards[k]) for k in keys]

    def gathered(self, keys, results, wts):
        for k, g in zip(keys, results):
            full = _col_full(g) if self.by_col[k] else _row_full(g)
            wts[k] = _TO_KERNEL[k](full) if k in _TO_KERNEL else full

    def scatter(self, keys, grads):
        jobs = []
        for k in keys:
            g = _FROM_KERNEL[k](grads[k]) if k in _FROM_KERNEL else grads[k]
            self.own32[k], blocks = _owner_blocks(g, self.by_col[k], self.me)
            jobs.append(("scatter", blocks))
        return jobs

    def scattered(self, keys, results):
        self.recv16.update(zip(keys, results))


def _step(x, target, gains, wts, comm):
    def jobs_in(keys):
        return comm.gather(keys) if comm else None

    def jobs_out(keys):
        return comm.scatter(keys, grads) if comm else None

    B, S, D = x.shape
    T = B * S
    x2 = x.reshape(T, D)
    t2 = target.reshape(T, D)

    pos = jnp.arange(S, dtype=F32)
    inv_freq = ROPE_THETA ** (-jnp.arange(0, QK_ROPE, 2, dtype=F32) / QK_ROPE)
    ang = pos[:, None] * inv_freq[None, :]
    cos, sin = jnp.cos(ang), jnp.sin(ang)
    ones, zeros = jnp.ones((S, 64), F32), jnp.zeros((S, 32), F32)
    rope_c = jnp.concatenate([ones, cos, cos, zeros], axis=1)
    rope_s = jnp.concatenate([0.0 * ones, -sin, sin, zeros], axis=1)

    qi = jnp.arange(WINDOW)[:, None]
    kj = jnp.arange(2 * WINDOW)[None, :]
    bucket = _t5_bucket(qi + WINDOW - kj).astype(jnp.int32)
    bias = _bias_expand(gains["rel_bias"], bucket).reshape(SWA_KV_HEADS, SWA_ROWS, 2 * WINDOW)
    sinks = jnp.broadcast_to(gains["attn_sinks"].reshape(SWA_HEADS, 1), (SWA_HEADS, 128))
    g_swa = _pair_heads(gains["g_out_swa"], 1)

    keys = ("in", "q", "kv", "o", "g2")
    (h1, n1, gate1, up1, act1), got = _ffn_fwd(x2, gains["g_ffn1"], wts["g1"], wts["u1"], wts["d1"], "ffn1_fwd", jobs_in(keys))
    if comm:
        comm.gathered(keys, got, wts)
    (u, c_q, c_kv, cqn, ckvn, q, k, v, qs, ks, vs) = _proj_fwd(
        h1, gains["g_mix"], wts["in"], gains["g_q_a"], wts["q"], gains["g_kv_a"], wts["kv"], rope_c, rope_s, S)
    keys = ("u2", "d2")
    (o_mla, lse), got = _mla_fwd(q, k, v, B, S, jobs_in(keys))
    if comm:
        comm.gathered(keys, got, wts)
    o_swa = _swa_fwd(qs, ks, vs, bias, sinks, B, S)
    h2, oc = _out_fwd(o_mla, o_swa, gains["g_out_mla"], g_swa, wts["o"], h1)
    (h3, n2, gate2, up2, act2), _ = _ffn_fwd(h2, gains["g_ffn2"], wts["g2"], wts["u2"], wts["d2"], "ffn2_fwd")
    dh3, loss, dg_final = _loss_head(h3, t2, gains["g_final"].reshape(1, D))

    grads = {}
    (dh2, dg_ffn2, dgate2, dup2), _ = _ffn_bwd(h2, gains["g_ffn2"], gate2, up2, dh3, wts["g2"], wts["u2"], wts["d2"], "ffn2_bwd")
    grads["g2"], _ = _matmul_tn(n2, dgate2, "dw_gate2")
    grads["u2"], _ = _matmul_tn(n2, dup2, "dw_up2")
    grads["d2"], _ = _matmul_tn(act2, dh3, "dw_down2", 0.5)
    do_mla, do_swa, dg_mla, dg_swa = _out_bwd(dh2, o_mla, o_swa, gains["g_out_mla"], g_swa, wts["o"])
    grads["o"], _ = _matmul_tn(oc, dh2, "dw_o")
    keys = ("g2", "u2", "d2")
    (dq, dk, dv), got = _mla_bwd(q, k, v, do_mla, o_mla, lse, B, S, jobs_out(keys))
    if comm:
        comm.scattered(keys, got)
    dqs, dks, dvs, dbias, dsink = _swa_bwd(qs, ks, vs, do_swa, bias, sinks, B, S)
    dh1, dproj, dql, dkvc, dg_mix, dg_q, dg_kv = _proj_bwd(
        dq, dk, dv, dqs, dks, dvs, c_q, c_kv, h1, dh2, gains["g_mix"], wts["in"], gains["g_q_a"], wts["q"],
        gains["g_kv_a"], wts["kv"], rope_c, rope_s, S)
    grads["in"], _ = _matmul_tn(u, dproj, "dw_in")
    grads["q"], _ = _matmul_tn(cqn, dql, "dw_q")
    grads["kv"], _ = _matmul_tn(ckvn, dkvc, "dw_kv")
    grads["d1"], _ = _matmul_tn(act1, dh1, "dw_down1", 0.5)
    keys = ("o", "in", "q", "kv", "d1")
    (dx, dg_ffn1, dgate1, dup1), got = _ffn_bwd(
        x2, gains["g_ffn1"], gate1, up1, dh1, wts["g1"], wts["u1"], wts["d1"], "ffn1_bwd", jobs_out(keys))
    if comm:
        comm.scattered(keys, got)
    grads["g1"], _ = _matmul_tn(n1, dgate1, "dw_gate1")
    grads["u1"], got = _matmul_tn(n1, dup1, "dw_up1", exchange=jobs_out(("g1",)))
    if comm:
        comm.scattered(("g1",), got)
        comm.scattered(("u1",), _exchange(jobs_out(("u1",)), "grad_exchange_last"))

    d_rel_bias, d_sinks = _bias_reduce(dbias.reshape(SWA_HEADS, WINDOW, 2 * WINDOW), dsink.reshape(SWA_HEADS, WINDOW, 128), bucket)
    small = {
        "g_ffn1": dg_ffn1, "g_mix": dg_mix, "g_q_a": dg_q, "g_kv_a": dg_kv, "attn_sinks": d_sinks,
        "rel_bias": d_rel_bias, "g_out_mla": dg_mla, "g_out_swa": _pair_heads(dg_swa, 1, inverse=True),
        "g_ffn2": dg_ffn2, "g_final": dg_final, "loss": loss,
    }
    return dx.reshape(B, S, D), grads, small


_WEIGHTS = ("g_ffn1", "w_ffn1_gate", "w_ffn1_up", "w_ffn1_down", "g_mix", "w_in", "g_q_a", "w_q_b", "g_kv_a", "w_kv_b",
            "attn_sinks", "rel_bias", "g_out_mla", "g_out_swa", "w_o", "g_ffn2", "w_ffn2_gate", "w_ffn2_up", "w_ffn2_down",
            "g_final")
_BIG = (("w_ffn1_gate", "g1", True), ("w_ffn1_up", "u1", True), ("w_ffn1_down", "d1", False), ("w_in", "in", True),
        ("w_q_b", "q", True), ("w_kv_b", "kv", True), ("w_o", "o", False), ("w_ffn2_gate", "g2", True),
        ("w_ffn2_up", "u2", True), ("w_ffn2_down", "d2", False))
_TO_KERNEL = {"in": _win_to_kernel, "q": _wq_to_kernel, "kv": _wkv_to_kernel, "o": _wo_to_kernel}
_FROM_KERNEL = {"in": _win_from_kernel, "q": _wq_from_kernel, "kv": _wkv_from_kernel, "o": _wo_from_kernel}


def kernel(x, g_ffn1, w_ffn1_gate, w_ffn1_up, w_ffn1_down, g_mix, w_in, g_q_a, w_q_b, g_kv_a, w_kv_b, attn_sinks, rel_bias, g_out_mla, g_out_swa, w_o, g_ffn2, w_ffn2_gate, w_ffn2_up, w_ffn2_down, g_final, loss_target, m_g_ffn1, m_w_ffn1_gate, m_w_ffn1_up, m_w_ffn1_down, m_g_mix, m_w_in, m_g_q_a, m_w_q_b, m_g_kv_a, m_w_kv_b, m_attn_sinks, m_rel_bias, m_g_out_mla, m_g_out_swa, m_w_o, m_g_ffn2, m_w_ffn2_gate, m_w_ffn2_up, m_w_ffn2_down, m_g_final, v_g_ffn1, v_w_ffn1_gate, v_w_ffn1_up, v_w_ffn1_down, v_g_mix, v_w_in, v_g_q_a, v_w_q_b, v_g_kv_a, v_w_kv_b, v_attn_sinks, v_rel_bias, v_g_out_mla, v_g_out_swa, v_w_o, v_g_ffn2, v_w_ffn2_gate, v_w_ffn2_up, v_w_ffn2_down, v_g_final):
    w = dict(zip(_WEIGHTS, (g_ffn1, w_ffn1_gate, w_ffn1_up, w_ffn1_down, g_mix, w_in, g_q_a, w_q_b, g_kv_a, w_kv_b, attn_sinks,
                            rel_bias, g_out_mla, g_out_swa, w_o, g_ffn2, w_ffn2_gate, w_ffn2_up, w_ffn2_down, g_final)))
    m = dict(zip(_WEIGHTS, (m_g_ffn1, m_w_ffn1_gate, m_w_ffn1_up, m_w_ffn1_down, m_g_mix, m_w_in, m_g_q_a, m_w_q_b, m_g_kv_a,
                            m_w_kv_b, m_attn_sinks, m_rel_bias, m_g_out_mla, m_g_out_swa, m_w_o, m_g_ffn2, m_w_ffn2_gate,
                            m_w_ffn2_up, m_w_ffn2_down, m_g_final)))
    v = dict(zip(_WEIGHTS, (v_g_ffn1, v_w_ffn1_gate, v_w_ffn1_up, v_w_ffn1_down, v_g_mix, v_w_in, v_g_q_a, v_w_q_b, v_g_kv_a,
                            v_w_kv_b, v_attn_sinks, v_rel_bias, v_g_out_mla, v_g_out_swa, v_w_o, v_g_ffn2, v_w_ffn2_gate,
                            v_w_ffn2_up, v_w_ffn2_down, v_g_final)))
    me = 4 * lax.axis_index("x") + 2 * lax.axis_index("y") + lax.axis_index("c")

    shards = dict(zip([key for _, key, _ in _BIG], _cast_bf16([w[name][0] for name, _, _ in _BIG])))
    comm = _Comm(shards, {key: by_col for _, key, by_col in _BIG}, me)
    first = ("g1", "u1", "d1")
    wts = {}
    comm.gathered(first, _all_gather([shards[k] for k in first], "weights_all_gather"), wts)

    gains = {n: (w[n] if n == "rel_bias" else w[n].reshape(1, -1)) for n in _SMALL}
    grad_x, _, small = _step(x, loss_target, gains, wts, comm)

    small_gathered = _all_gather([small[n] for n in _SMALL] + [small["loss"]], "small_all_gather")

    out_g, out_d, out_m, out_v = {}, {}, {}, {}
    me1 = me.astype(jnp.int32).reshape(1)
    for name, key, _ in _BIG:
        g, d, mn, vn = _adam_big(w[name][0], m[name][0], v[name][0], comm.own32[key], comm.recv16[key], me1)
        out_g[name], out_d[name], out_m[name], out_v[name] = g[None], d[None], mn[None], vn[None]
    as_2d = lambda src: [src[n] if src[n].ndim == 2 else src[n].reshape(1, -1) for n in _SMALL]
    loss, updates = _adam_small(as_2d(w), as_2d(m), as_2d(v), small_gathered[:-1], small_gathered[-1])
    for n, (g, d, mn, vn) in zip(_SMALL, updates):
        out_g[n], out_d[n], out_m[n], out_v[n] = (a.reshape(w[n].shape) for a in (g, d, mn, vn))

    return (loss[0, 0], grad_x, *[out_g[n] for n in _WEIGHTS], *[out_d[n] for n in _WEIGHTS],
            *[out_m[n] for n in _WEIGHTS], *[out_v[n] for n in _WEIGHTS])
```

```python
import functools
import math

import jax
import jax.numpy as jnp
from jax import lax
from jax.experimental import pallas as pl
from jax.experimental.pallas import tpu as pltpu

F32 = jnp.float32
BF16 = jnp.bfloat16
MESH = pl.DeviceIdType.MESH

EPS = 1e-6
NEG = -1e30
N_DEV = 8

MLA_HEADS = 8
Q_LORA = 256
KV_LORA = 128
QK_NOPE = 64
QK_ROPE = 32
V_HEAD = 64
ROPE_THETA = 10000.0
SWA_HEADS = 8
SWA_KV_HEADS = 2
SWA_HEAD_DIM = 64
WINDOW = 128
REL_BUCKETS = 32
REL_MAX_DIST = 128
HEAD_BLOCK = 128
MLA_Q_TILES = (512, 256, 128)
MLA_SCALE = (QK_NOPE + QK_ROPE) ** -0.5
SWA_SCALE = SWA_HEAD_DIM ** -0.5

ADAM_LR = 0.001
ADAM_B1 = 0.9
ADAM_B2 = 0.999
ADAM_EPS = 1e-08
ADAM_WD = 0.01
ADAM_STEP = 10

VMEM_LIMIT = 56 * 1024 * 1024


def _cparams(semantics=None):
    return pltpu.CompilerParams(dimension_semantics=semantics, vmem_limit_bytes=VMEM_LIMIT)


def _dot(a, b):
    return jnp.dot(a, b, preferred_element_type=F32)


def _dot_nt(a, b):
    return lax.dot_general(a, b, (((1,), (1,)), ((), ())), preferred_element_type=F32)


def _dot_tn(a, b):
    return lax.dot_general(a, b, (((0,), (0,)), ((), ())), preferred_element_type=F32)


def _rms_fwd(x, g):
    r = lax.rsqrt(jnp.mean(x * x, axis=-1, keepdims=True) + EPS)
    return x * r * g, r


def _rms_bwd(x, g, r, dy):
    xh = x * r
    dyg = dy * g
    dx = r * (dyg - xh * jnp.mean(dyg * xh, axis=-1, keepdims=True))
    return dx, jnp.sum(dy * xh, axis=0, keepdims=True)


def _swap16(x):
    n = x.shape[-1]
    up = pltpu.roll(x, n - 16, 1)
    down = pltpu.roll(x, 16, 1)
    lane = lax.broadcasted_iota(jnp.int32, x.shape, 1) % HEAD_BLOCK
    return jnp.where((lane >= 64) & (lane < 80), up, jnp.where((lane >= 80) & (lane < 96), down, 0.0))


def _const_spec(shape):
    return pl.BlockSpec(shape, lambda *_: (0,) * len(shape), pipeline_mode=pl.Buffered(1))


def _acc_spec(shape):
    return pl.BlockSpec(shape, lambda *_: (0,) * len(shape))


def _row_spec(tm, width):
    return pl.BlockSpec((tm, width), lambda i: (i, 0))


def _cast_bf16(arrays):
    n = len(arrays)

    def body(*refs):
        for src, dst in zip(refs[:n], refs[n:]):
            dst[...] = src[...].astype(BF16)

    return pl.pallas_call(
        body, name="cast_weights",
        out_shape=[jax.ShapeDtypeStruct(a.shape, BF16) for a in arrays],
        compiler_params=_cparams(),
    )(*arrays)


def _all_gather(arrays, name):
    n = len(arrays)

    def body(*refs):
        ins, outs = refs[:n], refs[n:2 * n]
        send_sems, recv_sems, local_sems = refs[2 * n:]
        x, y, c = lax.axis_index("x"), lax.axis_index("y"), lax.axis_index("c")
        me, sibling = (x, y, c), (x, y, 1 - c)
        chips = [(1 - x, y), (x, 1 - y), (1 - x, 1 - y)]

        def slot(a, dev):
            return outs[a].at[4 * dev[0] + 2 * dev[1] + dev[2]]

        def copy(a, k, block, to, src=None):
            return pltpu.make_async_remote_copy(
                src_ref=slot(a, block) if src is None else src, dst_ref=slot(a, block),
                send_sem=send_sems.at[a, k], recv_sem=recv_sems.at[a, k], device_id=to, device_id_type=MESH)

        mine = [pltpu.make_async_copy(ins[a], slot(a, me), local_sems.at[a]) for a in range(n)]
        for cp in mine:
            cp.start()
        first = []
        for a in range(n):
            first.append(copy(a, 0, me, sibling, src=ins[a]))
            first += [copy(a, 1 + j, me, (*chip, c), src=ins[a]) for j, chip in enumerate(chips)]
        for cp in first:
            cp.start()
        passed = []
        for j, chip in enumerate(chips):
            for a in range(n):
                copy(a, 1 + j, (*chip, c), me).wait_recv()
                cp = copy(a, 4 + j, (*chip, c), sibling)
                cp.start()
                passed.append(cp)
        for a in range(n):
            copy(a, 0, sibling, me).wait_recv()
            for j, chip in enumerate(chips):
                copy(a, 4 + j, (*chip, 1 - c), me).wait_recv()
        for cp in first + passed:
            cp.wait_send()
        for cp in mine:
            cp.wait()

    any_spec = pl.BlockSpec(memory_space=pl.ANY)
    return pl.pallas_call(
        body, name=name,
        out_shape=[jax.ShapeDtypeStruct((N_DEV,) + a.shape, a.dtype) for a in arrays],
        in_specs=[any_spec] * n, out_specs=[any_spec] * n,
        scratch_shapes=[pltpu.SemaphoreType.DMA((n, 7)), pltpu.SemaphoreType.DMA((n, 7)), pltpu.SemaphoreType.DMA((n,))],
    )(*arrays)


def _exchange_shapes(jobs):
    dsts = [jax.ShapeDtypeStruct(((N_DEV,) + a.shape) if kind == "gather" else a.shape, a.dtype) for kind, a in jobs]
    n = len(jobs)
    sems = [pltpu.SemaphoreType.DMA((n, N_DEV - 1)), pltpu.SemaphoreType.DMA((n, N_DEV - 1)), pltpu.SemaphoreType.DMA((n,))]
    return dsts, sems


def _exchange_copies(kinds, srcs, dsts, send_sems, recv_sems, local_sems):
    x, y, c = lax.axis_index("x"), lax.axis_index("y"), lax.axis_index("c")
    me = 4 * x + 2 * y + c
    remote, local = [], []
    for a, kind in enumerate(kinds):
        for k in range(1, N_DEV):
            peer = (1 - x if k & 4 else x, 1 - y if k & 2 else y, 1 - c if k & 1 else c)
            src = srcs[a] if kind == "gather" else srcs[a].at[4 * peer[0] + 2 * peer[1] + peer[2]]
            remote.append(pltpu.make_async_remote_copy(
                src_ref=src, dst_ref=dsts[a].at[me], send_sem=send_sems.at[a, k - 1], recv_sem=recv_sems.at[a, k - 1],
                device_id=peer, device_id_type=MESH))
        src = srcs[a] if kind == "gather" else srcs[a].at[me]
        local.append(pltpu.make_async_copy(src, dsts[a].at[me], local_sems.at[a]))
    return remote, local


def _exchange(jobs, name):
    n = len(jobs)
    kinds = [kind for kind, _ in jobs]
    dst_shapes, sems = _exchange_shapes(jobs)

    def body(*refs):
        remote, local = _exchange_copies(kinds, refs[:n], refs[n:2 * n], *refs[2 * n:])
        for cp in local + remote:
            cp.start()
        for cp in remote + local:
            cp.wait()

    any_spec = pl.BlockSpec(memory_space=pl.ANY)
    return pl.pallas_call(
        body, name=name, out_shape=dst_shapes, in_specs=[any_spec] * n, out_specs=[any_spec] * n, scratch_shapes=sems,
    )(*[a for _, a in jobs])


def _pcall(body, *, name, grid, in_specs, out_specs, out_shape, args, scratch_shapes=(), exchange=None):
    semantics = ("arbitrary",) * len(grid)
    if not exchange:
        outs = pl.pallas_call(body, name=name, grid=grid, in_specs=list(in_specs), out_specs=list(out_specs),
                              out_shape=list(out_shape), scratch_shapes=list(scratch_shapes),
                              compiler_params=_cparams(semantics))(*args)
        return list(outs), []
    kinds = [kind for kind, _ in exchange]
    n_in, n_out, n_job = len(in_specs), len(out_specs), len(exchange)
    dst_shapes, sems = _exchange_shapes(exchange)

    def with_exchange(*refs):
        ins, srcs = refs[:n_in], refs[n_in:n_in + n_job]
        outs, dsts = refs[n_in + n_job:n_in + n_job + n_out], refs[n_in + n_job + n_out:n_in + 2 * n_job + n_out]
        scratch, sem_refs = refs[n_in + 2 * n_job + n_out:-3], refs[-3:]
        first = functools.reduce(jnp.logical_and, [pl.program_id(d) == 0 for d in range(len(grid))])
        last = functools.reduce(jnp.logical_and, [pl.program_id(d) == grid[d] - 1 for d in range(len(grid))])

        @pl.when(first)
        def _():
            remote, local = _exchange_copies(kinds, srcs, dsts, *sem_refs)
            for cp in local + remote:
                cp.start()

        body(*ins, *outs, *scratch)

        @pl.when(last)
        def _():
            remote, local = _exchange_copies(kinds, srcs, dsts, *sem_refs)
            for cp in remote + local:
                cp.wait()

    any_spec = pl.BlockSpec(memory_space=pl.ANY)
    outs = pl.pallas_call(
        with_exchange, name=name, grid=grid, in_specs=list(in_specs) + [any_spec] * n_job,
        out_specs=list(out_specs) + [any_spec] * n_job, out_shape=list(out_shape) + dst_shapes,
        scratch_shapes=list(scratch_shapes) + sems, compiler_params=_cparams(semantics),
    )(*args, *[a for _, a in exchange])
    return list(outs[:n_out]), list(outs[n_out:])


def _adam_math(w, g, m, v):
    m = ADAM_B1 * m + (1.0 - ADAM_B1) * g
    v = ADAM_B2 * v + (1.0 - ADAM_B2) * (g * g)
    m_hat = m / (1.0 - ADAM_B1 ** ADAM_STEP)
    v_hat = v / (1.0 - ADAM_B2 ** ADAM_STEP)
    delta = -ADAM_LR * (m_hat / (jnp.sqrt(v_hat) + ADAM_EPS) + ADAM_WD * w)
    return delta, m, v


def _adam_big(w, m, v, own32, recv16, me):
    r, c = w.shape
    tr = r if r <= 512 else (512 if r % 512 == 0 else 352)
    assert r % tr == 0

    def body(me_ref, w_ref, m_ref, v_ref, s_ref, r_ref, g_out, d_out, m_out, v_out):
        g = s_ref[...]
        for k in range(1, N_DEV):
            g = g + r_ref[jnp.bitwise_xor(me_ref[0], k)].astype(F32)
        d, mn, vn = _adam_math(w_ref[...], g, m_ref[...], v_ref[...])
        g_out[...] = g
        d_out[...] = d
        m_out[...] = mn
        v_out[...] = vn

    blk = pl.BlockSpec((tr, c), lambda i, s: (i, 0))
    grid_spec = pltpu.PrefetchScalarGridSpec(
        num_scalar_prefetch=1, grid=(r // tr,),
        in_specs=[blk, blk, blk, blk, pl.BlockSpec((N_DEV, tr, c), lambda i, s: (0, i, 0))],
        out_specs=[blk, blk, blk, blk])
    return pl.pallas_call(
        body, name="adamw_big", grid_spec=grid_spec,
        out_shape=[jax.ShapeDtypeStruct((r, c), F32)] * 4,
        compiler_params=_cparams(("arbitrary",)),
    )(me, w, m, v, own32, recv16)


def _adam_small(ws, ms, vs, partials, loss_parts):
    n = len(ws)

    def body(*refs):
        w_refs, m_refs, v_refs, p_refs = (refs[i * n:(i + 1) * n] for i in range(4))
        l_ref, loss_out = refs[4 * n], refs[4 * n + 1]
        outs = refs[4 * n + 2:]

        def total(ref):
            s = ref[0]
            for dev in range(1, N_DEV):
                s = s + ref[dev]
            return s

        loss_out[...] = total(l_ref)
        for i in range(n):
            r, c = w_refs[i].shape
            g = total(p_refs[i])[:, :c]
            d, mn, vn = _adam_math(w_refs[i][...], g, m_refs[i][...], v_refs[i][...])
            for j, val in enumerate((g, d, mn, vn)):
                outs[4 * i + j][...] = val

    shapes = [jax.ShapeDtypeStruct((1, 128), F32)]
    for w in ws:
        shapes += [jax.ShapeDtypeStruct(w.shape, F32)] * 4
    outs = pl.pallas_call(body, name="adamw_small", out_shape=shapes, compiler_params=_cparams())(*ws, *ms, *vs, *partials, loss_parts)
    return outs[0], [outs[1 + 4 * i:5 + 4 * i] for i in range(n)]


def _pick(n, prefer):
    for t in prefer:
        if n % t == 0:
            return t
    return n


def _ffn_fwd(x, g, wg, wu, wd, name, exchange=None):
    T, D = x.shape
    F = wg.shape[1]
    tm = _pick(T, (512, 256, 128))
    fc = _pick(F, (256, 128))

    def body(x_ref, g_ref, wg_ref, wu_ref, wd_ref, h_ref, n_ref, gate_ref, up_ref, act_ref):
        xv = x_ref[...]
        n, _ = _rms_fwd(xv, g_ref[...])
        nb = n.astype(BF16)
        n_ref[...] = nb
        acc = jnp.zeros((tm, D), F32)
        for f0 in range(0, F, fc):
            gate = _dot(nb, wg_ref[:, f0:f0 + fc])
            up = _dot(nb, wu_ref[:, f0:f0 + fc])
            gate_ref[:, f0:f0 + fc] = gate.astype(BF16)
            up_ref[:, f0:f0 + fc] = up.astype(BF16)
            act = (gate * (1.0 / (1.0 + jnp.exp(-gate))) * up).astype(BF16)
            act_ref[:, f0:f0 + fc] = act
            acc = acc + _dot(act, wd_ref[f0:f0 + fc, :])
        h_ref[...] = xv + 0.5 * acc

    return _pcall(
        body, name=name, grid=(T // tm,),
        in_specs=[_row_spec(tm, D), _const_spec((1, D)), _const_spec((D, F)), _const_spec((D, F)), _const_spec((F, D))],
        out_specs=[_row_spec(tm, D), _row_spec(tm, D), _row_spec(tm, F), _row_spec(tm, F), _row_spec(tm, F)],
        out_shape=[jax.ShapeDtypeStruct((T, D), F32), jax.ShapeDtypeStruct((T, D), BF16),
                   jax.ShapeDtypeStruct((T, F), BF16), jax.ShapeDtypeStruct((T, F), BF16), jax.ShapeDtypeStruct((T, F), BF16)],
        args=(x, g, wg, wu, wd), exchange=exchange)


def _ffn_bwd(x, g, gate, up, dh, wg, wu, wd, name, exchange=None):
    T, D = x.shape
    F = wg.shape[1]
    tm = _pick(T, (512, 256, 128))
    halves = 2
    Fh = F // halves
    assert Fh * halves == F and Fh % 128 == 0
    chunks = [(f0, min(f0 + 512, Fh)) for f0 in range(0, Fh, 512)]

    def body(x_ref, g_ref, gate_ref, up_ref, dh_ref, wg_ref, wu_ref, wd_ref, dx_ref, dg_ref, dgate_ref, dup_ref, dn_ref):
        i, h = pl.program_id(0), pl.program_id(1)
        dhv = dh_ref[...]
        dho = (0.5 * dhv).astype(BF16)
        dn = jnp.zeros((tm, D), F32)
        for f0, f1 in chunks:
            d_act = _dot_nt(dho, wd_ref[f0:f1, :])
            gt = gate_ref[:, f0:f1].astype(F32)
            u = up_ref[:, f0:f1].astype(F32)
            sg = 1.0 / (1.0 + jnp.exp(-gt))
            silu = gt * sg
            d_up = (d_act * silu).astype(BF16)
            d_gate = (d_act * u * (sg + silu * (1.0 - sg))).astype(BF16)
            dup_ref[:, f0:f1] = d_up
            dgate_ref[:, f0:f1] = d_gate
            dn = dn + _dot_nt(d_gate, wg_ref[:, f0:f1]) + _dot_nt(d_up, wu_ref[:, f0:f1])

        @pl.when(h == 0)
        def _():
            dn_ref[...] = dn

        @pl.when(h > 0)
        def _():
            dn_ref[...] += dn

        @pl.when((i == 0) & (h == 0))
        def _():
            dg_ref[...] = jnp.zeros_like(dg_ref)

        @pl.when(h == halves - 1)
        def _():
            xv, gv = x_ref[...], g_ref[...]
            _, r = _rms_fwd(xv, gv)
            dxn, dgp = _rms_bwd(xv, gv, r, dn_ref[...])
            dx_ref[...] = dhv + dxn
            dg_ref[...] += dgp

    tile = pl.BlockSpec((tm, D), lambda i, h: (i, 0))
    wide = pl.BlockSpec((tm, Fh), lambda i, h: (i, h))
    return _pcall(
        body, name=name, grid=(T // tm, halves),
        in_specs=[tile, _const_spec((1, D)), wide, wide, tile,
                  pl.BlockSpec((D, Fh), lambda i, h: (0, h)), pl.BlockSpec((D, Fh), lambda i, h: (0, h)),
                  pl.BlockSpec((Fh, D), lambda i, h: (h, 0))],
        out_specs=[tile, _acc_spec((1, D)), wide, wide],
        out_shape=[jax.ShapeDtypeStruct((T, D), F32), jax.ShapeDtypeStruct((1, D), F32),
                   jax.ShapeDtypeStruct((T, F), BF16), jax.ShapeDtypeStruct((T, F), BF16)],
        scratch_shapes=[pltpu.VMEM((tm, D), F32)],
        args=(x, g, gate, up, dh, wg, wu, wd), exchange=exchange)


def _matmul_tn(a, b, name, scale=None, exchange=None):
    T, K = a.shape
    N = b.shape[1]
    tk = _pick(K, (1024, 1408, 512, 256, 128))
    tn = _pick(N, (1024, 1408, 1280, 768, 512, 256, 128))
    tt = _pick(T, (2048, 1024, 512, 256, 128))
    n_t = T // tt

    def body(a_ref, b_ref, o_ref):
        @pl.when(pl.program_id(2) == 0)
        def _():
            o_ref[...] = jnp.zeros_like(o_ref)

        o_ref[...] += _dot_tn(a_ref[...].astype(BF16), b_ref[...].astype(BF16))

        if scale is not None:
            @pl.when(pl.program_id(2) == n_t - 1)
            def _():
                o_ref[...] = o_ref[...] * scale

    outs, got = _pcall(
        body, name=name, grid=(K // tk, N // tn, n_t),
        in_specs=[pl.BlockSpec((tt, tk), lambda i, j, t: (t, i)), pl.BlockSpec((tt, tn), lambda i, j, t: (t, j))],
        out_specs=[pl.BlockSpec((tk, tn), lambda i, j, t: (i, j))],
        out_shape=[jax.ShapeDtypeStruct((K, N), F32)],
        args=(a, b), exchange=exchange)
    return outs[0], got


def _rope_spec(tm, S):
    per = S // tm
    return pl.BlockSpec((tm, HEAD_BLOCK), lambda i: (i % per, 0))


def _proj_fwd(h1, g_mix, w_in, g_q, wq, g_kv, wkv, rope_c, rope_s, S):
    T, D = h1.shape
    P = w_in.shape[1]
    tm = _pick(S, (512, 256, 128))
    QW = MLA_HEADS * HEAD_BLOCK
    VW = MLA_HEADS * V_HEAD
    SQ = SWA_HEADS * SWA_HEAD_DIM
    SK = SWA_KV_HEADS * SWA_HEAD_DIM
    o_cq, o_ckv, o_qs, o_ks, o_vs, o_kpe = 0, Q_LORA, Q_LORA + KV_LORA, Q_LORA + KV_LORA + SQ, Q_LORA + KV_LORA + SQ + SK, Q_LORA + KV_LORA + SQ + 2 * SK
    assert P == o_kpe + HEAD_BLOCK

    def body(h_ref, gm_ref, win_ref, gq_ref, wq_ref, gkv_ref, wkv_ref, c_ref, s_ref,
             u_ref, cq_ref, ckv_ref, cqn_ref, ckvn_ref, q_ref, k_ref, v_ref, qs_ref, ks_ref, vs_ref):
        u, _ = _rms_fwd(h_ref[...], gm_ref[...])
        ub = u.astype(BF16)
        u_ref[...] = ub
        proj = _dot(ub, win_ref[...])
        c_q = proj[:, o_cq:o_ckv]
        c_kv = proj[:, o_ckv:o_qs]
        cq_ref[...] = c_q
        ckv_ref[...] = c_kv
        qs_ref[...] = proj[:, o_qs:o_ks].astype(BF16)
        ks_ref[...] = proj[:, o_ks:o_vs].astype(BF16)
        vs_ref[...] = proj[:, o_vs:o_kpe].astype(BF16)
        cb, sb = c_ref[...], s_ref[...]
        kpe = proj[:, o_kpe:P]
        kpe = kpe * cb + _swap16(kpe) * sb
        cqn, _ = _rms_fwd(c_q, gq_ref[...])
        cqn = cqn.astype(BF16)
        cqn_ref[...] = cqn
        q = _dot(cqn, wq_ref[...])
        q = q * jnp.tile(cb, (1, MLA_HEADS)) + _swap16(q) * jnp.tile(sb, (1, MLA_HEADS))
        q_ref[...] = q.astype(BF16)
        ckvn, _ = _rms_fwd(c_kv, gkv_ref[...])
        ckvn = ckvn.astype(BF16)
        ckvn_ref[...] = ckvn
        kv = _dot(ckvn, wkv_ref[...])
        k_ref[...] = (kv[:, :QW] + jnp.tile(kpe, (1, MLA_HEADS))).astype(BF16)
        v_ref[...] = kv[:, QW:].astype(BF16)

    widths = [(D, BF16), (Q_LORA, F32), (KV_LORA, F32), (Q_LORA, BF16), (KV_LORA, BF16), (QW, BF16), (QW, BF16), (VW, BF16),
              (SQ, BF16), (SK, BF16), (SK, BF16)]
    return pl.pallas_call(
        body, name="proj_fwd", grid=(T // tm,),
        in_specs=[_row_spec(tm, D), _const_spec((1, D)), _const_spec((D, P)), _const_spec((1, Q_LORA)),
                  _const_spec((Q_LORA, QW)), _const_spec((1, KV_LORA)), _const_spec((KV_LORA, QW + VW)),
                  _rope_spec(tm, S), _rope_spec(tm, S)],
        out_specs=[_row_spec(tm, w) for w, _ in widths],
        out_shape=[jax.ShapeDtypeStruct((T, w), dt) for w, dt in widths],
        compiler_params=_cparams(("arbitrary",)),
    )(h1, g_mix, w_in, g_q, wq, g_kv, wkv, rope_c, rope_s)


def _proj_bwd(dq, dk, dv, dqs, dks, dvs, c_q, c_kv, h1, dh2, g_mix, w_in, g_q, wq, g_kv, wkv, rope_c, rope_s, S):
    T, D = h1.shape
    P = w_in.shape[1]
    tm = _pick(S, (256, 128))
    QW = MLA_HEADS * HEAD_BLOCK
    VW = MLA_HEADS * V_HEAD
    SQ = SWA_HEADS * SWA_HEAD_DIM
    SK = SWA_KV_HEADS * SWA_HEAD_DIM

    def body(dq_ref, dk_ref, dv_ref, dqs_ref, dks_ref, dvs_ref, cq_ref, ckv_ref, h_ref, dh2_ref, gm_ref, win_ref, gq_ref,
             wq_ref, gkv_ref, wkv_ref, c_ref, s_ref, dh1_ref, dproj_ref, dql_ref, dkv_ref, dgm_ref, dgq_ref, dgkv_ref):
        cb, sb = c_ref[...], s_ref[...]
        dqv = dq_ref[...]
        dql = dqv * jnp.tile(cb, (1, MLA_HEADS)) + _swap16(dqv * jnp.tile(sb, (1, MLA_HEADS)))
        dql = dql.astype(BF16)
        dql_ref[...] = dql
        c_q = cq_ref[...]
        _, rq = _rms_fwd(c_q, gq_ref[...])
        d_cq, dgq = _rms_bwd(c_q, gq_ref[...], rq, _dot_nt(dql, wq_ref[...]))

        dkv_all = dk_ref[...]
        dkpe = dkv_all[:, 0:HEAD_BLOCK]
        for h in range(1, MLA_HEADS):
            dkpe = dkpe + dkv_all[:, h * HEAD_BLOCK:(h + 1) * HEAD_BLOCK]
        lane = lax.broadcasted_iota(jnp.int32, dkpe.shape, 1)
        dkpe = jnp.where((lane >= 64) & (lane < 96), dkpe, 0.0)
        dkpe = dkpe * cb + _swap16(dkpe * sb)
        dkvc = jnp.concatenate([dkv_all.astype(BF16), dv_ref[...].astype(BF16)], axis=1)
        dkv_ref[...] = dkvc
        c_kv = ckv_ref[...]
        _, rkv = _rms_fwd(c_kv, gkv_ref[...])
        d_ckv, dgkv = _rms_bwd(c_kv, gkv_ref[...], rkv, _dot_nt(dkvc, wkv_ref[...]))

        dproj = jnp.concatenate([d_cq.astype(BF16), d_ckv.astype(BF16), dqs_ref[...].astype(BF16),
                                 dks_ref[...].astype(BF16), dvs_ref[...].astype(BF16), dkpe.astype(BF16)], axis=1)
        dproj_ref[...] = dproj
        hv = h_ref[...]
        _, rm = _rms_fwd(hv, gm_ref[...])
        dxn, dgm = _rms_bwd(hv, gm_ref[...], rm, _dot_nt(dproj, win_ref[...]))
        dh1_ref[...] = dh2_ref[...] + dxn

        @pl.when(pl.program_id(0) == 0)
        def _():
            dgm_ref[...] = jnp.zeros_like(dgm_ref)
            dgq_ref[...] = jnp.zeros_like(dgq_ref)
            dgkv_ref[...] = jnp.zeros_like(dgkv_ref)

        dgm_ref[...] += dgm
        dgq_ref[...] += dgq
        dgkv_ref[...] += dgkv

    return pl.pallas_call(
        body, name="proj_bwd", grid=(T // tm,),
        in_specs=[_row_spec(tm, QW), _row_spec(tm, QW), _row_spec(tm, VW), _row_spec(tm, SQ), _row_spec(tm, SK), _row_spec(tm, SK),
                  _row_spec(tm, Q_LORA), _row_spec(tm, KV_LORA), _row_spec(tm, D), _row_spec(tm, D),
                  _const_spec((1, D)), _const_spec((D, P)), _const_spec((1, Q_LORA)), _const_spec((Q_LORA, QW)),
                  _const_spec((1, KV_LORA)), _const_spec((KV_LORA, QW + VW)), _rope_spec(tm, S), _rope_spec(tm, S)],
        out_specs=[_row_spec(tm, D), _row_spec(tm, P), _row_spec(tm, QW), _row_spec(tm, QW + VW),
                   _acc_spec((1, D)), _acc_spec((1, Q_LORA)), _acc_spec((1, KV_LORA))],
        out_shape=[jax.ShapeDtypeStruct((T, D), F32), jax.ShapeDtypeStruct((T, P), BF16), jax.ShapeDtypeStruct((T, QW), BF16),
                   jax.ShapeDtypeStruct((T, QW + VW), BF16), jax.ShapeDtypeStruct((1, D), F32),
                   jax.ShapeDtypeStruct((1, Q_LORA), F32), jax.ShapeDtypeStruct((1, KV_LORA), F32)],
        compiler_params=_cparams(("arbitrary",)),
    )(dq, dk, dv, dqs, dks, dvs, c_q, c_kv, h1, dh2, g_mix, w_in, g_q, wq, g_kv, wkv, rope_c, rope_s)


def _out_fwd(o_mla, o_swa, g_mla, g_swa, w_o, h1):
    T, D = h1.shape
    W = o_mla.shape[1]
    tm = _pick(T, (512, 256, 128))

    def body(om_ref, os_ref, gm_ref, gs_ref, wo_ref, h_ref, h2_ref, oc_ref):
        a, _ = _rms_fwd(om_ref[...], gm_ref[...])
        b, _ = _rms_fwd(os_ref[...], gs_ref[...])
        oc = jnp.concatenate([a.astype(BF16), b.astype(BF16)], axis=1)
        oc_ref[...] = oc
        h2_ref[...] = h_ref[...] + _dot(oc, wo_ref[...])

    return pl.pallas_call(
        body, name="out_fwd", grid=(T // tm,),
        in_specs=[_row_spec(tm, W), _row_spec(tm, W), _const_spec((1, W)), _const_spec((1, W)), _const_spec((2 * W, D)),
                  _row_spec(tm, D)],
        out_specs=[_row_spec(tm, D), _row_spec(tm, 2 * W)],
        out_shape=[jax.ShapeDtypeStruct((T, D), F32), jax.ShapeDtypeStruct((T, 2 * W), BF16)],
        compiler_params=_cparams(("arbitrary",)),
    )(o_mla, o_swa, g_mla, g_swa, w_o, h1)


def _out_bwd(dh2, o_mla, o_swa, g_mla, g_swa, w_o):
    T, D = dh2.shape
    W = o_mla.shape[1]
    tm = _pick(T, (512, 256, 128))

    def body(dh_ref, om_ref, os_ref, gm_ref, gs_ref, wo_ref, dom_ref, dos_ref, dgm_ref, dgs_ref):
        doc = _dot_nt(dh_ref[...].astype(BF16), wo_ref[...])
        om, osw = om_ref[...], os_ref[...]
        _, ra = _rms_fwd(om, gm_ref[...])
        _, rb = _rms_fwd(osw, gs_ref[...])
        da, dga = _rms_bwd(om, gm_ref[...], ra, doc[:, :W])
        db, dgb = _rms_bwd(osw, gs_ref[...], rb, doc[:, W:])
        dom_ref[...] = da
        dos_ref[...] = db

        @pl.when(pl.program_id(0) == 0)
        def _():
            dgm_ref[...] = jnp.zeros_like(dgm_ref)
            dgs_ref[...] = jnp.zeros_like(dgs_ref)

        dgm_ref[...] += dga
        dgs_ref[...] += dgb

    return pl.pallas_call(
        body, name="out_bwd", grid=(T // tm,),
        in_specs=[_row_spec(tm, D), _row_spec(tm, W), _row_spec(tm, W), _const_spec((1, W)), _const_spec((1, W)),
                  _const_spec((2 * W, D))],
        out_specs=[_row_spec(tm, W), _row_spec(tm, W), _acc_spec((1, W)), _acc_spec((1, W))],
        out_shape=[jax.ShapeDtypeStruct((T, W), F32), jax.ShapeDtypeStruct((T, W), F32),
                   jax.ShapeDtypeStruct((1, W), F32), jax.ShapeDtypeStruct((1, W), F32)],
        compiler_params=_cparams(("arbitrary",)),
    )(dh2, o_mla, o_swa, g_mla, g_swa, w_o)


def _loss_head(h3, target, g_final):
    T, D = h3.shape
    tm = _pick(T, (512, 256, 128))

    def body(h_ref, t_ref, g_ref, dh_ref, loss_ref, dg_ref):
        hv, gv = h_ref[...], g_ref[...]
        y, r = _rms_fwd(hv, gv)
        diff = y - t_ref[...]
        dxn, dgp = _rms_bwd(hv, gv, r, diff * (1.0 / D))
        dh_ref[...] = dxn
        part = 0.5 * jnp.sum(jnp.sum(diff * diff, axis=-1, keepdims=True) * (1.0 / D), axis=0, keepdims=True)

        @pl.when(pl.program_id(0) == 0)
        def _():
            loss_ref[...] = jnp.zeros_like(loss_ref)
            dg_ref[...] = jnp.zeros_like(dg_ref)

        loss_ref[...] += jnp.broadcast_to(part, loss_ref.shape)
        dg_ref[...] += dgp

    return pl.pallas_call(
        body, name="loss_head", grid=(T // tm,),
        in_specs=[_row_spec(tm, D), _row_spec(tm, D), _const_spec((1, D))],
        out_specs=[_row_spec(tm, D), _acc_spec((1, 128)), _acc_spec((1, D))],
        out_shape=[jax.ShapeDtypeStruct((T, D), F32), jax.ShapeDtypeStruct((1, 128), F32), jax.ShapeDtypeStruct((1, D), F32)],
        compiler_params=_cparams(("arbitrary",)),
    )(h3, target, g_final)


def _half_mask(shape, half):
    lane = lax.broadcasted_iota(jnp.int32, shape, len(shape) - 1)
    return (lane < 64) if half == 0 else (lane >= 64)


def _mla_fwd(q, k, v, B, S, exchange=None):
    T = B * S
    tq = _pick(S, MLA_Q_TILES)
    nq = S // tq
    pairs = MLA_HEADS // 2

    def body(q_ref, k_ref, v_ref, o_ref, lse_ref):
        causal = lax.broadcasted_iota(jnp.int32, (tq, tq), 1) <= lax.broadcasted_iota(jnp.int32, (tq, tq), 0)
        low = _half_mask((tq, 2 * V_HEAD), 0)
        for qi in range(nq):
            rows = slice(qi * tq, (qi + 1) * tq)
            past = slice(0, qi * tq)
            outs, lses = [], []
            for half in range(2):
                lanes = slice(half * HEAD_BLOCK, (half + 1) * HEAD_BLOCK)
                qv = q_ref[rows, lanes]
                s_d = jnp.where(causal, _dot_nt(qv, k_ref[rows, lanes]) * MLA_SCALE, NEG)
                m = jnp.max(s_d, axis=-1, keepdims=True)
                if qi:
                    s_p = _dot_nt(qv, k_ref[past, lanes]) * MLA_SCALE
                    m = jnp.maximum(m, jnp.max(s_p, axis=-1, keepdims=True))
                p_d = jnp.exp(s_d - m)
                l = jnp.sum(p_d, axis=-1, keepdims=True)
                acc = _dot(p_d.astype(BF16), v_ref[rows, :])
                if qi:
                    p_p = jnp.exp(s_p - m)
                    l = l + jnp.sum(p_p, axis=-1, keepdims=True)
                    acc = acc + _dot(p_p.astype(BF16), v_ref[past, :])
                outs.append(acc * (1.0 / l))
                lses.append(jnp.broadcast_to(m + jnp.log(l), (tq, 2 * V_HEAD)))
            o_ref[rows, :] = jnp.where(low, outs[0], outs[1])
            lse_ref[rows, :] = jnp.where(low, lses[0], lses[1])

    blk2 = pl.BlockSpec((S, 2 * HEAD_BLOCK), lambda b, j: (b, j))
    blk1 = pl.BlockSpec((S, 2 * V_HEAD), lambda b, j: (b, j))
    return _pcall(
        body, name="mla_fwd", grid=(B, pairs),
        in_specs=[blk2, blk2, blk1], out_specs=[blk1, blk1],
        out_shape=[jax.ShapeDtypeStruct((T, pairs * 2 * V_HEAD), F32)] * 2,
        args=(q, k, v), exchange=exchange)


def _mla_bwd(q, k, v, do, o, lse, B, S, exchange=None):
    T = B * S
    tq = _pick(S, MLA_Q_TILES)
    nq = S // tq
    pairs = MLA_HEADS // 2

    def body(q_ref, k_ref, v_ref, do_ref, o_ref, lse_ref, dq_ref, dk_ref, dv_ref, dk_acc):
        causal = lax.broadcasted_iota(jnp.int32, (tq, tq), 1) <= lax.broadcasted_iota(jnp.int32, (tq, tq), 0)
        for half in range(2):
            lanes = slice(half * HEAD_BLOCK, (half + 1) * HEAD_BLOCK)
            col = half * V_HEAD
            own = _half_mask((tq, 2 * V_HEAD), half)
            for qi in reversed(range(nq)):
                rows = slice(qi * tq, (qi + 1) * tq)
                past = slice(0, qi * tq)
                first = qi == nq - 1
                qv = q_ref[rows, lanes]
                dov = jnp.where(own, do_ref[rows, :], 0.0)
                dsum = jnp.sum(dov * o_ref[rows, :], axis=-1, keepdims=True)
                dob = dov.astype(BF16)
                lse_col = lse_ref[rows, col:col + 1]

                def grads(keys, mask):
                    kv, vv = k_ref[keys, lanes], v_ref[keys, :]
                    p = jnp.exp(_dot_nt(qv, kv) * MLA_SCALE - lse_col)
                    if mask is not None:
                        p = jnp.where(mask, p, 0.0)
                    ds = (p * (_dot_nt(dob, vv) - dsum) * MLA_SCALE).astype(BF16)
                    dk, dv = _dot_tn(ds, qv), _dot_tn(p.astype(BF16), dob)
                    if first:
                        dk_acc[keys, :] = dk
                    else:
                        dk_acc[keys, :] += dk
                    if first and half == 0:
                        dv_ref[keys, :] = dv
                    else:
                        dv_ref[keys, :] += dv
                    return _dot(ds, kv)

                dq = grads(rows, causal)
                if qi:
                    dq = dq + grads(past, None)
                dq_ref[rows, lanes] = dq
            dk_ref[:, lanes] = dk_acc[...]

    blk2 = pl.BlockSpec((S, 2 * HEAD_BLOCK), lambda b, j: (b, j))
    blk1 = pl.BlockSpec((S, 2 * V_HEAD), lambda b, j: (b, j))
    return _pcall(
        body, name="mla_bwd", grid=(B, pairs),
        in_specs=[blk2, blk2, blk1, blk1, blk1, blk1], out_specs=[blk2, blk2, blk1],
        out_shape=[jax.ShapeDtypeStruct((T, MLA_HEADS * HEAD_BLOCK), F32), jax.ShapeDtypeStruct((T, MLA_HEADS * HEAD_BLOCK), F32),
                   jax.ShapeDtypeStruct((T, pairs * 2 * V_HEAD), F32)],
        scratch_shapes=[pltpu.VMEM((S, HEAD_BLOCK), F32)],
        args=(q, k, v, do, o, lse), exchange=exchange)


SWA_GROUP = SWA_HEADS // SWA_KV_HEADS
SWA_ROWS = SWA_GROUP * WINDOW


SWA_CHUNK = 8


def _bdot(a, b, ca, cb):
    return lax.dot_general(a, b, (((ca,), (cb,)), ((0,), (0,))), preferred_element_type=F32)


def _swa_chunks(nb):
    cb = next(c for c in (SWA_CHUNK, 4, 2, 1) if nb % c == 0)
    return cb, [(n0, slice(n0 * WINDOW, (n0 + cb) * WINDOW)) for n0 in range(0, nb, cb)]


def _swa_keys(ref, n0, cb):
    W = WINDOW
    cur = ref[n0 * W:(n0 + cb) * W, :]
    if n0:
        prev = ref[(n0 - 1) * W:(n0 + cb - 1) * W, :]
    else:
        prev = jnp.concatenate([cur[:W], cur[:(cb - 1) * W]], axis=0) if cb > 1 else cur
    return jnp.concatenate([prev.reshape(cb, W, W), cur.reshape(cb, W, W)], axis=1)


def _swa_stack(ref, rows, cb):
    return jnp.concatenate([ref[rows, j * WINDOW:(j + 1) * WINDOW].reshape(cb, WINDOW, WINDOW) for j in range(SWA_GROUP)], axis=1)


def _swa_unstack(low, high, cb):
    sel = _half_mask((cb, WINDOW, WINDOW), 0)
    return jnp.concatenate([jnp.where(sel, low[:, j * WINDOW:(j + 1) * WINDOW], high[:, j * WINDOW:(j + 1) * WINDOW])
                            .reshape(cb * WINDOW, WINDOW) for j in range(SWA_GROUP)], axis=1)


def _swa_sink_col(sink_ref, g):
    return jnp.concatenate([jnp.broadcast_to(sink_ref[g * SWA_GROUP + j:g * SWA_GROUP + j + 1, 0:1], (WINDOW, 1))
                            for j in range(SWA_GROUP)], axis=0)


def _swa_probs(qm, kcat, bias_g, sink, first_chunk):
    shape = (qm.shape[0], SWA_ROWS, 2 * WINDOW)
    qi = lax.broadcasted_iota(jnp.int32, shape, 1) % WINDOW
    col = lax.broadcasted_iota(jnp.int32, shape, 2)
    valid = (col > qi) & (col <= qi + WINDOW)
    if first_chunk:
        valid = valid & ((col >= WINDOW) | (lax.broadcasted_iota(jnp.int32, shape, 0) > 0))
    s = jnp.where(valid, _bdot(qm, kcat, 2, 2) * SWA_SCALE + bias_g, NEG)
    m = jnp.maximum(jnp.max(s, axis=-1, keepdims=True), sink)
    e = jnp.exp(s - m)
    e_s = jnp.exp(sink - m)
    inv = 1.0 / (jnp.sum(e, axis=-1, keepdims=True) + e_s)
    return e * inv, e_s * inv


def _swa_fwd(q, k, v, bias, sinks, B, S):
    T = B * S
    W = WINDOW
    nb = S // W

    cb, chunks = _swa_chunks(nb)

    def body(q_ref, k_ref, v_ref, bias_ref, sink_ref, o_ref):
        sink_cols = [_swa_sink_col(sink_ref, g) for g in range(SWA_KV_HEADS)]
        for n0, rows in chunks:
            kcat, vcat = _swa_keys(k_ref, n0, cb), _swa_keys(v_ref, n0, cb)
            qcat = _swa_stack(q_ref, rows, cb)
            outs = []
            for g in range(SWA_KV_HEADS):
                qm = jnp.where(_half_mask(qcat.shape, g), qcat, jnp.zeros_like(qcat))
                p, _ = _swa_probs(qm, kcat, bias_ref[g], sink_cols[g], n0 == 0)
                outs.append(_bdot(p.astype(BF16), vcat, 2, 1))
            o_ref[rows, :] = _swa_unstack(outs[0], outs[1], cb)

    SQ, SK = SWA_HEADS * SWA_HEAD_DIM, SWA_KV_HEADS * SWA_HEAD_DIM
    return pl.pallas_call(
        body, name="swa_fwd", grid=(B,),
        in_specs=[pl.BlockSpec((S, SQ), lambda b: (b, 0)), pl.BlockSpec((S, SK), lambda b: (b, 0)),
                  pl.BlockSpec((S, SK), lambda b: (b, 0)), _const_spec((SWA_KV_HEADS, SWA_ROWS, 2 * W)),
                  _const_spec((SWA_HEADS, 128))],
        out_specs=pl.BlockSpec((S, SQ), lambda b: (b, 0)),
        out_shape=jax.ShapeDtypeStruct((T, SQ), F32),
        compiler_params=_cparams(("arbitrary",)),
    )(q, k, v, bias, sinks)


def _swa_bwd(q, k, v, do, bias, sinks, B, S):
    T = B * S
    W = WINDOW
    nb = S // W
    cb, chunks = _swa_chunks(nb)

    def body(q_ref, k_ref, v_ref, do_ref, bias_ref, sink_ref, dq_ref, dk_ref, dv_ref, dbias_ref, dsink_ref):
        @pl.when(pl.program_id(0) == 0)
        def _():
            dbias_ref[...] = jnp.zeros_like(dbias_ref)
            dsink_ref[...] = jnp.zeros_like(dsink_ref)

        dk_ref[...] = jnp.zeros_like(dk_ref)
        dv_ref[...] = jnp.zeros_like(dv_ref)
        sink_cols = [_swa_sink_col(sink_ref, g) for g in range(SWA_KV_HEADS)]
        for n0, rows in chunks:
            kcat, vcat = _swa_keys(k_ref, n0, cb), _swa_keys(v_ref, n0, cb)
            qcat = _swa_stack(q_ref, rows, cb)
            docat = _swa_stack(do_ref, rows, cb)
            dk = jnp.zeros((cb, 2 * W, W), F32)
            dv = jnp.zeros((cb, 2 * W, W), F32)
            dqs = []
            for g in range(SWA_KV_HEADS):
                own = _half_mask(qcat.shape, g)
                qm = jnp.where(own, qcat, jnp.zeros_like(qcat))
                dom = jnp.where(own, docat, 0.0).astype(BF16)
                p, p_s = _swa_probs(qm, kcat, bias_ref[g], sink_cols[g], n0 == 0)
                dp = _bdot(dom, vcat, 2, 2)
                dsum = jnp.sum(p * dp, axis=-1, keepdims=True)
                ds = p * (dp - dsum)
                dbias_ref[g] += jnp.sum(ds, axis=0)
                dsink_ref[g] += jnp.broadcast_to(-jnp.sum(p_s * dsum, axis=0), (SWA_ROWS, 128))
                dsb = (ds * SWA_SCALE).astype(BF16)
                dqs.append(_bdot(dsb, kcat, 2, 1))
                dk = dk + _bdot(dsb, qm, 1, 1)
                dv = dv + _bdot(p.astype(BF16), dom, 1, 1)
            dq_ref[rows, :] = _swa_unstack(dqs[0], dqs[1], cb)
            dk_ref[rows, :] += dk[:, W:].reshape(cb * W, W)
            dv_ref[rows, :] += dv[:, W:].reshape(cb * W, W)
            dk_prev, dv_prev = dk[:, :W].reshape(cb * W, W), dv[:, :W].reshape(cb * W, W)
            if n0:
                before = slice((n0 - 1) * W, (n0 + cb - 1) * W)
                dk_ref[before, :] += dk_prev
                dv_ref[before, :] += dv_prev
            elif cb > 1:
                before = slice(0, (cb - 1) * W)
                dk_ref[before, :] += dk_prev[W:]
                dv_ref[before, :] += dv_prev[W:]

    SQ, SK = SWA_HEADS * SWA_HEAD_DIM, SWA_KV_HEADS * SWA_HEAD_DIM
    row = lambda w: pl.BlockSpec((S, w), lambda b: (b, 0))
    return pl.pallas_call(
        body, name="swa_bwd", grid=(B,),
        in_specs=[row(SQ), row(SK), row(SK), row(SQ), _const_spec((SWA_KV_HEADS, SWA_ROWS, 2 * W)), _const_spec((SWA_HEADS, 128))],
        out_specs=[row(SQ), row(SK), row(SK), _acc_spec((SWA_KV_HEADS, SWA_ROWS, 2 * W)), _acc_spec((SWA_KV_HEADS, SWA_ROWS, 128))],
        out_shape=[jax.ShapeDtypeStruct((T, SQ), F32), jax.ShapeDtypeStruct((T, SK), F32), jax.ShapeDtypeStruct((T, SK), F32),
                   jax.ShapeDtypeStruct((SWA_KV_HEADS, SWA_ROWS, 2 * W), F32), jax.ShapeDtypeStruct((SWA_KV_HEADS, SWA_ROWS, 128), F32)],
        compiler_params=_cparams(("arbitrary",)),
    )(q, k, v, do, bias, sinks)


def _bias_expand(rel_bias, bucket):
    W = WINDOW

    def body(rb_ref, bucket_ref, out_ref):
        bk = bucket_ref[...]
        for h in range(SWA_HEADS):
            def add(b, acc):
                return jnp.where(bk == b, rb_ref[b, h], acc)

            out_ref[h] = lax.fori_loop(0, REL_BUCKETS, add, jnp.zeros((W, 2 * W), F32))

    return pl.pallas_call(
        body, name="bias_expand",
        in_specs=[pl.BlockSpec(memory_space=pltpu.SMEM), pl.BlockSpec(memory_space=pltpu.VMEM)],
        out_specs=pl.BlockSpec(memory_space=pltpu.VMEM),
        out_shape=jax.ShapeDtypeStruct((SWA_HEADS, W, 2 * W), F32),
        compiler_params=_cparams(),
    )(rel_bias, bucket)


def _bias_reduce(dbias, dsink_rows, bucket):
    def body(db_ref, ds_ref, bucket_ref, out_ref, sink_out_ref):
        sink_lane = lax.broadcasted_iota(jnp.int32, (1, 128), 1)
        sink_acc = jnp.zeros((1, 128), F32)
        for h in range(SWA_HEADS):
            sink_acc = jnp.where(sink_lane == h, jnp.sum(ds_ref[h], axis=0, keepdims=True), sink_acc)
        sink_out_ref[...] = sink_acc
        bk = bucket_ref[...]
        row = lax.broadcasted_iota(jnp.int32, (REL_BUCKETS, 128), 0)
        lane = lax.broadcasted_iota(jnp.int32, (REL_BUCKETS, 128), 1)
        acc = jnp.zeros((REL_BUCKETS, 128), F32)
        for h in range(SWA_HEADS):
            dbh = db_ref[h]

            def add(b, acc):
                part = jnp.sum(jnp.sum(jnp.where(bk == b, dbh, 0.0), axis=-1, keepdims=True), axis=0, keepdims=True)
                return jnp.where((row == b) & (lane == h), part, acc)

            acc = lax.fori_loop(0, REL_BUCKETS, add, acc)
        out_ref[...] = acc

    return pl.pallas_call(
        body, name="bias_reduce",
        out_shape=[jax.ShapeDtypeStruct((REL_BUCKETS, 128), F32), jax.ShapeDtypeStruct((1, 128), F32)],
        compiler_params=_cparams(),
    )(dbias, dsink_rows, bucket)


def _t5_bucket(dist):
    n = jnp.maximum(dist, 0)
    max_exact = REL_BUCKETS // 2
    nf = jnp.maximum(n, 1).astype(F32)
    large = max_exact + (jnp.log(nf / max_exact) / math.log(REL_MAX_DIST / max_exact) * (REL_BUCKETS - max_exact)).astype(jnp.int32)
    large = jnp.minimum(large, REL_BUCKETS - 1)
    return jnp.where(n < max_exact, n, large)


def _pair_heads(a, axis, inverse=False):
    shp = a.shape
    a = a.reshape(shp[:axis] + ((4, 2, 64) if inverse else (2, 4, 64)) + shp[axis + 1:])
    a = jnp.swapaxes(a, axis, axis + 1)
    return a.reshape(shp)


def _col_full(g):
    return jnp.transpose(g, (1, 0, 2)).reshape(g.shape[1], N_DEV * g.shape[2])


def _row_full(g):
    return g.reshape(N_DEV * g.shape[1], g.shape[2])


def _owner_blocks(w, by_col, me):
    if by_col:
        r, c = w.shape[0], w.shape[1] // N_DEV
        own = lax.dynamic_slice(w, (0, me * c), (r, c))
        blocks = jnp.transpose(w.astype(BF16).reshape(r, N_DEV, c), (1, 0, 2))
    else:
        r, c = w.shape[0] // N_DEV, w.shape[1]
        own = lax.dynamic_slice(w, (me * r, 0), (r, c))
        blocks = w.astype(BF16).reshape(N_DEV, r, c)
    return own, blocks


_IN_WIDTHS = (Q_LORA, KV_LORA, QK_ROPE, SWA_HEADS * SWA_HEAD_DIM, SWA_KV_HEADS * SWA_HEAD_DIM, SWA_KV_HEADS * SWA_HEAD_DIM)


def _split_in(w):
    parts, o = [], 0
    for wd in _IN_WIDTHS:
        parts.append(w[:, o:o + wd])
        o += wd
    return parts


def _win_to_kernel(w):
    cq, ckv, kpe, qs, ks, vs = _split_in(w)
    kpe = jnp.pad(kpe, ((0, 0), (64, 32)))
    return jnp.concatenate([cq, ckv, _pair_heads(qs, 1), ks, vs, kpe], axis=1)


def _win_from_kernel(dw):
    o = [0, Q_LORA, Q_LORA + KV_LORA, Q_LORA + KV_LORA + 512, Q_LORA + KV_LORA + 640, Q_LORA + KV_LORA + 768]
    cq, ckv, qs, ks, vs, kpe = [dw[:, a:b] for a, b in zip(o, o[1:] + [dw.shape[1]])]
    return jnp.concatenate([cq, ckv, kpe[:, 64:96], _pair_heads(qs, 1, inverse=True), ks, vs], axis=1)


def _wq_to_kernel(w):
    return jnp.pad(w.reshape(Q_LORA, MLA_HEADS, QK_NOPE + QK_ROPE), ((0, 0), (0, 0), (0, 32))).reshape(Q_LORA, MLA_HEADS * HEAD_BLOCK)


def _wq_from_kernel(dw):
    return dw.reshape(Q_LORA, MLA_HEADS, HEAD_BLOCK)[:, :, :QK_NOPE + QK_ROPE].reshape(Q_LORA, -1)


def _wkv_to_kernel(w):
    w3 = w.reshape(KV_LORA, MLA_HEADS, QK_NOPE + V_HEAD)
    kpart = jnp.pad(w3[:, :, :QK_NOPE], ((0, 0), (0, 0), (0, 64))).reshape(KV_LORA, MLA_HEADS * HEAD_BLOCK)
    return jnp.concatenate([kpart, w3[:, :, QK_NOPE:].reshape(KV_LORA, MLA_HEADS * V_HEAD)], axis=1)


def _wkv_from_kernel(dw):
    dk = dw[:, :MLA_HEADS * HEAD_BLOCK].reshape(KV_LORA, MLA_HEADS, HEAD_BLOCK)[:, :, :QK_NOPE]
    dv = dw[:, MLA_HEADS * HEAD_BLOCK:].reshape(KV_LORA, MLA_HEADS, V_HEAD)
    return jnp.concatenate([dk, dv], axis=2).reshape(KV_LORA, -1)


def _wo_to_kernel(w):
    return jnp.concatenate([w[:512], _pair_heads(w[512:], 0)], axis=0)


def _wo_from_kernel(dw):
    return jnp.concatenate([dw[:512], _pair_heads(dw[512:], 0, inverse=True)], axis=0)


_SMALL = ("g_ffn1", "g_mix", "g_q_a", "g_kv_a", "attn_sinks", "rel_bias", "g_out_mla", "g_out_swa", "g_ffn2", "g_final")


class _Comm:
    def __init__(self, shards, by_col, me):
        self.shards, self.by_col, self.me = shards, by_col, me
        self.own32, self.recv16 = {}, {}

    def gather(self, keys):
        return [("gather", self.shards[k]) for k in keys]

    def gathered(self, keys, results, wts):
        for k, g in zip(keys, results):
            full = _col_full(g) if self.by_col[k] else _row_full(g)
            wts[k] = _TO_KERNEL[k](full) if k in _TO_KERNEL else full

    def scatter(self, keys, grads):
        jobs = []
        for k in keys:
            g = _FROM_KERNEL[k](grads[k]) if k in _FROM_KERNEL else grads[k]
            self.own32[k], blocks = _owner_blocks(g, self.by_col[k], self.me)
            jobs.append(("scatter", blocks))
        return jobs

    def scattered(self, keys, results):
        self.recv16.update(zip(keys, results))


def _step(x, target, gains, wts, comm):
    def jobs_in(keys):
        return comm.gather(keys) if comm else None

    def jobs_out(keys):
        return comm.scatter(keys, grads) if comm else None

    B, S, D = x.shape
    T = B * S
    x2 = x.reshape(T, D)
    t2 = target.reshape(T, D)

    pos = jnp.arange(S, dtype=F32)
    inv_freq = ROPE_THETA ** (-jnp.arange(0, QK_ROPE, 2, dtype=F32) / QK_ROPE)
    ang = pos[:, None] * inv_freq[None, :]
    cos, sin = jnp.cos(ang), jnp.sin(ang)
    ones, zeros = jnp.ones((S, 64), F32), jnp.zeros((S, 32), F32)
    rope_c = jnp.concatenate([ones, cos, cos, zeros], axis=1)
    rope_s = jnp.concatenate([0.0 * ones, -sin, sin, zeros], axis=1)

    qi = jnp.arange(WINDOW)[:, None]
    kj = jnp.arange(2 * WINDOW)[None, :]
    bucket = _t5_bucket(qi + WINDOW - kj).astype(jnp.int32)
    bias = _bias_expand(gains["rel_bias"], bucket).reshape(SWA_KV_HEADS, SWA_ROWS, 2 * WINDOW)
    sinks = jnp.broadcast_to(gains["attn_sinks"].reshape(SWA_HEADS, 1), (SWA_HEADS, 128))
    g_swa = _pair_heads(gains["g_out_swa"], 1)

    keys = ("in", "q", "kv", "o", "g2")
    (h1, n1, gate1, up1, act1), got = _ffn_fwd(x2, gains["g_ffn1"], wts["g1"], wts["u1"], wts["d1"], "ffn1_fwd", jobs_in(keys))
    if comm:
        comm.gathered(keys, got, wts)
    (u, c_q, c_kv, cqn, ckvn, q, k, v, qs, ks, vs) = _proj_fwd(
        h1, gains["g_mix"], wts["in"], gains["g_q_a"], wts["q"], gains["g_kv_a"], wts["kv"], rope_c, rope_s, S)
    keys = ("u2", "d2")
    (o_mla, lse), got = _mla_fwd(q, k, v, B, S, jobs_in(keys))
    if comm:
        comm.gathered(keys, got, wts)
    o_swa = _swa_fwd(qs, ks, vs, bias, sinks, B, S)
    h2, oc = _out_fwd(o_mla, o_swa, gains["g_out_mla"], g_swa, wts["o"], h1)
    (h3, n2, gate2, up2, act2), _ = _ffn_fwd(h2, gains["g_ffn2"], wts["g2"], wts["u2"], wts["d2"], "ffn2_fwd")
    dh3, loss, dg_final = _loss_head(h3, t2, gains["g_final"].reshape(1, D))

    grads = {}
    (dh2, dg_ffn2, dgate2, dup2), _ = _ffn_bwd(h2, gains["g_ffn2"], gate2, up2, dh3, wts["g2"], wts["u2"], wts["d2"], "ffn2_bwd")
    grads["g2"], _ = _matmul_tn(n2, dgate2, "dw_gate2")
    grads["u2"], _ = _matmul_tn(n2, dup2, "dw_up2")
    grads["d2"], _ = _matmul_tn(act2, dh3, "dw_down2", 0.5)
    do_mla, do_swa, dg_mla, dg_swa = _out_bwd(dh2, o_mla, o_swa, gains["g_out_mla"], g_swa, wts["o"])
    grads["o"], _ = _matmul_tn(oc, dh2, "dw_o")
    keys = ("g2", "u2", "d2")
    (dq, dk, dv), got = _mla_bwd(q, k, v, do_mla, o_mla, lse, B, S, jobs_out(keys))
    if comm:
        comm.scattered(keys, got)
    dqs, dks, dvs, dbias, dsink = _swa_bwd(qs, ks, vs, do_swa, bias, sinks, B, S)
    dh1, dproj, dql, dkvc, dg_mix, dg_q, dg_kv = _proj_bwd(
        dq, dk, dv, dqs, dks, dvs, c_q, c_kv, h1, dh2, gains["g_mix"], wts["in"], gains["g_q_a"], wts["q"],
        gains["g_kv_a"], wts["kv"], rope_c, rope_s, S)
    grads["in"], _ = _matmul_tn(u, dproj, "dw_in")
    grads["q"], _ = _matmul_tn(cqn, dql, "dw_q")
    grads["kv"], _ = _matmul_tn(ckvn, dkvc, "dw_kv")
    grads["d1"], _ = _matmul_tn(act1, dh1, "dw_down1", 0.5)
    keys = ("o", "in", "q", "kv", "d1")
    (dx, dg_ffn1, dgate1, dup1), got = _ffn_bwd(
        x2, gains["g_ffn1"], gate1, up1, dh1, wts["g1"], wts["u1"], wts["d1"], "ffn1_bwd", jobs_out(keys))
    if comm:
        comm.scattered(keys, got)
    grads["g1"], _ = _matmul_tn(n1, dgate1, "dw_gate1")
    grads["u1"], got = _matmul_tn(n1, dup1, "dw_up1", exchange=jobs_out(("g1",)))
    if comm:
        comm.scattered(("g1",), got)
        comm.scattered(("u1",), _exchange(jobs_out(("u1",)), "grad_exchange_last"))

    d_rel_bias, d_sinks = _bias_reduce(dbias.reshape(SWA_HEADS, WINDOW, 2 * WINDOW), dsink.reshape(SWA_HEADS, WINDOW, 128), bucket)
    small = {
        "g_ffn1": dg_ffn1, "g_mix": dg_mix, "g_q_a": dg_q, "g_kv_a": dg_kv, "attn_sinks": d_sinks,
        "rel_bias": d_rel_bias, "g_out_mla": dg_mla, "g_out_swa": _pair_heads(dg_swa, 1, inverse=True),
        "g_ffn2": dg_ffn2, "g_final": dg_final, "loss": loss,
    }
    return dx.reshape(B, S, D), grads, small


_WEIGHTS = ("g_ffn1", "w_ffn1_gate", "w_ffn1_up", "w_ffn1_down", "g_mix", "w_in", "g_q_a", "w_q_b", "g_kv_a", "w_kv_b",
            "attn_sinks", "rel_bias", "g_out_mla", "g_out_swa", "w_o", "g_ffn2", "w_ffn2_gate", "w_ffn2_up", "w_ffn2_down",
            "g_final")
_BIG = (("w_ffn1_gate", "g1", True), ("w_ffn1_up", "u1", True), ("w_ffn1_down", "d1", False), ("w_in", "in", True),
        ("w_q_b", "q", True), ("w_kv_b", "kv", True), ("w_o", "o", False), ("w_ffn2_gate", "g2", True),
        ("w_ffn2_up", "u2", True), ("w_ffn2_down", "d2", False))
_TO_KERNEL = {"in": _win_to_kernel, "q": _wq_to_kernel, "kv": _wkv_to_kernel, "o": _wo_to_kernel}
_FROM_KERNEL = {"in": _win_from_kernel, "q": _wq_from_kernel, "kv": _wkv_from_kernel, "o": _wo_from_kernel}


def kernel(x, g_ffn1, w_ffn1_gate, w_ffn1_up, w_ffn1_down, g_mix, w_in, g_q_a, w_q_b, g_kv_a, w_kv_b, attn_sinks, rel_bias, g_out_mla, g_out_swa, w_o, g_ffn2, w_ffn2_gate, w_ffn2_up, w_ffn2_down, g_final, loss_target, m_g_ffn1, m_w_ffn1_gate, m_w_ffn1_up, m_w_ffn1_down, m_g_mix, m_w_in, m_g_q_a, m_w_q_b, m_g_kv_a, m_w_kv_b, m_attn_sinks, m_rel_bias, m_g_out_mla, m_g_out_swa, m_w_o, m_g_ffn2, m_w_ffn2_gate, m_w_ffn2_up, m_w_ffn2_down, m_g_final, v_g_ffn1, v_w_ffn1_gate, v_w_ffn1_up, v_w_ffn1_down, v_g_mix, v_w_in, v_g_q_a, v_w_q_b, v_g_kv_a, v_w_kv_b, v_attn_sinks, v_rel_bias, v_g_out_mla, v_g_out_swa, v_w_o, v_g_ffn2, v_w_ffn2_gate, v_w_ffn2_up, v_w_ffn2_down, v_g_final):
    w = dict(zip(_WEIGHTS, (g_ffn1, w_ffn1_gate, w_ffn1_up, w_ffn1_down, g_mix, w_in, g_q_a, w_q_b, g_kv_a, w_kv_b, attn_sinks,
                            rel_bias, g_out_mla, g_out_swa, w_o, g_ffn2, w_ffn2_gate, w_ffn2_up, w_ffn2_down, g_final)))
    m = dict(zip(_WEIGHTS, (m_g_ffn1, m_w_ffn1_gate, m_w_ffn1_up, m_w_ffn1_down, m_g_mix, m_w_in, m_g_q_a, m_w_q_b, m_g_kv_a,
                            m_w_kv_b, m_attn_sinks, m_rel_bias, m_g_out_mla, m_g_out_swa, m_w_o, m_g_ffn2, m_w_ffn2_gate,
                            m_w_ffn2_up, m_w_ffn2_down, m_g_final)))
    v = dict(zip(_WEIGHTS, (v_g_ffn1, v_w_ffn1_gate, v_w_ffn1_up, v_w_ffn1_down, v_g_mix, v_w_in, v_g_q_a, v_w_q_b, v_g_kv_a,
                            v_w_kv_b, v_attn_sinks, v_rel_bias, v_g_out_mla, v_g_out_swa, v_w_o, v_g_ffn2, v_w_ffn2_gate,
                            v_w_ffn2_up, v_w_ffn2_down, v_g_final)))
    me = 4 * lax.axis_index("x") + 2 * lax.axis_index("y") + lax.axis_index("c")

    shards = dict(zip([key for _, key, _ in _BIG], _cast_bf16([w[name][0] for name, _, _ in _BIG])))
    comm = _Comm(shards, {key: by_col for _, key, by_col in _BIG}, me)
    first = ("g1", "u1", "d1")
    wts = {}
    comm.gathered(first, _all_gather([shards[k] for k in first], "weights_all_gather"), wts)

    gains = {n: (w[n] if n == "rel_bias" else w[n].reshape(1, -1)) for n in _SMALL}
    grad_x, _, small = _step(x, loss_target, gains, wts, comm)

    small_gathered = _all_gather([small[n] for n in _SMALL] + [small["loss"]], "small_all_gather")

    out_g, out_d, out_m, out_v = {}, {}, {}, {}
    me1 = me.astype(jnp.int32).reshape(1)
    for name, key, _ in _BIG:
        g, d, mn, vn = _adam_big(w[name][0], m[name][0], v[name][0], comm.own32[key], comm.recv16[key], me1)
        out_g[name], out_d[name], out_m[name], out_v[name] = g[None], d[None], mn[None], vn[None]
    as_2d = lambda src: [src[n] if src[n].ndim == 2 else src[n].reshape(1, -1) for n in _SMALL]
    loss, updates = _adam_small(as_2d(w), as_2d(m), as_2d(v), small_gathered[:-1], small_gathered[-1])
    for n, (g, d, mn, vn) in zip(_SMALL, updates):
        out_g[n], out_d[n], out_m[n], out_v[n] = (a.reshape(w[n].shape) for a in (g, d, mn, vn))

    return (loss[0, 0], grad_x, *[out_g[n] for n in _WEIGHTS], *[out_d[n] for n in _WEIGHTS],
            *[out_m[n] for n in _WEIGHTS], *[out_v[n] for n in _WEIGHTS])
```

```python
import functools
import math

import jax
import jax.numpy as jnp
from jax import lax
from jax.experimental import pallas as pl
from jax.experimental.pallas import tpu as pltpu

F32 = jnp.float32
BF16 = jnp.bfloat16
MESH = pl.DeviceIdType.MESH

EPS = 1e-6
NEG = -1e30
N_DEV = 8

MLA_HEADS = 8
Q_LORA = 256
KV_LORA = 128
QK_NOPE = 64
QK_ROPE = 32
V_HEAD = 64
ROPE_THETA = 10000.0
SWA_HEADS = 8
SWA_KV_HEADS = 2
SWA_HEAD_DIM = 64
WINDOW = 128
REL_BUCKETS = 32
REL_MAX_DIST = 128
HEAD_BLOCK = 128
MLA_Q_TILES = (512, 256, 128)
MLA_SCALE = (QK_NOPE + QK_ROPE) ** -0.5
SWA_SCALE = SWA_HEAD_DIM ** -0.5

ADAM_LR = 0.001
ADAM_B1 = 0.9
ADAM_B2 = 0.999
ADAM_EPS = 1e-08
ADAM_WD = 0.01
ADAM_STEP = 10

VMEM_LIMIT = 56 * 1024 * 1024


def _cparams(semantics=None):
    return pltpu.CompilerParams(dimension_semantics=semantics, vmem_limit_bytes=VMEM_LIMIT)


def _dot(a, b):
    return jnp.dot(a, b, preferred_element_type=F32)


def _dot_nt(a, b):
    return lax.dot_general(a, b, (((1,), (1,)), ((), ())), preferred_element_type=F32)


def _dot_tn(a, b):
    return lax.dot_general(a, b, (((0,), (0,)), ((), ())), preferred_element_type=F32)


def _rms_fwd(x, g):
    r = lax.rsqrt(jnp.mean(x * x, axis=-1, keepdims=True) + EPS)
    return x * r * g, r


def _rms_bwd(x, g, r, dy):
    xh = x * r
    dyg = dy * g
    dx = r * (dyg - xh * jnp.mean(dyg * xh, axis=-1, keepdims=True))
    return dx, jnp.sum(dy * xh, axis=0, keepdims=True)


def _swap16(x):
    n = x.shape[-1]
    up = pltpu.roll(x, n - 16, 1)
    down = pltpu.roll(x, 16, 1)
    lane = lax.broadcasted_iota(jnp.int32, x.shape, 1) % HEAD_BLOCK
    return jnp.where((lane >= 64) & (lane < 80), up, jnp.where((lane >= 80) & (lane < 96), down, 0.0))


def _const_spec(shape):
    return pl.BlockSpec(shape, lambda *_: (0,) * len(shape), pipeline_mode=pl.Buffered(1))


def _acc_spec(shape):
    return pl.BlockSpec(shape, lambda *_: (0,) * len(shape))


def _row_spec(tm, width):
    return pl.BlockSpec((tm, width), lambda i: (i, 0))


def _cast_bf16(arrays):
    n = len(arrays)

    def body(*refs):
        for src, dst in zip(refs[:n], refs[n:]):
            dst[...] = src[...].astype(BF16)

    return pl.pallas_call(
        body, name="cast_weights",
        out_shape=[jax.ShapeDtypeStruct(a.shape, BF16) for a in arrays],
        compiler_params=_cparams(),
    )(*arrays)


def _all_gather(arrays, name):
    n = len(arrays)

    def body(*refs):
        ins, outs = refs[:n], refs[n:2 * n]
        send_sems, recv_sems, local_sems = refs[2 * n:]
        x, y, c = lax.axis_index("x"), lax.axis_index("y"), lax.axis_index("c")
        me, sibling = (x, y, c), (x, y, 1 - c)
        chips = [(1 - x, y), (x, 1 - y), (1 - x, 1 - y)]

        def slot(a, dev):
            return outs[a].at[4 * dev[0] + 2 * dev[1] + dev[2]]

        def copy(a, k, block, to, src=None):
            return pltpu.make_async_remote_copy(
                src_ref=slot(a, block) if src is None else src, dst_ref=slot(a, block),
                send_sem=send_sems.at[a, k], recv_sem=recv_sems.at[a, k], device_id=to, device_id_type=MESH)

        mine = [pltpu.make_async_copy(ins[a], slot(a, me), local_sems.at[a]) for a in range(n)]
        for cp in mine:
            cp.start()
        first = []
        for a in range(n):
            first.append(copy(a, 0, me, sibling, src=ins[a]))
            first += [copy(a, 1 + j, me, (*chip, c), src=ins[a]) for j, chip in enumerate(chips)]
        for cp in first:
            cp.start()
        passed = []
        for j, chip in enumerate(chips):
            for a in range(n):
                copy(a, 1 + j, (*chip, c), me).wait_recv()
                cp = copy(a, 4 + j, (*chip, c), sibling)
                cp.start()
                passed.append(cp)
        for a in range(n):
            copy(a, 0, sibling, me).wait_recv()
            for j, chip in enumerate(chips):
                copy(a, 4 + j, (*chip, 1 - c), me).wait_recv()
        for cp in first + passed:
            cp.wait_send()
        for cp in mine:
            cp.wait()

    any_spec = pl.BlockSpec(memory_space=pl.ANY)
    return pl.pallas_call(
        body, name=name,
        out_shape=[jax.ShapeDtypeStruct((N_DEV,) + a.shape, a.dtype) for a in arrays],
        in_specs=[any_spec] * n, out_specs=[any_spec] * n,
        scratch_shapes=[pltpu.SemaphoreType.DMA((n, 7)), pltpu.SemaphoreType.DMA((n, 7)), pltpu.SemaphoreType.DMA((n,))],
    )(*arrays)


def _exchange_shapes(jobs):
    dsts = [jax.ShapeDtypeStruct(((N_DEV,) + a.shape) if kind == "gather" else a.shape, a.dtype) for kind, a in jobs]
    n = len(jobs)
    sems = [pltpu.SemaphoreType.DMA((n, N_DEV - 1)), pltpu.SemaphoreType.DMA((n, N_DEV - 1)), pltpu.SemaphoreType.DMA((n,))]
    return dsts, sems


def _exchange_copies(kinds, srcs, dsts, send_sems, recv_sems, local_sems):
    x, y, c = lax.axis_index("x"), lax.axis_index("y"), lax.axis_index("c")
    me = 4 * x + 2 * y + c
    remote, local = [], []
    for a, kind in enumerate(kinds):
        for k in range(1, N_DEV):
            peer = (1 - x if k & 4 else x, 1 - y if k & 2 else y, 1 - c if k & 1 else c)
            src = srcs[a] if kind == "gather" else srcs[a].at[4 * peer[0] + 2 * peer[1] + peer[2]]
            remote.append(pltpu.make_async_remote_copy(
                src_ref=src, dst_ref=dsts[a].at[me], send_sem=send_sems.at[a, k - 1], recv_sem=recv_sems.at[a, k - 1],
                device_id=peer, device_id_type=MESH))
        src = srcs[a] if kind == "gather" else srcs[a].at[me]
        local.append(pltpu.make_async_copy(src, dsts[a].at[me], local_sems.at[a]))
    return remote, local


def _exchange(jobs, name):
    n = len(jobs)
    kinds = [kind for kind, _ in jobs]
    dst_shapes, sems = _exchange_shapes(jobs)

    def body(*refs):
        remote, local = _exchange_copies(kinds, refs[:n], refs[n:2 * n], *refs[2 * n:])
        for cp in local + remote:
            cp.start()
        for cp in remote + local:
            cp.wait()

    any_spec = pl.BlockSpec(memory_space=pl.ANY)
    return pl.pallas_call(
        body, name=name, out_shape=dst_shapes, in_specs=[any_spec] * n, out_specs=[any_spec] * n, scratch_shapes=sems,
    )(*[a for _, a in jobs])


def _pcall(body, *, name, grid, in_specs, out_specs, out_shape, args, scratch_shapes=(), exchange=None):
    semantics = ("arbitrary",) * len(grid)
    if not exchange:
        outs = pl.pallas_call(body, name=name, grid=grid, in_specs=list(in_specs), out_specs=list(out_specs),
                              out_shape=list(out_shape), scratch_shapes=list(scratch_shapes),
                              compiler_params=_cparams(semantics))(*args)
        return list(outs), []
    kinds = [kind for kind, _ in exchange]
    n_in, n_out, n_job = len(in_specs), len(out_specs), len(exchange)
    dst_shapes, sems = _exchange_shapes(exchange)

    def with_exchange(*refs):
        ins, srcs = refs[:n_in], refs[n_in:n_in + n_job]
        outs, dsts = refs[n_in + n_job:n_in + n_job + n_out], refs[n_in + n_job + n_out:n_in + 2 * n_job + n_out]
        scratch, sem_refs = refs[n_in + 2 * n_job + n_out:-3], refs[-3:]
        first = functools.reduce(jnp.logical_and, [pl.program_id(d) == 0 for d in range(len(grid))])
        last = functools.reduce(jnp.logical_and, [pl.program_id(d) == grid[d] - 1 for d in range(len(grid))])

        @pl.when(first)
        def _():
            remote, local = _exchange_copies(kinds, srcs, dsts, *sem_refs)
            for cp in local + remote:
                cp.start()

        body(*ins, *outs, *scratch)

        @pl.when(last)
        def _():
            remote, local = _exchange_copies(kinds, srcs, dsts, *sem_refs)
            for cp in remote + local:
                cp.wait()

    any_spec = pl.BlockSpec(memory_space=pl.ANY)
    outs = pl.pallas_call(
        with_exchange, name=name, grid=grid, in_specs=list(in_specs) + [any_spec] * n_job,
        out_specs=list(out_specs) + [any_spec] * n_job, out_shape=list(out_shape) + dst_shapes,
        scratch_shapes=list(scratch_shapes) + sems, compiler_params=_cparams(semantics),
    )(*args, *[a for _, a in exchange])
    return list(outs[:n_out]), list(outs[n_out:])


def _adam_math(w, g, m, v):
    m = ADAM_B1 * m + (1.0 - ADAM_B1) * g
    v = ADAM_B2 * v + (1.0 - ADAM_B2) * (g * g)
    m_hat = m / (1.0 - ADAM_B1 ** ADAM_STEP)
    v_hat = v / (1.0 - ADAM_B2 ** ADAM_STEP)
    delta = -ADAM_LR * (m_hat / (jnp.sqrt(v_hat) + ADAM_EPS) + ADAM_WD * w)
    return delta, m, v


def _adam_big(w, m, v, own32, recv16, me):
    r, c = w.shape
    tr = r if r <= 512 else (512 if r % 512 == 0 else 352)
    assert r % tr == 0

    def body(me_ref, w_ref, m_ref, v_ref, s_ref, r_ref, g_out, d_out, m_out, v_out):
        g = s_ref[...]
        for k in range(1, N_DEV):
            g = g + r_ref[jnp.bitwise_xor(me_ref[0], k)].astype(F32)
        d, mn, vn = _adam_math(w_ref[...], g, m_ref[...], v_ref[...])
        g_out[...] = g
        d_out[...] = d
        m_out[...] = mn
        v_out[...] = vn

    blk = pl.BlockSpec((tr, c), lambda i, s: (i, 0))
    grid_spec = pltpu.PrefetchScalarGridSpec(
        num_scalar_prefetch=1, grid=(r // tr,),
        in_specs=[blk, blk, blk, blk, pl.BlockSpec((N_DEV, tr, c), lambda i, s: (0, i, 0))],
        out_specs=[blk, blk, blk, blk])
    return pl.pallas_call(
        body, name="adamw_big", grid_spec=grid_spec,
        out_shape=[jax.ShapeDtypeStruct((r, c), F32)] * 4,
        compiler_params=_cparams(("arbitrary",)),
    )(me, w, m, v, own32, recv16)


def _adam_small(ws, ms, vs, partials, loss_parts):
    n = len(ws)

    def body(*refs):
        w_refs, m_refs, v_refs, p_refs = (refs[i * n:(i + 1) * n] for i in range(4))
        l_ref, loss_out = refs[4 * n], refs[4 * n + 1]
        outs = refs[4 * n + 2:]

        def total(ref):
            s = ref[0]
            for dev in range(1, N_DEV):
                s = s + ref[dev]
            return s

        loss_out[...] = total(l_ref)
        for i in range(n):
            r, c = w_refs[i].shape
            g = total(p_refs[i])[:, :c]
            d, mn, vn = _adam_math(w_refs[i][...], g, m_refs[i][...], v_refs[i][...])
            for j, val in enumerate((g, d, mn, vn)):
                outs[4 * i + j][...] = val

    shapes = [jax.ShapeDtypeStruct((1, 128), F32)]
    for w in ws:
        shapes += [jax.ShapeDtypeStruct(w.shape, F32)] * 4
    outs = pl.pallas_call(body, name="adamw_small", out_shape=shapes, compiler_params=_cparams())(*ws, *ms, *vs, *partials, loss_parts)
    return outs[0], [outs[1 + 4 * i:5 + 4 * i] for i in range(n)]


def _pick(n, prefer):
    for t in prefer:
        if n % t == 0:
            return t
    return n


def _ffn_fwd(x, g, wg, wu, wd, name, exchange=None):
    T, D = x.shape
    F = wg.shape[1]
    tm = _pick(T, (512, 256, 128))
    fc = _pick(F, (256, 128))

    def body(x_ref, g_ref, wg_ref, wu_ref, wd_ref, h_ref, n_ref, gate_ref, up_ref, act_ref):
        xv = x_ref[...]
        n, _ = _rms_fwd(xv, g_ref[...])
        nb = n.astype(BF16)
        n_ref[...] = nb
        acc = jnp.zeros((tm, D), F32)
        for f0 in range(0, F, fc):
            gate = _dot(nb, wg_ref[:, f0:f0 + fc])
            up = _dot(nb, wu_ref[:, f0:f0 + fc])
            gate_ref[:, f0:f0 + fc] = gate.astype(BF16)
            up_ref[:, f0:f0 + fc] = up.astype(BF16)
            act = (gate * (1.0 / (1.0 + jnp.exp(-gate))) * up).astype(BF16)
            act_ref[:, f0:f0 + fc] = act
            acc = acc + _dot(act, wd_ref[f0:f0 + fc, :])
        h_ref[...] = xv + 0.5 * acc

    return _pcall(
        body, name=name, grid=(T // tm,),
        in_specs=[_row_spec(tm, D), _const_spec((1, D)), _const_spec((D, F)), _const_spec((D, F)), _const_spec((F, D))],
        out_specs=[_row_spec(tm, D), _row_spec(tm, D), _row_spec(tm, F), _row_spec(tm, F), _row_spec(tm, F)],
        out_shape=[jax.ShapeDtypeStruct((T, D), F32), jax.ShapeDtypeStruct((T, D), BF16),
                   jax.ShapeDtypeStruct((T, F), BF16), jax.ShapeDtypeStruct((T, F), BF16), jax.ShapeDtypeStruct((T, F), BF16)],
        args=(x, g, wg, wu, wd), exchange=exchange)


FFN_BWD_CHUNK = 1408


def _ffn_bwd(x, g, gate, up, dh, wgt, wut, wdt, name, exchange=None):
    T, D = x.shape
    F = wgt.shape[0]
    tm = _pick(T, (512, 256, 128))
    halves = 2
    Fh = F // halves
    assert Fh * halves == F and Fh % 128 == 0
    chunks = [(f0, min(f0 + FFN_BWD_CHUNK, Fh)) for f0 in range(0, Fh, FFN_BWD_CHUNK)]

    def body(x_ref, g_ref, gate_ref, up_ref, dh_ref, wg_ref, wu_ref, wd_ref, dx_ref, dg_ref, dgate_ref, dup_ref, dn_ref):
        i, h = pl.program_id(0), pl.program_id(1)
        dhv = dh_ref[...]
        dho = (0.5 * dhv).astype(BF16)
        dn = jnp.zeros((tm, D), F32)
        for f0, f1 in chunks:
            d_act = _dot(dho, wd_ref[:, f0:f1])
            gt = gate_ref[:, f0:f1].astype(F32)
            u = up_ref[:, f0:f1].astype(F32)
            sg = 1.0 / (1.0 + jnp.exp(-gt))
            silu = gt * sg
            d_up = (d_act * silu).astype(BF16)
            d_gate = (d_act * u * (sg + silu * (1.0 - sg))).astype(BF16)
            dup_ref[:, f0:f1] = d_up
            dgate_ref[:, f0:f1] = d_gate
            dn = dn + _dot(d_gate, wg_ref[f0:f1, :]) + _dot(d_up, wu_ref[f0:f1, :])

        @pl.when(h == 0)
        def _():
            dn_ref[...] = dn

        @pl.when(h > 0)
        def _():
            dn_ref[...] += dn

        @pl.when((i == 0) & (h == 0))
        def _():
            dg_ref[...] = jnp.zeros_like(dg_ref)

        @pl.when(h == halves - 1)
        def _():
            xv, gv = x_ref[...], g_ref[...]
            _, r = _rms_fwd(xv, gv)
            dxn, dgp = _rms_bwd(xv, gv, r, dn_ref[...])
            dx_ref[...] = dhv + dxn
            dg_ref[...] += dgp

    tile = pl.BlockSpec((tm, D), lambda i, h: (i, 0))
    wide = pl.BlockSpec((tm, Fh), lambda i, h: (i, h))
    return _pcall(
        body, name=name, grid=(T // tm, halves),
        in_specs=[tile, _const_spec((1, D)), wide, wide, tile,
                  pl.BlockSpec((Fh, D), lambda i, h: (h, 0)), pl.BlockSpec((Fh, D), lambda i, h: (h, 0)),
                  pl.BlockSpec((D, Fh), lambda i, h: (0, h))],
        out_specs=[tile, _acc_spec((1, D)), wide, wide],
        out_shape=[jax.ShapeDtypeStruct((T, D), F32), jax.ShapeDtypeStruct((1, D), F32),
                   jax.ShapeDtypeStruct((T, F), BF16), jax.ShapeDtypeStruct((T, F), BF16)],
        scratch_shapes=[pltpu.VMEM((tm, D), F32)],
        args=(x, g, gate, up, dh, wgt, wut, wdt), exchange=exchange)


def _matmul_tn(a, b, name, scale=None, exchange=None):
    T, K = a.shape
    N = b.shape[1]
    tk = _pick(K, (1024, 1408, 1280, 768, 512, 256, 128))
    tn = _pick(N, (1024, 1408, 1280, 768, 512, 256, 128))
    tt = _pick(T, (2048, 1024, 512, 256, 128))
    n_t = T // tt

    def body(a_ref, b_ref, o_ref):
        @pl.when(pl.program_id(2) == 0)
        def _():
            o_ref[...] = jnp.zeros_like(o_ref)

        o_ref[...] += _dot_tn(a_ref[...].astype(BF16), b_ref[...].astype(BF16))

        if scale is not None:
            @pl.when(pl.program_id(2) == n_t - 1)
            def _():
                o_ref[...] = o_ref[...] * scale

    outs, got = _pcall(
        body, name=name, grid=(K // tk, N // tn, n_t),
        in_specs=[pl.BlockSpec((tt, tk), lambda i, j, t: (t, i)), pl.BlockSpec((tt, tn), lambda i, j, t: (t, j))],
        out_specs=[pl.BlockSpec((tk, tn), lambda i, j, t: (i, j))],
        out_shape=[jax.ShapeDtypeStruct((K, N), F32)],
        args=(a, b), exchange=exchange)
    return outs[0], got


def _rope_spec(tm, S):
    per = S // tm
    return pl.BlockSpec((tm, HEAD_BLOCK), lambda i: (i % per, 0))


def _proj_fwd(h1, g_mix, w_in, g_q, wq, g_kv, wkv, rope_c, rope_s, S):
    T, D = h1.shape
    P = w_in.shape[1]
    tm = _pick(S, (512, 256, 128))
    QW = MLA_HEADS * HEAD_BLOCK
    VW = MLA_HEADS * V_HEAD
    SQ = SWA_HEADS * SWA_HEAD_DIM
    SK = SWA_KV_HEADS * SWA_HEAD_DIM
    o_cq, o_ckv, o_qs, o_ks, o_vs, o_kpe = 0, Q_LORA, Q_LORA + KV_LORA, Q_LORA + KV_LORA + SQ, Q_LORA + KV_LORA + SQ + SK, Q_LORA + KV_LORA + SQ + 2 * SK
    assert P == o_kpe + HEAD_BLOCK

    def body(h_ref, gm_ref, win_ref, gq_ref, wq_ref, gkv_ref, wkv_ref, c_ref, s_ref,
             u_ref, cq_ref, ckv_ref, cqn_ref, ckvn_ref, q_ref, k_ref, v_ref, qs_ref, ks_ref, vs_ref):
        u, _ = _rms_fwd(h_ref[...], gm_ref[...])
        ub = u.astype(BF16)
        u_ref[...] = ub
        proj = _dot(ub, win_ref[...])
        c_q = proj[:, o_cq:o_ckv]
        c_kv = proj[:, o_ckv:o_qs]
        cq_ref[...] = c_q
        ckv_ref[...] = c_kv
        qs_ref[...] = proj[:, o_qs:o_ks].astype(BF16)
        ks_ref[...] = proj[:, o_ks:o_vs].astype(BF16)
        vs_ref[...] = proj[:, o_vs:o_kpe].astype(BF16)
        cb, sb = c_ref[...], s_ref[...]
        kpe = proj[:, o_kpe:P]
        kpe = kpe * cb + _swap16(kpe) * sb
        cqn, _ = _rms_fwd(c_q, gq_ref[...])
        cqn = cqn.astype(BF16)
        cqn_ref[...] = cqn
        q = _dot(cqn, wq_ref[...])
        q = q * jnp.tile(cb, (1, MLA_HEADS)) + _swap16(q) * jnp.tile(sb, (1, MLA_HEADS))
        q_ref[...] = q.astype(BF16)
        ckvn, _ = _rms_fwd(c_kv, gkv_ref[...])
        ckvn = ckvn.astype(BF16)
        ckvn_ref[...] = ckvn
        kv = _dot(ckvn, wkv_ref[...])
        k_ref[...] = (kv[:, :QW] + jnp.tile(kpe, (1, MLA_HEADS))).astype(BF16)
        v_ref[...] = kv[:, QW:].astype(BF16)

    widths = [(D, BF16), (Q_LORA, F32), (KV_LORA, F32), (Q_LORA, BF16), (KV_LORA, BF16), (QW, BF16), (QW, BF16), (VW, BF16),
              (SQ, BF16), (SK, BF16), (SK, BF16)]
    return pl.pallas_call(
        body, name="proj_fwd", grid=(T // tm,),
        in_specs=[_row_spec(tm, D), _const_spec((1, D)), _const_spec((D, P)), _const_spec((1, Q_LORA)),
                  _const_spec((Q_LORA, QW)), _const_spec((1, KV_LORA)), _const_spec((KV_LORA, QW + VW)),
                  _rope_spec(tm, S), _rope_spec(tm, S)],
        out_specs=[_row_spec(tm, w) for w, _ in widths],
        out_shape=[jax.ShapeDtypeStruct((T, w), dt) for w, dt in widths],
        compiler_params=_cparams(("arbitrary",)),
    )(h1, g_mix, w_in, g_q, wq, g_kv, wkv, rope_c, rope_s)


def _proj_bwd(dq, dk, dv, dqs, dks, dvs, c_q, c_kv, h1, dh2, g_mix, w_in, g_q, wq, g_kv, wkv, rope_c, rope_s, S):
    T, D = h1.shape
    P = w_in.shape[0]
    tm = _pick(S, (256, 128))
    QW = MLA_HEADS * HEAD_BLOCK
    VW = MLA_HEADS * V_HEAD
    SQ = SWA_HEADS * SWA_HEAD_DIM
    SK = SWA_KV_HEADS * SWA_HEAD_DIM

    def body(dq_ref, dk_ref, dv_ref, dqs_ref, dks_ref, dvs_ref, cq_ref, ckv_ref, h_ref, dh2_ref, gm_ref, win_ref, gq_ref,
             wq_ref, gkv_ref, wkv_ref, c_ref, s_ref, dh1_ref, dproj_ref, dql_ref, dkv_ref, dgm_ref, dgq_ref, dgkv_ref):
        cb, sb = c_ref[...], s_ref[...]
        dqv = dq_ref[...]
        dql = dqv * jnp.tile(cb, (1, MLA_HEADS)) + _swap16(dqv * jnp.tile(sb, (1, MLA_HEADS)))
        dql = dql.astype(BF16)
        dql_ref[...] = dql
        c_q = cq_ref[...]
        _, rq = _rms_fwd(c_q, gq_ref[...])
        d_cq, dgq = _rms_bwd(c_q, gq_ref[...], rq, _dot(dql, wq_ref[...]))

        dkv_all = dk_ref[...]
        dkpe = dkv_all[:, 0:HEAD_BLOCK]
        for h in range(1, MLA_HEADS):
            dkpe = dkpe + dkv_all[:, h * HEAD_BLOCK:(h + 1) * HEAD_BLOCK]
        lane = lax.broadcasted_iota(jnp.int32, dkpe.shape, 1)
        dkpe = jnp.where((lane >= 64) & (lane < 96), dkpe, 0.0)
        dkpe = dkpe * cb + _swap16(dkpe * sb)
        dkvc = jnp.concatenate([dkv_all.astype(BF16), dv_ref[...].astype(BF16)], axis=1)
        dkv_ref[...] = dkvc
        c_kv = ckv_ref[...]
        _, rkv = _rms_fwd(c_kv, gkv_ref[...])
        d_ckv, dgkv = _rms_bwd(c_kv, gkv_ref[...], rkv, _dot(dkvc, wkv_ref[...]))

        dproj = jnp.concatenate([d_cq.astype(BF16), d_ckv.astype(BF16), dqs_ref[...].astype(BF16),
                                 dks_ref[...].astype(BF16), dvs_ref[...].astype(BF16), dkpe.astype(BF16)], axis=1)
        dproj_ref[...] = dproj
        hv = h_ref[...]
        _, rm = _rms_fwd(hv, gm_ref[...])
        dxn, dgm = _rms_bwd(hv, gm_ref[...], rm, _dot(dproj, win_ref[...]))
        dh1_ref[...] = dh2_ref[...] + dxn

        @pl.when(pl.program_id(0) == 0)
        def _():
            dgm_ref[...] = jnp.zeros_like(dgm_ref)
            dgq_ref[...] = jnp.zeros_like(dgq_ref)
            dgkv_ref[...] = jnp.zeros_like(dgkv_ref)

        dgm_ref[...] += dgm
        dgq_ref[...] += dgq
        dgkv_ref[...] += dgkv

    return pl.pallas_call(
        body, name="proj_bwd", grid=(T // tm,),
        in_specs=[_row_spec(tm, QW), _row_spec(tm, QW), _row_spec(tm, VW), _row_spec(tm, SQ), _row_spec(tm, SK), _row_spec(tm, SK),
                  _row_spec(tm, Q_LORA), _row_spec(tm, KV_LORA), _row_spec(tm, D), _row_spec(tm, D),
                  _const_spec((1, D)), _const_spec((P, D)), _const_spec((1, Q_LORA)), _const_spec((QW, Q_LORA)),
                  _const_spec((1, KV_LORA)), _const_spec((QW + VW, KV_LORA)), _rope_spec(tm, S), _rope_spec(tm, S)],
        out_specs=[_row_spec(tm, D), _row_spec(tm, P), _row_spec(tm, QW), _row_spec(tm, QW + VW),
                   _acc_spec((1, D)), _acc_spec((1, Q_LORA)), _acc_spec((1, KV_LORA))],
        out_shape=[jax.ShapeDtypeStruct((T, D), F32), jax.ShapeDtypeStruct((T, P), BF16), jax.ShapeDtypeStruct((T, QW), BF16),
                   jax.ShapeDtypeStruct((T, QW + VW), BF16), jax.ShapeDtypeStruct((1, D), F32),
                   jax.ShapeDtypeStruct((1, Q_LORA), F32), jax.ShapeDtypeStruct((1, KV_LORA), F32)],
        compiler_params=_cparams(("arbitrary",)),
    )(dq, dk, dv, dqs, dks, dvs, c_q, c_kv, h1, dh2, g_mix, w_in, g_q, wq, g_kv, wkv, rope_c, rope_s)


def _out_fwd(o_mla, o_swa, g_mla, g_swa, w_o, h1):
    T, D = h1.shape
    W = o_mla.shape[1]
    tm = _pick(T, (512, 256, 128))

    def body(om_ref, os_ref, gm_ref, gs_ref, wo_ref, h_ref, h2_ref, oc_ref):
        a, _ = _rms_fwd(om_ref[...], gm_ref[...])
        b, _ = _rms_fwd(os_ref[...], gs_ref[...])
        oc = jnp.concatenate([a.astype(BF16), b.astype(BF16)], axis=1)
        oc_ref[...] = oc
        h2_ref[...] = h_ref[...] + _dot(oc, wo_ref[...])

    return pl.pallas_call(
        body, name="out_fwd", grid=(T // tm,),
        in_specs=[_row_spec(tm, W), _row_spec(tm, W), _const_spec((1, W)), _const_spec((1, W)), _const_spec((2 * W, D)),
                  _row_spec(tm, D)],
        out_specs=[_row_spec(tm, D), _row_spec(tm, 2 * W)],
        out_shape=[jax.ShapeDtypeStruct((T, D), F32), jax.ShapeDtypeStruct((T, 2 * W), BF16)],
        compiler_params=_cparams(("arbitrary",)),
    )(o_mla, o_swa, g_mla, g_swa, w_o, h1)


def _out_bwd(dh2, o_mla, o_swa, g_mla, g_swa, w_o):
    T, D = dh2.shape
    W = o_mla.shape[1]
    tm = _pick(T, (512, 256, 128))

    def body(dh_ref, om_ref, os_ref, gm_ref, gs_ref, wo_ref, dom_ref, dos_ref, dgm_ref, dgs_ref):
        doc = _dot_nt(dh_ref[...].astype(BF16), wo_ref[...])
        om, osw = om_ref[...], os_ref[...]
        _, ra = _rms_fwd(om, gm_ref[...])
        _, rb = _rms_fwd(osw, gs_ref[...])
        da, dga = _rms_bwd(om, gm_ref[...], ra, doc[:, :W])
        db, dgb = _rms_bwd(osw, gs_ref[...], rb, doc[:, W:])
        dom_ref[...] = da
        dos_ref[...] = db

        @pl.when(pl.program_id(0) == 0)
        def _():
            dgm_ref[...] = jnp.zeros_like(dgm_ref)
            dgs_ref[...] = jnp.zeros_like(dgs_ref)

        dgm_ref[...] += dga
        dgs_ref[...] += dgb

    return pl.pallas_call(
        body, name="out_bwd", grid=(T // tm,),
        in_specs=[_row_spec(tm, D), _row_spec(tm, W), _row_spec(tm, W), _const_spec((1, W)), _const_spec((1, W)),
                  _const_spec((2 * W, D))],
        out_specs=[_row_spec(tm, W), _row_spec(tm, W), _acc_spec((1, W)), _acc_spec((1, W))],
        out_shape=[jax.ShapeDtypeStruct((T, W), F32), jax.ShapeDtypeStruct((T, W), F32),
                   jax.ShapeDtypeStruct((1, W), F32), jax.ShapeDtypeStruct((1, W), F32)],
        compiler_params=_cparams(("arbitrary",)),
    )(dh2, o_mla, o_swa, g_mla, g_swa, w_o)


def _loss_head(h3, target, g_final):
    T, D = h3.shape
    tm = _pick(T, (512, 256, 128))

    def body(h_ref, t_ref, g_ref, dh_ref, loss_ref, dg_ref):
        hv, gv = h_ref[...], g_ref[...]
        y, r = _rms_fwd(hv, gv)
        diff = y - t_ref[...]
        dxn, dgp = _rms_bwd(hv, gv, r, diff * (1.0 / D))
        dh_ref[...] = dxn
        part = 0.5 * jnp.sum(jnp.sum(diff * diff, axis=-1, keepdims=True) * (1.0 / D), axis=0, keepdims=True)

        @pl.when(pl.program_id(0) == 0)
        def _():
            loss_ref[...] = jnp.zeros_like(loss_ref)
            dg_ref[...] = jnp.zeros_like(dg_ref)

        loss_ref[...] += jnp.broadcast_to(part, loss_ref.shape)
        dg_ref[...] += dgp

    return pl.pallas_call(
        body, name="loss_head", grid=(T // tm,),
        in_specs=[_row_spec(tm, D), _row_spec(tm, D), _const_spec((1, D))],
        out_specs=[_row_spec(tm, D), _acc_spec((1, 128)), _acc_spec((1, D))],
        out_shape=[jax.ShapeDtypeStruct((T, D), F32), jax.ShapeDtypeStruct((1, 128), F32), jax.ShapeDtypeStruct((1, D), F32)],
        compiler_params=_cparams(("arbitrary",)),
    )(h3, target, g_final)


def _half_mask(shape, half):
    lane = lax.broadcasted_iota(jnp.int32, shape, len(shape) - 1)
    return (lane < 64) if half == 0 else (lane >= 64)


def _mla_fwd(q, k, v, B, S, exchange=None):
    T = B * S
    tq = _pick(S, MLA_Q_TILES)
    nq = S // tq
    pairs = MLA_HEADS // 2

    def body(q_ref, k_ref, v_ref, o_ref, lse_ref):
        causal = lax.broadcasted_iota(jnp.int32, (tq, tq), 1) <= lax.broadcasted_iota(jnp.int32, (tq, tq), 0)
        low = _half_mask((tq, 2 * V_HEAD), 0)
        for qi in range(nq):
            rows = slice(qi * tq, (qi + 1) * tq)
            past = slice(0, qi * tq)
            outs, lses = [], []
            for half in range(2):
                lanes = slice(half * HEAD_BLOCK, (half + 1) * HEAD_BLOCK)
                qv = q_ref[rows, lanes]
                s_d = jnp.where(causal, _dot_nt(qv, k_ref[rows, lanes]) * MLA_SCALE, NEG)
                m = jnp.max(s_d, axis=-1, keepdims=True)
                if qi:
                    s_p = _dot_nt(qv, k_ref[past, lanes]) * MLA_SCALE
                    m = jnp.maximum(m, jnp.max(s_p, axis=-1, keepdims=True))
                p_d = jnp.exp(s_d - m)
                l = jnp.sum(p_d, axis=-1, keepdims=True)
                acc = _dot(p_d.astype(BF16), v_ref[rows, :])
                if qi:
                    p_p = jnp.exp(s_p - m)
                    l = l + jnp.sum(p_p, axis=-1, keepdims=True)
                    acc = acc + _dot(p_p.astype(BF16), v_ref[past, :])
                outs.append(acc * (1.0 / l))
                lses.append(jnp.broadcast_to(m + jnp.log(l), (tq, 2 * V_HEAD)))
            o_ref[rows, :] = jnp.where(low, outs[0], outs[1])
            lse_ref[rows, :] = jnp.where(low, lses[0], lses[1])

    blk2 = pl.BlockSpec((S, 2 * HEAD_BLOCK), lambda b, j: (b, j))
    blk1 = pl.BlockSpec((S, 2 * V_HEAD), lambda b, j: (b, j))
    return _pcall(
        body, name="mla_fwd", grid=(B, pairs),
        in_specs=[blk2, blk2, blk1], out_specs=[blk1, blk1],
        out_shape=[jax.ShapeDtypeStruct((T, pairs * 2 * V_HEAD), F32)] * 2,
        args=(q, k, v), exchange=exchange)


def _mla_bwd(q, k, v, do, o, lse, B, S, exchange=None):
    T = B * S
    tq = _pick(S, MLA_Q_TILES)
    nq = S // tq
    pairs = MLA_HEADS // 2

    def body(q_ref, k_ref, v_ref, do_ref, o_ref, lse_ref, dq_ref, dk_ref, dv_ref, dk_acc):
        causal = lax.broadcasted_iota(jnp.int32, (tq, tq), 1) <= lax.broadcasted_iota(jnp.int32, (tq, tq), 0)
        for half in range(2):
            lanes = slice(half * HEAD_BLOCK, (half + 1) * HEAD_BLOCK)
            col = half * V_HEAD
            own = _half_mask((tq, 2 * V_HEAD), half)
            for qi in reversed(range(nq)):
                rows = slice(qi * tq, (qi + 1) * tq)
                past = slice(0, qi * tq)
                first = qi == nq - 1
                qv = q_ref[rows, lanes]
                dov = jnp.where(own, do_ref[rows, :], 0.0)
                dsum = jnp.sum(dov * o_ref[rows, :], axis=-1, keepdims=True)
                dob = dov.astype(BF16)
                lse_col = lse_ref[rows, col:col + 1]

                def grads(keys, mask):
                    kv, vv = k_ref[keys, lanes], v_ref[keys, :]
                    p = jnp.exp(_dot_nt(qv, kv) * MLA_SCALE - lse_col)
                    if mask is not None:
                        p = jnp.where(mask, p, 0.0)
                    ds = (p * (_dot_nt(dob, vv) - dsum) * MLA_SCALE).astype(BF16)
                    dk, dv = _dot_tn(ds, qv), _dot_tn(p.astype(BF16), dob)
                    if first:
                        dk_acc[keys, :] = dk
                    else:
                        dk_acc[keys, :] += dk
                    if first and half == 0:
                        dv_ref[keys, :] = dv
                    else:
                        dv_ref[keys, :] += dv
                    return _dot(ds, kv)

                dq = grads(rows, causal)
                if qi:
                    dq = dq + grads(past, None)
                dq_ref[rows, lanes] = dq
            dk_ref[:, lanes] = dk_acc[...]

    blk2 = pl.BlockSpec((S, 2 * HEAD_BLOCK), lambda b, j: (b, j))
    blk1 = pl.BlockSpec((S, 2 * V_HEAD), lambda b, j: (b, j))
    return _pcall(
        body, name="mla_bwd", grid=(B, pairs),
        in_specs=[blk2, blk2, blk1, blk1, blk1, blk1], out_specs=[blk2, blk2, blk1],
        out_shape=[jax.ShapeDtypeStruct((T, MLA_HEADS * HEAD_BLOCK), F32), jax.ShapeDtypeStruct((T, MLA_HEADS * HEAD_BLOCK), F32),
                   jax.ShapeDtypeStruct((T, pairs * 2 * V_HEAD), F32)],
        scratch_shapes=[pltpu.VMEM((S, HEAD_BLOCK), F32)],
        args=(q, k, v, do, o, lse), exchange=exchange)


SWA_GROUP = SWA_HEADS // SWA_KV_HEADS
SWA_ROWS = SWA_GROUP * WINDOW


SWA_CHUNK = 8


def _bdot(a, b, ca, cb):
    return lax.dot_general(a, b, (((ca,), (cb,)), ((0,), (0,))), preferred_element_type=F32)


def _swa_chunks(nb):
    cb = next(c for c in (SWA_CHUNK, 4, 2, 1) if nb % c == 0)
    return cb, [(n0, slice(n0 * WINDOW, (n0 + cb) * WINDOW)) for n0 in range(0, nb, cb)]


def _swa_keys(ref, n0, cb):
    W = WINDOW
    cur = ref[n0 * W:(n0 + cb) * W, :]
    if n0:
        prev = ref[(n0 - 1) * W:(n0 + cb - 1) * W, :]
    else:
        prev = jnp.concatenate([cur[:W], cur[:(cb - 1) * W]], axis=0) if cb > 1 else cur
    return jnp.concatenate([prev.reshape(cb, W, W), cur.reshape(cb, W, W)], axis=1)


def _swa_stack(ref, rows, cb):
    return jnp.concatenate([ref[rows, j * WINDOW:(j + 1) * WINDOW].reshape(cb, WINDOW, WINDOW) for j in range(SWA_GROUP)], axis=1)


def _swa_unstack(low, high, cb):
    sel = _half_mask((cb, WINDOW, WINDOW), 0)
    return jnp.concatenate([jnp.where(sel, low[:, j * WINDOW:(j + 1) * WINDOW], high[:, j * WINDOW:(j + 1) * WINDOW])
                            .reshape(cb * WINDOW, WINDOW) for j in range(SWA_GROUP)], axis=1)


def _swa_sink_col(sink_ref, g):
    return jnp.concatenate([jnp.broadcast_to(sink_ref[g * SWA_GROUP + j:g * SWA_GROUP + j + 1, 0:1], (WINDOW, 1))
                            for j in range(SWA_GROUP)], axis=0)


def _swa_probs(qm, kcat, bias_g, sink, first_chunk):
    shape = (qm.shape[0], SWA_ROWS, 2 * WINDOW)
    qi = lax.broadcasted_iota(jnp.int32, shape, 1) % WINDOW
    col = lax.broadcasted_iota(jnp.int32, shape, 2)
    valid = (col > qi) & (col <= qi + WINDOW)
    if first_chunk:
        valid = valid & ((col >= WINDOW) | (lax.broadcasted_iota(jnp.int32, shape, 0) > 0))
    s = jnp.where(valid, _bdot(qm, kcat, 2, 2) * SWA_SCALE + bias_g, NEG)
    m = jnp.maximum(jnp.max(s, axis=-1, keepdims=True), sink)
    e = jnp.exp(s - m)
    e_s = jnp.exp(sink - m)
    inv = 1.0 / (jnp.sum(e, axis=-1, keepdims=True) + e_s)
    return e * inv, e_s * inv


def _swa_fwd(q, k, v, bias, sinks, B, S):
    T = B * S
    W = WINDOW
    nb = S // W

    cb, chunks = _swa_chunks(nb)

    def body(q_ref, k_ref, v_ref, bias_ref, sink_ref, o_ref):
        sink_cols = [_swa_sink_col(sink_ref, g) for g in range(SWA_KV_HEADS)]
        for n0, rows in chunks:
            kcat, vcat = _swa_keys(k_ref, n0, cb), _swa_keys(v_ref, n0, cb)
            qcat = _swa_stack(q_ref, rows, cb)
            outs = []
            for g in range(SWA_KV_HEADS):
                qm = jnp.where(_half_mask(qcat.shape, g), qcat, jnp.zeros_like(qcat))
                p, _ = _swa_probs(qm, kcat, bias_ref[g], sink_cols[g], n0 == 0)
                outs.append(_bdot(p.astype(BF16), vcat, 2, 1))
            o_ref[rows, :] = _swa_unstack(outs[0], outs[1], cb)

    SQ, SK = SWA_HEADS * SWA_HEAD_DIM, SWA_KV_HEADS * SWA_HEAD_DIM
    return pl.pallas_call(
        body, name="swa_fwd", grid=(B,),
        in_specs=[pl.BlockSpec((S, SQ), lambda b: (b, 0)), pl.BlockSpec((S, SK), lambda b: (b, 0)),
                  pl.BlockSpec((S, SK), lambda b: (b, 0)), _const_spec((SWA_KV_HEADS, SWA_ROWS, 2 * W)),
                  _const_spec((SWA_HEADS, 128))],
        out_specs=pl.BlockSpec((S, SQ), lambda b: (b, 0)),
        out_shape=jax.ShapeDtypeStruct((T, SQ), F32),
        compiler_params=_cparams(("arbitrary",)),
    )(q, k, v, bias, sinks)


def _swa_bwd(q, k, v, do, bias, sinks, B, S):
    T = B * S
    W = WINDOW
    nb = S // W
    cb, chunks = _swa_chunks(nb)

    def body(q_ref, k_ref, v_ref, do_ref, bias_ref, sink_ref, dq_ref, dk_ref, dv_ref, dbias_ref, dsink_ref):
        @pl.when(pl.program_id(0) == 0)
        def _():
            dbias_ref[...] = jnp.zeros_like(dbias_ref)
            dsink_ref[...] = jnp.zeros_like(dsink_ref)

        dk_ref[...] = jnp.zeros_like(dk_ref)
        dv_ref[...] = jnp.zeros_like(dv_ref)
        sink_cols = [_swa_sink_col(sink_ref, g) for g in range(SWA_KV_HEADS)]
        for n0, rows in chunks:
            kcat, vcat = _swa_keys(k_ref, n0, cb), _swa_keys(v_ref, n0, cb)
            qcat = _swa_stack(q_ref, rows, cb)
            docat = _swa_stack(do_ref, rows, cb)
            dk = jnp.zeros((cb, 2 * W, W), F32)
            dv = jnp.zeros((cb, 2 * W, W), F32)
            dqs = []
            for g in range(SWA_KV_HEADS):
                own = _half_mask(qcat.shape, g)
                qm = jnp.where(own, qcat, jnp.zeros_like(qcat))
                dom = jnp.where(own, docat, 0.0).astype(BF16)
                p, p_s = _swa_probs(qm, kcat, bias_ref[g], sink_cols[g], n0 == 0)
                dp = _bdot(dom, vcat, 2, 2)
                dsum = jnp.sum(p * dp, axis=-1, keepdims=True)
                ds = p * (dp - dsum)
                dbias_ref[g] += jnp.sum(ds, axis=0)
                dsink_ref[g] += jnp.broadcast_to(-jnp.sum(p_s * dsum, axis=0), (SWA_ROWS, 128))
                dsb = (ds * SWA_SCALE).astype(BF16)
                dqs.append(_bdot(dsb, kcat, 2, 1))
                dk = dk + _bdot(dsb, qm, 1, 1)
                dv = dv + _bdot(p.astype(BF16), dom, 1, 1)
            dq_ref[rows, :] = _swa_unstack(dqs[0], dqs[1], cb)
            dk_ref[rows, :] += dk[:, W:].reshape(cb * W, W)
            dv_ref[rows, :] += dv[:, W:].reshape(cb * W, W)
            dk_prev, dv_prev = dk[:, :W].reshape(cb * W, W), dv[:, :W].reshape(cb * W, W)
            if n0:
                before = slice((n0 - 1) * W, (n0 + cb - 1) * W)
                dk_ref[before, :] += dk_prev
                dv_ref[before, :] += dv_prev
            elif cb > 1:
                before = slice(0, (cb - 1) * W)
                dk_ref[before, :] += dk_prev[W:]
                dv_ref[before, :] += dv_prev[W:]

    SQ, SK = SWA_HEADS * SWA_HEAD_DIM, SWA_KV_HEADS * SWA_HEAD_DIM
    row = lambda w: pl.BlockSpec((S, w), lambda b: (b, 0))
    return pl.pallas_call(
        body, name="swa_bwd", grid=(B,),
        in_specs=[row(SQ), row(SK), row(SK), row(SQ), _const_spec((SWA_KV_HEADS, SWA_ROWS, 2 * W)), _const_spec((SWA_HEADS, 128))],
        out_specs=[row(SQ), row(SK), row(SK), _acc_spec((SWA_KV_HEADS, SWA_ROWS, 2 * W)), _acc_spec((SWA_KV_HEADS, SWA_ROWS, 128))],
        out_shape=[jax.ShapeDtypeStruct((T, SQ), F32), jax.ShapeDtypeStruct((T, SK), F32), jax.ShapeDtypeStruct((T, SK), F32),
                   jax.ShapeDtypeStruct((SWA_KV_HEADS, SWA_ROWS, 2 * W), F32), jax.ShapeDtypeStruct((SWA_KV_HEADS, SWA_ROWS, 128), F32)],
        compiler_params=_cparams(("arbitrary",)),
    )(q, k, v, do, bias, sinks)


def _bias_expand(rel_bias, bucket):
    W = WINDOW

    def body(rb_ref, bucket_ref, out_ref):
        bk = bucket_ref[...]
        for h in range(SWA_HEADS):
            def add(b, acc):
                return jnp.where(bk == b, rb_ref[b, h], acc)

            out_ref[h] = lax.fori_loop(0, REL_BUCKETS, add, jnp.zeros((W, 2 * W), F32))

    return pl.pallas_call(
        body, name="bias_expand",
        in_specs=[pl.BlockSpec(memory_space=pltpu.SMEM), pl.BlockSpec(memory_space=pltpu.VMEM)],
        out_specs=pl.BlockSpec(memory_space=pltpu.VMEM),
        out_shape=jax.ShapeDtypeStruct((SWA_HEADS, W, 2 * W), F32),
        compiler_params=_cparams(),
    )(rel_bias, bucket)


def _bias_reduce(dbias, dsink_rows, bucket):
    def body(db_ref, ds_ref, bucket_ref, out_ref, sink_out_ref):
        sink_lane = lax.broadcasted_iota(jnp.int32, (1, 128), 1)
        sink_acc = jnp.zeros((1, 128), F32)
        for h in range(SWA_HEADS):
            sink_acc = jnp.where(sink_lane == h, jnp.sum(ds_ref[h], axis=0, keepdims=True), sink_acc)
        sink_out_ref[...] = sink_acc
        bk = bucket_ref[...]
        row = lax.broadcasted_iota(jnp.int32, (REL_BUCKETS, 128), 0)
        lane = lax.broadcasted_iota(jnp.int32, (REL_BUCKETS, 128), 1)
        acc = jnp.zeros((REL_BUCKETS, 128), F32)
        for h in range(SWA_HEADS):
            dbh = db_ref[h]

            def add(b, acc):
                part = jnp.sum(jnp.sum(jnp.where(bk == b, dbh, 0.0), axis=-1, keepdims=True), axis=0, keepdims=True)
                return jnp.where((row == b) & (lane == h), part, acc)

            acc = lax.fori_loop(0, REL_BUCKETS, add, acc)
        out_ref[...] = acc

    return pl.pallas_call(
        body, name="bias_reduce",
        out_shape=[jax.ShapeDtypeStruct((REL_BUCKETS, 128), F32), jax.ShapeDtypeStruct((1, 128), F32)],
        compiler_params=_cparams(),
    )(dbias, dsink_rows, bucket)


def _t5_bucket(dist):
    n = jnp.maximum(dist, 0)
    max_exact = REL_BUCKETS // 2
    nf = jnp.maximum(n, 1).astype(F32)
    large = max_exact + (jnp.log(nf / max_exact) / math.log(REL_MAX_DIST / max_exact) * (REL_BUCKETS - max_exact)).astype(jnp.int32)
    large = jnp.minimum(large, REL_BUCKETS - 1)
    return jnp.where(n < max_exact, n, large)


def _pair_heads(a, axis, inverse=False):
    shp = a.shape
    a = a.reshape(shp[:axis] + ((4, 2, 64) if inverse else (2, 4, 64)) + shp[axis + 1:])
    a = jnp.swapaxes(a, axis, axis + 1)
    return a.reshape(shp)


def _owner_blocks(w, me):
    r, c = w.shape[0] // N_DEV, w.shape[1]
    return lax.dynamic_slice(w, (me * r, 0), (r, c)), w.astype(BF16).reshape(N_DEV, r, c)


_IN_WIDTHS = (Q_LORA, KV_LORA, QK_ROPE, SWA_HEADS * SWA_HEAD_DIM, SWA_KV_HEADS * SWA_HEAD_DIM, SWA_KV_HEADS * SWA_HEAD_DIM)


def _win_to_kernel(wt):
    parts, o = [], 0
    for wd in _IN_WIDTHS:
        parts.append(wt[o:o + wd])
        o += wd
    cq, ckv, kpe, qs, ks, vs = parts
    return jnp.concatenate([cq, ckv, _pair_heads(qs, 0), ks, vs, jnp.pad(kpe, ((64, 32), (0, 0)))], axis=0)


def _win_from_kernel(dwt):
    o = [0, Q_LORA, Q_LORA + KV_LORA, Q_LORA + KV_LORA + 512, Q_LORA + KV_LORA + 640, Q_LORA + KV_LORA + 768]
    cq, ckv, qs, ks, vs, kpe = [dwt[a:b] for a, b in zip(o, o[1:] + [dwt.shape[0]])]
    return jnp.concatenate([cq, ckv, kpe[64:96], _pair_heads(qs, 0, inverse=True), ks, vs], axis=0)


def _wq_to_kernel(wt):
    return jnp.pad(wt.reshape(MLA_HEADS, QK_NOPE + QK_ROPE, Q_LORA), ((0, 0), (0, 32), (0, 0))).reshape(MLA_HEADS * HEAD_BLOCK, Q_LORA)


def _wq_from_kernel(dwt):
    return dwt.reshape(MLA_HEADS, HEAD_BLOCK, Q_LORA)[:, :QK_NOPE + QK_ROPE].reshape(-1, Q_LORA)


def _wkv_to_kernel(wt):
    w3 = wt.reshape(MLA_HEADS, QK_NOPE + V_HEAD, KV_LORA)
    kpart = jnp.pad(w3[:, :QK_NOPE], ((0, 0), (0, 64), (0, 0))).reshape(MLA_HEADS * HEAD_BLOCK, KV_LORA)
    return jnp.concatenate([kpart, w3[:, QK_NOPE:].reshape(MLA_HEADS * V_HEAD, KV_LORA)], axis=0)


def _wkv_from_kernel(dwt):
    dk = dwt[:MLA_HEADS * HEAD_BLOCK].reshape(MLA_HEADS, HEAD_BLOCK, KV_LORA)[:, :QK_NOPE]
    dv = dwt[MLA_HEADS * HEAD_BLOCK:].reshape(MLA_HEADS, V_HEAD, KV_LORA)
    return jnp.concatenate([dk, dv], axis=1).reshape(-1, KV_LORA)


def _wo_to_kernel(w):
    return jnp.concatenate([w[:512], _pair_heads(w[512:], 0)], axis=0)


def _wo_from_kernel(dw):
    return jnp.concatenate([dw[:512], _pair_heads(dw[512:], 0, inverse=True)], axis=0)


def _kernel_weights(key, stored):
    if key in ("g1", "u1", "g2", "u2"):
        return {key: stored.T, key + "t": stored}
    if key in ("d1", "d2"):
        return {key: stored, key + "t": stored.T}
    if key == "o":
        return {key: _wo_to_kernel(stored)}
    kt = _TO_KERNEL[key](stored)
    return {key: kt.T, key + "t": kt}


_SMALL = ("g_ffn1", "g_mix", "g_q_a", "g_kv_a", "attn_sinks", "rel_bias", "g_out_mla", "g_out_swa", "g_ffn2", "g_final")


class _Comm:
    def __init__(self, shards, me):
        self.shards, self.me = shards, me
        self.own32, self.recv16 = {}, {}

    def gather(self, keys):
        return [("gather", self.shards[k]) for k in keys]

    def gathered(self, keys, results, wts):
        for k, g in zip(keys, results):
            wts.update(_kernel_weights(k, g.reshape(N_DEV * g.shape[1], g.shape[2])))

    def scatter(self, keys, grads):
        jobs = []
        for k in keys:
            g = _FROM_KERNEL[k](grads[k]) if k in _FROM_KERNEL else grads[k]
            self.own32[k], blocks = _owner_blocks(g, self.me)
            jobs.append(("scatter", blocks))
        return jobs

    def scattered(self, keys, results):
        self.recv16.update(zip(keys, results))


def _step(x, target, gains, wts, comm):
    def jobs_in(keys):
        return comm.gather(keys) if comm else None

    def jobs_out(keys):
        return comm.scatter(keys, grads) if comm else None

    B, S, D = x.shape
    T = B * S
    x2 = x.reshape(T, D)
    t2 = target.reshape(T, D)

    pos = jnp.arange(S, dtype=F32)
    inv_freq = ROPE_THETA ** (-jnp.arange(0, QK_ROPE, 2, dtype=F32) / QK_ROPE)
    ang = pos[:, None] * inv_freq[None, :]
    cos, sin = jnp.cos(ang), jnp.sin(ang)
    ones, zeros = jnp.ones((S, 64), F32), jnp.zeros((S, 32), F32)
    rope_c = jnp.concatenate([ones, cos, cos, zeros], axis=1)
    rope_s = jnp.concatenate([0.0 * ones, -sin, sin, zeros], axis=1)

    qi = jnp.arange(WINDOW)[:, None]
    kj = jnp.arange(2 * WINDOW)[None, :]
    bucket = _t5_bucket(qi + WINDOW - kj).astype(jnp.int32)
    bias = _bias_expand(gains["rel_bias"], bucket).reshape(SWA_KV_HEADS, SWA_ROWS, 2 * WINDOW)
    sinks = jnp.broadcast_to(gains["attn_sinks"].reshape(SWA_HEADS, 1), (SWA_HEADS, 128))
    g_swa = _pair_heads(gains["g_out_swa"], 1)

    keys = ("in", "q", "kv", "o", "g2")
    (h1, n1, gate1, up1, act1), got = _ffn_fwd(x2, gains["g_ffn1"], wts["g1"], wts["u1"], wts["d1"], "ffn1_fwd", jobs_in(keys))
    if comm:
        comm.gathered(keys, got, wts)
    (u, c_q, c_kv, cqn, ckvn, q, k, v, qs, ks, vs) = _proj_fwd(
        h1, gains["g_mix"], wts["in"], gains["g_q_a"], wts["q"], gains["g_kv_a"], wts["kv"], rope_c, rope_s, S)
    keys = ("u2", "d2")
    (o_mla, lse), got = _mla_fwd(q, k, v, B, S, jobs_in(keys))
    if comm:
        comm.gathered(keys, got, wts)
    o_swa = _swa_fwd(qs, ks, vs, bias, sinks, B, S)
    h2, oc = _out_fwd(o_mla, o_swa, gains["g_out_mla"], g_swa, wts["o"], h1)
    (h3, n2, gate2, up2, act2), _ = _ffn_fwd(h2, gains["g_ffn2"], wts["g2"], wts["u2"], wts["d2"], "ffn2_fwd")
    dh3, loss, dg_final = _loss_head(h3, t2, gains["g_final"].reshape(1, D))

    grads = {}
    (dh2, dg_ffn2, dgate2, dup2), _ = _ffn_bwd(h2, gains["g_ffn2"], gate2, up2, dh3, wts["g2t"], wts["u2t"], wts["d2t"], "ffn2_bwd")
    grads["g2"], _ = _matmul_tn(dgate2, n2, "dw_gate2")
    grads["u2"], _ = _matmul_tn(dup2, n2, "dw_up2")
    grads["d2"], _ = _matmul_tn(act2, dh3, "dw_down2", 0.5)
    do_mla, do_swa, dg_mla, dg_swa = _out_bwd(dh2, o_mla, o_swa, gains["g_out_mla"], g_swa, wts["o"])
    grads["o"], _ = _matmul_tn(oc, dh2, "dw_o")
    keys = ("g2", "u2", "d2")
    (dq, dk, dv), got = _mla_bwd(q, k, v, do_mla, o_mla, lse, B, S, jobs_out(keys))
    if comm:
        comm.scattered(keys, got)
    dqs, dks, dvs, dbias, dsink = _swa_bwd(qs, ks, vs, do_swa, bias, sinks, B, S)
    dh1, dproj, dql, dkvc, dg_mix, dg_q, dg_kv = _proj_bwd(
        dq, dk, dv, dqs, dks, dvs, c_q, c_kv, h1, dh2, gains["g_mix"], wts["int"], gains["g_q_a"], wts["qt"],
        gains["g_kv_a"], wts["kvt"], rope_c, rope_s, S)
    grads["in"], _ = _matmul_tn(dproj, u, "dw_in")
    grads["q"], _ = _matmul_tn(dql, cqn, "dw_q")
    grads["kv"], _ = _matmul_tn(dkvc, ckvn, "dw_kv")
    grads["d1"], _ = _matmul_tn(act1, dh1, "dw_down1", 0.5)
    keys = ("o", "in", "q", "kv", "d1")
    (dx, dg_ffn1, dgate1, dup1), got = _ffn_bwd(
        x2, gains["g_ffn1"], gate1, up1, dh1, wts["g1t"], wts["u1t"], wts["d1t"], "ffn1_bwd", jobs_out(keys))
    if comm:
        comm.scattered(keys, got)
    grads["g1"], _ = _matmul_tn(dgate1, n1, "dw_gate1")
    grads["u1"], got = _matmul_tn(dup1, n1, "dw_up1", exchange=jobs_out(("g1",)))
    if comm:
        comm.scattered(("g1",), got)
        comm.scattered(("u1",), _exchange(jobs_out(("u1",)), "grad_exchange_last"))

    d_rel_bias, d_sinks = _bias_reduce(dbias.reshape(SWA_HEADS, WINDOW, 2 * WINDOW), dsink.reshape(SWA_HEADS, WINDOW, 128), bucket)
    small = {
        "g_ffn1": dg_ffn1, "g_mix": dg_mix, "g_q_a": dg_q, "g_kv_a": dg_kv, "attn_sinks": d_sinks,
        "rel_bias": d_rel_bias, "g_out_mla": dg_mla, "g_out_swa": _pair_heads(dg_swa, 1, inverse=True),
        "g_ffn2": dg_ffn2, "g_final": dg_final, "loss": loss,
    }
    return dx.reshape(B, S, D), grads, small


_WEIGHTS = ("g_ffn1", "w_ffn1_gate", "w_ffn1_up", "w_ffn1_down", "g_mix", "w_in", "g_q_a", "w_q_b", "g_kv_a", "w_kv_b",
            "attn_sinks", "rel_bias", "g_out_mla", "g_out_swa", "w_o", "g_ffn2", "w_ffn2_gate", "w_ffn2_up", "w_ffn2_down",
            "g_final")
_BIG = (("w_ffn1_gate", "g1", True), ("w_ffn1_up", "u1", True), ("w_ffn1_down", "d1", False), ("w_in", "in", True),
        ("w_q_b", "q", True), ("w_kv_b", "kv", True), ("w_o", "o", False), ("w_ffn2_gate", "g2", True),
        ("w_ffn2_up", "u2", True), ("w_ffn2_down", "d2", False))
_TO_KERNEL = {"in": _win_to_kernel, "q": _wq_to_kernel, "kv": _wkv_to_kernel}
_FROM_KERNEL = {"in": _win_from_kernel, "q": _wq_from_kernel, "kv": _wkv_from_kernel, "o": _wo_from_kernel}


def kernel(x, g_ffn1, w_ffn1_gate, w_ffn1_up, w_ffn1_down, g_mix, w_in, g_q_a, w_q_b, g_kv_a, w_kv_b, attn_sinks, rel_bias, g_out_mla, g_out_swa, w_o, g_ffn2, w_ffn2_gate, w_ffn2_up, w_ffn2_down, g_final, loss_target, m_g_ffn1, m_w_ffn1_gate, m_w_ffn1_up, m_w_ffn1_down, m_g_mix, m_w_in, m_g_q_a, m_w_q_b, m_g_kv_a, m_w_kv_b, m_attn_sinks, m_rel_bias, m_g_out_mla, m_g_out_swa, m_w_o, m_g_ffn2, m_w_ffn2_gate, m_w_ffn2_up, m_w_ffn2_down, m_g_final, v_g_ffn1, v_w_ffn1_gate, v_w_ffn1_up, v_w_ffn1_down, v_g_mix, v_w_in, v_g_q_a, v_w_q_b, v_g_kv_a, v_w_kv_b, v_attn_sinks, v_rel_bias, v_g_out_mla, v_g_out_swa, v_w_o, v_g_ffn2, v_w_ffn2_gate, v_w_ffn2_up, v_w_ffn2_down, v_g_final):
    w = dict(zip(_WEIGHTS, (g_ffn1, w_ffn1_gate, w_ffn1_up, w_ffn1_down, g_mix, w_in, g_q_a, w_q_b, g_kv_a, w_kv_b, attn_sinks,
                            rel_bias, g_out_mla, g_out_swa, w_o, g_ffn2, w_ffn2_gate, w_ffn2_up, w_ffn2_down, g_final)))
    m = dict(zip(_WEIGHTS, (m_g_ffn1, m_w_ffn1_gate, m_w_ffn1_up, m_w_ffn1_down, m_g_mix, m_w_in, m_g_q_a, m_w_q_b, m_g_kv_a,
                            m_w_kv_b, m_attn_sinks, m_rel_bias, m_g_out_mla, m_g_out_swa, m_w_o, m_g_ffn2, m_w_ffn2_gate,
                            m_w_ffn2_up, m_w_ffn2_down, m_g_final)))
    v = dict(zip(_WEIGHTS, (v_g_ffn1, v_w_ffn1_gate, v_w_ffn1_up, v_w_ffn1_down, v_g_mix, v_w_in, v_g_q_a, v_w_q_b, v_g_kv_a,
                            v_w_kv_b, v_attn_sinks, v_rel_bias, v_g_out_mla, v_g_out_swa, v_w_o, v_g_ffn2, v_w_ffn2_gate,
                            v_w_ffn2_up, v_w_ffn2_down, v_g_final)))
    me = 4 * lax.axis_index("x") + 2 * lax.axis_index("y") + lax.axis_index("c")

    stored = lambda src, name, by_col: src[name][0].T if by_col else src[name][0]
    shards = dict(zip([key for _, key, _ in _BIG], _cast_bf16([stored(w, name, by_col) for name, _, by_col in _BIG])))
    comm = _Comm(shards, me)
    first = ("g1", "u1", "d1")
    wts = {}
    comm.gathered(first, _all_gather([shards[k] for k in first], "weights_all_gather"), wts)

    gains = {n: (w[n] if n == "rel_bias" else w[n].reshape(1, -1)) for n in _SMALL}
    grad_x, _, small = _step(x, loss_target, gains, wts, comm)

    small_gathered = _all_gather([small[n] for n in _SMALL] + [small["loss"]], "small_all_gather")

    out_g, out_d, out_m, out_v = {}, {}, {}, {}
    me1 = me.astype(jnp.int32).reshape(1)
    for name, key, by_col in _BIG:
        updates = _adam_big(stored(w, name, by_col), stored(m, name, by_col), stored(v, name, by_col),
                            comm.own32[key], comm.recv16[key], me1)
        out_g[name], out_d[name], out_m[name], out_v[name] = ((a.T if by_col else a)[None] for a in updates)
    as_2d = lambda src: [src[n] if src[n].ndim == 2 else src[n].reshape(1, -1) for n in _SMALL]
    loss, updates = _adam_small(as_2d(w), as_2d(m), as_2d(v), small_gathered[:-1], small_gathered[-1])
    for n, (g, d, mn, vn) in zip(_SMALL, updates):
        out_g[n], out_d[n], out_m[n], out_v[n] = (a.reshape(w[n].shape) for a in (g, d, mn, vn))

    return (loss[0, 0], grad_x, *[out_g[n] for n in _WEIGHTS], *[out_d[n] for n in _WEIGHTS],
            *[out_m[n] for n in _WEIGHTS], *[out_v[n] for n in _WEIGHTS])
```

```python
import functools
import math

import jax
import jax.numpy as jnp
from jax import lax
from jax.experimental import pallas as pl
from jax.experimental.pallas import tpu as pltpu

F32 = jnp.float32
BF16 = jnp.bfloat16
MESH = pl.DeviceIdType.MESH

EPS = 1e-6
NEG = -1e30
N_DEV = 8

MLA_HEADS = 8
Q_LORA = 256
KV_LORA = 128
QK_NOPE = 64
QK_ROPE = 32
V_HEAD = 64
ROPE_THETA = 10000.0
SWA_HEADS = 8
SWA_KV_HEADS = 2
SWA_HEAD_DIM = 64
WINDOW = 128
REL_BUCKETS = 32
REL_MAX_DIST = 128
HEAD_BLOCK = 128
MLA_Q_TILES = (512, 256, 128)
MLA_SCALE = (QK_NOPE + QK_ROPE) ** -0.5
SWA_SCALE = SWA_HEAD_DIM ** -0.5

ADAM_LR = 0.001
ADAM_B1 = 0.9
ADAM_B2 = 0.999
ADAM_EPS = 1e-08
ADAM_WD = 0.01
ADAM_STEP = 10

VMEM_LIMIT = 56 * 1024 * 1024


def _cparams(semantics=None):
    return pltpu.CompilerParams(dimension_semantics=semantics, vmem_limit_bytes=VMEM_LIMIT)


def _dot(a, b):
    return jnp.dot(a, b, preferred_element_type=F32)


def _dot_nt(a, b):
    return lax.dot_general(a, b, (((1,), (1,)), ((), ())), preferred_element_type=F32)


def _dot_tn(a, b):
    return lax.dot_general(a, b, (((0,), (0,)), ((), ())), preferred_element_type=F32)


def _rms_fwd(x, g):
    r = lax.rsqrt(jnp.mean(x * x, axis=-1, keepdims=True) + EPS)
    return x * r * g, r


def _rms_bwd(x, g, r, dy):
    xh = x * r
    dyg = dy * g
    dx = r * (dyg - xh * jnp.mean(dyg * xh, axis=-1, keepdims=True))
    return dx, jnp.sum(dy * xh, axis=0, keepdims=True)


def _swap16(x):
    n = x.shape[-1]
    up = pltpu.roll(x, n - 16, 1)
    down = pltpu.roll(x, 16, 1)
    lane = lax.broadcasted_iota(jnp.int32, x.shape, 1) % HEAD_BLOCK
    return jnp.where((lane >= 64) & (lane < 80), up, jnp.where((lane >= 80) & (lane < 96), down, 0.0))


def _const_spec(shape):
    return pl.BlockSpec(shape, lambda *_: (0,) * len(shape), pipeline_mode=pl.Buffered(1))


def _acc_spec(shape):
    return pl.BlockSpec(shape, lambda *_: (0,) * len(shape))


def _row_spec(tm, width):
    return pl.BlockSpec((tm, width), lambda i: (i, 0))


def _cast_bf16(arrays):
    n = len(arrays)

    def body(*refs):
        for src, dst in zip(refs[:n], refs[n:]):
            dst[...] = src[...].astype(BF16)

    return pl.pallas_call(
        body, name="cast_weights",
        out_shape=[jax.ShapeDtypeStruct(a.shape, BF16) for a in arrays],
        compiler_params=_cparams(),
    )(*arrays)


def _all_gather(arrays, name):
    n = len(arrays)

    def body(*refs):
        ins, outs = refs[:n], refs[n:2 * n]
        send_sems, recv_sems, local_sems = refs[2 * n:]
        x, y, c = lax.axis_index("x"), lax.axis_index("y"), lax.axis_index("c")
        me, sibling = (x, y, c), (x, y, 1 - c)
        chips = [(1 - x, y), (x, 1 - y), (1 - x, 1 - y)]

        def slot(a, dev):
            return outs[a].at[4 * dev[0] + 2 * dev[1] + dev[2]]

        def copy(a, k, block, to, src=None):
            return pltpu.make_async_remote_copy(
                src_ref=slot(a, block) if src is None else src, dst_ref=slot(a, block),
                send_sem=send_sems.at[a, k], recv_sem=recv_sems.at[a, k], device_id=to, device_id_type=MESH)

        mine = [pltpu.make_async_copy(ins[a], slot(a, me), local_sems.at[a]) for a in range(n)]
        for cp in mine:
            cp.start()
        first = []
        for a in range(n):
            first.append(copy(a, 0, me, sibling, src=ins[a]))
            first += [copy(a, 1 + j, me, (*chip, c), src=ins[a]) for j, chip in enumerate(chips)]
        for cp in first:
            cp.start()
        passed = []
        for j, chip in enumerate(chips):
            for a in range(n):
                copy(a, 1 + j, (*chip, c), me).wait_recv()
                cp = copy(a, 4 + j, (*chip, c), sibling)
                cp.start()
                passed.append(cp)
        for a in range(n):
            copy(a, 0, sibling, me).wait_recv()
            for j, chip in enumerate(chips):
                copy(a, 4 + j, (*chip, 1 - c), me).wait_recv()
        for cp in first + passed:
            cp.wait_send()
        for cp in mine:
            cp.wait()

    any_spec = pl.BlockSpec(memory_space=pl.ANY)
    return pl.pallas_call(
        body, name=name,
        out_shape=[jax.ShapeDtypeStruct((N_DEV,) + a.shape, a.dtype) for a in arrays],
        in_specs=[any_spec] * n, out_specs=[any_spec] * n,
        scratch_shapes=[pltpu.SemaphoreType.DMA((n, 7)), pltpu.SemaphoreType.DMA((n, 7)), pltpu.SemaphoreType.DMA((n,))],
    )(*arrays)


def _exchange_shapes(jobs):
    dsts = [jax.ShapeDtypeStruct(((N_DEV,) + a.shape) if kind == "gather" else a.shape, a.dtype) for kind, a in jobs]
    n = len(jobs)
    sems = [pltpu.SemaphoreType.DMA((n, N_DEV - 1)), pltpu.SemaphoreType.DMA((n, N_DEV - 1)), pltpu.SemaphoreType.DMA((n,))]
    return dsts, sems


def _exchange_copies(kinds, srcs, dsts, send_sems, recv_sems, local_sems):
    x, y, c = lax.axis_index("x"), lax.axis_index("y"), lax.axis_index("c")
    me = 4 * x + 2 * y + c
    remote, local = [], []
    for a, kind in enumerate(kinds):
        for k in range(1, N_DEV):
            peer = (1 - x if k & 4 else x, 1 - y if k & 2 else y, 1 - c if k & 1 else c)
            src = srcs[a] if kind == "gather" else srcs[a].at[4 * peer[0] + 2 * peer[1] + peer[2]]
            remote.append(pltpu.make_async_remote_copy(
                src_ref=src, dst_ref=dsts[a].at[me], send_sem=send_sems.at[a, k - 1], recv_sem=recv_sems.at[a, k - 1],
                device_id=peer, device_id_type=MESH))
        src = srcs[a] if kind == "gather" else srcs[a].at[me]
        local.append(pltpu.make_async_copy(src, dsts[a].at[me], local_sems.at[a]))
    return remote, local


def _exchange(jobs, name):
    n = len(jobs)
    kinds = [kind for kind, _ in jobs]
    dst_shapes, sems = _exchange_shapes(jobs)

    def body(*refs):
        remote, local = _exchange_copies(kinds, refs[:n], refs[n:2 * n], *refs[2 * n:])
        for cp in local + remote:
            cp.start()
        for cp in remote + local:
            cp.wait()

    any_spec = pl.BlockSpec(memory_space=pl.ANY)
    return pl.pallas_call(
        body, name=name, out_shape=dst_shapes, in_specs=[any_spec] * n, out_specs=[any_spec] * n, scratch_shapes=sems,
    )(*[a for _, a in jobs])


def _pcall(body, *, name, grid, in_specs, out_specs, out_shape, args, scratch_shapes=(), exchange=None):
    semantics = ("arbitrary",) * len(grid)
    if not exchange:
        outs = pl.pallas_call(body, name=name, grid=grid, in_specs=list(in_specs), out_specs=list(out_specs),
                              out_shape=list(out_shape), scratch_shapes=list(scratch_shapes),
                              compiler_params=_cparams(semantics))(*args)
        return list(outs), []
    kinds = [kind for kind, _ in exchange]
    n_in, n_out, n_job = len(in_specs), len(out_specs), len(exchange)
    dst_shapes, sems = _exchange_shapes(exchange)

    def with_exchange(*refs):
        ins, srcs = refs[:n_in], refs[n_in:n_in + n_job]
        outs, dsts = refs[n_in + n_job:n_in + n_job + n_out], refs[n_in + n_job + n_out:n_in + 2 * n_job + n_out]
        scratch, sem_refs = refs[n_in + 2 * n_job + n_out:-3], refs[-3:]
        first = functools.reduce(jnp.logical_and, [pl.program_id(d) == 0 for d in range(len(grid))])
        last = functools.reduce(jnp.logical_and, [pl.program_id(d) == grid[d] - 1 for d in range(len(grid))])

        @pl.when(first)
        def _():
            remote, local = _exchange_copies(kinds, srcs, dsts, *sem_refs)
            for cp in local + remote:
                cp.start()

        body(*ins, *outs, *scratch)

        @pl.when(last)
        def _():
            remote, local = _exchange_copies(kinds, srcs, dsts, *sem_refs)
            for cp in remote + local:
                cp.wait()

    any_spec = pl.BlockSpec(memory_space=pl.ANY)
    outs = pl.pallas_call(
        with_exchange, name=name, grid=grid, in_specs=list(in_specs) + [any_spec] * n_job,
        out_specs=list(out_specs) + [any_spec] * n_job, out_shape=list(out_shape) + dst_shapes,
        scratch_shapes=list(scratch_shapes) + sems, compiler_params=_cparams(semantics),
    )(*args, *[a for _, a in exchange])
    return list(outs[:n_out]), list(outs[n_out:])


def _adam_math(w, g, m, v):
    m = ADAM_B1 * m + (1.0 - ADAM_B1) * g
    v = ADAM_B2 * v + (1.0 - ADAM_B2) * (g * g)
    m_hat = m / (1.0 - ADAM_B1 ** ADAM_STEP)
    v_hat = v / (1.0 - ADAM_B2 ** ADAM_STEP)
    delta = -ADAM_LR * (m_hat / (jnp.sqrt(v_hat) + ADAM_EPS) + ADAM_WD * w)
    return delta, m, v


def _adam_big(w, m, v, own32, recv16, me):
    r, c = w.shape
    tr = r if r <= 512 else (512 if r % 512 == 0 else 352)
    assert r % tr == 0

    def body(me_ref, w_ref, m_ref, v_ref, s_ref, r_ref, g_out, d_out, m_out, v_out):
        g = s_ref[...]
        for k in range(1, N_DEV):
            g = g + r_ref[jnp.bitwise_xor(me_ref[0], k)].astype(F32)
        d, mn, vn = _adam_math(w_ref[...], g, m_ref[...], v_ref[...])
        g_out[...] = g
        d_out[...] = d
        m_out[...] = mn
        v_out[...] = vn

    blk = pl.BlockSpec((tr, c), lambda i, s: (i, 0))
    grid_spec = pltpu.PrefetchScalarGridSpec(
        num_scalar_prefetch=1, grid=(r // tr,),
        in_specs=[blk, blk, blk, blk, pl.BlockSpec((N_DEV, tr, c), lambda i, s: (0, i, 0))],
        out_specs=[blk, blk, blk, blk])
    return pl.pallas_call(
        body, name="adamw_big", grid_spec=grid_spec,
        out_shape=[jax.ShapeDtypeStruct((r, c), F32)] * 4,
        compiler_params=_cparams(("arbitrary",)),
    )(me, w, m, v, own32, recv16)


def _adam_small(ws, ms, vs, partials, loss_parts):
    n = len(ws)

    def body(*refs):
        w_refs, m_refs, v_refs, p_refs = (refs[i * n:(i + 1) * n] for i in range(4))
        l_ref, loss_out = refs[4 * n], refs[4 * n + 1]
        outs = refs[4 * n + 2:]

        def total(ref):
            s = ref[0]
            for dev in range(1, N_DEV):
                s = s + ref[dev]
            return s

        loss_out[...] = total(l_ref)
        for i in range(n):
            r, c = w_refs[i].shape
            g = total(p_refs[i])[:, :c]
            d, mn, vn = _adam_math(w_refs[i][...], g, m_refs[i][...], v_refs[i][...])
            for j, val in enumerate((g, d, mn, vn)):
                outs[4 * i + j][...] = val

    shapes = [jax.ShapeDtypeStruct((1, 128), F32)]
    for w in ws:
        shapes += [jax.ShapeDtypeStruct(w.shape, F32)] * 4
    outs = pl.pallas_call(body, name="adamw_small", out_shape=shapes, compiler_params=_cparams())(*ws, *ms, *vs, *partials, loss_parts)
    return outs[0], [outs[1 + 4 * i:5 + 4 * i] for i in range(n)]


def _pick(n, prefer):
    for t in prefer:
        if n % t == 0:
            return t
    return n


def _ffn_fwd(x, g, wg, wu, wd, name, exchange=None, loss_head=None):
    T, D = x.shape
    F = wg.shape[1]
    tm = _pick(T, (512, 256, 128))
    fc = _pick(F, (256, 128))

    def body(x_ref, g_ref, wg_ref, wu_ref, wd_ref, *rest):
        h_ref, n_ref, gate_ref, up_ref, act_ref = rest[-7:-2] if loss_head else rest
        xv = x_ref[...]
        n, _ = _rms_fwd(xv, g_ref[...])
        nb = n.astype(BF16)
        n_ref[...] = nb
        acc = jnp.zeros((tm, D), F32)
        for f0 in range(0, F, fc):
            gate = _dot(nb, wg_ref[:, f0:f0 + fc])
            up = _dot(nb, wu_ref[:, f0:f0 + fc])
            gate_ref[:, f0:f0 + fc] = gate.astype(BF16)
            up_ref[:, f0:f0 + fc] = up.astype(BF16)
            act = (gate * (1.0 / (1.0 + jnp.exp(-gate))) * up).astype(BF16)
            act_ref[:, f0:f0 + fc] = act
            acc = acc + _dot(act, wd_ref[f0:f0 + fc, :])
        hv = xv + 0.5 * acc
        if not loss_head:
            h_ref[...] = hv
            return
        t_ref, gf_ref, loss_ref, dgf_ref = rest[0], rest[1], rest[-2], rest[-1]
        gf = gf_ref[...]
        y, r = _rms_fwd(hv, gf)
        diff = y - t_ref[...]
        dh, dgp = _rms_bwd(hv, gf, r, diff * (1.0 / D))
        h_ref[...] = dh
        part = 0.5 * jnp.sum(jnp.sum(diff * diff, axis=-1, keepdims=True) * (1.0 / D), axis=0, keepdims=True)

        @pl.when(pl.program_id(0) == 0)
        def _():
            loss_ref[...] = jnp.zeros_like(loss_ref)
            dgf_ref[...] = jnp.zeros_like(dgf_ref)

        loss_ref[...] += jnp.broadcast_to(part, loss_ref.shape)
        dgf_ref[...] += dgp

    in_specs = [_row_spec(tm, D), _const_spec((1, D)), _const_spec((D, F)), _const_spec((D, F)), _const_spec((F, D))]
    out_specs = [_row_spec(tm, D), _row_spec(tm, D), _row_spec(tm, F), _row_spec(tm, F), _row_spec(tm, F)]
    out_shape = [jax.ShapeDtypeStruct((T, D), F32), jax.ShapeDtypeStruct((T, D), BF16),
                 jax.ShapeDtypeStruct((T, F), BF16), jax.ShapeDtypeStruct((T, F), BF16), jax.ShapeDtypeStruct((T, F), BF16)]
    args = (x, g, wg, wu, wd)
    if loss_head:
        in_specs += [_row_spec(tm, D), _const_spec((1, D))]
        out_specs += [_acc_spec((1, 128)), _acc_spec((1, D))]
        out_shape += [jax.ShapeDtypeStruct((1, 128), F32), jax.ShapeDtypeStruct((1, D), F32)]
        args += tuple(loss_head)
    return _pcall(body, name=name, grid=(T // tm,), in_specs=in_specs, out_specs=out_specs, out_shape=out_shape, args=args,
                  exchange=exchange)


FFN_BWD_CHUNK = 1408
FFN_BWD_GROUP = 1408


def _ffn_bwd(x, g, gate, up, dh, wgt, wut, wdt, name, exchange=None):
    T, D = x.shape
    F = wgt.shape[0]
    tm = _pick(T, (512, 256, 128))
    halves = 2
    Fh = F // halves
    assert Fh * halves == F and Fh % 128 == 0
    groups = [(g0, min(g0 + FFN_BWD_GROUP, Fh)) for g0 in range(0, Fh, FFN_BWD_GROUP)]

    def body(x_ref, g_ref, gate_ref, up_ref, dh_ref, wg_ref, wu_ref, wd_ref, dx_ref, dg_ref, dgate_ref, dup_ref, dn_ref):
        i, h = pl.program_id(0), pl.program_id(1)
        dhv = dh_ref[...]
        dho = (0.5 * dhv).astype(BF16)
        dn = jnp.zeros((tm, D), F32)
        for g0, g1 in groups:
            for f0 in range(g0, g1, FFN_BWD_CHUNK):
                f1 = min(f0 + FFN_BWD_CHUNK, g1)
                d_act = _dot(dho, wd_ref[:, f0:f1])
                gt = gate_ref[:, f0:f1].astype(F32)
                u = up_ref[:, f0:f1].astype(F32)
                sg = 1.0 / (1.0 + jnp.exp(-gt))
                silu = gt * sg
                dup_ref[:, f0:f1] = (d_act * silu).astype(BF16)
                dgate_ref[:, f0:f1] = (d_act * u * (sg + silu * (1.0 - sg))).astype(BF16)
            dn = dn + _dot(dgate_ref[:, g0:g1], wg_ref[g0:g1, :]) + _dot(dup_ref[:, g0:g1], wu_ref[g0:g1, :])

        @pl.when(h == 0)
        def _():
            dn_ref[...] = dn

        @pl.when(h > 0)
        def _():
            dn_ref[...] += dn

        @pl.when((i == 0) & (h == 0))
        def _():
            dg_ref[...] = jnp.zeros_like(dg_ref)

        @pl.when(h == halves - 1)
        def _():
            xv, gv = x_ref[...], g_ref[...]
            _, r = _rms_fwd(xv, gv)
            dxn, dgp = _rms_bwd(xv, gv, r, dn_ref[...])
            dx_ref[...] = dhv + dxn
            dg_ref[...] += dgp

    tile = pl.BlockSpec((tm, D), lambda i, h: (i, 0))
    wide = pl.BlockSpec((tm, Fh), lambda i, h: (i, h))
    return _pcall(
        body, name=name, grid=(T // tm, halves),
        in_specs=[tile, _const_spec((1, D)), wide, wide, tile,
                  pl.BlockSpec((Fh, D), lambda i, h: (h, 0)), pl.BlockSpec((Fh, D), lambda i, h: (h, 0)),
                  pl.BlockSpec((D, Fh), lambda i, h: (0, h))],
        out_specs=[tile, _acc_spec((1, D)), wide, wide],
        out_shape=[jax.ShapeDtypeStruct((T, D), F32), jax.ShapeDtypeStruct((1, D), F32),
                   jax.ShapeDtypeStruct((T, F), BF16), jax.ShapeDtypeStruct((T, F), BF16)],
        scratch_shapes=[pltpu.VMEM((tm, D), F32)],
        args=(x, g, gate, up, dh, wgt, wut, wdt), exchange=exchange)


def _matmul_tn(a, b, name, scale=None, exchange=None):
    T, K = a.shape
    N = b.shape[1]
    tk = _pick(K, (1024, 1408, 1280, 768, 512, 256, 128))
    tn = _pick(N, (1024, 1408, 1280, 768, 512, 256, 128))
    tt = _pick(T, (2048, 1024, 512, 256, 128))
    n_t = T // tt

    def body(a_ref, b_ref, o_ref, o16_ref):
        @pl.when(pl.program_id(2) == 0)
        def _():
            o_ref[...] = jnp.zeros_like(o_ref)

        o_ref[...] += _dot_tn(a_ref[...].astype(BF16), b_ref[...].astype(BF16))

        @pl.when(pl.program_id(2) == n_t - 1)
        def _():
            if scale is not None:
                o_ref[...] = o_ref[...] * scale
            o16_ref[...] = o_ref[...].astype(BF16)

    tile = pl.BlockSpec((tk, tn), lambda i, j, t: (i, j))
    outs, got = _pcall(
        body, name=name, grid=(K // tk, N // tn, n_t),
        in_specs=[pl.BlockSpec((tt, tk), lambda i, j, t: (t, i)), pl.BlockSpec((tt, tn), lambda i, j, t: (t, j))],
        out_specs=[tile, tile],
        out_shape=[jax.ShapeDtypeStruct((K, N), F32), jax.ShapeDtypeStruct((K, N), BF16)],
        args=(a, b), exchange=exchange)
    return tuple(outs), got


def _rope_spec(tm, S):
    per = S // tm
    return pl.BlockSpec((tm, HEAD_BLOCK), lambda i: (i % per, 0))


def _proj_fwd(h1, g_mix, w_in, g_q, wq, g_kv, wkv, rope_c, rope_s, S):
    T, D = h1.shape
    P = w_in.shape[1]
    tm = _pick(S, (512, 256, 128))
    QW = MLA_HEADS * HEAD_BLOCK
    VW = MLA_HEADS * V_HEAD
    SQ = SWA_HEADS * SWA_HEAD_DIM
    SK = SWA_KV_HEADS * SWA_HEAD_DIM
    o_cq, o_ckv, o_qs, o_ks, o_vs, o_kpe = 0, Q_LORA, Q_LORA + KV_LORA, Q_LORA + KV_LORA + SQ, Q_LORA + KV_LORA + SQ + SK, Q_LORA + KV_LORA + SQ + 2 * SK
    assert P == o_kpe + HEAD_BLOCK

    def body(h_ref, gm_ref, win_ref, gq_ref, wq_ref, gkv_ref, wkv_ref, c_ref, s_ref,
             u_ref, cq_ref, ckv_ref, cqn_ref, ckvn_ref, q_ref, k_ref, v_ref, qs_ref, ks_ref, vs_ref):
        u, _ = _rms_fwd(h_ref[...], gm_ref[...])
        ub = u.astype(BF16)
        u_ref[...] = ub
        proj = _dot(ub, win_ref[...])
        c_q = proj[:, o_cq:o_ckv]
        c_kv = proj[:, o_ckv:o_qs]
        cq_ref[...] = c_q
        ckv_ref[...] = c_kv
        qs_ref[...] = proj[:, o_qs:o_ks].astype(BF16)
        ks_ref[...] = proj[:, o_ks:o_vs].astype(BF16)
        vs_ref[...] = proj[:, o_vs:o_kpe].astype(BF16)
        cb, sb = c_ref[...], s_ref[...]
        kpe = proj[:, o_kpe:P]
        kpe = kpe * cb + _swap16(kpe) * sb
        cqn, _ = _rms_fwd(c_q, gq_ref[...])
        cqn = cqn.astype(BF16)
        cqn_ref[...] = cqn
        q = _dot(cqn, wq_ref[...])
        q = q * jnp.tile(cb, (1, MLA_HEADS)) + _swap16(q) * jnp.tile(sb, (1, MLA_HEADS))
        q_ref[...] = q.astype(BF16)
        ckvn, _ = _rms_fwd(c_kv, gkv_ref[...])
        ckvn = ckvn.astype(BF16)
        ckvn_ref[...] = ckvn
        kv = _dot(ckvn, wkv_ref[...])
        k_ref[...] = (kv[:, :QW] + jnp.tile(kpe, (1, MLA_HEADS))).astype(BF16)
        v_ref[...] = kv[:, QW:].astype(BF16)

    widths = [(D, BF16), (Q_LORA, F32), (KV_LORA, F32), (Q_LORA, BF16), (KV_LORA, BF16), (QW, BF16), (QW, BF16), (VW, BF16),
              (SQ, BF16), (SK, BF16), (SK, BF16)]
    return pl.pallas_call(
        body, name="proj_fwd", grid=(T // tm,),
        in_specs=[_row_spec(tm, D), _const_spec((1, D)), _const_spec((D, P)), _const_spec((1, Q_LORA)),
                  _const_spec((Q_LORA, QW)), _const_spec((1, KV_LORA)), _const_spec((KV_LORA, QW + VW)),
                  _rope_spec(tm, S), _rope_spec(tm, S)],
        out_specs=[_row_spec(tm, w) for w, _ in widths],
        out_shape=[jax.ShapeDtypeStruct((T, w), dt) for w, dt in widths],
        compiler_params=_cparams(("arbitrary",)),
    )(h1, g_mix, w_in, g_q, wq, g_kv, wkv, rope_c, rope_s)


def _proj_bwd(dq, dk, dv, dqs, dks, dvs, c_q, c_kv, h1, dh2, g_mix, w_in, g_q, wq, g_kv, wkv, rope_c, rope_s, S):
    T, D = h1.shape
    P = w_in.shape[0]
    tm = _pick(S, (256, 128))
    QW = MLA_HEADS * HEAD_BLOCK
    VW = MLA_HEADS * V_HEAD
    SQ = SWA_HEADS * SWA_HEAD_DIM
    SK = SWA_KV_HEADS * SWA_HEAD_DIM

    def body(dq_ref, dk_ref, dv_ref, dqs_ref, dks_ref, dvs_ref, cq_ref, ckv_ref, h_ref, dh2_ref, gm_ref, win_ref, gq_ref,
             wq_ref, gkv_ref, wkv_ref, c_ref, s_ref, dh1_ref, dproj_ref, dql_ref, dkv_ref, dgm_ref, dgq_ref, dgkv_ref):
        cb, sb = c_ref[...], s_ref[...]
        dqv = dq_ref[...]
        dql = dqv * jnp.tile(cb, (1, MLA_HEADS)) + _swap16(dqv * jnp.tile(sb, (1, MLA_HEADS)))
        dql = dql.astype(BF16)
        dql_ref[...] = dql
        c_q = cq_ref[...]
        _, rq = _rms_fwd(c_q, gq_ref[...])
        d_cq, dgq = _rms_bwd(c_q, gq_ref[...], rq, _dot(dql, wq_ref[...]))

        dkv_all = dk_ref[...]
        dkpe = dkv_all[:, 0:HEAD_BLOCK]
        for h in range(1, MLA_HEADS):
            dkpe = dkpe + dkv_all[:, h * HEAD_BLOCK:(h + 1) * HEAD_BLOCK]
        lane = lax.broadcasted_iota(jnp.int32, dkpe.shape, 1)
        dkpe = jnp.where((lane >= 64) & (lane < 96), dkpe, 0.0)
        dkpe = dkpe * cb + _swap16(dkpe * sb)
        dkvc = jnp.concatenate([dkv_all.astype(BF16), dv_ref[...].astype(BF16)], axis=1)
        dkv_ref[...] = dkvc
        c_kv = ckv_ref[...]
        _, rkv = _rms_fwd(c_kv, gkv_ref[...])
        d_ckv, dgkv = _rms_bwd(c_kv, gkv_ref[...], rkv, _dot(dkvc, wkv_ref[...]))

        dproj = jnp.concatenate([d_cq.astype(BF16), d_ckv.astype(BF16), dqs_ref[...].astype(BF16),
                                 dks_ref[...].astype(BF16), dvs_ref[...].astype(BF16), dkpe.astype(BF16)], axis=1)
        dproj_ref[...] = dproj
        hv = h_ref[...]
        _, rm = _rms_fwd(hv, gm_ref[...])
        dxn, dgm = _rms_bwd(hv, gm_ref[...], rm, _dot(dproj, win_ref[...]))
        dh1_ref[...] = dh2_ref[...] + dxn

        @pl.when(pl.program_id(0) == 0)
        def _():
            dgm_ref[...] = jnp.zeros_like(dgm_ref)
            dgq_ref[...] = jnp.zeros_like(dgq_ref)
            dgkv_ref[...] = jnp.zeros_like(dgkv_ref)

        dgm_ref[...] += dgm
        dgq_ref[...] += dgq
        dgkv_ref[...] += dgkv

    return pl.pallas_call(
        body, name="proj_bwd", grid=(T // tm,),
        in_specs=[_row_spec(tm, QW), _row_spec(tm, QW), _row_spec(tm, VW), _row_spec(tm, SQ), _row_spec(tm, SK), _row_spec(tm, SK),
                  _row_spec(tm, Q_LORA), _row_spec(tm, KV_LORA), _row_spec(tm, D), _row_spec(tm, D),
                  _const_spec((1, D)), _const_spec((P, D)), _const_spec((1, Q_LORA)), _const_spec((QW, Q_LORA)),
                  _const_spec((1, KV_LORA)), _const_spec((QW + VW, KV_LORA)), _rope_spec(tm, S), _rope_spec(tm, S)],
        out_specs=[_row_spec(tm, D), _row_spec(tm, P), _row_spec(tm, QW), _row_spec(tm, QW + VW),
                   _acc_spec((1, D)), _acc_spec((1, Q_LORA)), _acc_spec((1, KV_LORA))],
        out_shape=[jax.ShapeDtypeStruct((T, D), F32), jax.ShapeDtypeStruct((T, P), BF16), jax.ShapeDtypeStruct((T, QW), BF16),
                   jax.ShapeDtypeStruct((T, QW + VW), BF16), jax.ShapeDtypeStruct((1, D), F32),
                   jax.ShapeDtypeStruct((1, Q_LORA), F32), jax.ShapeDtypeStruct((1, KV_LORA), F32)],
        compiler_params=_cparams(("arbitrary",)),
    )(dq, dk, dv, dqs, dks, dvs, c_q, c_kv, h1, dh2, g_mix, w_in, g_q, wq, g_kv, wkv, rope_c, rope_s)


def _out_fwd(o_mla, o_swa, g_mla, g_swa, w_o, h1):
    T, D = h1.shape
    W = o_mla.shape[1]
    tm = _pick(T, (512, 256, 128))

    def body(om_ref, os_ref, gm_ref, gs_ref, wo_ref, h_ref, h2_ref, oc_ref):
        a, _ = _rms_fwd(om_ref[...], gm_ref[...])
        b, _ = _rms_fwd(os_ref[...], gs_ref[...])
        oc = jnp.concatenate([a.astype(BF16), b.astype(BF16)], axis=1)
        oc_ref[...] = oc
        h2_ref[...] = h_ref[...] + _dot(oc, wo_ref[...])

    return pl.pallas_call(
        body, name="out_fwd", grid=(T // tm,),
        in_specs=[_row_spec(tm, W), _row_spec(tm, W), _const_spec((1, W)), _const_spec((1, W)), _const_spec((2 * W, D)),
                  _row_spec(tm, D)],
        out_specs=[_row_spec(tm, D), _row_spec(tm, 2 * W)],
        out_shape=[jax.ShapeDtypeStruct((T, D), F32), jax.ShapeDtypeStruct((T, 2 * W), BF16)],
        compiler_params=_cparams(("arbitrary",)),
    )(o_mla, o_swa, g_mla, g_swa, w_o, h1)


def _out_bwd(dh2, o_mla, o_swa, g_mla, g_swa, w_o):
    T, D = dh2.shape
    W = o_mla.shape[1]
    tm = _pick(T, (512, 256, 128))

    def body(dh_ref, om_ref, os_ref, gm_ref, gs_ref, wo_ref, dom_ref, dos_ref, dgm_ref, dgs_ref):
        doc = _dot_nt(dh_ref[...].astype(BF16), wo_ref[...])
        om, osw = om_ref[...], os_ref[...]
        _, ra = _rms_fwd(om, gm_ref[...])
        _, rb = _rms_fwd(osw, gs_ref[...])
        da, dga = _rms_bwd(om, gm_ref[...], ra, doc[:, :W])
        db, dgb = _rms_bwd(osw, gs_ref[...], rb, doc[:, W:])
        dom_ref[...] = da
        dos_ref[...] = db

        @pl.when(pl.program_id(0) == 0)
        def _():
            dgm_ref[...] = jnp.zeros_like(dgm_ref)
            dgs_ref[...] = jnp.zeros_like(dgs_ref)

        dgm_ref[...] += dga
        dgs_ref[...] += dgb

    return pl.pallas_call(
        body, name="out_bwd", grid=(T // tm,),
        in_specs=[_row_spec(tm, D), _row_spec(tm, W), _row_spec(tm, W), _const_spec((1, W)), _const_spec((1, W)),
                  _const_spec((2 * W, D))],
        out_specs=[_row_spec(tm, W), _row_spec(tm, W), _acc_spec((1, W)), _acc_spec((1, W))],
        out_shape=[jax.ShapeDtypeStruct((T, W), F32), jax.ShapeDtypeStruct((T, W), F32),
                   jax.ShapeDtypeStruct((1, W), F32), jax.ShapeDtypeStruct((1, W), F32)],
        compiler_params=_cparams(("arbitrary",)),
    )(dh2, o_mla, o_swa, g_mla, g_swa, w_o)


def _half_mask(shape, half):
    lane = lax.broadcasted_iota(jnp.int32, shape, len(shape) - 1)
    return (lane < 64) if half == 0 else (lane >= 64)


def _mla_fwd(q, k, v, B, S, exchange=None):
    T = B * S
    tq = _pick(S, MLA_Q_TILES)
    nq = S // tq
    pairs = MLA_HEADS // 2

    def body(q_ref, k_ref, v_ref, o_ref, lse_ref):
        causal = lax.broadcasted_iota(jnp.int32, (tq, tq), 1) <= lax.broadcasted_iota(jnp.int32, (tq, tq), 0)
        low = _half_mask((tq, 2 * V_HEAD), 0)
        for qi in range(nq):
            rows = slice(qi * tq, (qi + 1) * tq)
            past = slice(0, qi * tq)
            outs, lses = [], []
            for half in range(2):
                lanes = slice(half * HEAD_BLOCK, (half + 1) * HEAD_BLOCK)
                qv = q_ref[rows, lanes]
                s_d = jnp.where(causal, _dot_nt(qv, k_ref[rows, lanes]) * MLA_SCALE, NEG)
                m = jnp.max(s_d, axis=-1, keepdims=True)
                if qi:
                    s_p = _dot_nt(qv, k_ref[past, lanes]) * MLA_SCALE
                    m = jnp.maximum(m, jnp.max(s_p, axis=-1, keepdims=True))
                p_d = jnp.exp(s_d - m)
                l = jnp.sum(p_d, axis=-1, keepdims=True)
                acc = _dot(p_d.astype(BF16), v_ref[rows, :])
                if qi:
                    p_p = jnp.exp(s_p - m)
                    l = l + jnp.sum(p_p, axis=-1, keepdims=True)
                    acc = acc + _dot(p_p.astype(BF16), v_ref[past, :])
                outs.append(acc * (1.0 / l))
                lses.append(jnp.broadcast_to(m + jnp.log(l), (tq, 2 * V_HEAD)))
            o_ref[rows, :] = jnp.where(low, outs[0], outs[1])
            lse_ref[rows, :] = jnp.where(low, lses[0], lses[1])

    blk2 = pl.BlockSpec((S, 2 * HEAD_BLOCK), lambda b, j: (b, j))
    blk1 = pl.BlockSpec((S, 2 * V_HEAD), lambda b, j: (b, j))
    return _pcall(
        body, name="mla_fwd", grid=(B, pairs),
        in_specs=[blk2, blk2, blk1], out_specs=[blk1, blk1],
        out_shape=[jax.ShapeDtypeStruct((T, pairs * 2 * V_HEAD), F32)] * 2,
        args=(q, k, v), exchange=exchange)


def _mla_bwd(q, k, v, do, o, lse, B, S, exchange=None):
    T = B * S
    tq = _pick(S, MLA_Q_TILES)
    nq = S // tq
    pairs = MLA_HEADS // 2

    def body(q_ref, k_ref, v_ref, do_ref, o_ref, lse_ref, dq_ref, dk_ref, dv_ref, dk_acc):
        causal = lax.broadcasted_iota(jnp.int32, (tq, tq), 1) <= lax.broadcasted_iota(jnp.int32, (tq, tq), 0)
        for half in range(2):
            lanes = slice(half * HEAD_BLOCK, (half + 1) * HEAD_BLOCK)
            col = half * V_HEAD
            own = _half_mask((tq, 2 * V_HEAD), half)
            for qi in reversed(range(nq)):
                rows = slice(qi * tq, (qi + 1) * tq)
                past = slice(0, qi * tq)
                first = qi == nq - 1
                qv = q_ref[rows, lanes]
                dov = jnp.where(own, do_ref[rows, :], 0.0)
                dsum = jnp.sum(dov * o_ref[rows, :], axis=-1, keepdims=True)
                dob = dov.astype(BF16)
                lse_col = lse_ref[rows, col:col + 1]

                def grads(keys, mask):
                    kv, vv = k_ref[keys, lanes], v_ref[keys, :]
                    p = jnp.exp(_dot_nt(qv, kv) * MLA_SCALE - lse_col)
                    if mask is not None:
                        p = jnp.where(mask, p, 0.0)
                    ds = (p * (_dot_nt(dob, vv) - dsum) * MLA_SCALE).astype(BF16)
                    dk, dv = _dot_tn(ds, qv), _dot_tn(p.astype(BF16), dob)
                    if first:
                        dk_acc[keys, :] = dk
                    else:
                        dk_acc[keys, :] += dk
                    if first and half == 0:
                        dv_ref[keys, :] = dv
                    else:
                        dv_ref[keys, :] += dv
                    return _dot(ds, kv)

                dq = grads(rows, causal)
                if qi:
                    dq = dq + grads(past, None)
                dq_ref[rows, lanes] = dq
            dk_ref[:, lanes] = dk_acc[...]

    blk2 = pl.BlockSpec((S, 2 * HEAD_BLOCK), lambda b, j: (b, j))
    blk1 = pl.BlockSpec((S, 2 * V_HEAD), lambda b, j: (b, j))
    return _pcall(
        body, name="mla_bwd", grid=(B, pairs),
        in_specs=[blk2, blk2, blk1, blk1, blk1, blk1], out_specs=[blk2, blk2, blk1],
        out_shape=[jax.ShapeDtypeStruct((T, MLA_HEADS * HEAD_BLOCK), F32), jax.ShapeDtypeStruct((T, MLA_HEADS * HEAD_BLOCK), F32),
                   jax.ShapeDtypeStruct((T, pairs * 2 * V_HEAD), F32)],
        scratch_shapes=[pltpu.VMEM((S, HEAD_BLOCK), F32)],
        args=(q, k, v, do, o, lse), exchange=exchange)


SWA_GROUP = SWA_HEADS // SWA_KV_HEADS
SWA_ROWS = SWA_GROUP * WINDOW


SWA_CHUNK = 8


def _bdot(a, b, ca, cb):
    return lax.dot_general(a, b, (((ca,), (cb,)), ((0,), (0,))), preferred_element_type=F32)


def _swa_chunks(nb):
    cb = next(c for c in (SWA_CHUNK, 4, 2, 1) if nb % c == 0)
    return cb, [(n0, slice(n0 * WINDOW, (n0 + cb) * WINDOW)) for n0 in range(0, nb, cb)]


def _swa_keys(ref, n0, cb):
    W = WINDOW
    cur = ref[n0 * W:(n0 + cb) * W, :]
    if n0:
        prev = ref[(n0 - 1) * W:(n0 + cb - 1) * W, :]
    else:
        prev = jnp.concatenate([cur[:W], cur[:(cb - 1) * W]], axis=0) if cb > 1 else cur
    return jnp.concatenate([prev.reshape(cb, W, W), cur.reshape(cb, W, W)], axis=1)


def _swa_stack(ref, rows, cb):
    return jnp.concatenate([ref[rows, j * WINDOW:(j + 1) * WINDOW].reshape(cb, WINDOW, WINDOW) for j in range(SWA_GROUP)], axis=1)


def _swa_unstack(low, high, cb):
    sel = _half_mask((cb, WINDOW, WINDOW), 0)
    return jnp.concatenate([jnp.where(sel, low[:, j * WINDOW:(j + 1) * WINDOW], high[:, j * WINDOW:(j + 1) * WINDOW])
                            .reshape(cb * WINDOW, WINDOW) for j in range(SWA_GROUP)], axis=1)


def _swa_sink_col(sink_ref, g):
    return jnp.concatenate([jnp.broadcast_to(sink_ref[g * SWA_GROUP + j:g * SWA_GROUP + j + 1, 0:1], (WINDOW, 1))
                            for j in range(SWA_GROUP)], axis=0)


def _swa_probs(qm, kcat, bias_g, sink, first_chunk):
    shape = (qm.shape[0], SWA_ROWS, 2 * WINDOW)
    qi = lax.broadcasted_iota(jnp.int32, shape, 1) % WINDOW
    col = lax.broadcasted_iota(jnp.int32, shape, 2)
    valid = (col > qi) & (col <= qi + WINDOW)
    if first_chunk:
        valid = valid & ((col >= WINDOW) | (lax.broadcasted_iota(jnp.int32, shape, 0) > 0))
    s = jnp.where(valid, _bdot(qm, kcat, 2, 2) * SWA_SCALE + bias_g, NEG)
    m = jnp.maximum(jnp.max(s, axis=-1, keepdims=True), sink)
    e = jnp.exp(s - m)
    e_s = jnp.exp(sink - m)
    inv = 1.0 / (jnp.sum(e, axis=-1, keepdims=True) + e_s)
    return e * inv, e_s * inv


def _swa_fwd(q, k, v, bias, sinks, B, S):
    T = B * S
    W = WINDOW
    nb = S // W

    cb, chunks = _swa_chunks(nb)

    def body(q_ref, k_ref, v_ref, bias_ref, sink_ref, o_ref):
        sink_cols = [_swa_sink_col(sink_ref, g) for g in range(SWA_KV_HEADS)]
        for n0, rows in chunks:
            kcat, vcat = _swa_keys(k_ref, n0, cb), _swa_keys(v_ref, n0, cb)
            qcat = _swa_stack(q_ref, rows, cb)
            outs = []
            for g in range(SWA_KV_HEADS):
                qm = jnp.where(_half_mask(qcat.shape, g), qcat, jnp.zeros_like(qcat))
                p, _ = _swa_probs(qm, kcat, bias_ref[g], sink_cols[g], n0 == 0)
                outs.append(_bdot(p.astype(BF16), vcat, 2, 1))
            o_ref[rows, :] = _swa_unstack(outs[0], outs[1], cb)

    SQ, SK = SWA_HEADS * SWA_HEAD_DIM, SWA_KV_HEADS * SWA_HEAD_DIM
    return pl.pallas_call(
        body, name="swa_fwd", grid=(B,),
        in_specs=[pl.BlockSpec((S, SQ), lambda b: (b, 0)), pl.BlockSpec((S, SK), lambda b: (b, 0)),
                  pl.BlockSpec((S, SK), lambda b: (b, 0)), _const_spec((SWA_KV_HEADS, SWA_ROWS, 2 * W)),
                  _const_spec((SWA_HEADS, 128))],
        out_specs=pl.BlockSpec((S, SQ), lambda b: (b, 0)),
        out_shape=jax.ShapeDtypeStruct((T, SQ), F32),
        compiler_params=_cparams(("arbitrary",)),
    )(q, k, v, bias, sinks)


def _swa_bwd(q, k, v, do, bias, sinks, B, S):
    T = B * S
    W = WINDOW
    nb = S // W
    cb, chunks = _swa_chunks(nb)

    def body(q_ref, k_ref, v_ref, do_ref, bias_ref, sink_ref, dq_ref, dk_ref, dv_ref, dbias_ref, dsink_ref):
        @pl.when(pl.program_id(0) == 0)
        def _():
            dbias_ref[...] = jnp.zeros_like(dbias_ref)
            dsink_ref[...] = jnp.zeros_like(dsink_ref)

        dk_ref[...] = jnp.zeros_like(dk_ref)
        dv_ref[...] = jnp.zeros_like(dv_ref)
        sink_cols = [_swa_sink_col(sink_ref, g) for g in range(SWA_KV_HEADS)]
        for n0, rows in chunks:
            kcat, vcat = _swa_keys(k_ref, n0, cb), _swa_keys(v_ref, n0, cb)
            qcat = _swa_stack(q_ref, rows, cb)
            docat = _swa_stack(do_ref, rows, cb)
            dk = jnp.zeros((cb, 2 * W, W), F32)
            dv = jnp.zeros((cb, 2 * W, W), F32)
            dqs = []
            for g in range(SWA_KV_HEADS):
                own = _half_mask(qcat.shape, g)
                qm = jnp.where(own, qcat, jnp.zeros_like(qcat))
                dom = jnp.where(own, docat, 0.0).astype(BF16)
                p, p_s = _swa_probs(qm, kcat, bias_ref[g], sink_cols[g], n0 == 0)
                dp = _bdot(dom, vcat, 2, 2)
                dsum = jnp.sum(p * dp, axis=-1, keepdims=True)
                ds = p * (dp - dsum)
                dbias_ref[g] += jnp.sum(ds, axis=0)
                dsink_ref[g] += jnp.broadcast_to(-jnp.sum(p_s * dsum, axis=0), (SWA_ROWS, 128))
                dsb = (ds * SWA_SCALE).astype(BF16)
                dqs.append(_bdot(dsb, kcat, 2, 1))
                dk = dk + _bdot(dsb, qm, 1, 1)
                dv = dv + _bdot(p.astype(BF16), dom, 1, 1)
            dq_ref[rows, :] = _swa_unstack(dqs[0], dqs[1], cb)
            dk_ref[rows, :] += dk[:, W:].reshape(cb * W, W)
            dv_ref[rows, :] += dv[:, W:].reshape(cb * W, W)
            dk_prev, dv_prev = dk[:, :W].reshape(cb * W, W), dv[:, :W].reshape(cb * W, W)
            if n0:
                before = slice((n0 - 1) * W, (n0 + cb - 1) * W)
                dk_ref[before, :] += dk_prev
                dv_ref[before, :] += dv_prev
            elif cb > 1:
                before = slice(0, (cb - 1) * W)
                dk_ref[before, :] += dk_prev[W:]
                dv_ref[before, :] += dv_prev[W:]

    SQ, SK = SWA_HEADS * SWA_HEAD_DIM, SWA_KV_HEADS * SWA_HEAD_DIM
    row = lambda w: pl.BlockSpec((S, w), lambda b: (b, 0))
    return pl.pallas_call(
        body, name="swa_bwd", grid=(B,),
        in_specs=[row(SQ), row(SK), row(SK), row(SQ), _const_spec((SWA_KV_HEADS, SWA_ROWS, 2 * W)), _const_spec((SWA_HEADS, 128))],
        out_specs=[row(SQ), row(SK), row(SK), _acc_spec((SWA_KV_HEADS, SWA_ROWS, 2 * W)), _acc_spec((SWA_KV_HEADS, SWA_ROWS, 128))],
        out_shape=[jax.ShapeDtypeStruct((T, SQ), F32), jax.ShapeDtypeStruct((T, SK), F32), jax.ShapeDtypeStruct((T, SK), F32),
                   jax.ShapeDtypeStruct((SWA_KV_HEADS, SWA_ROWS, 2 * W), F32), jax.ShapeDtypeStruct((SWA_KV_HEADS, SWA_ROWS, 128), F32)],
        compiler_params=_cparams(("arbitrary",)),
    )(q, k, v, do, bias, sinks)


def _bias_expand(rel_bias, bucket):
    W = WINDOW

    def body(rb_ref, bucket_ref, out_ref):
        bk = bucket_ref[...]
        for h in range(SWA_HEADS):
            def add(b, acc):
                return jnp.where(bk == b, rb_ref[b, h], acc)

            out_ref[h] = lax.fori_loop(0, REL_BUCKETS, add, jnp.zeros((W, 2 * W), F32))

    return pl.pallas_call(
        body, name="bias_expand",
        in_specs=[pl.BlockSpec(memory_space=pltpu.SMEM), pl.BlockSpec(memory_space=pltpu.VMEM)],
        out_specs=pl.BlockSpec(memory_space=pltpu.VMEM),
        out_shape=jax.ShapeDtypeStruct((SWA_HEADS, W, 2 * W), F32),
        compiler_params=_cparams(),
    )(rel_bias, bucket)


def _bias_reduce(dbias, dsink_rows, bucket):
    def body(db_ref, ds_ref, bucket_ref, out_ref, sink_out_ref, rows_ref):
        sink_lane = lax.broadcasted_iota(jnp.int32, (1, 128), 1)
        sink_acc = jnp.zeros((1, 128), F32)
        for h in range(SWA_HEADS):
            sink_acc = jnp.where(sink_lane == h, jnp.sum(ds_ref[h], axis=0, keepdims=True), sink_acc)
        sink_out_ref[...] = sink_acc
        bk = bucket_ref[...]
        for h in range(SWA_HEADS):
            dbh = db_ref[h]

            def add(b, carry):
                rows_ref[pl.ds(h * REL_BUCKETS + b, 1), :] = jnp.sum(jnp.where(bk == b, dbh, 0.0), axis=0, keepdims=True)
                return carry

            lax.fori_loop(0, REL_BUCKETS, add, 0)
        totals = jnp.sum(rows_ref[...], axis=-1, keepdims=True)
        lane = lax.broadcasted_iota(jnp.int32, (REL_BUCKETS, 128), 1)
        acc = jnp.zeros((REL_BUCKETS, 128), F32)
        for h in range(SWA_HEADS):
            acc = jnp.where(lane == h, totals[h * REL_BUCKETS:(h + 1) * REL_BUCKETS], acc)
        out_ref[...] = acc

    return pl.pallas_call(
        body, name="bias_reduce",
        out_shape=[jax.ShapeDtypeStruct((REL_BUCKETS, 128), F32), jax.ShapeDtypeStruct((1, 128), F32)],
        scratch_shapes=[pltpu.VMEM((SWA_HEADS * REL_BUCKETS, 2 * WINDOW), F32)],
        compiler_params=_cparams(),
    )(dbias, dsink_rows, bucket)


def _t5_bucket(dist):
    n = jnp.maximum(dist, 0)
    max_exact = REL_BUCKETS // 2
    nf = jnp.maximum(n, 1).astype(F32)
    large = max_exact + (jnp.log(nf / max_exact) / math.log(REL_MAX_DIST / max_exact) * (REL_BUCKETS - max_exact)).astype(jnp.int32)
    large = jnp.minimum(large, REL_BUCKETS - 1)
    return jnp.where(n < max_exact, n, large)


def _pair_heads(a, axis, inverse=False):
    shp = a.shape
    a = a.reshape(shp[:axis] + ((4, 2, 64) if inverse else (2, 4, 64)) + shp[axis + 1:])
    a = jnp.swapaxes(a, axis, axis + 1)
    return a.reshape(shp)


def _owner_blocks(w32, w16, me):
    r, c = w32.shape[0] // N_DEV, w32.shape[1]
    return lax.dynamic_slice(w32, (me * r, 0), (r, c)), w16.reshape(N_DEV, r, c)


_IN_WIDTHS = (Q_LORA, KV_LORA, QK_ROPE, SWA_HEADS * SWA_HEAD_DIM, SWA_KV_HEADS * SWA_HEAD_DIM, SWA_KV_HEADS * SWA_HEAD_DIM)


def _win_to_kernel(wt):
    parts, o = [], 0
    for wd in _IN_WIDTHS:
        parts.append(wt[o:o + wd])
        o += wd
    cq, ckv, kpe, qs, ks, vs = parts
    return jnp.concatenate([cq, ckv, _pair_heads(qs, 0), ks, vs, jnp.pad(kpe, ((64, 32), (0, 0)))], axis=0)


def _win_from_kernel(dwt):
    o = [0, Q_LORA, Q_LORA + KV_LORA, Q_LORA + KV_LORA + 512, Q_LORA + KV_LORA + 640, Q_LORA + KV_LORA + 768]
    cq, ckv, qs, ks, vs, kpe = [dwt[a:b] for a, b in zip(o, o[1:] + [dwt.shape[0]])]
    return jnp.concatenate([cq, ckv, kpe[64:96], _pair_heads(qs, 0, inverse=True), ks, vs], axis=0)


def _wq_to_kernel(wt):
    return jnp.pad(wt.reshape(MLA_HEADS, QK_NOPE + QK_ROPE, Q_LORA), ((0, 0), (0, 32), (0, 0))).reshape(MLA_HEADS * HEAD_BLOCK, Q_LORA)


def _wq_from_kernel(dwt):
    return dwt.reshape(MLA_HEADS, HEAD_BLOCK, Q_LORA)[:, :QK_NOPE + QK_ROPE].reshape(-1, Q_LORA)


def _wkv_to_kernel(wt):
    w3 = wt.reshape(MLA_HEADS, QK_NOPE + V_HEAD, KV_LORA)
    kpart = jnp.pad(w3[:, :QK_NOPE], ((0, 0), (0, 64), (0, 0))).reshape(MLA_HEADS * HEAD_BLOCK, KV_LORA)
    return jnp.concatenate([kpart, w3[:, QK_NOPE:].reshape(MLA_HEADS * V_HEAD, KV_LORA)], axis=0)


def _wkv_from_kernel(dwt):
    dk = dwt[:MLA_HEADS * HEAD_BLOCK].reshape(MLA_HEADS, HEAD_BLOCK, KV_LORA)[:, :QK_NOPE]
    dv = dwt[MLA_HEADS * HEAD_BLOCK:].reshape(MLA_HEADS, V_HEAD, KV_LORA)
    return jnp.concatenate([dk, dv], axis=1).reshape(-1, KV_LORA)


def _wo_to_kernel(w):
    return jnp.concatenate([w[:512], _pair_heads(w[512:], 0)], axis=0)


def _wo_from_kernel(dw):
    return jnp.concatenate([dw[:512], _pair_heads(dw[512:], 0, inverse=True)], axis=0)


def _kernel_weights(key, stored):
    if key in ("g1", "u1", "g2", "u2"):
        return {key: stored.T, key + "t": stored}
    if key in ("d1", "d2"):
        return {key: stored, key + "t": stored.T}
    if key == "o":
        return {key: _wo_to_kernel(stored)}
    kt = _TO_KERNEL[key](stored)
    return {key: kt.T, key + "t": kt}


_SMALL = ("g_ffn1", "g_mix", "g_q_a", "g_kv_a", "attn_sinks", "rel_bias", "g_out_mla", "g_out_swa", "g_ffn2", "g_final")


class _Comm:
    def __init__(self, shards, me):
        self.shards, self.me = shards, me
        self.own32, self.recv16 = {}, {}

    def gather(self, keys):
        return [("gather", self.shards[k]) for k in keys]

    def gathered(self, keys, results, wts):
        for k, g in zip(keys, results):
            wts.update(_kernel_weights(k, g.reshape(N_DEV * g.shape[1], g.shape[2])))

    def scatter(self, keys, grads):
        jobs = []
        for k in keys:
            g32, g16 = (_FROM_KERNEL[k](g) if k in _FROM_KERNEL else g for g in grads[k])
            self.own32[k], blocks = _owner_blocks(g32, g16, self.me)
            jobs.append(("scatter", blocks))
        return jobs

    def scattered(self, keys, results):
        self.recv16.update(zip(keys, results))


def _step(x, target, gains, wts, comm):
    def jobs_in(keys):
        return comm.gather(keys) if comm else None

    def jobs_out(keys):
        return comm.scatter(keys, grads) if comm else None

    B, S, D = x.shape
    T = B * S
    x2 = x.reshape(T, D)
    t2 = target.reshape(T, D)

    pos = jnp.arange(S, dtype=F32)
    inv_freq = ROPE_THETA ** (-jnp.arange(0, QK_ROPE, 2, dtype=F32) / QK_ROPE)
    ang = pos[:, None] * inv_freq[None, :]
    cos, sin = jnp.cos(ang), jnp.sin(ang)
    ones, zeros = jnp.ones((S, 64), F32), jnp.zeros((S, 32), F32)
    rope_c = jnp.concatenate([ones, cos, cos, zeros], axis=1)
    rope_s = jnp.concatenate([0.0 * ones, -sin, sin, zeros], axis=1)

    qi = jnp.arange(WINDOW)[:, None]
    kj = jnp.arange(2 * WINDOW)[None, :]
    bucket = _t5_bucket(qi + WINDOW - kj).astype(jnp.int32)
    bias = _bias_expand(gains["rel_bias"], bucket).reshape(SWA_KV_HEADS, SWA_ROWS, 2 * WINDOW)
    sinks = jnp.broadcast_to(gains["attn_sinks"].reshape(SWA_HEADS, 1), (SWA_HEADS, 128))
    g_swa = _pair_heads(gains["g_out_swa"], 1)

    keys = ("in", "q", "kv", "o", "g2")
    (h1, n1, gate1, up1, act1), got = _ffn_fwd(x2, gains["g_ffn1"], wts["g1"], wts["u1"], wts["d1"], "ffn1_fwd", jobs_in(keys))
    if comm:
        comm.gathered(keys, got, wts)
    (u, c_q, c_kv, cqn, ckvn, q, k, v, qs, ks, vs) = _proj_fwd(
        h1, gains["g_mix"], wts["in"], gains["g_q_a"], wts["q"], gains["g_kv_a"], wts["kv"], rope_c, rope_s, S)
    keys = ("u2", "d2")
    (o_mla, lse), got = _mla_fwd(q, k, v, B, S, jobs_in(keys))
    if comm:
        comm.gathered(keys, got, wts)
    o_swa = _swa_fwd(qs, ks, vs, bias, sinks, B, S)
    h2, oc = _out_fwd(o_mla, o_swa, gains["g_out_mla"], g_swa, wts["o"], h1)
    (dh3, n2, gate2, up2, act2, loss, dg_final), _ = _ffn_fwd(
        h2, gains["g_ffn2"], wts["g2"], wts["u2"], wts["d2"], "ffn2_fwd", loss_head=(t2, gains["g_final"].reshape(1, D)))

    grads = {}
    (dh2, dg_ffn2, dgate2, dup2), _ = _ffn_bwd(h2, gains["g_ffn2"], gate2, up2, dh3, wts["g2t"], wts["u2t"], wts["d2t"], "ffn2_bwd")
    grads["g2"], _ = _matmul_tn(dgate2, n2, "dw_gate2")
    grads["u2"], _ = _matmul_tn(dup2, n2, "dw_up2")
    grads["d2"], _ = _matmul_tn(act2, dh3, "dw_down2", 0.5)
    do_mla, do_swa, dg_mla, dg_swa = _out_bwd(dh2, o_mla, o_swa, gains["g_out_mla"], g_swa, wts["o"])
    grads["o"], _ = _matmul_tn(oc, dh2, "dw_o")
    keys = ("g2", "u2", "d2")
    (dq, dk, dv), got = _mla_bwd(q, k, v, do_mla, o_mla, lse, B, S, jobs_out(keys))
    if comm:
        comm.scattered(keys, got)
    dqs, dks, dvs, dbias, dsink = _swa_bwd(qs, ks, vs, do_swa, bias, sinks, B, S)
    dh1, dproj, dql, dkvc, dg_mix, dg_q, dg_kv = _proj_bwd(
        dq, dk, dv, dqs, dks, dvs, c_q, c_kv, h1, dh2, gains["g_mix"], wts["int"], gains["g_q_a"], wts["qt"],
        gains["g_kv_a"], wts["kvt"], rope_c, rope_s, S)
    grads["in"], _ = _matmul_tn(dproj, u, "dw_in")
    grads["q"], _ = _matmul_tn(dql, cqn, "dw_q")
    grads["kv"], _ = _matmul_tn(dkvc, ckvn, "dw_kv")
    grads["d1"], _ = _matmul_tn(act1, dh1, "dw_down1", 0.5)
    keys = ("o", "in", "q", "kv", "d1")
    (dx, dg_ffn1, dgate1, dup1), got = _ffn_bwd(
        x2, gains["g_ffn1"], gate1, up1, dh1, wts["g1t"], wts["u1t"], wts["d1t"], "ffn1_bwd", jobs_out(keys))
    if comm:
        comm.scattered(keys, got)
    grads["g1"], _ = _matmul_tn(dgate1, n1, "dw_gate1")
    grads["u1"], got = _matmul_tn(dup1, n1, "dw_up1", exchange=jobs_out(("g1",)))
    if comm:
        comm.scattered(("g1",), got)
        comm.scattered(("u1",), _exchange(jobs_out(("u1",)), "grad_exchange_last"))

    d_rel_bias, d_sinks = _bias_reduce(dbias.reshape(SWA_HEADS, WINDOW, 2 * WINDOW), dsink.reshape(SWA_HEADS, WINDOW, 128), bucket)
    small = {
        "g_ffn1": dg_ffn1, "g_mix": dg_mix, "g_q_a": dg_q, "g_kv_a": dg_kv, "attn_sinks": d_sinks,
        "rel_bias": d_rel_bias, "g_out_mla": dg_mla, "g_out_swa": _pair_heads(dg_swa, 1, inverse=True),
        "g_ffn2": dg_ffn2, "g_final": dg_final, "loss": loss,
    }
    return dx.reshape(B, S, D), grads, small


_WEIGHTS = ("g_ffn1", "w_ffn1_gate", "w_ffn1_up", "w_ffn1_down", "g_mix", "w_in", "g_q_a", "w_q_b", "g_kv_a", "w_kv_b",
            "attn_sinks", "rel_bias", "g_out_mla", "g_out_swa", "w_o", "g_ffn2", "w_ffn2_gate", "w_ffn2_up", "w_ffn2_down",
            "g_final")
_BIG = (("w_ffn1_gate", "g1", True), ("w_ffn1_up", "u1", True), ("w_ffn1_down", "d1", False), ("w_in", "in", True),
        ("w_q_b", "q", True), ("w_kv_b", "kv", True), ("w_o", "o", False), ("w_ffn2_gate", "g2", True),
        ("w_ffn2_up", "u2", True), ("w_ffn2_down", "d2", False))
_TO_KERNEL = {"in": _win_to_kernel, "q": _wq_to_kernel, "kv": _wkv_to_kernel}
_FROM_KERNEL = {"in": _win_from_kernel, "q": _wq_from_kernel, "kv": _wkv_from_kernel, "o": _wo_from_kernel}


def kernel(x, g_ffn1, w_ffn1_gate, w_ffn1_up, w_ffn1_down, g_mix, w_in, g_q_a, w_q_b, g_kv_a, w_kv_b, attn_sinks, rel_bias, g_out_mla, g_out_swa, w_o, g_ffn2, w_ffn2_gate, w_ffn2_up, w_ffn2_down, g_final, loss_target, m_g_ffn1, m_w_ffn1_gate, m_w_ffn1_up, m_w_ffn1_down, m_g_mix, m_w_in, m_g_q_a, m_w_q_b, m_g_kv_a, m_w_kv_b, m_attn_sinks, m_rel_bias, m_g_out_mla, m_g_out_swa, m_w_o, m_g_ffn2, m_w_ffn2_gate, m_w_ffn2_up, m_w_ffn2_down, m_g_final, v_g_ffn1, v_w_ffn1_gate, v_w_ffn1_up, v_w_ffn1_down, v_g_mix, v_w_in, v_g_q_a, v_w_q_b, v_g_kv_a, v_w_kv_b, v_attn_sinks, v_rel_bias, v_g_out_mla, v_g_out_swa, v_w_o, v_g_ffn2, v_w_ffn2_gate, v_w_ffn2_up, v_w_ffn2_down, v_g_final):
    w = dict(zip(_WEIGHTS, (g_ffn1, w_ffn1_gate, w_ffn1_up, w_ffn1_down, g_mix, w_in, g_q_a, w_q_b, g_kv_a, w_kv_b, attn_sinks,
                            rel_bias, g_out_mla, g_out_swa, w_o, g_ffn2, w_ffn2_gate, w_ffn2_up, w_ffn2_down, g_final)))
    m = dict(zip(_WEIGHTS, (m_g_ffn1, m_w_ffn1_gate, m_w_ffn1_up, m_w_ffn1_down, m_g_mix, m_w_in, m_g_q_a, m_w_q_b, m_g_kv_a,
                            m_w_kv_b, m_attn_sinks, m_rel_bias, m_g_out_mla, m_g_out_swa, m_w_o, m_g_ffn2, m_w_ffn2_gate,
                            m_w_ffn2_up, m_w_ffn2_down, m_g_final)))
    v = dict(zip(_WEIGHTS, (v_g_ffn1, v_w_ffn1_gate, v_w_ffn1_up, v_w_ffn1_down, v_g_mix, v_w_in, v_g_q_a, v_w_q_b, v_g_kv_a,
                            v_w_kv_b, v_attn_sinks, v_rel_bias, v_g_out_mla, v_g_out_swa, v_w_o, v_g_ffn2, v_w_ffn2_gate,
                            v_w_ffn2_up, v_w_ffn2_down, v_g_final)))
    me = 4 * lax.axis_index("x") + 2 * lax.axis_index("y") + lax.axis_index("c")

    stored = lambda src, name, by_col: src[name][0].T if by_col else src[name][0]
    shards = dict(zip([key for _, key, _ in _BIG], _cast_bf16([stored(w, name, by_col) for name, _, by_col in _BIG])))
    comm = _Comm(shards, me)
    first = ("g1", "u1", "d1")
    wts = {}
    comm.gathered(first, _all_gather([shards[k] for k in first], "weights_all_gather"), wts)

    gains = {n: (w[n] if n == "rel_bias" else w[n].reshape(1, -1)) for n in _SMALL}
    grad_x, _, small = _step(x, loss_target, gains, wts, comm)

    small_gathered = _all_gather([small[n] for n in _SMALL] + [small["loss"]], "small_all_gather")

    out_g, out_d, out_m, out_v = {}, {}, {}, {}
    me1 = me.astype(jnp.int32).reshape(1)
    for name, key, by_col in _BIG:
        updates = _adam_big(stored(w, name, by_col), stored(m, name, by_col), stored(v, name, by_col),
                            comm.own32[key], comm.recv16[key], me1)
        out_g[name], out_d[name], out_m[name], out_v[name] = ((a.T if by_col else a)[None] for a in updates)
    as_2d = lambda src: [src[n] if src[n].ndim == 2 else src[n].reshape(1, -1) for n in _SMALL]
    loss, updates = _adam_small(as_2d(w), as_2d(m), as_2d(v), small_gathered[:-1], small_gathered[-1])
    for n, (g, d, mn, vn) in zip(_SMALL, updates):
        out_g[n], out_d[n], out_m[n], out_v[n] = (a.reshape(w[n].shape) for a in (g, d, mn, vn))

    return (loss[0, 0], grad_x, *[out_g[n] for n in _WEIGHTS], *[out_d[n] for n in _WEIGHTS],
            *[out_m[n] for n in _WEIGHTS], *[out_v[n] for n in _WEIGHTS])
```

```python
import functools
import math

import jax
import jax.numpy as jnp
from jax import lax
from jax.experimental import pallas as pl
from jax.experimental.pallas import tpu as pltpu

F32 = jnp.float32
BF16 = jnp.bfloat16
MESH = pl.DeviceIdType.MESH

EPS = 1e-6
NEG = -1e30
N_DEV = 8

MLA_HEADS = 8
Q_LORA = 256
KV_LORA = 128
QK_NOPE = 64
QK_ROPE = 32
V_HEAD = 64
ROPE_THETA = 10000.0
SWA_HEADS = 8
SWA_KV_HEADS = 2
SWA_HEAD_DIM = 64
WINDOW = 128
REL_BUCKETS = 32
REL_MAX_DIST = 128
HEAD_BLOCK = 128
MLA_Q_TILES = (512, 256, 128)
MLA_SCALE = (QK_NOPE + QK_ROPE) ** -0.5
SWA_SCALE = SWA_HEAD_DIM ** -0.5

ADAM_LR = 0.001
ADAM_B1 = 0.9
ADAM_B2 = 0.999
ADAM_EPS = 1e-08
ADAM_WD = 0.01
ADAM_STEP = 10

VMEM_LIMIT = 56 * 1024 * 1024


def _cparams(semantics=None):
    return pltpu.CompilerParams(dimension_semantics=semantics, vmem_limit_bytes=VMEM_LIMIT)


def _dot(a, b):
    return jnp.dot(a, b, preferred_element_type=F32)


def _dot_nt(a, b):
    return lax.dot_general(a, b, (((1,), (1,)), ((), ())), preferred_element_type=F32)


def _dot_tn(a, b):
    return lax.dot_general(a, b, (((0,), (0,)), ((), ())), preferred_element_type=F32)


def _rms_fwd(x, g):
    r = lax.rsqrt(jnp.mean(x * x, axis=-1, keepdims=True) + EPS)
    return x * r * g, r


def _rms_bwd(x, g, r, dy):
    xh = x * r
    dyg = dy * g
    dx = r * (dyg - xh * jnp.mean(dyg * xh, axis=-1, keepdims=True))
    return dx, jnp.sum(dy * xh, axis=0, keepdims=True)


def _swap16(x):
    n = x.shape[-1]
    up = pltpu.roll(x, n - 16, 1)
    down = pltpu.roll(x, 16, 1)
    lane = lax.broadcasted_iota(jnp.int32, x.shape, 1) % HEAD_BLOCK
    return jnp.where((lane >= 64) & (lane < 80), up, jnp.where((lane >= 80) & (lane < 96), down, 0.0))


def _const_spec(shape):
    return pl.BlockSpec(shape, lambda *_: (0,) * len(shape), pipeline_mode=pl.Buffered(1))


def _acc_spec(shape):
    return pl.BlockSpec(shape, lambda *_: (0,) * len(shape))


def _row_spec(tm, width):
    return pl.BlockSpec((tm, width), lambda i: (i, 0))


def _cast_bf16(arrays):
    n = len(arrays)

    def body(*refs):
        for src, dst in zip(refs[:n], refs[n:]):
            dst[...] = src[...].astype(BF16)

    return pl.pallas_call(
        body, name="cast_weights",
        out_shape=[jax.ShapeDtypeStruct(a.shape, BF16) for a in arrays],
        compiler_params=_cparams(),
    )(*arrays)


def _all_gather(arrays, name):
    n = len(arrays)

    def body(*refs):
        ins, outs = refs[:n], refs[n:2 * n]
        send_sems, recv_sems, local_sems = refs[2 * n:]
        x, y, c = lax.axis_index("x"), lax.axis_index("y"), lax.axis_index("c")
        me, sibling = (x, y, c), (x, y, 1 - c)
        chips = [(1 - x, y), (x, 1 - y), (1 - x, 1 - y)]

        def slot(a, dev):
            return outs[a].at[4 * dev[0] + 2 * dev[1] + dev[2]]

        def copy(a, k, block, to, src=None):
            return pltpu.make_async_remote_copy(
                src_ref=slot(a, block) if src is None else src, dst_ref=slot(a, block),
                send_sem=send_sems.at[a, k], recv_sem=recv_sems.at[a, k], device_id=to, device_id_type=MESH)

        mine = [pltpu.make_async_copy(ins[a], slot(a, me), local_sems.at[a]) for a in range(n)]
        for cp in mine:
            cp.start()
        first = []
        for a in range(n):
            first.append(copy(a, 0, me, sibling, src=ins[a]))
            first += [copy(a, 1 + j, me, (*chip, c), src=ins[a]) for j, chip in enumerate(chips)]
        for cp in first:
            cp.start()
        passed = []
        for j, chip in enumerate(chips):
            for a in range(n):
                copy(a, 1 + j, (*chip, c), me).wait_recv()
                cp = copy(a, 4 + j, (*chip, c), sibling)
                cp.start()
                passed.append(cp)
        for a in range(n):
            copy(a, 0, sibling, me).wait_recv()
            for j, chip in enumerate(chips):
                copy(a, 4 + j, (*chip, 1 - c), me).wait_recv()
        for cp in first + passed:
            cp.wait_send()
        for cp in mine:
            cp.wait()

    any_spec = pl.BlockSpec(memory_space=pl.ANY)
    return pl.pallas_call(
        body, name=name,
        out_shape=[jax.ShapeDtypeStruct((N_DEV,) + a.shape, a.dtype) for a in arrays],
        in_specs=[any_spec] * n, out_specs=[any_spec] * n,
        scratch_shapes=[pltpu.SemaphoreType.DMA((n, 7)), pltpu.SemaphoreType.DMA((n, 7)), pltpu.SemaphoreType.DMA((n,))],
    )(*arrays)


def _exchange_shapes(jobs):
    dsts = [jax.ShapeDtypeStruct(((N_DEV,) + a.shape) if kind == "gather" else a.shape, a.dtype) for kind, a in jobs]
    n = len(jobs)
    sems = [pltpu.SemaphoreType.DMA((n, N_DEV - 1)), pltpu.SemaphoreType.DMA((n, N_DEV - 1)), pltpu.SemaphoreType.DMA((n,))]
    return dsts, sems


def _exchange_copies(kinds, srcs, dsts, send_sems, recv_sems, local_sems):
    x, y, c = lax.axis_index("x"), lax.axis_index("y"), lax.axis_index("c")
    me = 4 * x + 2 * y + c
    remote, local = [], []
    for a, kind in enumerate(kinds):
        for k in range(1, N_DEV):
            peer = (1 - x if k & 4 else x, 1 - y if k & 2 else y, 1 - c if k & 1 else c)
            src = srcs[a] if kind == "gather" else srcs[a].at[4 * peer[0] + 2 * peer[1] + peer[2]]
            remote.append(pltpu.make_async_remote_copy(
                src_ref=src, dst_ref=dsts[a].at[me], send_sem=send_sems.at[a, k - 1], recv_sem=recv_sems.at[a, k - 1],
                device_id=peer, device_id_type=MESH))
        src = srcs[a] if kind == "gather" else srcs[a].at[me]
        local.append(pltpu.make_async_copy(src, dsts[a].at[me], local_sems.at[a]))
    return remote, local


def _exchange(jobs, name):
    n = len(jobs)
    kinds = [kind for kind, _ in jobs]
    dst_shapes, sems = _exchange_shapes(jobs)

    def body(*refs):
        remote, local = _exchange_copies(kinds, refs[:n], refs[n:2 * n], *refs[2 * n:])
        for cp in local + remote:
            cp.start()
        for cp in remote + local:
            cp.wait()

    any_spec = pl.BlockSpec(memory_space=pl.ANY)
    return pl.pallas_call(
        body, name=name, out_shape=dst_shapes, in_specs=[any_spec] * n, out_specs=[any_spec] * n, scratch_shapes=sems,
    )(*[a for _, a in jobs])


def _pcall(body, *, name, grid, in_specs, out_specs, out_shape, args, scratch_shapes=(), exchange=None):
    semantics = ("arbitrary",) * len(grid)
    if not exchange:
        outs = pl.pallas_call(body, name=name, grid=grid, in_specs=list(in_specs), out_specs=list(out_specs),
                              out_shape=list(out_shape), scratch_shapes=list(scratch_shapes),
                              compiler_params=_cparams(semantics))(*args)
        return list(outs), []
    kinds = [kind for kind, _ in exchange]
    n_in, n_out, n_job = len(in_specs), len(out_specs), len(exchange)
    dst_shapes, sems = _exchange_shapes(exchange)

    def with_exchange(*refs):
        ins, srcs = refs[:n_in], refs[n_in:n_in + n_job]
        outs, dsts = refs[n_in + n_job:n_in + n_job + n_out], refs[n_in + n_job + n_out:n_in + 2 * n_job + n_out]
        scratch, sem_refs = refs[n_in + 2 * n_job + n_out:-3], refs[-3:]
        first = functools.reduce(jnp.logical_and, [pl.program_id(d) == 0 for d in range(len(grid))])
        last = functools.reduce(jnp.logical_and, [pl.program_id(d) == grid[d] - 1 for d in range(len(grid))])

        @pl.when(first)
        def _():
            remote, local = _exchange_copies(kinds, srcs, dsts, *sem_refs)
            for cp in local + remote:
                cp.start()

        body(*ins, *outs, *scratch)

        @pl.when(last)
        def _():
            remote, local = _exchange_copies(kinds, srcs, dsts, *sem_refs)
            for cp in remote + local:
                cp.wait()

    any_spec = pl.BlockSpec(memory_space=pl.ANY)
    outs = pl.pallas_call(
        with_exchange, name=name, grid=grid, in_specs=list(in_specs) + [any_spec] * n_job,
        out_specs=list(out_specs) + [any_spec] * n_job, out_shape=list(out_shape) + dst_shapes,
        scratch_shapes=list(scratch_shapes) + sems, compiler_params=_cparams(semantics),
    )(*args, *[a for _, a in exchange])
    return list(outs[:n_out]), list(outs[n_out:])


def _adam_math(w, g, m, v):
    m = ADAM_B1 * m + (1.0 - ADAM_B1) * g
    v = ADAM_B2 * v + (1.0 - ADAM_B2) * (g * g)
    m_hat = m / (1.0 - ADAM_B1 ** ADAM_STEP)
    v_hat = v / (1.0 - ADAM_B2 ** ADAM_STEP)
    delta = -ADAM_LR * (m_hat / (jnp.sqrt(v_hat) + ADAM_EPS) + ADAM_WD * w)
    return delta, m, v


def _adam_big(w, m, v, own32, recv16, me):
    r, c = w.shape
    tr = r if r <= 512 else (512 if r % 512 == 0 else 352)
    assert r % tr == 0

    def body(me_ref, w_ref, m_ref, v_ref, s_ref, r_ref, g_out, d_out, m_out, v_out):
        g = s_ref[...]
        for k in range(1, N_DEV):
            g = g + r_ref[jnp.bitwise_xor(me_ref[0], k)].astype(F32)
        d, mn, vn = _adam_math(w_ref[...], g, m_ref[...], v_ref[...])
        g_out[...] = g
        d_out[...] = d
        m_out[...] = mn
        v_out[...] = vn

    blk = pl.BlockSpec((tr, c), lambda i, s: (i, 0))
    grid_spec = pltpu.PrefetchScalarGridSpec(
        num_scalar_prefetch=1, grid=(r // tr,),
        in_specs=[blk, blk, blk, blk, pl.BlockSpec((N_DEV, tr, c), lambda i, s: (0, i, 0))],
        out_specs=[blk, blk, blk, blk])
    return pl.pallas_call(
        body, name="adamw_big", grid_spec=grid_spec,
        out_shape=[jax.ShapeDtypeStruct((r, c), F32)] * 4,
        compiler_params=_cparams(("arbitrary",)),
    )(me, w, m, v, own32, recv16)


def _adam_small(ws, ms, vs, partials, loss_parts):
    n = len(ws)

    def body(*refs):
        w_refs, m_refs, v_refs, p_refs = (refs[i * n:(i + 1) * n] for i in range(4))
        l_ref, loss_out = refs[4 * n], refs[4 * n + 1]
        outs = refs[4 * n + 2:]

        def total(ref):
            s = ref[0]
            for dev in range(1, N_DEV):
                s = s + ref[dev]
            return s

        loss_out[...] = total(l_ref)
        for i in range(n):
            r, c = w_refs[i].shape
            g = total(p_refs[i])[:, :c]
            d, mn, vn = _adam_math(w_refs[i][...], g, m_refs[i][...], v_refs[i][...])
            for j, val in enumerate((g, d, mn, vn)):
                outs[4 * i + j][...] = val

    shapes = [jax.ShapeDtypeStruct((1, 128), F32)]
    for w in ws:
        shapes += [jax.ShapeDtypeStruct(w.shape, F32)] * 4
    outs = pl.pallas_call(body, name="adamw_small", out_shape=shapes, compiler_params=_cparams())(*ws, *ms, *vs, *partials, loss_parts)
    return outs[0], [outs[1 + 4 * i:5 + 4 * i] for i in range(n)]


def _pick(n, prefer):
    for t in prefer:
        if n % t == 0:
            return t
    return n


def _ffn_fwd(x, g, wg, wu, wd, name, exchange=None, loss_head=None):
    T, D = x.shape
    F = wg.shape[0]
    tm = _pick(T, (512, 256, 128))
    fc = _pick(F, (256, 128))

    def body(x_ref, g_ref, wg_ref, wu_ref, wd_ref, *rest):
        h_ref, n_ref, gate_ref, up_ref, act_ref = rest[-7:-2] if loss_head else rest
        xv = x_ref[...]
        n, _ = _rms_fwd(xv, g_ref[...])
        nb = n.astype(BF16)
        n_ref[...] = nb
        acc = jnp.zeros((tm, D), F32)
        for f0 in range(0, F, fc):
            gate = _dot_nt(nb, wg_ref[f0:f0 + fc, :])
            up = _dot_nt(nb, wu_ref[f0:f0 + fc, :])
            gate_ref[:, f0:f0 + fc] = gate.astype(BF16)
            up_ref[:, f0:f0 + fc] = up.astype(BF16)
            act = (gate * (1.0 / (1.0 + jnp.exp(-gate))) * up).astype(BF16)
            act_ref[:, f0:f0 + fc] = act
            acc = acc + _dot(act, wd_ref[f0:f0 + fc, :])
        hv = xv + 0.5 * acc
        if not loss_head:
            h_ref[...] = hv
            return
        t_ref, gf_ref, loss_ref, dgf_ref = rest[0], rest[1], rest[-2], rest[-1]
        gf = gf_ref[...]
        y, r = _rms_fwd(hv, gf)
        diff = y - t_ref[...]
        dh, dgp = _rms_bwd(hv, gf, r, diff * (1.0 / D))
        h_ref[...] = dh
        part = 0.5 * jnp.sum(jnp.sum(diff * diff, axis=-1, keepdims=True) * (1.0 / D), axis=0, keepdims=True)

        @pl.when(pl.program_id(0) == 0)
        def _():
            loss_ref[...] = jnp.zeros_like(loss_ref)
            dgf_ref[...] = jnp.zeros_like(dgf_ref)

        loss_ref[...] += jnp.broadcast_to(part, loss_ref.shape)
        dgf_ref[...] += dgp

    in_specs = [_row_spec(tm, D), _const_spec((1, D)), _const_spec((F, D)), _const_spec((F, D)), _const_spec((F, D))]
    out_specs = [_row_spec(tm, D), _row_spec(tm, D), _row_spec(tm, F), _row_spec(tm, F), _row_spec(tm, F)]
    out_shape = [jax.ShapeDtypeStruct((T, D), F32), jax.ShapeDtypeStruct((T, D), BF16),
                 jax.ShapeDtypeStruct((T, F), BF16), jax.ShapeDtypeStruct((T, F), BF16), jax.ShapeDtypeStruct((T, F), BF16)]
    args = (x, g, wg, wu, wd)
    if loss_head:
        in_specs += [_row_spec(tm, D), _const_spec((1, D))]
        out_specs += [_acc_spec((1, 128)), _acc_spec((1, D))]
        out_shape += [jax.ShapeDtypeStruct((1, 128), F32), jax.ShapeDtypeStruct((1, D), F32)]
        args += tuple(loss_head)
    return _pcall(body, name=name, grid=(T // tm,), in_specs=in_specs, out_specs=out_specs, out_shape=out_shape, args=args,
                  exchange=exchange)


FFN_BWD_CHUNK = 1408
FFN_BWD_GROUP = 1408


def _ffn_bwd(x, g, gate, up, dh, wgt, wut, wdt, name, exchange=None):
    T, D = x.shape
    F = wgt.shape[0]
    tm = _pick(T, (512, 256, 128))
    halves = 2
    Fh = F // halves
    assert Fh * halves == F and Fh % 128 == 0
    groups = [(g0, min(g0 + FFN_BWD_GROUP, Fh)) for g0 in range(0, Fh, FFN_BWD_GROUP)]

    def body(x_ref, g_ref, gate_ref, up_ref, dh_ref, wg_ref, wu_ref, wd_ref, dx_ref, dg_ref, dgate_ref, dup_ref, dn_ref):
        i, h = pl.program_id(0), pl.program_id(1)
        dhv = dh_ref[...]
        dho = (0.5 * dhv).astype(BF16)
        dn = jnp.zeros((tm, D), F32)
        for g0, g1 in groups:
            for f0 in range(g0, g1, FFN_BWD_CHUNK):
                f1 = min(f0 + FFN_BWD_CHUNK, g1)
                d_act = _dot_nt(dho, wd_ref[f0:f1, :])
                gt = gate_ref[:, f0:f1].astype(F32)
                u = up_ref[:, f0:f1].astype(F32)
                sg = 1.0 / (1.0 + jnp.exp(-gt))
                silu = gt * sg
                dup_ref[:, f0:f1] = (d_act * silu).astype(BF16)
                dgate_ref[:, f0:f1] = (d_act * u * (sg + silu * (1.0 - sg))).astype(BF16)
            dn = dn + _dot(dgate_ref[:, g0:g1], wg_ref[g0:g1, :]) + _dot(dup_ref[:, g0:g1], wu_ref[g0:g1, :])

        @pl.when(h == 0)
        def _():
            dn_ref[...] = dn

        @pl.when(h > 0)
        def _():
            dn_ref[...] += dn

        @pl.when((i == 0) & (h == 0))
        def _():
            dg_ref[...] = jnp.zeros_like(dg_ref)

        @pl.when(h == halves - 1)
        def _():
            xv, gv = x_ref[...], g_ref[...]
            _, r = _rms_fwd(xv, gv)
            dxn, dgp = _rms_bwd(xv, gv, r, dn_ref[...])
            dx_ref[...] = dhv + dxn
            dg_ref[...] += dgp

    tile = pl.BlockSpec((tm, D), lambda i, h: (i, 0))
    wide = pl.BlockSpec((tm, Fh), lambda i, h: (i, h))
    return _pcall(
        body, name=name, grid=(T // tm, halves),
        in_specs=[tile, _const_spec((1, D)), wide, wide, tile,
                  pl.BlockSpec((Fh, D), lambda i, h: (h, 0)), pl.BlockSpec((Fh, D), lambda i, h: (h, 0)),
                  pl.BlockSpec((Fh, D), lambda i, h: (h, 0))],
        out_specs=[tile, _acc_spec((1, D)), wide, wide],
        out_shape=[jax.ShapeDtypeStruct((T, D), F32), jax.ShapeDtypeStruct((1, D), F32),
                   jax.ShapeDtypeStruct((T, F), BF16), jax.ShapeDtypeStruct((T, F), BF16)],
        scratch_shapes=[pltpu.VMEM((tm, D), F32)],
        args=(x, g, gate, up, dh, wgt, wut, wdt), exchange=exchange)


def _matmul_tn(a, b, name, scale=None, exchange=None):
    T, K = a.shape
    N = b.shape[1]
    tk = _pick(K, (1024, 1408, 1280, 768, 512, 256, 128))
    tn = _pick(N, (1024, 1408, 1280, 768, 512, 256, 128))
    tt = _pick(T, (2048, 1024, 512, 256, 128))
    n_t = T // tt

    def body(a_ref, b_ref, o_ref, o16_ref):
        @pl.when(pl.program_id(2) == 0)
        def _():
            o_ref[...] = jnp.zeros_like(o_ref)

        o_ref[...] += _dot_tn(a_ref[...].astype(BF16), b_ref[...].astype(BF16))

        @pl.when(pl.program_id(2) == n_t - 1)
        def _():
            if scale is not None:
                o_ref[...] = o_ref[...] * scale
            o16_ref[...] = o_ref[...].astype(BF16)

    tile = pl.BlockSpec((tk, tn), lambda i, j, t: (i, j))
    outs, got = _pcall(
        body, name=name, grid=(K // tk, N // tn, n_t),
        in_specs=[pl.BlockSpec((tt, tk), lambda i, j, t: (t, i)), pl.BlockSpec((tt, tn), lambda i, j, t: (t, j))],
        out_specs=[tile, tile],
        out_shape=[jax.ShapeDtypeStruct((K, N), F32), jax.ShapeDtypeStruct((K, N), BF16)],
        args=(a, b), exchange=exchange)
    return tuple(outs), got


def _rope_spec(tm, S):
    per = S // tm
    return pl.BlockSpec((tm, HEAD_BLOCK), lambda i: (i % per, 0))


def _proj_fwd(h1, g_mix, w_in, g_q, wq, g_kv, wkv, rope_c, rope_s, S):
    T, D = h1.shape
    P = w_in.shape[1]
    tm = _pick(S, (512, 256, 128))
    QW = MLA_HEADS * HEAD_BLOCK
    VW = MLA_HEADS * V_HEAD
    SQ = SWA_HEADS * SWA_HEAD_DIM
    SK = SWA_KV_HEADS * SWA_HEAD_DIM
    o_cq, o_ckv, o_qs, o_ks, o_vs, o_kpe = 0, Q_LORA, Q_LORA + KV_LORA, Q_LORA + KV_LORA + SQ, Q_LORA + KV_LORA + SQ + SK, Q_LORA + KV_LORA + SQ + 2 * SK
    assert P == o_kpe + HEAD_BLOCK

    def body(h_ref, gm_ref, win_ref, gq_ref, wq_ref, gkv_ref, wkv_ref, c_ref, s_ref,
             u_ref, cq_ref, ckv_ref, cqn_ref, ckvn_ref, q_ref, k_ref, v_ref, qs_ref, ks_ref, vs_ref):
        u, _ = _rms_fwd(h_ref[...], gm_ref[...])
        ub = u.astype(BF16)
        u_ref[...] = ub
        proj = _dot(ub, win_ref[...])
        c_q = proj[:, o_cq:o_ckv]
        c_kv = proj[:, o_ckv:o_qs]
        cq_ref[...] = c_q
        ckv_ref[...] = c_kv
        qs_ref[...] = proj[:, o_qs:o_ks].astype(BF16)
        ks_ref[...] = proj[:, o_ks:o_vs].astype(BF16)
        vs_ref[...] = proj[:, o_vs:o_kpe].astype(BF16)
        cb, sb = c_ref[...], s_ref[...]
        kpe = proj[:, o_kpe:P]
        kpe = kpe * cb + _swap16(kpe) * sb
        cqn, _ = _rms_fwd(c_q, gq_ref[...])
        cqn = cqn.astype(BF16)
        cqn_ref[...] = cqn
        q = _dot(cqn, wq_ref[...])
        q = q * jnp.tile(cb, (1, MLA_HEADS)) + _swap16(q) * jnp.tile(sb, (1, MLA_HEADS))
        q_ref[...] = q.astype(BF16)
        ckvn, _ = _rms_fwd(c_kv, gkv_ref[...])
        ckvn = ckvn.astype(BF16)
        ckvn_ref[...] = ckvn
        kv = _dot(ckvn, wkv_ref[...])
        k_ref[...] = (kv[:, :QW] + jnp.tile(kpe, (1, MLA_HEADS))).astype(BF16)
        v_ref[...] = kv[:, QW:].astype(BF16)

    widths = [(D, BF16), (Q_LORA, F32), (KV_LORA, F32), (Q_LORA, BF16), (KV_LORA, BF16), (QW, BF16), (QW, BF16), (VW, BF16),
              (SQ, BF16), (SK, BF16), (SK, BF16)]
    return pl.pallas_call(
        body, name="proj_fwd", grid=(T // tm,),
        in_specs=[_row_spec(tm, D), _const_spec((1, D)), _const_spec((D, P)), _const_spec((1, Q_LORA)),
                  _const_spec((Q_LORA, QW)), _const_spec((1, KV_LORA)), _const_spec((KV_LORA, QW + VW)),
                  _rope_spec(tm, S), _rope_spec(tm, S)],
        out_specs=[_row_spec(tm, w) for w, _ in widths],
        out_shape=[jax.ShapeDtypeStruct((T, w), dt) for w, dt in widths],
        compiler_params=_cparams(("arbitrary",)),
    )(h1, g_mix, w_in, g_q, wq, g_kv, wkv, rope_c, rope_s)


def _proj_bwd(dq, dk, dv, dqs, dks, dvs, c_q, c_kv, h1, dh2, g_mix, w_in, g_q, wq, g_kv, wkv, rope_c, rope_s, S):
    T, D = h1.shape
    P = w_in.shape[0]
    tm = _pick(S, (256, 128))
    QW = MLA_HEADS * HEAD_BLOCK
    VW = MLA_HEADS * V_HEAD
    SQ = SWA_HEADS * SWA_HEAD_DIM
    SK = SWA_KV_HEADS * SWA_HEAD_DIM

    def body(dq_ref, dk_ref, dv_ref, dqs_ref, dks_ref, dvs_ref, cq_ref, ckv_ref, h_ref, dh2_ref, gm_ref, win_ref, gq_ref,
             wq_ref, gkv_ref, wkv_ref, c_ref, s_ref, dh1_ref, dproj_ref, dql_ref, dkv_ref, dgm_ref, dgq_ref, dgkv_ref):
        cb, sb = c_ref[...], s_ref[...]
        dqv = dq_ref[...]
        dql = dqv * jnp.tile(cb, (1, MLA_HEADS)) + _swap16(dqv * jnp.tile(sb, (1, MLA_HEADS)))
        dql = dql.astype(BF16)
        dql_ref[...] = dql
        c_q = cq_ref[...]
        _, rq = _rms_fwd(c_q, gq_ref[...])
        d_cq, dgq = _rms_bwd(c_q, gq_ref[...], rq, _dot(dql, wq_ref[...]))

        dkv_all = dk_ref[...]
        dkpe = dkv_all[:, 0:HEAD_BLOCK]
        for h in range(1, MLA_HEADS):
            dkpe = dkpe + dkv_all[:, h * HEAD_BLOCK:(h + 1) * HEAD_BLOCK]
        lane = lax.broadcasted_iota(jnp.int32, dkpe.shape, 1)
        dkpe = jnp.where((lane >= 64) & (lane < 96), dkpe, 0.0)
        dkpe = dkpe * cb + _swap16(dkpe * sb)
        dkvc = jnp.concatenate([dkv_all.astype(BF16), dv_ref[...].astype(BF16)], axis=1)
        dkv_ref[...] = dkvc
        c_kv = ckv_ref[...]
        _, rkv = _rms_fwd(c_kv, gkv_ref[...])
        d_ckv, dgkv = _rms_bwd(c_kv, gkv_ref[...], rkv, _dot(dkvc, wkv_ref[...]))

        dproj = jnp.concatenate([d_cq.astype(BF16), d_ckv.astype(BF16), dqs_ref[...].astype(BF16),
                                 dks_ref[...].astype(BF16), dvs_ref[...].astype(BF16), dkpe.astype(BF16)], axis=1)
        dproj_ref[...] = dproj
        hv = h_ref[...]
        _, rm = _rms_fwd(hv, gm_ref[...])
        dxn, dgm = _rms_bwd(hv, gm_ref[...], rm, _dot(dproj, win_ref[...]))
        dh1_ref[...] = dh2_ref[...] + dxn

        @pl.when(pl.program_id(0) == 0)
        def _():
            dgm_ref[...] = jnp.zeros_like(dgm_ref)
            dgq_ref[...] = jnp.zeros_like(dgq_ref)
            dgkv_ref[...] = jnp.zeros_like(dgkv_ref)

        dgm_ref[...] += dgm
        dgq_ref[...] += dgq
        dgkv_ref[...] += dgkv

    return pl.pallas_call(
        body, name="proj_bwd", grid=(T // tm,),
        in_specs=[_row_spec(tm, QW), _row_spec(tm, QW), _row_spec(tm, VW), _row_spec(tm, SQ), _row_spec(tm, SK), _row_spec(tm, SK),
                  _row_spec(tm, Q_LORA), _row_spec(tm, KV_LORA), _row_spec(tm, D), _row_spec(tm, D),
                  _const_spec((1, D)), _const_spec((P, D)), _const_spec((1, Q_LORA)), _const_spec((QW, Q_LORA)),
                  _const_spec((1, KV_LORA)), _const_spec((QW + VW, KV_LORA)), _rope_spec(tm, S), _rope_spec(tm, S)],
        out_specs=[_row_spec(tm, D), _row_spec(tm, P), _row_spec(tm, QW), _row_spec(tm, QW + VW),
                   _acc_spec((1, D)), _acc_spec((1, Q_LORA)), _acc_spec((1, KV_LORA))],
        out_shape=[jax.ShapeDtypeStruct((T, D), F32), jax.ShapeDtypeStruct((T, P), BF16), jax.ShapeDtypeStruct((T, QW), BF16),
                   jax.ShapeDtypeStruct((T, QW + VW), BF16), jax.ShapeDtypeStruct((1, D), F32),
                   jax.ShapeDtypeStruct((1, Q_LORA), F32), jax.ShapeDtypeStruct((1, KV_LORA), F32)],
        compiler_params=_cparams(("arbitrary",)),
    )(dq, dk, dv, dqs, dks, dvs, c_q, c_kv, h1, dh2, g_mix, w_in, g_q, wq, g_kv, wkv, rope_c, rope_s)


def _out_fwd(o_mla, o_swa, g_mla, g_swa, w_o, h1):
    T, D = h1.shape
    W = o_mla.shape[1]
    tm = _pick(T, (512, 256, 128))

    def body(om_ref, os_ref, gm_ref, gs_ref, wo_ref, h_ref, h2_ref, oc_ref):
        a, _ = _rms_fwd(om_ref[...], gm_ref[...])
        b, _ = _rms_fwd(os_ref[...], gs_ref[...])
        oc = jnp.concatenate([a.astype(BF16), b.astype(BF16)], axis=1)
        oc_ref[...] = oc
        h2_ref[...] = h_ref[...] + _dot(oc, wo_ref[...])

    return pl.pallas_call(
        body, name="out_fwd", grid=(T // tm,),
        in_specs=[_row_spec(tm, W), _row_spec(tm, W), _const_spec((1, W)), _const_spec((1, W)), _const_spec((2 * W, D)),
                  _row_spec(tm, D)],
        out_specs=[_row_spec(tm, D), _row_spec(tm, 2 * W)],
        out_shape=[jax.ShapeDtypeStruct((T, D), F32), jax.ShapeDtypeStruct((T, 2 * W), BF16)],
        compiler_params=_cparams(("arbitrary",)),
    )(o_mla, o_swa, g_mla, g_swa, w_o, h1)


def _out_bwd(dh2, o_mla, o_swa, g_mla, g_swa, w_o):
    T, D = dh2.shape
    W = o_mla.shape[1]
    tm = _pick(T, (512, 256, 128))

    def body(dh_ref, om_ref, os_ref, gm_ref, gs_ref, wo_ref, dom_ref, dos_ref, dgm_ref, dgs_ref):
        doc = _dot_nt(dh_ref[...].astype(BF16), wo_ref[...])
        om, osw = om_ref[...], os_ref[...]
        _, ra = _rms_fwd(om, gm_ref[...])
        _, rb = _rms_fwd(osw, gs_ref[...])
        da, dga = _rms_bwd(om, gm_ref[...], ra, doc[:, :W])
        db, dgb = _rms_bwd(osw, gs_ref[...], rb, doc[:, W:])
        dom_ref[...] = da
        dos_ref[...] = db

        @pl.when(pl.program_id(0) == 0)
        def _():
            dgm_ref[...] = jnp.zeros_like(dgm_ref)
            dgs_ref[...] = jnp.zeros_like(dgs_ref)

        dgm_ref[...] += dga
        dgs_ref[...] += dgb

    return pl.pallas_call(
        body, name="out_bwd", grid=(T // tm,),
        in_specs=[_row_spec(tm, D), _row_spec(tm, W), _row_spec(tm, W), _const_spec((1, W)), _const_spec((1, W)),
                  _const_spec((2 * W, D))],
        out_specs=[_row_spec(tm, W), _row_spec(tm, W), _acc_spec((1, W)), _acc_spec((1, W))],
        out_shape=[jax.ShapeDtypeStruct((T, W), F32), jax.ShapeDtypeStruct((T, W), F32),
                   jax.ShapeDtypeStruct((1, W), F32), jax.ShapeDtypeStruct((1, W), F32)],
        compiler_params=_cparams(("arbitrary",)),
    )(dh2, o_mla, o_swa, g_mla, g_swa, w_o)


def _half_mask(shape, half):
    lane = lax.broadcasted_iota(jnp.int32, shape, len(shape) - 1)
    return (lane < 64) if half == 0 else (lane >= 64)


def _mla_fwd(q, k, v, B, S, exchange=None):
    T = B * S
    tq = _pick(S, MLA_Q_TILES)
    nq = S // tq
    pairs = MLA_HEADS // 2

    def body(q_ref, k_ref, v_ref, o_ref, lse_ref):
        causal = lax.broadcasted_iota(jnp.int32, (tq, tq), 1) <= lax.broadcasted_iota(jnp.int32, (tq, tq), 0)
        low = _half_mask((tq, 2 * V_HEAD), 0)
        for qi in range(nq):
            rows = slice(qi * tq, (qi + 1) * tq)
            past = slice(0, qi * tq)
            outs, lses = [], []
            for half in range(2):
                lanes = slice(half * HEAD_BLOCK, (half + 1) * HEAD_BLOCK)
                qv = q_ref[rows, lanes]
                s_d = jnp.where(causal, _dot_nt(qv, k_ref[rows, lanes]) * MLA_SCALE, NEG)
                m = jnp.max(s_d, axis=-1, keepdims=True)
                if qi:
                    s_p = _dot_nt(qv, k_ref[past, lanes]) * MLA_SCALE
                    m = jnp.maximum(m, jnp.max(s_p, axis=-1, keepdims=True))
                p_d = jnp.exp(s_d - m)
                l = jnp.sum(p_d, axis=-1, keepdims=True)
                acc = _dot(p_d.astype(BF16), v_ref[rows, :])
                if qi:
                    p_p = jnp.exp(s_p - m)
                    l = l + jnp.sum(p_p, axis=-1, keepdims=True)
                    acc = acc + _dot(p_p.astype(BF16), v_ref[past, :])
                outs.append(acc * (1.0 / l))
                lses.append(jnp.broadcast_to(m + jnp.log(l), (tq, 2 * V_HEAD)))
            o_ref[rows, :] = jnp.where(low, outs[0], outs[1])
            lse_ref[rows, :] = jnp.where(low, lses[0], lses[1])

    blk2 = pl.BlockSpec((S, 2 * HEAD_BLOCK), lambda b, j: (b, j))
    blk1 = pl.BlockSpec((S, 2 * V_HEAD), lambda b, j: (b, j))
    return _pcall(
        body, name="mla_fwd", grid=(B, pairs),
        in_specs=[blk2, blk2, blk1], out_specs=[blk1, blk1],
        out_shape=[jax.ShapeDtypeStruct((T, pairs * 2 * V_HEAD), F32)] * 2,
        args=(q, k, v), exchange=exchange)


def _mla_bwd(q, k, v, do, o, lse, B, S, exchange=None):
    T = B * S
    tq = _pick(S, MLA_Q_TILES)
    nq = S // tq
    pairs = MLA_HEADS // 2

    def body(q_ref, k_ref, v_ref, do_ref, o_ref, lse_ref, dq_ref, dk_ref, dv_ref, dk_acc):
        causal = lax.broadcasted_iota(jnp.int32, (tq, tq), 1) <= lax.broadcasted_iota(jnp.int32, (tq, tq), 0)
        for half in range(2):
            lanes = slice(half * HEAD_BLOCK, (half + 1) * HEAD_BLOCK)
            col = half * V_HEAD
            own = _half_mask((tq, 2 * V_HEAD), half)
            for qi in reversed(range(nq)):
                rows = slice(qi * tq, (qi + 1) * tq)
                past = slice(0, qi * tq)
                first = qi == nq - 1
                qv = q_ref[rows, lanes]
                dov = jnp.where(own, do_ref[rows, :], 0.0)
                dsum = jnp.sum(dov * o_ref[rows, :], axis=-1, keepdims=True)
                dob = dov.astype(BF16)
                lse_col = lse_ref[rows, col:col + 1]

                def grads(keys, mask):
                    kv, vv = k_ref[keys, lanes], v_ref[keys, :]
                    p = jnp.exp(_dot_nt(qv, kv) * MLA_SCALE - lse_col)
                    if mask is not None:
                        p = jnp.where(mask, p, 0.0)
                    ds = (p * (_dot_nt(dob, vv) - dsum) * MLA_SCALE).astype(BF16)
                    dk, dv = _dot_tn(ds, qv), _dot_tn(p.astype(BF16), dob)
                    if first:
                        dk_acc[keys, :] = dk
                    else:
                        dk_acc[keys, :] += dk
                    if first and half == 0:
                        dv_ref[keys, :] = dv
                    else:
                        dv_ref[keys, :] += dv
                    return _dot(ds, kv)

                dq = grads(rows, causal)
                if qi:
                    dq = dq + grads(past, None)
                dq_ref[rows, lanes] = dq
            dk_ref[:, lanes] = dk_acc[...]

    blk2 = pl.BlockSpec((S, 2 * HEAD_BLOCK), lambda b, j: (b, j))
    blk1 = pl.BlockSpec((S, 2 * V_HEAD), lambda b, j: (b, j))
    return _pcall(
        body, name="mla_bwd", grid=(B, pairs),
        in_specs=[blk2, blk2, blk1, blk1, blk1, blk1], out_specs=[blk2, blk2, blk1],
        out_shape=[jax.ShapeDtypeStruct((T, MLA_HEADS * HEAD_BLOCK), F32), jax.ShapeDtypeStruct((T, MLA_HEADS * HEAD_BLOCK), F32),
                   jax.ShapeDtypeStruct((T, pairs * 2 * V_HEAD), F32)],
        scratch_shapes=[pltpu.VMEM((S, HEAD_BLOCK), F32)],
        args=(q, k, v, do, o, lse), exchange=exchange)


SWA_GROUP = SWA_HEADS // SWA_KV_HEADS
SWA_ROWS = SWA_GROUP * WINDOW


SWA_CHUNK = 8


def _bdot(a, b, ca, cb):
    return lax.dot_general(a, b, (((ca,), (cb,)), ((0,), (0,))), preferred_element_type=F32)


def _swa_chunks(nb):
    cb = next(c for c in (SWA_CHUNK, 4, 2, 1) if nb % c == 0)
    return cb, [(n0, slice(n0 * WINDOW, (n0 + cb) * WINDOW)) for n0 in range(0, nb, cb)]


def _swa_keys(ref, n0, cb):
    W = WINDOW
    cur = ref[n0 * W:(n0 + cb) * W, :]
    if n0:
        prev = ref[(n0 - 1) * W:(n0 + cb - 1) * W, :]
    else:
        prev = jnp.concatenate([cur[:W], cur[:(cb - 1) * W]], axis=0) if cb > 1 else cur
    return jnp.concatenate([prev.reshape(cb, W, W), cur.reshape(cb, W, W)], axis=1)


def _swa_stack(ref, rows, cb):
    return jnp.concatenate([ref[rows, j * WINDOW:(j + 1) * WINDOW].reshape(cb, WINDOW, WINDOW) for j in range(SWA_GROUP)], axis=1)


def _swa_unstack(low, high, cb):
    sel = _half_mask((cb, WINDOW, WINDOW), 0)
    return jnp.concatenate([jnp.where(sel, low[:, j * WINDOW:(j + 1) * WINDOW], high[:, j * WINDOW:(j + 1) * WINDOW])
                            .reshape(cb * WINDOW, WINDOW) for j in range(SWA_GROUP)], axis=1)


def _swa_sink_col(sink_ref, g):
    return jnp.concatenate([jnp.broadcast_to(sink_ref[g * SWA_GROUP + j:g * SWA_GROUP + j + 1, 0:1], (WINDOW, 1))
                            for j in range(SWA_GROUP)], axis=0)


def _swa_probs(qm, kcat, bias_g, sink, first_chunk):
    shape = (qm.shape[0], SWA_ROWS, 2 * WINDOW)
    qi = lax.broadcasted_iota(jnp.int32, shape, 1) % WINDOW
    col = lax.broadcasted_iota(jnp.int32, shape, 2)
    valid = (col > qi) & (col <= qi + WINDOW)
    if first_chunk:
        valid = valid & ((col >= WINDOW) | (lax.broadcasted_iota(jnp.int32, shape, 0) > 0))
    s = jnp.where(valid, _bdot(qm, kcat, 2, 2) * SWA_SCALE + bias_g, NEG)
    m = jnp.maximum(jnp.max(s, axis=-1, keepdims=True), sink)
    e = jnp.exp(s - m)
    e_s = jnp.exp(sink - m)
    inv = 1.0 / (jnp.sum(e, axis=-1, keepdims=True) + e_s)
    return e * inv, e_s * inv


def _swa_fwd(q, k, v, bias, sinks, B, S):
    T = B * S
    W = WINDOW
    nb = S // W

    cb, chunks = _swa_chunks(nb)

    def body(q_ref, k_ref, v_ref, bias_ref, sink_ref, o_ref):
        sink_cols = [_swa_sink_col(sink_ref, g) for g in range(SWA_KV_HEADS)]
        for n0, rows in chunks:
            kcat, vcat = _swa_keys(k_ref, n0, cb), _swa_keys(v_ref, n0, cb)
            qcat = _swa_stack(q_ref, rows, cb)
            outs = []
            for g in range(SWA_KV_HEADS):
                qm = jnp.where(_half_mask(qcat.shape, g), qcat, jnp.zeros_like(qcat))
                p, _ = _swa_probs(qm, kcat, bias_ref[g], sink_cols[g], n0 == 0)
                outs.append(_bdot(p.astype(BF16), vcat, 2, 1))
            o_ref[rows, :] = _swa_unstack(outs[0], outs[1], cb)

    SQ, SK = SWA_HEADS * SWA_HEAD_DIM, SWA_KV_HEADS * SWA_HEAD_DIM
    return pl.pallas_call(
        body, name="swa_fwd", grid=(B,),
        in_specs=[pl.BlockSpec((S, SQ), lambda b: (b, 0)), pl.BlockSpec((S, SK), lambda b: (b, 0)),
                  pl.BlockSpec((S, SK), lambda b: (b, 0)), _const_spec((SWA_KV_HEADS, SWA_ROWS, 2 * W)),
                  _const_spec((SWA_HEADS, 128))],
        out_specs=pl.BlockSpec((S, SQ), lambda b: (b, 0)),
        out_shape=jax.ShapeDtypeStruct((T, SQ), F32),
        compiler_params=_cparams(("arbitrary",)),
    )(q, k, v, bias, sinks)


def _swa_bwd(q, k, v, do, bias, sinks, B, S):
    T = B * S
    W = WINDOW
    nb = S // W
    cb, chunks = _swa_chunks(nb)

    def body(q_ref, k_ref, v_ref, do_ref, bias_ref, sink_ref, dq_ref, dk_ref, dv_ref, dbias_ref, dsink_ref):
        @pl.when(pl.program_id(0) == 0)
        def _():
            dbias_ref[...] = jnp.zeros_like(dbias_ref)
            dsink_ref[...] = jnp.zeros_like(dsink_ref)

        dk_ref[...] = jnp.zeros_like(dk_ref)
        dv_ref[...] = jnp.zeros_like(dv_ref)
        sink_cols = [_swa_sink_col(sink_ref, g) for g in range(SWA_KV_HEADS)]
        for n0, rows in chunks:
            kcat, vcat = _swa_keys(k_ref, n0, cb), _swa_keys(v_ref, n0, cb)
            qcat = _swa_stack(q_ref, rows, cb)
            docat = _swa_stack(do_ref, rows, cb)
            dk = jnp.zeros((cb, 2 * W, W), F32)
            dv = jnp.zeros((cb, 2 * W, W), F32)
            dqs = []
            for g in range(SWA_KV_HEADS):
                own = _half_mask(qcat.shape, g)
                qm = jnp.where(own, qcat, jnp.zeros_like(qcat))
                dom = jnp.where(own, docat, 0.0).astype(BF16)
                p, p_s = _swa_probs(qm, kcat, bias_ref[g], sink_cols[g], n0 == 0)
                dp = _bdot(dom, vcat, 2, 2)
                dsum = jnp.sum(p * dp, axis=-1, keepdims=True)
                ds = p * (dp - dsum)
                dbias_ref[g] += jnp.sum(ds, axis=0)
                dsink_ref[g] += jnp.broadcast_to(-jnp.sum(p_s * dsum, axis=0), (SWA_ROWS, 128))
                dsb = (ds * SWA_SCALE).astype(BF16)
                dqs.append(_bdot(dsb, kcat, 2, 1))
                dk = dk + _bdot(dsb, qm, 1, 1)
                dv = dv + _bdot(p.astype(BF16), dom, 1, 1)
            dq_ref[rows, :] = _swa_unstack(dqs[0], dqs[1], cb)
            dk_ref[rows, :] += dk[:, W:].reshape(cb * W, W)
            dv_ref[rows, :] += dv[:, W:].reshape(cb * W, W)
            dk_prev, dv_prev = dk[:, :W].reshape(cb * W, W), dv[:, :W].reshape(cb * W, W)
            if n0:
                before = slice((n0 - 1) * W, (n0 + cb - 1) * W)
                dk_ref[before, :] += dk_prev
                dv_ref[before, :] += dv_prev
            elif cb > 1:
                before = slice(0, (cb - 1) * W)
                dk_ref[before, :] += dk_prev[W:]
                dv_ref[before, :] += dv_prev[W:]

    SQ, SK = SWA_HEADS * SWA_HEAD_DIM, SWA_KV_HEADS * SWA_HEAD_DIM
    row = lambda w: pl.BlockSpec((S, w), lambda b: (b, 0))
    return pl.pallas_call(
        body, name="swa_bwd", grid=(B,),
        in_specs=[row(SQ), row(SK), row(SK), row(SQ), _const_spec((SWA_KV_HEADS, SWA_ROWS, 2 * W)), _const_spec((SWA_HEADS, 128))],
        out_specs=[row(SQ), row(SK), row(SK), _acc_spec((SWA_KV_HEADS, SWA_ROWS, 2 * W)), _acc_spec((SWA_KV_HEADS, SWA_ROWS, 128))],
        out_shape=[jax.ShapeDtypeStruct((T, SQ), F32), jax.ShapeDtypeStruct((T, SK), F32), jax.ShapeDtypeStruct((T, SK), F32),
                   jax.ShapeDtypeStruct((SWA_KV_HEADS, SWA_ROWS, 2 * W), F32), jax.ShapeDtypeStruct((SWA_KV_HEADS, SWA_ROWS, 128), F32)],
        compiler_params=_cparams(("arbitrary",)),
    )(q, k, v, do, bias, sinks)


def _bias_expand(rel_bias, bucket):
    W = WINDOW

    def body(rb_ref, bucket_ref, out_ref):
        bk = bucket_ref[...]
        for h in range(SWA_HEADS):
            def add(b, acc):
                return jnp.where(bk == b, rb_ref[b, h], acc)

            out_ref[h] = lax.fori_loop(0, REL_BUCKETS, add, jnp.zeros((W, 2 * W), F32))

    return pl.pallas_call(
        body, name="bias_expand",
        in_specs=[pl.BlockSpec(memory_space=pltpu.SMEM), pl.BlockSpec(memory_space=pltpu.VMEM)],
        out_specs=pl.BlockSpec(memory_space=pltpu.VMEM),
        out_shape=jax.ShapeDtypeStruct((SWA_HEADS, W, 2 * W), F32),
        compiler_params=_cparams(),
    )(rel_bias, bucket)


def _bias_reduce(dbias, dsink_rows, bucket):
    def body(db_ref, ds_ref, bucket_ref, out_ref, sink_out_ref, rows_ref):
        sink_lane = lax.broadcasted_iota(jnp.int32, (1, 128), 1)
        sink_acc = jnp.zeros((1, 128), F32)
        for h in range(SWA_HEADS):
            sink_acc = jnp.where(sink_lane == h, jnp.sum(ds_ref[h], axis=0, keepdims=True), sink_acc)
        sink_out_ref[...] = sink_acc
        bk = bucket_ref[...]
        for h in range(SWA_HEADS):
            dbh = db_ref[h]

            def add(b, carry):
                rows_ref[pl.ds(h * REL_BUCKETS + b, 1), :] = jnp.sum(jnp.where(bk == b, dbh, 0.0), axis=0, keepdims=True)
                return carry

            lax.fori_loop(0, REL_BUCKETS, add, 0)
        totals = jnp.sum(rows_ref[...], axis=-1, keepdims=True)
        lane = lax.broadcasted_iota(jnp.int32, (REL_BUCKETS, 128), 1)
        acc = jnp.zeros((REL_BUCKETS, 128), F32)
        for h in range(SWA_HEADS):
            acc = jnp.where(lane == h, totals[h * REL_BUCKETS:(h + 1) * REL_BUCKETS], acc)
        out_ref[...] = acc

    return pl.pallas_call(
        body, name="bias_reduce",
        out_shape=[jax.ShapeDtypeStruct((REL_BUCKETS, 128), F32), jax.ShapeDtypeStruct((1, 128), F32)],
        scratch_shapes=[pltpu.VMEM((SWA_HEADS * REL_BUCKETS, 2 * WINDOW), F32)],
        compiler_params=_cparams(),
    )(dbias, dsink_rows, bucket)


def _t5_bucket(dist):
    n = jnp.maximum(dist, 0)
    max_exact = REL_BUCKETS // 2
    nf = jnp.maximum(n, 1).astype(F32)
    large = max_exact + (jnp.log(nf / max_exact) / math.log(REL_MAX_DIST / max_exact) * (REL_BUCKETS - max_exact)).astype(jnp.int32)
    large = jnp.minimum(large, REL_BUCKETS - 1)
    return jnp.where(n < max_exact, n, large)


def _pair_heads(a, axis, inverse=False):
    shp = a.shape
    a = a.reshape(shp[:axis] + ((4, 2, 64) if inverse else (2, 4, 64)) + shp[axis + 1:])
    a = jnp.swapaxes(a, axis, axis + 1)
    return a.reshape(shp)


def _owner_blocks(w32, w16, me):
    r, c = w32.shape[0] // N_DEV, w32.shape[1]
    return lax.dynamic_slice(w32, (me * r, 0), (r, c)), w16.reshape(N_DEV, r, c)


_IN_WIDTHS = (Q_LORA, KV_LORA, QK_ROPE, SWA_HEADS * SWA_HEAD_DIM, SWA_KV_HEADS * SWA_HEAD_DIM, SWA_KV_HEADS * SWA_HEAD_DIM)


def _win_to_kernel(wt):
    parts, o = [], 0
    for wd in _IN_WIDTHS:
        parts.append(wt[o:o + wd])
        o += wd
    cq, ckv, kpe, qs, ks, vs = parts
    return jnp.concatenate([cq, ckv, _pair_heads(qs, 0), ks, vs, jnp.pad(kpe, ((64, 32), (0, 0)))], axis=0)


def _win_from_kernel(dwt):
    o = [0, Q_LORA, Q_LORA + KV_LORA, Q_LORA + KV_LORA + 512, Q_LORA + KV_LORA + 640, Q_LORA + KV_LORA + 768]
    cq, ckv, qs, ks, vs, kpe = [dwt[a:b] for a, b in zip(o, o[1:] + [dwt.shape[0]])]
    return jnp.concatenate([cq, ckv, kpe[64:96], _pair_heads(qs, 0, inverse=True), ks, vs], axis=0)


def _wq_to_kernel(wt):
    return jnp.pad(wt.reshape(MLA_HEADS, QK_NOPE + QK_ROPE, Q_LORA), ((0, 0), (0, 32), (0, 0))).reshape(MLA_HEADS * HEAD_BLOCK, Q_LORA)


def _wq_from_kernel(dwt):
    return dwt.reshape(MLA_HEADS, HEAD_BLOCK, Q_LORA)[:, :QK_NOPE + QK_ROPE].reshape(-1, Q_LORA)


def _wkv_to_kernel(wt):
    w3 = wt.reshape(MLA_HEADS, QK_NOPE + V_HEAD, KV_LORA)
    kpart = jnp.pad(w3[:, :QK_NOPE], ((0, 0), (0, 64), (0, 0))).reshape(MLA_HEADS * HEAD_BLOCK, KV_LORA)
    return jnp.concatenate([kpart, w3[:, QK_NOPE:].reshape(MLA_HEADS * V_HEAD, KV_LORA)], axis=0)


def _wkv_from_kernel(dwt):
    dk = dwt[:MLA_HEADS * HEAD_BLOCK].reshape(MLA_HEADS, HEAD_BLOCK, KV_LORA)[:, :QK_NOPE]
    dv = dwt[MLA_HEADS * HEAD_BLOCK:].reshape(MLA_HEADS, V_HEAD, KV_LORA)
    return jnp.concatenate([dk, dv], axis=1).reshape(-1, KV_LORA)


def _wo_to_kernel(w):
    return jnp.concatenate([w[:512], _pair_heads(w[512:], 0)], axis=0)


def _wo_from_kernel(dw):
    return jnp.concatenate([dw[:512], _pair_heads(dw[512:], 0, inverse=True)], axis=0)


def _kernel_weights(key, stored):
    if key in ("g1", "u1", "g2", "u2"):
        return {key + "t": stored}
    if key in ("d1", "d2"):
        return {key: stored}
    if key == "o":
        return {key: _wo_to_kernel(stored)}
    kt = _TO_KERNEL[key](stored)
    return {key: kt.T, key + "t": kt}


_SMALL = ("g_ffn1", "g_mix", "g_q_a", "g_kv_a", "attn_sinks", "rel_bias", "g_out_mla", "g_out_swa", "g_ffn2", "g_final")


class _Comm:
    def __init__(self, shards, me):
        self.shards, self.me = shards, me
        self.own32, self.recv16 = {}, {}

    def gather(self, keys):
        return [("gather", self.shards[k]) for k in keys]

    def gathered(self, keys, results, wts):
        for k, g in zip(keys, results):
            wts.update(_kernel_weights(k, g.reshape(N_DEV * g.shape[1], g.shape[2])))

    def scatter(self, keys, grads):
        jobs = []
        for k in keys:
            g32, g16 = (_FROM_KERNEL[k](g) if k in _FROM_KERNEL else g for g in grads[k])
            self.own32[k], blocks = _owner_blocks(g32, g16, self.me)
            jobs.append(("scatter", blocks))
        return jobs

    def scattered(self, keys, results):
        self.recv16.update(zip(keys, results))


def _step(x, target, gains, wts, comm):
    def jobs_in(keys):
        return comm.gather(keys) if comm else None

    def jobs_out(keys):
        return comm.scatter(keys, grads) if comm else None

    B, S, D = x.shape
    T = B * S
    x2 = x.reshape(T, D)
    t2 = target.reshape(T, D)

    pos = jnp.arange(S, dtype=F32)
    inv_freq = ROPE_THETA ** (-jnp.arange(0, QK_ROPE, 2, dtype=F32) / QK_ROPE)
    ang = pos[:, None] * inv_freq[None, :]
    cos, sin = jnp.cos(ang), jnp.sin(ang)
    ones, zeros = jnp.ones((S, 64), F32), jnp.zeros((S, 32), F32)
    rope_c = jnp.concatenate([ones, cos, cos, zeros], axis=1)
    rope_s = jnp.concatenate([0.0 * ones, -sin, sin, zeros], axis=1)

    qi = jnp.arange(WINDOW)[:, None]
    kj = jnp.arange(2 * WINDOW)[None, :]
    bucket = _t5_bucket(qi + WINDOW - kj).astype(jnp.int32)
    bias = _bias_expand(gains["rel_bias"], bucket).reshape(SWA_KV_HEADS, SWA_ROWS, 2 * WINDOW)
    sinks = jnp.broadcast_to(gains["attn_sinks"].reshape(SWA_HEADS, 1), (SWA_HEADS, 128))
    g_swa = _pair_heads(gains["g_out_swa"], 1)

    keys = ("in", "q", "kv", "o", "g2")
    (h1, n1, gate1, up1, act1), got = _ffn_fwd(x2, gains["g_ffn1"], wts["g1t"], wts["u1t"], wts["d1"], "ffn1_fwd", jobs_in(keys))
    if comm:
        comm.gathered(keys, got, wts)
    (u, c_q, c_kv, cqn, ckvn, q, k, v, qs, ks, vs) = _proj_fwd(
        h1, gains["g_mix"], wts["in"], gains["g_q_a"], wts["q"], gains["g_kv_a"], wts["kv"], rope_c, rope_s, S)
    keys = ("u2", "d2")
    (o_mla, lse), got = _mla_fwd(q, k, v, B, S, jobs_in(keys))
    if comm:
        comm.gathered(keys, got, wts)
    o_swa = _swa_fwd(qs, ks, vs, bias, sinks, B, S)
    h2, oc = _out_fwd(o_mla, o_swa, gains["g_out_mla"], g_swa, wts["o"], h1)
    (dh3, n2, gate2, up2, act2, loss, dg_final), _ = _ffn_fwd(
        h2, gains["g_ffn2"], wts["g2t"], wts["u2t"], wts["d2"], "ffn2_fwd", loss_head=(t2, gains["g_final"].reshape(1, D)))

    grads = {}
    (dh2, dg_ffn2, dgate2, dup2), _ = _ffn_bwd(h2, gains["g_ffn2"], gate2, up2, dh3, wts["g2t"], wts["u2t"], wts["d2"], "ffn2_bwd")
    grads["g2"], _ = _matmul_tn(dgate2, n2, "dw_gate2")
    grads["u2"], _ = _matmul_tn(dup2, n2, "dw_up2")
    grads["d2"], _ = _matmul_tn(act2, dh3, "dw_down2", 0.5)
    do_mla, do_swa, dg_mla, dg_swa = _out_bwd(dh2, o_mla, o_swa, gains["g_out_mla"], g_swa, wts["o"])
    grads["o"], _ = _matmul_tn(oc, dh2, "dw_o")
    keys = ("g2", "u2", "d2")
    (dq, dk, dv), got = _mla_bwd(q, k, v, do_mla, o_mla, lse, B, S, jobs_out(keys))
    if comm:
        comm.scattered(keys, got)
    dqs, dks, dvs, dbias, dsink = _swa_bwd(qs, ks, vs, do_swa, bias, sinks, B, S)
    dh1, dproj, dql, dkvc, dg_mix, dg_q, dg_kv = _proj_bwd(
        dq, dk, dv, dqs, dks, dvs, c_q, c_kv, h1, dh2, gains["g_mix"], wts["int"], gains["g_q_a"], wts["qt"],
        gains["g_kv_a"], wts["kvt"], rope_c, rope_s, S)
    grads["in"], _ = _matmul_tn(dproj, u, "dw_in")
    grads["q"], _ = _matmul_tn(dql, cqn, "dw_q")
    grads["kv"], _ = _matmul_tn(dkvc, ckvn, "dw_kv")
    grads["d1"], _ = _matmul_tn(act1, dh1, "dw_down1", 0.5)
    keys = ("o", "in", "q", "kv", "d1")
    (dx, dg_ffn1, dgate1, dup1), got = _ffn_bwd(
        x2, gains["g_ffn1"], gate1, up1, dh1, wts["g1t"], wts["u1t"], wts["d1"], "ffn1_bwd", jobs_out(keys))
    if comm:
        comm.scattered(keys, got)
    grads["g1"], _ = _matmul_tn(dgate1, n1, "dw_gate1")
    grads["u1"], got = _matmul_tn(dup1, n1, "dw_up1", exchange=jobs_out(("g1",)))
    if comm:
        comm.scattered(("g1",), got)
        comm.scattered(("u1",), _exchange(jobs_out(("u1",)), "grad_exchange_last"))

    d_rel_bias, d_sinks = _bias_reduce(dbias.reshape(SWA_HEADS, WINDOW, 2 * WINDOW), dsink.reshape(SWA_HEADS, WINDOW, 128), bucket)
    small = {
        "g_ffn1": dg_ffn1, "g_mix": dg_mix, "g_q_a": dg_q, "g_kv_a": dg_kv, "attn_sinks": d_sinks,
        "rel_bias": d_rel_bias, "g_out_mla": dg_mla, "g_out_swa": _pair_heads(dg_swa, 1, inverse=True),
        "g_ffn2": dg_ffn2, "g_final": dg_final, "loss": loss,
    }
    return dx.reshape(B, S, D), grads, small


_WEIGHTS = ("g_ffn1", "w_ffn1_gate", "w_ffn1_up", "w_ffn1_down", "g_mix", "w_in", "g_q_a", "w_q_b", "g_kv_a", "w_kv_b",
            "attn_sinks", "rel_bias", "g_out_mla", "g_out_swa", "w_o", "g_ffn2", "w_ffn2_gate", "w_ffn2_up", "w_ffn2_down",
            "g_final")
_BIG = (("w_ffn1_gate", "g1", True), ("w_ffn1_up", "u1", True), ("w_ffn1_down", "d1", False), ("w_in", "in", True),
        ("w_q_b", "q", True), ("w_kv_b", "kv", True), ("w_o", "o", False), ("w_ffn2_gate", "g2", True),
        ("w_ffn2_up", "u2", True), ("w_ffn2_down", "d2", False))
_TO_KERNEL = {"in": _win_to_kernel, "q": _wq_to_kernel, "kv": _wkv_to_kernel}
_FROM_KERNEL = {"in": _win_from_kernel, "q": _wq_from_kernel, "kv": _wkv_from_kernel, "o": _wo_from_kernel}


def kernel(x, g_ffn1, w_ffn1_gate, w_ffn1_up, w_ffn1_down, g_mix, w_in, g_q_a, w_q_b, g_kv_a, w_kv_b, attn_sinks, rel_bias, g_out_mla, g_out_swa, w_o, g_ffn2, w_ffn2_gate, w_ffn2_up, w_ffn2_down, g_final, loss_target, m_g_ffn1, m_w_ffn1_gate, m_w_ffn1_up, m_w_ffn1_down, m_g_mix, m_w_in, m_g_q_a, m_w_q_b, m_g_kv_a, m_w_kv_b, m_attn_sinks, m_rel_bias, m_g_out_mla, m_g_out_swa, m_w_o, m_g_ffn2, m_w_ffn2_gate, m_w_ffn2_up, m_w_ffn2_down, m_g_final, v_g_ffn1, v_w_ffn1_gate, v_w_ffn1_up, v_w_ffn1_down, v_g_mix, v_w_in, v_g_q_a, v_w_q_b, v_g_kv_a, v_w_kv_b, v_attn_sinks, v_rel_bias, v_g_out_mla, v_g_out_swa, v_w_o, v_g_ffn2, v_w_ffn2_gate, v_w_ffn2_up, v_w_ffn2_down, v_g_final):
    w = dict(zip(_WEIGHTS, (g_ffn1, w_ffn1_gate, w_ffn1_up, w_ffn1_down, g_mix, w_in, g_q_a, w_q_b, g_kv_a, w_kv_b, attn_sinks,
                            rel_bias, g_out_mla, g_out_swa, w_o, g_ffn2, w_ffn2_gate, w_ffn2_up, w_ffn2_down, g_final)))
    m = dict(zip(_WEIGHTS, (m_g_ffn1, m_w_ffn1_gate, m_w_ffn1_up, m_w_ffn1_down, m_g_mix, m_w_in, m_g_q_a, m_w_q_b, m_g_kv_a,
                            m_w_kv_b, m_attn_sinks, m_rel_bias, m_g_out_mla, m_g_out_swa, m_w_o, m_g_ffn2, m_w_ffn2_gate,
                            m_w_ffn2_up, m_w_ffn2_down, m_g_final)))
    v = dict(zip(_WEIGHTS, (v_g_ffn1, v_w_ffn1_gate, v_w_ffn1_up, v_w_ffn1_down, v_g_mix, v_w_in, v_g_q_a, v_w_q_b, v_g_kv_a,
                            v_w_kv_b, v_attn_sinks, v_rel_bias, v_g_out_mla, v_g_out_swa, v_w_o, v_g_ffn2, v_w_ffn2_gate,
                            v_w_ffn2_up, v_w_ffn2_down, v_g_final)))
    me = 4 * lax.axis_index("x") + 2 * lax.axis_index("y") + lax.axis_index("c")

    stored = lambda src, name, by_col: src[name][0].T if by_col else src[name][0]
    shards = dict(zip([key for _, key, _ in _BIG], _cast_bf16([stored(w, name, by_col) for name, _, by_col in _BIG])))
    comm = _Comm(shards, me)
    first = ("g1", "u1", "d1")
    wts = {}
    comm.gathered(first, _all_gather([shards[k] for k in first], "weights_all_gather"), wts)

    gains = {n: (w[n] if n == "rel_bias" else w[n].reshape(1, -1)) for n in _SMALL}
    grad_x, _, small = _step(x, loss_target, gains, wts, comm)

    small_gathered = _all_gather([small[n] for n in _SMALL] + [small["loss"]], "small_all_gather")

    out_g, out_d, out_m, out_v = {}, {}, {}, {}
    me1 = me.astype(jnp.int32).reshape(1)
    for name, key, by_col in _BIG:
        updates = _adam_big(stored(w, name, by_col), stored(m, name, by_col), stored(v, name, by_col),
                            comm.own32[key], comm.recv16[key], me1)
        out_g[name], out_d[name], out_m[name], out_v[name] = ((a.T if by_col else a)[None] for a in updates)
    as_2d = lambda src: [src[n] if src[n].ndim == 2 else src[n].reshape(1, -1) for n in _SMALL]
    loss, updates = _adam_small(as_2d(w), as_2d(m), as_2d(v), small_gathered[:-1], small_gathered[-1])
    for n, (g, d, mn, vn) in zip(_SMALL, updates):
        out_g[n], out_d[n], out_m[n], out_v[n] = (a.reshape(w[n].shape) for a in (g, d, mn, vn))

    return (loss[0, 0], grad_x, *[out_g[n] for n in _WEIGHTS], *[out_d[n] for n in _WEIGHTS],
            *[out_m[n] for n in _WEIGHTS], *[out_v[n] for n in _WEIGHTS])
```

```python
import functools
import math

import jax
import jax.numpy as jnp
from jax import lax
from jax.experimental import pallas as pl
from jax.experimental.pallas import tpu as pltpu

F32 = jnp.float32
BF16 = jnp.bfloat16
MESH = pl.DeviceIdType.MESH

EPS = 1e-6
NEG = -1e30
N_DEV = 8

MLA_HEADS = 8
Q_LORA = 256
KV_LORA = 128
QK_NOPE = 64
QK_ROPE = 32
V_HEAD = 64
ROPE_THETA = 10000.0
SWA_HEADS = 8
SWA_KV_HEADS = 2
SWA_HEAD_DIM = 64
WINDOW = 128
REL_BUCKETS = 32
REL_MAX_DIST = 128
HEAD_BLOCK = 128
MLA_Q_TILES = (512, 256, 128)
MLA_SCALE = (QK_NOPE + QK_ROPE) ** -0.5
SWA_SCALE = SWA_HEAD_DIM ** -0.5

ADAM_LR = 0.001
ADAM_B1 = 0.9
ADAM_B2 = 0.999
ADAM_EPS = 1e-08
ADAM_WD = 0.01
ADAM_STEP = 10

VMEM_LIMIT = 56 * 1024 * 1024


def _cparams(semantics=None):
    return pltpu.CompilerParams(dimension_semantics=semantics, vmem_limit_bytes=VMEM_LIMIT)


def _dot(a, b):
    return jnp.dot(a, b, preferred_element_type=F32)


def _dot_nt(a, b):
    return lax.dot_general(a, b, (((1,), (1,)), ((), ())), preferred_element_type=F32)


def _dot_tn(a, b):
    return lax.dot_general(a, b, (((0,), (0,)), ((), ())), preferred_element_type=F32)


def _rms_fwd(x, g):
    r = lax.rsqrt(jnp.mean(x * x, axis=-1, keepdims=True) + EPS)
    return x * r * g, r


def _rms_bwd(x, g, r, dy):
    xh = x * r
    dyg = dy * g
    dx = r * (dyg - xh * jnp.mean(dyg * xh, axis=-1, keepdims=True))
    return dx, jnp.sum(dy * xh, axis=0, keepdims=True)


def _swap16(x):
    n = x.shape[-1]
    up = pltpu.roll(x, n - 16, 1)
    down = pltpu.roll(x, 16, 1)
    lane = lax.broadcasted_iota(jnp.int32, x.shape, 1) % HEAD_BLOCK
    return jnp.where((lane >= 64) & (lane < 80), up, jnp.where((lane >= 80) & (lane < 96), down, 0.0))


def _const_spec(shape):
    return pl.BlockSpec(shape, lambda *_: (0,) * len(shape), pipeline_mode=pl.Buffered(1))


def _acc_spec(shape):
    return pl.BlockSpec(shape, lambda *_: (0,) * len(shape))


def _row_spec(tm, width):
    return pl.BlockSpec((tm, width), lambda i: (i, 0))


def _cast_bf16(arrays):
    n = len(arrays)

    def body(*refs):
        for src, dst in zip(refs[:n], refs[n:]):
            dst[...] = src[...].astype(BF16)

    return pl.pallas_call(
        body, name="cast_weights",
        out_shape=[jax.ShapeDtypeStruct(a.shape, BF16) for a in arrays],
        compiler_params=_cparams(),
    )(*arrays)


def _all_gather(arrays, name):
    n = len(arrays)

    def body(*refs):
        ins, outs = refs[:n], refs[n:2 * n]
        send_sems, recv_sems, local_sems = refs[2 * n:]
        x, y, c = lax.axis_index("x"), lax.axis_index("y"), lax.axis_index("c")
        me, sibling = (x, y, c), (x, y, 1 - c)
        chips = [(1 - x, y), (x, 1 - y), (1 - x, 1 - y)]

        def slot(a, dev):
            return outs[a].at[4 * dev[0] + 2 * dev[1] + dev[2]]

        def copy(a, k, block, to, src=None):
            return pltpu.make_async_remote_copy(
                src_ref=slot(a, block) if src is None else src, dst_ref=slot(a, block),
                send_sem=send_sems.at[a, k], recv_sem=recv_sems.at[a, k], device_id=to, device_id_type=MESH)

        mine = [pltpu.make_async_copy(ins[a], slot(a, me), local_sems.at[a]) for a in range(n)]
        for cp in mine:
            cp.start()
        first = []
        for a in range(n):
            first.append(copy(a, 0, me, sibling, src=ins[a]))
            first += [copy(a, 1 + j, me, (*chip, c), src=ins[a]) for j, chip in enumerate(chips)]
        for cp in first:
            cp.start()
        passed = []
        for j, chip in enumerate(chips):
            for a in range(n):
                copy(a, 1 + j, (*chip, c), me).wait_recv()
                cp = copy(a, 4 + j, (*chip, c), sibling)
                cp.start()
                passed.append(cp)
        for a in range(n):
            copy(a, 0, sibling, me).wait_recv()
            for j, chip in enumerate(chips):
                copy(a, 4 + j, (*chip, 1 - c), me).wait_recv()
        for cp in first + passed:
            cp.wait_send()
        for cp in mine:
            cp.wait()

    any_spec = pl.BlockSpec(memory_space=pl.ANY)
    return pl.pallas_call(
        body, name=name,
        out_shape=[jax.ShapeDtypeStruct((N_DEV,) + a.shape, a.dtype) for a in arrays],
        in_specs=[any_spec] * n, out_specs=[any_spec] * n,
        scratch_shapes=[pltpu.SemaphoreType.DMA((n, 7)), pltpu.SemaphoreType.DMA((n, 7)), pltpu.SemaphoreType.DMA((n,))],
    )(*arrays)


def _exchange_shapes(jobs):
    dsts = [jax.ShapeDtypeStruct(((N_DEV,) + a.shape) if kind == "gather" else a.shape, a.dtype) for kind, a in jobs]
    n = len(jobs)
    sems = [pltpu.SemaphoreType.DMA((n, N_DEV - 1)), pltpu.SemaphoreType.DMA((n, N_DEV - 1)), pltpu.SemaphoreType.DMA((n,))]
    return dsts, sems


def _exchange_copies(kinds, srcs, dsts, send_sems, recv_sems, local_sems):
    x, y, c = lax.axis_index("x"), lax.axis_index("y"), lax.axis_index("c")
    me = 4 * x + 2 * y + c
    remote, local = [], []
    for a, kind in enumerate(kinds):
        for k in range(1, N_DEV):
            peer = (1 - x if k & 4 else x, 1 - y if k & 2 else y, 1 - c if k & 1 else c)
            src = srcs[a] if kind == "gather" else srcs[a].at[4 * peer[0] + 2 * peer[1] + peer[2]]
            remote.append(pltpu.make_async_remote_copy(
                src_ref=src, dst_ref=dsts[a].at[me], send_sem=send_sems.at[a, k - 1], recv_sem=recv_sems.at[a, k - 1],
                device_id=peer, device_id_type=MESH))
        src = srcs[a] if kind == "gather" else srcs[a].at[me]
        local.append(pltpu.make_async_copy(src, dsts[a].at[me], local_sems.at[a]))
    return remote, local


def _scatter_copies(src_ref, land_ref, send_sems, recv_sems):
    x, y, c = lax.axis_index("x"), lax.axis_index("y"), lax.axis_index("c")
    me = 4 * x + 2 * y + c
    copies = []
    for k in range(1, N_DEV):
        peer = (1 - x if k & 4 else x, 1 - y if k & 2 else y, 1 - c if k & 1 else c)
        copies.append(pltpu.make_async_remote_copy(
            src_ref=src_ref.at[4 * peer[0] + 2 * peer[1] + peer[2]], dst_ref=land_ref.at[me],
            send_sem=send_sems.at[k - 1], recv_sem=recv_sems.at[k - 1], device_id=peer, device_id_type=MESH))
    return copies


def _scatter_start(blocks, name):
    hbm = pl.BlockSpec(memory_space=pltpu.HBM)
    sem = pl.BlockSpec(memory_space=pltpu.SEMAPHORE)

    def body(src_ref, land_ref, send_sems, recv_sems, src_thru, land_thru, token):
        for cp in _scatter_copies(src_ref, land_ref, send_sems, recv_sems):
            cp.start()
        token[...] = jnp.zeros_like(token)

    *handle, token = pl.pallas_call(
        body, name=name,
        out_shape=(pltpu.SemaphoreType.DMA((N_DEV - 1,)), pltpu.SemaphoreType.DMA((N_DEV - 1,)),
                   pltpu.HBM(blocks.shape, blocks.dtype), pltpu.HBM(blocks.shape, blocks.dtype), jax.ShapeDtypeStruct((8, 128), F32)),
        in_specs=(hbm, hbm), out_specs=(sem, sem, hbm, hbm, pl.BlockSpec(memory_space=pltpu.VMEM)),
        input_output_aliases={0: 2, 1: 3},
        compiler_params=pltpu.CompilerParams(has_side_effects=pltpu.SideEffectType.DATAFLOW_SIDE_EFFECTING),
    )(pltpu.with_memory_space_constraint(blocks, pltpu.HBM),
      pltpu.with_memory_space_constraint(lax.empty(blocks.shape, blocks.dtype), pltpu.HBM))
    return handle, token


def _scatter_wait(handle, after, name):
    hbm = pl.BlockSpec(memory_space=pltpu.HBM)
    sem = pl.BlockSpec(memory_space=pltpu.SEMAPHORE)
    send_sems, recv_sems, src_thru, land_thru = handle

    def body(src_ref, land_ref, send_sems, recv_sems, after_ref, src_dead, got_ref):
        for cp in _scatter_copies(src_ref, land_ref, send_sems, recv_sems):
            cp.wait_send()
            cp.wait_recv()

    return pl.pallas_call(
        body, name=name,
        out_shape=(pltpu.HBM(src_thru.shape, src_thru.dtype), pltpu.HBM(land_thru.shape, land_thru.dtype)),
        in_specs=(hbm, hbm, sem, sem, pl.BlockSpec(memory_space=pl.ANY)), out_specs=(hbm, hbm),
        input_output_aliases={0: 0, 1: 1},
        compiler_params=pltpu.CompilerParams(has_side_effects=pltpu.SideEffectType.DATAFLOW_SIDE_EFFECTING),
    )(src_thru, land_thru, send_sems, recv_sems, after)[1]


def _pcall(body, *, name, grid, in_specs, out_specs, out_shape, args, scratch_shapes=(), exchange=None):
    semantics = ("arbitrary",) * len(grid)
    if not exchange:
        outs = pl.pallas_call(body, name=name, grid=grid, in_specs=list(in_specs), out_specs=list(out_specs),
                              out_shape=list(out_shape), scratch_shapes=list(scratch_shapes),
                              compiler_params=_cparams(semantics))(*args)
        return list(outs), []
    kinds = [kind for kind, _ in exchange]
    n_in, n_out, n_job = len(in_specs), len(out_specs), len(exchange)
    dst_shapes, sems = _exchange_shapes(exchange)

    def with_exchange(*refs):
        ins, srcs = refs[:n_in], refs[n_in:n_in + n_job]
        outs, dsts = refs[n_in + n_job:n_in + n_job + n_out], refs[n_in + n_job + n_out:n_in + 2 * n_job + n_out]
        scratch, sem_refs = refs[n_in + 2 * n_job + n_out:-3], refs[-3:]
        first = functools.reduce(jnp.logical_and, [pl.program_id(d) == 0 for d in range(len(grid))])
        last = functools.reduce(jnp.logical_and, [pl.program_id(d) == grid[d] - 1 for d in range(len(grid))])

        @pl.when(first)
        def _():
            remote, local = _exchange_copies(kinds, srcs, dsts, *sem_refs)
            for cp in local + remote:
                cp.start()

        body(*ins, *outs, *scratch)

        @pl.when(last)
        def _():
            remote, local = _exchange_copies(kinds, srcs, dsts, *sem_refs)
            for cp in remote + local:
                cp.wait()

    any_spec = pl.BlockSpec(memory_space=pl.ANY)
    outs = pl.pallas_call(
        with_exchange, name=name, grid=grid, in_specs=list(in_specs) + [any_spec] * n_job,
        out_specs=list(out_specs) + [any_spec] * n_job, out_shape=list(out_shape) + dst_shapes,
        scratch_shapes=list(scratch_shapes) + sems, compiler_params=_cparams(semantics),
    )(*args, *[a for _, a in exchange])
    return list(outs[:n_out]), list(outs[n_out:])


def _adam_math(w, g, m, v):
    m = ADAM_B1 * m + (1.0 - ADAM_B1) * g
    v = ADAM_B2 * v + (1.0 - ADAM_B2) * (g * g)
    m_hat = m / (1.0 - ADAM_B1 ** ADAM_STEP)
    v_hat = v / (1.0 - ADAM_B2 ** ADAM_STEP)
    delta = -ADAM_LR * (m_hat / (jnp.sqrt(v_hat) + ADAM_EPS) + ADAM_WD * w)
    return delta, m, v


def _adam_big(w, m, v, own32, recv16, me, after):
    r, c = w.shape
    tr = r if r <= 512 else (512 if r % 512 == 0 else 352)
    assert r % tr == 0

    def body(me_ref, w_ref, m_ref, v_ref, s_ref, r_ref, after_ref, g_out, d_out, m_out, v_out):
        g = s_ref[...]
        for k in range(1, N_DEV):
            g = g + r_ref[jnp.bitwise_xor(me_ref[0], k)].astype(F32)
        d, mn, vn = _adam_math(w_ref[...], g, m_ref[...], v_ref[...])
        g_out[...] = g
        d_out[...] = d
        m_out[...] = mn
        v_out[...] = vn

    blk = pl.BlockSpec((tr, c), lambda i, s: (i, 0))
    grid_spec = pltpu.PrefetchScalarGridSpec(
        num_scalar_prefetch=1, grid=(r // tr,),
        in_specs=[blk, blk, blk, blk, pl.BlockSpec((N_DEV, tr, c), lambda i, s: (0, i, 0)), pl.BlockSpec(memory_space=pl.ANY)],
        out_specs=[blk, blk, blk, blk])
    return pl.pallas_call(
        body, name="adamw_big", grid_spec=grid_spec,
        out_shape=[jax.ShapeDtypeStruct((r, c), F32)] * 4,
        compiler_params=_cparams(("arbitrary",)),
    )(me, w, m, v, own32, recv16, after)


def _adam_small(ws, ms, vs, partials, loss_parts):
    n = len(ws)

    def body(*refs):
        w_refs, m_refs, v_refs, p_refs = (refs[i * n:(i + 1) * n] for i in range(4))
        l_ref, loss_out = refs[4 * n], refs[4 * n + 1]
        outs = refs[4 * n + 2:]

        def total(ref):
            s = ref[0]
            for dev in range(1, N_DEV):
                s = s + ref[dev]
            return s

        loss_out[...] = total(l_ref)
        for i in range(n):
            r, c = w_refs[i].shape
            g = total(p_refs[i])[:, :c]
            d, mn, vn = _adam_math(w_refs[i][...], g, m_refs[i][...], v_refs[i][...])
            for j, val in enumerate((g, d, mn, vn)):
                outs[4 * i + j][...] = val

    shapes = [jax.ShapeDtypeStruct((1, 128), F32)]
    for w in ws:
        shapes += [jax.ShapeDtypeStruct(w.shape, F32)] * 4
    outs = pl.pallas_call(body, name="adamw_small", out_shape=shapes, compiler_params=_cparams())(*ws, *ms, *vs, *partials, loss_parts)
    return outs[0], [outs[1 + 4 * i:5 + 4 * i] for i in range(n)]


def _pick(n, prefer):
    for t in prefer:
        if n % t == 0:
            return t
    return n


def _ffn_fwd(x, g, wg, wu, wd, name, exchange=None, loss_head=None):
    T, D = x.shape
    F = wg.shape[0]
    tm = _pick(T, (512, 256, 128))
    fc = _pick(F, (256, 128))

    def body(x_ref, g_ref, wg_ref, wu_ref, wd_ref, *rest):
        h_ref, n_ref, gate_ref, up_ref, act_ref = rest[-7:-2] if loss_head else rest
        xv = x_ref[...]
        n, _ = _rms_fwd(xv, g_ref[...])
        nb = n.astype(BF16)
        n_ref[...] = nb
        acc = jnp.zeros((tm, D), F32)
        for f0 in range(0, F, fc):
            gate = _dot_nt(nb, wg_ref[f0:f0 + fc, :])
            up = _dot_nt(nb, wu_ref[f0:f0 + fc, :])
            gate_ref[:, f0:f0 + fc] = gate.astype(BF16)
            up_ref[:, f0:f0 + fc] = up.astype(BF16)
            act = (gate * (1.0 / (1.0 + jnp.exp(-gate))) * up).astype(BF16)
            act_ref[:, f0:f0 + fc] = act
            acc = acc + _dot(act, wd_ref[f0:f0 + fc, :])
        hv = xv + 0.5 * acc
        if not loss_head:
            h_ref[...] = hv
            return
        t_ref, gf_ref, loss_ref, dgf_ref = rest[0], rest[1], rest[-2], rest[-1]
        gf = gf_ref[...]
        y, r = _rms_fwd(hv, gf)
        diff = y - t_ref[...]
        dh, dgp = _rms_bwd(hv, gf, r, diff * (1.0 / D))
        h_ref[...] = dh
        part = 0.5 * jnp.sum(jnp.sum(diff * diff, axis=-1, keepdims=True) * (1.0 / D), axis=0, keepdims=True)

        @pl.when(pl.program_id(0) == 0)
        def _():
            loss_ref[...] = jnp.zeros_like(loss_ref)
            dgf_ref[...] = jnp.zeros_like(dgf_ref)

        loss_ref[...] += jnp.broadcast_to(part, loss_ref.shape)
        dgf_ref[...] += dgp

    in_specs = [_row_spec(tm, D), _const_spec((1, D)), _const_spec((F, D)), _const_spec((F, D)), _const_spec((F, D))]
    out_specs = [_row_spec(tm, D), _row_spec(tm, D), _row_spec(tm, F), _row_spec(tm, F), _row_spec(tm, F)]
    out_shape = [jax.ShapeDtypeStruct((T, D), F32), jax.ShapeDtypeStruct((T, D), BF16),
                 jax.ShapeDtypeStruct((T, F), BF16), jax.ShapeDtypeStruct((T, F), BF16), jax.ShapeDtypeStruct((T, F), BF16)]
    args = (x, g, wg, wu, wd)
    if loss_head:
        in_specs += [_row_spec(tm, D), _const_spec((1, D))]
        out_specs += [_acc_spec((1, 128)), _acc_spec((1, D))]
        out_shape += [jax.ShapeDtypeStruct((1, 128), F32), jax.ShapeDtypeStruct((1, D), F32)]
        args += tuple(loss_head)
    return _pcall(body, name=name, grid=(T // tm,), in_specs=in_specs, out_specs=out_specs, out_shape=out_shape, args=args,
                  exchange=exchange)


FFN_BWD_CHUNK = 1408
FFN_BWD_GROUP = 1408


def _ffn_bwd(x, g, gate, up, dh, wgt, wut, wdt, name, exchange=None):
    T, D = x.shape
    F = wgt.shape[0]
    tm = _pick(T, (512, 256, 128))
    halves = 2
    Fh = F // halves
    assert Fh * halves == F and Fh % 128 == 0
    groups = [(g0, min(g0 + FFN_BWD_GROUP, Fh)) for g0 in range(0, Fh, FFN_BWD_GROUP)]

    def body(x_ref, g_ref, gate_ref, up_ref, dh_ref, wg_ref, wu_ref, wd_ref, dx_ref, dg_ref, dgate_ref, dup_ref, dn_ref):
        i, h = pl.program_id(0), pl.program_id(1)
        dhv = dh_ref[...]
        dho = (0.5 * dhv).astype(BF16)
        dn = jnp.zeros((tm, D), F32)
        for g0, g1 in groups:
            for f0 in range(g0, g1, FFN_BWD_CHUNK):
                f1 = min(f0 + FFN_BWD_CHUNK, g1)
                d_act = _dot_nt(dho, wd_ref[f0:f1, :])
                gt = gate_ref[:, f0:f1].astype(F32)
                u = up_ref[:, f0:f1].astype(F32)
                sg = 1.0 / (1.0 + jnp.exp(-gt))
                silu = gt * sg
                dup_ref[:, f0:f1] = (d_act * silu).astype(BF16)
                dgate_ref[:, f0:f1] = (d_act * u * (sg + silu * (1.0 - sg))).astype(BF16)
            dn = dn + _dot(dgate_ref[:, g0:g1], wg_ref[g0:g1, :]) + _dot(dup_ref[:, g0:g1], wu_ref[g0:g1, :])

        @pl.when(h == 0)
        def _():
            dn_ref[...] = dn

        @pl.when(h > 0)
        def _():
            dn_ref[...] += dn

        @pl.when((i == 0) & (h == 0))
        def _():
            dg_ref[...] = jnp.zeros_like(dg_ref)

        @pl.when(h == halves - 1)
        def _():
            xv, gv = x_ref[...], g_ref[...]
            _, r = _rms_fwd(xv, gv)
            dxn, dgp = _rms_bwd(xv, gv, r, dn_ref[...])
            dx_ref[...] = dhv + dxn
            dg_ref[...] += dgp

    tile = pl.BlockSpec((tm, D), lambda i, h: (i, 0))
    wide = pl.BlockSpec((tm, Fh), lambda i, h: (i, h))
    return _pcall(
        body, name=name, grid=(T // tm, halves),
        in_specs=[tile, _const_spec((1, D)), wide, wide, tile,
                  pl.BlockSpec((Fh, D), lambda i, h: (h, 0)), pl.BlockSpec((Fh, D), lambda i, h: (h, 0)),
                  pl.BlockSpec((Fh, D), lambda i, h: (h, 0))],
        out_specs=[tile, _acc_spec((1, D)), wide, wide],
        out_shape=[jax.ShapeDtypeStruct((T, D), F32), jax.ShapeDtypeStruct((1, D), F32),
                   jax.ShapeDtypeStruct((T, F), BF16), jax.ShapeDtypeStruct((T, F), BF16)],
        scratch_shapes=[pltpu.VMEM((tm, D), F32)],
        args=(x, g, gate, up, dh, wgt, wut, wdt), exchange=exchange)


def _matmul_tn(a, b, name, scale=None, exchange=None):
    T, K = a.shape
    N = b.shape[1]
    tk = _pick(K, (1024, 1408, 1280, 768, 512, 256, 128))
    tn = _pick(N, (1024, 1408, 1280, 768, 512, 256, 128))
    tt = _pick(T, (2048, 1024, 512, 256, 128))
    n_t = T // tt

    def body(a_ref, b_ref, o_ref, o16_ref):
        @pl.when(pl.program_id(2) == 0)
        def _():
            o_ref[...] = jnp.zeros_like(o_ref)

        o_ref[...] += _dot_tn(a_ref[...].astype(BF16), b_ref[...].astype(BF16))

        @pl.when(pl.program_id(2) == n_t - 1)
        def _():
            if scale is not None:
                o_ref[...] = o_ref[...] * scale
            o16_ref[...] = o_ref[...].astype(BF16)

    tile = pl.BlockSpec((tk, tn), lambda i, j, t: (i, j))
    outs, got = _pcall(
        body, name=name, grid=(K // tk, N // tn, n_t),
        in_specs=[pl.BlockSpec((tt, tk), lambda i, j, t: (t, i)), pl.BlockSpec((tt, tn), lambda i, j, t: (t, j))],
        out_specs=[tile, tile],
        out_shape=[jax.ShapeDtypeStruct((K, N), F32), jax.ShapeDtypeStruct((K, N), BF16)],
        args=(a, b), exchange=exchange)
    return tuple(outs), got


def _rope_spec(tm, S):
    per = S // tm
    return pl.BlockSpec((tm, HEAD_BLOCK), lambda i: (i % per, 0))


def _proj_fwd(h1, g_mix, w_in, g_q, wq, g_kv, wkv, rope_c, rope_s, S):
    T, D = h1.shape
    P = w_in.shape[1]
    tm = _pick(S, (512, 256, 128))
    QW = MLA_HEADS * HEAD_BLOCK
    VW = MLA_HEADS * V_HEAD
    SQ = SWA_HEADS * SWA_HEAD_DIM
    SK = SWA_KV_HEADS * SWA_HEAD_DIM
    o_cq, o_ckv, o_qs, o_ks, o_vs, o_kpe = 0, Q_LORA, Q_LORA + KV_LORA, Q_LORA + KV_LORA + SQ, Q_LORA + KV_LORA + SQ + SK, Q_LORA + KV_LORA + SQ + 2 * SK
    assert P == o_kpe + HEAD_BLOCK

    def body(h_ref, gm_ref, win_ref, gq_ref, wq_ref, gkv_ref, wkv_ref, c_ref, s_ref,
             u_ref, cq_ref, ckv_ref, cqn_ref, ckvn_ref, q_ref, k_ref, v_ref, qs_ref, ks_ref, vs_ref):
        u, _ = _rms_fwd(h_ref[...], gm_ref[...])
        ub = u.astype(BF16)
        u_ref[...] = ub
        proj = _dot(ub, win_ref[...])
        c_q = proj[:, o_cq:o_ckv]
        c_kv = proj[:, o_ckv:o_qs]
        cq_ref[...] = c_q
        ckv_ref[...] = c_kv
        qs_ref[...] = proj[:, o_qs:o_ks].astype(BF16)
        ks_ref[...] = proj[:, o_ks:o_vs].astype(BF16)
        vs_ref[...] = proj[:, o_vs:o_kpe].astype(BF16)
        cb, sb = c_ref[...], s_ref[...]
        kpe = proj[:, o_kpe:P]
        kpe = kpe * cb + _swap16(kpe) * sb
        cqn, _ = _rms_fwd(c_q, gq_ref[...])
        cqn = cqn.astype(BF16)
        cqn_ref[...] = cqn
        q = _dot(cqn, wq_ref[...])
        q = q * jnp.tile(cb, (1, MLA_HEADS)) + _swap16(q) * jnp.tile(sb, (1, MLA_HEADS))
        q_ref[...] = q.astype(BF16)
        ckvn, _ = _rms_fwd(c_kv, gkv_ref[...])
        ckvn = ckvn.astype(BF16)
        ckvn_ref[...] = ckvn
        kv = _dot(ckvn, wkv_ref[...])
        k_ref[...] = (kv[:, :QW] + jnp.tile(kpe, (1, MLA_HEADS))).astype(BF16)
        v_ref[...] = kv[:, QW:].astype(BF16)

    widths = [(D, BF16), (Q_LORA, F32), (KV_LORA, F32), (Q_LORA, BF16), (KV_LORA, BF16), (QW, BF16), (QW, BF16), (VW, BF16),
              (SQ, BF16), (SK, BF16), (SK, BF16)]
    return pl.pallas_call(
        body, name="proj_fwd", grid=(T // tm,),
        in_specs=[_row_spec(tm, D), _const_spec((1, D)), _const_spec((D, P)), _const_spec((1, Q_LORA)),
                  _const_spec((Q_LORA, QW)), _const_spec((1, KV_LORA)), _const_spec((KV_LORA, QW + VW)),
                  _rope_spec(tm, S), _rope_spec(tm, S)],
        out_specs=[_row_spec(tm, w) for w, _ in widths],
        out_shape=[jax.ShapeDtypeStruct((T, w), dt) for w, dt in widths],
        compiler_params=_cparams(("arbitrary",)),
    )(h1, g_mix, w_in, g_q, wq, g_kv, wkv, rope_c, rope_s)


def _proj_bwd(dq, dk, dv, dqs, dks, dvs, c_q, c_kv, h1, dh2, g_mix, w_in, g_q, wq, g_kv, wkv, rope_c, rope_s, S):
    T, D = h1.shape
    P = w_in.shape[0]
    tm = _pick(S, (256, 128))
    QW = MLA_HEADS * HEAD_BLOCK
    VW = MLA_HEADS * V_HEAD
    SQ = SWA_HEADS * SWA_HEAD_DIM
    SK = SWA_KV_HEADS * SWA_HEAD_DIM

    def body(dq_ref, dk_ref, dv_ref, dqs_ref, dks_ref, dvs_ref, cq_ref, ckv_ref, h_ref, dh2_ref, gm_ref, win_ref, gq_ref,
             wq_ref, gkv_ref, wkv_ref, c_ref, s_ref, dh1_ref, dproj_ref, dql_ref, dkv_ref, dgm_ref, dgq_ref, dgkv_ref):
        cb, sb = c_ref[...], s_ref[...]
        dqv = dq_ref[...]
        dql = dqv * jnp.tile(cb, (1, MLA_HEADS)) + _swap16(dqv * jnp.tile(sb, (1, MLA_HEADS)))
        dql = dql.astype(BF16)
        dql_ref[...] = dql
        c_q = cq_ref[...]
        _, rq = _rms_fwd(c_q, gq_ref[...])
        d_cq, dgq = _rms_bwd(c_q, gq_ref[...], rq, _dot(dql, wq_ref[...]))

        dkv_all = dk_ref[...]
        dkpe = dkv_all[:, 0:HEAD_BLOCK]
        for h in range(1, MLA_HEADS):
            dkpe = dkpe + dkv_all[:, h * HEAD_BLOCK:(h + 1) * HEAD_BLOCK]
        lane = lax.broadcasted_iota(jnp.int32, dkpe.shape, 1)
        dkpe = jnp.where((lane >= 64) & (lane < 96), dkpe, 0.0)
        dkpe = dkpe * cb + _swap16(dkpe * sb)
        dkvc = jnp.concatenate([dkv_all.astype(BF16), dv_ref[...].astype(BF16)], axis=1)
        dkv_ref[...] = dkvc
        c_kv = ckv_ref[...]
        _, rkv = _rms_fwd(c_kv, gkv_ref[...])
        d_ckv, dgkv = _rms_bwd(c_kv, gkv_ref[...], rkv, _dot(dkvc, wkv_ref[...]))

        dproj = jnp.concatenate([d_cq.astype(BF16), d_ckv.astype(BF16), dqs_ref[...].astype(BF16),
                                 dks_ref[...].astype(BF16), dvs_ref[...].astype(BF16), dkpe.astype(BF16)], axis=1)
        dproj_ref[...] = dproj
        hv = h_ref[...]
        _, rm = _rms_fwd(hv, gm_ref[...])
        dxn, dgm = _rms_bwd(hv, gm_ref[...], rm, _dot(dproj, win_ref[...]))
        dh1_ref[...] = dh2_ref[...] + dxn

        @pl.when(pl.program_id(0) == 0)
        def _():
            dgm_ref[...] = jnp.zeros_like(dgm_ref)
            dgq_ref[...] = jnp.zeros_like(dgq_ref)
            dgkv_ref[...] = jnp.zeros_like(dgkv_ref)

        dgm_ref[...] += dgm
        dgq_ref[...] += dgq
        dgkv_ref[...] += dgkv

    return pl.pallas_call(
        body, name="proj_bwd", grid=(T // tm,),
        in_specs=[_row_spec(tm, QW), _row_spec(tm, QW), _row_spec(tm, VW), _row_spec(tm, SQ), _row_spec(tm, SK), _row_spec(tm, SK),
                  _row_spec(tm, Q_LORA), _row_spec(tm, KV_LORA), _row_spec(tm, D), _row_spec(tm, D),
                  _const_spec((1, D)), _const_spec((P, D)), _const_spec((1, Q_LORA)), _const_spec((QW, Q_LORA)),
                  _const_spec((1, KV_LORA)), _const_spec((QW + VW, KV_LORA)), _rope_spec(tm, S), _rope_spec(tm, S)],
        out_specs=[_row_spec(tm, D), _row_spec(tm, P), _row_spec(tm, QW), _row_spec(tm, QW + VW),
                   _acc_spec((1, D)), _acc_spec((1, Q_LORA)), _acc_spec((1, KV_LORA))],
        out_shape=[jax.ShapeDtypeStruct((T, D), F32), jax.ShapeDtypeStruct((T, P), BF16), jax.ShapeDtypeStruct((T, QW), BF16),
                   jax.ShapeDtypeStruct((T, QW + VW), BF16), jax.ShapeDtypeStruct((1, D), F32),
                   jax.ShapeDtypeStruct((1, Q_LORA), F32), jax.ShapeDtypeStruct((1, KV_LORA), F32)],
        compiler_params=_cparams(("arbitrary",)),
    )(dq, dk, dv, dqs, dks, dvs, c_q, c_kv, h1, dh2, g_mix, w_in, g_q, wq, g_kv, wkv, rope_c, rope_s)


def _out_fwd(o_mla, o_swa, g_mla, g_swa, w_o, h1):
    T, D = h1.shape
    W = o_mla.shape[1]
    tm = _pick(T, (512, 256, 128))

    def body(om_ref, os_ref, gm_ref, gs_ref, wo_ref, h_ref, h2_ref, oc_ref):
        a, _ = _rms_fwd(om_ref[...], gm_ref[...])
        b, _ = _rms_fwd(os_ref[...], gs_ref[...])
        oc = jnp.concatenate([a.astype(BF16), b.astype(BF16)], axis=1)
        oc_ref[...] = oc
        h2_ref[...] = h_ref[...] + _dot(oc, wo_ref[...])

    return pl.pallas_call(
        body, name="out_fwd", grid=(T // tm,),
        in_specs=[_row_spec(tm, W), _row_spec(tm, W), _const_spec((1, W)), _const_spec((1, W)), _const_spec((2 * W, D)),
                  _row_spec(tm, D)],
        out_specs=[_row_spec(tm, D), _row_spec(tm, 2 * W)],
        out_shape=[jax.ShapeDtypeStruct((T, D), F32), jax.ShapeDtypeStruct((T, 2 * W), BF16)],
        compiler_params=_cparams(("arbitrary",)),
    )(o_mla, o_swa, g_mla, g_swa, w_o, h1)


def _out_bwd(dh2, o_mla, o_swa, g_mla, g_swa, w_o):
    T, D = dh2.shape
    W = o_mla.shape[1]
    tm = _pick(T, (512, 256, 128))

    def body(dh_ref, om_ref, os_ref, gm_ref, gs_ref, wo_ref, dom_ref, dos_ref, dgm_ref, dgs_ref):
        doc = _dot_nt(dh_ref[...].astype(BF16), wo_ref[...])
        om, osw = om_ref[...], os_ref[...]
        _, ra = _rms_fwd(om, gm_ref[...])
        _, rb = _rms_fwd(osw, gs_ref[...])
        da, dga = _rms_bwd(om, gm_ref[...], ra, doc[:, :W])
        db, dgb = _rms_bwd(osw, gs_ref[...], rb, doc[:, W:])
        dom_ref[...] = da
        dos_ref[...] = db

        @pl.when(pl.program_id(0) == 0)
        def _():
            dgm_ref[...] = jnp.zeros_like(dgm_ref)
            dgs_ref[...] = jnp.zeros_like(dgs_ref)

        dgm_ref[...] += dga
        dgs_ref[...] += dgb

    return pl.pallas_call(
        body, name="out_bwd", grid=(T // tm,),
        in_specs=[_row_spec(tm, D), _row_spec(tm, W), _row_spec(tm, W), _const_spec((1, W)), _const_spec((1, W)),
                  _const_spec((2 * W, D))],
        out_specs=[_row_spec(tm, W), _row_spec(tm, W), _acc_spec((1, W)), _acc_spec((1, W))],
        out_shape=[jax.ShapeDtypeStruct((T, W), F32), jax.ShapeDtypeStruct((T, W), F32),
                   jax.ShapeDtypeStruct((1, W), F32), jax.ShapeDtypeStruct((1, W), F32)],
        compiler_params=_cparams(("arbitrary",)),
    )(dh2, o_mla, o_swa, g_mla, g_swa, w_o)


def _half_mask(shape, half):
    lane = lax.broadcasted_iota(jnp.int32, shape, len(shape) - 1)
    return (lane < 64) if half == 0 else (lane >= 64)


def _mla_fwd(q, k, v, B, S, exchange=None):
    T = B * S
    tq = _pick(S, MLA_Q_TILES)
    nq = S // tq
    pairs = MLA_HEADS // 2

    def body(q_ref, k_ref, v_ref, o_ref, lse_ref):
        causal = lax.broadcasted_iota(jnp.int32, (tq, tq), 1) <= lax.broadcasted_iota(jnp.int32, (tq, tq), 0)
        low = _half_mask((tq, 2 * V_HEAD), 0)
        for qi in range(nq):
            rows = slice(qi * tq, (qi + 1) * tq)
            past = slice(0, qi * tq)
            outs, lses = [], []
            for half in range(2):
                lanes = slice(half * HEAD_BLOCK, (half + 1) * HEAD_BLOCK)
                qv = q_ref[rows, lanes]
                s_d = jnp.where(causal, _dot_nt(qv, k_ref[rows, lanes]) * MLA_SCALE, NEG)
                m = jnp.max(s_d, axis=-1, keepdims=True)
                if qi:
                    s_p = _dot_nt(qv, k_ref[past, lanes]) * MLA_SCALE
                    m = jnp.maximum(m, jnp.max(s_p, axis=-1, keepdims=True))
                p_d = jnp.exp(s_d - m)
                l = jnp.sum(p_d, axis=-1, keepdims=True)
                acc = _dot(p_d.astype(BF16), v_ref[rows, :])
                if qi:
                    p_p = jnp.exp(s_p - m)
                    l = l + jnp.sum(p_p, axis=-1, keepdims=True)
                    acc = acc + _dot(p_p.astype(BF16), v_ref[past, :])
                outs.append(acc * (1.0 / l))
                lses.append(jnp.broadcast_to(m + jnp.log(l), (tq, 2 * V_HEAD)))
            o_ref[rows, :] = jnp.where(low, outs[0], outs[1])
            lse_ref[rows, :] = jnp.where(low, lses[0], lses[1])

    blk2 = pl.BlockSpec((S, 2 * HEAD_BLOCK), lambda b, j: (b, j))
    blk1 = pl.BlockSpec((S, 2 * V_HEAD), lambda b, j: (b, j))
    return _pcall(
        body, name="mla_fwd", grid=(B, pairs),
        in_specs=[blk2, blk2, blk1], out_specs=[blk1, blk1],
        out_shape=[jax.ShapeDtypeStruct((T, pairs * 2 * V_HEAD), F32)] * 2,
        args=(q, k, v), exchange=exchange)


def _mla_bwd(q, k, v, do, o, lse, B, S, exchange=None):
    T = B * S
    tq = _pick(S, MLA_Q_TILES)
    nq = S // tq
    pairs = MLA_HEADS // 2

    def body(q_ref, k_ref, v_ref, do_ref, o_ref, lse_ref, dq_ref, dk_ref, dv_ref, dk_acc):
        causal = lax.broadcasted_iota(jnp.int32, (tq, tq), 1) <= lax.broadcasted_iota(jnp.int32, (tq, tq), 0)
        for half in range(2):
            lanes = slice(half * HEAD_BLOCK, (half + 1) * HEAD_BLOCK)
            col = half * V_HEAD
            own = _half_mask((tq, 2 * V_HEAD), half)
            for qi in reversed(range(nq)):
                rows = slice(qi * tq, (qi + 1) * tq)
                past = slice(0, qi * tq)
                first = qi == nq - 1
                qv = q_ref[rows, lanes]
                dov = jnp.where(own, do_ref[rows, :], 0.0)
                dsum = jnp.sum(dov * o_ref[rows, :], axis=-1, keepdims=True)
                dob = dov.astype(BF16)
                lse_col = lse_ref[rows, col:col + 1]

                def grads(keys, mask):
                    kv, vv = k_ref[keys, lanes], v_ref[keys, :]
                    p = jnp.exp(_dot_nt(qv, kv) * MLA_SCALE - lse_col)
                    if mask is not None:
                        p = jnp.where(mask, p, 0.0)
                    ds = (p * (_dot_nt(dob, vv) - dsum) * MLA_SCALE).astype(BF16)
                    dk, dv = _dot_tn(ds, qv), _dot_tn(p.astype(BF16), dob)
                    if first:
                        dk_acc[keys, :] = dk
                    else:
                        dk_acc[keys, :] += dk
                    if first and half == 0:
                        dv_ref[keys, :] = dv
                    else:
                        dv_ref[keys, :] += dv
                    return _dot(ds, kv)

                dq = grads(rows, causal)
                if qi:
                    dq = dq + grads(past, None)
                dq_ref[rows, lanes] = dq
            dk_ref[:, lanes] = dk_acc[...]

    blk2 = pl.BlockSpec((S, 2 * HEAD_BLOCK), lambda b, j: (b, j))
    blk1 = pl.BlockSpec((S, 2 * V_HEAD), lambda b, j: (b, j))
    return _pcall(
        body, name="mla_bwd", grid=(B, pairs),
        in_specs=[blk2, blk2, blk1, blk1, blk1, blk1], out_specs=[blk2, blk2, blk1],
        out_shape=[jax.ShapeDtypeStruct((T, MLA_HEADS * HEAD_BLOCK), F32), jax.ShapeDtypeStruct((T, MLA_HEADS * HEAD_BLOCK), F32),
                   jax.ShapeDtypeStruct((T, pairs * 2 * V_HEAD), F32)],
        scratch_shapes=[pltpu.VMEM((S, HEAD_BLOCK), F32)],
        args=(q, k, v, do, o, lse), exchange=exchange)


SWA_GROUP = SWA_HEADS // SWA_KV_HEADS
SWA_ROWS = SWA_GROUP * WINDOW


SWA_CHUNK = 8


def _bdot(a, b, ca, cb):
    return lax.dot_general(a, b, (((ca,), (cb,)), ((0,), (0,))), preferred_element_type=F32)


def _swa_chunks(nb):
    cb = next(c for c in (SWA_CHUNK, 4, 2, 1) if nb % c == 0)
    return cb, [(n0, slice(n0 * WINDOW, (n0 + cb) * WINDOW)) for n0 in range(0, nb, cb)]


def _swa_keys(ref, n0, cb):
    W = WINDOW
    cur = ref[n0 * W:(n0 + cb) * W, :]
    if n0:
        prev = ref[(n0 - 1) * W:(n0 + cb - 1) * W, :]
    else:
        prev = jnp.concatenate([cur[:W], cur[:(cb - 1) * W]], axis=0) if cb > 1 else cur
    return jnp.concatenate([prev.reshape(cb, W, W), cur.reshape(cb, W, W)], axis=1)


def _swa_stack(ref, rows, cb):
    return jnp.concatenate([ref[rows, j * WINDOW:(j + 1) * WINDOW].reshape(cb, WINDOW, WINDOW) for j in range(SWA_GROUP)], axis=1)


def _swa_unstack(low, high, cb):
    sel = _half_mask((cb, WINDOW, WINDOW), 0)
    return jnp.concatenate([jnp.where(sel, low[:, j * WINDOW:(j + 1) * WINDOW], high[:, j * WINDOW:(j + 1) * WINDOW])
                            .reshape(cb * WINDOW, WINDOW) for j in range(SWA_GROUP)], axis=1)


def _swa_sink_col(sink_ref, g):
    return jnp.concatenate([jnp.broadcast_to(sink_ref[g * SWA_GROUP + j:g * SWA_GROUP + j + 1, 0:1], (WINDOW, 1))
                            for j in range(SWA_GROUP)], axis=0)


def _swa_probs(qm, kcat, bias_g, sink, first_chunk):
    shape = (qm.shape[0], SWA_ROWS, 2 * WINDOW)
    qi = lax.broadcasted_iota(jnp.int32, shape, 1) % WINDOW
    col = lax.broadcasted_iota(jnp.int32, shape, 2)
    valid = (col > qi) & (col <= qi + WINDOW)
    if first_chunk:
        valid = valid & ((col >= WINDOW) | (lax.broadcasted_iota(jnp.int32, shape, 0) > 0))
    s = jnp.where(valid, _bdot(qm, kcat, 2, 2) * SWA_SCALE + bias_g, NEG)
    m = jnp.maximum(jnp.max(s, axis=-1, keepdims=True), sink)
    e = jnp.exp(s - m)
    e_s = jnp.exp(sink - m)
    inv = 1.0 / (jnp.sum(e, axis=-1, keepdims=True) + e_s)
    return e * inv, e_s * inv


def _swa_fwd(q, k, v, bias, sinks, B, S):
    T = B * S
    W = WINDOW
    nb = S // W

    cb, chunks = _swa_chunks(nb)

    def body(q_ref, k_ref, v_ref, bias_ref, sink_ref, o_ref):
        sink_cols = [_swa_sink_col(sink_ref, g) for g in range(SWA_KV_HEADS)]
        for n0, rows in chunks:
            kcat, vcat = _swa_keys(k_ref, n0, cb), _swa_keys(v_ref, n0, cb)
            qcat = _swa_stack(q_ref, rows, cb)
            outs = []
            for g in range(SWA_KV_HEADS):
                qm = jnp.where(_half_mask(qcat.shape, g), qcat, jnp.zeros_like(qcat))
                p, _ = _swa_probs(qm, kcat, bias_ref[g], sink_cols[g], n0 == 0)
                outs.append(_bdot(p.astype(BF16), vcat, 2, 1))
            o_ref[rows, :] = _swa_unstack(outs[0], outs[1], cb)

    SQ, SK = SWA_HEADS * SWA_HEAD_DIM, SWA_KV_HEADS * SWA_HEAD_DIM
    return pl.pallas_call(
        body, name="swa_fwd", grid=(B,),
        in_specs=[pl.BlockSpec((S, SQ), lambda b: (b, 0)), pl.BlockSpec((S, SK), lambda b: (b, 0)),
                  pl.BlockSpec((S, SK), lambda b: (b, 0)), _const_spec((SWA_KV_HEADS, SWA_ROWS, 2 * W)),
                  _const_spec((SWA_HEADS, 128))],
        out_specs=pl.BlockSpec((S, SQ), lambda b: (b, 0)),
        out_shape=jax.ShapeDtypeStruct((T, SQ), F32),
        compiler_params=_cparams(("arbitrary",)),
    )(q, k, v, bias, sinks)


def _swa_bwd(q, k, v, do, bias, sinks, B, S):
    T = B * S
    W = WINDOW
    nb = S // W
    cb, chunks = _swa_chunks(nb)

    def body(q_ref, k_ref, v_ref, do_ref, bias_ref, sink_ref, dq_ref, dk_ref, dv_ref, dbias_ref, dsink_ref):
        @pl.when(pl.program_id(0) == 0)
        def _():
            dbias_ref[...] = jnp.zeros_like(dbias_ref)
            dsink_ref[...] = jnp.zeros_like(dsink_ref)

        dk_ref[...] = jnp.zeros_like(dk_ref)
        dv_ref[...] = jnp.zeros_like(dv_ref)
        sink_cols = [_swa_sink_col(sink_ref, g) for g in range(SWA_KV_HEADS)]
        for n0, rows in chunks:
            kcat, vcat = _swa_keys(k_ref, n0, cb), _swa_keys(v_ref, n0, cb)
            qcat = _swa_stack(q_ref, rows, cb)
            docat = _swa_stack(do_ref, rows, cb)
            dk = jnp.zeros((cb, 2 * W, W), F32)
            dv = jnp.zeros((cb, 2 * W, W), F32)
            dqs = []
            for g in range(SWA_KV_HEADS):
                own = _half_mask(qcat.shape, g)
                qm = jnp.where(own, qcat, jnp.zeros_like(qcat))
                dom = jnp.where(own, docat, 0.0).astype(BF16)
                p, p_s = _swa_probs(qm, kcat, bias_ref[g], sink_cols[g], n0 == 0)
                dp = _bdot(dom, vcat, 2, 2)
                dsum = jnp.sum(p * dp, axis=-1, keepdims=True)
                ds = p * (dp - dsum)
                dbias_ref[g] += jnp.sum(ds, axis=0)
                dsink_ref[g] += jnp.broadcast_to(-jnp.sum(p_s * dsum, axis=0), (SWA_ROWS, 128))
                dsb = (ds * SWA_SCALE).astype(BF16)
                dqs.append(_bdot(dsb, kcat, 2, 1))
                dk = dk + _bdot(dsb, qm, 1, 1)
                dv = dv + _bdot(p.astype(BF16), dom, 1, 1)
            dq_ref[rows, :] = _swa_unstack(dqs[0], dqs[1], cb)
            dk_ref[rows, :] += dk[:, W:].reshape(cb * W, W)
            dv_ref[rows, :] += dv[:, W:].reshape(cb * W, W)
            dk_prev, dv_prev = dk[:, :W].reshape(cb * W, W), dv[:, :W].reshape(cb * W, W)
            if n0:
                before = slice((n0 - 1) * W, (n0 + cb - 1) * W)
                dk_ref[before, :] += dk_prev
                dv_ref[before, :] += dv_prev
            elif cb > 1:
                before = slice(0, (cb - 1) * W)
                dk_ref[before, :] += dk_prev[W:]
                dv_ref[before, :] += dv_prev[W:]

    SQ, SK = SWA_HEADS * SWA_HEAD_DIM, SWA_KV_HEADS * SWA_HEAD_DIM
    row = lambda w: pl.BlockSpec((S, w), lambda b: (b, 0))
    return pl.pallas_call(
        body, name="swa_bwd", grid=(B,),
        in_specs=[row(SQ), row(SK), row(SK), row(SQ), _const_spec((SWA_KV_HEADS, SWA_ROWS, 2 * W)), _const_spec((SWA_HEADS, 128))],
        out_specs=[row(SQ), row(SK), row(SK), _acc_spec((SWA_KV_HEADS, SWA_ROWS, 2 * W)), _acc_spec((SWA_KV_HEADS, SWA_ROWS, 128))],
        out_shape=[jax.ShapeDtypeStruct((T, SQ), F32), jax.ShapeDtypeStruct((T, SK), F32), jax.ShapeDtypeStruct((T, SK), F32),
                   jax.ShapeDtypeStruct((SWA_KV_HEADS, SWA_ROWS, 2 * W), F32), jax.ShapeDtypeStruct((SWA_KV_HEADS, SWA_ROWS, 128), F32)],
        compiler_params=_cparams(("arbitrary",)),
    )(q, k, v, do, bias, sinks)


def _bias_expand(rel_bias, bucket):
    W = WINDOW

    def body(rb_ref, bucket_ref, out_ref):
        bk = bucket_ref[...]
        for h in range(SWA_HEADS):
            def add(b, acc):
                return jnp.where(bk == b, rb_ref[b, h], acc)

            out_ref[h] = lax.fori_loop(0, REL_BUCKETS, add, jnp.zeros((W, 2 * W), F32))

    return pl.pallas_call(
        body, name="bias_expand",
        in_specs=[pl.BlockSpec(memory_space=pltpu.SMEM), pl.BlockSpec(memory_space=pltpu.VMEM)],
        out_specs=pl.BlockSpec(memory_space=pltpu.VMEM),
        out_shape=jax.ShapeDtypeStruct((SWA_HEADS, W, 2 * W), F32),
        compiler_params=_cparams(),
    )(rel_bias, bucket)


def _bias_reduce(dbias, dsink_rows, bucket):
    def body(db_ref, ds_ref, bucket_ref, out_ref, sink_out_ref, rows_ref):
        sink_lane = lax.broadcasted_iota(jnp.int32, (1, 128), 1)
        sink_acc = jnp.zeros((1, 128), F32)
        for h in range(SWA_HEADS):
            sink_acc = jnp.where(sink_lane == h, jnp.sum(ds_ref[h], axis=0, keepdims=True), sink_acc)
        sink_out_ref[...] = sink_acc
        bk = bucket_ref[...]
        for h in range(SWA_HEADS):
            dbh = db_ref[h]

            def add(b, carry):
                rows_ref[pl.ds(h * REL_BUCKETS + b, 1), :] = jnp.sum(jnp.where(bk == b, dbh, 0.0), axis=0, keepdims=True)
                return carry

            lax.fori_loop(0, REL_BUCKETS, add, 0)
        totals = jnp.sum(rows_ref[...], axis=-1, keepdims=True)
        lane = lax.broadcasted_iota(jnp.int32, (REL_BUCKETS, 128), 1)
        acc = jnp.zeros((REL_BUCKETS, 128), F32)
        for h in range(SWA_HEADS):
            acc = jnp.where(lane == h, totals[h * REL_BUCKETS:(h + 1) * REL_BUCKETS], acc)
        out_ref[...] = acc

    return pl.pallas_call(
        body, name="bias_reduce",
        out_shape=[jax.ShapeDtypeStruct((REL_BUCKETS, 128), F32), jax.ShapeDtypeStruct((1, 128), F32)],
        scratch_shapes=[pltpu.VMEM((SWA_HEADS * REL_BUCKETS, 2 * WINDOW), F32)],
        compiler_params=_cparams(),
    )(dbias, dsink_rows, bucket)


def _t5_bucket(dist):
    n = jnp.maximum(dist, 0)
    max_exact = REL_BUCKETS // 2
    nf = jnp.maximum(n, 1).astype(F32)
    large = max_exact + (jnp.log(nf / max_exact) / math.log(REL_MAX_DIST / max_exact) * (REL_BUCKETS - max_exact)).astype(jnp.int32)
    large = jnp.minimum(large, REL_BUCKETS - 1)
    return jnp.where(n < max_exact, n, large)


def _pair_heads(a, axis, inverse=False):
    shp = a.shape
    a = a.reshape(shp[:axis] + ((4, 2, 64) if inverse else (2, 4, 64)) + shp[axis + 1:])
    a = jnp.swapaxes(a, axis, axis + 1)
    return a.reshape(shp)


def _owner_blocks(w32, w16, me):
    r, c = w32.shape[0] // N_DEV, w32.shape[1]
    return lax.dynamic_slice(w32, (me * r, 0), (r, c)), w16.reshape(N_DEV, r, c)


_IN_WIDTHS = (Q_LORA, KV_LORA, QK_ROPE, SWA_HEADS * SWA_HEAD_DIM, SWA_KV_HEADS * SWA_HEAD_DIM, SWA_KV_HEADS * SWA_HEAD_DIM)


def _win_to_kernel(wt):
    parts, o = [], 0
    for wd in _IN_WIDTHS:
        parts.append(wt[o:o + wd])
        o += wd
    cq, ckv, kpe, qs, ks, vs = parts
    return jnp.concatenate([cq, ckv, _pair_heads(qs, 0), ks, vs, jnp.pad(kpe, ((64, 32), (0, 0)))], axis=0)


def _win_from_kernel(dwt):
    o = [0, Q_LORA, Q_LORA + KV_LORA, Q_LORA + KV_LORA + 512, Q_LORA + KV_LORA + 640, Q_LORA + KV_LORA + 768]
    cq, ckv, qs, ks, vs, kpe = [dwt[a:b] for a, b in zip(o, o[1:] + [dwt.shape[0]])]
    return jnp.concatenate([cq, ckv, kpe[64:96], _pair_heads(qs, 0, inverse=True), ks, vs], axis=0)


def _wq_to_kernel(wt):
    return jnp.pad(wt.reshape(MLA_HEADS, QK_NOPE + QK_ROPE, Q_LORA), ((0, 0), (0, 32), (0, 0))).reshape(MLA_HEADS * HEAD_BLOCK, Q_LORA)


def _wq_from_kernel(dwt):
    return dwt.reshape(MLA_HEADS, HEAD_BLOCK, Q_LORA)[:, :QK_NOPE + QK_ROPE].reshape(-1, Q_LORA)


def _wkv_to_kernel(wt):
    w3 = wt.reshape(MLA_HEADS, QK_NOPE + V_HEAD, KV_LORA)
    kpart = jnp.pad(w3[:, :QK_NOPE], ((0, 0), (0, 64), (0, 0))).reshape(MLA_HEADS * HEAD_BLOCK, KV_LORA)
    return jnp.concatenate([kpart, w3[:, QK_NOPE:].reshape(MLA_HEADS * V_HEAD, KV_LORA)], axis=0)


def _wkv_from_kernel(dwt):
    dk = dwt[:MLA_HEADS * HEAD_BLOCK].reshape(MLA_HEADS, HEAD_BLOCK, KV_LORA)[:, :QK_NOPE]
    dv = dwt[MLA_HEADS * HEAD_BLOCK:].reshape(MLA_HEADS, V_HEAD, KV_LORA)
    return jnp.concatenate([dk, dv], axis=1).reshape(-1, KV_LORA)


def _wo_to_kernel(w):
    return jnp.concatenate([w[:512], _pair_heads(w[512:], 0)], axis=0)


def _wo_from_kernel(dw):
    return jnp.concatenate([dw[:512], _pair_heads(dw[512:], 0, inverse=True)], axis=0)


def _kernel_weights(key, stored):
    if key in ("g1", "u1", "g2", "u2"):
        return {key + "t": stored}
    if key in ("d1", "d2"):
        return {key: stored}
    if key == "o":
        return {key: _wo_to_kernel(stored)}
    kt = _TO_KERNEL[key](stored)
    return {key: kt.T, key + "t": kt}


_SMALL = ("g_ffn1", "g_mix", "g_q_a", "g_kv_a", "attn_sinks", "rel_bias", "g_out_mla", "g_out_swa", "g_ffn2", "g_final")


class _Comm:
    def __init__(self, shards, me):
        self.shards, self.me = shards, me
        self.own32, self.recv16 = {}, {}

    def gather(self, keys):
        return [("gather", self.shards[k]) for k in keys]

    def gathered(self, keys, results, wts):
        for k, g in zip(keys, results):
            wts.update(_kernel_weights(k, g.reshape(N_DEV * g.shape[1], g.shape[2])))

    def scatter(self, keys, grads):
        jobs = []
        for k in keys:
            g32, g16 = (_FROM_KERNEL[k](g) if k in _FROM_KERNEL else g for g in grads[k])
            self.own32[k], blocks = _owner_blocks(g32, g16, self.me)
            jobs.append(("scatter", blocks))
        return jobs

    def scattered(self, keys, results):
        self.recv16.update(zip(keys, results))

    def scatter_last(self, key, grads):
        (_, blocks), = self.scatter((key,), grads)
        self.last_key = key
        self.last, self.token = _scatter_start(blocks, "grad_exchange_last_start")

    def finish_last(self, after):
        self.recv16[self.last_key] = _scatter_wait(self.last, after, "grad_exchange_last_wait")


def _step(x, target, gains, wts, comm):
    def jobs_in(keys):
        return comm.gather(keys) if comm else None

    def jobs_out(keys):
        return comm.scatter(keys, grads) if comm else None

    B, S, D = x.shape
    T = B * S
    x2 = x.reshape(T, D)
    t2 = target.reshape(T, D)

    pos = jnp.arange(S, dtype=F32)
    inv_freq = ROPE_THETA ** (-jnp.arange(0, QK_ROPE, 2, dtype=F32) / QK_ROPE)
    ang = pos[:, None] * inv_freq[None, :]
    cos, sin = jnp.cos(ang), jnp.sin(ang)
    ones, zeros = jnp.ones((S, 64), F32), jnp.zeros((S, 32), F32)
    rope_c = jnp.concatenate([ones, cos, cos, zeros], axis=1)
    rope_s = jnp.concatenate([0.0 * ones, -sin, sin, zeros], axis=1)

    qi = jnp.arange(WINDOW)[:, None]
    kj = jnp.arange(2 * WINDOW)[None, :]
    bucket = _t5_bucket(qi + WINDOW - kj).astype(jnp.int32)
    bias = _bias_expand(gains["rel_bias"], bucket).reshape(SWA_KV_HEADS, SWA_ROWS, 2 * WINDOW)
    sinks = jnp.broadcast_to(gains["attn_sinks"].reshape(SWA_HEADS, 1), (SWA_HEADS, 128))
    g_swa = _pair_heads(gains["g_out_swa"], 1)

    keys = ("in", "q", "kv", "o", "g2")
    (h1, n1, gate1, up1, act1), got = _ffn_fwd(x2, gains["g_ffn1"], wts["g1t"], wts["u1t"], wts["d1"], "ffn1_fwd", jobs_in(keys))
    if comm:
        comm.gathered(keys, got, wts)
    (u, c_q, c_kv, cqn, ckvn, q, k, v, qs, ks, vs) = _proj_fwd(
        h1, gains["g_mix"], wts["in"], gains["g_q_a"], wts["q"], gains["g_kv_a"], wts["kv"], rope_c, rope_s, S)
    keys = ("u2", "d2")
    (o_mla, lse), got = _mla_fwd(q, k, v, B, S, jobs_in(keys))
    if comm:
        comm.gathered(keys, got, wts)
    o_swa = _swa_fwd(qs, ks, vs, bias, sinks, B, S)
    h2, oc = _out_fwd(o_mla, o_swa, gains["g_out_mla"], g_swa, wts["o"], h1)
    (dh3, n2, gate2, up2, act2, loss, dg_final), _ = _ffn_fwd(
        h2, gains["g_ffn2"], wts["g2t"], wts["u2t"], wts["d2"], "ffn2_fwd", loss_head=(t2, gains["g_final"].reshape(1, D)))

    grads = {}
    (dh2, dg_ffn2, dgate2, dup2), _ = _ffn_bwd(h2, gains["g_ffn2"], gate2, up2, dh3, wts["g2t"], wts["u2t"], wts["d2"], "ffn2_bwd")
    grads["g2"], _ = _matmul_tn(dgate2, n2, "dw_gate2")
    grads["u2"], _ = _matmul_tn(dup2, n2, "dw_up2")
    grads["d2"], _ = _matmul_tn(act2, dh3, "dw_down2", 0.5)
    do_mla, do_swa, dg_mla, dg_swa = _out_bwd(dh2, o_mla, o_swa, gains["g_out_mla"], g_swa, wts["o"])
    grads["o"], _ = _matmul_tn(oc, dh2, "dw_o")
    keys = ("g2", "u2", "d2")
    (dq, dk, dv), got = _mla_bwd(q, k, v, do_mla, o_mla, lse, B, S, jobs_out(keys))
    if comm:
        comm.scattered(keys, got)
    dqs, dks, dvs, dbias, dsink = _swa_bwd(qs, ks, vs, do_swa, bias, sinks, B, S)
    dh1, dproj, dql, dkvc, dg_mix, dg_q, dg_kv = _proj_bwd(
        dq, dk, dv, dqs, dks, dvs, c_q, c_kv, h1, dh2, gains["g_mix"], wts["int"], gains["g_q_a"], wts["qt"],
        gains["g_kv_a"], wts["kvt"], rope_c, rope_s, S)
    grads["in"], _ = _matmul_tn(dproj, u, "dw_in")
    grads["q"], _ = _matmul_tn(dql, cqn, "dw_q")
    grads["kv"], _ = _matmul_tn(dkvc, ckvn, "dw_kv")
    grads["d1"], _ = _matmul_tn(act1, dh1, "dw_down1", 0.5)
    keys = ("o", "in", "q", "kv", "d1")
    (dx, dg_ffn1, dgate1, dup1), got = _ffn_bwd(
        x2, gains["g_ffn1"], gate1, up1, dh1, wts["g1t"], wts["u1t"], wts["d1"], "ffn1_bwd", jobs_out(keys))
    if comm:
        comm.scattered(keys, got)
    grads["g1"], _ = _matmul_tn(dgate1, n1, "dw_gate1")
    grads["u1"], got = _matmul_tn(dup1, n1, "dw_up1", exchange=jobs_out(("g1",)))
    if comm:
        comm.scattered(("g1",), got)
        comm.scatter_last("u1", grads)

    d_rel_bias, d_sinks = _bias_reduce(dbias.reshape(SWA_HEADS, WINDOW, 2 * WINDOW), dsink.reshape(SWA_HEADS, WINDOW, 128), bucket)
    small = {
        "g_ffn1": dg_ffn1, "g_mix": dg_mix, "g_q_a": dg_q, "g_kv_a": dg_kv, "attn_sinks": d_sinks,
        "rel_bias": d_rel_bias, "g_out_mla": dg_mla, "g_out_swa": _pair_heads(dg_swa, 1, inverse=True),
        "g_ffn2": dg_ffn2, "g_final": dg_final, "loss": loss,
    }
    return dx.reshape(B, S, D), grads, small


_WEIGHTS = ("g_ffn1", "w_ffn1_gate", "w_ffn1_up", "w_ffn1_down", "g_mix", "w_in", "g_q_a", "w_q_b", "g_kv_a", "w_kv_b",
            "attn_sinks", "rel_bias", "g_out_mla", "g_out_swa", "w_o", "g_ffn2", "w_ffn2_gate", "w_ffn2_up", "w_ffn2_down",
            "g_final")
_BIG = (("w_ffn1_gate", "g1", True), ("w_ffn1_up", "u1", True), ("w_ffn1_down", "d1", False), ("w_in", "in", True),
        ("w_q_b", "q", True), ("w_kv_b", "kv", True), ("w_o", "o", False), ("w_ffn2_gate", "g2", True),
        ("w_ffn2_up", "u2", True), ("w_ffn2_down", "d2", False))
_TO_KERNEL = {"in": _win_to_kernel, "q": _wq_to_kernel, "kv": _wkv_to_kernel}
_FROM_KERNEL = {"in": _win_from_kernel, "q": _wq_from_kernel, "kv": _wkv_from_kernel, "o": _wo_from_kernel}


def kernel(x, g_ffn1, w_ffn1_gate, w_ffn1_up, w_ffn1_down, g_mix, w_in, g_q_a, w_q_b, g_kv_a, w_kv_b, attn_sinks, rel_bias, g_out_mla, g_out_swa, w_o, g_ffn2, w_ffn2_gate, w_ffn2_up, w_ffn2_down, g_final, loss_target, m_g_ffn1, m_w_ffn1_gate, m_w_ffn1_up, m_w_ffn1_down, m_g_mix, m_w_in, m_g_q_a, m_w_q_b, m_g_kv_a, m_w_kv_b, m_attn_sinks, m_rel_bias, m_g_out_mla, m_g_out_swa, m_w_o, m_g_ffn2, m_w_ffn2_gate, m_w_ffn2_up, m_w_ffn2_down, m_g_final, v_g_ffn1, v_w_ffn1_gate, v_w_ffn1_up, v_w_ffn1_down, v_g_mix, v_w_in, v_g_q_a, v_w_q_b, v_g_kv_a, v_w_kv_b, v_attn_sinks, v_rel_bias, v_g_out_mla, v_g_out_swa, v_w_o, v_g_ffn2, v_w_ffn2_gate, v_w_ffn2_up, v_w_ffn2_down, v_g_final):
    w = dict(zip(_WEIGHTS, (g_ffn1, w_ffn1_gate, w_ffn1_up, w_ffn1_down, g_mix, w_in, g_q_a, w_q_b, g_kv_a, w_kv_b, attn_sinks,
                            rel_bias, g_out_mla, g_out_swa, w_o, g_ffn2, w_ffn2_gate, w_ffn2_up, w_ffn2_down, g_final)))
    m = dict(zip(_WEIGHTS, (m_g_ffn1, m_w_ffn1_gate, m_w_ffn1_up, m_w_ffn1_down, m_g_mix, m_w_in, m_g_q_a, m_w_q_b, m_g_kv_a,
                            m_w_kv_b, m_attn_sinks, m_rel_bias, m_g_out_mla, m_g_out_swa, m_w_o, m_g_ffn2, m_w_ffn2_gate,
                            m_w_ffn2_up, m_w_ffn2_down, m_g_final)))
    v = dict(zip(_WEIGHTS, (v_g_ffn1, v_w_ffn1_gate, v_w_ffn1_up, v_w_ffn1_down, v_g_mix, v_w_in, v_g_q_a, v_w_q_b, v_g_kv_a,
                            v_w_kv_b, v_attn_sinks, v_rel_bias, v_g_out_mla, v_g_out_swa, v_w_o, v_g_ffn2, v_w_ffn2_gate,
                            v_w_ffn2_up, v_w_ffn2_down, v_g_final)))
    me = 4 * lax.axis_index("x") + 2 * lax.axis_index("y") + lax.axis_index("c")

    stored = lambda src, name, by_col: src[name][0].T if by_col else src[name][0]
    shards = dict(zip([key for _, key, _ in _BIG], _cast_bf16([stored(w, name, by_col) for name, _, by_col in _BIG])))
    comm = _Comm(shards, me)
    first = ("g1", "u1", "d1")
    wts = {}
    comm.gathered(first, _all_gather([shards[k] for k in first], "weights_all_gather"), wts)

    gains = {n: (w[n] if n == "rel_bias" else w[n].reshape(1, -1)) for n in _SMALL}
    grad_x, _, small = _step(x, loss_target, gains, wts, comm)

    small_gathered = _all_gather([small[n] for n in _SMALL] + [small["loss"]], "small_all_gather")

    out_g, out_d, out_m, out_v = {}, {}, {}, {}
    me1 = me.astype(jnp.int32).reshape(1)

    def update(name, key, by_col):
        updates = _adam_big(stored(w, name, by_col), stored(m, name, by_col), stored(v, name, by_col),
                            comm.own32[key], comm.recv16[key], me1, comm.token)
        out_g[name], out_d[name], out_m[name], out_v[name] = ((a.T if by_col else a)[None] for a in updates)

    for name, key, by_col in _BIG:
        if key != comm.last_key:
            update(name, key, by_col)
    as_2d = lambda src: [src[n] if src[n].ndim == 2 else src[n].reshape(1, -1) for n in _SMALL]
    loss, updates = _adam_small(as_2d(w), as_2d(m), as_2d(v), small_gathered[:-1], small_gathered[-1])
    for n, (g, d, mn, vn) in zip(_SMALL, updates):
        out_g[n], out_d[n], out_m[n], out_v[n] = (a.reshape(w[n].shape) for a in (g, d, mn, vn))
    comm.finish_last(after=loss)
    for name, key, by_col in _BIG:
        if key == comm.last_key:
            update(name, key, by_col)

    return (loss[0, 0], grad_x, *[out_g[n] for n in _WEIGHTS], *[out_d[n] for n in _WEIGHTS],
            *[out_m[n] for n in _WEIGHTS], *[out_v[n] for n in _WEIGHTS])
```

```python
import functools
import math

import jax
import jax.numpy as jnp
from jax import lax
from jax.experimental import pallas as pl
from jax.experimental.pallas import tpu as pltpu

F32 = jnp.float32
BF16 = jnp.bfloat16
MESH = pl.DeviceIdType.MESH

EPS = 1e-6
NEG = -1e30
N_DEV = 8

MLA_HEADS = 8
Q_LORA = 256
KV_LORA = 128
QK_NOPE = 64
QK_ROPE = 32
V_HEAD = 64
ROPE_THETA = 10000.0
SWA_HEADS = 8
SWA_KV_HEADS = 2
SWA_HEAD_DIM = 64
WINDOW = 128
REL_BUCKETS = 32
REL_MAX_DIST = 128
HEAD_BLOCK = 128
MLA_Q_TILES = (512, 256, 128)
MLA_SCALE = (QK_NOPE + QK_ROPE) ** -0.5
SWA_SCALE = SWA_HEAD_DIM ** -0.5

ADAM_LR = 0.001
ADAM_B1 = 0.9
ADAM_B2 = 0.999
ADAM_EPS = 1e-08
ADAM_WD = 0.01
ADAM_STEP = 10

VMEM_LIMIT = 56 * 1024 * 1024


def _cparams(semantics=None):
    return pltpu.CompilerParams(dimension_semantics=semantics, vmem_limit_bytes=VMEM_LIMIT)


def _dot(a, b):
    return jnp.dot(a, b, preferred_element_type=F32)


def _dot_nt(a, b):
    return lax.dot_general(a, b, (((1,), (1,)), ((), ())), preferred_element_type=F32)


def _dot_tn(a, b):
    return lax.dot_general(a, b, (((0,), (0,)), ((), ())), preferred_element_type=F32)


def _rms_fwd(x, g):
    r = lax.rsqrt(jnp.mean(x * x, axis=-1, keepdims=True) + EPS)
    return x * r * g, r


def _rms_bwd(x, g, r, dy):
    xh = x * r
    dyg = dy * g
    dx = r * (dyg - xh * jnp.mean(dyg * xh, axis=-1, keepdims=True))
    return dx, jnp.sum(dy * xh, axis=0, keepdims=True)


def _swap16(x):
    n = x.shape[-1]
    up = pltpu.roll(x, n - 16, 1)
    down = pltpu.roll(x, 16, 1)
    lane = lax.broadcasted_iota(jnp.int32, x.shape, 1) % HEAD_BLOCK
    return jnp.where((lane >= 64) & (lane < 80), up, jnp.where((lane >= 80) & (lane < 96), down, 0.0))


def _const_spec(shape):
    return pl.BlockSpec(shape, lambda *_: (0,) * len(shape), pipeline_mode=pl.Buffered(1))


def _acc_spec(shape):
    return pl.BlockSpec(shape, lambda *_: (0,) * len(shape))


def _row_spec(tm, width):
    return pl.BlockSpec((tm, width), lambda i: (i, 0))


def _cast_bf16(arrays):
    n = len(arrays)

    def body(*refs):
        for src, dst in zip(refs[:n], refs[n:]):
            dst[...] = src[...].astype(BF16)

    return pl.pallas_call(
        body, name="cast_weights",
        out_shape=[jax.ShapeDtypeStruct(a.shape, BF16) for a in arrays],
        compiler_params=_cparams(),
    )(*arrays)


def _all_gather(arrays, name):
    n = len(arrays)

    def body(*refs):
        ins, outs = refs[:n], refs[n:2 * n]
        send_sems, recv_sems, local_sems = refs[2 * n:]
        x, y, c = lax.axis_index("x"), lax.axis_index("y"), lax.axis_index("c")
        me, sibling = (x, y, c), (x, y, 1 - c)
        chips = [(1 - x, y), (x, 1 - y), (1 - x, 1 - y)]

        def slot(a, dev):
            return outs[a].at[4 * dev[0] + 2 * dev[1] + dev[2]]

        def copy(a, k, block, to, src=None):
            return pltpu.make_async_remote_copy(
                src_ref=slot(a, block) if src is None else src, dst_ref=slot(a, block),
                send_sem=send_sems.at[a, k], recv_sem=recv_sems.at[a, k], device_id=to, device_id_type=MESH)

        mine = [pltpu.make_async_copy(ins[a], slot(a, me), local_sems.at[a]) for a in range(n)]
        for cp in mine:
            cp.start()
        first = []
        for a in range(n):
            first.append(copy(a, 0, me, sibling, src=ins[a]))
            first += [copy(a, 1 + j, me, (*chip, c), src=ins[a]) for j, chip in enumerate(chips)]
        for cp in first:
            cp.start()
        passed = []
        for j, chip in enumerate(chips):
            for a in range(n):
                copy(a, 1 + j, (*chip, c), me).wait_recv()
                cp = copy(a, 4 + j, (*chip, c), sibling)
                cp.start()
                passed.append(cp)
        for a in range(n):
            copy(a, 0, sibling, me).wait_recv()
            for j, chip in enumerate(chips):
                copy(a, 4 + j, (*chip, 1 - c), me).wait_recv()
        for cp in first + passed:
            cp.wait_send()
        for cp in mine:
            cp.wait()

    any_spec = pl.BlockSpec(memory_space=pl.ANY)
    return pl.pallas_call(
        body, name=name,
        out_shape=[jax.ShapeDtypeStruct((N_DEV,) + a.shape, a.dtype) for a in arrays],
        in_specs=[any_spec] * n, out_specs=[any_spec] * n,
        scratch_shapes=[pltpu.SemaphoreType.DMA((n, 7)), pltpu.SemaphoreType.DMA((n, 7)), pltpu.SemaphoreType.DMA((n,))],
    )(*arrays)


def _exchange_shapes(jobs):
    dsts = [jax.ShapeDtypeStruct(((N_DEV,) + a.shape) if kind == "gather" else a.shape, a.dtype) for kind, a in jobs]
    n = len(jobs)
    sems = [pltpu.SemaphoreType.DMA((n, N_DEV - 1)), pltpu.SemaphoreType.DMA((n, N_DEV - 1)), pltpu.SemaphoreType.DMA((n,))]
    return dsts, sems


def _exchange_copies(kinds, srcs, dsts, send_sems, recv_sems, local_sems):
    x, y, c = lax.axis_index("x"), lax.axis_index("y"), lax.axis_index("c")
    me = 4 * x + 2 * y + c
    remote, local = [], []
    for a, kind in enumerate(kinds):
        for k in range(1, N_DEV):
            peer = (1 - x if k & 4 else x, 1 - y if k & 2 else y, 1 - c if k & 1 else c)
            src = srcs[a] if kind == "gather" else srcs[a].at[4 * peer[0] + 2 * peer[1] + peer[2]]
            remote.append(pltpu.make_async_remote_copy(
                src_ref=src, dst_ref=dsts[a].at[me], send_sem=send_sems.at[a, k - 1], recv_sem=recv_sems.at[a, k - 1],
                device_id=peer, device_id_type=MESH))
        src = srcs[a] if kind == "gather" else srcs[a].at[me]
        local.append(pltpu.make_async_copy(src, dsts[a].at[me], local_sems.at[a]))
    return remote, local


def _scatter_copies(srcs, lands, send_sems, recv_sems):
    x, y, c = lax.axis_index("x"), lax.axis_index("y"), lax.axis_index("c")
    me = 4 * x + 2 * y + c
    copies = []
    for a, (src_ref, land_ref) in enumerate(zip(srcs, lands)):
        for k in range(1, N_DEV):
            peer = (1 - x if k & 4 else x, 1 - y if k & 2 else y, 1 - c if k & 1 else c)
            copies.append(pltpu.make_async_remote_copy(
                src_ref=src_ref.at[4 * peer[0] + 2 * peer[1] + peer[2]], dst_ref=land_ref.at[me],
                send_sem=send_sems.at[a * (N_DEV - 1) + k - 1], recv_sem=recv_sems.at[a * (N_DEV - 1) + k - 1],
                device_id=peer, device_id_type=MESH))
    return copies


_SPLIT_EFFECT = pltpu.SideEffectType.DATAFLOW_SIDE_EFFECTING


def _scatter_start(blocks, name):
    n = len(blocks)
    hbm = pl.BlockSpec(memory_space=pltpu.HBM)
    sem = pl.BlockSpec(memory_space=pltpu.SEMAPHORE)

    def body(*refs):
        srcs, lands, send_sems, recv_sems, token = refs[:n], refs[n:2 * n], refs[2 * n], refs[2 * n + 1], refs[-1]
        for cp in _scatter_copies(srcs, lands, send_sems, recv_sems):
            cp.start()
        token[...] = jnp.zeros_like(token)

    through = [pltpu.HBM(b.shape, b.dtype) for b in blocks]
    *handle, token = pl.pallas_call(
        body, name=name,
        out_shape=(pltpu.SemaphoreType.DMA((n * (N_DEV - 1),)), pltpu.SemaphoreType.DMA((n * (N_DEV - 1),)), *through, *through,
                   jax.ShapeDtypeStruct((8, 128), F32)),
        in_specs=(hbm,) * (2 * n), out_specs=(sem, sem) + (hbm,) * (2 * n) + (pl.BlockSpec(memory_space=pltpu.VMEM),),
        input_output_aliases={i: 2 + i for i in range(2 * n)},
        compiler_params=pltpu.CompilerParams(has_side_effects=_SPLIT_EFFECT),
    )(*[pltpu.with_memory_space_constraint(b, pltpu.HBM) for b in blocks],
      *[pltpu.with_memory_space_constraint(lax.empty(b.shape, b.dtype), pltpu.HBM) for b in blocks])
    return handle, token


def _scatter_wait(handle, after, name):
    hbm = pl.BlockSpec(memory_space=pltpu.HBM)
    sem = pl.BlockSpec(memory_space=pltpu.SEMAPHORE)
    send_sems, recv_sems, *through = handle
    n = len(through) // 2

    def body(*refs):
        for cp in _scatter_copies(refs[:n], refs[n:2 * n], refs[2 * n], refs[2 * n + 1]):
            cp.wait_send()
            cp.wait_recv()

    outs = pl.pallas_call(
        body, name=name, out_shape=tuple(pltpu.HBM(t.shape, t.dtype) for t in through),
        in_specs=(hbm,) * (2 * n) + (sem, sem, pl.BlockSpec(memory_space=pl.ANY)), out_specs=(hbm,) * (2 * n),
        input_output_aliases={i: i for i in range(2 * n)},
        compiler_params=pltpu.CompilerParams(has_side_effects=_SPLIT_EFFECT),
    )(*through, send_sems, recv_sems, after)
    return list(outs[n:])


def _pcall(body, *, name, grid, in_specs, out_specs, out_shape, args, scratch_shapes=(), exchange=None, after=None):
    semantics = ("arbitrary",) * len(grid)
    exchange = exchange or []
    extra = [] if after is None else [after]
    if not exchange and not extra:
        outs = pl.pallas_call(body, name=name, grid=grid, in_specs=list(in_specs), out_specs=list(out_specs),
                              out_shape=list(out_shape), scratch_shapes=list(scratch_shapes),
                              compiler_params=_cparams(semantics))(*args)
        return list(outs), []
    kinds = [kind for kind, _ in exchange]
    n_in, n_out, n_job = len(in_specs), len(out_specs), len(exchange)
    dst_shapes, sems = _exchange_shapes(exchange) if exchange else ([], [])

    def wrapped(*refs):
        ins, srcs = refs[:n_in], refs[n_in:n_in + n_job]
        o0 = n_in + n_job + len(extra)
        outs, dsts, rest = refs[o0:o0 + n_out], refs[o0 + n_out:o0 + n_out + n_job], refs[o0 + n_out + n_job:]
        if not exchange:
            body(*ins, *outs, *rest)
            return
        scratch, sem_refs = rest[:-3], rest[-3:]
        first = functools.reduce(jnp.logical_and, [pl.program_id(d) == 0 for d in range(len(grid))])
        last = functools.reduce(jnp.logical_and, [pl.program_id(d) == grid[d] - 1 for d in range(len(grid))])

        @pl.when(first)
        def _():
            remote, local = _exchange_copies(kinds, srcs, dsts, *sem_refs)
            for cp in local + remote:
                cp.start()

        body(*ins, *outs, *scratch)

        @pl.when(last)
        def _():
            remote, local = _exchange_copies(kinds, srcs, dsts, *sem_refs)
            for cp in remote + local:
                cp.wait()

    any_spec = pl.BlockSpec(memory_space=pl.ANY)
    outs = pl.pallas_call(
        wrapped, name=name, grid=grid, in_specs=list(in_specs) + [any_spec] * (n_job + len(extra)),
        out_specs=list(out_specs) + [any_spec] * n_job, out_shape=list(out_shape) + dst_shapes,
        scratch_shapes=list(scratch_shapes) + sems, compiler_params=_cparams(semantics),
    )(*args, *[a for _, a in exchange], *extra)
    return list(outs[:n_out]), list(outs[n_out:])


def _adam_math(w, g, m, v):
    m = ADAM_B1 * m + (1.0 - ADAM_B1) * g
    v = ADAM_B2 * v + (1.0 - ADAM_B2) * (g * g)
    m_hat = m / (1.0 - ADAM_B1 ** ADAM_STEP)
    v_hat = v / (1.0 - ADAM_B2 ** ADAM_STEP)
    delta = -ADAM_LR * (m_hat / (jnp.sqrt(v_hat) + ADAM_EPS) + ADAM_WD * w)
    return delta, m, v


def _adam_big(w, m, v, own32, recv16, me, after):
    r, c = w.shape
    tr = r if r <= 512 else (512 if r % 512 == 0 else 352)
    assert r % tr == 0

    def body(me_ref, w_ref, m_ref, v_ref, s_ref, r_ref, after_ref, g_out, d_out, m_out, v_out):
        g = s_ref[...]
        for k in range(1, N_DEV):
            g = g + r_ref[jnp.bitwise_xor(me_ref[0], k)].astype(F32)
        d, mn, vn = _adam_math(w_ref[...], g, m_ref[...], v_ref[...])
        g_out[...] = g
        d_out[...] = d
        m_out[...] = mn
        v_out[...] = vn

    blk = pl.BlockSpec((tr, c), lambda i, s: (i, 0))
    grid_spec = pltpu.PrefetchScalarGridSpec(
        num_scalar_prefetch=1, grid=(r // tr,),
        in_specs=[blk, blk, blk, blk, pl.BlockSpec((N_DEV, tr, c), lambda i, s: (0, i, 0)), pl.BlockSpec(memory_space=pl.ANY)],
        out_specs=[blk, blk, blk, blk])
    return pl.pallas_call(
        body, name="adamw_big", grid_spec=grid_spec,
        out_shape=[jax.ShapeDtypeStruct((r, c), F32)] * 4,
        compiler_params=_cparams(("arbitrary",)),
    )(me, w, m, v, own32, recv16, after)


def _adam_small(ws, ms, vs, partials, loss_parts):
    n = len(ws)

    def body(*refs):
        w_refs, m_refs, v_refs, p_refs = (refs[i * n:(i + 1) * n] for i in range(4))
        l_ref, loss_out = refs[4 * n], refs[4 * n + 1]
        outs = refs[4 * n + 2:]

        def total(ref):
            s = ref[0]
            for dev in range(1, N_DEV):
                s = s + ref[dev]
            return s

        loss_out[...] = total(l_ref)
        for i in range(n):
            r, c = w_refs[i].shape
            g = total(p_refs[i])[:, :c]
            d, mn, vn = _adam_math(w_refs[i][...], g, m_refs[i][...], v_refs[i][...])
            for j, val in enumerate((g, d, mn, vn)):
                outs[4 * i + j][...] = val

    shapes = [jax.ShapeDtypeStruct((1, 128), F32)]
    for w in ws:
        shapes += [jax.ShapeDtypeStruct(w.shape, F32)] * 4
    outs = pl.pallas_call(body, name="adamw_small", out_shape=shapes, compiler_params=_cparams())(*ws, *ms, *vs, *partials, loss_parts)
    return outs[0], [outs[1 + 4 * i:5 + 4 * i] for i in range(n)]


def _pick(n, prefer):
    for t in prefer:
        if n % t == 0:
            return t
    return n


def _ffn_fwd(x, g, wg, wu, wd, name, exchange=None, loss_head=None):
    T, D = x.shape
    F = wg.shape[0]
    tm = _pick(T, (512, 256, 128))
    fc = _pick(F, (256, 128))

    def body(x_ref, g_ref, wg_ref, wu_ref, wd_ref, *rest):
        h_ref, n_ref, gate_ref, up_ref, act_ref = rest[-7:-2] if loss_head else rest
        xv = x_ref[...]
        n, _ = _rms_fwd(xv, g_ref[...])
        nb = n.astype(BF16)
        n_ref[...] = nb
        acc = jnp.zeros((tm, D), F32)
        for f0 in range(0, F, fc):
            gate = _dot_nt(nb, wg_ref[f0:f0 + fc, :])
            up = _dot_nt(nb, wu_ref[f0:f0 + fc, :])
            gate_ref[:, f0:f0 + fc] = gate.astype(BF16)
            up_ref[:, f0:f0 + fc] = up.astype(BF16)
            act = (gate * (1.0 / (1.0 + jnp.exp(-gate))) * up).astype(BF16)
            act_ref[:, f0:f0 + fc] = act
            acc = acc + _dot(act, wd_ref[f0:f0 + fc, :])
        hv = xv + 0.5 * acc
        if not loss_head:
            h_ref[...] = hv
            return
        t_ref, gf_ref, loss_ref, dgf_ref = rest[0], rest[1], rest[-2], rest[-1]
        gf = gf_ref[...]
        y, r = _rms_fwd(hv, gf)
        diff = y - t_ref[...]
        dh, dgp = _rms_bwd(hv, gf, r, diff * (1.0 / D))
        h_ref[...] = dh
        part = 0.5 * jnp.sum(jnp.sum(diff * diff, axis=-1, keepdims=True) * (1.0 / D), axis=0, keepdims=True)

        @pl.when(pl.program_id(0) == 0)
        def _():
            loss_ref[...] = jnp.zeros_like(loss_ref)
            dgf_ref[...] = jnp.zeros_like(dgf_ref)

        loss_ref[...] += jnp.broadcast_to(part, loss_ref.shape)
        dgf_ref[...] += dgp

    in_specs = [_row_spec(tm, D), _const_spec((1, D)), _const_spec((F, D)), _const_spec((F, D)), _const_spec((F, D))]
    out_specs = [_row_spec(tm, D), _row_spec(tm, D), _row_spec(tm, F), _row_spec(tm, F), _row_spec(tm, F)]
    out_shape = [jax.ShapeDtypeStruct((T, D), F32), jax.ShapeDtypeStruct((T, D), BF16),
                 jax.ShapeDtypeStruct((T, F), BF16), jax.ShapeDtypeStruct((T, F), BF16), jax.ShapeDtypeStruct((T, F), BF16)]
    args = (x, g, wg, wu, wd)
    if loss_head:
        in_specs += [_row_spec(tm, D), _const_spec((1, D))]
        out_specs += [_acc_spec((1, 128)), _acc_spec((1, D))]
        out_shape += [jax.ShapeDtypeStruct((1, 128), F32), jax.ShapeDtypeStruct((1, D), F32)]
        args += tuple(loss_head)
    return _pcall(body, name=name, grid=(T // tm,), in_specs=in_specs, out_specs=out_specs, out_shape=out_shape, args=args,
                  exchange=exchange)


FFN_BWD_CHUNK = 1408
FFN_BWD_GROUP = 1408


def _ffn_bwd(x, g, gate, up, dh, wgt, wut, wdt, name, exchange=None, after=None):
    T, D = x.shape
    F = wgt.shape[0]
    tm = _pick(T, (512, 256, 128))
    halves = 2
    Fh = F // halves
    assert Fh * halves == F and Fh % 128 == 0
    groups = [(g0, min(g0 + FFN_BWD_GROUP, Fh)) for g0 in range(0, Fh, FFN_BWD_GROUP)]

    def body(x_ref, g_ref, gate_ref, up_ref, dh_ref, wg_ref, wu_ref, wd_ref, dx_ref, dg_ref, dgate_ref, dup_ref, dn_ref):
        i, h = pl.program_id(0), pl.program_id(1)
        dhv = dh_ref[...]
        dho = (0.5 * dhv).astype(BF16)
        dn = jnp.zeros((tm, D), F32)
        for g0, g1 in groups:
            for f0 in range(g0, g1, FFN_BWD_CHUNK):
                f1 = min(f0 + FFN_BWD_CHUNK, g1)
                d_act = _dot_nt(dho, wd_ref[f0:f1, :])
                gt = gate_ref[:, f0:f1].astype(F32)
                u = up_ref[:, f0:f1].astype(F32)
                sg = 1.0 / (1.0 + jnp.exp(-gt))
                silu = gt * sg
                dup_ref[:, f0:f1] = (d_act * silu).astype(BF16)
                dgate_ref[:, f0:f1] = (d_act * u * (sg + silu * (1.0 - sg))).astype(BF16)
            dn = dn + _dot(dgate_ref[:, g0:g1], wg_ref[g0:g1, :]) + _dot(dup_ref[:, g0:g1], wu_ref[g0:g1, :])

        @pl.when(h == 0)
        def _():
            dn_ref[...] = dn

        @pl.when(h > 0)
        def _():
            dn_ref[...] += dn

        @pl.when((i == 0) & (h == 0))
        def _():
            dg_ref[...] = jnp.zeros_like(dg_ref)

        @pl.when(h == halves - 1)
        def _():
            xv, gv = x_ref[...], g_ref[...]
            _, r = _rms_fwd(xv, gv)
            dxn, dgp = _rms_bwd(xv, gv, r, dn_ref[...])
            dx_ref[...] = dhv + dxn
            dg_ref[...] += dgp

    tile = pl.BlockSpec((tm, D), lambda i, h: (i, 0))
    wide = pl.BlockSpec((tm, Fh), lambda i, h: (i, h))
    return _pcall(
        body, name=name, grid=(T // tm, halves),
        in_specs=[tile, _const_spec((1, D)), wide, wide, tile,
                  pl.BlockSpec((Fh, D), lambda i, h: (h, 0)), pl.BlockSpec((Fh, D), lambda i, h: (h, 0)),
                  pl.BlockSpec((Fh, D), lambda i, h: (h, 0))],
        out_specs=[tile, _acc_spec((1, D)), wide, wide],
        out_shape=[jax.ShapeDtypeStruct((T, D), F32), jax.ShapeDtypeStruct((1, D), F32),
                   jax.ShapeDtypeStruct((T, F), BF16), jax.ShapeDtypeStruct((T, F), BF16)],
        scratch_shapes=[pltpu.VMEM((tm, D), F32)],
        args=(x, g, gate, up, dh, wgt, wut, wdt), exchange=exchange, after=after)


def _matmul_tn(a, b, name, scale=None, exchange=None, after=None):
    T, K = a.shape
    N = b.shape[1]
    tk = _pick(K, (1024, 1408, 1280, 768, 512, 256, 128))
    tn = _pick(N, (1024, 1408, 1280, 768, 512, 256, 128))
    tt = _pick(T, (2048, 1024, 512, 256, 128))
    n_t = T // tt

    def body(a_ref, b_ref, o_ref, o16_ref):
        @pl.when(pl.program_id(2) == 0)
        def _():
            o_ref[...] = jnp.zeros_like(o_ref)

        o_ref[...] += _dot_tn(a_ref[...].astype(BF16), b_ref[...].astype(BF16))

        @pl.when(pl.program_id(2) == n_t - 1)
        def _():
            if scale is not None:
                o_ref[...] = o_ref[...] * scale
            o16_ref[...] = o_ref[...].astype(BF16)

    tile = pl.BlockSpec((tk, tn), lambda i, j, t: (i, j))
    outs, got = _pcall(
        body, name=name, grid=(K // tk, N // tn, n_t),
        in_specs=[pl.BlockSpec((tt, tk), lambda i, j, t: (t, i)), pl.BlockSpec((tt, tn), lambda i, j, t: (t, j))],
        out_specs=[tile, tile],
        out_shape=[jax.ShapeDtypeStruct((K, N), F32), jax.ShapeDtypeStruct((K, N), BF16)],
        args=(a, b), exchange=exchange, after=after)
    return tuple(outs), got


def _rope_spec(tm, S):
    per = S // tm
    return pl.BlockSpec((tm, HEAD_BLOCK), lambda i: (i % per, 0))


def _proj_fwd(h1, g_mix, w_in, g_q, wq, g_kv, wkv, rope_c, rope_s, S):
    T, D = h1.shape
    P = w_in.shape[1]
    tm = _pick(S, (512, 256, 128))
    QW = MLA_HEADS * HEAD_BLOCK
    VW = MLA_HEADS * V_HEAD
    SQ = SWA_HEADS * SWA_HEAD_DIM
    SK = SWA_KV_HEADS * SWA_HEAD_DIM
    o_cq, o_ckv, o_qs, o_ks, o_vs, o_kpe = 0, Q_LORA, Q_LORA + KV_LORA, Q_LORA + KV_LORA + SQ, Q_LORA + KV_LORA + SQ + SK, Q_LORA + KV_LORA + SQ + 2 * SK
    assert P == o_kpe + HEAD_BLOCK

    def body(h_ref, gm_ref, win_ref, gq_ref, wq_ref, gkv_ref, wkv_ref, c_ref, s_ref,
             u_ref, cq_ref, ckv_ref, cqn_ref, ckvn_ref, q_ref, k_ref, v_ref, qs_ref, ks_ref, vs_ref):
        u, _ = _rms_fwd(h_ref[...], gm_ref[...])
        ub = u.astype(BF16)
        u_ref[...] = ub
        proj = _dot(ub, win_ref[...])
        c_q = proj[:, o_cq:o_ckv]
        c_kv = proj[:, o_ckv:o_qs]
        cq_ref[...] = c_q
        ckv_ref[...] = c_kv
        qs_ref[...] = proj[:, o_qs:o_ks].astype(BF16)
        ks_ref[...] = proj[:, o_ks:o_vs].astype(BF16)
        vs_ref[...] = proj[:, o_vs:o_kpe].astype(BF16)
        cb, sb = c_ref[...], s_ref[...]
        kpe = proj[:, o_kpe:P]
        kpe = kpe * cb + _swap16(kpe) * sb
        cqn, _ = _rms_fwd(c_q, gq_ref[...])
        cqn = cqn.astype(BF16)
        cqn_ref[...] = cqn
        q = _dot(cqn, wq_ref[...])
        q = q * jnp.tile(cb, (1, MLA_HEADS)) + _swap16(q) * jnp.tile(sb, (1, MLA_HEADS))
        q_ref[...] = q.astype(BF16)
        ckvn, _ = _rms_fwd(c_kv, gkv_ref[...])
        ckvn = ckvn.astype(BF16)
        ckvn_ref[...] = ckvn
        kv = _dot(ckvn, wkv_ref[...])
        k_ref[...] = (kv[:, :QW] + jnp.tile(kpe, (1, MLA_HEADS))).astype(BF16)
        v_ref[...] = kv[:, QW:].astype(BF16)

    widths = [(D, BF16), (Q_LORA, F32), (KV_LORA, F32), (Q_LORA, BF16), (KV_LORA, BF16), (QW, BF16), (QW, BF16), (VW, BF16),
              (SQ, BF16), (SK, BF16), (SK, BF16)]
    return pl.pallas_call(
        body, name="proj_fwd", grid=(T // tm,),
        in_specs=[_row_spec(tm, D), _const_spec((1, D)), _const_spec((D, P)), _const_spec((1, Q_LORA)),
                  _const_spec((Q_LORA, QW)), _const_spec((1, KV_LORA)), _const_spec((KV_LORA, QW + VW)),
                  _rope_spec(tm, S), _rope_spec(tm, S)],
        out_specs=[_row_spec(tm, w) for w, _ in widths],
        out_shape=[jax.ShapeDtypeStruct((T, w), dt) for w, dt in widths],
        compiler_params=_cparams(("arbitrary",)),
    )(h1, g_mix, w_in, g_q, wq, g_kv, wkv, rope_c, rope_s)


def _proj_bwd(dq, dk, dv, dqs, dks, dvs, c_q, c_kv, h1, dh2, g_mix, w_in, g_q, wq, g_kv, wkv, rope_c, rope_s, S):
    T, D = h1.shape
    P = w_in.shape[0]
    tm = _pick(S, (256, 128))
    QW = MLA_HEADS * HEAD_BLOCK
    VW = MLA_HEADS * V_HEAD
    SQ = SWA_HEADS * SWA_HEAD_DIM
    SK = SWA_KV_HEADS * SWA_HEAD_DIM

    def body(dq_ref, dk_ref, dv_ref, dqs_ref, dks_ref, dvs_ref, cq_ref, ckv_ref, h_ref, dh2_ref, gm_ref, win_ref, gq_ref,
             wq_ref, gkv_ref, wkv_ref, c_ref, s_ref, dh1_ref, dproj_ref, dql_ref, dkv_ref, dgm_ref, dgq_ref, dgkv_ref):
        cb, sb = c_ref[...], s_ref[...]
        dqv = dq_ref[...]
        dql = dqv * jnp.tile(cb, (1, MLA_HEADS)) + _swap16(dqv * jnp.tile(sb, (1, MLA_HEADS)))
        dql = dql.astype(BF16)
        dql_ref[...] = dql
        c_q = cq_ref[...]
        _, rq = _rms_fwd(c_q, gq_ref[...])
        d_cq, dgq = _rms_bwd(c_q, gq_ref[...], rq, _dot(dql, wq_ref[...]))

        dkv_all = dk_ref[...]
        dkpe = dkv_all[:, 0:HEAD_BLOCK]
        for h in range(1, MLA_HEADS):
            dkpe = dkpe + dkv_all[:, h * HEAD_BLOCK:(h + 1) * HEAD_BLOCK]
        lane = lax.broadcasted_iota(jnp.int32, dkpe.shape, 1)
        dkpe = jnp.where((lane >= 64) & (lane < 96), dkpe, 0.0)
        dkpe = dkpe * cb + _swap16(dkpe * sb)
        dkvc = jnp.concatenate([dkv_all.astype(BF16), dv_ref[...].astype(BF16)], axis=1)
        dkv_ref[...] = dkvc
        c_kv = ckv_ref[...]
        _, rkv = _rms_fwd(c_kv, gkv_ref[...])
        d_ckv, dgkv = _rms_bwd(c_kv, gkv_ref[...], rkv, _dot(dkvc, wkv_ref[...]))

        dproj = jnp.concatenate([d_cq.astype(BF16), d_ckv.astype(BF16), dqs_ref[...].astype(BF16),
                                 dks_ref[...].astype(BF16), dvs_ref[...].astype(BF16), dkpe.astype(BF16)], axis=1)
        dproj_ref[...] = dproj
        hv = h_ref[...]
        _, rm = _rms_fwd(hv, gm_ref[...])
        dxn, dgm = _rms_bwd(hv, gm_ref[...], rm, _dot(dproj, win_ref[...]))
        dh1_ref[...] = dh2_ref[...] + dxn

        @pl.when(pl.program_id(0) == 0)
        def _():
            dgm_ref[...] = jnp.zeros_like(dgm_ref)
            dgq_ref[...] = jnp.zeros_like(dgq_ref)
            dgkv_ref[...] = jnp.zeros_like(dgkv_ref)

        dgm_ref[...] += dgm
        dgq_ref[...] += dgq
        dgkv_ref[...] += dgkv

    return pl.pallas_call(
        body, name="proj_bwd", grid=(T // tm,),
        in_specs=[_row_spec(tm, QW), _row_spec(tm, QW), _row_spec(tm, VW), _row_spec(tm, SQ), _row_spec(tm, SK), _row_spec(tm, SK),
                  _row_spec(tm, Q_LORA), _row_spec(tm, KV_LORA), _row_spec(tm, D), _row_spec(tm, D),
                  _const_spec((1, D)), _const_spec((P, D)), _const_spec((1, Q_LORA)), _const_spec((QW, Q_LORA)),
                  _const_spec((1, KV_LORA)), _const_spec((QW + VW, KV_LORA)), _rope_spec(tm, S), _rope_spec(tm, S)],
        out_specs=[_row_spec(tm, D), _row_spec(tm, P), _row_spec(tm, QW), _row_spec(tm, QW + VW),
                   _acc_spec((1, D)), _acc_spec((1, Q_LORA)), _acc_spec((1, KV_LORA))],
        out_shape=[jax.ShapeDtypeStruct((T, D), F32), jax.ShapeDtypeStruct((T, P), BF16), jax.ShapeDtypeStruct((T, QW), BF16),
                   jax.ShapeDtypeStruct((T, QW + VW), BF16), jax.ShapeDtypeStruct((1, D), F32),
                   jax.ShapeDtypeStruct((1, Q_LORA), F32), jax.ShapeDtypeStruct((1, KV_LORA), F32)],
        compiler_params=_cparams(("arbitrary",)),
    )(dq, dk, dv, dqs, dks, dvs, c_q, c_kv, h1, dh2, g_mix, w_in, g_q, wq, g_kv, wkv, rope_c, rope_s)


def _out_fwd(o_mla, o_swa, g_mla, g_swa, w_o, h1):
    T, D = h1.shape
    W = o_mla.shape[1]
    tm = _pick(T, (512, 256, 128))

    def body(om_ref, os_ref, gm_ref, gs_ref, wo_ref, h_ref, h2_ref, oc_ref):
        a, _ = _rms_fwd(om_ref[...], gm_ref[...])
        b, _ = _rms_fwd(os_ref[...], gs_ref[...])
        oc = jnp.concatenate([a.astype(BF16), b.astype(BF16)], axis=1)
        oc_ref[...] = oc
        h2_ref[...] = h_ref[...] + _dot(oc, wo_ref[...])

    return pl.pallas_call(
        body, name="out_fwd", grid=(T // tm,),
        in_specs=[_row_spec(tm, W), _row_spec(tm, W), _const_spec((1, W)), _const_spec((1, W)), _const_spec((2 * W, D)),
                  _row_spec(tm, D)],
        out_specs=[_row_spec(tm, D), _row_spec(tm, 2 * W)],
        out_shape=[jax.ShapeDtypeStruct((T, D), F32), jax.ShapeDtypeStruct((T, 2 * W), BF16)],
        compiler_params=_cparams(("arbitrary",)),
    )(o_mla, o_swa, g_mla, g_swa, w_o, h1)


def _out_bwd(dh2, o_mla, o_swa, g_mla, g_swa, w_o):
    T, D = dh2.shape
    W = o_mla.shape[1]
    tm = _pick(T, (512, 256, 128))

    def body(dh_ref, om_ref, os_ref, gm_ref, gs_ref, wo_ref, dom_ref, dos_ref, dgm_ref, dgs_ref):
        doc = _dot_nt(dh_ref[...].astype(BF16), wo_ref[...])
        om, osw = om_ref[...], os_ref[...]
        _, ra = _rms_fwd(om, gm_ref[...])
        _, rb = _rms_fwd(osw, gs_ref[...])
        da, dga = _rms_bwd(om, gm_ref[...], ra, doc[:, :W])
        db, dgb = _rms_bwd(osw, gs_ref[...], rb, doc[:, W:])
        dom_ref[...] = da
        dos_ref[...] = db

        @pl.when(pl.program_id(0) == 0)
        def _():
            dgm_ref[...] = jnp.zeros_like(dgm_ref)
            dgs_ref[...] = jnp.zeros_like(dgs_ref)

        dgm_ref[...] += dga
        dgs_ref[...] += dgb

    return pl.pallas_call(
        body, name="out_bwd", grid=(T // tm,),
        in_specs=[_row_spec(tm, D), _row_spec(tm, W), _row_spec(tm, W), _const_spec((1, W)), _const_spec((1, W)),
                  _const_spec((2 * W, D))],
        out_specs=[_row_spec(tm, W), _row_spec(tm, W), _acc_spec((1, W)), _acc_spec((1, W))],
        out_shape=[jax.ShapeDtypeStruct((T, W), F32), jax.ShapeDtypeStruct((T, W), F32),
                   jax.ShapeDtypeStruct((1, W), F32), jax.ShapeDtypeStruct((1, W), F32)],
        compiler_params=_cparams(("arbitrary",)),
    )(dh2, o_mla, o_swa, g_mla, g_swa, w_o)


def _half_mask(shape, half):
    lane = lax.broadcasted_iota(jnp.int32, shape, len(shape) - 1)
    return (lane < 64) if half == 0 else (lane >= 64)


def _mla_fwd(q, k, v, B, S, exchange=None):
    T = B * S
    tq = _pick(S, MLA_Q_TILES)
    nq = S // tq
    pairs = MLA_HEADS // 2

    def body(q_ref, k_ref, v_ref, o_ref, lse_ref):
        causal = lax.broadcasted_iota(jnp.int32, (tq, tq), 1) <= lax.broadcasted_iota(jnp.int32, (tq, tq), 0)
        low = _half_mask((tq, 2 * V_HEAD), 0)
        for qi in range(nq):
            rows = slice(qi * tq, (qi + 1) * tq)
            past = slice(0, qi * tq)
            outs, lses = [], []
            for half in range(2):
                lanes = slice(half * HEAD_BLOCK, (half + 1) * HEAD_BLOCK)
                qv = q_ref[rows, lanes]
                s_d = jnp.where(causal, _dot_nt(qv, k_ref[rows, lanes]) * MLA_SCALE, NEG)
                m = jnp.max(s_d, axis=-1, keepdims=True)
                if qi:
                    s_p = _dot_nt(qv, k_ref[past, lanes]) * MLA_SCALE
                    m = jnp.maximum(m, jnp.max(s_p, axis=-1, keepdims=True))
                p_d = jnp.exp(s_d - m)
                l = jnp.sum(p_d, axis=-1, keepdims=True)
                acc = _dot(p_d.astype(BF16), v_ref[rows, :])
                if qi:
                    p_p = jnp.exp(s_p - m)
                    l = l + jnp.sum(p_p, axis=-1, keepdims=True)
                    acc = acc + _dot(p_p.astype(BF16), v_ref[past, :])
                outs.append(acc * (1.0 / l))
                lses.append(jnp.broadcast_to(m + jnp.log(l), (tq, 2 * V_HEAD)))
            o_ref[rows, :] = jnp.where(low, outs[0], outs[1])
            lse_ref[rows, :] = jnp.where(low, lses[0], lses[1])

    blk2 = pl.BlockSpec((S, 2 * HEAD_BLOCK), lambda b, j: (b, j))
    blk1 = pl.BlockSpec((S, 2 * V_HEAD), lambda b, j: (b, j))
    return _pcall(
        body, name="mla_fwd", grid=(B, pairs),
        in_specs=[blk2, blk2, blk1], out_specs=[blk1, blk1],
        out_shape=[jax.ShapeDtypeStruct((T, pairs * 2 * V_HEAD), F32)] * 2,
        args=(q, k, v), exchange=exchange)


def _mla_bwd(q, k, v, do, o, lse, B, S, exchange=None):
    T = B * S
    tq = _pick(S, MLA_Q_TILES)
    nq = S // tq
    pairs = MLA_HEADS // 2

    def body(q_ref, k_ref, v_ref, do_ref, o_ref, lse_ref, dq_ref, dk_ref, dv_ref, dk_acc):
        causal = lax.broadcasted_iota(jnp.int32, (tq, tq), 1) <= lax.broadcasted_iota(jnp.int32, (tq, tq), 0)
        for half in range(2):
            lanes = slice(half * HEAD_BLOCK, (half + 1) * HEAD_BLOCK)
            col = half * V_HEAD
            own = _half_mask((tq, 2 * V_HEAD), half)
            for qi in reversed(range(nq)):
                rows = slice(qi * tq, (qi + 1) * tq)
                past = slice(0, qi * tq)
                first = qi == nq - 1
                qv = q_ref[rows, lanes]
                dov = jnp.where(own, do_ref[rows, :], 0.0)
                dsum = jnp.sum(dov * o_ref[rows, :], axis=-1, keepdims=True)
                dob = dov.astype(BF16)
                lse_col = lse_ref[rows, col:col + 1]

                def grads(keys, mask):
                    kv, vv = k_ref[keys, lanes], v_ref[keys, :]
                    p = jnp.exp(_dot_nt(qv, kv) * MLA_SCALE - lse_col)
                    if mask is not None:
                        p = jnp.where(mask, p, 0.0)
                    ds = (p * (_dot_nt(dob, vv) - dsum) * MLA_SCALE).astype(BF16)
                    dk, dv = _dot_tn(ds, qv), _dot_tn(p.astype(BF16), dob)
                    if first:
                        dk_acc[keys, :] = dk
                    else:
                        dk_acc[keys, :] += dk
                    if first and half == 0:
                        dv_ref[keys, :] = dv
                    else:
                        dv_ref[keys, :] += dv
                    return _dot(ds, kv)

                dq = grads(rows, causal)
                if qi:
                    dq = dq + grads(past, None)
                dq_ref[rows, lanes] = dq
            dk_ref[:, lanes] = dk_acc[...]

    blk2 = pl.BlockSpec((S, 2 * HEAD_BLOCK), lambda b, j: (b, j))
    blk1 = pl.BlockSpec((S, 2 * V_HEAD), lambda b, j: (b, j))
    return _pcall(
        body, name="mla_bwd", grid=(B, pairs),
        in_specs=[blk2, blk2, blk1, blk1, blk1, blk1], out_specs=[blk2, blk2, blk1],
        out_shape=[jax.ShapeDtypeStruct((T, MLA_HEADS * HEAD_BLOCK), F32), jax.ShapeDtypeStruct((T, MLA_HEADS * HEAD_BLOCK), F32),
                   jax.ShapeDtypeStruct((T, pairs * 2 * V_HEAD), F32)],
        scratch_shapes=[pltpu.VMEM((S, HEAD_BLOCK), F32)],
        args=(q, k, v, do, o, lse), exchange=exchange)


SWA_GROUP = SWA_HEADS // SWA_KV_HEADS
SWA_ROWS = SWA_GROUP * WINDOW


SWA_CHUNK = 8


def _bdot(a, b, ca, cb):
    return lax.dot_general(a, b, (((ca,), (cb,)), ((0,), (0,))), preferred_element_type=F32)


def _swa_chunks(nb):
    cb = next(c for c in (SWA_CHUNK, 4, 2, 1) if nb % c == 0)
    return cb, [(n0, slice(n0 * WINDOW, (n0 + cb) * WINDOW)) for n0 in range(0, nb, cb)]


def _swa_keys(ref, n0, cb):
    W = WINDOW
    cur = ref[n0 * W:(n0 + cb) * W, :]
    if n0:
        prev = ref[(n0 - 1) * W:(n0 + cb - 1) * W, :]
    else:
        prev = jnp.concatenate([cur[:W], cur[:(cb - 1) * W]], axis=0) if cb > 1 else cur
    return jnp.concatenate([prev.reshape(cb, W, W), cur.reshape(cb, W, W)], axis=1)


def _swa_stack(ref, rows, cb):
    return jnp.concatenate([ref[rows, j * WINDOW:(j + 1) * WINDOW].reshape(cb, WINDOW, WINDOW) for j in range(SWA_GROUP)], axis=1)


def _swa_unstack(low, high, cb):
    sel = _half_mask((cb, WINDOW, WINDOW), 0)
    return jnp.concatenate([jnp.where(sel, low[:, j * WINDOW:(j + 1) * WINDOW], high[:, j * WINDOW:(j + 1) * WINDOW])
                            .reshape(cb * WINDOW, WINDOW) for j in range(SWA_GROUP)], axis=1)


def _swa_sink_col(sink_ref, g):
    return jnp.concatenate([jnp.broadcast_to(sink_ref[g * SWA_GROUP + j:g * SWA_GROUP + j + 1, 0:1], (WINDOW, 1))
                            for j in range(SWA_GROUP)], axis=0)


def _swa_probs(qm, kcat, bias_g, sink, first_chunk):
    shape = (qm.shape[0], SWA_ROWS, 2 * WINDOW)
    qi = lax.broadcasted_iota(jnp.int32, shape, 1) % WINDOW
    col = lax.broadcasted_iota(jnp.int32, shape, 2)
    valid = (col > qi) & (col <= qi + WINDOW)
    if first_chunk:
        valid = valid & ((col >= WINDOW) | (lax.broadcasted_iota(jnp.int32, shape, 0) > 0))
    s = jnp.where(valid, _bdot(qm, kcat, 2, 2) * SWA_SCALE + bias_g, NEG)
    m = jnp.maximum(jnp.max(s, axis=-1, keepdims=True), sink)
    e = jnp.exp(s - m)
    e_s = jnp.exp(sink - m)
    inv = 1.0 / (jnp.sum(e, axis=-1, keepdims=True) + e_s)
    return e * inv, e_s * inv


def _swa_fwd(q, k, v, bias, sinks, B, S):
    T = B * S
    W = WINDOW
    nb = S // W

    cb, chunks = _swa_chunks(nb)

    def body(q_ref, k_ref, v_ref, bias_ref, sink_ref, o_ref):
        sink_cols = [_swa_sink_col(sink_ref, g) for g in range(SWA_KV_HEADS)]
        for n0, rows in chunks:
            kcat, vcat = _swa_keys(k_ref, n0, cb), _swa_keys(v_ref, n0, cb)
            qcat = _swa_stack(q_ref, rows, cb)
            outs = []
            for g in range(SWA_KV_HEADS):
                qm = jnp.where(_half_mask(qcat.shape, g), qcat, jnp.zeros_like(qcat))
                p, _ = _swa_probs(qm, kcat, bias_ref[g], sink_cols[g], n0 == 0)
                outs.append(_bdot(p.astype(BF16), vcat, 2, 1))
            o_ref[rows, :] = _swa_unstack(outs[0], outs[1], cb)

    SQ, SK = SWA_HEADS * SWA_HEAD_DIM, SWA_KV_HEADS * SWA_HEAD_DIM
    return pl.pallas_call(
        body, name="swa_fwd", grid=(B,),
        in_specs=[pl.BlockSpec((S, SQ), lambda b: (b, 0)), pl.BlockSpec((S, SK), lambda b: (b, 0)),
                  pl.BlockSpec((S, SK), lambda b: (b, 0)), _const_spec((SWA_KV_HEADS, SWA_ROWS, 2 * W)),
                  _const_spec((SWA_HEADS, 128))],
        out_specs=pl.BlockSpec((S, SQ), lambda b: (b, 0)),
        out_shape=jax.ShapeDtypeStruct((T, SQ), F32),
        compiler_params=_cparams(("arbitrary",)),
    )(q, k, v, bias, sinks)


def _swa_bwd(q, k, v, do, bias, sinks, B, S):
    T = B * S
    W = WINDOW
    nb = S // W
    cb, chunks = _swa_chunks(nb)

    def body(q_ref, k_ref, v_ref, do_ref, bias_ref, sink_ref, dq_ref, dk_ref, dv_ref, dbias_ref, dsink_ref):
        @pl.when(pl.program_id(0) == 0)
        def _():
            dbias_ref[...] = jnp.zeros_like(dbias_ref)
            dsink_ref[...] = jnp.zeros_like(dsink_ref)

        dk_ref[...] = jnp.zeros_like(dk_ref)
        dv_ref[...] = jnp.zeros_like(dv_ref)
        sink_cols = [_swa_sink_col(sink_ref, g) for g in range(SWA_KV_HEADS)]
        for n0, rows in chunks:
            kcat, vcat = _swa_keys(k_ref, n0, cb), _swa_keys(v_ref, n0, cb)
            qcat = _swa_stack(q_ref, rows, cb)
            docat = _swa_stack(do_ref, rows, cb)
            dk = jnp.zeros((cb, 2 * W, W), F32)
            dv = jnp.zeros((cb, 2 * W, W), F32)
            dqs = []
            for g in range(SWA_KV_HEADS):
                own = _half_mask(qcat.shape, g)
                qm = jnp.where(own, qcat, jnp.zeros_like(qcat))
                dom = jnp.where(own, docat, 0.0).astype(BF16)
                p, p_s = _swa_probs(qm, kcat, bias_ref[g], sink_cols[g], n0 == 0)
                dp = _bdot(dom, vcat, 2, 2)
                dsum = jnp.sum(p * dp, axis=-1, keepdims=True)
                ds = p * (dp - dsum)
                dbias_ref[g] += jnp.sum(ds, axis=0)
                dsink_ref[g] += jnp.broadcast_to(-jnp.sum(p_s * dsum, axis=0), (SWA_ROWS, 128))
                dsb = (ds * SWA_SCALE).astype(BF16)
                dqs.append(_bdot(dsb, kcat, 2, 1))
                dk = dk + _bdot(dsb, qm, 1, 1)
                dv = dv + _bdot(p.astype(BF16), dom, 1, 1)
            dq_ref[rows, :] = _swa_unstack(dqs[0], dqs[1], cb)
            dk_ref[rows, :] += dk[:, W:].reshape(cb * W, W)
            dv_ref[rows, :] += dv[:, W:].reshape(cb * W, W)
            dk_prev, dv_prev = dk[:, :W].reshape(cb * W, W), dv[:, :W].reshape(cb * W, W)
            if n0:
                before = slice((n0 - 1) * W, (n0 + cb - 1) * W)
                dk_ref[before, :] += dk_prev
                dv_ref[before, :] += dv_prev
            elif cb > 1:
                before = slice(0, (cb - 1) * W)
                dk_ref[before, :] += dk_prev[W:]
                dv_ref[before, :] += dv_prev[W:]

    SQ, SK = SWA_HEADS * SWA_HEAD_DIM, SWA_KV_HEADS * SWA_HEAD_DIM
    row = lambda w: pl.BlockSpec((S, w), lambda b: (b, 0))
    return pl.pallas_call(
        body, name="swa_bwd", grid=(B,),
        in_specs=[row(SQ), row(SK), row(SK), row(SQ), _const_spec((SWA_KV_HEADS, SWA_ROWS, 2 * W)), _const_spec((SWA_HEADS, 128))],
        out_specs=[row(SQ), row(SK), row(SK), _acc_spec((SWA_KV_HEADS, SWA_ROWS, 2 * W)), _acc_spec((SWA_KV_HEADS, SWA_ROWS, 128))],
        out_shape=[jax.ShapeDtypeStruct((T, SQ), F32), jax.ShapeDtypeStruct((T, SK), F32), jax.ShapeDtypeStruct((T, SK), F32),
                   jax.ShapeDtypeStruct((SWA_KV_HEADS, SWA_ROWS, 2 * W), F32), jax.ShapeDtypeStruct((SWA_KV_HEADS, SWA_ROWS, 128), F32)],
        compiler_params=_cparams(("arbitrary",)),
    )(q, k, v, do, bias, sinks)


def _bias_expand(rel_bias, bucket):
    W = WINDOW

    def body(rb_ref, bucket_ref, out_ref):
        bk = bucket_ref[...]
        for h in range(SWA_HEADS):
            def add(b, acc):
                return jnp.where(bk == b, rb_ref[b, h], acc)

            out_ref[h] = lax.fori_loop(0, REL_BUCKETS, add, jnp.zeros((W, 2 * W), F32))

    return pl.pallas_call(
        body, name="bias_expand",
        in_specs=[pl.BlockSpec(memory_space=pltpu.SMEM), pl.BlockSpec(memory_space=pltpu.VMEM)],
        out_specs=pl.BlockSpec(memory_space=pltpu.VMEM),
        out_shape=jax.ShapeDtypeStruct((SWA_HEADS, W, 2 * W), F32),
        compiler_params=_cparams(),
    )(rel_bias, bucket)


def _bias_reduce(dbias, dsink_rows, bucket):
    def body(db_ref, ds_ref, bucket_ref, out_ref, sink_out_ref, rows_ref):
        sink_lane = lax.broadcasted_iota(jnp.int32, (1, 128), 1)
        sink_acc = jnp.zeros((1, 128), F32)
        for h in range(SWA_HEADS):
            sink_acc = jnp.where(sink_lane == h, jnp.sum(ds_ref[h], axis=0, keepdims=True), sink_acc)
        sink_out_ref[...] = sink_acc
        bk = bucket_ref[...]
        for h in range(SWA_HEADS):
            dbh = db_ref[h]

            def add(b, carry):
                rows_ref[pl.ds(h * REL_BUCKETS + b, 1), :] = jnp.sum(jnp.where(bk == b, dbh, 0.0), axis=0, keepdims=True)
                return carry

            lax.fori_loop(0, REL_BUCKETS, add, 0)
        totals = jnp.sum(rows_ref[...], axis=-1, keepdims=True)
        lane = lax.broadcasted_iota(jnp.int32, (REL_BUCKETS, 128), 1)
        acc = jnp.zeros((REL_BUCKETS, 128), F32)
        for h in range(SWA_HEADS):
            acc = jnp.where(lane == h, totals[h * REL_BUCKETS:(h + 1) * REL_BUCKETS], acc)
        out_ref[...] = acc

    return pl.pallas_call(
        body, name="bias_reduce",
        out_shape=[jax.ShapeDtypeStruct((REL_BUCKETS, 128), F32), jax.ShapeDtypeStruct((1, 128), F32)],
        scratch_shapes=[pltpu.VMEM((SWA_HEADS * REL_BUCKETS, 2 * WINDOW), F32)],
        compiler_params=_cparams(),
    )(dbias, dsink_rows, bucket)


def _t5_bucket(dist):
    n = jnp.maximum(dist, 0)
    max_exact = REL_BUCKETS // 2
    nf = jnp.maximum(n, 1).astype(F32)
    large = max_exact + (jnp.log(nf / max_exact) / math.log(REL_MAX_DIST / max_exact) * (REL_BUCKETS - max_exact)).astype(jnp.int32)
    large = jnp.minimum(large, REL_BUCKETS - 1)
    return jnp.where(n < max_exact, n, large)


def _pair_heads(a, axis, inverse=False):
    shp = a.shape
    a = a.reshape(shp[:axis] + ((4, 2, 64) if inverse else (2, 4, 64)) + shp[axis + 1:])
    a = jnp.swapaxes(a, axis, axis + 1)
    return a.reshape(shp)


def _owner_blocks(w32, w16, me):
    r, c = w32.shape[0] // N_DEV, w32.shape[1]
    return lax.dynamic_slice(w32, (me * r, 0), (r, c)), w16.reshape(N_DEV, r, c)


_IN_WIDTHS = (Q_LORA, KV_LORA, QK_ROPE, SWA_HEADS * SWA_HEAD_DIM, SWA_KV_HEADS * SWA_HEAD_DIM, SWA_KV_HEADS * SWA_HEAD_DIM)


def _win_to_kernel(wt):
    parts, o = [], 0
    for wd in _IN_WIDTHS:
        parts.append(wt[o:o + wd])
        o += wd
    cq, ckv, kpe, qs, ks, vs = parts
    return jnp.concatenate([cq, ckv, _pair_heads(qs, 0), ks, vs, jnp.pad(kpe, ((64, 32), (0, 0)))], axis=0)


def _win_from_kernel(dwt):
    o = [0, Q_LORA, Q_LORA + KV_LORA, Q_LORA + KV_LORA + 512, Q_LORA + KV_LORA + 640, Q_LORA + KV_LORA + 768]
    cq, ckv, qs, ks, vs, kpe = [dwt[a:b] for a, b in zip(o, o[1:] + [dwt.shape[0]])]
    return jnp.concatenate([cq, ckv, kpe[64:96], _pair_heads(qs, 0, inverse=True), ks, vs], axis=0)


def _wq_to_kernel(wt):
    return jnp.pad(wt.reshape(MLA_HEADS, QK_NOPE + QK_ROPE, Q_LORA), ((0, 0), (0, 32), (0, 0))).reshape(MLA_HEADS * HEAD_BLOCK, Q_LORA)


def _wq_from_kernel(dwt):
    return dwt.reshape(MLA_HEADS, HEAD_BLOCK, Q_LORA)[:, :QK_NOPE + QK_ROPE].reshape(-1, Q_LORA)


def _wkv_to_kernel(wt):
    w3 = wt.reshape(MLA_HEADS, QK_NOPE + V_HEAD, KV_LORA)
    kpart = jnp.pad(w3[:, :QK_NOPE], ((0, 0), (0, 64), (0, 0))).reshape(MLA_HEADS * HEAD_BLOCK, KV_LORA)
    return jnp.concatenate([kpart, w3[:, QK_NOPE:].reshape(MLA_HEADS * V_HEAD, KV_LORA)], axis=0)


def _wkv_from_kernel(dwt):
    dk = dwt[:MLA_HEADS * HEAD_BLOCK].reshape(MLA_HEADS, HEAD_BLOCK, KV_LORA)[:, :QK_NOPE]
    dv = dwt[MLA_HEADS * HEAD_BLOCK:].reshape(MLA_HEADS, V_HEAD, KV_LORA)
    return jnp.concatenate([dk, dv], axis=1).reshape(-1, KV_LORA)


def _wo_to_kernel(w):
    return jnp.concatenate([w[:512], _pair_heads(w[512:], 0)], axis=0)


def _wo_from_kernel(dw):
    return jnp.concatenate([dw[:512], _pair_heads(dw[512:], 0, inverse=True)], axis=0)


def _kernel_weights(key, stored):
    if key in ("g1", "u1", "g2", "u2"):
        return {key + "t": stored}
    if key in ("d1", "d2"):
        return {key: stored}
    if key == "o":
        return {key: _wo_to_kernel(stored)}
    kt = _TO_KERNEL[key](stored)
    return {key: kt.T, key + "t": kt}


_SMALL = ("g_ffn1", "g_mix", "g_q_a", "g_kv_a", "attn_sinks", "rel_bias", "g_out_mla", "g_out_swa", "g_ffn2", "g_final")


class _Comm:
    def __init__(self, shards, me):
        self.shards, self.me = shards, me
        self.own32, self.recv16, self.pending, self.token = {}, {}, [], None

    def gather(self, keys):
        return [("gather", self.shards[k]) for k in keys]

    def gathered(self, keys, results, wts):
        for k, g in zip(keys, results):
            wts.update(_kernel_weights(k, g.reshape(N_DEV * g.shape[1], g.shape[2])))

    def scatter(self, keys, grads):
        jobs = []
        for k in keys:
            g32, g16 = (_FROM_KERNEL[k](g) if k in _FROM_KERNEL else g for g in grads[k])
            self.own32[k], blocks = _owner_blocks(g32, g16, self.me)
            jobs.append(("scatter", blocks))
        return jobs

    def scattered(self, keys, results):
        self.recv16.update(zip(keys, results))

    def scatter_start(self, keys, grads, name):
        handle, self.token = _scatter_start([blocks for _, blocks in self.scatter(keys, grads)], name + "_start")
        self.pending.append((keys, handle, name + "_wait"))

    def scatter_wait(self, after):
        keys, handle, name = self.pending.pop(0)
        self.scattered(keys, _scatter_wait(handle, after, name))
        return keys


def _step(x, target, gains, wts, comm):
    def jobs_in(keys):
        return comm.gather(keys) if comm else None

    def jobs_out(keys):
        return comm.scatter(keys, grads) if comm else None

    B, S, D = x.shape
    T = B * S
    x2 = x.reshape(T, D)
    t2 = target.reshape(T, D)

    pos = jnp.arange(S, dtype=F32)
    inv_freq = ROPE_THETA ** (-jnp.arange(0, QK_ROPE, 2, dtype=F32) / QK_ROPE)
    ang = pos[:, None] * inv_freq[None, :]
    cos, sin = jnp.cos(ang), jnp.sin(ang)
    ones, zeros = jnp.ones((S, 64), F32), jnp.zeros((S, 32), F32)
    rope_c = jnp.concatenate([ones, cos, cos, zeros], axis=1)
    rope_s = jnp.concatenate([0.0 * ones, -sin, sin, zeros], axis=1)

    qi = jnp.arange(WINDOW)[:, None]
    kj = jnp.arange(2 * WINDOW)[None, :]
    bucket = _t5_bucket(qi + WINDOW - kj).astype(jnp.int32)
    bias = _bias_expand(gains["rel_bias"], bucket).reshape(SWA_KV_HEADS, SWA_ROWS, 2 * WINDOW)
    sinks = jnp.broadcast_to(gains["attn_sinks"].reshape(SWA_HEADS, 1), (SWA_HEADS, 128))
    g_swa = _pair_heads(gains["g_out_swa"], 1)

    keys = ("in", "q", "kv", "o", "g2")
    (h1, n1, gate1, up1, act1), got = _ffn_fwd(x2, gains["g_ffn1"], wts["g1t"], wts["u1t"], wts["d1"], "ffn1_fwd", jobs_in(keys))
    if comm:
        comm.gathered(keys, got, wts)
    (u, c_q, c_kv, cqn, ckvn, q, k, v, qs, ks, vs) = _proj_fwd(
        h1, gains["g_mix"], wts["in"], gains["g_q_a"], wts["q"], gains["g_kv_a"], wts["kv"], rope_c, rope_s, S)
    keys = ("u2", "d2")
    (o_mla, lse), got = _mla_fwd(q, k, v, B, S, jobs_in(keys))
    if comm:
        comm.gathered(keys, got, wts)
    o_swa = _swa_fwd(qs, ks, vs, bias, sinks, B, S)
    h2, oc = _out_fwd(o_mla, o_swa, gains["g_out_mla"], g_swa, wts["o"], h1)
    (dh3, n2, gate2, up2, act2, loss, dg_final), _ = _ffn_fwd(
        h2, gains["g_ffn2"], wts["g2t"], wts["u2t"], wts["d2"], "ffn2_fwd", loss_head=(t2, gains["g_final"].reshape(1, D)))

    grads = {}
    (dh2, dg_ffn2, dgate2, dup2), _ = _ffn_bwd(h2, gains["g_ffn2"], gate2, up2, dh3, wts["g2t"], wts["u2t"], wts["d2"], "ffn2_bwd")
    grads["g2"], _ = _matmul_tn(dgate2, n2, "dw_gate2")
    grads["u2"], _ = _matmul_tn(dup2, n2, "dw_up2")
    grads["d2"], _ = _matmul_tn(act2, dh3, "dw_down2", 0.5)
    do_mla, do_swa, dg_mla, dg_swa = _out_bwd(dh2, o_mla, o_swa, gains["g_out_mla"], g_swa, wts["o"])
    grads["o"], _ = _matmul_tn(oc, dh2, "dw_o")
    keys = ("g2", "u2", "d2")
    (dq, dk, dv), got = _mla_bwd(q, k, v, do_mla, o_mla, lse, B, S, jobs_out(keys))
    if comm:
        comm.scattered(keys, got)
    dqs, dks, dvs, dbias, dsink = _swa_bwd(qs, ks, vs, do_swa, bias, sinks, B, S)
    dh1, dproj, dql, dkvc, dg_mix, dg_q, dg_kv = _proj_bwd(
        dq, dk, dv, dqs, dks, dvs, c_q, c_kv, h1, dh2, gains["g_mix"], wts["int"], gains["g_q_a"], wts["qt"],
        gains["g_kv_a"], wts["kvt"], rope_c, rope_s, S)
    grads["in"], _ = _matmul_tn(dproj, u, "dw_in")
    grads["q"], _ = _matmul_tn(dql, cqn, "dw_q")
    grads["kv"], _ = _matmul_tn(dkvc, ckvn, "dw_kv")
    grads["d1"], _ = _matmul_tn(act1, dh1, "dw_down1", 0.5)
    if comm:
        comm.scatter_start(("o", "in", "q", "kv", "d1"), grads, "grad_scatter_mid")
    (dx, dg_ffn1, dgate1, dup1), _ = _ffn_bwd(
        x2, gains["g_ffn1"], gate1, up1, dh1, wts["g1t"], wts["u1t"], wts["d1"], "ffn1_bwd", after=comm and comm.token)
    grads["g1"], _ = _matmul_tn(dgate1, n1, "dw_gate1")
    if comm:
        comm.scatter_start(("g1",), grads, "grad_scatter_gate1")
    grads["u1"], _ = _matmul_tn(dup1, n1, "dw_up1", after=comm and comm.token)
    if comm:
        comm.scatter_start(("u1",), grads, "grad_scatter_up1")

    d_rel_bias, d_sinks = _bias_reduce(dbias.reshape(SWA_HEADS, WINDOW, 2 * WINDOW), dsink.reshape(SWA_HEADS, WINDOW, 128), bucket)
    small = {
        "g_ffn1": dg_ffn1, "g_mix": dg_mix, "g_q_a": dg_q, "g_kv_a": dg_kv, "attn_sinks": d_sinks,
        "rel_bias": d_rel_bias, "g_out_mla": dg_mla, "g_out_swa": _pair_heads(dg_swa, 1, inverse=True),
        "g_ffn2": dg_ffn2, "g_final": dg_final, "loss": loss,
    }
    return dx.reshape(B, S, D), grads, small


_WEIGHTS = ("g_ffn1", "w_ffn1_gate", "w_ffn1_up", "w_ffn1_down", "g_mix", "w_in", "g_q_a", "w_q_b", "g_kv_a", "w_kv_b",
            "attn_sinks", "rel_bias", "g_out_mla", "g_out_swa", "w_o", "g_ffn2", "w_ffn2_gate", "w_ffn2_up", "w_ffn2_down",
            "g_final")
_BIG = (("w_ffn1_gate", "g1", True), ("w_ffn1_up", "u1", True), ("w_ffn1_down", "d1", False), ("w_in", "in", True),
        ("w_q_b", "q", True), ("w_kv_b", "kv", True), ("w_o", "o", False), ("w_ffn2_gate", "g2", True),
        ("w_ffn2_up", "u2", True), ("w_ffn2_down", "d2", False))
_TO_KERNEL = {"in": _win_to_kernel, "q": _wq_to_kernel, "kv": _wkv_to_kernel}
_FROM_KERNEL = {"in": _win_from_kernel, "q": _wq_from_kernel, "kv": _wkv_from_kernel, "o": _wo_from_kernel}


def kernel(x, g_ffn1, w_ffn1_gate, w_ffn1_up, w_ffn1_down, g_mix, w_in, g_q_a, w_q_b, g_kv_a, w_kv_b, attn_sinks, rel_bias, g_out_mla, g_out_swa, w_o, g_ffn2, w_ffn2_gate, w_ffn2_up, w_ffn2_down, g_final, loss_target, m_g_ffn1, m_w_ffn1_gate, m_w_ffn1_up, m_w_ffn1_down, m_g_mix, m_w_in, m_g_q_a, m_w_q_b, m_g_kv_a, m_w_kv_b, m_attn_sinks, m_rel_bias, m_g_out_mla, m_g_out_swa, m_w_o, m_g_ffn2, m_w_ffn2_gate, m_w_ffn2_up, m_w_ffn2_down, m_g_final, v_g_ffn1, v_w_ffn1_gate, v_w_ffn1_up, v_w_ffn1_down, v_g_mix, v_w_in, v_g_q_a, v_w_q_b, v_g_kv_a, v_w_kv_b, v_attn_sinks, v_rel_bias, v_g_out_mla, v_g_out_swa, v_w_o, v_g_ffn2, v_w_ffn2_gate, v_w_ffn2_up, v_w_ffn2_down, v_g_final):
    w = dict(zip(_WEIGHTS, (g_ffn1, w_ffn1_gate, w_ffn1_up, w_ffn1_down, g_mix, w_in, g_q_a, w_q_b, g_kv_a, w_kv_b, attn_sinks,
                            rel_bias, g_out_mla, g_out_swa, w_o, g_ffn2, w_ffn2_gate, w_ffn2_up, w_ffn2_down, g_final)))
    m = dict(zip(_WEIGHTS, (m_g_ffn1, m_w_ffn1_gate, m_w_ffn1_up, m_w_ffn1_down, m_g_mix, m_w_in, m_g_q_a, m_w_q_b, m_g_kv_a,
                            m_w_kv_b, m_attn_sinks, m_rel_bias, m_g_out_mla, m_g_out_swa, m_w_o, m_g_ffn2, m_w_ffn2_gate,
                            m_w_ffn2_up, m_w_ffn2_down, m_g_final)))
    v = dict(zip(_WEIGHTS, (v_g_ffn1, v_w_ffn1_gate, v_w_ffn1_up, v_w_ffn1_down, v_g_mix, v_w_in, v_g_q_a, v_w_q_b, v_g_kv_a,
                            v_w_kv_b, v_attn_sinks, v_rel_bias, v_g_out_mla, v_g_out_swa, v_w_o, v_g_ffn2, v_w_ffn2_gate,
                            v_w_ffn2_up, v_w_ffn2_down, v_g_final)))
    me = 4 * lax.axis_index("x") + 2 * lax.axis_index("y") + lax.axis_index("c")

    stored = lambda src, name, by_col: src[name][0].T if by_col else src[name][0]
    shards = dict(zip([key for _, key, _ in _BIG], _cast_bf16([stored(w, name, by_col) for name, _, by_col in _BIG])))
    comm = _Comm(shards, me)
    first = ("g1", "u1", "d1")
    wts = {}
    comm.gathered(first, _all_gather([shards[k] for k in first], "weights_all_gather"), wts)

    gains = {n: (w[n] if n == "rel_bias" else w[n].reshape(1, -1)) for n in _SMALL}
    grad_x, _, small = _step(x, loss_target, gains, wts, comm)

    small_gathered = _all_gather([small[n] for n in _SMALL] + [small["loss"]], "small_all_gather")

    out_g, out_d, out_m, out_v = {}, {}, {}, {}
    me1 = me.astype(jnp.int32).reshape(1)
    big = {key: (name, by_col) for name, key, by_col in _BIG}

    def update(key):
        name, by_col = big[key]
        updates = _adam_big(stored(w, name, by_col), stored(m, name, by_col), stored(v, name, by_col),
                            comm.own32[key], comm.recv16[key], me1, comm.token)
        out_g[name], out_d[name], out_m[name], out_v[name] = ((a.T if by_col else a)[None] for a in updates)
        return updates[0]

    done = [update(key) for key in ("g2", "u2", "d2")]
    for _ in range(2):
        done = [update(key) for key in comm.scatter_wait(after=done[-1])]
    as_2d = lambda src: [src[n] if src[n].ndim == 2 else src[n].reshape(1, -1) for n in _SMALL]
    loss, updates = _adam_small(as_2d(w), as_2d(m), as_2d(v), small_gathered[:-1], small_gathered[-1])
    for n, (g, d, mn, vn) in zip(_SMALL, updates):
        out_g[n], out_d[n], out_m[n], out_v[n] = (a.reshape(w[n].shape) for a in (g, d, mn, vn))
    for key in comm.scatter_wait(after=loss):
        update(key)

    return (loss[0, 0], grad_x, *[out_g[n] for n in _WEIGHTS], *[out_d[n] for n in _WEIGHTS],
            *[out_m[n] for n in _WEIGHTS], *[out_v[n] for n in _WEIGHTS])
```

```python
import functools
import math

import jax
import jax.numpy as jnp
from jax import lax
from jax.experimental import pallas as pl
from jax.experimental.pallas import tpu as pltpu

F32 = jnp.float32
BF16 = jnp.bfloat16
MESH = pl.DeviceIdType.MESH

EPS = 1e-6
NEG = -1e30
N_DEV = 8

MLA_HEADS = 8
Q_LORA = 256
KV_LORA = 128
QK_NOPE = 64
QK_ROPE = 32
V_HEAD = 64
ROPE_THETA = 10000.0
SWA_HEADS = 8
SWA_KV_HEADS = 2
SWA_HEAD_DIM = 64
WINDOW = 128
REL_BUCKETS = 32
REL_MAX_DIST = 128
HEAD_BLOCK = 128
MLA_Q_TILES = (512, 256, 128)
MLA_SCALE = (QK_NOPE + QK_ROPE) ** -0.5
MLA_SCALE_LOG2 = MLA_SCALE * math.log2(math.e)
SWA_SCALE = SWA_HEAD_DIM ** -0.5

ADAM_LR = 0.001
ADAM_B1 = 0.9
ADAM_B2 = 0.999
ADAM_EPS = 1e-08
ADAM_WD = 0.01
ADAM_STEP = 10

VMEM_LIMIT = 56 * 1024 * 1024


def _cparams(semantics=None):
    return pltpu.CompilerParams(dimension_semantics=semantics, vmem_limit_bytes=VMEM_LIMIT)


def _dot(a, b):
    return jnp.dot(a, b, preferred_element_type=F32)


def _dot_nt(a, b):
    return lax.dot_general(a, b, (((1,), (1,)), ((), ())), preferred_element_type=F32)


def _dot_tn(a, b):
    return lax.dot_general(a, b, (((0,), (0,)), ((), ())), preferred_element_type=F32)


def _rms_fwd(x, g):
    r = lax.rsqrt(jnp.mean(x * x, axis=-1, keepdims=True) + EPS)
    return x * r * g, r


def _rms_bwd(x, g, r, dy):
    xh = x * r
    dyg = dy * g
    dx = r * (dyg - xh * jnp.mean(dyg * xh, axis=-1, keepdims=True))
    return dx, jnp.sum(dy * xh, axis=0, keepdims=True)


def _swap16(x):
    n = x.shape[-1]
    up = pltpu.roll(x, n - 16, 1)
    down = pltpu.roll(x, 16, 1)
    lane = lax.broadcasted_iota(jnp.int32, x.shape, 1) % HEAD_BLOCK
    return jnp.where((lane >= 64) & (lane < 80), up, jnp.where((lane >= 80) & (lane < 96), down, 0.0))


def _const_spec(shape):
    return pl.BlockSpec(shape, lambda *_: (0,) * len(shape), pipeline_mode=pl.Buffered(1))


def _acc_spec(shape):
    return pl.BlockSpec(shape, lambda *_: (0,) * len(shape))


def _row_spec(tm, width):
    return pl.BlockSpec((tm, width), lambda i: (i, 0))


def _cast_bf16(arrays):
    n = len(arrays)

    def body(*refs):
        for src, dst in zip(refs[:n], refs[n:]):
            dst[...] = src[...].astype(BF16)

    return pl.pallas_call(
        body, name="cast_weights",
        out_shape=[jax.ShapeDtypeStruct(a.shape, BF16) for a in arrays],
        compiler_params=_cparams(),
    )(*arrays)


def _all_gather(arrays, name):
    n = len(arrays)

    def body(*refs):
        ins, outs = refs[:n], refs[n:2 * n]
        send_sems, recv_sems, local_sems = refs[2 * n:]
        x, y, c = lax.axis_index("x"), lax.axis_index("y"), lax.axis_index("c")
        me, sibling = (x, y, c), (x, y, 1 - c)
        chips = [(1 - x, y), (x, 1 - y), (1 - x, 1 - y)]

        def slot(a, dev):
            return outs[a].at[4 * dev[0] + 2 * dev[1] + dev[2]]

        def copy(a, k, block, to, src=None):
            return pltpu.make_async_remote_copy(
                src_ref=slot(a, block) if src is None else src, dst_ref=slot(a, block),
                send_sem=send_sems.at[a, k], recv_sem=recv_sems.at[a, k], device_id=to, device_id_type=MESH)

        mine = [pltpu.make_async_copy(ins[a], slot(a, me), local_sems.at[a]) for a in range(n)]
        for cp in mine:
            cp.start()
        first = []
        for a in range(n):
            first.append(copy(a, 0, me, sibling, src=ins[a]))
            first += [copy(a, 1 + j, me, (*chip, c), src=ins[a]) for j, chip in enumerate(chips)]
        for cp in first:
            cp.start()
        passed = []
        for j, chip in enumerate(chips):
            for a in range(n):
                copy(a, 1 + j, (*chip, c), me).wait_recv()
                cp = copy(a, 4 + j, (*chip, c), sibling)
                cp.start()
                passed.append(cp)
        for a in range(n):
            copy(a, 0, sibling, me).wait_recv()
            for j, chip in enumerate(chips):
                copy(a, 4 + j, (*chip, 1 - c), me).wait_recv()
        for cp in first + passed:
            cp.wait_send()
        for cp in mine:
            cp.wait()

    any_spec = pl.BlockSpec(memory_space=pl.ANY)
    return pl.pallas_call(
        body, name=name,
        out_shape=[jax.ShapeDtypeStruct((N_DEV,) + a.shape, a.dtype) for a in arrays],
        in_specs=[any_spec] * n, out_specs=[any_spec] * n,
        scratch_shapes=[pltpu.SemaphoreType.DMA((n, 7)), pltpu.SemaphoreType.DMA((n, 7)), pltpu.SemaphoreType.DMA((n,))],
    )(*arrays)


def _exchange_shapes(jobs):
    dsts = [jax.ShapeDtypeStruct(((N_DEV,) + a.shape) if kind == "gather" else a.shape, a.dtype) for kind, a in jobs]
    n = len(jobs)
    sems = [pltpu.SemaphoreType.DMA((n, N_DEV - 1)), pltpu.SemaphoreType.DMA((n, N_DEV - 1)), pltpu.SemaphoreType.DMA((n,))]
    return dsts, sems


def _exchange_copies(kinds, srcs, dsts, send_sems, recv_sems, local_sems):
    x, y, c = lax.axis_index("x"), lax.axis_index("y"), lax.axis_index("c")
    me = 4 * x + 2 * y + c
    remote, local = [], []
    for a, kind in enumerate(kinds):
        for k in range(1, N_DEV):
            peer = (1 - x if k & 4 else x, 1 - y if k & 2 else y, 1 - c if k & 1 else c)
            src = srcs[a] if kind == "gather" else srcs[a].at[4 * peer[0] + 2 * peer[1] + peer[2]]
            remote.append(pltpu.make_async_remote_copy(
                src_ref=src, dst_ref=dsts[a].at[me], send_sem=send_sems.at[a, k - 1], recv_sem=recv_sems.at[a, k - 1],
                device_id=peer, device_id_type=MESH))
        src = srcs[a] if kind == "gather" else srcs[a].at[me]
        local.append(pltpu.make_async_copy(src, dsts[a].at[me], local_sems.at[a]))
    return remote, local


def _scatter_copies(srcs, lands, send_sems, recv_sems):
    x, y, c = lax.axis_index("x"), lax.axis_index("y"), lax.axis_index("c")
    me = 4 * x + 2 * y + c
    copies = []
    for a, (src_ref, land_ref) in enumerate(zip(srcs, lands)):
        for k in range(1, N_DEV):
            peer = (1 - x if k & 4 else x, 1 - y if k & 2 else y, 1 - c if k & 1 else c)
            copies.append(pltpu.make_async_remote_copy(
                src_ref=src_ref.at[4 * peer[0] + 2 * peer[1] + peer[2]], dst_ref=land_ref.at[me],
                send_sem=send_sems.at[a * (N_DEV - 1) + k - 1], recv_sem=recv_sems.at[a * (N_DEV - 1) + k - 1],
                device_id=peer, device_id_type=MESH))
    return copies


_SPLIT_EFFECT = pltpu.SideEffectType.DATAFLOW_SIDE_EFFECTING


def _scatter_start(blocks, name):
    n = len(blocks)
    hbm = pl.BlockSpec(memory_space=pltpu.HBM)
    sem = pl.BlockSpec(memory_space=pltpu.SEMAPHORE)

    def body(*refs):
        srcs, lands, send_sems, recv_sems, token = refs[:n], refs[n:2 * n], refs[2 * n], refs[2 * n + 1], refs[-1]
        for cp in _scatter_copies(srcs, lands, send_sems, recv_sems):
            cp.start()
        token[...] = jnp.zeros_like(token)

    through = [pltpu.HBM(b.shape, b.dtype) for b in blocks]
    *handle, token = pl.pallas_call(
        body, name=name,
        out_shape=(pltpu.SemaphoreType.DMA((n * (N_DEV - 1),)), pltpu.SemaphoreType.DMA((n * (N_DEV - 1),)), *through, *through,
                   jax.ShapeDtypeStruct((8, 128), F32)),
        in_specs=(hbm,) * (2 * n), out_specs=(sem, sem) + (hbm,) * (2 * n) + (pl.BlockSpec(memory_space=pltpu.VMEM),),
        input_output_aliases={i: 2 + i for i in range(2 * n)},
        compiler_params=pltpu.CompilerParams(has_side_effects=_SPLIT_EFFECT),
    )(*[pltpu.with_memory_space_constraint(b, pltpu.HBM) for b in blocks],
      *[pltpu.with_memory_space_constraint(lax.empty(b.shape, b.dtype), pltpu.HBM) for b in blocks])
    return handle, token


def _scatter_wait(handle, after, name):
    hbm = pl.BlockSpec(memory_space=pltpu.HBM)
    sem = pl.BlockSpec(memory_space=pltpu.SEMAPHORE)
    send_sems, recv_sems, *through = handle
    n = len(through) // 2

    def body(*refs):
        for cp in _scatter_copies(refs[:n], refs[n:2 * n], refs[2 * n], refs[2 * n + 1]):
            cp.wait_send()
            cp.wait_recv()

    outs = pl.pallas_call(
        body, name=name, out_shape=tuple(pltpu.HBM(t.shape, t.dtype) for t in through),
        in_specs=(hbm,) * (2 * n) + (sem, sem, pl.BlockSpec(memory_space=pl.ANY)), out_specs=(hbm,) * (2 * n),
        input_output_aliases={i: i for i in range(2 * n)},
        compiler_params=pltpu.CompilerParams(has_side_effects=_SPLIT_EFFECT),
    )(*through, send_sems, recv_sems, after)
    return list(outs[n:])


def _pcall(body, *, name, grid, in_specs, out_specs, out_shape, args, scratch_shapes=(), exchange=None, after=None):
    semantics = ("arbitrary",) * len(grid)
    exchange = exchange or []
    extra = [] if after is None else [after]
    if not exchange and not extra:
        outs = pl.pallas_call(body, name=name, grid=grid, in_specs=list(in_specs), out_specs=list(out_specs),
                              out_shape=list(out_shape), scratch_shapes=list(scratch_shapes),
                              compiler_params=_cparams(semantics))(*args)
        return list(outs), []
    kinds = [kind for kind, _ in exchange]
    n_in, n_out, n_job = len(in_specs), len(out_specs), len(exchange)
    dst_shapes, sems = _exchange_shapes(exchange) if exchange else ([], [])

    def wrapped(*refs):
        ins, srcs = refs[:n_in], refs[n_in:n_in + n_job]
        o0 = n_in + n_job + len(extra)
        outs, dsts, rest = refs[o0:o0 + n_out], refs[o0 + n_out:o0 + n_out + n_job], refs[o0 + n_out + n_job:]
        if not exchange:
            body(*ins, *outs, *rest)
            return
        scratch, sem_refs = rest[:-3], rest[-3:]
        first = functools.reduce(jnp.logical_and, [pl.program_id(d) == 0 for d in range(len(grid))])
        last = functools.reduce(jnp.logical_and, [pl.program_id(d) == grid[d] - 1 for d in range(len(grid))])

        @pl.when(first)
        def _():
            remote, local = _exchange_copies(kinds, srcs, dsts, *sem_refs)
            for cp in local + remote:
                cp.start()

        body(*ins, *outs, *scratch)

        @pl.when(last)
        def _():
            remote, local = _exchange_copies(kinds, srcs, dsts, *sem_refs)
            for cp in remote + local:
                cp.wait()

    any_spec = pl.BlockSpec(memory_space=pl.ANY)
    outs = pl.pallas_call(
        wrapped, name=name, grid=grid, in_specs=list(in_specs) + [any_spec] * (n_job + len(extra)),
        out_specs=list(out_specs) + [any_spec] * n_job, out_shape=list(out_shape) + dst_shapes,
        scratch_shapes=list(scratch_shapes) + sems, compiler_params=_cparams(semantics),
    )(*args, *[a for _, a in exchange], *extra)
    return list(outs[:n_out]), list(outs[n_out:])


def _adam_math(w, g, m, v):
    m = ADAM_B1 * m + (1.0 - ADAM_B1) * g
    v = ADAM_B2 * v + (1.0 - ADAM_B2) * (g * g)
    m_hat = m / (1.0 - ADAM_B1 ** ADAM_STEP)
    v_hat = v / (1.0 - ADAM_B2 ** ADAM_STEP)
    delta = -ADAM_LR * (m_hat / (jnp.sqrt(v_hat) + ADAM_EPS) + ADAM_WD * w)
    return delta, m, v


def _adam_big(w, m, v, own32, recv16, me, after):
    r, c = w.shape
    tr = r if r <= 512 else (512 if r % 512 == 0 else 352)
    assert r % tr == 0

    def body(me_ref, w_ref, m_ref, v_ref, s_ref, r_ref, after_ref, g_out, d_out, m_out, v_out):
        g = s_ref[...]
        for k in range(1, N_DEV):
            g = g + r_ref[jnp.bitwise_xor(me_ref[0], k)].astype(F32)
        d, mn, vn = _adam_math(w_ref[...], g, m_ref[...], v_ref[...])
        g_out[...] = g
        d_out[...] = d
        m_out[...] = mn
        v_out[...] = vn

    blk = pl.BlockSpec((tr, c), lambda i, s: (i, 0))
    grid_spec = pltpu.PrefetchScalarGridSpec(
        num_scalar_prefetch=1, grid=(r // tr,),
        in_specs=[blk, blk, blk, blk, pl.BlockSpec((N_DEV, tr, c), lambda i, s: (0, i, 0)), pl.BlockSpec(memory_space=pl.ANY)],
        out_specs=[blk, blk, blk, blk])
    return pl.pallas_call(
        body, name="adamw_big", grid_spec=grid_spec,
        out_shape=[jax.ShapeDtypeStruct((r, c), F32)] * 4,
        compiler_params=_cparams(("arbitrary",)),
    )(me, w, m, v, own32, recv16, after)


def _adam_small(ws, ms, vs, partials, loss_parts):
    n = len(ws)

    def body(*refs):
        w_refs, m_refs, v_refs, p_refs = (refs[i * n:(i + 1) * n] for i in range(4))
        l_ref, loss_out = refs[4 * n], refs[4 * n + 1]
        outs = refs[4 * n + 2:]

        def total(ref):
            s = ref[0]
            for dev in range(1, N_DEV):
                s = s + ref[dev]
            return s

        loss_out[...] = total(l_ref)
        for i in range(n):
            r, c = w_refs[i].shape
            g = total(p_refs[i])[:, :c]
            d, mn, vn = _adam_math(w_refs[i][...], g, m_refs[i][...], v_refs[i][...])
            for j, val in enumerate((g, d, mn, vn)):
                outs[4 * i + j][...] = val

    shapes = [jax.ShapeDtypeStruct((1, 128), F32)]
    for w in ws:
        shapes += [jax.ShapeDtypeStruct(w.shape, F32)] * 4
    outs = pl.pallas_call(body, name="adamw_small", out_shape=shapes, compiler_params=_cparams())(*ws, *ms, *vs, *partials, loss_parts)
    return outs[0], [outs[1 + 4 * i:5 + 4 * i] for i in range(n)]


def _pick(n, prefer):
    for t in prefer:
        if n % t == 0:
            return t
    return n


def _ffn_fwd(x, g, wg, wu, wd, name, exchange=None, loss_head=None):
    T, D = x.shape
    F = wg.shape[0]
    tm = _pick(T, (512, 256, 128))
    fc = _pick(F, (256, 128))

    def body(x_ref, g_ref, wg_ref, wu_ref, wd_ref, *rest):
        h_ref, n_ref, gate_ref, up_ref, act_ref = rest[-7:-2] if loss_head else rest
        xv = x_ref[...]
        n, _ = _rms_fwd(xv, g_ref[...])
        nb = n.astype(BF16)
        n_ref[...] = nb
        acc = jnp.zeros((tm, D), F32)
        for f0 in range(0, F, fc):
            gate = _dot_nt(nb, wg_ref[f0:f0 + fc, :])
            up = _dot_nt(nb, wu_ref[f0:f0 + fc, :])
            gate_ref[:, f0:f0 + fc] = gate.astype(BF16)
            up_ref[:, f0:f0 + fc] = up.astype(BF16)
            act = (gate * (1.0 / (1.0 + jnp.exp(-gate))) * up).astype(BF16)
            act_ref[:, f0:f0 + fc] = act
            acc = acc + _dot(act, wd_ref[f0:f0 + fc, :])
        hv = xv + 0.5 * acc
        if not loss_head:
            h_ref[...] = hv
            return
        t_ref, gf_ref, loss_ref, dgf_ref = rest[0], rest[1], rest[-2], rest[-1]
        gf = gf_ref[...]
        y, r = _rms_fwd(hv, gf)
        diff = y - t_ref[...]
        dh, dgp = _rms_bwd(hv, gf, r, diff * (1.0 / D))
        h_ref[...] = dh
        part = 0.5 * jnp.sum(jnp.sum(diff * diff, axis=-1, keepdims=True) * (1.0 / D), axis=0, keepdims=True)

        @pl.when(pl.program_id(0) == 0)
        def _():
            loss_ref[...] = jnp.zeros_like(loss_ref)
            dgf_ref[...] = jnp.zeros_like(dgf_ref)

        loss_ref[...] += jnp.broadcast_to(part, loss_ref.shape)
        dgf_ref[...] += dgp

    in_specs = [_row_spec(tm, D), _const_spec((1, D)), _const_spec((F, D)), _const_spec((F, D)), _const_spec((F, D))]
    out_specs = [_row_spec(tm, D), _row_spec(tm, D), _row_spec(tm, F), _row_spec(tm, F), _row_spec(tm, F)]
    out_shape = [jax.ShapeDtypeStruct((T, D), F32), jax.ShapeDtypeStruct((T, D), BF16),
                 jax.ShapeDtypeStruct((T, F), BF16), jax.ShapeDtypeStruct((T, F), BF16), jax.ShapeDtypeStruct((T, F), BF16)]
    args = (x, g, wg, wu, wd)
    if loss_head:
        in_specs += [_row_spec(tm, D), _const_spec((1, D))]
        out_specs += [_acc_spec((1, 128)), _acc_spec((1, D))]
        out_shape += [jax.ShapeDtypeStruct((1, 128), F32), jax.ShapeDtypeStruct((1, D), F32)]
        args += tuple(loss_head)
    return _pcall(body, name=name, grid=(T // tm,), in_specs=in_specs, out_specs=out_specs, out_shape=out_shape, args=args,
                  exchange=exchange)


FFN_BWD_CHUNK = 1408
FFN_BWD_GROUP = 1408


def _ffn_bwd(x, g, gate, up, dh, wgt, wut, wdt, name, exchange=None, after=None):
    T, D = x.shape
    F = wgt.shape[0]
    tm = _pick(T, (512, 256, 128))
    halves = 2
    Fh = F // halves
    assert Fh * halves == F and Fh % 128 == 0
    groups = [(g0, min(g0 + FFN_BWD_GROUP, Fh)) for g0 in range(0, Fh, FFN_BWD_GROUP)]

    def body(x_ref, g_ref, gate_ref, up_ref, dh_ref, wg_ref, wu_ref, wd_ref, dx_ref, dg_ref, dgate_ref, dup_ref, dn_ref):
        i, h = pl.program_id(0), pl.program_id(1)
        dhv = dh_ref[...]
        dho = (0.5 * dhv).astype(BF16)
        dn = jnp.zeros((tm, D), F32)
        for g0, g1 in groups:
            for f0 in range(g0, g1, FFN_BWD_CHUNK):
                f1 = min(f0 + FFN_BWD_CHUNK, g1)
                d_act = _dot_nt(dho, wd_ref[f0:f1, :])
                gt = gate_ref[:, f0:f1].astype(F32)
                u = up_ref[:, f0:f1].astype(F32)
                sg = 1.0 / (1.0 + jnp.exp(-gt))
                silu = gt * sg
                dup_ref[:, f0:f1] = (d_act * silu).astype(BF16)
                dgate_ref[:, f0:f1] = (d_act * u * (sg + silu * (1.0 - sg))).astype(BF16)
            dn = dn + _dot(dgate_ref[:, g0:g1], wg_ref[g0:g1, :]) + _dot(dup_ref[:, g0:g1], wu_ref[g0:g1, :])

        @pl.when(h == 0)
        def _():
            dn_ref[...] = dn

        @pl.when(h > 0)
        def _():
            dn_ref[...] += dn

        @pl.when((i == 0) & (h == 0))
        def _():
            dg_ref[...] = jnp.zeros_like(dg_ref)

        @pl.when(h == halves - 1)
        def _():
            xv, gv = x_ref[...], g_ref[...]
            _, r = _rms_fwd(xv, gv)
            dxn, dgp = _rms_bwd(xv, gv, r, dn_ref[...])
            dx_ref[...] = dhv + dxn
            dg_ref[...] += dgp

    tile = pl.BlockSpec((tm, D), lambda i, h: (i, 0))
    wide = pl.BlockSpec((tm, Fh), lambda i, h: (i, h))
    return _pcall(
        body, name=name, grid=(T // tm, halves),
        in_specs=[tile, _const_spec((1, D)), wide, wide, tile,
                  pl.BlockSpec((Fh, D), lambda i, h: (h, 0)), pl.BlockSpec((Fh, D), lambda i, h: (h, 0)),
                  pl.BlockSpec((Fh, D), lambda i, h: (h, 0))],
        out_specs=[tile, _acc_spec((1, D)), wide, wide],
        out_shape=[jax.ShapeDtypeStruct((T, D), F32), jax.ShapeDtypeStruct((1, D), F32),
                   jax.ShapeDtypeStruct((T, F), BF16), jax.ShapeDtypeStruct((T, F), BF16)],
        scratch_shapes=[pltpu.VMEM((tm, D), F32)],
        args=(x, g, gate, up, dh, wgt, wut, wdt), exchange=exchange, after=after)


def _matmul_tn(a, b, name, scale=None, exchange=None, after=None):
    T, K = a.shape
    N = b.shape[1]
    tk = _pick(K, (1024, 1408, 1280, 768, 512, 256, 128))
    tn = _pick(N, (1024, 1408, 1280, 768, 512, 256, 128))
    tt = _pick(T, (2048, 1024, 512, 256, 128))
    n_t = T // tt

    def body(a_ref, b_ref, o_ref, o16_ref):
        @pl.when(pl.program_id(2) == 0)
        def _():
            o_ref[...] = jnp.zeros_like(o_ref)

        o_ref[...] += _dot_tn(a_ref[...].astype(BF16), b_ref[...].astype(BF16))

        @pl.when(pl.program_id(2) == n_t - 1)
        def _():
            if scale is not None:
                o_ref[...] = o_ref[...] * scale
            o16_ref[...] = o_ref[...].astype(BF16)

    tile = pl.BlockSpec((tk, tn), lambda i, j, t: (i, j))
    outs, got = _pcall(
        body, name=name, grid=(K // tk, N // tn, n_t),
        in_specs=[pl.BlockSpec((tt, tk), lambda i, j, t: (t, i)), pl.BlockSpec((tt, tn), lambda i, j, t: (t, j))],
        out_specs=[tile, tile],
        out_shape=[jax.ShapeDtypeStruct((K, N), F32), jax.ShapeDtypeStruct((K, N), BF16)],
        args=(a, b), exchange=exchange, after=after)
    return tuple(outs), got


def _rope_spec(tm, S):
    per = S // tm
    return pl.BlockSpec((tm, HEAD_BLOCK), lambda i: (i % per, 0))


def _proj_fwd(h1, g_mix, w_in, g_q, wq, g_kv, wkv, rope_c, rope_s, S):
    T, D = h1.shape
    P = w_in.shape[0]
    tm = _pick(S, (512, 256, 128))
    QW = MLA_HEADS * HEAD_BLOCK
    VW = MLA_HEADS * V_HEAD
    SQ = SWA_HEADS * SWA_HEAD_DIM
    SK = SWA_KV_HEADS * SWA_HEAD_DIM
    o_cq, o_ckv, o_qs, o_ks, o_vs, o_kpe = 0, Q_LORA, Q_LORA + KV_LORA, Q_LORA + KV_LORA + SQ, Q_LORA + KV_LORA + SQ + SK, Q_LORA + KV_LORA + SQ + 2 * SK
    assert P == o_kpe + HEAD_BLOCK

    def body(h_ref, gm_ref, win_ref, gq_ref, wq_ref, gkv_ref, wkv_ref, c_ref, s_ref,
             u_ref, cq_ref, ckv_ref, cqn_ref, ckvn_ref, q_ref, k_ref, v_ref, qs_ref, ks_ref, vs_ref):
        u, _ = _rms_fwd(h_ref[...], gm_ref[...])
        ub = u.astype(BF16)
        u_ref[...] = ub
        proj = _dot_nt(ub, win_ref[...])
        c_q = proj[:, o_cq:o_ckv]
        c_kv = proj[:, o_ckv:o_qs]
        cq_ref[...] = c_q
        ckv_ref[...] = c_kv
        qs_ref[...] = proj[:, o_qs:o_ks].astype(BF16)
        ks_ref[...] = proj[:, o_ks:o_vs].astype(BF16)
        vs_ref[...] = proj[:, o_vs:o_kpe].astype(BF16)
        cb, sb = c_ref[...], s_ref[...]
        kpe = proj[:, o_kpe:P]
        kpe = kpe * cb + _swap16(kpe) * sb
        cqn, _ = _rms_fwd(c_q, gq_ref[...])
        cqn = cqn.astype(BF16)
        cqn_ref[...] = cqn
        q = _dot_nt(cqn, wq_ref[...])
        q = q * jnp.tile(cb, (1, MLA_HEADS)) + _swap16(q) * jnp.tile(sb, (1, MLA_HEADS))
        q_ref[...] = q.astype(BF16)
        ckvn, _ = _rms_fwd(c_kv, gkv_ref[...])
        ckvn = ckvn.astype(BF16)
        ckvn_ref[...] = ckvn
        kv = _dot_nt(ckvn, wkv_ref[...])
        k_ref[...] = (kv[:, :QW] + jnp.tile(kpe, (1, MLA_HEADS))).astype(BF16)
        v_ref[...] = kv[:, QW:].astype(BF16)

    widths = [(D, BF16), (Q_LORA, F32), (KV_LORA, F32), (Q_LORA, BF16), (KV_LORA, BF16), (QW, BF16), (QW, BF16), (VW, BF16),
              (SQ, BF16), (SK, BF16), (SK, BF16)]
    return pl.pallas_call(
        body, name="proj_fwd", grid=(T // tm,),
        in_specs=[_row_spec(tm, D), _const_spec((1, D)), _const_spec((P, D)), _const_spec((1, Q_LORA)),
                  _const_spec((QW, Q_LORA)), _const_spec((1, KV_LORA)), _const_spec((QW + VW, KV_LORA)),
                  _rope_spec(tm, S), _rope_spec(tm, S)],
        out_specs=[_row_spec(tm, w) for w, _ in widths],
        out_shape=[jax.ShapeDtypeStruct((T, w), dt) for w, dt in widths],
        compiler_params=_cparams(("arbitrary",)),
    )(h1, g_mix, w_in, g_q, wq, g_kv, wkv, rope_c, rope_s)


def _proj_bwd(dq, dk, dv, dqs, dks, dvs, c_q, c_kv, h1, dh2, g_mix, w_in, g_q, wq, g_kv, wkv, rope_c, rope_s, S):
    T, D = h1.shape
    P = w_in.shape[0]
    tm = _pick(S, (256, 128))
    QW = MLA_HEADS * HEAD_BLOCK
    VW = MLA_HEADS * V_HEAD
    SQ = SWA_HEADS * SWA_HEAD_DIM
    SK = SWA_KV_HEADS * SWA_HEAD_DIM

    def body(dq_ref, dk_ref, dv_ref, dqs_ref, dks_ref, dvs_ref, cq_ref, ckv_ref, h_ref, dh2_ref, gm_ref, win_ref, gq_ref,
             wq_ref, gkv_ref, wkv_ref, c_ref, s_ref, dh1_ref, dproj_ref, dql_ref, dkv_ref, dgm_ref, dgq_ref, dgkv_ref):
        cb, sb = c_ref[...], s_ref[...]
        dqv = dq_ref[...]
        dql = dqv * jnp.tile(cb, (1, MLA_HEADS)) + _swap16(dqv * jnp.tile(sb, (1, MLA_HEADS)))
        dql = dql.astype(BF16)
        dql_ref[...] = dql
        c_q = cq_ref[...]
        _, rq = _rms_fwd(c_q, gq_ref[...])
        d_cq, dgq = _rms_bwd(c_q, gq_ref[...], rq, _dot(dql, wq_ref[...]))

        dkv_all = dk_ref[...]
        dkpe = dkv_all[:, 0:HEAD_BLOCK]
        for h in range(1, MLA_HEADS):
            dkpe = dkpe + dkv_all[:, h * HEAD_BLOCK:(h + 1) * HEAD_BLOCK]
        lane = lax.broadcasted_iota(jnp.int32, dkpe.shape, 1)
        dkpe = jnp.where((lane >= 64) & (lane < 96), dkpe, 0.0)
        dkpe = dkpe * cb + _swap16(dkpe * sb)
        dkvc = jnp.concatenate([dkv_all.astype(BF16), dv_ref[...].astype(BF16)], axis=1)
        dkv_ref[...] = dkvc
        c_kv = ckv_ref[...]
        _, rkv = _rms_fwd(c_kv, gkv_ref[...])
        d_ckv, dgkv = _rms_bwd(c_kv, gkv_ref[...], rkv, _dot(dkvc, wkv_ref[...]))

        dproj = jnp.concatenate([d_cq.astype(BF16), d_ckv.astype(BF16), dqs_ref[...].astype(BF16),
                                 dks_ref[...].astype(BF16), dvs_ref[...].astype(BF16), dkpe.astype(BF16)], axis=1)
        dproj_ref[...] = dproj
        hv = h_ref[...]
        _, rm = _rms_fwd(hv, gm_ref[...])
        dxn, dgm = _rms_bwd(hv, gm_ref[...], rm, _dot(dproj, win_ref[...]))
        dh1_ref[...] = dh2_ref[...] + dxn

        @pl.when(pl.program_id(0) == 0)
        def _():
            dgm_ref[...] = jnp.zeros_like(dgm_ref)
            dgq_ref[...] = jnp.zeros_like(dgq_ref)
            dgkv_ref[...] = jnp.zeros_like(dgkv_ref)

        dgm_ref[...] += dgm
        dgq_ref[...] += dgq
        dgkv_ref[...] += dgkv

    return pl.pallas_call(
        body, name="proj_bwd", grid=(T // tm,),
        in_specs=[_row_spec(tm, QW), _row_spec(tm, QW), _row_spec(tm, VW), _row_spec(tm, SQ), _row_spec(tm, SK), _row_spec(tm, SK),
                  _row_spec(tm, Q_LORA), _row_spec(tm, KV_LORA), _row_spec(tm, D), _row_spec(tm, D),
                  _const_spec((1, D)), _const_spec((P, D)), _const_spec((1, Q_LORA)), _const_spec((QW, Q_LORA)),
                  _const_spec((1, KV_LORA)), _const_spec((QW + VW, KV_LORA)), _rope_spec(tm, S), _rope_spec(tm, S)],
        out_specs=[_row_spec(tm, D), _row_spec(tm, P), _row_spec(tm, QW), _row_spec(tm, QW + VW),
                   _acc_spec((1, D)), _acc_spec((1, Q_LORA)), _acc_spec((1, KV_LORA))],
        out_shape=[jax.ShapeDtypeStruct((T, D), F32), jax.ShapeDtypeStruct((T, P), BF16), jax.ShapeDtypeStruct((T, QW), BF16),
                   jax.ShapeDtypeStruct((T, QW + VW), BF16), jax.ShapeDtypeStruct((1, D), F32),
                   jax.ShapeDtypeStruct((1, Q_LORA), F32), jax.ShapeDtypeStruct((1, KV_LORA), F32)],
        compiler_params=_cparams(("arbitrary",)),
    )(dq, dk, dv, dqs, dks, dvs, c_q, c_kv, h1, dh2, g_mix, w_in, g_q, wq, g_kv, wkv, rope_c, rope_s)


def _out_fwd(o_mla, o_swa, g_mla, g_swa, w_o, h1):
    T, D = h1.shape
    W = o_mla.shape[1]
    tm = _pick(T, (512, 256, 128))

    def body(om_ref, os_ref, gm_ref, gs_ref, wo_ref, h_ref, h2_ref, oc_ref):
        a, _ = _rms_fwd(om_ref[...], gm_ref[...])
        b, _ = _rms_fwd(os_ref[...], gs_ref[...])
        oc = jnp.concatenate([a.astype(BF16), b.astype(BF16)], axis=1)
        oc_ref[...] = oc
        h2_ref[...] = h_ref[...] + _dot(oc, wo_ref[...])

    return pl.pallas_call(
        body, name="out_fwd", grid=(T // tm,),
        in_specs=[_row_spec(tm, W), _row_spec(tm, W), _const_spec((1, W)), _const_spec((1, W)), _const_spec((2 * W, D)),
                  _row_spec(tm, D)],
        out_specs=[_row_spec(tm, D), _row_spec(tm, 2 * W)],
        out_shape=[jax.ShapeDtypeStruct((T, D), F32), jax.ShapeDtypeStruct((T, 2 * W), BF16)],
        compiler_params=_cparams(("arbitrary",)),
    )(o_mla, o_swa, g_mla, g_swa, w_o, h1)


def _out_bwd(dh2, o_mla, o_swa, g_mla, g_swa, w_o):
    T, D = dh2.shape
    W = o_mla.shape[1]
    tm = _pick(T, (512, 256, 128))

    def body(dh_ref, om_ref, os_ref, gm_ref, gs_ref, wo_ref, dom_ref, dos_ref, dgm_ref, dgs_ref):
        doc = _dot_nt(dh_ref[...].astype(BF16), wo_ref[...])
        om, osw = om_ref[...], os_ref[...]
        _, ra = _rms_fwd(om, gm_ref[...])
        _, rb = _rms_fwd(osw, gs_ref[...])
        da, dga = _rms_bwd(om, gm_ref[...], ra, doc[:, :W])
        db, dgb = _rms_bwd(osw, gs_ref[...], rb, doc[:, W:])
        dom_ref[...] = da
        dos_ref[...] = db

        @pl.when(pl.program_id(0) == 0)
        def _():
            dgm_ref[...] = jnp.zeros_like(dgm_ref)
            dgs_ref[...] = jnp.zeros_like(dgs_ref)

        dgm_ref[...] += dga
        dgs_ref[...] += dgb

    return pl.pallas_call(
        body, name="out_bwd", grid=(T // tm,),
        in_specs=[_row_spec(tm, D), _row_spec(tm, W), _row_spec(tm, W), _const_spec((1, W)), _const_spec((1, W)),
                  _const_spec((2 * W, D))],
        out_specs=[_row_spec(tm, W), _row_spec(tm, W), _acc_spec((1, W)), _acc_spec((1, W))],
        out_shape=[jax.ShapeDtypeStruct((T, W), F32), jax.ShapeDtypeStruct((T, W), F32),
                   jax.ShapeDtypeStruct((1, W), F32), jax.ShapeDtypeStruct((1, W), F32)],
        compiler_params=_cparams(("arbitrary",)),
    )(dh2, o_mla, o_swa, g_mla, g_swa, w_o)


def _half_mask(shape, half):
    lane = lax.broadcasted_iota(jnp.int32, shape, len(shape) - 1)
    return (lane < 64) if half == 0 else (lane >= 64)


def _mla_fwd(q, k, v, B, S, exchange=None):
    T = B * S
    tq = _pick(S, MLA_Q_TILES)
    nq = S // tq
    pairs = MLA_HEADS // 2

    def body(q_ref, k_ref, v_ref, o_ref, lse_ref):
        causal = lax.broadcasted_iota(jnp.int32, (tq, tq), 1) <= lax.broadcasted_iota(jnp.int32, (tq, tq), 0)
        low = _half_mask((tq, 2 * V_HEAD), 0)
        for qi in range(nq):
            rows = slice(qi * tq, (qi + 1) * tq)
            past = slice(0, qi * tq)
            outs, lses = [], []
            for half in range(2):
                lanes = slice(half * HEAD_BLOCK, (half + 1) * HEAD_BLOCK)
                qv = q_ref[rows, lanes]
                s_d = jnp.where(causal, _dot_nt(qv, k_ref[rows, lanes]) * MLA_SCALE_LOG2, NEG)
                m = jnp.max(s_d, axis=-1, keepdims=True)
                if qi:
                    s_p = _dot_nt(qv, k_ref[past, lanes]) * MLA_SCALE_LOG2
                    m = jnp.maximum(m, jnp.max(s_p, axis=-1, keepdims=True))
                p_d = jnp.exp2(s_d - m)
                l = jnp.sum(p_d, axis=-1, keepdims=True)
                acc = _dot(p_d.astype(BF16), v_ref[rows, :])
                if qi:
                    p_p = jnp.exp2(s_p - m)
                    l = l + jnp.sum(p_p, axis=-1, keepdims=True)
                    acc = acc + _dot(p_p.astype(BF16), v_ref[past, :])
                outs.append(acc * (1.0 / l))
                lses.append(jnp.broadcast_to(m + jnp.log2(l), (tq, 2 * V_HEAD)))
            o_ref[rows, :] = jnp.where(low, outs[0], outs[1])
            lse_ref[rows, :] = jnp.where(low, lses[0], lses[1])

    blk2 = pl.BlockSpec((S, 2 * HEAD_BLOCK), lambda b, j: (b, j))
    blk1 = pl.BlockSpec((S, 2 * V_HEAD), lambda b, j: (b, j))
    return _pcall(
        body, name="mla_fwd", grid=(B, pairs),
        in_specs=[blk2, blk2, blk1], out_specs=[blk1, blk1],
        out_shape=[jax.ShapeDtypeStruct((T, pairs * 2 * V_HEAD), F32)] * 2,
        args=(q, k, v), exchange=exchange)


def _mla_bwd(q, k, v, do, o, lse, B, S, exchange=None):
    T = B * S
    tq = _pick(S, MLA_Q_TILES)
    nq = S // tq
    pairs = MLA_HEADS // 2

    def body(q_ref, k_ref, v_ref, do_ref, o_ref, lse_ref, dq_ref, dk_ref, dv_ref, dk_acc):
        causal = lax.broadcasted_iota(jnp.int32, (tq, tq), 1) <= lax.broadcasted_iota(jnp.int32, (tq, tq), 0)
        for half in range(2):
            lanes = slice(half * HEAD_BLOCK, (half + 1) * HEAD_BLOCK)
            col = half * V_HEAD
            own = _half_mask((tq, 2 * V_HEAD), half)
            for qi in reversed(range(nq)):
                rows = slice(qi * tq, (qi + 1) * tq)
                past = slice(0, qi * tq)
                first = qi == nq - 1
                qv = q_ref[rows, lanes]
                dov = jnp.where(own, do_ref[rows, :], 0.0)
                dsum = jnp.sum(dov * o_ref[rows, :], axis=-1, keepdims=True)
                dob = dov.astype(BF16)
                lse_col = lse_ref[rows, col:col + 1]

                def grads(keys, mask):
                    kv, vv = k_ref[keys, lanes], v_ref[keys, :]
                    p = jnp.exp2(_dot_nt(qv, kv) * MLA_SCALE_LOG2 - lse_col)
                    if mask is not None:
                        p = jnp.where(mask, p, 0.0)
                    ds = (p * (_dot_nt(dob, vv) - dsum) * MLA_SCALE).astype(BF16)
                    dk, dv = _dot_tn(ds, qv), _dot_tn(p.astype(BF16), dob)
                    if first:
                        dk_acc[keys, :] = dk
                    else:
                        dk_acc[keys, :] += dk
                    if first and half == 0:
                        dv_ref[keys, :] = dv
                    else:
                        dv_ref[keys, :] += dv
                    return _dot(ds, kv)

                dq = grads(rows, causal)
                if qi:
                    dq = dq + grads(past, None)
                dq_ref[rows, lanes] = dq
            dk_ref[:, lanes] = dk_acc[...]

    blk2 = pl.BlockSpec((S, 2 * HEAD_BLOCK), lambda b, j: (b, j))
    blk1 = pl.BlockSpec((S, 2 * V_HEAD), lambda b, j: (b, j))
    return _pcall(
        body, name="mla_bwd", grid=(B, pairs),
        in_specs=[blk2, blk2, blk1, blk1, blk1, blk1], out_specs=[blk2, blk2, blk1],
        out_shape=[jax.ShapeDtypeStruct((T, MLA_HEADS * HEAD_BLOCK), F32), jax.ShapeDtypeStruct((T, MLA_HEADS * HEAD_BLOCK), F32),
                   jax.ShapeDtypeStruct((T, pairs * 2 * V_HEAD), F32)],
        scratch_shapes=[pltpu.VMEM((S, HEAD_BLOCK), F32)],
        args=(q, k, v, do, o, lse), exchange=exchange)


SWA_GROUP = SWA_HEADS // SWA_KV_HEADS
SWA_ROWS = SWA_GROUP * WINDOW


SWA_CHUNK = 8


def _bdot(a, b, ca, cb):
    return lax.dot_general(a, b, (((ca,), (cb,)), ((0,), (0,))), preferred_element_type=F32)


def _swa_chunks(nb):
    cb = next(c for c in (SWA_CHUNK, 4, 2, 1) if nb % c == 0)
    return cb, [(n0, slice(n0 * WINDOW, (n0 + cb) * WINDOW)) for n0 in range(0, nb, cb)]


def _swa_keys(ref, n0, cb):
    W = WINDOW
    cur = ref[n0 * W:(n0 + cb) * W, :]
    if n0:
        prev = ref[(n0 - 1) * W:(n0 + cb - 1) * W, :]
    else:
        prev = jnp.concatenate([cur[:W], cur[:(cb - 1) * W]], axis=0) if cb > 1 else cur
    return jnp.concatenate([prev.reshape(cb, W, W), cur.reshape(cb, W, W)], axis=1)


def _swa_stack(ref, rows, cb):
    return jnp.concatenate([ref[rows, j * WINDOW:(j + 1) * WINDOW].reshape(cb, WINDOW, WINDOW) for j in range(SWA_GROUP)], axis=1)


def _swa_unstack(low, high, cb):
    sel = _half_mask((cb, WINDOW, WINDOW), 0)
    return jnp.concatenate([jnp.where(sel, low[:, j * WINDOW:(j + 1) * WINDOW], high[:, j * WINDOW:(j + 1) * WINDOW])
                            .reshape(cb * WINDOW, WINDOW) for j in range(SWA_GROUP)], axis=1)


def _swa_sink_col(sink_ref, g):
    return jnp.concatenate([jnp.broadcast_to(sink_ref[g * SWA_GROUP + j:g * SWA_GROUP + j + 1, 0:1], (WINDOW, 1))
                            for j in range(SWA_GROUP)], axis=0)


def _swa_probs(qm, kcat, bias_g, sink, first_chunk):
    shape = (qm.shape[0], SWA_ROWS, 2 * WINDOW)
    qi = lax.broadcasted_iota(jnp.int32, shape, 1) % WINDOW
    col = lax.broadcasted_iota(jnp.int32, shape, 2)
    valid = (col > qi) & (col <= qi + WINDOW)
    if first_chunk:
        valid = valid & ((col >= WINDOW) | (lax.broadcasted_iota(jnp.int32, shape, 0) > 0))
    s = jnp.where(valid, _bdot(qm, kcat, 2, 2) * SWA_SCALE + bias_g, NEG)
    m = jnp.maximum(jnp.max(s, axis=-1, keepdims=True), sink)
    e = jnp.exp(s - m)
    e_s = jnp.exp(sink - m)
    inv = 1.0 / (jnp.sum(e, axis=-1, keepdims=True) + e_s)
    return e * inv, e_s * inv


def _swa_fwd(q, k, v, bias, sinks, B, S):
    T = B * S
    W = WINDOW
    nb = S // W

    cb, chunks = _swa_chunks(nb)

    def body(q_ref, k_ref, v_ref, bias_ref, sink_ref, o_ref):
        sink_cols = [_swa_sink_col(sink_ref, g) for g in range(SWA_KV_HEADS)]
        for n0, rows in chunks:
            kcat, vcat = _swa_keys(k_ref, n0, cb), _swa_keys(v_ref, n0, cb)
            qcat = _swa_stack(q_ref, rows, cb)
            outs = []
            for g in range(SWA_KV_HEADS):
                qm = jnp.where(_half_mask(qcat.shape, g), qcat, jnp.zeros_like(qcat))
                p, _ = _swa_probs(qm, kcat, bias_ref[g], sink_cols[g], n0 == 0)
                outs.append(_bdot(p.astype(BF16), vcat, 2, 1))
            o_ref[rows, :] = _swa_unstack(outs[0], outs[1], cb)

    SQ, SK = SWA_HEADS * SWA_HEAD_DIM, SWA_KV_HEADS * SWA_HEAD_DIM
    return pl.pallas_call(
        body, name="swa_fwd", grid=(B,),
        in_specs=[pl.BlockSpec((S, SQ), lambda b: (b, 0)), pl.BlockSpec((S, SK), lambda b: (b, 0)),
                  pl.BlockSpec((S, SK), lambda b: (b, 0)), _const_spec((SWA_KV_HEADS, SWA_ROWS, 2 * W)),
                  _const_spec((SWA_HEADS, 128))],
        out_specs=pl.BlockSpec((S, SQ), lambda b: (b, 0)),
        out_shape=jax.ShapeDtypeStruct((T, SQ), F32),
        compiler_params=_cparams(("arbitrary",)),
    )(q, k, v, bias, sinks)


def _swa_bwd(q, k, v, do, bias, sinks, B, S):
    T = B * S
    W = WINDOW
    nb = S // W
    cb, chunks = _swa_chunks(nb)

    def body(q_ref, k_ref, v_ref, do_ref, bias_ref, sink_ref, dq_ref, dk_ref, dv_ref, dbias_ref, dsink_ref):
        @pl.when(pl.program_id(0) == 0)
        def _():
            dbias_ref[...] = jnp.zeros_like(dbias_ref)
            dsink_ref[...] = jnp.zeros_like(dsink_ref)

        dk_ref[...] = jnp.zeros_like(dk_ref)
        dv_ref[...] = jnp.zeros_like(dv_ref)
        sink_cols = [_swa_sink_col(sink_ref, g) for g in range(SWA_KV_HEADS)]
        for n0, rows in chunks:
            kcat, vcat = _swa_keys(k_ref, n0, cb), _swa_keys(v_ref, n0, cb)
            qcat = _swa_stack(q_ref, rows, cb)
            docat = _swa_stack(do_ref, rows, cb)
            dk = jnp.zeros((cb, 2 * W, W), F32)
            dv = jnp.zeros((cb, 2 * W, W), F32)
            dqs = []
            for g in range(SWA_KV_HEADS):
                own = _half_mask(qcat.shape, g)
                qm = jnp.where(own, qcat, jnp.zeros_like(qcat))
                dom = jnp.where(own, docat, 0.0).astype(BF16)
                p, p_s = _swa_probs(qm, kcat, bias_ref[g], sink_cols[g], n0 == 0)
                dp = _bdot(dom, vcat, 2, 2)
                dsum = jnp.sum(p * dp, axis=-1, keepdims=True)
                ds = p * (dp - dsum)
                dbias_ref[g] += jnp.sum(ds, axis=0)
                dsink_ref[g] += jnp.broadcast_to(-jnp.sum(p_s * dsum, axis=0), (SWA_ROWS, 128))
                dsb = (ds * SWA_SCALE).astype(BF16)
                dqs.append(_bdot(dsb, kcat, 2, 1))
                dk = dk + _bdot(dsb, qm, 1, 1)
                dv = dv + _bdot(p.astype(BF16), dom, 1, 1)
            dq_ref[rows, :] = _swa_unstack(dqs[0], dqs[1], cb)
            dk_ref[rows, :] += dk[:, W:].reshape(cb * W, W)
            dv_ref[rows, :] += dv[:, W:].reshape(cb * W, W)
            dk_prev, dv_prev = dk[:, :W].reshape(cb * W, W), dv[:, :W].reshape(cb * W, W)
            if n0:
                before = slice((n0 - 1) * W, (n0 + cb - 1) * W)
                dk_ref[before, :] += dk_prev
                dv_ref[before, :] += dv_prev
            elif cb > 1:
                before = slice(0, (cb - 1) * W)
                dk_ref[before, :] += dk_prev[W:]
                dv_ref[before, :] += dv_prev[W:]

    SQ, SK = SWA_HEADS * SWA_HEAD_DIM, SWA_KV_HEADS * SWA_HEAD_DIM
    row = lambda w: pl.BlockSpec((S, w), lambda b: (b, 0))
    return pl.pallas_call(
        body, name="swa_bwd", grid=(B,),
        in_specs=[row(SQ), row(SK), row(SK), row(SQ), _const_spec((SWA_KV_HEADS, SWA_ROWS, 2 * W)), _const_spec((SWA_HEADS, 128))],
        out_specs=[row(SQ), row(SK), row(SK), _acc_spec((SWA_KV_HEADS, SWA_ROWS, 2 * W)), _acc_spec((SWA_KV_HEADS, SWA_ROWS, 128))],
        out_shape=[jax.ShapeDtypeStruct((T, SQ), F32), jax.ShapeDtypeStruct((T, SK), F32), jax.ShapeDtypeStruct((T, SK), F32),
                   jax.ShapeDtypeStruct((SWA_KV_HEADS, SWA_ROWS, 2 * W), F32), jax.ShapeDtypeStruct((SWA_KV_HEADS, SWA_ROWS, 128), F32)],
        compiler_params=_cparams(("arbitrary",)),
    )(q, k, v, do, bias, sinks)


def _bias_expand(rel_bias, bucket):
    W = WINDOW

    def body(rb_ref, bucket_ref, out_ref):
        bk = bucket_ref[...]
        for h in range(SWA_HEADS):
            def add(b, acc):
                return jnp.where(bk == b, rb_ref[b, h], acc)

            out_ref[h] = lax.fori_loop(0, REL_BUCKETS, add, jnp.zeros((W, 2 * W), F32))

    return pl.pallas_call(
        body, name="bias_expand",
        in_specs=[pl.BlockSpec(memory_space=pltpu.SMEM), pl.BlockSpec(memory_space=pltpu.VMEM)],
        out_specs=pl.BlockSpec(memory_space=pltpu.VMEM),
        out_shape=jax.ShapeDtypeStruct((SWA_HEADS, W, 2 * W), F32),
        compiler_params=_cparams(),
    )(rel_bias, bucket)


def _bias_reduce(dbias, dsink_rows, bucket):
    def body(db_ref, ds_ref, bucket_ref, out_ref, sink_out_ref, rows_ref):
        sink_lane = lax.broadcasted_iota(jnp.int32, (1, 128), 1)
        sink_acc = jnp.zeros((1, 128), F32)
        for h in range(SWA_HEADS):
            sink_acc = jnp.where(sink_lane == h, jnp.sum(ds_ref[h], axis=0, keepdims=True), sink_acc)
        sink_out_ref[...] = sink_acc
        bk = bucket_ref[...]
        for h in range(SWA_HEADS):
            dbh = db_ref[h]

            def add(b, carry):
                rows_ref[pl.ds(h * REL_BUCKETS + b, 1), :] = jnp.sum(jnp.where(bk == b, dbh, 0.0), axis=0, keepdims=True)
                return carry

            lax.fori_loop(0, REL_BUCKETS, add, 0)
        totals = jnp.sum(rows_ref[...], axis=-1, keepdims=True)
        lane = lax.broadcasted_iota(jnp.int32, (REL_BUCKETS, 128), 1)
        acc = jnp.zeros((REL_BUCKETS, 128), F32)
        for h in range(SWA_HEADS):
            acc = jnp.where(lane == h, totals[h * REL_BUCKETS:(h + 1) * REL_BUCKETS], acc)
        out_ref[...] = acc

    return pl.pallas_call(
        body, name="bias_reduce",
        out_shape=[jax.ShapeDtypeStruct((REL_BUCKETS, 128), F32), jax.ShapeDtypeStruct((1, 128), F32)],
        scratch_shapes=[pltpu.VMEM((SWA_HEADS * REL_BUCKETS, 2 * WINDOW), F32)],
        compiler_params=_cparams(),
    )(dbias, dsink_rows, bucket)


def _t5_bucket(dist):
    n = jnp.maximum(dist, 0)
    max_exact = REL_BUCKETS // 2
    nf = jnp.maximum(n, 1).astype(F32)
    large = max_exact + (jnp.log(nf / max_exact) / math.log(REL_MAX_DIST / max_exact) * (REL_BUCKETS - max_exact)).astype(jnp.int32)
    large = jnp.minimum(large, REL_BUCKETS - 1)
    return jnp.where(n < max_exact, n, large)


def _pair_heads(a, axis, inverse=False):
    shp = a.shape
    a = a.reshape(shp[:axis] + ((4, 2, 64) if inverse else (2, 4, 64)) + shp[axis + 1:])
    a = jnp.swapaxes(a, axis, axis + 1)
    return a.reshape(shp)


def _owner_blocks(w32, w16, me):
    r, c = w32.shape[0] // N_DEV, w32.shape[1]
    return lax.dynamic_slice(w32, (me * r, 0), (r, c)), w16.reshape(N_DEV, r, c)


_IN_WIDTHS = (Q_LORA, KV_LORA, QK_ROPE, SWA_HEADS * SWA_HEAD_DIM, SWA_KV_HEADS * SWA_HEAD_DIM, SWA_KV_HEADS * SWA_HEAD_DIM)


def _win_to_kernel(wt):
    parts, o = [], 0
    for wd in _IN_WIDTHS:
        parts.append(wt[o:o + wd])
        o += wd
    cq, ckv, kpe, qs, ks, vs = parts
    return jnp.concatenate([cq, ckv, _pair_heads(qs, 0), ks, vs, jnp.pad(kpe, ((64, 32), (0, 0)))], axis=0)


def _win_from_kernel(dwt):
    o = [0, Q_LORA, Q_LORA + KV_LORA, Q_LORA + KV_LORA + 512, Q_LORA + KV_LORA + 640, Q_LORA + KV_LORA + 768]
    cq, ckv, qs, ks, vs, kpe = [dwt[a:b] for a, b in zip(o, o[1:] + [dwt.shape[0]])]
    return jnp.concatenate([cq, ckv, kpe[64:96], _pair_heads(qs, 0, inverse=True), ks, vs], axis=0)


def _wq_to_kernel(wt):
    return jnp.pad(wt.reshape(MLA_HEADS, QK_NOPE + QK_ROPE, Q_LORA), ((0, 0), (0, 32), (0, 0))).reshape(MLA_HEADS * HEAD_BLOCK, Q_LORA)


def _wq_from_kernel(dwt):
    return dwt.reshape(MLA_HEADS, HEAD_BLOCK, Q_LORA)[:, :QK_NOPE + QK_ROPE].reshape(-1, Q_LORA)


def _wkv_to_kernel(wt):
    w3 = wt.reshape(MLA_HEADS, QK_NOPE + V_HEAD, KV_LORA)
    kpart = jnp.pad(w3[:, :QK_NOPE], ((0, 0), (0, 64), (0, 0))).reshape(MLA_HEADS * HEAD_BLOCK, KV_LORA)
    return jnp.concatenate([kpart, w3[:, QK_NOPE:].reshape(MLA_HEADS * V_HEAD, KV_LORA)], axis=0)


def _wkv_from_kernel(dwt):
    dk = dwt[:MLA_HEADS * HEAD_BLOCK].reshape(MLA_HEADS, HEAD_BLOCK, KV_LORA)[:, :QK_NOPE]
    dv = dwt[MLA_HEADS * HEAD_BLOCK:].reshape(MLA_HEADS, V_HEAD, KV_LORA)
    return jnp.concatenate([dk, dv], axis=1).reshape(-1, KV_LORA)


def _wo_to_kernel(w):
    return jnp.concatenate([w[:512], _pair_heads(w[512:], 0)], axis=0)


def _wo_from_kernel(dw):
    return jnp.concatenate([dw[:512], _pair_heads(dw[512:], 0, inverse=True)], axis=0)


def _kernel_weights(key, stored):
    if key in ("g1", "u1", "g2", "u2"):
        return {key + "t": stored}
    if key in ("d1", "d2"):
        return {key: stored}
    if key == "o":
        return {key: _wo_to_kernel(stored)}
    return {key + "t": _TO_KERNEL[key](stored)}


_SMALL = ("g_ffn1", "g_mix", "g_q_a", "g_kv_a", "attn_sinks", "rel_bias", "g_out_mla", "g_out_swa", "g_ffn2", "g_final")


class _Comm:
    def __init__(self, shards, me):
        self.shards, self.me = shards, me
        self.own32, self.recv16, self.pending, self.token = {}, {}, [], None

    def gather(self, keys):
        return [("gather", self.shards[k]) for k in keys]

    def gathered(self, keys, results, wts):
        for k, g in zip(keys, results):
            wts.update(_kernel_weights(k, g.reshape(N_DEV * g.shape[1], g.shape[2])))

    def scatter(self, keys, grads):
        jobs = []
        for k in keys:
            g32, g16 = (_FROM_KERNEL[k](g) if k in _FROM_KERNEL else g for g in grads[k])
            self.own32[k], blocks = _owner_blocks(g32, g16, self.me)
            jobs.append(("scatter", blocks))
        return jobs

    def scattered(self, keys, results):
        self.recv16.update(zip(keys, results))

    def scatter_start(self, keys, grads, name):
        handle, self.token = _scatter_start([blocks for _, blocks in self.scatter(keys, grads)], name + "_start")
        self.pending.append((keys, handle, name + "_wait"))

    def scatter_wait(self, after):
        keys, handle, name = self.pending.pop(0)
        self.scattered(keys, _scatter_wait(handle, after, name))
        return keys


def _step(x, target, gains, wts, comm):
    def jobs_in(keys):
        return comm.gather(keys) if comm else None

    def jobs_out(keys):
        return comm.scatter(keys, grads) if comm else None

    B, S, D = x.shape
    T = B * S
    x2 = x.reshape(T, D)
    t2 = target.reshape(T, D)

    pos = jnp.arange(S, dtype=F32)
    inv_freq = ROPE_THETA ** (-jnp.arange(0, QK_ROPE, 2, dtype=F32) / QK_ROPE)
    ang = pos[:, None] * inv_freq[None, :]
    cos, sin = jnp.cos(ang), jnp.sin(ang)
    ones, zeros = jnp.ones((S, 64), F32), jnp.zeros((S, 32), F32)
    rope_c = jnp.concatenate([ones, cos, cos, zeros], axis=1)
    rope_s = jnp.concatenate([0.0 * ones, -sin, sin, zeros], axis=1)

    qi = jnp.arange(WINDOW)[:, None]
    kj = jnp.arange(2 * WINDOW)[None, :]
    bucket = _t5_bucket(qi + WINDOW - kj).astype(jnp.int32)
    bias = _bias_expand(gains["rel_bias"], bucket).reshape(SWA_KV_HEADS, SWA_ROWS, 2 * WINDOW)
    sinks = jnp.broadcast_to(gains["attn_sinks"].reshape(SWA_HEADS, 1), (SWA_HEADS, 128))
    g_swa = _pair_heads(gains["g_out_swa"], 1)

    keys = ("in", "q", "kv", "o", "g2")
    (h1, n1, gate1, up1, act1), got = _ffn_fwd(x2, gains["g_ffn1"], wts["g1t"], wts["u1t"], wts["d1"], "ffn1_fwd", jobs_in(keys))
    if comm:
        comm.gathered(keys, got, wts)
    (u, c_q, c_kv, cqn, ckvn, q, k, v, qs, ks, vs) = _proj_fwd(
        h1, gains["g_mix"], wts["int"], gains["g_q_a"], wts["qt"], gains["g_kv_a"], wts["kvt"], rope_c, rope_s, S)
    keys = ("u2", "d2")
    (o_mla, lse), got = _mla_fwd(q, k, v, B, S, jobs_in(keys))
    if comm:
        comm.gathered(keys, got, wts)
    o_swa = _swa_fwd(qs, ks, vs, bias, sinks, B, S)
    h2, oc = _out_fwd(o_mla, o_swa, gains["g_out_mla"], g_swa, wts["o"], h1)
    (dh3, n2, gate2, up2, act2, loss, dg_final), _ = _ffn_fwd(
        h2, gains["g_ffn2"], wts["g2t"], wts["u2t"], wts["d2"], "ffn2_fwd", loss_head=(t2, gains["g_final"].reshape(1, D)))

    grads = {}
    (dh2, dg_ffn2, dgate2, dup2), _ = _ffn_bwd(h2, gains["g_ffn2"], gate2, up2, dh3, wts["g2t"], wts["u2t"], wts["d2"], "ffn2_bwd")
    grads["g2"], _ = _matmul_tn(dgate2, n2, "dw_gate2")
    grads["u2"], _ = _matmul_tn(dup2, n2, "dw_up2")
    grads["d2"], _ = _matmul_tn(act2, dh3, "dw_down2", 0.5)
    do_mla, do_swa, dg_mla, dg_swa = _out_bwd(dh2, o_mla, o_swa, gains["g_out_mla"], g_swa, wts["o"])
    grads["o"], _ = _matmul_tn(oc, dh2, "dw_o")
    keys = ("g2", "u2", "d2")
    (dq, dk, dv), got = _mla_bwd(q, k, v, do_mla, o_mla, lse, B, S, jobs_out(keys))
    if comm:
        comm.scattered(keys, got)
    dqs, dks, dvs, dbias, dsink = _swa_bwd(qs, ks, vs, do_swa, bias, sinks, B, S)
    dh1, dproj, dql, dkvc, dg_mix, dg_q, dg_kv = _proj_bwd(
        dq, dk, dv, dqs, dks, dvs, c_q, c_kv, h1, dh2, gains["g_mix"], wts["int"], gains["g_q_a"], wts["qt"],
        gains["g_kv_a"], wts["kvt"], rope_c, rope_s, S)
    grads["in"], _ = _matmul_tn(dproj, u, "dw_in")
    grads["q"], _ = _matmul_tn(dql, cqn, "dw_q")
    grads["kv"], _ = _matmul_tn(dkvc, ckvn, "dw_kv")
    grads["d1"], _ = _matmul_tn(act1, dh1, "dw_down1", 0.5)
    if comm:
        comm.scatter_start(("o", "in", "q", "kv", "d1"), grads, "grad_scatter_mid")
    (dx, dg_ffn1, dgate1, dup1), _ = _ffn_bwd(
        x2, gains["g_ffn1"], gate1, up1, dh1, wts["g1t"], wts["u1t"], wts["d1"], "ffn1_bwd", after=comm and comm.token)
    grads["g1"], _ = _matmul_tn(dgate1, n1, "dw_gate1")
    if comm:
        comm.scatter_start(("g1",), grads, "grad_scatter_gate1")
    grads["u1"], _ = _matmul_tn(dup1, n1, "dw_up1", after=comm and comm.token)
    if comm:
        comm.scatter_start(("u1",), grads, "grad_scatter_up1")

    d_rel_bias, d_sinks = _bias_reduce(dbias.reshape(SWA_HEADS, WINDOW, 2 * WINDOW), dsink.reshape(SWA_HEADS, WINDOW, 128), bucket)
    small = {
        "g_ffn1": dg_ffn1, "g_mix": dg_mix, "g_q_a": dg_q, "g_kv_a": dg_kv, "attn_sinks": d_sinks,
        "rel_bias": d_rel_bias, "g_out_mla": dg_mla, "g_out_swa": _pair_heads(dg_swa, 1, inverse=True),
        "g_ffn2": dg_ffn2, "g_final": dg_final, "loss": loss,
    }
    return dx.reshape(B, S, D), grads, small


_WEIGHTS = ("g_ffn1", "w_ffn1_gate", "w_ffn1_up", "w_ffn1_down", "g_mix", "w_in", "g_q_a", "w_q_b", "g_kv_a", "w_kv_b",
            "attn_sinks", "rel_bias", "g_out_mla", "g_out_swa", "w_o", "g_ffn2", "w_ffn2_gate", "w_ffn2_up", "w_ffn2_down",
            "g_final")
_BIG = (("w_ffn1_gate", "g1", True), ("w_ffn1_up", "u1", True), ("w_ffn1_down", "d1", False), ("w_in", "in", True),
        ("w_q_b", "q", True), ("w_kv_b", "kv", True), ("w_o", "o", False), ("w_ffn2_gate", "g2", True),
        ("w_ffn2_up", "u2", True), ("w_ffn2_down", "d2", False))
_TO_KERNEL = {"in": _win_to_kernel, "q": _wq_to_kernel, "kv": _wkv_to_kernel}
_FROM_KERNEL = {"in": _win_from_kernel, "q": _wq_from_kernel, "kv": _wkv_from_kernel, "o": _wo_from_kernel}


def kernel(x, g_ffn1, w_ffn1_gate, w_ffn1_up, w_ffn1_down, g_mix, w_in, g_q_a, w_q_b, g_kv_a, w_kv_b, attn_sinks, rel_bias, g_out_mla, g_out_swa, w_o, g_ffn2, w_ffn2_gate, w_ffn2_up, w_ffn2_down, g_final, loss_target, m_g_ffn1, m_w_ffn1_gate, m_w_ffn1_up, m_w_ffn1_down, m_g_mix, m_w_in, m_g_q_a, m_w_q_b, m_g_kv_a, m_w_kv_b, m_attn_sinks, m_rel_bias, m_g_out_mla, m_g_out_swa, m_w_o, m_g_ffn2, m_w_ffn2_gate, m_w_ffn2_up, m_w_ffn2_down, m_g_final, v_g_ffn1, v_w_ffn1_gate, v_w_ffn1_up, v_w_ffn1_down, v_g_mix, v_w_in, v_g_q_a, v_w_q_b, v_g_kv_a, v_w_kv_b, v_attn_sinks, v_rel_bias, v_g_out_mla, v_g_out_swa, v_w_o, v_g_ffn2, v_w_ffn2_gate, v_w_ffn2_up, v_w_ffn2_down, v_g_final):
    w = dict(zip(_WEIGHTS, (g_ffn1, w_ffn1_gate, w_ffn1_up, w_ffn1_down, g_mix, w_in, g_q_a, w_q_b, g_kv_a, w_kv_b, attn_sinks,
                            rel_bias, g_out_mla, g_out_swa, w_o, g_ffn2, w_ffn2_gate, w_ffn2_up, w_ffn2_down, g_final)))
    m = dict(zip(_WEIGHTS, (m_g_ffn1, m_w_ffn1_gate, m_w_ffn1_up, m_w_ffn1_down, m_g_mix, m_w_in, m_g_q_a, m_w_q_b, m_g_kv_a,
                            m_w_kv_b, m_attn_sinks, m_rel_bias, m_g_out_mla, m_g_out_swa, m_w_o, m_g_ffn2, m_w_ffn2_gate,
                            m_w_ffn2_up, m_w_ffn2_down, m_g_final)))
    v = dict(zip(_WEIGHTS, (v_g_ffn1, v_w_ffn1_gate, v_w_ffn1_up, v_w_ffn1_down, v_g_mix, v_w_in, v_g_q_a, v_w_q_b, v_g_kv_a,
                            v_w_kv_b, v_attn_sinks, v_rel_bias, v_g_out_mla, v_g_out_swa, v_w_o, v_g_ffn2, v_w_ffn2_gate,
                            v_w_ffn2_up, v_w_ffn2_down, v_g_final)))
    me = 4 * lax.axis_index("x") + 2 * lax.axis_index("y") + lax.axis_index("c")

    stored = lambda src, name, by_col: src[name][0].T if by_col else src[name][0]
    shards = dict(zip([key for _, key, _ in _BIG], _cast_bf16([stored(w, name, by_col) for name, _, by_col in _BIG])))
    comm = _Comm(shards, me)
    first = ("g1", "u1", "d1")
    wts = {}
    comm.gathered(first, _all_gather([shards[k] for k in first], "weights_all_gather"), wts)

    gains = {n: (w[n] if n == "rel_bias" else w[n].reshape(1, -1)) for n in _SMALL}
    grad_x, _, small = _step(x, loss_target, gains, wts, comm)

    small_gathered = _all_gather([small[n] for n in _SMALL] + [small["loss"]], "small_all_gather")

    out_g, out_d, out_m, out_v = {}, {}, {}, {}
    me1 = me.astype(jnp.int32).reshape(1)
    big = {key: (name, by_col) for name, key, by_col in _BIG}

    def update(key):
        name, by_col = big[key]
        updates = _adam_big(stored(w, name, by_col), stored(m, name, by_col), stored(v, name, by_col),
                            comm.own32[key], comm.recv16[key], me1, comm.token)
        out_g[name], out_d[name], out_m[name], out_v[name] = ((a.T if by_col else a)[None] for a in updates)
        return updates[0]

    done = [update(key) for key in ("g2", "u2", "d2")]
    for _ in range(2):
        done = [update(key) for key in comm.scatter_wait(after=done[-1])]
    as_2d = lambda src: [src[n] if src[n].ndim == 2 else src[n].reshape(1, -1) for n in _SMALL]
    loss, updates = _adam_small(as_2d(w), as_2d(m), as_2d(v), small_gathered[:-1], small_gathered[-1])
    for n, (g, d, mn, vn) in zip(_SMALL, updates):
        out_g[n], out_d[n], out_m[n], out_v[n] = (a.reshape(w[n].shape) for a in (g, d, mn, vn))
    for key in comm.scatter_wait(after=loss):
        update(key)

    return (loss[0, 0], grad_x, *[out_g[n] for n in _WEIGHTS], *[out_d[n] for n in _WEIGHTS],
            *[out_m[n] for n in _WEIGHTS], *[out_v[n] for n in _WEIGHTS])
```

```python
import functools
import math

import jax
import jax.numpy as jnp
from jax import lax
from jax.experimental import pallas as pl
from jax.experimental.pallas import tpu as pltpu

F32 = jnp.float32
BF16 = jnp.bfloat16
MESH = pl.DeviceIdType.MESH

EPS = 1e-6
NEG = -1e30
N_DEV = 8

MLA_HEADS = 8
Q_LORA = 256
KV_LORA = 128
QK_NOPE = 64
QK_ROPE = 32
V_HEAD = 64
ROPE_THETA = 10000.0
SWA_HEADS = 8
SWA_KV_HEADS = 2
SWA_HEAD_DIM = 64
WINDOW = 128
REL_BUCKETS = 32
REL_MAX_DIST = 128
HEAD_BLOCK = 128
MLA_Q_TILES = (512, 256, 128)
MLA_SCALE = (QK_NOPE + QK_ROPE) ** -0.5
MLA_SCALE_LOG2 = MLA_SCALE * math.log2(math.e)
SWA_SCALE = SWA_HEAD_DIM ** -0.5

ADAM_LR = 0.001
ADAM_B1 = 0.9
ADAM_B2 = 0.999
ADAM_EPS = 1e-08
ADAM_WD = 0.01
ADAM_STEP = 10

VMEM_LIMIT = 56 * 1024 * 1024


def _cparams(semantics=None):
    return pltpu.CompilerParams(dimension_semantics=semantics, vmem_limit_bytes=VMEM_LIMIT)


def _dot(a, b):
    return jnp.dot(a, b, preferred_element_type=F32)


def _dot_nt(a, b):
    return lax.dot_general(a, b, (((1,), (1,)), ((), ())), preferred_element_type=F32)


def _dot_tn(a, b):
    return lax.dot_general(a, b, (((0,), (0,)), ((), ())), preferred_element_type=F32)


def _rms_fwd(x, g):
    r = lax.rsqrt(jnp.mean(x * x, axis=-1, keepdims=True) + EPS)
    return x * r * g, r


def _rms_bwd(x, g, r, dy):
    xh = x * r
    dyg = dy * g
    dx = r * (dyg - xh * jnp.mean(dyg * xh, axis=-1, keepdims=True))
    return dx, jnp.sum(dy * xh, axis=0, keepdims=True)


def _swap16(x):
    n = x.shape[-1]
    up = pltpu.roll(x, n - 16, 1)
    down = pltpu.roll(x, 16, 1)
    lane = lax.broadcasted_iota(jnp.int32, x.shape, 1) % HEAD_BLOCK
    return jnp.where((lane >= 64) & (lane < 80), up, jnp.where((lane >= 80) & (lane < 96), down, 0.0))


def _const_spec(shape):
    return pl.BlockSpec(shape, lambda *_: (0,) * len(shape), pipeline_mode=pl.Buffered(1))


def _acc_spec(shape):
    return pl.BlockSpec(shape, lambda *_: (0,) * len(shape))


def _row_spec(tm, width):
    return pl.BlockSpec((tm, width), lambda i: (i, 0))


def _cast_bf16(arrays):
    n = len(arrays)

    def body(*refs):
        for src, dst in zip(refs[:n], refs[n:]):
            dst[...] = src[...].astype(BF16)

    return pl.pallas_call(
        body, name="cast_weights",
        out_shape=[jax.ShapeDtypeStruct(a.shape, BF16) for a in arrays],
        compiler_params=_cparams(),
    )(*arrays)


def _all_gather(arrays, name):
    n = len(arrays)

    def body(*refs):
        ins, outs = refs[:n], refs[n:2 * n]
        send_sems, recv_sems, local_sems = refs[2 * n:]
        x, y, c = lax.axis_index("x"), lax.axis_index("y"), lax.axis_index("c")
        me, sibling = (x, y, c), (x, y, 1 - c)
        chips = [(1 - x, y), (x, 1 - y), (1 - x, 1 - y)]

        def slot(a, dev):
            return outs[a].at[4 * dev[0] + 2 * dev[1] + dev[2]]

        def copy(a, k, block, to, src=None):
            return pltpu.make_async_remote_copy(
                src_ref=slot(a, block) if src is None else src, dst_ref=slot(a, block),
                send_sem=send_sems.at[a, k], recv_sem=recv_sems.at[a, k], device_id=to, device_id_type=MESH)

        mine = [pltpu.make_async_copy(ins[a], slot(a, me), local_sems.at[a]) for a in range(n)]
        for cp in mine:
            cp.start()
        first = []
        for a in range(n):
            first.append(copy(a, 0, me, sibling, src=ins[a]))
            first += [copy(a, 1 + j, me, (*chip, c), src=ins[a]) for j, chip in enumerate(chips)]
        for cp in first:
            cp.start()
        passed = []
        for j, chip in enumerate(chips):
            for a in range(n):
                copy(a, 1 + j, (*chip, c), me).wait_recv()
                cp = copy(a, 4 + j, (*chip, c), sibling)
                cp.start()
                passed.append(cp)
        for a in range(n):
            copy(a, 0, sibling, me).wait_recv()
            for j, chip in enumerate(chips):
                copy(a, 4 + j, (*chip, 1 - c), me).wait_recv()
        for cp in first + passed:
            cp.wait_send()
        for cp in mine:
            cp.wait()

    any_spec = pl.BlockSpec(memory_space=pl.ANY)
    return pl.pallas_call(
        body, name=name,
        out_shape=[jax.ShapeDtypeStruct((N_DEV,) + a.shape, a.dtype) for a in arrays],
        in_specs=[any_spec] * n, out_specs=[any_spec] * n,
        scratch_shapes=[pltpu.SemaphoreType.DMA((n, 7)), pltpu.SemaphoreType.DMA((n, 7)), pltpu.SemaphoreType.DMA((n,))],
    )(*arrays)


def _exchange_shapes(jobs):
    dsts = [jax.ShapeDtypeStruct(((N_DEV,) + a.shape) if kind == "gather" else a.shape, a.dtype) for kind, a in jobs]
    n = len(jobs)
    sems = [pltpu.SemaphoreType.DMA((n, N_DEV - 1)), pltpu.SemaphoreType.DMA((n, N_DEV - 1)), pltpu.SemaphoreType.DMA((n,))]
    return dsts, sems


def _exchange_copies(kinds, srcs, dsts, send_sems, recv_sems, local_sems):
    x, y, c = lax.axis_index("x"), lax.axis_index("y"), lax.axis_index("c")
    me = 4 * x + 2 * y + c
    remote, local = [], []
    for a, kind in enumerate(kinds):
        for k in range(1, N_DEV):
            peer = (1 - x if k & 4 else x, 1 - y if k & 2 else y, 1 - c if k & 1 else c)
            src = srcs[a] if kind == "gather" else srcs[a].at[4 * peer[0] + 2 * peer[1] + peer[2]]
            remote.append(pltpu.make_async_remote_copy(
                src_ref=src, dst_ref=dsts[a].at[me], send_sem=send_sems.at[a, k - 1], recv_sem=recv_sems.at[a, k - 1],
                device_id=peer, device_id_type=MESH))
        src = srcs[a] if kind == "gather" else srcs[a].at[me]
        local.append(pltpu.make_async_copy(src, dsts[a].at[me], local_sems.at[a]))
    return remote, local


def _scatter_copies(srcs, lands, send_sems, recv_sems):
    x, y, c = lax.axis_index("x"), lax.axis_index("y"), lax.axis_index("c")
    me = 4 * x + 2 * y + c
    copies = []
    for a, (src_ref, land_ref) in enumerate(zip(srcs, lands)):
        for k in range(1, N_DEV):
            peer = (1 - x if k & 4 else x, 1 - y if k & 2 else y, 1 - c if k & 1 else c)
            copies.append(pltpu.make_async_remote_copy(
                src_ref=src_ref.at[4 * peer[0] + 2 * peer[1] + peer[2]], dst_ref=land_ref.at[me],
                send_sem=send_sems.at[a * (N_DEV - 1) + k - 1], recv_sem=recv_sems.at[a * (N_DEV - 1) + k - 1],
                device_id=peer, device_id_type=MESH))
    return copies


_SPLIT_EFFECT = pltpu.SideEffectType.DATAFLOW_SIDE_EFFECTING


def _scatter_start(blocks, name):
    n = len(blocks)
    hbm = pl.BlockSpec(memory_space=pltpu.HBM)
    sem = pl.BlockSpec(memory_space=pltpu.SEMAPHORE)

    def body(*refs):
        srcs, lands, send_sems, recv_sems, token = refs[:n], refs[n:2 * n], refs[2 * n], refs[2 * n + 1], refs[-1]
        for cp in _scatter_copies(srcs, lands, send_sems, recv_sems):
            cp.start()
        token[...] = jnp.zeros_like(token)

    through = [pltpu.HBM(b.shape, b.dtype) for b in blocks]
    *handle, token = pl.pallas_call(
        body, name=name,
        out_shape=(pltpu.SemaphoreType.DMA((n * (N_DEV - 1),)), pltpu.SemaphoreType.DMA((n * (N_DEV - 1),)), *through, *through,
                   jax.ShapeDtypeStruct((8, 128), F32)),
        in_specs=(hbm,) * (2 * n), out_specs=(sem, sem) + (hbm,) * (2 * n) + (pl.BlockSpec(memory_space=pltpu.VMEM),),
        input_output_aliases={i: 2 + i for i in range(2 * n)},
        compiler_params=pltpu.CompilerParams(has_side_effects=_SPLIT_EFFECT),
    )(*[pltpu.with_memory_space_constraint(b, pltpu.HBM) for b in blocks],
      *[pltpu.with_memory_space_constraint(lax.empty(b.shape, b.dtype), pltpu.HBM) for b in blocks])
    return handle, token


def _scatter_wait(handle, after, name):
    hbm = pl.BlockSpec(memory_space=pltpu.HBM)
    sem = pl.BlockSpec(memory_space=pltpu.SEMAPHORE)
    send_sems, recv_sems, *through = handle
    n = len(through) // 2

    def body(*refs):
        for cp in _scatter_copies(refs[:n], refs[n:2 * n], refs[2 * n], refs[2 * n + 1]):
            cp.wait_send()
            cp.wait_recv()

    outs = pl.pallas_call(
        body, name=name, out_shape=tuple(pltpu.HBM(t.shape, t.dtype) for t in through),
        in_specs=(hbm,) * (2 * n) + (sem, sem, pl.BlockSpec(memory_space=pl.ANY)), out_specs=(hbm,) * (2 * n),
        input_output_aliases={i: i for i in range(2 * n)},
        compiler_params=pltpu.CompilerParams(has_side_effects=_SPLIT_EFFECT),
    )(*through, send_sems, recv_sems, after)
    return list(outs[n:])


def _pcall(body, *, name, grid, in_specs, out_specs, out_shape, args, scratch_shapes=(), exchange=None, after=None):
    semantics = ("arbitrary",) * len(grid)
    exchange = exchange or []
    extra = [] if after is None else [after]
    if not exchange and not extra:
        outs = pl.pallas_call(body, name=name, grid=grid, in_specs=list(in_specs), out_specs=list(out_specs),
                              out_shape=list(out_shape), scratch_shapes=list(scratch_shapes),
                              compiler_params=_cparams(semantics))(*args)
        return list(outs), []
    kinds = [kind for kind, _ in exchange]
    n_in, n_out, n_job = len(in_specs), len(out_specs), len(exchange)
    dst_shapes, sems = _exchange_shapes(exchange) if exchange else ([], [])

    def wrapped(*refs):
        ins, srcs = refs[:n_in], refs[n_in:n_in + n_job]
        o0 = n_in + n_job + len(extra)
        outs, dsts, rest = refs[o0:o0 + n_out], refs[o0 + n_out:o0 + n_out + n_job], refs[o0 + n_out + n_job:]
        if not exchange:
            body(*ins, *outs, *rest)
            return
        scratch, sem_refs = rest[:-3], rest[-3:]
        first = functools.reduce(jnp.logical_and, [pl.program_id(d) == 0 for d in range(len(grid))])
        last = functools.reduce(jnp.logical_and, [pl.program_id(d) == grid[d] - 1 for d in range(len(grid))])

        @pl.when(first)
        def _():
            remote, local = _exchange_copies(kinds, srcs, dsts, *sem_refs)
            for cp in local + remote:
                cp.start()

        body(*ins, *outs, *scratch)

        @pl.when(last)
        def _():
            remote, local = _exchange_copies(kinds, srcs, dsts, *sem_refs)
            for cp in remote + local:
                cp.wait()

    any_spec = pl.BlockSpec(memory_space=pl.ANY)
    outs = pl.pallas_call(
        wrapped, name=name, grid=grid, in_specs=list(in_specs) + [any_spec] * (n_job + len(extra)),
        out_specs=list(out_specs) + [any_spec] * n_job, out_shape=list(out_shape) + dst_shapes,
        scratch_shapes=list(scratch_shapes) + sems, compiler_params=_cparams(semantics),
    )(*args, *[a for _, a in exchange], *extra)
    return list(outs[:n_out]), list(outs[n_out:])


def _adam_math(w, g, m, v):
    m = ADAM_B1 * m + (1.0 - ADAM_B1) * g
    v = ADAM_B2 * v + (1.0 - ADAM_B2) * (g * g)
    m_hat = m / (1.0 - ADAM_B1 ** ADAM_STEP)
    v_hat = v / (1.0 - ADAM_B2 ** ADAM_STEP)
    delta = -ADAM_LR * (m_hat / (jnp.sqrt(v_hat) + ADAM_EPS) + ADAM_WD * w)
    return delta, m, v


def _adam_big(w, m, v, own32, recv16, me, after):
    r, c = w.shape
    tr = r if r <= 512 else (512 if r % 512 == 0 else 352)
    assert r % tr == 0

    def body(me_ref, w_ref, m_ref, v_ref, s_ref, r_ref, after_ref, g_out, d_out, m_out, v_out):
        g = s_ref[...]
        for k in range(1, N_DEV):
            g = g + r_ref[jnp.bitwise_xor(me_ref[0], k)].astype(F32)
        d, mn, vn = _adam_math(w_ref[...], g, m_ref[...], v_ref[...])
        g_out[...] = g
        d_out[...] = d
        m_out[...] = mn
        v_out[...] = vn

    blk = pl.BlockSpec((tr, c), lambda i, s: (i, 0))
    grid_spec = pltpu.PrefetchScalarGridSpec(
        num_scalar_prefetch=1, grid=(r // tr,),
        in_specs=[blk, blk, blk, blk, pl.BlockSpec((N_DEV, tr, c), lambda i, s: (0, i, 0)), pl.BlockSpec(memory_space=pl.ANY)],
        out_specs=[blk, blk, blk, blk])
    return pl.pallas_call(
        body, name="adamw_big", grid_spec=grid_spec,
        out_shape=[jax.ShapeDtypeStruct((r, c), F32)] * 4,
        compiler_params=_cparams(("arbitrary",)),
    )(me, w, m, v, own32, recv16, after)


def _adam_small(ws, ms, vs, partials, loss_parts):
    n = len(ws)

    def body(*refs):
        w_refs, m_refs, v_refs, p_refs = (refs[i * n:(i + 1) * n] for i in range(4))
        l_ref, loss_out = refs[4 * n], refs[4 * n + 1]
        outs = refs[4 * n + 2:]

        def total(ref):
            s = ref[0]
            for dev in range(1, N_DEV):
                s = s + ref[dev]
            return s

        loss_out[...] = total(l_ref)
        for i in range(n):
            r, c = w_refs[i].shape
            g = total(p_refs[i])[:, :c]
            d, mn, vn = _adam_math(w_refs[i][...], g, m_refs[i][...], v_refs[i][...])
            for j, val in enumerate((g, d, mn, vn)):
                outs[4 * i + j][...] = val

    shapes = [jax.ShapeDtypeStruct((1, 128), F32)]
    for w in ws:
        shapes += [jax.ShapeDtypeStruct(w.shape, F32)] * 4
    outs = pl.pallas_call(body, name="adamw_small", out_shape=shapes, compiler_params=_cparams())(*ws, *ms, *vs, *partials, loss_parts)
    return outs[0], [outs[1 + 4 * i:5 + 4 * i] for i in range(n)]


def _pick(n, prefer):
    for t in prefer:
        if n % t == 0:
            return t
    return n


def _ffn_fwd(x, g, wg, wu, wd, name, exchange=None, loss_head=None):
    T, D = x.shape
    F = wg.shape[0]
    tm = _pick(T, (512, 256, 128))
    fc = _pick(F, (256, 128))

    def body(x_ref, g_ref, wg_ref, wu_ref, wd_ref, *rest):
        h_ref, n_ref, gate_ref, up_ref, act_ref = rest[-7:-2] if loss_head else rest
        xv = x_ref[...]
        n, _ = _rms_fwd(xv, g_ref[...])
        nb = n.astype(BF16)
        n_ref[...] = nb
        acc = jnp.zeros((tm, D), F32)
        for f0 in range(0, F, fc):
            gate = _dot_nt(nb, wg_ref[f0:f0 + fc, :])
            up = _dot_nt(nb, wu_ref[f0:f0 + fc, :])
            gate_ref[:, f0:f0 + fc] = gate.astype(BF16)
            up_ref[:, f0:f0 + fc] = up.astype(BF16)
            act = (gate * (1.0 / (1.0 + jnp.exp(-gate))) * up).astype(BF16)
            act_ref[:, f0:f0 + fc] = act
            acc = acc + _dot(act, wd_ref[f0:f0 + fc, :])
        hv = xv + 0.5 * acc
        if not loss_head:
            h_ref[...] = hv
            return
        t_ref, gf_ref, loss_ref, dgf_ref = rest[0], rest[1], rest[-2], rest[-1]
        gf = gf_ref[...]
        y, r = _rms_fwd(hv, gf)
        diff = y - t_ref[...]
        dh, dgp = _rms_bwd(hv, gf, r, diff * (1.0 / D))
        h_ref[...] = dh
        part = 0.5 * jnp.sum(jnp.sum(diff * diff, axis=-1, keepdims=True) * (1.0 / D), axis=0, keepdims=True)

        @pl.when(pl.program_id(0) == 0)
        def _():
            loss_ref[...] = jnp.zeros_like(loss_ref)
            dgf_ref[...] = jnp.zeros_like(dgf_ref)

        loss_ref[...] += jnp.broadcast_to(part, loss_ref.shape)
        dgf_ref[...] += dgp

    in_specs = [_row_spec(tm, D), _const_spec((1, D)), _const_spec((F, D)), _const_spec((F, D)), _const_spec((F, D))]
    out_specs = [_row_spec(tm, D), _row_spec(tm, D), _row_spec(tm, F), _row_spec(tm, F), _row_spec(tm, F)]
    out_shape = [jax.ShapeDtypeStruct((T, D), F32), jax.ShapeDtypeStruct((T, D), BF16),
                 jax.ShapeDtypeStruct((T, F), BF16), jax.ShapeDtypeStruct((T, F), BF16), jax.ShapeDtypeStruct((T, F), BF16)]
    args = (x, g, wg, wu, wd)
    if loss_head:
        in_specs += [_row_spec(tm, D), _const_spec((1, D))]
        out_specs += [_acc_spec((1, 128)), _acc_spec((1, D))]
        out_shape += [jax.ShapeDtypeStruct((1, 128), F32), jax.ShapeDtypeStruct((1, D), F32)]
        args += tuple(loss_head)
    return _pcall(body, name=name, grid=(T // tm,), in_specs=in_specs, out_specs=out_specs, out_shape=out_shape, args=args,
                  exchange=exchange)


FFN_BWD_CHUNK = 1408
FFN_BWD_GROUP = 1408


def _ffn_bwd(x, g, gate, up, dh, wgt, wut, wdt, name, exchange=None, after=None):
    T, D = x.shape
    F = wgt.shape[0]
    tm = _pick(T, (512, 256, 128))
    halves = 2
    Fh = F // halves
    assert Fh * halves == F and Fh % 128 == 0
    groups = [(g0, min(g0 + FFN_BWD_GROUP, Fh)) for g0 in range(0, Fh, FFN_BWD_GROUP)]

    def body(x_ref, g_ref, gate_ref, up_ref, dh_ref, wg_ref, wu_ref, wd_ref, dx_ref, dg_ref, dgate_ref, dup_ref, dn_ref):
        i, h = pl.program_id(0), pl.program_id(1)
        dhv = dh_ref[...]
        dho = (0.5 * dhv).astype(BF16)
        dn = jnp.zeros((tm, D), F32)
        for g0, g1 in groups:
            for f0 in range(g0, g1, FFN_BWD_CHUNK):
                f1 = min(f0 + FFN_BWD_CHUNK, g1)
                d_act = _dot_nt(dho, wd_ref[f0:f1, :])
                gt = gate_ref[:, f0:f1].astype(F32)
                u = up_ref[:, f0:f1].astype(F32)
                sg = 1.0 / (1.0 + jnp.exp(-gt))
                silu = gt * sg
                dup_ref[:, f0:f1] = (d_act * silu).astype(BF16)
                dgate_ref[:, f0:f1] = (d_act * u * (sg + silu * (1.0 - sg))).astype(BF16)
            dn = dn + _dot(dgate_ref[:, g0:g1], wg_ref[g0:g1, :]) + _dot(dup_ref[:, g0:g1], wu_ref[g0:g1, :])

        @pl.when(h == 0)
        def _():
            dn_ref[...] = dn

        @pl.when(h > 0)
        def _():
            dn_ref[...] += dn

        @pl.when((i == 0) & (h == 0))
        def _():
            dg_ref[...] = jnp.zeros_like(dg_ref)

        @pl.when(h == halves - 1)
        def _():
            xv, gv = x_ref[...], g_ref[...]
            _, r = _rms_fwd(xv, gv)
            dxn, dgp = _rms_bwd(xv, gv, r, dn_ref[...])
            dx_ref[...] = dhv + dxn
            dg_ref[...] += dgp

    tile = pl.BlockSpec((tm, D), lambda i, h: (i, 0))
    wide = pl.BlockSpec((tm, Fh), lambda i, h: (i, h))
    return _pcall(
        body, name=name, grid=(T // tm, halves),
        in_specs=[tile, _const_spec((1, D)), wide, wide, tile,
                  pl.BlockSpec((Fh, D), lambda i, h: (h, 0)), pl.BlockSpec((Fh, D), lambda i, h: (h, 0)),
                  pl.BlockSpec((Fh, D), lambda i, h: (h, 0))],
        out_specs=[tile, _acc_spec((1, D)), wide, wide],
        out_shape=[jax.ShapeDtypeStruct((T, D), F32), jax.ShapeDtypeStruct((1, D), F32),
                   jax.ShapeDtypeStruct((T, F), BF16), jax.ShapeDtypeStruct((T, F), BF16)],
        scratch_shapes=[pltpu.VMEM((tm, D), F32)],
        args=(x, g, gate, up, dh, wgt, wut, wdt), exchange=exchange, after=after)


def _matmul_tn(a, b, name, scale=None, exchange=None, after=None):
    T, K = a.shape
    N = b.shape[1]
    tk = _pick(K, (1024, 1408, 1280, 768, 512, 256, 128))
    tn = _pick(N, (1024, 1408, 1280, 768, 512, 256, 128))
    tt = _pick(T, (2048, 1024, 512, 256, 128))
    n_t = T // tt

    def body(a_ref, b_ref, o_ref, o16_ref):
        @pl.when(pl.program_id(2) == 0)
        def _():
            o_ref[...] = jnp.zeros_like(o_ref)

        o_ref[...] += _dot_tn(a_ref[...].astype(BF16), b_ref[...].astype(BF16))

        @pl.when(pl.program_id(2) == n_t - 1)
        def _():
            if scale is not None:
                o_ref[...] = o_ref[...] * scale
            o16_ref[...] = o_ref[...].astype(BF16)

    tile = pl.BlockSpec((tk, tn), lambda i, j, t: (i, j))
    outs, got = _pcall(
        body, name=name, grid=(K // tk, N // tn, n_t),
        in_specs=[pl.BlockSpec((tt, tk), lambda i, j, t: (t, i)), pl.BlockSpec((tt, tn), lambda i, j, t: (t, j))],
        out_specs=[tile, tile],
        out_shape=[jax.ShapeDtypeStruct((K, N), F32), jax.ShapeDtypeStruct((K, N), BF16)],
        args=(a, b), exchange=exchange, after=after)
    return tuple(outs), got


def _rope_spec(tm, S):
    per = S // tm
    return pl.BlockSpec((tm, HEAD_BLOCK), lambda i: (i % per, 0))


def _proj_fwd(h1, g_mix, w_in, g_q, wq, g_kv, wkv, rope_c, rope_s, S):
    T, D = h1.shape
    P = w_in.shape[0]
    tm = _pick(S, (512, 256, 128))
    QW = MLA_HEADS * HEAD_BLOCK
    VW = MLA_HEADS * V_HEAD
    SQ = SWA_HEADS * SWA_HEAD_DIM
    SK = SWA_KV_HEADS * SWA_HEAD_DIM
    o_cq, o_ckv, o_qs, o_ks, o_vs, o_kpe = 0, Q_LORA, Q_LORA + KV_LORA, Q_LORA + KV_LORA + SQ, Q_LORA + KV_LORA + SQ + SK, Q_LORA + KV_LORA + SQ + 2 * SK
    assert P == o_kpe + HEAD_BLOCK

    def body(h_ref, gm_ref, win_ref, gq_ref, wq_ref, gkv_ref, wkv_ref, c_ref, s_ref,
             u_ref, cq_ref, ckv_ref, cqn_ref, ckvn_ref, q_ref, k_ref, v_ref, qs_ref, ks_ref, vs_ref):
        u, _ = _rms_fwd(h_ref[...], gm_ref[...])
        ub = u.astype(BF16)
        u_ref[...] = ub
        proj = _dot_nt(ub, win_ref[...])
        c_q = proj[:, o_cq:o_ckv]
        c_kv = proj[:, o_ckv:o_qs]
        cq_ref[...] = c_q
        ckv_ref[...] = c_kv
        qs_ref[...] = proj[:, o_qs:o_ks].astype(BF16)
        ks_ref[...] = proj[:, o_ks:o_vs].astype(BF16)
        vs_ref[...] = proj[:, o_vs:o_kpe].astype(BF16)
        cb, sb = c_ref[...], s_ref[...]
        kpe = proj[:, o_kpe:P]
        kpe = kpe * cb + _swap16(kpe) * sb
        cqn, _ = _rms_fwd(c_q, gq_ref[...])
        cqn = cqn.astype(BF16)
        cqn_ref[...] = cqn
        q = _dot_nt(cqn, wq_ref[...])
        q = q * jnp.tile(cb, (1, MLA_HEADS)) + _swap16(q) * jnp.tile(sb, (1, MLA_HEADS))
        q_ref[...] = q.astype(BF16)
        ckvn, _ = _rms_fwd(c_kv, gkv_ref[...])
        ckvn = ckvn.astype(BF16)
        ckvn_ref[...] = ckvn
        kv = _dot_nt(ckvn, wkv_ref[...])
        k_ref[...] = (kv[:, :QW] + jnp.tile(kpe, (1, MLA_HEADS))).astype(BF16)
        v_ref[...] = kv[:, QW:].astype(BF16)

    widths = [(D, BF16), (Q_LORA, F32), (KV_LORA, F32), (Q_LORA, BF16), (KV_LORA, BF16), (QW, BF16), (QW, BF16), (VW, BF16),
              (SQ, BF16), (SK, BF16), (SK, BF16)]
    return pl.pallas_call(
        body, name="proj_fwd", grid=(T // tm,),
        in_specs=[_row_spec(tm, D), _const_spec((1, D)), _const_spec((P, D)), _const_spec((1, Q_LORA)),
                  _const_spec((QW, Q_LORA)), _const_spec((1, KV_LORA)), _const_spec((QW + VW, KV_LORA)),
                  _rope_spec(tm, S), _rope_spec(tm, S)],
        out_specs=[_row_spec(tm, w) for w, _ in widths],
        out_shape=[jax.ShapeDtypeStruct((T, w), dt) for w, dt in widths],
        compiler_params=_cparams(("arbitrary",)),
    )(h1, g_mix, w_in, g_q, wq, g_kv, wkv, rope_c, rope_s)


def _proj_bwd(dq, dk, dv, dqs, dks, dvs, c_q, c_kv, h1, dh2, g_mix, w_in, g_q, wq, g_kv, wkv, rope_c, rope_s, S):
    T, D = h1.shape
    P = w_in.shape[0]
    tm = _pick(S, (256, 128))
    QW = MLA_HEADS * HEAD_BLOCK
    VW = MLA_HEADS * V_HEAD
    SQ = SWA_HEADS * SWA_HEAD_DIM
    SK = SWA_KV_HEADS * SWA_HEAD_DIM

    def body(dq_ref, dk_ref, dv_ref, dqs_ref, dks_ref, dvs_ref, cq_ref, ckv_ref, h_ref, dh2_ref, gm_ref, win_ref, gq_ref,
             wq_ref, gkv_ref, wkv_ref, c_ref, s_ref, dh1_ref, dproj_ref, dql_ref, dkv_ref, dgm_ref, dgq_ref, dgkv_ref):
        cb, sb = c_ref[...], s_ref[...]
        dqv = dq_ref[...]
        dql = dqv * jnp.tile(cb, (1, MLA_HEADS)) + _swap16(dqv * jnp.tile(sb, (1, MLA_HEADS)))
        dql = dql.astype(BF16)
        dql_ref[...] = dql
        c_q = cq_ref[...]
        _, rq = _rms_fwd(c_q, gq_ref[...])
        d_cq, dgq = _rms_bwd(c_q, gq_ref[...], rq, _dot(dql, wq_ref[...]))

        dkv_all = dk_ref[...]
        dkpe = dkv_all[:, 0:HEAD_BLOCK]
        for h in range(1, MLA_HEADS):
            dkpe = dkpe + dkv_all[:, h * HEAD_BLOCK:(h + 1) * HEAD_BLOCK]
        lane = lax.broadcasted_iota(jnp.int32, dkpe.shape, 1)
        dkpe = jnp.where((lane >= 64) & (lane < 96), dkpe, 0.0)
        dkpe = dkpe * cb + _swap16(dkpe * sb)
        dkvc = jnp.concatenate([dkv_all.astype(BF16), dv_ref[...].astype(BF16)], axis=1)
        dkv_ref[...] = dkvc
        c_kv = ckv_ref[...]
        _, rkv = _rms_fwd(c_kv, gkv_ref[...])
        d_ckv, dgkv = _rms_bwd(c_kv, gkv_ref[...], rkv, _dot(dkvc, wkv_ref[...]))

        dproj = jnp.concatenate([d_cq.astype(BF16), d_ckv.astype(BF16), dqs_ref[...].astype(BF16),
                                 dks_ref[...].astype(BF16), dvs_ref[...].astype(BF16), dkpe.astype(BF16)], axis=1)
        dproj_ref[...] = dproj
        hv = h_ref[...]
        _, rm = _rms_fwd(hv, gm_ref[...])
        dxn, dgm = _rms_bwd(hv, gm_ref[...], rm, _dot(dproj, win_ref[...]))
        dh1_ref[...] = dh2_ref[...] + dxn

        @pl.when(pl.program_id(0) == 0)
        def _():
            dgm_ref[...] = jnp.zeros_like(dgm_ref)
            dgq_ref[...] = jnp.zeros_like(dgq_ref)
            dgkv_ref[...] = jnp.zeros_like(dgkv_ref)

        dgm_ref[...] += dgm
        dgq_ref[...] += dgq
        dgkv_ref[...] += dgkv

    return pl.pallas_call(
        body, name="proj_bwd", grid=(T // tm,),
        in_specs=[_row_spec(tm, QW), _row_spec(tm, QW), _row_spec(tm, VW), _row_spec(tm, SQ), _row_spec(tm, SK), _row_spec(tm, SK),
                  _row_spec(tm, Q_LORA), _row_spec(tm, KV_LORA), _row_spec(tm, D), _row_spec(tm, D),
                  _const_spec((1, D)), _const_spec((P, D)), _const_spec((1, Q_LORA)), _const_spec((QW, Q_LORA)),
                  _const_spec((1, KV_LORA)), _const_spec((QW + VW, KV_LORA)), _rope_spec(tm, S), _rope_spec(tm, S)],
        out_specs=[_row_spec(tm, D), _row_spec(tm, P), _row_spec(tm, QW), _row_spec(tm, QW + VW),
                   _acc_spec((1, D)), _acc_spec((1, Q_LORA)), _acc_spec((1, KV_LORA))],
        out_shape=[jax.ShapeDtypeStruct((T, D), F32), jax.ShapeDtypeStruct((T, P), BF16), jax.ShapeDtypeStruct((T, QW), BF16),
                   jax.ShapeDtypeStruct((T, QW + VW), BF16), jax.ShapeDtypeStruct((1, D), F32),
                   jax.ShapeDtypeStruct((1, Q_LORA), F32), jax.ShapeDtypeStruct((1, KV_LORA), F32)],
        compiler_params=_cparams(("arbitrary",)),
    )(dq, dk, dv, dqs, dks, dvs, c_q, c_kv, h1, dh2, g_mix, w_in, g_q, wq, g_kv, wkv, rope_c, rope_s)


def _out_fwd(o_mla, o_swa, g_mla, g_swa, w_o, h1):
    T, D = h1.shape
    W = o_mla.shape[1]
    tm = _pick(T, (512, 256, 128))

    def body(om_ref, os_ref, gm_ref, gs_ref, wo_ref, h_ref, h2_ref, oc_ref):
        a, _ = _rms_fwd(om_ref[...], gm_ref[...])
        b, _ = _rms_fwd(os_ref[...], gs_ref[...])
        oc = jnp.concatenate([a.astype(BF16), b.astype(BF16)], axis=1)
        oc_ref[...] = oc
        h2_ref[...] = h_ref[...] + _dot(oc, wo_ref[...])

    return pl.pallas_call(
        body, name="out_fwd", grid=(T // tm,),
        in_specs=[_row_spec(tm, W), _row_spec(tm, W), _const_spec((1, W)), _const_spec((1, W)), _const_spec((2 * W, D)),
                  _row_spec(tm, D)],
        out_specs=[_row_spec(tm, D), _row_spec(tm, 2 * W)],
        out_shape=[jax.ShapeDtypeStruct((T, D), F32), jax.ShapeDtypeStruct((T, 2 * W), BF16)],
        compiler_params=_cparams(("arbitrary",)),
    )(o_mla, o_swa, g_mla, g_swa, w_o, h1)


def _out_bwd(dh2, o_mla, o_swa, g_mla, g_swa, w_o):
    T, D = dh2.shape
    W = o_mla.shape[1]
    tm = _pick(T, (512, 256, 128))

    def body(dh_ref, om_ref, os_ref, gm_ref, gs_ref, wo_ref, dom_ref, dos_ref, dgm_ref, dgs_ref):
        doc = _dot_nt(dh_ref[...].astype(BF16), wo_ref[...])
        om, osw = om_ref[...], os_ref[...]
        _, ra = _rms_fwd(om, gm_ref[...])
        _, rb = _rms_fwd(osw, gs_ref[...])
        da, dga = _rms_bwd(om, gm_ref[...], ra, doc[:, :W])
        db, dgb = _rms_bwd(osw, gs_ref[...], rb, doc[:, W:])
        dom_ref[...] = da
        dos_ref[...] = db

        @pl.when(pl.program_id(0) == 0)
        def _():
            dgm_ref[...] = jnp.zeros_like(dgm_ref)
            dgs_ref[...] = jnp.zeros_like(dgs_ref)

        dgm_ref[...] += dga
        dgs_ref[...] += dgb

    return pl.pallas_call(
        body, name="out_bwd", grid=(T // tm,),
        in_specs=[_row_spec(tm, D), _row_spec(tm, W), _row_spec(tm, W), _const_spec((1, W)), _const_spec((1, W)),
                  _const_spec((2 * W, D))],
        out_specs=[_row_spec(tm, W), _row_spec(tm, W), _acc_spec((1, W)), _acc_spec((1, W))],
        out_shape=[jax.ShapeDtypeStruct((T, W), F32), jax.ShapeDtypeStruct((T, W), F32),
                   jax.ShapeDtypeStruct((1, W), F32), jax.ShapeDtypeStruct((1, W), F32)],
        compiler_params=_cparams(("arbitrary",)),
    )(dh2, o_mla, o_swa, g_mla, g_swa, w_o)


def _half_mask(shape, half):
    lane = lax.broadcasted_iota(jnp.int32, shape, len(shape) - 1)
    return (lane < 64) if half == 0 else (lane >= 64)


def _mla_fwd(q, k, v, B, S, exchange=None):
    T = B * S
    tq = _pick(S, MLA_Q_TILES)
    nq = S // tq
    pairs = MLA_HEADS // 2

    def body(q_ref, k_ref, v_ref, o_ref, lse_ref):
        causal = lax.broadcasted_iota(jnp.int32, (tq, tq), 1) <= lax.broadcasted_iota(jnp.int32, (tq, tq), 0)
        low = _half_mask((tq, 2 * V_HEAD), 0)
        for qi in range(nq):
            rows = slice(qi * tq, (qi + 1) * tq)
            past = slice(0, qi * tq)
            outs, lses = [], []
            for half in range(2):
                lanes = slice(half * HEAD_BLOCK, (half + 1) * HEAD_BLOCK)
                qv = q_ref[rows, lanes]
                s_d = jnp.where(causal, _dot_nt(qv, k_ref[rows, lanes]) * MLA_SCALE_LOG2, NEG)
                m = jnp.max(s_d, axis=-1, keepdims=True)
                if qi:
                    s_p = _dot_nt(qv, k_ref[past, lanes]) * MLA_SCALE_LOG2
                    m = jnp.maximum(m, jnp.max(s_p, axis=-1, keepdims=True))
                p_d = jnp.exp2(s_d - m)
                l = jnp.sum(p_d, axis=-1, keepdims=True)
                acc = _dot(p_d.astype(BF16), v_ref[rows, :])
                if qi:
                    p_p = jnp.exp2(s_p - m)
                    l = l + jnp.sum(p_p, axis=-1, keepdims=True)
                    acc = acc + _dot(p_p.astype(BF16), v_ref[past, :])
                outs.append(acc * (1.0 / l))
                lses.append(jnp.broadcast_to(m + jnp.log2(l), (tq, 2 * V_HEAD)))
            o_ref[rows, :] = jnp.where(low, outs[0], outs[1])
            lse_ref[rows, :] = jnp.where(low, lses[0], lses[1])

    blk2 = pl.BlockSpec((S, 2 * HEAD_BLOCK), lambda b, j: (b, j))
    blk1 = pl.BlockSpec((S, 2 * V_HEAD), lambda b, j: (b, j))
    return _pcall(
        body, name="mla_fwd", grid=(B, pairs),
        in_specs=[blk2, blk2, blk1], out_specs=[blk1, blk1],
        out_shape=[jax.ShapeDtypeStruct((T, pairs * 2 * V_HEAD), F32)] * 2,
        args=(q, k, v), exchange=exchange)


def _mla_bwd(q, k, v, do, o, lse, B, S, exchange=None):
    T = B * S
    tq = _pick(S, MLA_Q_TILES)
    nq = S // tq
    pairs = MLA_HEADS // 2

    def body(q_ref, k_ref, v_ref, do_ref, o_ref, lse_ref, dq_ref, dk_ref, dv_ref, dkt_acc, dvt_acc):
        causal = lax.broadcasted_iota(jnp.int32, (tq, tq), 1) <= lax.broadcasted_iota(jnp.int32, (tq, tq), 0)
        for half in range(2):
            lanes = slice(half * HEAD_BLOCK, (half + 1) * HEAD_BLOCK)
            col = half * V_HEAD
            own = _half_mask((tq, 2 * V_HEAD), half)
            for qi in reversed(range(nq)):
                rows = slice(qi * tq, (qi + 1) * tq)
                past = slice(0, qi * tq)
                first = qi == nq - 1
                qv = q_ref[rows, lanes]
                dov = jnp.where(own, do_ref[rows, :], 0.0)
                dsum = jnp.sum(dov * o_ref[rows, :], axis=-1, keepdims=True)
                dob = dov.astype(BF16)
                lse_col = lse_ref[rows, col:col + 1]

                def grads(keys, mask):
                    kv, vv = k_ref[keys, lanes], v_ref[keys, :]
                    p = jnp.exp2(_dot_nt(qv, kv) * MLA_SCALE_LOG2 - lse_col)
                    if mask is not None:
                        p = jnp.where(mask, p, 0.0)
                    ds = (p * (_dot_nt(dob, vv) - dsum) * MLA_SCALE).astype(BF16)
                    dkt, dvt = _dot_tn(qv, ds), _dot_tn(dob, p.astype(BF16))
                    if first:
                        dkt_acc[:, keys] = dkt
                    else:
                        dkt_acc[:, keys] += dkt
                    if first and half == 0:
                        dvt_acc[:, keys] = dvt
                    else:
                        dvt_acc[:, keys] += dvt
                    return _dot(ds, kv)

                dq = grads(rows, causal)
                if qi:
                    dq = dq + grads(past, None)
                dq_ref[rows, lanes] = dq
            dk_ref[:, lanes] = dkt_acc[...].T
        dv_ref[...] = dvt_acc[...].T

    blk2 = pl.BlockSpec((S, 2 * HEAD_BLOCK), lambda b, j: (b, j))
    blk1 = pl.BlockSpec((S, 2 * V_HEAD), lambda b, j: (b, j))
    return _pcall(
        body, name="mla_bwd", grid=(B, pairs),
        in_specs=[blk2, blk2, blk1, blk1, blk1, blk1], out_specs=[blk2, blk2, blk1],
        out_shape=[jax.ShapeDtypeStruct((T, MLA_HEADS * HEAD_BLOCK), F32), jax.ShapeDtypeStruct((T, MLA_HEADS * HEAD_BLOCK), F32),
                   jax.ShapeDtypeStruct((T, pairs * 2 * V_HEAD), F32)],
        scratch_shapes=[pltpu.VMEM((HEAD_BLOCK, S), F32), pltpu.VMEM((2 * V_HEAD, S), F32)],
        args=(q, k, v, do, o, lse), exchange=exchange)


SWA_GROUP = SWA_HEADS // SWA_KV_HEADS
SWA_ROWS = SWA_GROUP * WINDOW


SWA_CHUNK = 8


def _bdot(a, b, ca, cb):
    return lax.dot_general(a, b, (((ca,), (cb,)), ((0,), (0,))), preferred_element_type=F32)


def _swa_chunks(nb):
    cb = next(c for c in (SWA_CHUNK, 4, 2, 1) if nb % c == 0)
    return cb, [(n0, slice(n0 * WINDOW, (n0 + cb) * WINDOW)) for n0 in range(0, nb, cb)]


def _swa_keys(ref, n0, cb):
    W = WINDOW
    cur = ref[n0 * W:(n0 + cb) * W, :]
    if n0:
        prev = ref[(n0 - 1) * W:(n0 + cb - 1) * W, :]
    else:
        prev = jnp.concatenate([cur[:W], cur[:(cb - 1) * W]], axis=0) if cb > 1 else cur
    return jnp.concatenate([prev.reshape(cb, W, W), cur.reshape(cb, W, W)], axis=1)


def _swa_stack(ref, rows, cb):
    return jnp.concatenate([ref[rows, j * WINDOW:(j + 1) * WINDOW].reshape(cb, WINDOW, WINDOW) for j in range(SWA_GROUP)], axis=1)


def _swa_unstack(low, high, cb):
    sel = _half_mask((cb, WINDOW, WINDOW), 0)
    return jnp.concatenate([jnp.where(sel, low[:, j * WINDOW:(j + 1) * WINDOW], high[:, j * WINDOW:(j + 1) * WINDOW])
                            .reshape(cb * WINDOW, WINDOW) for j in range(SWA_GROUP)], axis=1)


def _swa_sink_col(sink_ref, g):
    return jnp.concatenate([jnp.broadcast_to(sink_ref[g * SWA_GROUP + j:g * SWA_GROUP + j + 1, 0:1], (WINDOW, 1))
                            for j in range(SWA_GROUP)], axis=0)


def _swa_probs(qm, kcat, bias_g, sink, first_chunk):
    shape = (qm.shape[0], SWA_ROWS, 2 * WINDOW)
    qi = lax.broadcasted_iota(jnp.int32, shape, 1) % WINDOW
    col = lax.broadcasted_iota(jnp.int32, shape, 2)
    valid = (col > qi) & (col <= qi + WINDOW)
    if first_chunk:
        valid = valid & ((col >= WINDOW) | (lax.broadcasted_iota(jnp.int32, shape, 0) > 0))
    s = jnp.where(valid, _bdot(qm, kcat, 2, 2) * SWA_SCALE + bias_g, NEG)
    m = jnp.maximum(jnp.max(s, axis=-1, keepdims=True), sink)
    e = jnp.exp(s - m)
    e_s = jnp.exp(sink - m)
    inv = 1.0 / (jnp.sum(e, axis=-1, keepdims=True) + e_s)
    return e * inv, e_s * inv


def _swa_fwd(q, k, v, bias, sinks, B, S):
    T = B * S
    W = WINDOW
    nb = S // W

    cb, chunks = _swa_chunks(nb)

    def body(q_ref, k_ref, v_ref, bias_ref, sink_ref, o_ref):
        sink_cols = [_swa_sink_col(sink_ref, g) for g in range(SWA_KV_HEADS)]
        for n0, rows in chunks:
            kcat, vcat = _swa_keys(k_ref, n0, cb), _swa_keys(v_ref, n0, cb)
            qcat = _swa_stack(q_ref, rows, cb)
            outs = []
            for g in range(SWA_KV_HEADS):
                qm = jnp.where(_half_mask(qcat.shape, g), qcat, jnp.zeros_like(qcat))
                p, _ = _swa_probs(qm, kcat, bias_ref[g], sink_cols[g], n0 == 0)
                outs.append(_bdot(p.astype(BF16), vcat, 2, 1))
            o_ref[rows, :] = _swa_unstack(outs[0], outs[1], cb)

    SQ, SK = SWA_HEADS * SWA_HEAD_DIM, SWA_KV_HEADS * SWA_HEAD_DIM
    return pl.pallas_call(
        body, name="swa_fwd", grid=(B,),
        in_specs=[pl.BlockSpec((S, SQ), lambda b: (b, 0)), pl.BlockSpec((S, SK), lambda b: (b, 0)),
                  pl.BlockSpec((S, SK), lambda b: (b, 0)), _const_spec((SWA_KV_HEADS, SWA_ROWS, 2 * W)),
                  _const_spec((SWA_HEADS, 128))],
        out_specs=pl.BlockSpec((S, SQ), lambda b: (b, 0)),
        out_shape=jax.ShapeDtypeStruct((T, SQ), F32),
        compiler_params=_cparams(("arbitrary",)),
    )(q, k, v, bias, sinks)


def _swa_bwd(q, k, v, do, bias, sinks, B, S):
    T = B * S
    W = WINDOW
    nb = S // W
    cb, chunks = _swa_chunks(nb)

    def body(q_ref, k_ref, v_ref, do_ref, bias_ref, sink_ref, dq_ref, dk_ref, dv_ref, dbias_ref, dsink_ref, dkt_acc, dvt_acc):
        @pl.when(pl.program_id(0) == 0)
        def _():
            dbias_ref[...] = jnp.zeros_like(dbias_ref)
            dsink_ref[...] = jnp.zeros_like(dsink_ref)

        dkt_acc[...] = jnp.zeros_like(dkt_acc)
        dvt_acc[...] = jnp.zeros_like(dvt_acc)
        sink_cols = [_swa_sink_col(sink_ref, g) for g in range(SWA_KV_HEADS)]
        for n0, rows in chunks:
            kcat, vcat = _swa_keys(k_ref, n0, cb), _swa_keys(v_ref, n0, cb)
            qcat = _swa_stack(q_ref, rows, cb)
            docat = _swa_stack(do_ref, rows, cb)
            dkt = jnp.zeros((cb, W, 2 * W), F32)
            dvt = jnp.zeros((cb, W, 2 * W), F32)
            dqs = []
            for g in range(SWA_KV_HEADS):
                own = _half_mask(qcat.shape, g)
                qm = jnp.where(own, qcat, jnp.zeros_like(qcat))
                dom = jnp.where(own, docat, 0.0).astype(BF16)
                p, p_s = _swa_probs(qm, kcat, bias_ref[g], sink_cols[g], n0 == 0)
                dp = _bdot(dom, vcat, 2, 2)
                dsum = jnp.sum(p * dp, axis=-1, keepdims=True)
                ds = p * (dp - dsum)
                dbias_ref[g] += jnp.sum(ds, axis=0)
                dsink_ref[g] += jnp.broadcast_to(-jnp.sum(p_s * dsum, axis=0), (SWA_ROWS, 128))
                dsb = (ds * SWA_SCALE).astype(BF16)
                dqs.append(_bdot(dsb, kcat, 2, 1))
                dkt = dkt + _bdot(qm, dsb, 1, 1)
                dvt = dvt + _bdot(dom, p.astype(BF16), 1, 1)
            dq_ref[rows, :] = _swa_unstack(dqs[0], dqs[1], cb)
            for i in range(cb):
                n = n0 + i
                if n:
                    cols = slice((n - 1) * W, (n + 1) * W)
                    dkt_acc[:, cols] += dkt[i]
                    dvt_acc[:, cols] += dvt[i]
                else:
                    dkt_acc[:, :W] += dkt[i][:, W:]
                    dvt_acc[:, :W] += dvt[i][:, W:]
        dk_ref[...] = dkt_acc[...].T
        dv_ref[...] = dvt_acc[...].T

    SQ, SK = SWA_HEADS * SWA_HEAD_DIM, SWA_KV_HEADS * SWA_HEAD_DIM
    row = lambda w: pl.BlockSpec((S, w), lambda b: (b, 0))
    return pl.pallas_call(
        body, name="swa_bwd", grid=(B,),
        in_specs=[row(SQ), row(SK), row(SK), row(SQ), _const_spec((SWA_KV_HEADS, SWA_ROWS, 2 * W)), _const_spec((SWA_HEADS, 128))],
        out_specs=[row(SQ), row(SK), row(SK), _acc_spec((SWA_KV_HEADS, SWA_ROWS, 2 * W)), _acc_spec((SWA_KV_HEADS, SWA_ROWS, 128))],
        out_shape=[jax.ShapeDtypeStruct((T, SQ), F32), jax.ShapeDtypeStruct((T, SK), F32), jax.ShapeDtypeStruct((T, SK), F32),
                   jax.ShapeDtypeStruct((SWA_KV_HEADS, SWA_ROWS, 2 * W), F32), jax.ShapeDtypeStruct((SWA_KV_HEADS, SWA_ROWS, 128), F32)],
        scratch_shapes=[pltpu.VMEM((SK, S), F32), pltpu.VMEM((SK, S), F32)],
        compiler_params=_cparams(("arbitrary",)),
    )(q, k, v, do, bias, sinks)


def _bias_expand(rel_bias, bucket):
    W = WINDOW

    def body(rb_ref, bucket_ref, out_ref):
        bk = bucket_ref[...]
        for h in range(SWA_HEADS):
            def add(b, acc):
                return jnp.where(bk == b, rb_ref[b, h], acc)

            out_ref[h] = lax.fori_loop(0, REL_BUCKETS, add, jnp.zeros((W, 2 * W), F32))

    return pl.pallas_call(
        body, name="bias_expand",
        in_specs=[pl.BlockSpec(memory_space=pltpu.SMEM), pl.BlockSpec(memory_space=pltpu.VMEM)],
        out_specs=pl.BlockSpec(memory_space=pltpu.VMEM),
        out_shape=jax.ShapeDtypeStruct((SWA_HEADS, W, 2 * W), F32),
        compiler_params=_cparams(),
    )(rel_bias, bucket)


def _bias_reduce(dbias, dsink_rows, bucket):
    def body(db_ref, ds_ref, bucket_ref, out_ref, sink_out_ref, rows_ref):
        sink_lane = lax.broadcasted_iota(jnp.int32, (1, 128), 1)
        sink_acc = jnp.zeros((1, 128), F32)
        for h in range(SWA_HEADS):
            sink_acc = jnp.where(sink_lane == h, jnp.sum(ds_ref[h], axis=0, keepdims=True), sink_acc)
        sink_out_ref[...] = sink_acc
        bk = bucket_ref[...]
        for h in range(SWA_HEADS):
            dbh = db_ref[h]

            def add(b, carry):
                rows_ref[pl.ds(h * REL_BUCKETS + b, 1), :] = jnp.sum(jnp.where(bk == b, dbh, 0.0), axis=0, keepdims=True)
                return carry

            lax.fori_loop(0, REL_BUCKETS, add, 0)
        totals = jnp.sum(rows_ref[...], axis=-1, keepdims=True)
        lane = lax.broadcasted_iota(jnp.int32, (REL_BUCKETS, 128), 1)
        acc = jnp.zeros((REL_BUCKETS, 128), F32)
        for h in range(SWA_HEADS):
            acc = jnp.where(lane == h, totals[h * REL_BUCKETS:(h + 1) * REL_BUCKETS], acc)
        out_ref[...] = acc

    return pl.pallas_call(
        body, name="bias_reduce",
        out_shape=[jax.ShapeDtypeStruct((REL_BUCKETS, 128), F32), jax.ShapeDtypeStruct((1, 128), F32)],
        scratch_shapes=[pltpu.VMEM((SWA_HEADS * REL_BUCKETS, 2 * WINDOW), F32)],
        compiler_params=_cparams(),
    )(dbias, dsink_rows, bucket)


def _t5_bucket(dist):
    n = jnp.maximum(dist, 0)
    max_exact = REL_BUCKETS // 2
    nf = jnp.maximum(n, 1).astype(F32)
    large = max_exact + (jnp.log(nf / max_exact) / math.log(REL_MAX_DIST / max_exact) * (REL_BUCKETS - max_exact)).astype(jnp.int32)
    large = jnp.minimum(large, REL_BUCKETS - 1)
    return jnp.where(n < max_exact, n, large)


def _pair_heads(a, axis, inverse=False):
    shp = a.shape
    a = a.reshape(shp[:axis] + ((4, 2, 64) if inverse else (2, 4, 64)) + shp[axis + 1:])
    a = jnp.swapaxes(a, axis, axis + 1)
    return a.reshape(shp)


def _owner_blocks(w32, w16, me):
    r, c = w32.shape[0] // N_DEV, w32.shape[1]
    return lax.dynamic_slice(w32, (me * r, 0), (r, c)), w16.reshape(N_DEV, r, c)


_IN_WIDTHS = (Q_LORA, KV_LORA, QK_ROPE, SWA_HEADS * SWA_HEAD_DIM, SWA_KV_HEADS * SWA_HEAD_DIM, SWA_KV_HEADS * SWA_HEAD_DIM)


def _win_to_kernel(wt):
    parts, o = [], 0
    for wd in _IN_WIDTHS:
        parts.append(wt[o:o + wd])
        o += wd
    cq, ckv, kpe, qs, ks, vs = parts
    return jnp.concatenate([cq, ckv, _pair_heads(qs, 0), ks, vs, jnp.pad(kpe, ((64, 32), (0, 0)))], axis=0)


def _win_from_kernel(dwt):
    o = [0, Q_LORA, Q_LORA + KV_LORA, Q_LORA + KV_LORA + 512, Q_LORA + KV_LORA + 640, Q_LORA + KV_LORA + 768]
    cq, ckv, qs, ks, vs, kpe = [dwt[a:b] for a, b in zip(o, o[1:] + [dwt.shape[0]])]
    return jnp.concatenate([cq, ckv, kpe[64:96], _pair_heads(qs, 0, inverse=True), ks, vs], axis=0)


def _wq_to_kernel(wt):
    return jnp.pad(wt.reshape(MLA_HEADS, QK_NOPE + QK_ROPE, Q_LORA), ((0, 0), (0, 32), (0, 0))).reshape(MLA_HEADS * HEAD_BLOCK, Q_LORA)


def _wq_from_kernel(dwt):
    return dwt.reshape(MLA_HEADS, HEAD_BLOCK, Q_LORA)[:, :QK_NOPE + QK_ROPE].reshape(-1, Q_LORA)


def _wkv_to_kernel(wt):
    w3 = wt.reshape(MLA_HEADS, QK_NOPE + V_HEAD, KV_LORA)
    kpart = jnp.pad(w3[:, :QK_NOPE], ((0, 0), (0, 64), (0, 0))).reshape(MLA_HEADS * HEAD_BLOCK, KV_LORA)
    return jnp.concatenate([kpart, w3[:, QK_NOPE:].reshape(MLA_HEADS * V_HEAD, KV_LORA)], axis=0)


def _wkv_from_kernel(dwt):
    dk = dwt[:MLA_HEADS * HEAD_BLOCK].reshape(MLA_HEADS, HEAD_BLOCK, KV_LORA)[:, :QK_NOPE]
    dv = dwt[MLA_HEADS * HEAD_BLOCK:].reshape(MLA_HEADS, V_HEAD, KV_LORA)
    return jnp.concatenate([dk, dv], axis=1).reshape(-1, KV_LORA)


def _wo_to_kernel(w):
    return jnp.concatenate([w[:512], _pair_heads(w[512:], 0)], axis=0)


def _wo_from_kernel(dw):
    return jnp.concatenate([dw[:512], _pair_heads(dw[512:], 0, inverse=True)], axis=0)


def _kernel_weights(key, stored):
    if key in ("g1", "u1", "g2", "u2"):
        return {key + "t": stored}
    if key in ("d1", "d2"):
        return {key: stored}
    if key == "o":
        return {key: _wo_to_kernel(stored)}
    return {key + "t": _TO_KERNEL[key](stored)}


_SMALL = ("g_ffn1", "g_mix", "g_q_a", "g_kv_a", "attn_sinks", "rel_bias", "g_out_mla", "g_out_swa", "g_ffn2", "g_final")


class _Comm:
    def __init__(self, shards, me):
        self.shards, self.me = shards, me
        self.own32, self.recv16, self.pending, self.token = {}, {}, [], None

    def gather(self, keys):
        return [("gather", self.shards[k]) for k in keys]

    def gathered(self, keys, results, wts):
        for k, g in zip(keys, results):
            wts.update(_kernel_weights(k, g.reshape(N_DEV * g.shape[1], g.shape[2])))

    def scatter(self, keys, grads):
        jobs = []
        for k in keys:
            g32, g16 = (_FROM_KERNEL[k](g) if k in _FROM_KERNEL else g for g in grads[k])
            self.own32[k], blocks = _owner_blocks(g32, g16, self.me)
            jobs.append(("scatter", blocks))
        return jobs

    def scattered(self, keys, results):
        self.recv16.update(zip(keys, results))

    def scatter_start(self, keys, grads, name):
        handle, self.token = _scatter_start([blocks for _, blocks in self.scatter(keys, grads)], name + "_start")
        self.pending.append((keys, handle, name + "_wait"))

    def scatter_wait(self, after):
        keys, handle, name = self.pending.pop(0)
        self.scattered(keys, _scatter_wait(handle, after, name))
        return keys


def _step(x, target, gains, wts, comm):
    def jobs_in(keys):
        return comm.gather(keys) if comm else None

    def jobs_out(keys):
        return comm.scatter(keys, grads) if comm else None

    B, S, D = x.shape
    T = B * S
    x2 = x.reshape(T, D)
    t2 = target.reshape(T, D)

    pos = jnp.arange(S, dtype=F32)
    inv_freq = ROPE_THETA ** (-jnp.arange(0, QK_ROPE, 2, dtype=F32) / QK_ROPE)
    ang = pos[:, None] * inv_freq[None, :]
    cos, sin = jnp.cos(ang), jnp.sin(ang)
    ones, zeros = jnp.ones((S, 64), F32), jnp.zeros((S, 32), F32)
    rope_c = jnp.concatenate([ones, cos, cos, zeros], axis=1)
    rope_s = jnp.concatenate([0.0 * ones, -sin, sin, zeros], axis=1)

    qi = jnp.arange(WINDOW)[:, None]
    kj = jnp.arange(2 * WINDOW)[None, :]
    bucket = _t5_bucket(qi + WINDOW - kj).astype(jnp.int32)
    bias = _bias_expand(gains["rel_bias"], bucket).reshape(SWA_KV_HEADS, SWA_ROWS, 2 * WINDOW)
    sinks = jnp.broadcast_to(gains["attn_sinks"].reshape(SWA_HEADS, 1), (SWA_HEADS, 128))
    g_swa = _pair_heads(gains["g_out_swa"], 1)

    keys = ("in", "q", "kv", "o", "g2")
    (h1, n1, gate1, up1, act1), got = _ffn_fwd(x2, gains["g_ffn1"], wts["g1t"], wts["u1t"], wts["d1"], "ffn1_fwd", jobs_in(keys))
    if comm:
        comm.gathered(keys, got, wts)
    (u, c_q, c_kv, cqn, ckvn, q, k, v, qs, ks, vs) = _proj_fwd(
        h1, gains["g_mix"], wts["int"], gains["g_q_a"], wts["qt"], gains["g_kv_a"], wts["kvt"], rope_c, rope_s, S)
    keys = ("u2", "d2")
    (o_mla, lse), got = _mla_fwd(q, k, v, B, S, jobs_in(keys))
    if comm:
        comm.gathered(keys, got, wts)
    o_swa = _swa_fwd(qs, ks, vs, bias, sinks, B, S)
    h2, oc = _out_fwd(o_mla, o_swa, gains["g_out_mla"], g_swa, wts["o"], h1)
    (dh3, n2, gate2, up2, act2, loss, dg_final), _ = _ffn_fwd(
        h2, gains["g_ffn2"], wts["g2t"], wts["u2t"], wts["d2"], "ffn2_fwd", loss_head=(t2, gains["g_final"].reshape(1, D)))

    grads = {}
    (dh2, dg_ffn2, dgate2, dup2), _ = _ffn_bwd(h2, gains["g_ffn2"], gate2, up2, dh3, wts["g2t"], wts["u2t"], wts["d2"], "ffn2_bwd")
    grads["g2"], _ = _matmul_tn(dgate2, n2, "dw_gate2")
    grads["u2"], _ = _matmul_tn(dup2, n2, "dw_up2")
    grads["d2"], _ = _matmul_tn(act2, dh3, "dw_down2", 0.5)
    do_mla, do_swa, dg_mla, dg_swa = _out_bwd(dh2, o_mla, o_swa, gains["g_out_mla"], g_swa, wts["o"])
    grads["o"], _ = _matmul_tn(oc, dh2, "dw_o")
    keys = ("g2", "u2", "d2")
    (dq, dk, dv), got = _mla_bwd(q, k, v, do_mla, o_mla, lse, B, S, jobs_out(keys))
    if comm:
        comm.scattered(keys, got)
    dqs, dks, dvs, dbias, dsink = _swa_bwd(qs, ks, vs, do_swa, bias, sinks, B, S)
    dh1, dproj, dql, dkvc, dg_mix, dg_q, dg_kv = _proj_bwd(
        dq, dk, dv, dqs, dks, dvs, c_q, c_kv, h1, dh2, gains["g_mix"], wts["int"], gains["g_q_a"], wts["qt"],
        gains["g_kv_a"], wts["kvt"], rope_c, rope_s, S)
    grads["in"], _ = _matmul_tn(dproj, u, "dw_in")
    grads["q"], _ = _matmul_tn(dql, cqn, "dw_q")
    grads["kv"], _ = _matmul_tn(dkvc, ckvn, "dw_kv")
    grads["d1"], _ = _matmul_tn(act1, dh1, "dw_down1", 0.5)
    if comm:
        comm.scatter_start(("o", "in", "q", "kv", "d1"), grads, "grad_scatter_mid")
    (dx, dg_ffn1, dgate1, dup1), _ = _ffn_bwd(
        x2, gains["g_ffn1"], gate1, up1, dh1, wts["g1t"], wts["u1t"], wts["d1"], "ffn1_bwd", after=comm and comm.token)
    grads["g1"], _ = _matmul_tn(dgate1, n1, "dw_gate1")
    if comm:
        comm.scatter_start(("g1",), grads, "grad_scatter_gate1")
    grads["u1"], _ = _matmul_tn(dup1, n1, "dw_up1", after=comm and comm.token)
    if comm:
        comm.scatter_start(("u1",), grads, "grad_scatter_up1")

    d_rel_bias, d_sinks = _bias_reduce(dbias.reshape(SWA_HEADS, WINDOW, 2 * WINDOW), dsink.reshape(SWA_HEADS, WINDOW, 128), bucket)
    small = {
        "g_ffn1": dg_ffn1, "g_mix": dg_mix, "g_q_a": dg_q, "g_kv_a": dg_kv, "attn_sinks": d_sinks,
        "rel_bias": d_rel_bias, "g_out_mla": dg_mla, "g_out_swa": _pair_heads(dg_swa, 1, inverse=True),
        "g_ffn2": dg_ffn2, "g_final": dg_final, "loss": loss,
    }
    return dx.reshape(B, S, D), grads, small


_WEIGHTS = ("g_ffn1", "w_ffn1_gate", "w_ffn1_up", "w_ffn1_down", "g_mix", "w_in", "g_q_a", "w_q_b", "g_kv_a", "w_kv_b",
            "attn_sinks", "rel_bias", "g_out_mla", "g_out_swa", "w_o", "g_ffn2", "w_ffn2_gate", "w_ffn2_up", "w_ffn2_down",
            "g_final")
_BIG = (("w_ffn1_gate", "g1", True), ("w_ffn1_up", "u1", True), ("w_ffn1_down", "d1", False), ("w_in", "in", True),
        ("w_q_b", "q", True), ("w_kv_b", "kv", True), ("w_o", "o", False), ("w_ffn2_gate", "g2", True),
        ("w_ffn2_up", "u2", True), ("w_ffn2_down", "d2", False))
_TO_KERNEL = {"in": _win_to_kernel, "q": _wq_to_kernel, "kv": _wkv_to_kernel}
_FROM_KERNEL = {"in": _win_from_kernel, "q": _wq_from_kernel, "kv": _wkv_from_kernel, "o": _wo_from_kernel}


def kernel(x, g_ffn1, w_ffn1_gate, w_ffn1_up, w_ffn1_down, g_mix, w_in, g_q_a, w_q_b, g_kv_a, w_kv_b, attn_sinks, rel_bias, g_out_mla, g_out_swa, w_o, g_ffn2, w_ffn2_gate, w_ffn2_up, w_ffn2_down, g_final, loss_target, m_g_ffn1, m_w_ffn1_gate, m_w_ffn1_up, m_w_ffn1_down, m_g_mix, m_w_in, m_g_q_a, m_w_q_b, m_g_kv_a, m_w_kv_b, m_attn_sinks, m_rel_bias, m_g_out_mla, m_g_out_swa, m_w_o, m_g_ffn2, m_w_ffn2_gate, m_w_ffn2_up, m_w_ffn2_down, m_g_final, v_g_ffn1, v_w_ffn1_gate, v_w_ffn1_up, v_w_ffn1_down, v_g_mix, v_w_in, v_g_q_a, v_w_q_b, v_g_kv_a, v_w_kv_b, v_attn_sinks, v_rel_bias, v_g_out_mla, v_g_out_swa, v_w_o, v_g_ffn2, v_w_ffn2_gate, v_w_ffn2_up, v_w_ffn2_down, v_g_final):
    w = dict(zip(_WEIGHTS, (g_ffn1, w_ffn1_gate, w_ffn1_up, w_ffn1_down, g_mix, w_in, g_q_a, w_q_b, g_kv_a, w_kv_b, attn_sinks,
                            rel_bias, g_out_mla, g_out_swa, w_o, g_ffn2, w_ffn2_gate, w_ffn2_up, w_ffn2_down, g_final)))
    m = dict(zip(_WEIGHTS, (m_g_ffn1, m_w_ffn1_gate, m_w_ffn1_up, m_w_ffn1_down, m_g_mix, m_w_in, m_g_q_a, m_w_q_b, m_g_kv_a,
                            m_w_kv_b, m_attn_sinks, m_rel_bias, m_g_out_mla, m_g_out_swa, m_w_o, m_g_ffn2, m_w_ffn2_gate,
                            m_w_ffn2_up, m_w_ffn2_down, m_g_final)))
    v = dict(zip(_WEIGHTS, (v_g_ffn1, v_w_ffn1_gate, v_w_ffn1_up, v_w_ffn1_down, v_g_mix, v_w_in, v_g_q_a, v_w_q_b, v_g_kv_a,
                            v_w_kv_b, v_attn_sinks, v_rel_bias, v_g_out_mla, v_g_out_swa, v_w_o, v_g_ffn2, v_w_ffn2_gate,
                            v_w_ffn2_up, v_w_ffn2_down, v_g_final)))
    me = 4 * lax.axis_index("x") + 2 * lax.axis_index("y") + lax.axis_index("c")

    stored = lambda src, name, by_col: src[name][0].T if by_col else src[name][0]
    shards = dict(zip([key for _, key, _ in _BIG], _cast_bf16([stored(w, name, by_col) for name, _, by_col in _BIG])))
    comm = _Comm(shards, me)
    first = ("g1", "u1", "d1")
    wts = {}
    comm.gathered(first, _all_gather([shards[k] for k in first], "weights_all_gather"), wts)

    gains = {n: (w[n] if n == "rel_bias" else w[n].reshape(1, -1)) for n in _SMALL}
    grad_x, _, small = _step(x, loss_target, gains, wts, comm)

    small_gathered = _all_gather([small[n] for n in _SMALL] + [small["loss"]], "small_all_gather")

    out_g, out_d, out_m, out_v = {}, {}, {}, {}
    me1 = me.astype(jnp.int32).reshape(1)
    big = {key: (name, by_col) for name, key, by_col in _BIG}

    def update(key):
        name, by_col = big[key]
        updates = _adam_big(stored(w, name, by_col), stored(m, name, by_col), stored(v, name, by_col),
                            comm.own32[key], comm.recv16[key], me1, comm.token)
        out_g[name], out_d[name], out_m[name], out_v[name] = ((a.T if by_col else a)[None] for a in updates)
        return updates[0]

    done = [update(key) for key in ("g2", "u2", "d2")]
    for _ in range(2):
        done = [update(key) for key in comm.scatter_wait(after=done[-1])]
    as_2d = lambda src: [src[n] if src[n].ndim == 2 else src[n].reshape(1, -1) for n in _SMALL]
    loss, updates = _adam_small(as_2d(w), as_2d(m), as_2d(v), small_gathered[:-1], small_gathered[-1])
    for n, (g, d, mn, vn) in zip(_SMALL, updates):
        out_g[n], out_d[n], out_m[n], out_v[n] = (a.reshape(w[n].shape) for a in (g, d, mn, vn))
    for key in comm.scatter_wait(after=loss):
        update(key)

    return (loss[0, 0], grad_x, *[out_g[n] for n in _WEIGHTS], *[out_d[n] for n in _WEIGHTS],
            *[out_m[n] for n in _WEIGHTS], *[out_v[n] for n in _WEIGHTS])
```

```python
import functools
import math

import jax
import jax.numpy as jnp
from jax import lax
from jax.experimental import pallas as pl
from jax.experimental.pallas import tpu as pltpu

F32 = jnp.float32
BF16 = jnp.bfloat16
MESH = pl.DeviceIdType.MESH

EPS = 1e-6
NEG = -1e30
N_DEV = 8

MLA_HEADS = 8
Q_LORA = 256
KV_LORA = 128
QK_NOPE = 64
QK_ROPE = 32
V_HEAD = 64
ROPE_THETA = 10000.0
SWA_HEADS = 8
SWA_KV_HEADS = 2
SWA_HEAD_DIM = 64
WINDOW = 128
REL_BUCKETS = 32
REL_MAX_DIST = 128
HEAD_BLOCK = 128
MLA_Q_TILES = (512, 256, 128)
MLA_SCALE = (QK_NOPE + QK_ROPE) ** -0.5
MLA_SCALE_LOG2 = MLA_SCALE * math.log2(math.e)
SWA_SCALE = SWA_HEAD_DIM ** -0.5

ADAM_LR = 0.001
ADAM_B1 = 0.9
ADAM_B2 = 0.999
ADAM_EPS = 1e-08
ADAM_WD = 0.01
ADAM_STEP = 10

VMEM_LIMIT = 56 * 1024 * 1024


def _cparams(semantics=None):
    return pltpu.CompilerParams(dimension_semantics=semantics, vmem_limit_bytes=VMEM_LIMIT)


def _dot(a, b):
    return jnp.dot(a, b, preferred_element_type=F32)


def _dot_nt(a, b):
    return lax.dot_general(a, b, (((1,), (1,)), ((), ())), preferred_element_type=F32)


def _dot_tn(a, b):
    return lax.dot_general(a, b, (((0,), (0,)), ((), ())), preferred_element_type=F32)


def _rms_fwd(x, g):
    r = lax.rsqrt(jnp.mean(x * x, axis=-1, keepdims=True) + EPS)
    return x * r * g, r


def _rms_bwd(x, g, r, dy):
    xh = x * r
    dyg = dy * g
    dx = r * (dyg - xh * jnp.mean(dyg * xh, axis=-1, keepdims=True))
    return dx, jnp.sum(dy * xh, axis=0, keepdims=True)


def _swap16(x):
    n = x.shape[-1]
    up = pltpu.roll(x, n - 16, 1)
    down = pltpu.roll(x, 16, 1)
    lane = lax.broadcasted_iota(jnp.int32, x.shape, 1) % HEAD_BLOCK
    return jnp.where((lane >= 64) & (lane < 80), up, jnp.where((lane >= 80) & (lane < 96), down, 0.0))


def _const_spec(shape):
    return pl.BlockSpec(shape, lambda *_: (0,) * len(shape), pipeline_mode=pl.Buffered(1))


def _acc_spec(shape):
    return pl.BlockSpec(shape, lambda *_: (0,) * len(shape))


def _row_spec(tm, width):
    return pl.BlockSpec((tm, width), lambda i: (i, 0))


def _cast_bf16(arrays):
    n = len(arrays)

    def body(*refs):
        for src, dst in zip(refs[:n], refs[n:]):
            dst[...] = src[...].astype(BF16)

    return pl.pallas_call(
        body, name="cast_weights",
        out_shape=[jax.ShapeDtypeStruct(a.shape, BF16) for a in arrays],
        compiler_params=_cparams(),
    )(*arrays)


def _all_gather(arrays, name):
    n = len(arrays)

    def body(*refs):
        ins, outs = refs[:n], refs[n:2 * n]
        send_sems, recv_sems, local_sems = refs[2 * n:]
        x, y, c = lax.axis_index("x"), lax.axis_index("y"), lax.axis_index("c")
        me, sibling = (x, y, c), (x, y, 1 - c)
        chips = [(1 - x, y), (x, 1 - y), (1 - x, 1 - y)]

        def slot(a, dev):
            return outs[a].at[4 * dev[0] + 2 * dev[1] + dev[2]]

        def copy(a, k, block, to, src=None):
            return pltpu.make_async_remote_copy(
                src_ref=slot(a, block) if src is None else src, dst_ref=slot(a, block),
                send_sem=send_sems.at[a, k], recv_sem=recv_sems.at[a, k], device_id=to, device_id_type=MESH)

        mine = [pltpu.make_async_copy(ins[a], slot(a, me), local_sems.at[a]) for a in range(n)]
        for cp in mine:
            cp.start()
        first = []
        for a in range(n):
            first.append(copy(a, 0, me, sibling, src=ins[a]))
            first += [copy(a, 1 + j, me, (*chip, c), src=ins[a]) for j, chip in enumerate(chips)]
        for cp in first:
            cp.start()
        passed = []
        for j, chip in enumerate(chips):
            for a in range(n):
                copy(a, 1 + j, (*chip, c), me).wait_recv()
                cp = copy(a, 4 + j, (*chip, c), sibling)
                cp.start()
                passed.append(cp)
        for a in range(n):
            copy(a, 0, sibling, me).wait_recv()
            for j, chip in enumerate(chips):
                copy(a, 4 + j, (*chip, 1 - c), me).wait_recv()
        for cp in first + passed:
            cp.wait_send()
        for cp in mine:
            cp.wait()

    any_spec = pl.BlockSpec(memory_space=pl.ANY)
    return pl.pallas_call(
        body, name=name,
        out_shape=[jax.ShapeDtypeStruct((N_DEV,) + a.shape, a.dtype) for a in arrays],
        in_specs=[any_spec] * n, out_specs=[any_spec] * n,
        scratch_shapes=[pltpu.SemaphoreType.DMA((n, 7)), pltpu.SemaphoreType.DMA((n, 7)), pltpu.SemaphoreType.DMA((n,))],
    )(*arrays)


def _exchange_shapes(jobs):
    dsts = [jax.ShapeDtypeStruct(((N_DEV,) + a.shape) if kind == "gather" else a.shape, a.dtype) for kind, a in jobs]
    n = len(jobs)
    sems = [pltpu.SemaphoreType.DMA((n, N_DEV - 1)), pltpu.SemaphoreType.DMA((n, N_DEV - 1)), pltpu.SemaphoreType.DMA((n,))]
    return dsts, sems


def _exchange_copies(kinds, srcs, dsts, send_sems, recv_sems, local_sems):
    x, y, c = lax.axis_index("x"), lax.axis_index("y"), lax.axis_index("c")
    me = 4 * x + 2 * y + c
    remote, local = [], []
    for a, kind in enumerate(kinds):
        for k in range(1, N_DEV):
            peer = (1 - x if k & 4 else x, 1 - y if k & 2 else y, 1 - c if k & 1 else c)
            src = srcs[a] if kind == "gather" else srcs[a].at[4 * peer[0] + 2 * peer[1] + peer[2]]
            remote.append(pltpu.make_async_remote_copy(
                src_ref=src, dst_ref=dsts[a].at[me], send_sem=send_sems.at[a, k - 1], recv_sem=recv_sems.at[a, k - 1],
                device_id=peer, device_id_type=MESH))
        src = srcs[a] if kind == "gather" else srcs[a].at[me]
        local.append(pltpu.make_async_copy(src, dsts[a].at[me], local_sems.at[a]))
    return remote, local


def _scatter_copies(srcs, lands, send_sems, recv_sems):
    x, y, c = lax.axis_index("x"), lax.axis_index("y"), lax.axis_index("c")
    me = 4 * x + 2 * y + c
    copies = []
    for a, (src_ref, land_ref) in enumerate(zip(srcs, lands)):
        for k in range(1, N_DEV):
            peer = (1 - x if k & 4 else x, 1 - y if k & 2 else y, 1 - c if k & 1 else c)
            copies.append(pltpu.make_async_remote_copy(
                src_ref=src_ref.at[4 * peer[0] + 2 * peer[1] + peer[2]], dst_ref=land_ref.at[me],
                send_sem=send_sems.at[a * (N_DEV - 1) + k - 1], recv_sem=recv_sems.at[a * (N_DEV - 1) + k - 1],
                device_id=peer, device_id_type=MESH))
    return copies


_SPLIT_EFFECT = pltpu.SideEffectType.DATAFLOW_SIDE_EFFECTING


def _scatter_start(blocks, name):
    n = len(blocks)
    hbm = pl.BlockSpec(memory_space=pltpu.HBM)
    sem = pl.BlockSpec(memory_space=pltpu.SEMAPHORE)

    def body(*refs):
        srcs, lands, send_sems, recv_sems, token = refs[:n], refs[n:2 * n], refs[2 * n], refs[2 * n + 1], refs[-1]
        for cp in _scatter_copies(srcs, lands, send_sems, recv_sems):
            cp.start()
        token[...] = jnp.zeros_like(token)

    through = [pltpu.HBM(b.shape, b.dtype) for b in blocks]
    *handle, token = pl.pallas_call(
        body, name=name,
        out_shape=(pltpu.SemaphoreType.DMA((n * (N_DEV - 1),)), pltpu.SemaphoreType.DMA((n * (N_DEV - 1),)), *through, *through,
                   jax.ShapeDtypeStruct((8, 128), F32)),
        in_specs=(hbm,) * (2 * n), out_specs=(sem, sem) + (hbm,) * (2 * n) + (pl.BlockSpec(memory_space=pltpu.VMEM),),
        input_output_aliases={i: 2 + i for i in range(2 * n)},
        compiler_params=pltpu.CompilerParams(has_side_effects=_SPLIT_EFFECT),
    )(*[pltpu.with_memory_space_constraint(b, pltpu.HBM) for b in blocks],
      *[pltpu.with_memory_space_constraint(lax.empty(b.shape, b.dtype), pltpu.HBM) for b in blocks])
    return handle, token


def _scatter_wait(handle, after, name):
    hbm = pl.BlockSpec(memory_space=pltpu.HBM)
    sem = pl.BlockSpec(memory_space=pltpu.SEMAPHORE)
    send_sems, recv_sems, *through = handle
    n = len(through) // 2

    def body(*refs):
        for cp in _scatter_copies(refs[:n], refs[n:2 * n], refs[2 * n], refs[2 * n + 1]):
            cp.wait_send()
            cp.wait_recv()

    outs = pl.pallas_call(
        body, name=name, out_shape=tuple(pltpu.HBM(t.shape, t.dtype) for t in through),
        in_specs=(hbm,) * (2 * n) + (sem, sem, pl.BlockSpec(memory_space=pl.ANY)), out_specs=(hbm,) * (2 * n),
        input_output_aliases={i: i for i in range(2 * n)},
        compiler_params=pltpu.CompilerParams(has_side_effects=_SPLIT_EFFECT),
    )(*through, send_sems, recv_sems, after)
    return list(outs[n:])


def _pcall(body, *, name, grid, in_specs, out_specs, out_shape, args, scratch_shapes=(), exchange=None, after=None):
    semantics = ("arbitrary",) * len(grid)
    exchange = exchange or []
    extra = [] if after is None else [after]
    if not exchange and not extra:
        outs = pl.pallas_call(body, name=name, grid=grid, in_specs=list(in_specs), out_specs=list(out_specs),
                              out_shape=list(out_shape), scratch_shapes=list(scratch_shapes),
                              compiler_params=_cparams(semantics))(*args)
        return list(outs), []
    kinds = [kind for kind, _ in exchange]
    n_in, n_out, n_job = len(in_specs), len(out_specs), len(exchange)
    dst_shapes, sems = _exchange_shapes(exchange) if exchange else ([], [])

    def wrapped(*refs):
        ins, srcs = refs[:n_in], refs[n_in:n_in + n_job]
        o0 = n_in + n_job + len(extra)
        outs, dsts, rest = refs[o0:o0 + n_out], refs[o0 + n_out:o0 + n_out + n_job], refs[o0 + n_out + n_job:]
        if not exchange:
            body(*ins, *outs, *rest)
            return
        scratch, sem_refs = rest[:-3], rest[-3:]
        first = functools.reduce(jnp.logical_and, [pl.program_id(d) == 0 for d in range(len(grid))])
        last = functools.reduce(jnp.logical_and, [pl.program_id(d) == grid[d] - 1 for d in range(len(grid))])

        @pl.when(first)
        def _():
            remote, local = _exchange_copies(kinds, srcs, dsts, *sem_refs)
            for cp in local + remote:
                cp.start()

        body(*ins, *outs, *scratch)

        @pl.when(last)
        def _():
            remote, local = _exchange_copies(kinds, srcs, dsts, *sem_refs)
            for cp in remote + local:
                cp.wait()

    any_spec = pl.BlockSpec(memory_space=pl.ANY)
    outs = pl.pallas_call(
        wrapped, name=name, grid=grid, in_specs=list(in_specs) + [any_spec] * (n_job + len(extra)),
        out_specs=list(out_specs) + [any_spec] * n_job, out_shape=list(out_shape) + dst_shapes,
        scratch_shapes=list(scratch_shapes) + sems, compiler_params=_cparams(semantics),
    )(*args, *[a for _, a in exchange], *extra)
    return list(outs[:n_out]), list(outs[n_out:])


def _adam_math(w, g, m, v):
    m = ADAM_B1 * m + (1.0 - ADAM_B1) * g
    v = ADAM_B2 * v + (1.0 - ADAM_B2) * (g * g)
    m_hat = m / (1.0 - ADAM_B1 ** ADAM_STEP)
    v_hat = v / (1.0 - ADAM_B2 ** ADAM_STEP)
    delta = -ADAM_LR * (m_hat / (jnp.sqrt(v_hat) + ADAM_EPS) + ADAM_WD * w)
    return delta, m, v


def _adam_big(w, m, v, own32, recv16, me, after):
    r, c = w.shape
    tr = r if r <= 512 else (512 if r % 512 == 0 else 352)
    assert r % tr == 0

    def body(me_ref, w_ref, m_ref, v_ref, s_ref, r_ref, after_ref, g_out, d_out, m_out, v_out):
        g = s_ref[...]
        for k in range(1, N_DEV):
            g = g + r_ref[jnp.bitwise_xor(me_ref[0], k)].astype(F32)
        d, mn, vn = _adam_math(w_ref[...], g, m_ref[...], v_ref[...])
        g_out[...] = g
        d_out[...] = d
        m_out[...] = mn
        v_out[...] = vn

    blk = pl.BlockSpec((tr, c), lambda i, s: (i, 0))
    grid_spec = pltpu.PrefetchScalarGridSpec(
        num_scalar_prefetch=1, grid=(r // tr,),
        in_specs=[blk, blk, blk, pl.BlockSpec((None, tr, c), lambda i, s: (s[0], i, 0)),
                  pl.BlockSpec((N_DEV, tr, c), lambda i, s: (0, i, 0)), pl.BlockSpec(memory_space=pl.ANY)],
        out_specs=[blk, blk, blk, blk])
    return pl.pallas_call(
        body, name="adamw_big", grid_spec=grid_spec,
        out_shape=[jax.ShapeDtypeStruct((r, c), F32)] * 4,
        compiler_params=_cparams(("arbitrary",)),
    )(me, w, m, v, own32, recv16, after)


def _adam_small(ws, ms, vs, partials, loss_parts):
    n = len(ws)

    def body(*refs):
        w_refs, m_refs, v_refs, p_refs = (refs[i * n:(i + 1) * n] for i in range(4))
        l_ref, loss_out = refs[4 * n], refs[4 * n + 1]
        outs = refs[4 * n + 2:]

        def total(ref):
            s = ref[0]
            for dev in range(1, N_DEV):
                s = s + ref[dev]
            return s

        loss_out[...] = total(l_ref)
        for i in range(n):
            r, c = w_refs[i].shape
            g = total(p_refs[i])[:, :c]
            d, mn, vn = _adam_math(w_refs[i][...], g, m_refs[i][...], v_refs[i][...])
            for j, val in enumerate((g, d, mn, vn)):
                outs[4 * i + j][...] = val

    shapes = [jax.ShapeDtypeStruct((1, 128), F32)]
    for w in ws:
        shapes += [jax.ShapeDtypeStruct(w.shape, F32)] * 4
    outs = pl.pallas_call(body, name="adamw_small", out_shape=shapes, compiler_params=_cparams())(*ws, *ms, *vs, *partials, loss_parts)
    return outs[0], [outs[1 + 4 * i:5 + 4 * i] for i in range(n)]


def _pick(n, prefer):
    for t in prefer:
        if n % t == 0:
            return t
    return n


def _ffn_fwd(x, g, wg, wu, wd, name, exchange=None, loss_head=None):
    T, D = x.shape
    F = wg.shape[0]
    tm = _pick(T, (512, 256, 128))
    fc = _pick(F, (256, 128))

    def body(x_ref, g_ref, wg_ref, wu_ref, wd_ref, *rest):
        h_ref, n_ref, gate_ref, up_ref, act_ref = rest[-7:-2] if loss_head else rest
        xv = x_ref[...]
        n, _ = _rms_fwd(xv, g_ref[...])
        nb = n.astype(BF16)
        n_ref[...] = nb
        acc = jnp.zeros((tm, D), F32)
        for f0 in range(0, F, fc):
            gate = _dot_nt(nb, wg_ref[f0:f0 + fc, :])
            up = _dot_nt(nb, wu_ref[f0:f0 + fc, :])
            gate_ref[:, f0:f0 + fc] = gate.astype(BF16)
            up_ref[:, f0:f0 + fc] = up.astype(BF16)
            act = (gate * (1.0 / (1.0 + jnp.exp(-gate))) * up).astype(BF16)
            act_ref[:, f0:f0 + fc] = act
            acc = acc + _dot(act, wd_ref[f0:f0 + fc, :])
        hv = xv + 0.5 * acc
        if not loss_head:
            h_ref[...] = hv
            return
        t_ref, gf_ref, loss_ref, dgf_ref = rest[0], rest[1], rest[-2], rest[-1]
        gf = gf_ref[...]
        y, r = _rms_fwd(hv, gf)
        diff = y - t_ref[...]
        dh, dgp = _rms_bwd(hv, gf, r, diff * (1.0 / D))
        h_ref[...] = dh
        part = 0.5 * jnp.sum(jnp.sum(diff * diff, axis=-1, keepdims=True) * (1.0 / D), axis=0, keepdims=True)

        @pl.when(pl.program_id(0) == 0)
        def _():
            loss_ref[...] = jnp.zeros_like(loss_ref)
            dgf_ref[...] = jnp.zeros_like(dgf_ref)

        loss_ref[...] += jnp.broadcast_to(part, loss_ref.shape)
        dgf_ref[...] += dgp

    in_specs = [_row_spec(tm, D), _const_spec((1, D)), _const_spec((F, D)), _const_spec((F, D)), _const_spec((F, D))]
    out_specs = [_row_spec(tm, D), _row_spec(tm, D), _row_spec(tm, F), _row_spec(tm, F), _row_spec(tm, F)]
    out_shape = [jax.ShapeDtypeStruct((T, D), F32), jax.ShapeDtypeStruct((T, D), BF16),
                 jax.ShapeDtypeStruct((T, F), BF16), jax.ShapeDtypeStruct((T, F), BF16), jax.ShapeDtypeStruct((T, F), BF16)]
    args = (x, g, wg, wu, wd)
    if loss_head:
        in_specs += [_row_spec(tm, D), _const_spec((1, D))]
        out_specs += [_acc_spec((1, 128)), _acc_spec((1, D))]
        out_shape += [jax.ShapeDtypeStruct((1, 128), F32), jax.ShapeDtypeStruct((1, D), F32)]
        args += tuple(loss_head)
    return _pcall(body, name=name, grid=(T // tm,), in_specs=in_specs, out_specs=out_specs, out_shape=out_shape, args=args,
                  exchange=exchange)


FFN_BWD_CHUNK = 1408
FFN_BWD_GROUP = 1408


def _ffn_bwd(x, g, gate, up, dh, wgt, wut, wdt, name, exchange=None, after=None):
    T, D = x.shape
    F = wgt.shape[0]
    tm = _pick(T, (512, 256, 128))
    halves = 2
    Fh = F // halves
    assert Fh * halves == F and Fh % 128 == 0
    groups = [(g0, min(g0 + FFN_BWD_GROUP, Fh)) for g0 in range(0, Fh, FFN_BWD_GROUP)]

    def body(x_ref, g_ref, gate_ref, up_ref, dh_ref, wg_ref, wu_ref, wd_ref, dx_ref, dg_ref, dgate_ref, dup_ref, dn_ref):
        i, h = pl.program_id(0), pl.program_id(1)
        dhv = dh_ref[...]
        dho = (0.5 * dhv).astype(BF16)
        dn = jnp.zeros((tm, D), F32)
        for g0, g1 in groups:
            for f0 in range(g0, g1, FFN_BWD_CHUNK):
                f1 = min(f0 + FFN_BWD_CHUNK, g1)
                d_act = _dot_nt(dho, wd_ref[f0:f1, :])
                gt = gate_ref[:, f0:f1].astype(F32)
                u = up_ref[:, f0:f1].astype(F32)
                sg = 1.0 / (1.0 + jnp.exp(-gt))
                silu = gt * sg
                dup_ref[:, f0:f1] = (d_act * silu).astype(BF16)
                dgate_ref[:, f0:f1] = (d_act * u * (sg + silu * (1.0 - sg))).astype(BF16)
            dn = dn + _dot(dgate_ref[:, g0:g1], wg_ref[g0:g1, :]) + _dot(dup_ref[:, g0:g1], wu_ref[g0:g1, :])

        @pl.when(h == 0)
        def _():
            dn_ref[...] = dn

        @pl.when(h > 0)
        def _():
            dn_ref[...] += dn

        @pl.when((i == 0) & (h == 0))
        def _():
            dg_ref[...] = jnp.zeros_like(dg_ref)

        @pl.when(h == halves - 1)
        def _():
            xv, gv = x_ref[...], g_ref[...]
            _, r = _rms_fwd(xv, gv)
            dxn, dgp = _rms_bwd(xv, gv, r, dn_ref[...])
            dx_ref[...] = dhv + dxn
            dg_ref[...] += dgp

    tile = pl.BlockSpec((tm, D), lambda i, h: (i, 0))
    wide = pl.BlockSpec((tm, Fh), lambda i, h: (i, h))
    return _pcall(
        body, name=name, grid=(T // tm, halves),
        in_specs=[tile, _const_spec((1, D)), wide, wide, tile,
                  pl.BlockSpec((Fh, D), lambda i, h: (h, 0)), pl.BlockSpec((Fh, D), lambda i, h: (h, 0)),
                  pl.BlockSpec((Fh, D), lambda i, h: (h, 0))],
        out_specs=[tile, _acc_spec((1, D)), wide, wide],
        out_shape=[jax.ShapeDtypeStruct((T, D), F32), jax.ShapeDtypeStruct((1, D), F32),
                   jax.ShapeDtypeStruct((T, F), BF16), jax.ShapeDtypeStruct((T, F), BF16)],
        scratch_shapes=[pltpu.VMEM((tm, D), F32)],
        args=(x, g, gate, up, dh, wgt, wut, wdt), exchange=exchange, after=after)


def _matmul_tn(a, b, name, scale=None, exchange=None, after=None):
    T, K = a.shape
    N = b.shape[1]
    tk = _pick(K, (1024, 1408, 1280, 768, 512, 256, 128))
    tn = _pick(N, (1024, 1408, 1280, 768, 512, 256, 128))
    tt = _pick(T, (2048, 1024, 512, 256, 128))
    n_t = T // tt

    def body(a_ref, b_ref, o_ref, o16_ref):
        @pl.when(pl.program_id(2) == 0)
        def _():
            o_ref[...] = jnp.zeros_like(o_ref)

        o_ref[...] += _dot_tn(a_ref[...].astype(BF16), b_ref[...].astype(BF16))

        @pl.when(pl.program_id(2) == n_t - 1)
        def _():
            if scale is not None:
                o_ref[...] = o_ref[...] * scale
            o16_ref[...] = o_ref[...].astype(BF16)

    tile = pl.BlockSpec((tk, tn), lambda i, j, t: (i, j))
    outs, got = _pcall(
        body, name=name, grid=(K // tk, N // tn, n_t),
        in_specs=[pl.BlockSpec((tt, tk), lambda i, j, t: (t, i)), pl.BlockSpec((tt, tn), lambda i, j, t: (t, j))],
        out_specs=[tile, tile],
        out_shape=[jax.ShapeDtypeStruct((K, N), F32), jax.ShapeDtypeStruct((K, N), BF16)],
        args=(a, b), exchange=exchange, after=after)
    return tuple(outs), got


def _rope_spec(tm, S):
    per = S // tm
    return pl.BlockSpec((tm, HEAD_BLOCK), lambda i: (i % per, 0))


def _proj_fwd(h1, g_mix, w_in, g_q, wq, g_kv, wkv, rope_c, rope_s, S):
    T, D = h1.shape
    P = w_in.shape[0]
    tm = _pick(S, (512, 256, 128))
    QW = MLA_HEADS * HEAD_BLOCK
    VW = MLA_HEADS * V_HEAD
    SQ = SWA_HEADS * SWA_HEAD_DIM
    SK = SWA_KV_HEADS * SWA_HEAD_DIM
    o_cq, o_ckv, o_qs, o_ks, o_vs, o_kpe = 0, Q_LORA, Q_LORA + KV_LORA, Q_LORA + KV_LORA + SQ, Q_LORA + KV_LORA + SQ + SK, Q_LORA + KV_LORA + SQ + 2 * SK
    assert P == o_kpe + HEAD_BLOCK

    def body(h_ref, gm_ref, win_ref, gq_ref, wq_ref, gkv_ref, wkv_ref, c_ref, s_ref,
             u_ref, cq_ref, ckv_ref, cqn_ref, ckvn_ref, q_ref, k_ref, v_ref, qs_ref, ks_ref, vs_ref):
        u, _ = _rms_fwd(h_ref[...], gm_ref[...])
        ub = u.astype(BF16)
        u_ref[...] = ub
        proj = _dot_nt(ub, win_ref[...])
        c_q = proj[:, o_cq:o_ckv]
        c_kv = proj[:, o_ckv:o_qs]
        cq_ref[...] = c_q
        ckv_ref[...] = c_kv
        qs_ref[...] = proj[:, o_qs:o_ks].astype(BF16)
        ks_ref[...] = proj[:, o_ks:o_vs].astype(BF16)
        vs_ref[...] = proj[:, o_vs:o_kpe].astype(BF16)
        cb, sb = c_ref[...], s_ref[...]
        kpe = proj[:, o_kpe:P]
        kpe = kpe * cb + _swap16(kpe) * sb
        cqn, _ = _rms_fwd(c_q, gq_ref[...])
        cqn = cqn.astype(BF16)
        cqn_ref[...] = cqn
        q = _dot_nt(cqn, wq_ref[...])
        q = q * jnp.tile(cb, (1, MLA_HEADS)) + _swap16(q) * jnp.tile(sb, (1, MLA_HEADS))
        q_ref[...] = q.astype(BF16)
        ckvn, _ = _rms_fwd(c_kv, gkv_ref[...])
        ckvn = ckvn.astype(BF16)
        ckvn_ref[...] = ckvn
        kv = _dot_nt(ckvn, wkv_ref[...])
        k_ref[...] = (kv[:, :QW] + jnp.tile(kpe, (1, MLA_HEADS))).astype(BF16)
        v_ref[...] = kv[:, QW:].astype(BF16)

    widths = [(D, BF16), (Q_LORA, F32), (KV_LORA, F32), (Q_LORA, BF16), (KV_LORA, BF16), (QW, BF16), (QW, BF16), (VW, BF16),
              (SQ, BF16), (SK, BF16), (SK, BF16)]
    return pl.pallas_call(
        body, name="proj_fwd", grid=(T // tm,),
        in_specs=[_row_spec(tm, D), _const_spec((1, D)), _const_spec((P, D)), _const_spec((1, Q_LORA)),
                  _const_spec((QW, Q_LORA)), _const_spec((1, KV_LORA)), _const_spec((QW + VW, KV_LORA)),
                  _rope_spec(tm, S), _rope_spec(tm, S)],
        out_specs=[_row_spec(tm, w) for w, _ in widths],
        out_shape=[jax.ShapeDtypeStruct((T, w), dt) for w, dt in widths],
        compiler_params=_cparams(("arbitrary",)),
    )(h1, g_mix, w_in, g_q, wq, g_kv, wkv, rope_c, rope_s)


def _proj_bwd(dq, dk, dv, dqs, dks, dvs, c_q, c_kv, h1, dh2, g_mix, w_in, g_q, wq, g_kv, wkv, rope_c, rope_s, S):
    T, D = h1.shape
    P = w_in.shape[0]
    tm = _pick(S, (512, 256, 128))
    QW = MLA_HEADS * HEAD_BLOCK
    VW = MLA_HEADS * V_HEAD
    SQ = SWA_HEADS * SWA_HEAD_DIM
    SK = SWA_KV_HEADS * SWA_HEAD_DIM

    def body(dq_ref, dk_ref, dv_ref, dqs_ref, dks_ref, dvs_ref, cq_ref, ckv_ref, h_ref, dh2_ref, gm_ref, win_ref, gq_ref,
             wq_ref, gkv_ref, wkv_ref, c_ref, s_ref, dh1_ref, dproj_ref, dql_ref, dkv_ref, dgm_ref, dgq_ref, dgkv_ref):
        cb, sb = c_ref[...], s_ref[...]
        dqv = dq_ref[...]
        dql = dqv * jnp.tile(cb, (1, MLA_HEADS)) + _swap16(dqv * jnp.tile(sb, (1, MLA_HEADS)))
        dql = dql.astype(BF16)
        dql_ref[...] = dql
        c_q = cq_ref[...]
        _, rq = _rms_fwd(c_q, gq_ref[...])
        d_cq, dgq = _rms_bwd(c_q, gq_ref[...], rq, _dot(dql, wq_ref[...]))

        dkv_all = dk_ref[...]
        dkpe = dkv_all[:, 0:HEAD_BLOCK]
        for h in range(1, MLA_HEADS):
            dkpe = dkpe + dkv_all[:, h * HEAD_BLOCK:(h + 1) * HEAD_BLOCK]
        lane = lax.broadcasted_iota(jnp.int32, dkpe.shape, 1)
        dkpe = jnp.where((lane >= 64) & (lane < 96), dkpe, 0.0)
        dkpe = dkpe * cb + _swap16(dkpe * sb)
        dkvc = jnp.concatenate([dkv_all.astype(BF16), dv_ref[...].astype(BF16)], axis=1)
        dkv_ref[...] = dkvc
        c_kv = ckv_ref[...]
        _, rkv = _rms_fwd(c_kv, gkv_ref[...])
        d_ckv, dgkv = _rms_bwd(c_kv, gkv_ref[...], rkv, _dot(dkvc, wkv_ref[...]))

        dproj = jnp.concatenate([d_cq.astype(BF16), d_ckv.astype(BF16), dqs_ref[...].astype(BF16),
                                 dks_ref[...].astype(BF16), dvs_ref[...].astype(BF16), dkpe.astype(BF16)], axis=1)
        dproj_ref[...] = dproj
        hv = h_ref[...]
        _, rm = _rms_fwd(hv, gm_ref[...])
        dxn, dgm = _rms_bwd(hv, gm_ref[...], rm, _dot(dproj, win_ref[...]))
        dh1_ref[...] = dh2_ref[...] + dxn

        @pl.when(pl.program_id(0) == 0)
        def _():
            dgm_ref[...] = jnp.zeros_like(dgm_ref)
            dgq_ref[...] = jnp.zeros_like(dgq_ref)
            dgkv_ref[...] = jnp.zeros_like(dgkv_ref)

        dgm_ref[...] += dgm
        dgq_ref[...] += dgq
        dgkv_ref[...] += dgkv

    return pl.pallas_call(
        body, name="proj_bwd", grid=(T // tm,),
        in_specs=[_row_spec(tm, QW), _row_spec(tm, QW), _row_spec(tm, VW), _row_spec(tm, SQ), _row_spec(tm, SK), _row_spec(tm, SK),
                  _row_spec(tm, Q_LORA), _row_spec(tm, KV_LORA), _row_spec(tm, D), _row_spec(tm, D),
                  _const_spec((1, D)), _const_spec((P, D)), _const_spec((1, Q_LORA)), _const_spec((QW, Q_LORA)),
                  _const_spec((1, KV_LORA)), _const_spec((QW + VW, KV_LORA)), _rope_spec(tm, S), _rope_spec(tm, S)],
        out_specs=[_row_spec(tm, D), _row_spec(tm, P), _row_spec(tm, QW), _row_spec(tm, QW + VW),
                   _acc_spec((1, D)), _acc_spec((1, Q_LORA)), _acc_spec((1, KV_LORA))],
        out_shape=[jax.ShapeDtypeStruct((T, D), F32), jax.ShapeDtypeStruct((T, P), BF16), jax.ShapeDtypeStruct((T, QW), BF16),
                   jax.ShapeDtypeStruct((T, QW + VW), BF16), jax.ShapeDtypeStruct((1, D), F32),
                   jax.ShapeDtypeStruct((1, Q_LORA), F32), jax.ShapeDtypeStruct((1, KV_LORA), F32)],
        compiler_params=_cparams(("arbitrary",)),
    )(dq, dk, dv, dqs, dks, dvs, c_q, c_kv, h1, dh2, g_mix, w_in, g_q, wq, g_kv, wkv, rope_c, rope_s)


def _out_fwd(o_mla, o_swa, g_mla, g_swa, w_o, h1):
    T, D = h1.shape
    W = o_mla.shape[1]
    tm = _pick(T, (512, 256, 128))

    def body(om_ref, os_ref, gm_ref, gs_ref, wo_ref, h_ref, h2_ref, oc_ref):
        a, _ = _rms_fwd(om_ref[...], gm_ref[...])
        b, _ = _rms_fwd(os_ref[...], gs_ref[...])
        oc = jnp.concatenate([a.astype(BF16), b.astype(BF16)], axis=1)
        oc_ref[...] = oc
        h2_ref[...] = h_ref[...] + _dot(oc, wo_ref[...])

    return pl.pallas_call(
        body, name="out_fwd", grid=(T // tm,),
        in_specs=[_row_spec(tm, W), _row_spec(tm, W), _const_spec((1, W)), _const_spec((1, W)), _const_spec((2 * W, D)),
                  _row_spec(tm, D)],
        out_specs=[_row_spec(tm, D), _row_spec(tm, 2 * W)],
        out_shape=[jax.ShapeDtypeStruct((T, D), F32), jax.ShapeDtypeStruct((T, 2 * W), BF16)],
        compiler_params=_cparams(("arbitrary",)),
    )(o_mla, o_swa, g_mla, g_swa, w_o, h1)


def _out_bwd(dh2, o_mla, o_swa, g_mla, g_swa, w_o):
    T, D = dh2.shape
    W = o_mla.shape[1]
    tm = _pick(T, (512, 256, 128))

    def body(dh_ref, om_ref, os_ref, gm_ref, gs_ref, wo_ref, dom_ref, dos_ref, dgm_ref, dgs_ref):
        doc = _dot_nt(dh_ref[...].astype(BF16), wo_ref[...])
        om, osw = om_ref[...], os_ref[...]
        _, ra = _rms_fwd(om, gm_ref[...])
        _, rb = _rms_fwd(osw, gs_ref[...])
        da, dga = _rms_bwd(om, gm_ref[...], ra, doc[:, :W])
        db, dgb = _rms_bwd(osw, gs_ref[...], rb, doc[:, W:])
        dom_ref[...] = da
        dos_ref[...] = db

        @pl.when(pl.program_id(0) == 0)
        def _():
            dgm_ref[...] = jnp.zeros_like(dgm_ref)
            dgs_ref[...] = jnp.zeros_like(dgs_ref)

        dgm_ref[...] += dga
        dgs_ref[...] += dgb

    return pl.pallas_call(
        body, name="out_bwd", grid=(T // tm,),
        in_specs=[_row_spec(tm, D), _row_spec(tm, W), _row_spec(tm, W), _const_spec((1, W)), _const_spec((1, W)),
                  _const_spec((2 * W, D))],
        out_specs=[_row_spec(tm, W), _row_spec(tm, W), _acc_spec((1, W)), _acc_spec((1, W))],
        out_shape=[jax.ShapeDtypeStruct((T, W), F32), jax.ShapeDtypeStruct((T, W), F32),
                   jax.ShapeDtypeStruct((1, W), F32), jax.ShapeDtypeStruct((1, W), F32)],
        compiler_params=_cparams(("arbitrary",)),
    )(dh2, o_mla, o_swa, g_mla, g_swa, w_o)


def _half_mask(shape, half):
    lane = lax.broadcasted_iota(jnp.int32, shape, len(shape) - 1)
    return (lane < 64) if half == 0 else (lane >= 64)


def _mla_fwd(q, k, v, B, S, exchange=None):
    T = B * S
    tq = _pick(S, MLA_Q_TILES)
    nq = S // tq
    pairs = MLA_HEADS // 2

    def body(q_ref, k_ref, v_ref, o_ref, lse_ref):
        causal = lax.broadcasted_iota(jnp.int32, (tq, tq), 1) <= lax.broadcasted_iota(jnp.int32, (tq, tq), 0)
        low = _half_mask((tq, 2 * V_HEAD), 0)
        for qi in range(nq):
            rows = slice(qi * tq, (qi + 1) * tq)
            past = slice(0, qi * tq)
            outs, lses = [], []
            for half in range(2):
                lanes = slice(half * HEAD_BLOCK, (half + 1) * HEAD_BLOCK)
                qv = q_ref[rows, lanes]
                s_d = jnp.where(causal, _dot_nt(qv, k_ref[rows, lanes]) * MLA_SCALE_LOG2, NEG)
                m = jnp.max(s_d, axis=-1, keepdims=True)
                if qi:
                    s_p = _dot_nt(qv, k_ref[past, lanes]) * MLA_SCALE_LOG2
                    m = jnp.maximum(m, jnp.max(s_p, axis=-1, keepdims=True))
                p_d = jnp.exp2(s_d - m)
                l = jnp.sum(p_d, axis=-1, keepdims=True)
                acc = _dot(p_d.astype(BF16), v_ref[rows, :])
                if qi:
                    p_p = jnp.exp2(s_p - m)
                    l = l + jnp.sum(p_p, axis=-1, keepdims=True)
                    acc = acc + _dot(p_p.astype(BF16), v_ref[past, :])
                outs.append(acc * (1.0 / l))
                lses.append(jnp.broadcast_to(m + jnp.log2(l), (tq, 2 * V_HEAD)))
            o_ref[rows, :] = jnp.where(low, outs[0], outs[1])
            lse_ref[rows, :] = jnp.where(low, lses[0], lses[1])

    blk2 = pl.BlockSpec((S, 2 * HEAD_BLOCK), lambda b, j: (b, j))
    blk1 = pl.BlockSpec((S, 2 * V_HEAD), lambda b, j: (b, j))
    return _pcall(
        body, name="mla_fwd", grid=(B, pairs),
        in_specs=[blk2, blk2, blk1], out_specs=[blk1, blk1],
        out_shape=[jax.ShapeDtypeStruct((T, pairs * 2 * V_HEAD), F32)] * 2,
        args=(q, k, v), exchange=exchange)


def _mla_bwd(q, k, v, do, o, lse, B, S, exchange=None):
    T = B * S
    tq = _pick(S, MLA_Q_TILES)
    nq = S // tq
    pairs = MLA_HEADS // 2

    def body(q_ref, k_ref, v_ref, do_ref, o_ref, lse_ref, dq_ref, dk_ref, dv_ref, dkt_acc, dvt_acc):
        causal = lax.broadcasted_iota(jnp.int32, (tq, tq), 1) <= lax.broadcasted_iota(jnp.int32, (tq, tq), 0)
        for half in range(2):
            lanes = slice(half * HEAD_BLOCK, (half + 1) * HEAD_BLOCK)
            col = half * V_HEAD
            own = _half_mask((tq, 2 * V_HEAD), half)
            for qi in reversed(range(nq)):
                rows = slice(qi * tq, (qi + 1) * tq)
                past = slice(0, qi * tq)
                first = qi == nq - 1
                qv = q_ref[rows, lanes]
                dov = jnp.where(own, do_ref[rows, :], 0.0)
                dsum = jnp.sum(dov * o_ref[rows, :], axis=-1, keepdims=True)
                dob = dov.astype(BF16)
                lse_col = lse_ref[rows, col:col + 1]

                def grads(keys, mask):
                    kv, vv = k_ref[keys, lanes], v_ref[keys, :]
                    p = jnp.exp2(_dot_nt(qv, kv) * MLA_SCALE_LOG2 - lse_col)
                    if mask is not None:
                        p = jnp.where(mask, p, 0.0)
                    ds = (p * (_dot_nt(dob, vv) - dsum) * MLA_SCALE).astype(BF16)
                    dkt, dvt = _dot_tn(qv, ds), _dot_tn(dob, p.astype(BF16))
                    if first:
                        dkt_acc[:, keys] = dkt
                    else:
                        dkt_acc[:, keys] += dkt
                    if first and half == 0:
                        dvt_acc[:, keys] = dvt
                    else:
                        dvt_acc[:, keys] += dvt
                    return _dot(ds, kv)

                dq = grads(rows, causal)
                if qi:
                    dq = dq + grads(past, None)
                dq_ref[rows, lanes] = dq
            dk_ref[:, lanes] = dkt_acc[...].T
        dv_ref[...] = dvt_acc[...].T

    blk2 = pl.BlockSpec((S, 2 * HEAD_BLOCK), lambda b, j: (b, j))
    blk1 = pl.BlockSpec((S, 2 * V_HEAD), lambda b, j: (b, j))
    return _pcall(
        body, name="mla_bwd", grid=(B, pairs),
        in_specs=[blk2, blk2, blk1, blk1, blk1, blk1], out_specs=[blk2, blk2, blk1],
        out_shape=[jax.ShapeDtypeStruct((T, MLA_HEADS * HEAD_BLOCK), F32), jax.ShapeDtypeStruct((T, MLA_HEADS * HEAD_BLOCK), F32),
                   jax.ShapeDtypeStruct((T, pairs * 2 * V_HEAD), F32)],
        scratch_shapes=[pltpu.VMEM((HEAD_BLOCK, S), F32), pltpu.VMEM((2 * V_HEAD, S), F32)],
        args=(q, k, v, do, o, lse), exchange=exchange)


SWA_GROUP = SWA_HEADS // SWA_KV_HEADS
SWA_ROWS = SWA_GROUP * WINDOW


SWA_CHUNK = 8


def _bdot(a, b, ca, cb):
    return lax.dot_general(a, b, (((ca,), (cb,)), ((0,), (0,))), preferred_element_type=F32)


def _swa_chunks(nb):
    cb = next(c for c in (SWA_CHUNK, 4, 2, 1) if nb % c == 0)
    return cb, [(n0, slice(n0 * WINDOW, (n0 + cb) * WINDOW)) for n0 in range(0, nb, cb)]


def _swa_keys(ref, n0, cb):
    W = WINDOW
    cur = ref[n0 * W:(n0 + cb) * W, :]
    if n0:
        prev = ref[(n0 - 1) * W:(n0 + cb - 1) * W, :]
    else:
        prev = jnp.concatenate([cur[:W], cur[:(cb - 1) * W]], axis=0) if cb > 1 else cur
    return jnp.concatenate([prev.reshape(cb, W, W), cur.reshape(cb, W, W)], axis=1)


def _swa_stack(ref, rows, cb):
    return jnp.concatenate([ref[rows, j * WINDOW:(j + 1) * WINDOW].reshape(cb, WINDOW, WINDOW) for j in range(SWA_GROUP)], axis=1)


def _swa_unstack(low, high, cb):
    sel = _half_mask((cb, WINDOW, WINDOW), 0)
    return jnp.concatenate([jnp.where(sel, low[:, j * WINDOW:(j + 1) * WINDOW], high[:, j * WINDOW:(j + 1) * WINDOW])
                            .reshape(cb * WINDOW, WINDOW) for j in range(SWA_GROUP)], axis=1)


def _swa_sink_col(sink_ref, g):
    return jnp.concatenate([jnp.broadcast_to(sink_ref[g * SWA_GROUP + j:g * SWA_GROUP + j + 1, 0:1], (WINDOW, 1))
                            for j in range(SWA_GROUP)], axis=0)


def _swa_probs(qm, kcat, bias_g, sink, first_chunk):
    shape = (qm.shape[0], SWA_ROWS, 2 * WINDOW)
    qi = lax.broadcasted_iota(jnp.int32, shape, 1) % WINDOW
    col = lax.broadcasted_iota(jnp.int32, shape, 2)
    valid = (col > qi) & (col <= qi + WINDOW)
    if first_chunk:
        valid = valid & ((col >= WINDOW) | (lax.broadcasted_iota(jnp.int32, shape, 0) > 0))
    s = jnp.where(valid, _bdot(qm, kcat, 2, 2) * SWA_SCALE + bias_g, NEG)
    m = jnp.maximum(jnp.max(s, axis=-1, keepdims=True), sink)
    e = jnp.exp(s - m)
    e_s = jnp.exp(sink - m)
    inv = 1.0 / (jnp.sum(e, axis=-1, keepdims=True) + e_s)
    return e * inv, e_s * inv


def _swa_fwd(q, k, v, bias, sinks, B, S):
    T = B * S
    W = WINDOW
    nb = S // W

    cb, chunks = _swa_chunks(nb)

    def body(q_ref, k_ref, v_ref, bias_ref, sink_ref, o_ref):
        sink_cols = [_swa_sink_col(sink_ref, g) for g in range(SWA_KV_HEADS)]
        for n0, rows in chunks:
            kcat, vcat = _swa_keys(k_ref, n0, cb), _swa_keys(v_ref, n0, cb)
            qcat = _swa_stack(q_ref, rows, cb)
            outs = []
            for g in range(SWA_KV_HEADS):
                qm = jnp.where(_half_mask(qcat.shape, g), qcat, jnp.zeros_like(qcat))
                p, _ = _swa_probs(qm, kcat, bias_ref[g], sink_cols[g], n0 == 0)
                outs.append(_bdot(p.astype(BF16), vcat, 2, 1))
            o_ref[rows, :] = _swa_unstack(outs[0], outs[1], cb)

    SQ, SK = SWA_HEADS * SWA_HEAD_DIM, SWA_KV_HEADS * SWA_HEAD_DIM
    return pl.pallas_call(
        body, name="swa_fwd", grid=(B,),
        in_specs=[pl.BlockSpec((S, SQ), lambda b: (b, 0)), pl.BlockSpec((S, SK), lambda b: (b, 0)),
                  pl.BlockSpec((S, SK), lambda b: (b, 0)), _const_spec((SWA_KV_HEADS, SWA_ROWS, 2 * W)),
                  _const_spec((SWA_HEADS, 128))],
        out_specs=pl.BlockSpec((S, SQ), lambda b: (b, 0)),
        out_shape=jax.ShapeDtypeStruct((T, SQ), F32),
        compiler_params=_cparams(("arbitrary",)),
    )(q, k, v, bias, sinks)


def _swa_bwd(q, k, v, do, bias, sinks, B, S):
    T = B * S
    W = WINDOW
    nb = S // W
    cb, chunks = _swa_chunks(nb)

    def body(q_ref, k_ref, v_ref, do_ref, bias_ref, sink_ref, dq_ref, dk_ref, dv_ref, dbias_ref, dsink_ref, dkt_acc, dvt_acc):
        @pl.when(pl.program_id(0) == 0)
        def _():
            dbias_ref[...] = jnp.zeros_like(dbias_ref)
            dsink_ref[...] = jnp.zeros_like(dsink_ref)

        dkt_acc[...] = jnp.zeros_like(dkt_acc)
        dvt_acc[...] = jnp.zeros_like(dvt_acc)
        sink_cols = [_swa_sink_col(sink_ref, g) for g in range(SWA_KV_HEADS)]
        for n0, rows in chunks:
            kcat, vcat = _swa_keys(k_ref, n0, cb), _swa_keys(v_ref, n0, cb)
            qcat = _swa_stack(q_ref, rows, cb)
            docat = _swa_stack(do_ref, rows, cb)
            dkt = jnp.zeros((cb, W, 2 * W), F32)
            dvt = jnp.zeros((cb, W, 2 * W), F32)
            dqs = []
            for g in range(SWA_KV_HEADS):
                own = _half_mask(qcat.shape, g)
                qm = jnp.where(own, qcat, jnp.zeros_like(qcat))
                dom = jnp.where(own, docat, 0.0).astype(BF16)
                p, p_s = _swa_probs(qm, kcat, bias_ref[g], sink_cols[g], n0 == 0)
                dp = _bdot(dom, vcat, 2, 2)
                dsum = jnp.sum(p * dp, axis=-1, keepdims=True)
                ds = p * (dp - dsum)
                dbias_ref[g] += jnp.sum(ds, axis=0)
                dsink_ref[g] += jnp.broadcast_to(-jnp.sum(p_s * dsum, axis=0), (SWA_ROWS, 128))
                dsb = (ds * SWA_SCALE).astype(BF16)
                dqs.append(_bdot(dsb, kcat, 2, 1))
                dkt = dkt + _bdot(qm, dsb, 1, 1)
                dvt = dvt + _bdot(dom, p.astype(BF16), 1, 1)
            dq_ref[rows, :] = _swa_unstack(dqs[0], dqs[1], cb)
            for i in range(cb):
                n = n0 + i
                if n:
                    cols = slice((n - 1) * W, (n + 1) * W)
                    dkt_acc[:, cols] += dkt[i]
                    dvt_acc[:, cols] += dvt[i]
                else:
                    dkt_acc[:, :W] += dkt[i][:, W:]
                    dvt_acc[:, :W] += dvt[i][:, W:]
        dk_ref[...] = dkt_acc[...].T
        dv_ref[...] = dvt_acc[...].T

    SQ, SK = SWA_HEADS * SWA_HEAD_DIM, SWA_KV_HEADS * SWA_HEAD_DIM
    row = lambda w: pl.BlockSpec((S, w), lambda b: (b, 0))
    return pl.pallas_call(
        body, name="swa_bwd", grid=(B,),
        in_specs=[row(SQ), row(SK), row(SK), row(SQ), _const_spec((SWA_KV_HEADS, SWA_ROWS, 2 * W)), _const_spec((SWA_HEADS, 128))],
        out_specs=[row(SQ), row(SK), row(SK), _acc_spec((SWA_KV_HEADS, SWA_ROWS, 2 * W)), _acc_spec((SWA_KV_HEADS, SWA_ROWS, 128))],
        out_shape=[jax.ShapeDtypeStruct((T, SQ), F32), jax.ShapeDtypeStruct((T, SK), F32), jax.ShapeDtypeStruct((T, SK), F32),
                   jax.ShapeDtypeStruct((SWA_KV_HEADS, SWA_ROWS, 2 * W), F32), jax.ShapeDtypeStruct((SWA_KV_HEADS, SWA_ROWS, 128), F32)],
        scratch_shapes=[pltpu.VMEM((SK, S), F32), pltpu.VMEM((SK, S), F32)],
        compiler_params=_cparams(("arbitrary",)),
    )(q, k, v, do, bias, sinks)


def _bias_expand(rel_bias, bucket):
    W = WINDOW

    def body(rb_ref, bucket_ref, out_ref):
        bk = bucket_ref[...]
        for h in range(SWA_HEADS):
            def add(b, acc):
                return jnp.where(bk == b, rb_ref[b, h], acc)

            out_ref[h] = lax.fori_loop(0, REL_BUCKETS, add, jnp.zeros((W, 2 * W), F32))

    return pl.pallas_call(
        body, name="bias_expand",
        in_specs=[pl.BlockSpec(memory_space=pltpu.SMEM), pl.BlockSpec(memory_space=pltpu.VMEM)],
        out_specs=pl.BlockSpec(memory_space=pltpu.VMEM),
        out_shape=jax.ShapeDtypeStruct((SWA_HEADS, W, 2 * W), F32),
        compiler_params=_cparams(),
    )(rel_bias, bucket)


def _bias_reduce(dbias, dsink_rows, bucket):
    def body(db_ref, ds_ref, bucket_ref, out_ref, sink_out_ref, rows_ref):
        sink_lane = lax.broadcasted_iota(jnp.int32, (1, 128), 1)
        sink_acc = jnp.zeros((1, 128), F32)
        for h in range(SWA_HEADS):
            sink_acc = jnp.where(sink_lane == h, jnp.sum(ds_ref[h], axis=0, keepdims=True), sink_acc)
        sink_out_ref[...] = sink_acc
        bk = bucket_ref[...]
        for h in range(SWA_HEADS):
            dbh = db_ref[h]

            def add(b, carry):
                rows_ref[pl.ds(h * REL_BUCKETS + b, 1), :] = jnp.sum(jnp.where(bk == b, dbh, 0.0), axis=0, keepdims=True)
                return carry

            lax.fori_loop(0, REL_BUCKETS, add, 0)
        totals = jnp.sum(rows_ref[...], axis=-1, keepdims=True)
        lane = lax.broadcasted_iota(jnp.int32, (REL_BUCKETS, 128), 1)
        acc = jnp.zeros((REL_BUCKETS, 128), F32)
        for h in range(SWA_HEADS):
            acc = jnp.where(lane == h, totals[h * REL_BUCKETS:(h + 1) * REL_BUCKETS], acc)
        out_ref[...] = acc

    return pl.pallas_call(
        body, name="bias_reduce",
        out_shape=[jax.ShapeDtypeStruct((REL_BUCKETS, 128), F32), jax.ShapeDtypeStruct((1, 128), F32)],
        scratch_shapes=[pltpu.VMEM((SWA_HEADS * REL_BUCKETS, 2 * WINDOW), F32)],
        compiler_params=_cparams(),
    )(dbias, dsink_rows, bucket)


def _t5_bucket(dist):
    n = jnp.maximum(dist, 0)
    max_exact = REL_BUCKETS // 2
    nf = jnp.maximum(n, 1).astype(F32)
    large = max_exact + (jnp.log(nf / max_exact) / math.log(REL_MAX_DIST / max_exact) * (REL_BUCKETS - max_exact)).astype(jnp.int32)
    large = jnp.minimum(large, REL_BUCKETS - 1)
    return jnp.where(n < max_exact, n, large)


def _pair_heads(a, axis, inverse=False):
    shp = a.shape
    a = a.reshape(shp[:axis] + ((4, 2, 64) if inverse else (2, 4, 64)) + shp[axis + 1:])
    a = jnp.swapaxes(a, axis, axis + 1)
    return a.reshape(shp)


def _owner_blocks(w32, w16):
    r, c = w32.shape[0] // N_DEV, w32.shape[1]
    return w32.reshape(N_DEV, r, c), w16.reshape(N_DEV, r, c)


_IN_WIDTHS = (Q_LORA, KV_LORA, QK_ROPE, SWA_HEADS * SWA_HEAD_DIM, SWA_KV_HEADS * SWA_HEAD_DIM, SWA_KV_HEADS * SWA_HEAD_DIM)


def _win_to_kernel(wt):
    parts, o = [], 0
    for wd in _IN_WIDTHS:
        parts.append(wt[o:o + wd])
        o += wd
    cq, ckv, kpe, qs, ks, vs = parts
    return jnp.concatenate([cq, ckv, _pair_heads(qs, 0), ks, vs, jnp.pad(kpe, ((64, 32), (0, 0)))], axis=0)


def _win_from_kernel(dwt):
    o = [0, Q_LORA, Q_LORA + KV_LORA, Q_LORA + KV_LORA + 512, Q_LORA + KV_LORA + 640, Q_LORA + KV_LORA + 768]
    cq, ckv, qs, ks, vs, kpe = [dwt[a:b] for a, b in zip(o, o[1:] + [dwt.shape[0]])]
    return jnp.concatenate([cq, ckv, kpe[64:96], _pair_heads(qs, 0, inverse=True), ks, vs], axis=0)


def _wq_to_kernel(wt):
    return jnp.pad(wt.reshape(MLA_HEADS, QK_NOPE + QK_ROPE, Q_LORA), ((0, 0), (0, 32), (0, 0))).reshape(MLA_HEADS * HEAD_BLOCK, Q_LORA)


def _wq_from_kernel(dwt):
    return dwt.reshape(MLA_HEADS, HEAD_BLOCK, Q_LORA)[:, :QK_NOPE + QK_ROPE].reshape(-1, Q_LORA)


def _wkv_to_kernel(wt):
    w3 = wt.reshape(MLA_HEADS, QK_NOPE + V_HEAD, KV_LORA)
    kpart = jnp.pad(w3[:, :QK_NOPE], ((0, 0), (0, 64), (0, 0))).reshape(MLA_HEADS * HEAD_BLOCK, KV_LORA)
    return jnp.concatenate([kpart, w3[:, QK_NOPE:].reshape(MLA_HEADS * V_HEAD, KV_LORA)], axis=0)


def _wkv_from_kernel(dwt):
    dk = dwt[:MLA_HEADS * HEAD_BLOCK].reshape(MLA_HEADS, HEAD_BLOCK, KV_LORA)[:, :QK_NOPE]
    dv = dwt[MLA_HEADS * HEAD_BLOCK:].reshape(MLA_HEADS, V_HEAD, KV_LORA)
    return jnp.concatenate([dk, dv], axis=1).reshape(-1, KV_LORA)


def _wo_to_kernel(w):
    return jnp.concatenate([w[:512], _pair_heads(w[512:], 0)], axis=0)


def _wo_from_kernel(dw):
    return jnp.concatenate([dw[:512], _pair_heads(dw[512:], 0, inverse=True)], axis=0)


def _kernel_weights(key, stored):
    if key in ("g1", "u1", "g2", "u2"):
        return {key + "t": stored}
    if key in ("d1", "d2"):
        return {key: stored}
    if key == "o":
        return {key: _wo_to_kernel(stored)}
    return {key + "t": _TO_KERNEL[key](stored)}


_SMALL = ("g_ffn1", "g_mix", "g_q_a", "g_kv_a", "attn_sinks", "rel_bias", "g_out_mla", "g_out_swa", "g_ffn2", "g_final")


class _Comm:
    def __init__(self, shards):
        self.shards = shards
        self.own32, self.recv16, self.pending, self.token = {}, {}, [], None

    def gather(self, keys):
        return [("gather", self.shards[k]) for k in keys]

    def gathered(self, keys, results, wts):
        for k, g in zip(keys, results):
            wts.update(_kernel_weights(k, g.reshape(N_DEV * g.shape[1], g.shape[2])))

    def scatter(self, keys, grads):
        jobs = []
        for k in keys:
            g32, g16 = (_FROM_KERNEL[k](g) if k in _FROM_KERNEL else g for g in grads[k])
            self.own32[k], blocks = _owner_blocks(g32, g16)
            jobs.append(("scatter", blocks))
        return jobs

    def scattered(self, keys, results):
        self.recv16.update(zip(keys, results))

    def scatter_start(self, keys, grads, name):
        handle, self.token = _scatter_start([blocks for _, blocks in self.scatter(keys, grads)], name + "_start")
        self.pending.append((keys, handle, name + "_wait"))

    def scatter_wait(self, after):
        keys, handle, name = self.pending.pop(0)
        self.scattered(keys, _scatter_wait(handle, after, name))
        return keys


def _step(x, target, gains, wts, comm):
    def jobs_in(keys):
        return comm.gather(keys) if comm else None

    def jobs_out(keys):
        return comm.scatter(keys, grads) if comm else None

    B, S, D = x.shape
    T = B * S
    x2 = x.reshape(T, D)
    t2 = target.reshape(T, D)

    pos = jnp.arange(S, dtype=F32)
    inv_freq = ROPE_THETA ** (-jnp.arange(0, QK_ROPE, 2, dtype=F32) / QK_ROPE)
    ang = pos[:, None] * inv_freq[None, :]
    cos, sin = jnp.cos(ang), jnp.sin(ang)
    ones, zeros = jnp.ones((S, 64), F32), jnp.zeros((S, 32), F32)
    rope_c = jnp.concatenate([ones, cos, cos, zeros], axis=1)
    rope_s = jnp.concatenate([0.0 * ones, -sin, sin, zeros], axis=1)

    qi = jnp.arange(WINDOW)[:, None]
    kj = jnp.arange(2 * WINDOW)[None, :]
    bucket = _t5_bucket(qi + WINDOW - kj).astype(jnp.int32)
    bias = _bias_expand(gains["rel_bias"], bucket).reshape(SWA_KV_HEADS, SWA_ROWS, 2 * WINDOW)
    sinks = jnp.broadcast_to(gains["attn_sinks"].reshape(SWA_HEADS, 1), (SWA_HEADS, 128))
    g_swa = _pair_heads(gains["g_out_swa"], 1)

    keys = ("in", "q", "kv", "o", "g2")
    (h1, n1, gate1, up1, act1), got = _ffn_fwd(x2, gains["g_ffn1"], wts["g1t"], wts["u1t"], wts["d1"], "ffn1_fwd", jobs_in(keys))
    if comm:
        comm.gathered(keys, got, wts)
    (u, c_q, c_kv, cqn, ckvn, q, k, v, qs, ks, vs) = _proj_fwd(
        h1, gains["g_mix"], wts["int"], gains["g_q_a"], wts["qt"], gains["g_kv_a"], wts["kvt"], rope_c, rope_s, S)
    keys = ("u2", "d2")
    (o_mla, lse), got = _mla_fwd(q, k, v, B, S, jobs_in(keys))
    if comm:
        comm.gathered(keys, got, wts)
    o_swa = _swa_fwd(qs, ks, vs, bias, sinks, B, S)
    h2, oc = _out_fwd(o_mla, o_swa, gains["g_out_mla"], g_swa, wts["o"], h1)
    (dh3, n2, gate2, up2, act2, loss, dg_final), _ = _ffn_fwd(
        h2, gains["g_ffn2"], wts["g2t"], wts["u2t"], wts["d2"], "ffn2_fwd", loss_head=(t2, gains["g_final"].reshape(1, D)))

    grads = {}
    (dh2, dg_ffn2, dgate2, dup2), _ = _ffn_bwd(h2, gains["g_ffn2"], gate2, up2, dh3, wts["g2t"], wts["u2t"], wts["d2"], "ffn2_bwd")
    grads["g2"], _ = _matmul_tn(dgate2, n2, "dw_gate2")
    grads["u2"], _ = _matmul_tn(dup2, n2, "dw_up2")
    grads["d2"], _ = _matmul_tn(act2, dh3, "dw_down2", 0.5)
    do_mla, do_swa, dg_mla, dg_swa = _out_bwd(dh2, o_mla, o_swa, gains["g_out_mla"], g_swa, wts["o"])
    grads["o"], _ = _matmul_tn(oc, dh2, "dw_o")
    keys = ("g2", "u2", "d2")
    (dq, dk, dv), got = _mla_bwd(q, k, v, do_mla, o_mla, lse, B, S, jobs_out(keys))
    if comm:
        comm.scattered(keys, got)
    dqs, dks, dvs, dbias, dsink = _swa_bwd(qs, ks, vs, do_swa, bias, sinks, B, S)
    dh1, dproj, dql, dkvc, dg_mix, dg_q, dg_kv = _proj_bwd(
        dq, dk, dv, dqs, dks, dvs, c_q, c_kv, h1, dh2, gains["g_mix"], wts["int"], gains["g_q_a"], wts["qt"],
        gains["g_kv_a"], wts["kvt"], rope_c, rope_s, S)
    grads["in"], _ = _matmul_tn(dproj, u, "dw_in")
    grads["q"], _ = _matmul_tn(dql, cqn, "dw_q")
    grads["kv"], _ = _matmul_tn(dkvc, ckvn, "dw_kv")
    grads["d1"], _ = _matmul_tn(act1, dh1, "dw_down1", 0.5)
    if comm:
        comm.scatter_start(("o", "in", "q", "kv", "d1"), grads, "grad_scatter_mid")
    (dx, dg_ffn1, dgate1, dup1), _ = _ffn_bwd(
        x2, gains["g_ffn1"], gate1, up1, dh1, wts["g1t"], wts["u1t"], wts["d1"], "ffn1_bwd", after=comm and comm.token)
    grads["g1"], _ = _matmul_tn(dgate1, n1, "dw_gate1")
    if comm:
        comm.scatter_start(("g1",), grads, "grad_scatter_gate1")
    grads["u1"], _ = _matmul_tn(dup1, n1, "dw_up1", after=comm and comm.token)
    if comm:
        comm.scatter_start(("u1",), grads, "grad_scatter_up1")

    d_rel_bias, d_sinks = _bias_reduce(dbias.reshape(SWA_HEADS, WINDOW, 2 * WINDOW), dsink.reshape(SWA_HEADS, WINDOW, 128), bucket)
    small = {
        "g_ffn1": dg_ffn1, "g_mix": dg_mix, "g_q_a": dg_q, "g_kv_a": dg_kv, "attn_sinks": d_sinks,
        "rel_bias": d_rel_bias, "g_out_mla": dg_mla, "g_out_swa": _pair_heads(dg_swa, 1, inverse=True),
        "g_ffn2": dg_ffn2, "g_final": dg_final, "loss": loss,
    }
    return dx.reshape(B, S, D), grads, small


_WEIGHTS = ("g_ffn1", "w_ffn1_gate", "w_ffn1_up", "w_ffn1_down", "g_mix", "w_in", "g_q_a", "w_q_b", "g_kv_a", "w_kv_b",
            "attn_sinks", "rel_bias", "g_out_mla", "g_out_swa", "w_o", "g_ffn2", "w_ffn2_gate", "w_ffn2_up", "w_ffn2_down",
            "g_final")
_BIG = (("w_ffn1_gate", "g1", True), ("w_ffn1_up", "u1", True), ("w_ffn1_down", "d1", False), ("w_in", "in", True),
        ("w_q_b", "q", True), ("w_kv_b", "kv", True), ("w_o", "o", False), ("w_ffn2_gate", "g2", True),
        ("w_ffn2_up", "u2", True), ("w_ffn2_down", "d2", False))
_TO_KERNEL = {"in": _win_to_kernel, "q": _wq_to_kernel, "kv": _wkv_to_kernel}
_FROM_KERNEL = {"in": _win_from_kernel, "q": _wq_from_kernel, "kv": _wkv_from_kernel, "o": _wo_from_kernel}


def kernel(x, g_ffn1, w_ffn1_gate, w_ffn1_up, w_ffn1_down, g_mix, w_in, g_q_a, w_q_b, g_kv_a, w_kv_b, attn_sinks, rel_bias, g_out_mla, g_out_swa, w_o, g_ffn2, w_ffn2_gate, w_ffn2_up, w_ffn2_down, g_final, loss_target, m_g_ffn1, m_w_ffn1_gate, m_w_ffn1_up, m_w_ffn1_down, m_g_mix, m_w_in, m_g_q_a, m_w_q_b, m_g_kv_a, m_w_kv_b, m_attn_sinks, m_rel_bias, m_g_out_mla, m_g_out_swa, m_w_o, m_g_ffn2, m_w_ffn2_gate, m_w_ffn2_up, m_w_ffn2_down, m_g_final, v_g_ffn1, v_w_ffn1_gate, v_w_ffn1_up, v_w_ffn1_down, v_g_mix, v_w_in, v_g_q_a, v_w_q_b, v_g_kv_a, v_w_kv_b, v_attn_sinks, v_rel_bias, v_g_out_mla, v_g_out_swa, v_w_o, v_g_ffn2, v_w_ffn2_gate, v_w_ffn2_up, v_w_ffn2_down, v_g_final):
    w = dict(zip(_WEIGHTS, (g_ffn1, w_ffn1_gate, w_ffn1_up, w_ffn1_down, g_mix, w_in, g_q_a, w_q_b, g_kv_a, w_kv_b, attn_sinks,
                            rel_bias, g_out_mla, g_out_swa, w_o, g_ffn2, w_ffn2_gate, w_ffn2_up, w_ffn2_down, g_final)))
    m = dict(zip(_WEIGHTS, (m_g_ffn1, m_w_ffn1_gate, m_w_ffn1_up, m_w_ffn1_down, m_g_mix, m_w_in, m_g_q_a, m_w_q_b, m_g_kv_a,
                            m_w_kv_b, m_attn_sinks, m_rel_bias, m_g_out_mla, m_g_out_swa, m_w_o, m_g_ffn2, m_w_ffn2_gate,
                            m_w_ffn2_up, m_w_ffn2_down, m_g_final)))
    v = dict(zip(_WEIGHTS, (v_g_ffn1, v_w_ffn1_gate, v_w_ffn1_up, v_w_ffn1_down, v_g_mix, v_w_in, v_g_q_a, v_w_q_b, v_g_kv_a,
                            v_w_kv_b, v_attn_sinks, v_rel_bias, v_g_out_mla, v_g_out_swa, v_w_o, v_g_ffn2, v_w_ffn2_gate,
                            v_w_ffn2_up, v_w_ffn2_down, v_g_final)))
    me = 4 * lax.axis_index("x") + 2 * lax.axis_index("y") + lax.axis_index("c")

    stored = lambda src, name, by_col: src[name][0].T if by_col else src[name][0]
    shards = dict(zip([key for _, key, _ in _BIG], _cast_bf16([stored(w, name, by_col) for name, _, by_col in _BIG])))
    comm = _Comm(shards)
    first = ("g1", "u1", "d1")
    wts = {}
    comm.gathered(first, _all_gather([shards[k] for k in first], "weights_all_gather"), wts)

    gains = {n: (w[n] if n == "rel_bias" else w[n].reshape(1, -1)) for n in _SMALL}
    grad_x, _, small = _step(x, loss_target, gains, wts, comm)

    small_gathered = _all_gather([small[n] for n in _SMALL] + [small["loss"]], "small_all_gather")

    out_g, out_d, out_m, out_v = {}, {}, {}, {}
    me1 = me.astype(jnp.int32).reshape(1)
    big = {key: (name, by_col) for name, key, by_col in _BIG}

    def update(key):
        name, by_col = big[key]
        updates = _adam_big(stored(w, name, by_col), stored(m, name, by_col), stored(v, name, by_col),
                            comm.own32[key], comm.recv16[key], me1, comm.token)
        out_g[name], out_d[name], out_m[name], out_v[name] = ((a.T if by_col else a)[None] for a in updates)
        return updates[0]

    done = [update(key) for key in ("g2", "u2", "d2")]
    for _ in range(2):
        done = [update(key) for key in comm.scatter_wait(after=done[-1])]
    as_2d = lambda src: [src[n] if src[n].ndim == 2 else src[n].reshape(1, -1) for n in _SMALL]
    loss, updates = _adam_small(as_2d(w), as_2d(m), as_2d(v), small_gathered[:-1], small_gathered[-1])
    for n, (g, d, mn, vn) in zip(_SMALL, updates):
        out_g[n], out_d[n], out_m[n], out_v[n] = (a.reshape(w[n].shape) for a in (g, d, mn, vn))
    for key in comm.scatter_wait(after=loss):
        update(key)

    return (loss[0, 0], grad_x, *[out_g[n] for n in _WEIGHTS], *[out_d[n] for n in _WEIGHTS],
            *[out_m[n] for n in _WEIGHTS], *[out_v[n] for n in _WEIGHTS])
```

```python
import functools
import math

import jax
import jax.numpy as jnp
from jax import lax
from jax.experimental import pallas as pl
from jax.experimental.pallas import tpu as pltpu

F32 = jnp.float32
BF16 = jnp.bfloat16
MESH = pl.DeviceIdType.MESH

EPS = 1e-6
NEG = -1e30
N_DEV = 8

MLA_HEADS = 8
Q_LORA = 256
KV_LORA = 128
QK_NOPE = 64
QK_ROPE = 32
V_HEAD = 64
ROPE_THETA = 10000.0
SWA_HEADS = 8
SWA_KV_HEADS = 2
SWA_HEAD_DIM = 64
WINDOW = 128
REL_BUCKETS = 32
REL_MAX_DIST = 128
HEAD_BLOCK = 128
MLA_Q_TILES = (512, 256, 128)
MLA_SCALE = (QK_NOPE + QK_ROPE) ** -0.5
MLA_SCALE_LOG2 = MLA_SCALE * math.log2(math.e)
SWA_SCALE = SWA_HEAD_DIM ** -0.5

ADAM_LR = 0.001
ADAM_B1 = 0.9
ADAM_B2 = 0.999
ADAM_EPS = 1e-08
ADAM_WD = 0.01
ADAM_STEP = 10

VMEM_LIMIT = 56 * 1024 * 1024


def _cparams(semantics=None):
    return pltpu.CompilerParams(dimension_semantics=semantics, vmem_limit_bytes=VMEM_LIMIT)


def _dot(a, b):
    return jnp.dot(a, b, preferred_element_type=F32)


def _dot_nt(a, b):
    return lax.dot_general(a, b, (((1,), (1,)), ((), ())), preferred_element_type=F32)


def _dot_tn(a, b):
    return lax.dot_general(a, b, (((0,), (0,)), ((), ())), preferred_element_type=F32)


def _rms_fwd(x, g):
    r = lax.rsqrt(jnp.mean(x * x, axis=-1, keepdims=True) + EPS)
    return x * r * g, r


def _rms_bwd(x, g, r, dy):
    xh = x * r
    dyg = dy * g
    dx = r * (dyg - xh * jnp.mean(dyg * xh, axis=-1, keepdims=True))
    return dx, jnp.sum(dy * xh, axis=0, keepdims=True)


def _swap16(x):
    n = x.shape[-1]
    up = pltpu.roll(x, n - 16, 1)
    down = pltpu.roll(x, 16, 1)
    lane = lax.broadcasted_iota(jnp.int32, x.shape, 1) % HEAD_BLOCK
    return jnp.where((lane >= 64) & (lane < 80), up, jnp.where((lane >= 80) & (lane < 96), down, 0.0))


def _const_spec(shape):
    return pl.BlockSpec(shape, lambda *_: (0,) * len(shape), pipeline_mode=pl.Buffered(1))


def _acc_spec(shape):
    return pl.BlockSpec(shape, lambda *_: (0,) * len(shape))


def _row_spec(tm, width):
    return pl.BlockSpec((tm, width), lambda i: (i, 0))


def _cast_bf16(arrays):
    n = len(arrays)

    def body(*refs):
        for src, dst in zip(refs[:n], refs[n:]):
            dst[...] = src[...].astype(BF16)

    return pl.pallas_call(
        body, name="cast_weights",
        out_shape=[jax.ShapeDtypeStruct(a.shape, BF16) for a in arrays],
        compiler_params=_cparams(),
    )(*arrays)


def _all_gather(arrays, name):
    n = len(arrays)

    def body(*refs):
        ins, outs = refs[:n], refs[n:2 * n]
        send_sems, recv_sems, local_sems = refs[2 * n:]
        x, y, c = lax.axis_index("x"), lax.axis_index("y"), lax.axis_index("c")
        me, sibling = (x, y, c), (x, y, 1 - c)
        chips = [(1 - x, y), (x, 1 - y), (1 - x, 1 - y)]

        def slot(a, dev):
            return outs[a].at[4 * dev[0] + 2 * dev[1] + dev[2]]

        def copy(a, k, block, to, src=None):
            return pltpu.make_async_remote_copy(
                src_ref=slot(a, block) if src is None else src, dst_ref=slot(a, block),
                send_sem=send_sems.at[a, k], recv_sem=recv_sems.at[a, k], device_id=to, device_id_type=MESH)

        mine = [pltpu.make_async_copy(ins[a], slot(a, me), local_sems.at[a]) for a in range(n)]
        for cp in mine:
            cp.start()
        first = []
        for a in range(n):
            first.append(copy(a, 0, me, sibling, src=ins[a]))
            first += [copy(a, 1 + j, me, (*chip, c), src=ins[a]) for j, chip in enumerate(chips)]
        for cp in first:
            cp.start()
        passed = []
        for j, chip in enumerate(chips):
            for a in range(n):
                copy(a, 1 + j, (*chip, c), me).wait_recv()
                cp = copy(a, 4 + j, (*chip, c), sibling)
                cp.start()
                passed.append(cp)
        for a in range(n):
            copy(a, 0, sibling, me).wait_recv()
            for j, chip in enumerate(chips):
                copy(a, 4 + j, (*chip, 1 - c), me).wait_recv()
        for cp in first + passed:
            cp.wait_send()
        for cp in mine:
            cp.wait()

    any_spec = pl.BlockSpec(memory_space=pl.ANY)
    return pl.pallas_call(
        body, name=name,
        out_shape=[jax.ShapeDtypeStruct((N_DEV,) + a.shape, a.dtype) for a in arrays],
        in_specs=[any_spec] * n, out_specs=[any_spec] * n,
        scratch_shapes=[pltpu.SemaphoreType.DMA((n, 7)), pltpu.SemaphoreType.DMA((n, 7)), pltpu.SemaphoreType.DMA((n,))],
    )(*arrays)


def _exchange_shapes(jobs):
    dsts = [jax.ShapeDtypeStruct(((N_DEV,) + a.shape) if kind == "gather" else a.shape, a.dtype) for kind, a in jobs]
    n = len(jobs)
    sems = [pltpu.SemaphoreType.DMA((n, N_DEV - 1)), pltpu.SemaphoreType.DMA((n, N_DEV - 1)), pltpu.SemaphoreType.DMA((n,))]
    return dsts, sems


def _exchange_copies(kinds, srcs, dsts, send_sems, recv_sems, local_sems):
    x, y, c = lax.axis_index("x"), lax.axis_index("y"), lax.axis_index("c")
    me = 4 * x + 2 * y + c
    remote, local = [], []
    for a, kind in enumerate(kinds):
        for k in range(1, N_DEV):
            peer = (1 - x if k & 4 else x, 1 - y if k & 2 else y, 1 - c if k & 1 else c)
            src = srcs[a] if kind == "gather" else srcs[a].at[4 * peer[0] + 2 * peer[1] + peer[2]]
            remote.append(pltpu.make_async_remote_copy(
                src_ref=src, dst_ref=dsts[a].at[me], send_sem=send_sems.at[a, k - 1], recv_sem=recv_sems.at[a, k - 1],
                device_id=peer, device_id_type=MESH))
        src = srcs[a] if kind == "gather" else srcs[a].at[me]
        local.append(pltpu.make_async_copy(src, dsts[a].at[me], local_sems.at[a]))
    return remote, local


def _scatter_copies(srcs, lands, send_sems, recv_sems):
    x, y, c = lax.axis_index("x"), lax.axis_index("y"), lax.axis_index("c")
    me = 4 * x + 2 * y + c
    copies = []
    for a, (src_ref, land_ref) in enumerate(zip(srcs, lands)):
        for k in range(1, N_DEV):
            peer = (1 - x if k & 4 else x, 1 - y if k & 2 else y, 1 - c if k & 1 else c)
            copies.append(pltpu.make_async_remote_copy(
                src_ref=src_ref.at[4 * peer[0] + 2 * peer[1] + peer[2]], dst_ref=land_ref.at[me],
                send_sem=send_sems.at[a * (N_DEV - 1) + k - 1], recv_sem=recv_sems.at[a * (N_DEV - 1) + k - 1],
                device_id=peer, device_id_type=MESH))
    return copies


_SPLIT_EFFECT = pltpu.SideEffectType.DATAFLOW_SIDE_EFFECTING


def _scatter_start(blocks, name):
    n = len(blocks)
    hbm = pl.BlockSpec(memory_space=pltpu.HBM)
    sem = pl.BlockSpec(memory_space=pltpu.SEMAPHORE)

    def body(*refs):
        srcs, lands, send_sems, recv_sems, token = refs[:n], refs[n:2 * n], refs[2 * n], refs[2 * n + 1], refs[-1]
        for cp in _scatter_copies(srcs, lands, send_sems, recv_sems):
            cp.start()
        token[...] = jnp.zeros_like(token)

    through = [pltpu.HBM(b.shape, b.dtype) for b in blocks]
    *handle, token = pl.pallas_call(
        body, name=name,
        out_shape=(pltpu.SemaphoreType.DMA((n * (N_DEV - 1),)), pltpu.SemaphoreType.DMA((n * (N_DEV - 1),)), *through, *through,
                   jax.ShapeDtypeStruct((8, 128), F32)),
        in_specs=(hbm,) * (2 * n), out_specs=(sem, sem) + (hbm,) * (2 * n) + (pl.BlockSpec(memory_space=pltpu.VMEM),),
        input_output_aliases={i: 2 + i for i in range(2 * n)},
        compiler_params=pltpu.CompilerParams(has_side_effects=_SPLIT_EFFECT),
    )(*[pltpu.with_memory_space_constraint(b, pltpu.HBM) for b in blocks],
      *[pltpu.with_memory_space_constraint(lax.empty(b.shape, b.dtype), pltpu.HBM) for b in blocks])
    return handle, token


def _scatter_wait(handle, after, name):
    hbm = pl.BlockSpec(memory_space=pltpu.HBM)
    sem = pl.BlockSpec(memory_space=pltpu.SEMAPHORE)
    send_sems, recv_sems, *through = handle
    n = len(through) // 2

    def body(*refs):
        for cp in _scatter_copies(refs[:n], refs[n:2 * n], refs[2 * n], refs[2 * n + 1]):
            cp.wait_send()
            cp.wait_recv()

    outs = pl.pallas_call(
        body, name=name, out_shape=tuple(pltpu.HBM(t.shape, t.dtype) for t in through),
        in_specs=(hbm,) * (2 * n) + (sem, sem, pl.BlockSpec(memory_space=pl.ANY)), out_specs=(hbm,) * (2 * n),
        input_output_aliases={i: i for i in range(2 * n)},
        compiler_params=pltpu.CompilerParams(has_side_effects=_SPLIT_EFFECT),
    )(*through, send_sems, recv_sems, after)
    return list(outs[n:])


def _pcall(body, *, name, grid, in_specs, out_specs, out_shape, args, scratch_shapes=(), exchange=None, after=None):
    semantics = ("arbitrary",) * len(grid)
    exchange = exchange or []
    extra = [] if after is None else [after]
    if not exchange and not extra:
        outs = pl.pallas_call(body, name=name, grid=grid, in_specs=list(in_specs), out_specs=list(out_specs),
                              out_shape=list(out_shape), scratch_shapes=list(scratch_shapes),
                              compiler_params=_cparams(semantics))(*args)
        return list(outs), []
    kinds = [kind for kind, _ in exchange]
    n_in, n_out, n_job = len(in_specs), len(out_specs), len(exchange)
    dst_shapes, sems = _exchange_shapes(exchange) if exchange else ([], [])

    def wrapped(*refs):
        ins, srcs = refs[:n_in], refs[n_in:n_in + n_job]
        o0 = n_in + n_job + len(extra)
        outs, dsts, rest = refs[o0:o0 + n_out], refs[o0 + n_out:o0 + n_out + n_job], refs[o0 + n_out + n_job:]
        if not exchange:
            body(*ins, *outs, *rest)
            return
        scratch, sem_refs = rest[:-3], rest[-3:]
        first = functools.reduce(jnp.logical_and, [pl.program_id(d) == 0 for d in range(len(grid))])
        last = functools.reduce(jnp.logical_and, [pl.program_id(d) == grid[d] - 1 for d in range(len(grid))])

        @pl.when(first)
        def _():
            remote, local = _exchange_copies(kinds, srcs, dsts, *sem_refs)
            for cp in local + remote:
                cp.start()

        body(*ins, *outs, *scratch)

        @pl.when(last)
        def _():
            remote, local = _exchange_copies(kinds, srcs, dsts, *sem_refs)
            for cp in remote + local:
                cp.wait()

    any_spec = pl.BlockSpec(memory_space=pl.ANY)
    outs = pl.pallas_call(
        wrapped, name=name, grid=grid, in_specs=list(in_specs) + [any_spec] * (n_job + len(extra)),
        out_specs=list(out_specs) + [any_spec] * n_job, out_shape=list(out_shape) + dst_shapes,
        scratch_shapes=list(scratch_shapes) + sems, compiler_params=_cparams(semantics),
    )(*args, *[a for _, a in exchange], *extra)
    return list(outs[:n_out]), list(outs[n_out:])


def _adam_math(w, g, m, v):
    m = ADAM_B1 * m + (1.0 - ADAM_B1) * g
    v = ADAM_B2 * v + (1.0 - ADAM_B2) * (g * g)
    m_hat = m / (1.0 - ADAM_B1 ** ADAM_STEP)
    v_hat = v / (1.0 - ADAM_B2 ** ADAM_STEP)
    delta = -ADAM_LR * (m_hat / (jnp.sqrt(v_hat) + ADAM_EPS) + ADAM_WD * w)
    return delta, m, v


def _adam_big(w, m, v, own32, recv16, me, after):
    r, c = w.shape
    tr = _pick(r, (176,))
    assert r % tr == 0

    def body(me_ref, w_ref, m_ref, v_ref, s_ref, r_ref, after_ref, g_out, d_out, m_out, v_out):
        g = s_ref[...]
        for k in range(1, N_DEV):
            g = g + r_ref[jnp.bitwise_xor(me_ref[0], k)].astype(F32)
        d, mn, vn = _adam_math(w_ref[...], g, m_ref[...], v_ref[...])
        g_out[...] = g
        d_out[...] = d
        m_out[...] = mn
        v_out[...] = vn

    blk = pl.BlockSpec((tr, c), lambda i, s: (i, 0))
    grid_spec = pltpu.PrefetchScalarGridSpec(
        num_scalar_prefetch=1, grid=(r // tr,),
        in_specs=[blk, blk, blk, pl.BlockSpec((None, tr, c), lambda i, s: (s[0], i, 0)),
                  pl.BlockSpec((N_DEV, tr, c), lambda i, s: (0, i, 0)), pl.BlockSpec(memory_space=pl.ANY)],
        out_specs=[blk, blk, blk, blk])
    return pl.pallas_call(
        body, name="adamw_big", grid_spec=grid_spec,
        out_shape=[jax.ShapeDtypeStruct((r, c), F32)] * 4,
        compiler_params=_cparams(("arbitrary",)),
    )(me, w, m, v, own32, recv16, after)


def _adam_small(ws, ms, vs, partials, loss_parts):
    n = len(ws)

    def body(*refs):
        w_refs, m_refs, v_refs, p_refs = (refs[i * n:(i + 1) * n] for i in range(4))
        l_ref, loss_out = refs[4 * n], refs[4 * n + 1]
        outs = refs[4 * n + 2:]

        def total(ref):
            s = ref[0]
            for dev in range(1, N_DEV):
                s = s + ref[dev]
            return s

        loss_out[...] = total(l_ref)
        for i in range(n):
            r, c = w_refs[i].shape
            g = total(p_refs[i])[:, :c]
            d, mn, vn = _adam_math(w_refs[i][...], g, m_refs[i][...], v_refs[i][...])
            for j, val in enumerate((g, d, mn, vn)):
                outs[4 * i + j][...] = val

    shapes = [jax.ShapeDtypeStruct((1, 128), F32)]
    for w in ws:
        shapes += [jax.ShapeDtypeStruct(w.shape, F32)] * 4
    outs = pl.pallas_call(body, name="adamw_small", out_shape=shapes, compiler_params=_cparams())(*ws, *ms, *vs, *partials, loss_parts)
    return outs[0], [outs[1 + 4 * i:5 + 4 * i] for i in range(n)]


def _pick(n, prefer):
    for t in prefer:
        if n % t == 0:
            return t
    return n


def _ffn_fwd(x, g, wg, wu, wd, name, exchange=None, loss_head=None):
    T, D = x.shape
    F = wg.shape[0]
    tm = _pick(T, (512, 256, 128))
    fc = _pick(F, (256, 128))

    def body(x_ref, g_ref, wg_ref, wu_ref, wd_ref, *rest):
        h_ref, n_ref, gate_ref, up_ref, act_ref = rest[-7:-2] if loss_head else rest
        xv = x_ref[...]
        n, _ = _rms_fwd(xv, g_ref[...])
        nb = n.astype(BF16)
        n_ref[...] = nb
        acc = jnp.zeros((tm, D), F32)
        for f0 in range(0, F, fc):
            gate = _dot_nt(nb, wg_ref[f0:f0 + fc, :])
            up = _dot_nt(nb, wu_ref[f0:f0 + fc, :])
            gate_ref[:, f0:f0 + fc] = gate.astype(BF16)
            up_ref[:, f0:f0 + fc] = up.astype(BF16)
            act = (gate * (1.0 / (1.0 + jnp.exp(-gate))) * up).astype(BF16)
            act_ref[:, f0:f0 + fc] = act
            acc = acc + _dot(act, wd_ref[f0:f0 + fc, :])
        hv = xv + 0.5 * acc
        if not loss_head:
            h_ref[...] = hv
            return
        t_ref, gf_ref, loss_ref, dgf_ref = rest[0], rest[1], rest[-2], rest[-1]
        gf = gf_ref[...]
        y, r = _rms_fwd(hv, gf)
        diff = y - t_ref[...]
        dh, dgp = _rms_bwd(hv, gf, r, diff * (1.0 / D))
        h_ref[...] = dh
        part = 0.5 * jnp.sum(jnp.sum(diff * diff, axis=-1, keepdims=True) * (1.0 / D), axis=0, keepdims=True)

        @pl.when(pl.program_id(0) == 0)
        def _():
            loss_ref[...] = jnp.zeros_like(loss_ref)
            dgf_ref[...] = jnp.zeros_like(dgf_ref)

        loss_ref[...] += jnp.broadcast_to(part, loss_ref.shape)
        dgf_ref[...] += dgp

    in_specs = [_row_spec(tm, D), _const_spec((1, D)), _const_spec((F, D)), _const_spec((F, D)), _const_spec((F, D))]
    out_specs = [_row_spec(tm, D), _row_spec(tm, D), _row_spec(tm, F), _row_spec(tm, F), _row_spec(tm, F)]
    out_shape = [jax.ShapeDtypeStruct((T, D), F32), jax.ShapeDtypeStruct((T, D), BF16),
                 jax.ShapeDtypeStruct((T, F), BF16), jax.ShapeDtypeStruct((T, F), BF16), jax.ShapeDtypeStruct((T, F), BF16)]
    args = (x, g, wg, wu, wd)
    if loss_head:
        in_specs += [_row_spec(tm, D), _const_spec((1, D))]
        out_specs += [_acc_spec((1, 128)), _acc_spec((1, D))]
        out_shape += [jax.ShapeDtypeStruct((1, 128), F32), jax.ShapeDtypeStruct((1, D), F32)]
        args += tuple(loss_head)
    return _pcall(body, name=name, grid=(T // tm,), in_specs=in_specs, out_specs=out_specs, out_shape=out_shape, args=args,
                  exchange=exchange)


FFN_BWD_CHUNK = 1408
FFN_BWD_GROUP = 1408


def _ffn_bwd(x, g, gate, up, dh, wgt, wut, wdt, name, exchange=None, after=None):
    T, D = x.shape
    F = wgt.shape[0]
    tm = _pick(T, (512, 256, 128))
    halves = 2
    Fh = F // halves
    assert Fh * halves == F and Fh % 128 == 0
    groups = [(g0, min(g0 + FFN_BWD_GROUP, Fh)) for g0 in range(0, Fh, FFN_BWD_GROUP)]

    def body(x_ref, g_ref, gate_ref, up_ref, dh_ref, wg_ref, wu_ref, wd_ref, dx_ref, dg_ref, dgate_ref, dup_ref, dn_ref):
        i, h = pl.program_id(0), pl.program_id(1)
        dhv = dh_ref[...]
        dho = (0.5 * dhv).astype(BF16)
        dn = jnp.zeros((tm, D), F32)
        for g0, g1 in groups:
            for f0 in range(g0, g1, FFN_BWD_CHUNK):
                f1 = min(f0 + FFN_BWD_CHUNK, g1)
                d_act = _dot_nt(dho, wd_ref[f0:f1, :])
                gt = gate_ref[:, f0:f1].astype(F32)
                u = up_ref[:, f0:f1].astype(F32)
                sg = 1.0 / (1.0 + jnp.exp(-gt))
                silu = gt * sg
                dup_ref[:, f0:f1] = (d_act * silu).astype(BF16)
                dgate_ref[:, f0:f1] = (d_act * u * (sg + silu * (1.0 - sg))).astype(BF16)
            dn = dn + _dot(dgate_ref[:, g0:g1], wg_ref[g0:g1, :]) + _dot(dup_ref[:, g0:g1], wu_ref[g0:g1, :])

        @pl.when(h == 0)
        def _():
            dn_ref[...] = dn

        @pl.when(h > 0)
        def _():
            dn_ref[...] += dn

        @pl.when((i == 0) & (h == 0))
        def _():
            dg_ref[...] = jnp.zeros_like(dg_ref)

        @pl.when(h == halves - 1)
        def _():
            xv, gv = x_ref[...], g_ref[...]
            _, r = _rms_fwd(xv, gv)
            dxn, dgp = _rms_bwd(xv, gv, r, dn_ref[...])
            dx_ref[...] = dhv + dxn
            dg_ref[...] += dgp

    tile = pl.BlockSpec((tm, D), lambda i, h: (i, 0))
    wide = pl.BlockSpec((tm, Fh), lambda i, h: (i, h))
    return _pcall(
        body, name=name, grid=(T // tm, halves),
        in_specs=[tile, _const_spec((1, D)), wide, wide, tile,
                  pl.BlockSpec((Fh, D), lambda i, h: (h, 0)), pl.BlockSpec((Fh, D), lambda i, h: (h, 0)),
                  pl.BlockSpec((Fh, D), lambda i, h: (h, 0))],
        out_specs=[tile, _acc_spec((1, D)), wide, wide],
        out_shape=[jax.ShapeDtypeStruct((T, D), F32), jax.ShapeDtypeStruct((1, D), F32),
                   jax.ShapeDtypeStruct((T, F), BF16), jax.ShapeDtypeStruct((T, F), BF16)],
        scratch_shapes=[pltpu.VMEM((tm, D), F32)],
        args=(x, g, gate, up, dh, wgt, wut, wdt), exchange=exchange, after=after)


def _matmul_tn(a, b, name, scale=None, exchange=None, after=None):
    T, K = a.shape
    N = b.shape[1]
    tk = _pick(K, (1024, 1408, 1280, 768, 512, 256, 128))
    tn = _pick(N, (1024, 1408, 1280, 768, 512, 256, 128))
    tt = _pick(T, (2048, 1024, 512, 256, 128))
    n_t = T // tt

    def body(a_ref, b_ref, o_ref, o16_ref):
        @pl.when(pl.program_id(2) == 0)
        def _():
            o_ref[...] = jnp.zeros_like(o_ref)

        o_ref[...] += _dot_tn(a_ref[...].astype(BF16), b_ref[...].astype(BF16))

        @pl.when(pl.program_id(2) == n_t - 1)
        def _():
            if scale is not None:
                o_ref[...] = o_ref[...] * scale
            o16_ref[...] = o_ref[...].astype(BF16)

    tile = pl.BlockSpec((tk, tn), lambda i, j, t: (i, j))
    outs, got = _pcall(
        body, name=name, grid=(K // tk, N // tn, n_t),
        in_specs=[pl.BlockSpec((tt, tk), lambda i, j, t: (t, i)), pl.BlockSpec((tt, tn), lambda i, j, t: (t, j))],
        out_specs=[tile, tile],
        out_shape=[jax.ShapeDtypeStruct((K, N), F32), jax.ShapeDtypeStruct((K, N), BF16)],
        args=(a, b), exchange=exchange, after=after)
    return tuple(outs), got


def _rope_spec(tm, S):
    per = S // tm
    return pl.BlockSpec((tm, HEAD_BLOCK), lambda i: (i % per, 0))


def _proj_fwd(h1, g_mix, w_in, g_q, wq, g_kv, wkv, rope_c, rope_s, S):
    T, D = h1.shape
    P = w_in.shape[0]
    tm = _pick(S, (512, 256, 128))
    QW = MLA_HEADS * HEAD_BLOCK
    VW = MLA_HEADS * V_HEAD
    SQ = SWA_HEADS * SWA_HEAD_DIM
    SK = SWA_KV_HEADS * SWA_HEAD_DIM
    o_cq, o_ckv, o_qs, o_ks, o_vs, o_kpe = 0, Q_LORA, Q_LORA + KV_LORA, Q_LORA + KV_LORA + SQ, Q_LORA + KV_LORA + SQ + SK, Q_LORA + KV_LORA + SQ + 2 * SK
    assert P == o_kpe + HEAD_BLOCK

    def body(h_ref, gm_ref, win_ref, gq_ref, wq_ref, gkv_ref, wkv_ref, c_ref, s_ref,
             u_ref, cq_ref, ckv_ref, cqn_ref, ckvn_ref, q_ref, k_ref, v_ref, qs_ref, ks_ref, vs_ref):
        u, _ = _rms_fwd(h_ref[...], gm_ref[...])
        ub = u.astype(BF16)
        u_ref[...] = ub
        proj = _dot_nt(ub, win_ref[...])
        c_q = proj[:, o_cq:o_ckv]
        c_kv = proj[:, o_ckv:o_qs]
        cq_ref[...] = c_q
        ckv_ref[...] = c_kv
        qs_ref[...] = proj[:, o_qs:o_ks].astype(BF16)
        ks_ref[...] = proj[:, o_ks:o_vs].astype(BF16)
        vs_ref[...] = proj[:, o_vs:o_kpe].astype(BF16)
        cb, sb = c_ref[...], s_ref[...]
        kpe = proj[:, o_kpe:P]
        kpe = kpe * cb + _swap16(kpe) * sb
        cqn, _ = _rms_fwd(c_q, gq_ref[...])
        cqn = cqn.astype(BF16)
        cqn_ref[...] = cqn
        q = _dot_nt(cqn, wq_ref[...])
        q = q * jnp.tile(cb, (1, MLA_HEADS)) + _swap16(q) * jnp.tile(sb, (1, MLA_HEADS))
        q_ref[...] = q.astype(BF16)
        ckvn, _ = _rms_fwd(c_kv, gkv_ref[...])
        ckvn = ckvn.astype(BF16)
        ckvn_ref[...] = ckvn
        kv = _dot_nt(ckvn, wkv_ref[...])
        k_ref[...] = (kv[:, :QW] + jnp.tile(kpe, (1, MLA_HEADS))).astype(BF16)
        v_ref[...] = kv[:, QW:].astype(BF16)

    widths = [(D, BF16), (Q_LORA, F32), (KV_LORA, F32), (Q_LORA, BF16), (KV_LORA, BF16), (QW, BF16), (QW, BF16), (VW, BF16),
              (SQ, BF16), (SK, BF16), (SK, BF16)]
    return pl.pallas_call(
        body, name="proj_fwd", grid=(T // tm,),
        in_specs=[_row_spec(tm, D), _const_spec((1, D)), _const_spec((P, D)), _const_spec((1, Q_LORA)),
                  _const_spec((QW, Q_LORA)), _const_spec((1, KV_LORA)), _const_spec((QW + VW, KV_LORA)),
                  _rope_spec(tm, S), _rope_spec(tm, S)],
        out_specs=[_row_spec(tm, w) for w, _ in widths],
        out_shape=[jax.ShapeDtypeStruct((T, w), dt) for w, dt in widths],
        compiler_params=_cparams(("arbitrary",)),
    )(h1, g_mix, w_in, g_q, wq, g_kv, wkv, rope_c, rope_s)


def _proj_bwd(dq, dk, dv, dqs, dks, dvs, c_q, c_kv, h1, dh2, g_mix, w_in, g_q, wq, g_kv, wkv, rope_c, rope_s, S):
    T, D = h1.shape
    P = w_in.shape[0]
    tm = _pick(S, (512, 256, 128))
    QW = MLA_HEADS * HEAD_BLOCK
    VW = MLA_HEADS * V_HEAD
    SQ = SWA_HEADS * SWA_HEAD_DIM
    SK = SWA_KV_HEADS * SWA_HEAD_DIM

    def body(dq_ref, dk_ref, dv_ref, dqs_ref, dks_ref, dvs_ref, cq_ref, ckv_ref, h_ref, dh2_ref, gm_ref, win_ref, gq_ref,
             wq_ref, gkv_ref, wkv_ref, c_ref, s_ref, dh1_ref, dproj_ref, dql_ref, dkv_ref, dgm_ref, dgq_ref, dgkv_ref):
        cb, sb = c_ref[...], s_ref[...]
        dqv = dq_ref[...]
        dql = dqv * jnp.tile(cb, (1, MLA_HEADS)) + _swap16(dqv * jnp.tile(sb, (1, MLA_HEADS)))
        dql = dql.astype(BF16)
        dql_ref[...] = dql
        c_q = cq_ref[...]
        _, rq = _rms_fwd(c_q, gq_ref[...])
        d_cq, dgq = _rms_bwd(c_q, gq_ref[...], rq, _dot(dql, wq_ref[...]))

        dkv_all = dk_ref[...]
        dkpe = dkv_all[:, 0:HEAD_BLOCK]
        for h in range(1, MLA_HEADS):
            dkpe = dkpe + dkv_all[:, h * HEAD_BLOCK:(h + 1) * HEAD_BLOCK]
        lane = lax.broadcasted_iota(jnp.int32, dkpe.shape, 1)
        dkpe = jnp.where((lane >= 64) & (lane < 96), dkpe, 0.0)
        dkpe = dkpe * cb + _swap16(dkpe * sb)
        dkvc = jnp.concatenate([dkv_all.astype(BF16), dv_ref[...].astype(BF16)], axis=1)
        dkv_ref[...] = dkvc
        c_kv = ckv_ref[...]
        _, rkv = _rms_fwd(c_kv, gkv_ref[...])
        d_ckv, dgkv = _rms_bwd(c_kv, gkv_ref[...], rkv, _dot(dkvc, wkv_ref[...]))

        dproj = jnp.concatenate([d_cq.astype(BF16), d_ckv.astype(BF16), dqs_ref[...].astype(BF16),
                                 dks_ref[...].astype(BF16), dvs_ref[...].astype(BF16), dkpe.astype(BF16)], axis=1)
        dproj_ref[...] = dproj
        hv = h_ref[...]
        _, rm = _rms_fwd(hv, gm_ref[...])
        dxn, dgm = _rms_bwd(hv, gm_ref[...], rm, _dot(dproj, win_ref[...]))
        dh1_ref[...] = dh2_ref[...] + dxn

        @pl.when(pl.program_id(0) == 0)
        def _():
            dgm_ref[...] = jnp.zeros_like(dgm_ref)
            dgq_ref[...] = jnp.zeros_like(dgq_ref)
            dgkv_ref[...] = jnp.zeros_like(dgkv_ref)

        dgm_ref[...] += dgm
        dgq_ref[...] += dgq
        dgkv_ref[...] += dgkv

    return pl.pallas_call(
        body, name="proj_bwd", grid=(T // tm,),
        in_specs=[_row_spec(tm, QW), _row_spec(tm, QW), _row_spec(tm, VW), _row_spec(tm, SQ), _row_spec(tm, SK), _row_spec(tm, SK),
                  _row_spec(tm, Q_LORA), _row_spec(tm, KV_LORA), _row_spec(tm, D), _row_spec(tm, D),
                  _const_spec((1, D)), _const_spec((P, D)), _const_spec((1, Q_LORA)), _const_spec((QW, Q_LORA)),
                  _const_spec((1, KV_LORA)), _const_spec((QW + VW, KV_LORA)), _rope_spec(tm, S), _rope_spec(tm, S)],
        out_specs=[_row_spec(tm, D), _row_spec(tm, P), _row_spec(tm, QW), _row_spec(tm, QW + VW),
                   _acc_spec((1, D)), _acc_spec((1, Q_LORA)), _acc_spec((1, KV_LORA))],
        out_shape=[jax.ShapeDtypeStruct((T, D), F32), jax.ShapeDtypeStruct((T, P), BF16), jax.ShapeDtypeStruct((T, QW), BF16),
                   jax.ShapeDtypeStruct((T, QW + VW), BF16), jax.ShapeDtypeStruct((1, D), F32),
                   jax.ShapeDtypeStruct((1, Q_LORA), F32), jax.ShapeDtypeStruct((1, KV_LORA), F32)],
        compiler_params=_cparams(("arbitrary",)),
    )(dq, dk, dv, dqs, dks, dvs, c_q, c_kv, h1, dh2, g_mix, w_in, g_q, wq, g_kv, wkv, rope_c, rope_s)


def _out_fwd(o_mla, o_swa, g_mla, g_swa, w_o, h1):
    T, D = h1.shape
    W = o_mla.shape[1]
    tm = _pick(T, (1024, 512, 256, 128))

    def body(om_ref, os_ref, gm_ref, gs_ref, wo_ref, h_ref, h2_ref, oc_ref):
        a, _ = _rms_fwd(om_ref[...], gm_ref[...])
        b, _ = _rms_fwd(os_ref[...], gs_ref[...])
        oc = jnp.concatenate([a.astype(BF16), b.astype(BF16)], axis=1)
        oc_ref[...] = oc
        h2_ref[...] = h_ref[...] + _dot(oc, wo_ref[...])

    return pl.pallas_call(
        body, name="out_fwd", grid=(T // tm,),
        in_specs=[_row_spec(tm, W), _row_spec(tm, W), _const_spec((1, W)), _const_spec((1, W)), _const_spec((2 * W, D)),
                  _row_spec(tm, D)],
        out_specs=[_row_spec(tm, D), _row_spec(tm, 2 * W)],
        out_shape=[jax.ShapeDtypeStruct((T, D), F32), jax.ShapeDtypeStruct((T, 2 * W), BF16)],
        compiler_params=_cparams(("arbitrary",)),
    )(o_mla, o_swa, g_mla, g_swa, w_o, h1)


def _out_bwd(dh2, o_mla, o_swa, g_mla, g_swa, w_o):
    T, D = dh2.shape
    W = o_mla.shape[1]
    tm = _pick(T, (1024, 512, 256, 128))

    def body(dh_ref, om_ref, os_ref, gm_ref, gs_ref, wo_ref, dom_ref, dos_ref, dgm_ref, dgs_ref):
        doc = _dot_nt(dh_ref[...].astype(BF16), wo_ref[...])
        om, osw = om_ref[...], os_ref[...]
        _, ra = _rms_fwd(om, gm_ref[...])
        _, rb = _rms_fwd(osw, gs_ref[...])
        da, dga = _rms_bwd(om, gm_ref[...], ra, doc[:, :W])
        db, dgb = _rms_bwd(osw, gs_ref[...], rb, doc[:, W:])
        dom_ref[...] = da
        dos_ref[...] = db

        @pl.when(pl.program_id(0) == 0)
        def _():
            dgm_ref[...] = jnp.zeros_like(dgm_ref)
            dgs_ref[...] = jnp.zeros_like(dgs_ref)

        dgm_ref[...] += dga
        dgs_ref[...] += dgb

    return pl.pallas_call(
        body, name="out_bwd", grid=(T // tm,),
        in_specs=[_row_spec(tm, D), _row_spec(tm, W), _row_spec(tm, W), _const_spec((1, W)), _const_spec((1, W)),
                  _const_spec((2 * W, D))],
        out_specs=[_row_spec(tm, W), _row_spec(tm, W), _acc_spec((1, W)), _acc_spec((1, W))],
        out_shape=[jax.ShapeDtypeStruct((T, W), F32), jax.ShapeDtypeStruct((T, W), F32),
                   jax.ShapeDtypeStruct((1, W), F32), jax.ShapeDtypeStruct((1, W), F32)],
        compiler_params=_cparams(("arbitrary",)),
    )(dh2, o_mla, o_swa, g_mla, g_swa, w_o)


def _half_mask(shape, half):
    lane = lax.broadcasted_iota(jnp.int32, shape, len(shape) - 1)
    return (lane < 64) if half == 0 else (lane >= 64)


def _mla_fwd(q, k, v, B, S, exchange=None):
    T = B * S
    tq = _pick(S, MLA_Q_TILES)
    nq = S // tq
    pairs = MLA_HEADS // 2

    def body(q_ref, k_ref, v_ref, o_ref, lse_ref):
        causal = lax.broadcasted_iota(jnp.int32, (tq, tq), 1) <= lax.broadcasted_iota(jnp.int32, (tq, tq), 0)
        low = _half_mask((tq, 2 * V_HEAD), 0)
        for qi in range(nq):
            rows = slice(qi * tq, (qi + 1) * tq)
            past = slice(0, qi * tq)
            outs, lses = [], []
            for half in range(2):
                lanes = slice(half * HEAD_BLOCK, (half + 1) * HEAD_BLOCK)
                qv = q_ref[rows, lanes]
                s_d = jnp.where(causal, _dot_nt(qv, k_ref[rows, lanes]) * MLA_SCALE_LOG2, NEG)
                m = jnp.max(s_d, axis=-1, keepdims=True)
                if qi:
                    s_p = _dot_nt(qv, k_ref[past, lanes]) * MLA_SCALE_LOG2
                    m = jnp.maximum(m, jnp.max(s_p, axis=-1, keepdims=True))
                p_d = jnp.exp2(s_d - m)
                l = jnp.sum(p_d, axis=-1, keepdims=True)
                acc = _dot(p_d.astype(BF16), v_ref[rows, :])
                if qi:
                    p_p = jnp.exp2(s_p - m)
                    l = l + jnp.sum(p_p, axis=-1, keepdims=True)
                    acc = acc + _dot(p_p.astype(BF16), v_ref[past, :])
                outs.append(acc * (1.0 / l))
                lses.append(jnp.broadcast_to(m + jnp.log2(l), (tq, 2 * V_HEAD)))
            o_ref[rows, :] = jnp.where(low, outs[0], outs[1])
            lse_ref[rows, :] = jnp.where(low, lses[0], lses[1])

    blk2 = pl.BlockSpec((S, 2 * HEAD_BLOCK), lambda b, j: (b, j))
    blk1 = pl.BlockSpec((S, 2 * V_HEAD), lambda b, j: (b, j))
    return _pcall(
        body, name="mla_fwd", grid=(B, pairs),
        in_specs=[blk2, blk2, blk1], out_specs=[blk1, blk1],
        out_shape=[jax.ShapeDtypeStruct((T, pairs * 2 * V_HEAD), F32)] * 2,
        args=(q, k, v), exchange=exchange)


def _mla_bwd(q, k, v, do, o, lse, B, S, exchange=None):
    T = B * S
    tq = _pick(S, MLA_Q_TILES)
    nq = S // tq
    pairs = MLA_HEADS // 2

    def body(q_ref, k_ref, v_ref, do_ref, o_ref, lse_ref, dq_ref, dk_ref, dv_ref, dkt_acc, dvt_acc):
        causal = lax.broadcasted_iota(jnp.int32, (tq, tq), 1) <= lax.broadcasted_iota(jnp.int32, (tq, tq), 0)
        for half in range(2):
            lanes = slice(half * HEAD_BLOCK, (half + 1) * HEAD_BLOCK)
            col = half * V_HEAD
            own = _half_mask((tq, 2 * V_HEAD), half)
            for qi in reversed(range(nq)):
                rows = slice(qi * tq, (qi + 1) * tq)
                past = slice(0, qi * tq)
                first = qi == nq - 1
                qv = q_ref[rows, lanes]
                dov = jnp.where(own, do_ref[rows, :], 0.0)
                dsum = jnp.sum(dov * o_ref[rows, :], axis=-1, keepdims=True)
                dob = dov.astype(BF16)
                lse_col = lse_ref[rows, col:col + 1]

                def grads(keys, mask):
                    kv, vv = k_ref[keys, lanes], v_ref[keys, :]
                    p = jnp.exp2(_dot_nt(qv, kv) * MLA_SCALE_LOG2 - lse_col)
                    if mask is not None:
                        p = jnp.where(mask, p, 0.0)
                    ds = (p * (_dot_nt(dob, vv) - dsum) * MLA_SCALE).astype(BF16)
                    dkt, dvt = _dot_tn(qv, ds), _dot_tn(dob, p.astype(BF16))
                    if first:
                        dkt_acc[:, keys] = dkt
                    else:
                        dkt_acc[:, keys] += dkt
                    if first and half == 0:
                        dvt_acc[:, keys] = dvt
                    else:
                        dvt_acc[:, keys] += dvt
                    return _dot(ds, kv)

                dq = grads(rows, causal)
                if qi:
                    dq = dq + grads(past, None)
                dq_ref[rows, lanes] = dq
            dk_ref[:, lanes] = dkt_acc[...].T
        dv_ref[...] = dvt_acc[...].T.astype(BF16)

    blk2 = pl.BlockSpec((S, 2 * HEAD_BLOCK), lambda b, j: (b, j))
    blk1 = pl.BlockSpec((S, 2 * V_HEAD), lambda b, j: (b, j))
    return _pcall(
        body, name="mla_bwd", grid=(B, pairs),
        in_specs=[blk2, blk2, blk1, blk1, blk1, blk1], out_specs=[blk2, blk2, blk1],
        out_shape=[jax.ShapeDtypeStruct((T, MLA_HEADS * HEAD_BLOCK), F32), jax.ShapeDtypeStruct((T, MLA_HEADS * HEAD_BLOCK), F32),
                   jax.ShapeDtypeStruct((T, pairs * 2 * V_HEAD), BF16)],
        scratch_shapes=[pltpu.VMEM((HEAD_BLOCK, S), F32), pltpu.VMEM((2 * V_HEAD, S), F32)],
        args=(q, k, v, do, o, lse), exchange=exchange)


SWA_GROUP = SWA_HEADS // SWA_KV_HEADS
SWA_ROWS = SWA_GROUP * WINDOW


SWA_CHUNK = 8


def _bdot(a, b, ca, cb):
    return lax.dot_general(a, b, (((ca,), (cb,)), ((0,), (0,))), preferred_element_type=F32)


def _swa_chunks(nb):
    cb = next(c for c in (SWA_CHUNK, 4, 2, 1) if nb % c == 0)
    return cb, [(n0, slice(n0 * WINDOW, (n0 + cb) * WINDOW)) for n0 in range(0, nb, cb)]


def _swa_keys(ref, n0, cb):
    W = WINDOW
    cur = ref[n0 * W:(n0 + cb) * W, :]
    if n0:
        prev = ref[(n0 - 1) * W:(n0 + cb - 1) * W, :]
    else:
        prev = jnp.concatenate([cur[:W], cur[:(cb - 1) * W]], axis=0) if cb > 1 else cur
    return jnp.concatenate([prev.reshape(cb, W, W), cur.reshape(cb, W, W)], axis=1)


def _swa_stack(ref, rows, cb):
    return jnp.concatenate([ref[rows, j * WINDOW:(j + 1) * WINDOW].reshape(cb, WINDOW, WINDOW) for j in range(SWA_GROUP)], axis=1)


def _swa_unstack(low, high, cb):
    sel = _half_mask((cb, WINDOW, WINDOW), 0)
    return jnp.concatenate([jnp.where(sel, low[:, j * WINDOW:(j + 1) * WINDOW], high[:, j * WINDOW:(j + 1) * WINDOW])
                            .reshape(cb * WINDOW, WINDOW) for j in range(SWA_GROUP)], axis=1)


def _swa_sink_col(sink_ref, g):
    return jnp.concatenate([jnp.broadcast_to(sink_ref[g * SWA_GROUP + j:g * SWA_GROUP + j + 1, 0:1], (WINDOW, 1))
                            for j in range(SWA_GROUP)], axis=0)


def _swa_probs(qm, kcat, bias_g, sink, first_chunk):
    shape = (qm.shape[0], SWA_ROWS, 2 * WINDOW)
    qi = lax.broadcasted_iota(jnp.int32, shape, 1) % WINDOW
    col = lax.broadcasted_iota(jnp.int32, shape, 2)
    valid = (col > qi) & (col <= qi + WINDOW)
    if first_chunk:
        valid = valid & ((col >= WINDOW) | (lax.broadcasted_iota(jnp.int32, shape, 0) > 0))
    s = jnp.where(valid, _bdot(qm, kcat, 2, 2) * SWA_SCALE + bias_g, NEG)
    m = jnp.maximum(jnp.max(s, axis=-1, keepdims=True), sink)
    e = jnp.exp(s - m)
    e_s = jnp.exp(sink - m)
    inv = 1.0 / (jnp.sum(e, axis=-1, keepdims=True) + e_s)
    return e * inv, e_s * inv


def _swa_fwd(q, k, v, bias, sinks, B, S):
    T = B * S
    W = WINDOW
    nb = S // W

    cb, chunks = _swa_chunks(nb)

    def body(q_ref, k_ref, v_ref, bias_ref, sink_ref, o_ref):
        sink_cols = [_swa_sink_col(sink_ref, g) for g in range(SWA_KV_HEADS)]
        for n0, rows in chunks:
            kcat, vcat = _swa_keys(k_ref, n0, cb), _swa_keys(v_ref, n0, cb)
            qcat = _swa_stack(q_ref, rows, cb)
            outs = []
            for g in range(SWA_KV_HEADS):
                qm = jnp.where(_half_mask(qcat.shape, g), qcat, jnp.zeros_like(qcat))
                p, _ = _swa_probs(qm, kcat, bias_ref[g], sink_cols[g], n0 == 0)
                outs.append(_bdot(p.astype(BF16), vcat, 2, 1))
            o_ref[rows, :] = _swa_unstack(outs[0], outs[1], cb)

    SQ, SK = SWA_HEADS * SWA_HEAD_DIM, SWA_KV_HEADS * SWA_HEAD_DIM
    return pl.pallas_call(
        body, name="swa_fwd", grid=(B,),
        in_specs=[pl.BlockSpec((S, SQ), lambda b: (b, 0)), pl.BlockSpec((S, SK), lambda b: (b, 0)),
                  pl.BlockSpec((S, SK), lambda b: (b, 0)), _const_spec((SWA_KV_HEADS, SWA_ROWS, 2 * W)),
                  _const_spec((SWA_HEADS, 128))],
        out_specs=pl.BlockSpec((S, SQ), lambda b: (b, 0)),
        out_shape=jax.ShapeDtypeStruct((T, SQ), F32),
        compiler_params=_cparams(("arbitrary",)),
    )(q, k, v, bias, sinks)


def _swa_bwd(q, k, v, do, bias, sinks, B, S):
    T = B * S
    W = WINDOW
    nb = S // W
    cb, chunks = _swa_chunks(nb)

    def body(q_ref, k_ref, v_ref, do_ref, bias_ref, sink_ref, dq_ref, dk_ref, dv_ref, dbias_ref, dsink_ref, dkt_acc, dvt_acc):
        @pl.when(pl.program_id(0) == 0)
        def _():
            dbias_ref[...] = jnp.zeros_like(dbias_ref)
            dsink_ref[...] = jnp.zeros_like(dsink_ref)

        dkt_acc[...] = jnp.zeros_like(dkt_acc)
        dvt_acc[...] = jnp.zeros_like(dvt_acc)
        sink_cols = [_swa_sink_col(sink_ref, g) for g in range(SWA_KV_HEADS)]
        for n0, rows in chunks:
            kcat, vcat = _swa_keys(k_ref, n0, cb), _swa_keys(v_ref, n0, cb)
            qcat = _swa_stack(q_ref, rows, cb)
            docat = _swa_stack(do_ref, rows, cb)
            dkt = jnp.zeros((cb, W, 2 * W), F32)
            dvt = jnp.zeros((cb, W, 2 * W), F32)
            dqs = []
            for g in range(SWA_KV_HEADS):
                own = _half_mask(qcat.shape, g)
                qm = jnp.where(own, qcat, jnp.zeros_like(qcat))
                dom = jnp.where(own, docat, 0.0).astype(BF16)
                p, p_s = _swa_probs(qm, kcat, bias_ref[g], sink_cols[g], n0 == 0)
                dp = _bdot(dom, vcat, 2, 2)
                dsum = jnp.sum(p * dp, axis=-1, keepdims=True)
                ds = p * (dp - dsum)
                dbias_ref[g] += jnp.sum(ds, axis=0)
                dsink_ref[g] += jnp.broadcast_to(-jnp.sum(p_s * dsum, axis=0), (SWA_ROWS, 128))
                dsb = (ds * SWA_SCALE).astype(BF16)
                dqs.append(_bdot(dsb, kcat, 2, 1))
                dkt = dkt + _bdot(qm, dsb, 1, 1)
                dvt = dvt + _bdot(dom, p.astype(BF16), 1, 1)
            dq_ref[rows, :] = _swa_unstack(dqs[0], dqs[1], cb).astype(BF16)
            for i in range(cb):
                n = n0 + i
                if n:
                    cols = slice((n - 1) * W, (n + 1) * W)
                    dkt_acc[:, cols] += dkt[i]
                    dvt_acc[:, cols] += dvt[i]
                else:
                    dkt_acc[:, :W] += dkt[i][:, W:]
                    dvt_acc[:, :W] += dvt[i][:, W:]
        dk_ref[...] = dkt_acc[...].T.astype(BF16)
        dv_ref[...] = dvt_acc[...].T.astype(BF16)

    SQ, SK = SWA_HEADS * SWA_HEAD_DIM, SWA_KV_HEADS * SWA_HEAD_DIM
    row = lambda w: pl.BlockSpec((S, w), lambda b: (b, 0))
    return pl.pallas_call(
        body, name="swa_bwd", grid=(B,),
        in_specs=[row(SQ), row(SK), row(SK), row(SQ), _const_spec((SWA_KV_HEADS, SWA_ROWS, 2 * W)), _const_spec((SWA_HEADS, 128))],
        out_specs=[row(SQ), row(SK), row(SK), _acc_spec((SWA_KV_HEADS, SWA_ROWS, 2 * W)), _acc_spec((SWA_KV_HEADS, SWA_ROWS, 128))],
        out_shape=[jax.ShapeDtypeStruct((T, SQ), BF16), jax.ShapeDtypeStruct((T, SK), BF16), jax.ShapeDtypeStruct((T, SK), BF16),
                   jax.ShapeDtypeStruct((SWA_KV_HEADS, SWA_ROWS, 2 * W), F32), jax.ShapeDtypeStruct((SWA_KV_HEADS, SWA_ROWS, 128), F32)],
        scratch_shapes=[pltpu.VMEM((SK, S), F32), pltpu.VMEM((SK, S), F32)],
        compiler_params=_cparams(("arbitrary",)),
    )(q, k, v, do, bias, sinks)


def _bias_expand(rel_bias, bucket):
    W = WINDOW

    def body(rb_ref, bucket_ref, out_ref):
        bk = bucket_ref[...]
        for h in range(SWA_HEADS):
            def add(b, acc):
                return jnp.where(bk == b, rb_ref[b, h], acc)

            out_ref[h] = lax.fori_loop(0, REL_BUCKETS, add, jnp.zeros((W, 2 * W), F32))

    return pl.pallas_call(
        body, name="bias_expand",
        in_specs=[pl.BlockSpec(memory_space=pltpu.SMEM), pl.BlockSpec(memory_space=pltpu.VMEM)],
        out_specs=pl.BlockSpec(memory_space=pltpu.VMEM),
        out_shape=jax.ShapeDtypeStruct((SWA_HEADS, W, 2 * W), F32),
        compiler_params=_cparams(),
    )(rel_bias, bucket)


def _bias_reduce(dbias, dsink_rows, bucket):
    def body(db_ref, ds_ref, bucket_ref, out_ref, sink_out_ref, rows_ref):
        sink_lane = lax.broadcasted_iota(jnp.int32, (1, 128), 1)
        sink_acc = jnp.zeros((1, 128), F32)
        for h in range(SWA_HEADS):
            sink_acc = jnp.where(sink_lane == h, jnp.sum(ds_ref[h], axis=0, keepdims=True), sink_acc)
        sink_out_ref[...] = sink_acc
        bk = bucket_ref[...]
        for h in range(SWA_HEADS):
            dbh = db_ref[h]

            def add(b, carry):
                rows_ref[pl.ds(h * REL_BUCKETS + b, 1), :] = jnp.sum(jnp.where(bk == b, dbh, 0.0), axis=0, keepdims=True)
                return carry

            lax.fori_loop(0, REL_BUCKETS, add, 0)
        totals = jnp.sum(rows_ref[...], axis=-1, keepdims=True)
        lane = lax.broadcasted_iota(jnp.int32, (REL_BUCKETS, 128), 1)
        acc = jnp.zeros((REL_BUCKETS, 128), F32)
        for h in range(SWA_HEADS):
            acc = jnp.where(lane == h, totals[h * REL_BUCKETS:(h + 1) * REL_BUCKETS], acc)
        out_ref[...] = acc

    return pl.pallas_call(
        body, name="bias_reduce",
        out_shape=[jax.ShapeDtypeStruct((REL_BUCKETS, 128), F32), jax.ShapeDtypeStruct((1, 128), F32)],
        scratch_shapes=[pltpu.VMEM((SWA_HEADS * REL_BUCKETS, 2 * WINDOW), F32)],
        compiler_params=_cparams(),
    )(dbias, dsink_rows, bucket)


def _t5_bucket(dist):
    n = jnp.maximum(dist, 0)
    max_exact = REL_BUCKETS // 2
    nf = jnp.maximum(n, 1).astype(F32)
    large = max_exact + (jnp.log(nf / max_exact) / math.log(REL_MAX_DIST / max_exact) * (REL_BUCKETS - max_exact)).astype(jnp.int32)
    large = jnp.minimum(large, REL_BUCKETS - 1)
    return jnp.where(n < max_exact, n, large)


def _pair_heads(a, axis, inverse=False):
    shp = a.shape
    a = a.reshape(shp[:axis] + ((4, 2, 64) if inverse else (2, 4, 64)) + shp[axis + 1:])
    a = jnp.swapaxes(a, axis, axis + 1)
    return a.reshape(shp)


def _owner_blocks(w32, w16):
    r, c = w32.shape[0] // N_DEV, w32.shape[1]
    return w32.reshape(N_DEV, r, c), w16.reshape(N_DEV, r, c)


_IN_WIDTHS = (Q_LORA, KV_LORA, QK_ROPE, SWA_HEADS * SWA_HEAD_DIM, SWA_KV_HEADS * SWA_HEAD_DIM, SWA_KV_HEADS * SWA_HEAD_DIM)


def _win_to_kernel(wt):
    parts, o = [], 0
    for wd in _IN_WIDTHS:
        parts.append(wt[o:o + wd])
        o += wd
    cq, ckv, kpe, qs, ks, vs = parts
    return jnp.concatenate([cq, ckv, _pair_heads(qs, 0), ks, vs, jnp.pad(kpe, ((64, 32), (0, 0)))], axis=0)


def _win_from_kernel(dwt):
    o = [0, Q_LORA, Q_LORA + KV_LORA, Q_LORA + KV_LORA + 512, Q_LORA + KV_LORA + 640, Q_LORA + KV_LORA + 768]
    cq, ckv, qs, ks, vs, kpe = [dwt[a:b] for a, b in zip(o, o[1:] + [dwt.shape[0]])]
    return jnp.concatenate([cq, ckv, kpe[64:96], _pair_heads(qs, 0, inverse=True), ks, vs], axis=0)


def _wq_to_kernel(wt):
    return jnp.pad(wt.reshape(MLA_HEADS, QK_NOPE + QK_ROPE, Q_LORA), ((0, 0), (0, 32), (0, 0))).reshape(MLA_HEADS * HEAD_BLOCK, Q_LORA)


def _wq_from_kernel(dwt):
    return dwt.reshape(MLA_HEADS, HEAD_BLOCK, Q_LORA)[:, :QK_NOPE + QK_ROPE].reshape(-1, Q_LORA)


def _wkv_to_kernel(wt):
    w3 = wt.reshape(MLA_HEADS, QK_NOPE + V_HEAD, KV_LORA)
    kpart = jnp.pad(w3[:, :QK_NOPE], ((0, 0), (0, 64), (0, 0))).reshape(MLA_HEADS * HEAD_BLOCK, KV_LORA)
    return jnp.concatenate([kpart, w3[:, QK_NOPE:].reshape(MLA_HEADS * V_HEAD, KV_LORA)], axis=0)


def _wkv_from_kernel(dwt):
    dk = dwt[:MLA_HEADS * HEAD_BLOCK].reshape(MLA_HEADS, HEAD_BLOCK, KV_LORA)[:, :QK_NOPE]
    dv = dwt[MLA_HEADS * HEAD_BLOCK:].reshape(MLA_HEADS, V_HEAD, KV_LORA)
    return jnp.concatenate([dk, dv], axis=1).reshape(-1, KV_LORA)


def _wo_to_kernel(w):
    return jnp.concatenate([w[:512], _pair_heads(w[512:], 0)], axis=0)


def _wo_from_kernel(dw):
    return jnp.concatenate([dw[:512], _pair_heads(dw[512:], 0, inverse=True)], axis=0)


def _kernel_weights(key, stored):
    if key in ("g1", "u1", "g2", "u2"):
        return {key + "t": stored}
    if key in ("d1", "d2"):
        return {key: stored}
    if key == "o":
        return {key: _wo_to_kernel(stored)}
    return {key + "t": _TO_KERNEL[key](stored)}


_SMALL = ("g_ffn1", "g_mix", "g_q_a", "g_kv_a", "attn_sinks", "rel_bias", "g_out_mla", "g_out_swa", "g_ffn2", "g_final")


class _Comm:
    def __init__(self, shards):
        self.shards = shards
        self.own32, self.recv16, self.pending, self.token = {}, {}, [], None

    def gather(self, keys):
        return [("gather", self.shards[k]) for k in keys]

    def gathered(self, keys, results, wts):
        for k, g in zip(keys, results):
            wts.update(_kernel_weights(k, g.reshape(N_DEV * g.shape[1], g.shape[2])))

    def scatter(self, keys, grads):
        jobs = []
        for k in keys:
            g32, g16 = (_FROM_KERNEL[k](g) if k in _FROM_KERNEL else g for g in grads[k])
            self.own32[k], blocks = _owner_blocks(g32, g16)
            jobs.append(("scatter", blocks))
        return jobs

    def scattered(self, keys, results):
        self.recv16.update(zip(keys, results))

    def scatter_start(self, keys, grads, name):
        handle, self.token = _scatter_start([blocks for _, blocks in self.scatter(keys, grads)], name + "_start")
        self.pending.append((keys, handle, name + "_wait"))

    def scatter_wait(self, after):
        keys, handle, name = self.pending.pop(0)
        self.scattered(keys, _scatter_wait(handle, after, name))
        return keys


def _step(x, target, gains, wts, comm):
    def jobs_in(keys):
        return comm.gather(keys) if comm else None

    def jobs_out(keys):
        return comm.scatter(keys, grads) if comm else None

    B, S, D = x.shape
    T = B * S
    x2 = x.reshape(T, D)
    t2 = target.reshape(T, D)

    pos = jnp.arange(S, dtype=F32)
    inv_freq = ROPE_THETA ** (-jnp.arange(0, QK_ROPE, 2, dtype=F32) / QK_ROPE)
    ang = pos[:, None] * inv_freq[None, :]
    cos, sin = jnp.cos(ang), jnp.sin(ang)
    ones, zeros = jnp.ones((S, 64), F32), jnp.zeros((S, 32), F32)
    rope_c = jnp.concatenate([ones, cos, cos, zeros], axis=1)
    rope_s = jnp.concatenate([0.0 * ones, -sin, sin, zeros], axis=1)

    qi = jnp.arange(WINDOW)[:, None]
    kj = jnp.arange(2 * WINDOW)[None, :]
    bucket = _t5_bucket(qi + WINDOW - kj).astype(jnp.int32)
    bias = _bias_expand(gains["rel_bias"], bucket).reshape(SWA_KV_HEADS, SWA_ROWS, 2 * WINDOW)
    sinks = jnp.broadcast_to(gains["attn_sinks"].reshape(SWA_HEADS, 1), (SWA_HEADS, 128))
    g_swa = _pair_heads(gains["g_out_swa"], 1)

    keys = ("in", "q", "kv", "o", "g2")
    (h1, n1, gate1, up1, act1), got = _ffn_fwd(x2, gains["g_ffn1"], wts["g1t"], wts["u1t"], wts["d1"], "ffn1_fwd", jobs_in(keys))
    if comm:
        comm.gathered(keys, got, wts)
    (u, c_q, c_kv, cqn, ckvn, q, k, v, qs, ks, vs) = _proj_fwd(
        h1, gains["g_mix"], wts["int"], gains["g_q_a"], wts["qt"], gains["g_kv_a"], wts["kvt"], rope_c, rope_s, S)
    keys = ("u2", "d2")
    (o_mla, lse), got = _mla_fwd(q, k, v, B, S, jobs_in(keys))
    if comm:
        comm.gathered(keys, got, wts)
    o_swa = _swa_fwd(qs, ks, vs, bias, sinks, B, S)
    h2, oc = _out_fwd(o_mla, o_swa, gains["g_out_mla"], g_swa, wts["o"], h1)
    (dh3, n2, gate2, up2, act2, loss, dg_final), _ = _ffn_fwd(
        h2, gains["g_ffn2"], wts["g2t"], wts["u2t"], wts["d2"], "ffn2_fwd", loss_head=(t2, gains["g_final"].reshape(1, D)))

    grads = {}
    (dh2, dg_ffn2, dgate2, dup2), _ = _ffn_bwd(h2, gains["g_ffn2"], gate2, up2, dh3, wts["g2t"], wts["u2t"], wts["d2"], "ffn2_bwd")
    grads["g2"], _ = _matmul_tn(dgate2, n2, "dw_gate2")
    grads["u2"], _ = _matmul_tn(dup2, n2, "dw_up2")
    grads["d2"], _ = _matmul_tn(act2, dh3, "dw_down2", 0.5)
    do_mla, do_swa, dg_mla, dg_swa = _out_bwd(dh2, o_mla, o_swa, gains["g_out_mla"], g_swa, wts["o"])
    grads["o"], _ = _matmul_tn(oc, dh2, "dw_o")
    keys = ("g2", "u2", "d2")
    (dq, dk, dv), got = _mla_bwd(q, k, v, do_mla, o_mla, lse, B, S, jobs_out(keys))
    if comm:
        comm.scattered(keys, got)
    dqs, dks, dvs, dbias, dsink = _swa_bwd(qs, ks, vs, do_swa, bias, sinks, B, S)
    dh1, dproj, dql, dkvc, dg_mix, dg_q, dg_kv = _proj_bwd(
        dq, dk, dv, dqs, dks, dvs, c_q, c_kv, h1, dh2, gains["g_mix"], wts["int"], gains["g_q_a"], wts["qt"],
        gains["g_kv_a"], wts["kvt"], rope_c, rope_s, S)
    grads["in"], _ = _matmul_tn(dproj, u, "dw_in")
    grads["q"], _ = _matmul_tn(dql, cqn, "dw_q")
    grads["kv"], _ = _matmul_tn(dkvc, ckvn, "dw_kv")
    grads["d1"], _ = _matmul_tn(act1, dh1, "dw_down1", 0.5)
    if comm:
        comm.scatter_start(("o", "in", "q", "kv", "d1"), grads, "grad_scatter_mid")
    (dx, dg_ffn1, dgate1, dup1), _ = _ffn_bwd(
        x2, gains["g_ffn1"], gate1, up1, dh1, wts["g1t"], wts["u1t"], wts["d1"], "ffn1_bwd", after=comm and comm.token)
    grads["g1"], _ = _matmul_tn(dgate1, n1, "dw_gate1")
    if comm:
        comm.scatter_start(("g1",), grads, "grad_scatter_gate1")
    grads["u1"], _ = _matmul_tn(dup1, n1, "dw_up1", after=comm and comm.token)
    if comm:
        comm.scatter_start(("u1",), grads, "grad_scatter_up1")

    d_rel_bias, d_sinks = _bias_reduce(dbias.reshape(SWA_HEADS, WINDOW, 2 * WINDOW), dsink.reshape(SWA_HEADS, WINDOW, 128), bucket)
    small = {
        "g_ffn1": dg_ffn1, "g_mix": dg_mix, "g_q_a": dg_q, "g_kv_a": dg_kv, "attn_sinks": d_sinks,
        "rel_bias": d_rel_bias, "g_out_mla": dg_mla, "g_out_swa": _pair_heads(dg_swa, 1, inverse=True),
        "g_ffn2": dg_ffn2, "g_final": dg_final, "loss": loss,
    }
    return dx.reshape(B, S, D), grads, small


_WEIGHTS = ("g_ffn1", "w_ffn1_gate", "w_ffn1_up", "w_ffn1_down", "g_mix", "w_in", "g_q_a", "w_q_b", "g_kv_a", "w_kv_b",
            "attn_sinks", "rel_bias", "g_out_mla", "g_out_swa", "w_o", "g_ffn2", "w_ffn2_gate", "w_ffn2_up", "w_ffn2_down",
            "g_final")
_BIG = (("w_ffn1_gate", "g1", True), ("w_ffn1_up", "u1", True), ("w_ffn1_down", "d1", False), ("w_in", "in", True),
        ("w_q_b", "q", True), ("w_kv_b", "kv", True), ("w_o", "o", False), ("w_ffn2_gate", "g2", True),
        ("w_ffn2_up", "u2", True), ("w_ffn2_down", "d2", False))
_TO_KERNEL = {"in": _win_to_kernel, "q": _wq_to_kernel, "kv": _wkv_to_kernel}
_FROM_KERNEL = {"in": _win_from_kernel, "q": _wq_from_kernel, "kv": _wkv_from_kernel, "o": _wo_from_kernel}


def kernel(x, g_ffn1, w_ffn1_gate, w_ffn1_up, w_ffn1_down, g_mix, w_in, g_q_a, w_q_b, g_kv_a, w_kv_b, attn_sinks, rel_bias, g_out_mla, g_out_swa, w_o, g_ffn2, w_ffn2_gate, w_ffn2_up, w_ffn2_down, g_final, loss_target, m_g_ffn1, m_w_ffn1_gate, m_w_ffn1_up, m_w_ffn1_down, m_g_mix, m_w_in, m_g_q_a, m_w_q_b, m_g_kv_a, m_w_kv_b, m_attn_sinks, m_rel_bias, m_g_out_mla, m_g_out_swa, m_w_o, m_g_ffn2, m_w_ffn2_gate, m_w_ffn2_up, m_w_ffn2_down, m_g_final, v_g_ffn1, v_w_ffn1_gate, v_w_ffn1_up, v_w_ffn1_down, v_g_mix, v_w_in, v_g_q_a, v_w_q_b, v_g_kv_a, v_w_kv_b, v_attn_sinks, v_rel_bias, v_g_out_mla, v_g_out_swa, v_w_o, v_g_ffn2, v_w_ffn2_gate, v_w_ffn2_up, v_w_ffn2_down, v_g_final):
    w = dict(zip(_WEIGHTS, (g_ffn1, w_ffn1_gate, w_ffn1_up, w_ffn1_down, g_mix, w_in, g_q_a, w_q_b, g_kv_a, w_kv_b, attn_sinks,
                            rel_bias, g_out_mla, g_out_swa, w_o, g_ffn2, w_ffn2_gate, w_ffn2_up, w_ffn2_down, g_final)))
    m = dict(zip(_WEIGHTS, (m_g_ffn1, m_w_ffn1_gate, m_w_ffn1_up, m_w_ffn1_down, m_g_mix, m_w_in, m_g_q_a, m_w_q_b, m_g_kv_a,
                            m_w_kv_b, m_attn_sinks, m_rel_bias, m_g_out_mla, m_g_out_swa, m_w_o, m_g_ffn2, m_w_ffn2_gate,
                            m_w_ffn2_up, m_w_ffn2_down, m_g_final)))
    v = dict(zip(_WEIGHTS, (v_g_ffn1, v_w_ffn1_gate, v_w_ffn1_up, v_w_ffn1_down, v_g_mix, v_w_in, v_g_q_a, v_w_q_b, v_g_kv_a,
                            v_w_kv_b, v_attn_sinks, v_rel_bias, v_g_out_mla, v_g_out_swa, v_w_o, v_g_ffn2, v_w_ffn2_gate,
                            v_w_ffn2_up, v_w_ffn2_down, v_g_final)))
    me = 4 * lax.axis_index("x") + 2 * lax.axis_index("y") + lax.axis_index("c")

    stored = lambda src, name, by_col: src[name][0].T if by_col else src[name][0]
    shards = dict(zip([key for _, key, _ in _BIG], _cast_bf16([stored(w, name, by_col) for name, _, by_col in _BIG])))
    comm = _Comm(shards)
    first = ("g1", "u1", "d1")
    wts = {}
    comm.gathered(first, _all_gather([shards[k] for k in first], "weights_all_gather"), wts)

    gains = {n: (w[n] if n == "rel_bias" else w[n].reshape(1, -1)) for n in _SMALL}
    grad_x, _, small = _step(x, loss_target, gains, wts, comm)

    small_gathered = _all_gather([small[n] for n in _SMALL] + [small["loss"]], "small_all_gather")

    out_g, out_d, out_m, out_v = {}, {}, {}, {}
    me1 = me.astype(jnp.int32).reshape(1)
    big = {key: (name, by_col) for name, key, by_col in _BIG}

    def update(key):
        name, by_col = big[key]
        updates = _adam_big(stored(w, name, by_col), stored(m, name, by_col), stored(v, name, by_col),
                            comm.own32[key], comm.recv16[key], me1, comm.token)
        out_g[name], out_d[name], out_m[name], out_v[name] = ((a.T if by_col else a)[None] for a in updates)
        return updates[0]

    done = [update(key) for key in ("g2", "u2", "d2")]
    for _ in range(2):
        done = [update(key) for key in comm.scatter_wait(after=done[-1])]
    as_2d = lambda src: [src[n] if src[n].ndim == 2 else src[n].reshape(1, -1) for n in _SMALL]
    loss, updates = _adam_small(as_2d(w), as_2d(m), as_2d(v), small_gathered[:-1], small_gathered[-1])
    for n, (g, d, mn, vn) in zip(_SMALL, updates):
        out_g[n], out_d[n], out_m[n], out_v[n] = (a.reshape(w[n].shape) for a in (g, d, mn, vn))
    for key in comm.scatter_wait(after=loss):
        update(key)

    return (loss[0, 0], grad_x, *[out_g[n] for n in _WEIGHTS], *[out_d[n] for n in _WEIGHTS],
            *[out_m[n] for n in _WEIGHTS], *[out_v[n] for n in _WEIGHTS])
```

```python
import functools
import math

import jax
import jax.numpy as jnp
from jax import lax
from jax.experimental import pallas as pl
from jax.experimental.pallas import tpu as pltpu

F32 = jnp.float32
BF16 = jnp.bfloat16
MESH = pl.DeviceIdType.MESH

EPS = 1e-6
NEG = -1e30
N_DEV = 8

MLA_HEADS = 8
Q_LORA = 256
KV_LORA = 128
QK_NOPE = 64
QK_ROPE = 32
V_HEAD = 64
ROPE_THETA = 10000.0
SWA_HEADS = 8
SWA_KV_HEADS = 2
SWA_HEAD_DIM = 64
WINDOW = 128
REL_BUCKETS = 32
REL_MAX_DIST = 128
HEAD_BLOCK = 128
MLA_Q_TILES = (512, 256, 128)
MLA_SCALE = (QK_NOPE + QK_ROPE) ** -0.5
MLA_SCALE_LOG2 = MLA_SCALE * math.log2(math.e)
SWA_SCALE = SWA_HEAD_DIM ** -0.5

ADAM_LR = 0.001
ADAM_B1 = 0.9
ADAM_B2 = 0.999
ADAM_EPS = 1e-08
ADAM_WD = 0.01
ADAM_STEP = 10

VMEM_LIMIT = 56 * 1024 * 1024


def _cparams(semantics=None):
    return pltpu.CompilerParams(dimension_semantics=semantics, vmem_limit_bytes=VMEM_LIMIT)


def _dot(a, b):
    return jnp.dot(a, b, preferred_element_type=F32)


def _dot_nt(a, b):
    return lax.dot_general(a, b, (((1,), (1,)), ((), ())), preferred_element_type=F32)


def _dot_tn(a, b):
    return lax.dot_general(a, b, (((0,), (0,)), ((), ())), preferred_element_type=F32)


def _rms_fwd(x, g):
    r = lax.rsqrt(jnp.mean(x * x, axis=-1, keepdims=True) + EPS)
    return x * r * g, r


def _rms_bwd(x, g, r, dy):
    xh = x * r
    dyg = dy * g
    dx = r * (dyg - xh * jnp.mean(dyg * xh, axis=-1, keepdims=True))
    return dx, jnp.sum(dy * xh, axis=0, keepdims=True)


def _swap16(x):
    n = x.shape[-1]
    up = pltpu.roll(x, n - 16, 1)
    down = pltpu.roll(x, 16, 1)
    lane = lax.broadcasted_iota(jnp.int32, x.shape, 1) % HEAD_BLOCK
    return jnp.where((lane >= 64) & (lane < 80), up, jnp.where((lane >= 80) & (lane < 96), down, 0.0))


def _const_spec(shape):
    return pl.BlockSpec(shape, lambda *_: (0,) * len(shape), pipeline_mode=pl.Buffered(1))


def _acc_spec(shape):
    return pl.BlockSpec(shape, lambda *_: (0,) * len(shape))


def _row_spec(tm, width):
    return pl.BlockSpec((tm, width), lambda i: (i, 0))


def _cast_bf16(arrays):
    n = len(arrays)

    def body(*refs):
        for src, dst in zip(refs[:n], refs[n:]):
            dst[...] = src[...].astype(BF16)

    return pl.pallas_call(
        body, name="cast_weights",
        out_shape=[jax.ShapeDtypeStruct(a.shape, BF16) for a in arrays],
        compiler_params=_cparams(),
    )(*arrays)


def _all_gather(arrays, name):
    n = len(arrays)

    def body(*refs):
        ins, outs = refs[:n], refs[n:2 * n]
        send_sems, recv_sems, local_sems = refs[2 * n:]
        x, y, c = lax.axis_index("x"), lax.axis_index("y"), lax.axis_index("c")
        me, sibling = (x, y, c), (x, y, 1 - c)
        chips = [(1 - x, y), (x, 1 - y), (1 - x, 1 - y)]

        def slot(a, dev):
            return outs[a].at[4 * dev[0] + 2 * dev[1] + dev[2]]

        def copy(a, k, block, to, src=None):
            return pltpu.make_async_remote_copy(
                src_ref=slot(a, block) if src is None else src, dst_ref=slot(a, block),
                send_sem=send_sems.at[a, k], recv_sem=recv_sems.at[a, k], device_id=to, device_id_type=MESH)

        mine = [pltpu.make_async_copy(ins[a], slot(a, me), local_sems.at[a]) for a in range(n)]
        for cp in mine:
            cp.start()
        first = []
        for a in range(n):
            first.append(copy(a, 0, me, sibling, src=ins[a]))
            first += [copy(a, 1 + j, me, (*chip, c), src=ins[a]) for j, chip in enumerate(chips)]
        for cp in first:
            cp.start()
        passed = []
        for j, chip in enumerate(chips):
            for a in range(n):
                copy(a, 1 + j, (*chip, c), me).wait_recv()
                cp = copy(a, 4 + j, (*chip, c), sibling)
                cp.start()
                passed.append(cp)
        for a in range(n):
            copy(a, 0, sibling, me).wait_recv()
            for j, chip in enumerate(chips):
                copy(a, 4 + j, (*chip, 1 - c), me).wait_recv()
        for cp in first + passed:
            cp.wait_send()
        for cp in mine:
            cp.wait()

    any_spec = pl.BlockSpec(memory_space=pl.ANY)
    return pl.pallas_call(
        body, name=name,
        out_shape=[jax.ShapeDtypeStruct((N_DEV,) + a.shape, a.dtype) for a in arrays],
        in_specs=[any_spec] * n, out_specs=[any_spec] * n,
        scratch_shapes=[pltpu.SemaphoreType.DMA((n, 7)), pltpu.SemaphoreType.DMA((n, 7)), pltpu.SemaphoreType.DMA((n,))],
    )(*arrays)


def _exchange_shapes(jobs):
    dsts = [jax.ShapeDtypeStruct(((N_DEV,) + a.shape) if kind == "gather" else a.shape, a.dtype) for kind, a in jobs]
    n = len(jobs)
    sems = [pltpu.SemaphoreType.DMA((n, N_DEV - 1)), pltpu.SemaphoreType.DMA((n, N_DEV - 1)), pltpu.SemaphoreType.DMA((n,))]
    return dsts, sems


def _exchange_copies(kinds, srcs, dsts, send_sems, recv_sems, local_sems):
    x, y, c = lax.axis_index("x"), lax.axis_index("y"), lax.axis_index("c")
    me = 4 * x + 2 * y + c
    remote, local = [], []
    for a, kind in enumerate(kinds):
        for k in range(1, N_DEV):
            peer = (1 - x if k & 4 else x, 1 - y if k & 2 else y, 1 - c if k & 1 else c)
            src = srcs[a] if kind == "gather" else srcs[a].at[4 * peer[0] + 2 * peer[1] + peer[2]]
            remote.append(pltpu.make_async_remote_copy(
                src_ref=src, dst_ref=dsts[a].at[me], send_sem=send_sems.at[a, k - 1], recv_sem=recv_sems.at[a, k - 1],
                device_id=peer, device_id_type=MESH))
        src = srcs[a] if kind == "gather" else srcs[a].at[me]
        local.append(pltpu.make_async_copy(src, dsts[a].at[me], local_sems.at[a]))
    return remote, local


def _scatter_copies(srcs, lands, send_sems, recv_sems):
    x, y, c = lax.axis_index("x"), lax.axis_index("y"), lax.axis_index("c")
    me = 4 * x + 2 * y + c
    copies = []
    for a, (src_ref, land_ref) in enumerate(zip(srcs, lands)):
        for k in range(1, N_DEV):
            peer = (1 - x if k & 4 else x, 1 - y if k & 2 else y, 1 - c if k & 1 else c)
            copies.append(pltpu.make_async_remote_copy(
                src_ref=src_ref.at[4 * peer[0] + 2 * peer[1] + peer[2]], dst_ref=land_ref.at[me],
                send_sem=send_sems.at[a * (N_DEV - 1) + k - 1], recv_sem=recv_sems.at[a * (N_DEV - 1) + k - 1],
                device_id=peer, device_id_type=MESH))
    return copies


_SPLIT_EFFECT = pltpu.SideEffectType.DATAFLOW_SIDE_EFFECTING


def _scatter_start(blocks, name):
    n = len(blocks)
    hbm = pl.BlockSpec(memory_space=pltpu.HBM)
    sem = pl.BlockSpec(memory_space=pltpu.SEMAPHORE)

    def body(*refs):
        srcs, lands, send_sems, recv_sems, token = refs[:n], refs[n:2 * n], refs[2 * n], refs[2 * n + 1], refs[-1]
        for cp in _scatter_copies(srcs, lands, send_sems, recv_sems):
            cp.start()
        token[...] = jnp.zeros_like(token)

    through = [pltpu.HBM(b.shape, b.dtype) for b in blocks]
    *handle, token = pl.pallas_call(
        body, name=name,
        out_shape=(pltpu.SemaphoreType.DMA((n * (N_DEV - 1),)), pltpu.SemaphoreType.DMA((n * (N_DEV - 1),)), *through, *through,
                   jax.ShapeDtypeStruct((8, 128), F32)),
        in_specs=(hbm,) * (2 * n), out_specs=(sem, sem) + (hbm,) * (2 * n) + (pl.BlockSpec(memory_space=pltpu.VMEM),),
        input_output_aliases={i: 2 + i for i in range(2 * n)},
        compiler_params=pltpu.CompilerParams(has_side_effects=_SPLIT_EFFECT),
    )(*[pltpu.with_memory_space_constraint(b, pltpu.HBM) for b in blocks],
      *[pltpu.with_memory_space_constraint(lax.empty(b.shape, b.dtype), pltpu.HBM) for b in blocks])
    return handle, token


def _scatter_wait(handle, after, name):
    hbm = pl.BlockSpec(memory_space=pltpu.HBM)
    sem = pl.BlockSpec(memory_space=pltpu.SEMAPHORE)
    send_sems, recv_sems, *through = handle
    n = len(through) // 2

    def body(*refs):
        for cp in _scatter_copies(refs[:n], refs[n:2 * n], refs[2 * n], refs[2 * n + 1]):
            cp.wait_send()
            cp.wait_recv()

    outs = pl.pallas_call(
        body, name=name, out_shape=tuple(pltpu.HBM(t.shape, t.dtype) for t in through),
        in_specs=(hbm,) * (2 * n) + (sem, sem, pl.BlockSpec(memory_space=pl.ANY)), out_specs=(hbm,) * (2 * n),
        input_output_aliases={i: i for i in range(2 * n)},
        compiler_params=pltpu.CompilerParams(has_side_effects=_SPLIT_EFFECT),
    )(*through, send_sems, recv_sems, after)
    return list(outs[n:])


def _pcall(body, *, name, grid, in_specs, out_specs, out_shape, args, scratch_shapes=(), exchange=None, after=None):
    semantics = ("arbitrary",) * len(grid)
    exchange = exchange or []
    extra = [] if after is None else [after]
    if not exchange and not extra:
        outs = pl.pallas_call(body, name=name, grid=grid, in_specs=list(in_specs), out_specs=list(out_specs),
                              out_shape=list(out_shape), scratch_shapes=list(scratch_shapes),
                              compiler_params=_cparams(semantics))(*args)
        return list(outs), []
    kinds = [kind for kind, _ in exchange]
    n_in, n_out, n_job = len(in_specs), len(out_specs), len(exchange)
    dst_shapes, sems = _exchange_shapes(exchange) if exchange else ([], [])

    def wrapped(*refs):
        ins, srcs = refs[:n_in], refs[n_in:n_in + n_job]
        o0 = n_in + n_job + len(extra)
        outs, dsts, rest = refs[o0:o0 + n_out], refs[o0 + n_out:o0 + n_out + n_job], refs[o0 + n_out + n_job:]
        if not exchange:
            body(*ins, *outs, *rest)
            return
        scratch, sem_refs = rest[:-3], rest[-3:]
        first = functools.reduce(jnp.logical_and, [pl.program_id(d) == 0 for d in range(len(grid))])
        last = functools.reduce(jnp.logical_and, [pl.program_id(d) == grid[d] - 1 for d in range(len(grid))])

        @pl.when(first)
        def _():
            remote, local = _exchange_copies(kinds, srcs, dsts, *sem_refs)
            for cp in local + remote:
                cp.start()

        body(*ins, *outs, *scratch)

        @pl.when(last)
        def _():
            remote, local = _exchange_copies(kinds, srcs, dsts, *sem_refs)
            for cp in remote + local:
                cp.wait()

    any_spec = pl.BlockSpec(memory_space=pl.ANY)
    outs = pl.pallas_call(
        wrapped, name=name, grid=grid, in_specs=list(in_specs) + [any_spec] * (n_job + len(extra)),
        out_specs=list(out_specs) + [any_spec] * n_job, out_shape=list(out_shape) + dst_shapes,
        scratch_shapes=list(scratch_shapes) + sems, compiler_params=_cparams(semantics),
    )(*args, *[a for _, a in exchange], *extra)
    return list(outs[:n_out]), list(outs[n_out:])


def _adam_math(w, g, m, v):
    m = ADAM_B1 * m + (1.0 - ADAM_B1) * g
    v = ADAM_B2 * v + (1.0 - ADAM_B2) * (g * g)
    m_hat = m / (1.0 - ADAM_B1 ** ADAM_STEP)
    v_hat = v / (1.0 - ADAM_B2 ** ADAM_STEP)
    delta = -ADAM_LR * (m_hat / (jnp.sqrt(v_hat) + ADAM_EPS) + ADAM_WD * w)
    return delta, m, v


def _adam_big(w, m, v, own32, recv16, me, after):
    r, c = w.shape
    tr = _pick(r, (176,))
    assert r % tr == 0

    def body(me_ref, w_ref, m_ref, v_ref, s_ref, r_ref, after_ref, g_out, d_out, m_out, v_out):
        g = s_ref[...]
        for k in range(1, N_DEV):
            g = g + r_ref[jnp.bitwise_xor(me_ref[0], k)].astype(F32)
        d, mn, vn = _adam_math(w_ref[...], g, m_ref[...], v_ref[...])
        g_out[...] = g
        d_out[...] = d
        m_out[...] = mn
        v_out[...] = vn

    blk = pl.BlockSpec((tr, c), lambda i, s: (i, 0))
    grid_spec = pltpu.PrefetchScalarGridSpec(
        num_scalar_prefetch=1, grid=(r // tr,),
        in_specs=[blk, blk, blk, pl.BlockSpec((None, tr, c), lambda i, s: (s[0], i, 0)),
                  pl.BlockSpec((N_DEV, tr, c), lambda i, s: (0, i, 0)), pl.BlockSpec(memory_space=pl.ANY)],
        out_specs=[blk, blk, blk, blk])
    return pl.pallas_call(
        body, name="adamw_big", grid_spec=grid_spec,
        out_shape=[jax.ShapeDtypeStruct((r, c), F32)] * 4,
        compiler_params=_cparams(("arbitrary",)),
    )(me, w, m, v, own32, recv16, after)


def _adam_small(ws, ms, vs, partials, loss_parts):
    n = len(ws)

    def body(*refs):
        w_refs, m_refs, v_refs, p_refs = (refs[i * n:(i + 1) * n] for i in range(4))
        l_ref, loss_out = refs[4 * n], refs[4 * n + 1]
        outs = refs[4 * n + 2:]

        def total(ref):
            s = ref[0]
            for dev in range(1, N_DEV):
                s = s + ref[dev]
            return s

        loss_out[...] = total(l_ref)
        for i in range(n):
            r, c = w_refs[i].shape
            g = total(p_refs[i])[:, :c]
            d, mn, vn = _adam_math(w_refs[i][...], g, m_refs[i][...], v_refs[i][...])
            for j, val in enumerate((g, d, mn, vn)):
                outs[4 * i + j][...] = val

    shapes = [jax.ShapeDtypeStruct((1, 128), F32)]
    for w in ws:
        shapes += [jax.ShapeDtypeStruct(w.shape, F32)] * 4
    outs = pl.pallas_call(body, name="adamw_small", out_shape=shapes, compiler_params=_cparams())(*ws, *ms, *vs, *partials, loss_parts)
    return outs[0], [outs[1 + 4 * i:5 + 4 * i] for i in range(n)]


def _pick(n, prefer):
    for t in prefer:
        if n % t == 0:
            return t
    return n


FFN_FWD_GROUP = 512


def _ffn_fwd(x, g, wg, wu, wd, name, exchange=None, loss_head=None):
    T, D = x.shape
    F = wg.shape[0]
    tm = _pick(T, (512, 256, 128))
    fc = _pick(F, (256, 128))

    def body(x_ref, g_ref, wg_ref, wu_ref, wd_ref, *rest):
        h_ref, n_ref, gate_ref, up_ref, act_ref = rest[-7:-2] if loss_head else rest
        xv = x_ref[...]
        n, _ = _rms_fwd(xv, g_ref[...])
        nb = n.astype(BF16)
        n_ref[...] = nb
        acc = jnp.zeros((tm, D), F32)
        for g0 in range(0, F, FFN_FWD_GROUP):
            g1 = min(g0 + FFN_FWD_GROUP, F)
            for f0 in range(g0, g1, fc):
                f1 = min(f0 + fc, g1)
                gate = _dot_nt(nb, wg_ref[f0:f1, :])
                up = _dot_nt(nb, wu_ref[f0:f1, :])
                gate_ref[:, f0:f1] = gate.astype(BF16)
                up_ref[:, f0:f1] = up.astype(BF16)
                act_ref[:, f0:f1] = (gate * (1.0 / (1.0 + jnp.exp(-gate))) * up).astype(BF16)
            acc = acc + _dot(act_ref[:, g0:g1], wd_ref[g0:g1, :])
        hv = xv + 0.5 * acc
        if not loss_head:
            h_ref[...] = hv
            return
        t_ref, gf_ref, loss_ref, dgf_ref = rest[0], rest[1], rest[-2], rest[-1]
        gf = gf_ref[...]
        y, r = _rms_fwd(hv, gf)
        diff = y - t_ref[...]
        dh, dgp = _rms_bwd(hv, gf, r, diff * (1.0 / D))
        h_ref[...] = dh
        part = 0.5 * jnp.sum(jnp.sum(diff * diff, axis=-1, keepdims=True) * (1.0 / D), axis=0, keepdims=True)

        @pl.when(pl.program_id(0) == 0)
        def _():
            loss_ref[...] = jnp.zeros_like(loss_ref)
            dgf_ref[...] = jnp.zeros_like(dgf_ref)

        loss_ref[...] += jnp.broadcast_to(part, loss_ref.shape)
        dgf_ref[...] += dgp

    in_specs = [_row_spec(tm, D), _const_spec((1, D)), _const_spec((F, D)), _const_spec((F, D)), _const_spec((F, D))]
    out_specs = [_row_spec(tm, D), _row_spec(tm, D), _row_spec(tm, F), _row_spec(tm, F), _row_spec(tm, F)]
    out_shape = [jax.ShapeDtypeStruct((T, D), F32), jax.ShapeDtypeStruct((T, D), BF16),
                 jax.ShapeDtypeStruct((T, F), BF16), jax.ShapeDtypeStruct((T, F), BF16), jax.ShapeDtypeStruct((T, F), BF16)]
    args = (x, g, wg, wu, wd)
    if loss_head:
        in_specs += [_row_spec(tm, D), _const_spec((1, D))]
        out_specs += [_acc_spec((1, 128)), _acc_spec((1, D))]
        out_shape += [jax.ShapeDtypeStruct((1, 128), F32), jax.ShapeDtypeStruct((1, D), F32)]
        args += tuple(loss_head)
    return _pcall(body, name=name, grid=(T // tm,), in_specs=in_specs, out_specs=out_specs, out_shape=out_shape, args=args,
                  exchange=exchange)


FFN_BWD_CHUNK = 1408
FFN_BWD_GROUP = 1408


def _ffn_bwd(x, g, gate, up, dh, wgt, wut, wdt, name, exchange=None, after=None):
    T, D = x.shape
    F = wgt.shape[0]
    tm = _pick(T, (512, 256, 128))
    halves = 2
    Fh = F // halves
    assert Fh * halves == F and Fh % 128 == 0
    groups = [(g0, min(g0 + FFN_BWD_GROUP, Fh)) for g0 in range(0, Fh, FFN_BWD_GROUP)]

    def body(x_ref, g_ref, gate_ref, up_ref, dh_ref, wg_ref, wu_ref, wd_ref, dx_ref, dg_ref, dgate_ref, dup_ref, dn_ref):
        i, h = pl.program_id(0), pl.program_id(1)
        dhv = dh_ref[...]
        dho = (0.5 * dhv).astype(BF16)
        dn = jnp.zeros((tm, D), F32)
        for g0, g1 in groups:
            for f0 in range(g0, g1, FFN_BWD_CHUNK):
                f1 = min(f0 + FFN_BWD_CHUNK, g1)
                d_act = _dot_nt(dho, wd_ref[f0:f1, :])
                gt = gate_ref[:, f0:f1].astype(F32)
                u = up_ref[:, f0:f1].astype(F32)
                sg = 1.0 / (1.0 + jnp.exp(-gt))
                silu = gt * sg
                dup_ref[:, f0:f1] = (d_act * silu).astype(BF16)
                dgate_ref[:, f0:f1] = (d_act * u * (sg + silu * (1.0 - sg))).astype(BF16)
            dn = dn + _dot(dgate_ref[:, g0:g1], wg_ref[g0:g1, :]) + _dot(dup_ref[:, g0:g1], wu_ref[g0:g1, :])

        @pl.when(h == 0)
        def _():
            dn_ref[...] = dn

        @pl.when(h > 0)
        def _():
            dn_ref[...] += dn

        @pl.when((i == 0) & (h == 0))
        def _():
            dg_ref[...] = jnp.zeros_like(dg_ref)

        @pl.when(h == halves - 1)
        def _():
            xv, gv = x_ref[...], g_ref[...]
            _, r = _rms_fwd(xv, gv)
            dxn, dgp = _rms_bwd(xv, gv, r, dn_ref[...])
            dx_ref[...] = dhv + dxn
            dg_ref[...] += dgp

    tile = pl.BlockSpec((tm, D), lambda i, h: (i, 0))
    wide = pl.BlockSpec((tm, Fh), lambda i, h: (i, h))
    return _pcall(
        body, name=name, grid=(T // tm, halves),
        in_specs=[tile, _const_spec((1, D)), wide, wide, tile,
                  pl.BlockSpec((Fh, D), lambda i, h: (h, 0)), pl.BlockSpec((Fh, D), lambda i, h: (h, 0)),
                  pl.BlockSpec((Fh, D), lambda i, h: (h, 0))],
        out_specs=[tile, _acc_spec((1, D)), wide, wide],
        out_shape=[jax.ShapeDtypeStruct((T, D), F32), jax.ShapeDtypeStruct((1, D), F32),
                   jax.ShapeDtypeStruct((T, F), BF16), jax.ShapeDtypeStruct((T, F), BF16)],
        scratch_shapes=[pltpu.VMEM((tm, D), F32)],
        args=(x, g, gate, up, dh, wgt, wut, wdt), exchange=exchange, after=after)


def _matmul_tn(a, b, name, scale=None, exchange=None, after=None):
    T, K = a.shape
    N = b.shape[1]
    tk = _pick(K, (1024, 1408, 1280, 768, 512, 256, 128))
    tn = _pick(N, (1024, 1408, 1280, 768, 512, 256, 128))
    tt = _pick(T, (2048, 1024, 512, 256, 128))
    n_t = T // tt

    def body(a_ref, b_ref, o_ref, o16_ref):
        @pl.when(pl.program_id(2) == 0)
        def _():
            o_ref[...] = jnp.zeros_like(o_ref)

        o_ref[...] += _dot_tn(a_ref[...].astype(BF16), b_ref[...].astype(BF16))

        @pl.when(pl.program_id(2) == n_t - 1)
        def _():
            if scale is not None:
                o_ref[...] = o_ref[...] * scale
            o16_ref[...] = o_ref[...].astype(BF16)

    tile = pl.BlockSpec((tk, tn), lambda i, j, t: (i, j))
    outs, got = _pcall(
        body, name=name, grid=(K // tk, N // tn, n_t),
        in_specs=[pl.BlockSpec((tt, tk), lambda i, j, t: (t, i)), pl.BlockSpec((tt, tn), lambda i, j, t: (t, j))],
        out_specs=[tile, tile],
        out_shape=[jax.ShapeDtypeStruct((K, N), F32), jax.ShapeDtypeStruct((K, N), BF16)],
        args=(a, b), exchange=exchange, after=after)
    return tuple(outs), got


def _rope_spec(tm, S):
    per = S // tm
    return pl.BlockSpec((tm, HEAD_BLOCK), lambda i: (i % per, 0))


def _proj_fwd(h1, g_mix, w_in, g_q, wq, g_kv, wkv, rope_c, rope_s, S):
    T, D = h1.shape
    P = w_in.shape[0]
    tm = _pick(S, (512, 256, 128))
    QW = MLA_HEADS * HEAD_BLOCK
    VW = MLA_HEADS * V_HEAD
    SQ = SWA_HEADS * SWA_HEAD_DIM
    SK = SWA_KV_HEADS * SWA_HEAD_DIM
    o_cq, o_ckv, o_qs, o_ks, o_vs, o_kpe = 0, Q_LORA, Q_LORA + KV_LORA, Q_LORA + KV_LORA + SQ, Q_LORA + KV_LORA + SQ + SK, Q_LORA + KV_LORA + SQ + 2 * SK
    assert P == o_kpe + HEAD_BLOCK

    def body(h_ref, gm_ref, win_ref, gq_ref, wq_ref, gkv_ref, wkv_ref, c_ref, s_ref,
             u_ref, cq_ref, ckv_ref, cqn_ref, ckvn_ref, q_ref, k_ref, v_ref, qs_ref, ks_ref, vs_ref):
        u, _ = _rms_fwd(h_ref[...], gm_ref[...])
        ub = u.astype(BF16)
        u_ref[...] = ub
        proj = _dot_nt(ub, win_ref[...])
        c_q = proj[:, o_cq:o_ckv]
        c_kv = proj[:, o_ckv:o_qs]
        cq_ref[...] = c_q
        ckv_ref[...] = c_kv
        qs_ref[...] = proj[:, o_qs:o_ks].astype(BF16)
        ks_ref[...] = proj[:, o_ks:o_vs].astype(BF16)
        vs_ref[...] = proj[:, o_vs:o_kpe].astype(BF16)
        cb, sb = c_ref[...], s_ref[...]
        kpe = proj[:, o_kpe:P]
        kpe = kpe * cb + _swap16(kpe) * sb
        cqn, _ = _rms_fwd(c_q, gq_ref[...])
        cqn = cqn.astype(BF16)
        cqn_ref[...] = cqn
        q = _dot_nt(cqn, wq_ref[...])
        q = q * jnp.tile(cb, (1, MLA_HEADS)) + _swap16(q) * jnp.tile(sb, (1, MLA_HEADS))
        q_ref[...] = q.astype(BF16)
        ckvn, _ = _rms_fwd(c_kv, gkv_ref[...])
        ckvn = ckvn.astype(BF16)
        ckvn_ref[...] = ckvn
        kv = _dot_nt(ckvn, wkv_ref[...])
        k_ref[...] = (kv[:, :QW] + jnp.tile(kpe, (1, MLA_HEADS))).astype(BF16)
        v_ref[...] = kv[:, QW:].astype(BF16)

    widths = [(D, BF16), (Q_LORA, F32), (KV_LORA, F32), (Q_LORA, BF16), (KV_LORA, BF16), (QW, BF16), (QW, BF16), (VW, BF16),
              (SQ, BF16), (SK, BF16), (SK, BF16)]
    return pl.pallas_call(
        body, name="proj_fwd", grid=(T // tm,),
        in_specs=[_row_spec(tm, D), _const_spec((1, D)), _const_spec((P, D)), _const_spec((1, Q_LORA)),
                  _const_spec((QW, Q_LORA)), _const_spec((1, KV_LORA)), _const_spec((QW + VW, KV_LORA)),
                  _rope_spec(tm, S), _rope_spec(tm, S)],
        out_specs=[_row_spec(tm, w) for w, _ in widths],
        out_shape=[jax.ShapeDtypeStruct((T, w), dt) for w, dt in widths],
        compiler_params=_cparams(("arbitrary",)),
    )(h1, g_mix, w_in, g_q, wq, g_kv, wkv, rope_c, rope_s)


def _proj_bwd(dq, dk, dv, dqs, dks, dvs, c_q, c_kv, h1, dh2, g_mix, w_in, g_q, wq, g_kv, wkv, rope_c, rope_s, S):
    T, D = h1.shape
    P = w_in.shape[0]
    tm = _pick(S, (512, 256, 128))
    QW = MLA_HEADS * HEAD_BLOCK
    VW = MLA_HEADS * V_HEAD
    SQ = SWA_HEADS * SWA_HEAD_DIM
    SK = SWA_KV_HEADS * SWA_HEAD_DIM

    def body(dq_ref, dk_ref, dv_ref, dqs_ref, dks_ref, dvs_ref, cq_ref, ckv_ref, h_ref, dh2_ref, gm_ref, win_ref, gq_ref,
             wq_ref, gkv_ref, wkv_ref, c_ref, s_ref, dh1_ref, dproj_ref, dql_ref, dkv_ref, dgm_ref, dgq_ref, dgkv_ref):
        cb, sb = c_ref[...], s_ref[...]
        dqv = dq_ref[...]
        dql = dqv * jnp.tile(cb, (1, MLA_HEADS)) + _swap16(dqv * jnp.tile(sb, (1, MLA_HEADS)))
        dql = dql.astype(BF16)
        dql_ref[...] = dql
        c_q = cq_ref[...]
        _, rq = _rms_fwd(c_q, gq_ref[...])
        d_cq, dgq = _rms_bwd(c_q, gq_ref[...], rq, _dot(dql, wq_ref[...]))

        dkv_all = dk_ref[...]
        dkpe = dkv_all[:, 0:HEAD_BLOCK]
        for h in range(1, MLA_HEADS):
            dkpe = dkpe + dkv_all[:, h * HEAD_BLOCK:(h + 1) * HEAD_BLOCK]
        lane = lax.broadcasted_iota(jnp.int32, dkpe.shape, 1)
        dkpe = jnp.where((lane >= 64) & (lane < 96), dkpe, 0.0)
        dkpe = dkpe * cb + _swap16(dkpe * sb)
        dkvc = jnp.concatenate([dkv_all.astype(BF16), dv_ref[...].astype(BF16)], axis=1)
        dkv_ref[...] = dkvc
        c_kv = ckv_ref[...]
        _, rkv = _rms_fwd(c_kv, gkv_ref[...])
        d_ckv, dgkv = _rms_bwd(c_kv, gkv_ref[...], rkv, _dot(dkvc, wkv_ref[...]))

        dproj = jnp.concatenate([d_cq.astype(BF16), d_ckv.astype(BF16), dqs_ref[...].astype(BF16),
                                 dks_ref[...].astype(BF16), dvs_ref[...].astype(BF16), dkpe.astype(BF16)], axis=1)
        dproj_ref[...] = dproj
        hv = h_ref[...]
        _, rm = _rms_fwd(hv, gm_ref[...])
        dxn, dgm = _rms_bwd(hv, gm_ref[...], rm, _dot(dproj, win_ref[...]))
        dh1_ref[...] = dh2_ref[...] + dxn

        @pl.when(pl.program_id(0) == 0)
        def _():
            dgm_ref[...] = jnp.zeros_like(dgm_ref)
            dgq_ref[...] = jnp.zeros_like(dgq_ref)
            dgkv_ref[...] = jnp.zeros_like(dgkv_ref)

        dgm_ref[...] += dgm
        dgq_ref[...] += dgq
        dgkv_ref[...] += dgkv

    return pl.pallas_call(
        body, name="proj_bwd", grid=(T // tm,),
        in_specs=[_row_spec(tm, QW), _row_spec(tm, QW), _row_spec(tm, VW), _row_spec(tm, SQ), _row_spec(tm, SK), _row_spec(tm, SK),
                  _row_spec(tm, Q_LORA), _row_spec(tm, KV_LORA), _row_spec(tm, D), _row_spec(tm, D),
                  _const_spec((1, D)), _const_spec((P, D)), _const_spec((1, Q_LORA)), _const_spec((QW, Q_LORA)),
                  _const_spec((1, KV_LORA)), _const_spec((QW + VW, KV_LORA)), _rope_spec(tm, S), _rope_spec(tm, S)],
        out_specs=[_row_spec(tm, D), _row_spec(tm, P), _row_spec(tm, QW), _row_spec(tm, QW + VW),
                   _acc_spec((1, D)), _acc_spec((1, Q_LORA)), _acc_spec((1, KV_LORA))],
        out_shape=[jax.ShapeDtypeStruct((T, D), F32), jax.ShapeDtypeStruct((T, P), BF16), jax.ShapeDtypeStruct((T, QW), BF16),
                   jax.ShapeDtypeStruct((T, QW + VW), BF16), jax.ShapeDtypeStruct((1, D), F32),
                   jax.ShapeDtypeStruct((1, Q_LORA), F32), jax.ShapeDtypeStruct((1, KV_LORA), F32)],
        compiler_params=_cparams(("arbitrary",)),
    )(dq, dk, dv, dqs, dks, dvs, c_q, c_kv, h1, dh2, g_mix, w_in, g_q, wq, g_kv, wkv, rope_c, rope_s)


def _out_fwd(o_mla, o_swa, g_mla, g_swa, w_o, h1):
    T, D = h1.shape
    W = o_mla.shape[1]
    tm = _pick(T, (1024, 512, 256, 128))

    def body(om_ref, os_ref, gm_ref, gs_ref, wo_ref, h_ref, h2_ref, oc_ref):
        a, _ = _rms_fwd(om_ref[...], gm_ref[...])
        b, _ = _rms_fwd(os_ref[...], gs_ref[...])
        oc = jnp.concatenate([a.astype(BF16), b.astype(BF16)], axis=1)
        oc_ref[...] = oc
        h2_ref[...] = h_ref[...] + _dot(oc, wo_ref[...])

    return pl.pallas_call(
        body, name="out_fwd", grid=(T // tm,),
        in_specs=[_row_spec(tm, W), _row_spec(tm, W), _const_spec((1, W)), _const_spec((1, W)), _const_spec((2 * W, D)),
                  _row_spec(tm, D)],
        out_specs=[_row_spec(tm, D), _row_spec(tm, 2 * W)],
        out_shape=[jax.ShapeDtypeStruct((T, D), F32), jax.ShapeDtypeStruct((T, 2 * W), BF16)],
        compiler_params=_cparams(("arbitrary",)),
    )(o_mla, o_swa, g_mla, g_swa, w_o, h1)


def _out_bwd(dh2, o_mla, o_swa, g_mla, g_swa, w_o):
    T, D = dh2.shape
    W = o_mla.shape[1]
    tm = _pick(T, (1024, 512, 256, 128))

    def body(dh_ref, om_ref, os_ref, gm_ref, gs_ref, wo_ref, dom_ref, dos_ref, dgm_ref, dgs_ref):
        doc = _dot_nt(dh_ref[...].astype(BF16), wo_ref[...])
        om, osw = om_ref[...], os_ref[...]
        _, ra = _rms_fwd(om, gm_ref[...])
        _, rb = _rms_fwd(osw, gs_ref[...])
        da, dga = _rms_bwd(om, gm_ref[...], ra, doc[:, :W])
        db, dgb = _rms_bwd(osw, gs_ref[...], rb, doc[:, W:])
        dom_ref[...] = da
        dos_ref[...] = db

        @pl.when(pl.program_id(0) == 0)
        def _():
            dgm_ref[...] = jnp.zeros_like(dgm_ref)
            dgs_ref[...] = jnp.zeros_like(dgs_ref)

        dgm_ref[...] += dga
        dgs_ref[...] += dgb

    return pl.pallas_call(
        body, name="out_bwd", grid=(T // tm,),
        in_specs=[_row_spec(tm, D), _row_spec(tm, W), _row_spec(tm, W), _const_spec((1, W)), _const_spec((1, W)),
                  _const_spec((2 * W, D))],
        out_specs=[_row_spec(tm, W), _row_spec(tm, W), _acc_spec((1, W)), _acc_spec((1, W))],
        out_shape=[jax.ShapeDtypeStruct((T, W), F32), jax.ShapeDtypeStruct((T, W), F32),
                   jax.ShapeDtypeStruct((1, W), F32), jax.ShapeDtypeStruct((1, W), F32)],
        compiler_params=_cparams(("arbitrary",)),
    )(dh2, o_mla, o_swa, g_mla, g_swa, w_o)


def _half_mask(shape, half):
    lane = lax.broadcasted_iota(jnp.int32, shape, len(shape) - 1)
    return (lane < 64) if half == 0 else (lane >= 64)


def _mla_fwd(q, k, v, B, S, exchange=None):
    T = B * S
    tq = _pick(S, MLA_Q_TILES)
    nq = S // tq
    pairs = MLA_HEADS // 2

    def body(q_ref, k_ref, v_ref, o_ref, lse_ref):
        causal = lax.broadcasted_iota(jnp.int32, (tq, tq), 1) <= lax.broadcasted_iota(jnp.int32, (tq, tq), 0)
        low = _half_mask((tq, 2 * V_HEAD), 0)
        for qi in range(nq):
            rows = slice(qi * tq, (qi + 1) * tq)
            past = slice(0, qi * tq)
            outs, lses = [], []
            for half in range(2):
                lanes = slice(half * HEAD_BLOCK, (half + 1) * HEAD_BLOCK)
                qv = q_ref[rows, lanes]
                s_d = jnp.where(causal, _dot_nt(qv, k_ref[rows, lanes]) * MLA_SCALE_LOG2, NEG)
                m = jnp.max(s_d, axis=-1, keepdims=True)
                if qi:
                    s_p = _dot_nt(qv, k_ref[past, lanes]) * MLA_SCALE_LOG2
                    m = jnp.maximum(m, jnp.max(s_p, axis=-1, keepdims=True))
                p_d = jnp.exp2(s_d - m)
                l = jnp.sum(p_d, axis=-1, keepdims=True)
                acc = _dot(p_d.astype(BF16), v_ref[rows, :])
                if qi:
                    p_p = jnp.exp2(s_p - m)
                    l = l + jnp.sum(p_p, axis=-1, keepdims=True)
                    acc = acc + _dot(p_p.astype(BF16), v_ref[past, :])
                outs.append(acc * (1.0 / l))
                lses.append(jnp.broadcast_to(m + jnp.log2(l), (tq, 2 * V_HEAD)))
            o_ref[rows, :] = jnp.where(low, outs[0], outs[1])
            lse_ref[rows, :] = jnp.where(low, lses[0], lses[1])

    blk2 = pl.BlockSpec((S, 2 * HEAD_BLOCK), lambda b, j: (b, j))
    blk1 = pl.BlockSpec((S, 2 * V_HEAD), lambda b, j: (b, j))
    return _pcall(
        body, name="mla_fwd", grid=(B, pairs),
        in_specs=[blk2, blk2, blk1], out_specs=[blk1, blk1],
        out_shape=[jax.ShapeDtypeStruct((T, pairs * 2 * V_HEAD), F32)] * 2,
        args=(q, k, v), exchange=exchange)


def _mla_bwd(q, k, v, do, o, lse, B, S, exchange=None):
    T = B * S
    tq = _pick(S, MLA_Q_TILES)
    nq = S // tq
    pairs = MLA_HEADS // 2

    def body(q_ref, k_ref, v_ref, do_ref, o_ref, lse_ref, dq_ref, dk_ref, dv_ref, dkt_acc, dvt_acc):
        causal = lax.broadcasted_iota(jnp.int32, (tq, tq), 1) <= lax.broadcasted_iota(jnp.int32, (tq, tq), 0)
        for half in range(2):
            lanes = slice(half * HEAD_BLOCK, (half + 1) * HEAD_BLOCK)
            col = half * V_HEAD
            own = _half_mask((tq, 2 * V_HEAD), half)
            for qi in reversed(range(nq)):
                rows = slice(qi * tq, (qi + 1) * tq)
                past = slice(0, qi * tq)
                first = qi == nq - 1
                qv = q_ref[rows, lanes]
                dov = jnp.where(own, do_ref[rows, :], 0.0)
                dsum = jnp.sum(dov * o_ref[rows, :], axis=-1, keepdims=True)
                dob = dov.astype(BF16)
                lse_col = lse_ref[rows, col:col + 1]

                def grads(keys, mask):
                    kv, vv = k_ref[keys, lanes], v_ref[keys, :]
                    p = jnp.exp2(_dot_nt(qv, kv) * MLA_SCALE_LOG2 - lse_col)
                    if mask is not None:
                        p = jnp.where(mask, p, 0.0)
                    ds = (p * (_dot_nt(dob, vv) - dsum) * MLA_SCALE).astype(BF16)
                    dkt, dvt = _dot_tn(qv, ds), _dot_tn(dob, p.astype(BF16))
                    if first:
                        dkt_acc[:, keys] = dkt
                    else:
                        dkt_acc[:, keys] += dkt
                    if first and half == 0:
                        dvt_acc[:, keys] = dvt
                    else:
                        dvt_acc[:, keys] += dvt
                    return _dot(ds, kv)

                dq = grads(rows, causal)
                if qi:
                    dq = dq + grads(past, None)
                dq_ref[rows, lanes] = dq
            dk_ref[:, lanes] = dkt_acc[...].T
        dv_ref[...] = dvt_acc[...].T.astype(BF16)

    blk2 = pl.BlockSpec((S, 2 * HEAD_BLOCK), lambda b, j: (b, j))
    blk1 = pl.BlockSpec((S, 2 * V_HEAD), lambda b, j: (b, j))
    return _pcall(
        body, name="mla_bwd", grid=(B, pairs),
        in_specs=[blk2, blk2, blk1, blk1, blk1, blk1], out_specs=[blk2, blk2, blk1],
        out_shape=[jax.ShapeDtypeStruct((T, MLA_HEADS * HEAD_BLOCK), F32), jax.ShapeDtypeStruct((T, MLA_HEADS * HEAD_BLOCK), F32),
                   jax.ShapeDtypeStruct((T, pairs * 2 * V_HEAD), BF16)],
        scratch_shapes=[pltpu.VMEM((HEAD_BLOCK, S), F32), pltpu.VMEM((2 * V_HEAD, S), F32)],
        args=(q, k, v, do, o, lse), exchange=exchange)


SWA_GROUP = SWA_HEADS // SWA_KV_HEADS
SWA_ROWS = SWA_GROUP * WINDOW


SWA_CHUNK = 8


def _bdot(a, b, ca, cb):
    return lax.dot_general(a, b, (((ca,), (cb,)), ((0,), (0,))), preferred_element_type=F32)


def _swa_chunks(nb):
    cb = next(c for c in (SWA_CHUNK, 4, 2, 1) if nb % c == 0)
    return cb, [(n0, slice(n0 * WINDOW, (n0 + cb) * WINDOW)) for n0 in range(0, nb, cb)]


def _swa_keys(ref, n0, cb):
    W = WINDOW
    cur = ref[n0 * W:(n0 + cb) * W, :]
    if n0:
        prev = ref[(n0 - 1) * W:(n0 + cb - 1) * W, :]
    else:
        prev = jnp.concatenate([cur[:W], cur[:(cb - 1) * W]], axis=0) if cb > 1 else cur
    return jnp.concatenate([prev.reshape(cb, W, W), cur.reshape(cb, W, W)], axis=1)


def _swa_stack(ref, rows, cb):
    return jnp.concatenate([ref[rows, j * WINDOW:(j + 1) * WINDOW].reshape(cb, WINDOW, WINDOW) for j in range(SWA_GROUP)], axis=1)


def _swa_unstack(low, high, cb):
    sel = _half_mask((cb, WINDOW, WINDOW), 0)
    return jnp.concatenate([jnp.where(sel, low[:, j * WINDOW:(j + 1) * WINDOW], high[:, j * WINDOW:(j + 1) * WINDOW])
                            .reshape(cb * WINDOW, WINDOW) for j in range(SWA_GROUP)], axis=1)


def _swa_sink_col(sink_ref, g):
    return jnp.concatenate([jnp.broadcast_to(sink_ref[g * SWA_GROUP + j:g * SWA_GROUP + j + 1, 0:1], (WINDOW, 1))
                            for j in range(SWA_GROUP)], axis=0)


def _swa_probs(qm, kcat, bias_g, sink, first_chunk):
    shape = (qm.shape[0], SWA_ROWS, 2 * WINDOW)
    qi = lax.broadcasted_iota(jnp.int32, shape, 1) % WINDOW
    col = lax.broadcasted_iota(jnp.int32, shape, 2)
    valid = (col > qi) & (col <= qi + WINDOW)
    if first_chunk:
        valid = valid & ((col >= WINDOW) | (lax.broadcasted_iota(jnp.int32, shape, 0) > 0))
    s = jnp.where(valid, _bdot(qm, kcat, 2, 2) * SWA_SCALE + bias_g, NEG)
    m = jnp.maximum(jnp.max(s, axis=-1, keepdims=True), sink)
    e = jnp.exp(s - m)
    e_s = jnp.exp(sink - m)
    inv = 1.0 / (jnp.sum(e, axis=-1, keepdims=True) + e_s)
    return e * inv, e_s * inv


def _swa_fwd(q, k, v, bias, sinks, B, S):
    T = B * S
    W = WINDOW
    nb = S // W

    cb, chunks = _swa_chunks(nb)

    def body(q_ref, k_ref, v_ref, bias_ref, sink_ref, o_ref):
        sink_cols = [_swa_sink_col(sink_ref, g) for g in range(SWA_KV_HEADS)]
        for n0, rows in chunks:
            kcat, vcat = _swa_keys(k_ref, n0, cb), _swa_keys(v_ref, n0, cb)
            qcat = _swa_stack(q_ref, rows, cb)
            outs = []
            for g in range(SWA_KV_HEADS):
                qm = jnp.where(_half_mask(qcat.shape, g), qcat, jnp.zeros_like(qcat))
                p, _ = _swa_probs(qm, kcat, bias_ref[g], sink_cols[g], n0 == 0)
                outs.append(_bdot(p.astype(BF16), vcat, 2, 1))
            o_ref[rows, :] = _swa_unstack(outs[0], outs[1], cb)

    SQ, SK = SWA_HEADS * SWA_HEAD_DIM, SWA_KV_HEADS * SWA_HEAD_DIM
    return pl.pallas_call(
        body, name="swa_fwd", grid=(B,),
        in_specs=[pl.BlockSpec((S, SQ), lambda b: (b, 0)), pl.BlockSpec((S, SK), lambda b: (b, 0)),
                  pl.BlockSpec((S, SK), lambda b: (b, 0)), _const_spec((SWA_KV_HEADS, SWA_ROWS, 2 * W)),
                  _const_spec((SWA_HEADS, 128))],
        out_specs=pl.BlockSpec((S, SQ), lambda b: (b, 0)),
        out_shape=jax.ShapeDtypeStruct((T, SQ), F32),
        compiler_params=_cparams(("arbitrary",)),
    )(q, k, v, bias, sinks)


def _swa_bwd(q, k, v, do, bias, sinks, B, S):
    T = B * S
    W = WINDOW
    nb = S // W
    cb, chunks = _swa_chunks(nb)

    def body(q_ref, k_ref, v_ref, do_ref, bias_ref, sink_ref, dq_ref, dk_ref, dv_ref, dbias_ref, dsink_ref, dkt_acc, dvt_acc):
        @pl.when(pl.program_id(0) == 0)
        def _():
            dbias_ref[...] = jnp.zeros_like(dbias_ref)
            dsink_ref[...] = jnp.zeros_like(dsink_ref)

        dkt_acc[...] = jnp.zeros_like(dkt_acc)
        dvt_acc[...] = jnp.zeros_like(dvt_acc)
        sink_cols = [_swa_sink_col(sink_ref, g) for g in range(SWA_KV_HEADS)]
        for n0, rows in chunks:
            kcat, vcat = _swa_keys(k_ref, n0, cb), _swa_keys(v_ref, n0, cb)
            qcat = _swa_stack(q_ref, rows, cb)
            docat = _swa_stack(do_ref, rows, cb)
            dkt = jnp.zeros((cb, W, 2 * W), F32)
            dvt = jnp.zeros((cb, W, 2 * W), F32)
            dqs = []
            for g in range(SWA_KV_HEADS):
                own = _half_mask(qcat.shape, g)
                qm = jnp.where(own, qcat, jnp.zeros_like(qcat))
                dom = jnp.where(own, docat, 0.0).astype(BF16)
                p, p_s = _swa_probs(qm, kcat, bias_ref[g], sink_cols[g], n0 == 0)
                dp = _bdot(dom, vcat, 2, 2)
                dsum = jnp.sum(p * dp, axis=-1, keepdims=True)
                ds = p * (dp - dsum)
                dbias_ref[g] += jnp.sum(ds, axis=0)
                dsink_ref[g] += jnp.broadcast_to(-jnp.sum(p_s * dsum, axis=0), (SWA_ROWS, 128))
                dsb = (ds * SWA_SCALE).astype(BF16)
                dqs.append(_bdot(dsb, kcat, 2, 1))
                dkt = dkt + _bdot(qm, dsb, 1, 1)
                dvt = dvt + _bdot(dom, p.astype(BF16), 1, 1)
            dq_ref[rows, :] = _swa_unstack(dqs[0], dqs[1], cb).astype(BF16)
            for i in range(cb):
                n = n0 + i
                if n:
                    cols = slice((n - 1) * W, (n + 1) * W)
                    dkt_acc[:, cols] += dkt[i]
                    dvt_acc[:, cols] += dvt[i]
                else:
                    dkt_acc[:, :W] += dkt[i][:, W:]
                    dvt_acc[:, :W] += dvt[i][:, W:]
        dk_ref[...] = dkt_acc[...].T.astype(BF16)
        dv_ref[...] = dvt_acc[...].T.astype(BF16)

    SQ, SK = SWA_HEADS * SWA_HEAD_DIM, SWA_KV_HEADS * SWA_HEAD_DIM
    row = lambda w: pl.BlockSpec((S, w), lambda b: (b, 0))
    return pl.pallas_call(
        body, name="swa_bwd", grid=(B,),
        in_specs=[row(SQ), row(SK), row(SK), row(SQ), _const_spec((SWA_KV_HEADS, SWA_ROWS, 2 * W)), _const_spec((SWA_HEADS, 128))],
        out_specs=[row(SQ), row(SK), row(SK), _acc_spec((SWA_KV_HEADS, SWA_ROWS, 2 * W)), _acc_spec((SWA_KV_HEADS, SWA_ROWS, 128))],
        out_shape=[jax.ShapeDtypeStruct((T, SQ), BF16), jax.ShapeDtypeStruct((T, SK), BF16), jax.ShapeDtypeStruct((T, SK), BF16),
                   jax.ShapeDtypeStruct((SWA_KV_HEADS, SWA_ROWS, 2 * W), F32), jax.ShapeDtypeStruct((SWA_KV_HEADS, SWA_ROWS, 128), F32)],
        scratch_shapes=[pltpu.VMEM((SK, S), F32), pltpu.VMEM((SK, S), F32)],
        compiler_params=_cparams(("arbitrary",)),
    )(q, k, v, do, bias, sinks)


def _bias_expand(rel_bias, bucket):
    W = WINDOW

    def body(rb_ref, bucket_ref, out_ref):
        bk = bucket_ref[...]
        for h in range(SWA_HEADS):
            def add(b, acc):
                return jnp.where(bk == b, rb_ref[b, h], acc)

            out_ref[h] = lax.fori_loop(0, REL_BUCKETS, add, jnp.zeros((W, 2 * W), F32))

    return pl.pallas_call(
        body, name="bias_expand",
        in_specs=[pl.BlockSpec(memory_space=pltpu.SMEM), pl.BlockSpec(memory_space=pltpu.VMEM)],
        out_specs=pl.BlockSpec(memory_space=pltpu.VMEM),
        out_shape=jax.ShapeDtypeStruct((SWA_HEADS, W, 2 * W), F32),
        compiler_params=_cparams(),
    )(rel_bias, bucket)


def _bias_reduce(dbias, dsink_rows, bucket):
    def body(db_ref, ds_ref, bucket_ref, out_ref, sink_out_ref, rows_ref):
        sink_lane = lax.broadcasted_iota(jnp.int32, (1, 128), 1)
        sink_acc = jnp.zeros((1, 128), F32)
        for h in range(SWA_HEADS):
            sink_acc = jnp.where(sink_lane == h, jnp.sum(ds_ref[h], axis=0, keepdims=True), sink_acc)
        sink_out_ref[...] = sink_acc
        bk = bucket_ref[...]
        for h in range(SWA_HEADS):
            dbh = db_ref[h]

            def add(b, carry):
                rows_ref[pl.ds(h * REL_BUCKETS + b, 1), :] = jnp.sum(jnp.where(bk == b, dbh, 0.0), axis=0, keepdims=True)
                return carry

            lax.fori_loop(0, REL_BUCKETS, add, 0)
        totals = jnp.sum(rows_ref[...], axis=-1, keepdims=True)
        lane = lax.broadcasted_iota(jnp.int32, (REL_BUCKETS, 128), 1)
        acc = jnp.zeros((REL_BUCKETS, 128), F32)
        for h in range(SWA_HEADS):
            acc = jnp.where(lane == h, totals[h * REL_BUCKETS:(h + 1) * REL_BUCKETS], acc)
        out_ref[...] = acc

    return pl.pallas_call(
        body, name="bias_reduce",
        out_shape=[jax.ShapeDtypeStruct((REL_BUCKETS, 128), F32), jax.ShapeDtypeStruct((1, 128), F32)],
        scratch_shapes=[pltpu.VMEM((SWA_HEADS * REL_BUCKETS, 2 * WINDOW), F32)],
        compiler_params=_cparams(),
    )(dbias, dsink_rows, bucket)


def _t5_bucket(dist):
    n = jnp.maximum(dist, 0)
    max_exact = REL_BUCKETS // 2
    nf = jnp.maximum(n, 1).astype(F32)
    large = max_exact + (jnp.log(nf / max_exact) / math.log(REL_MAX_DIST / max_exact) * (REL_BUCKETS - max_exact)).astype(jnp.int32)
    large = jnp.minimum(large, REL_BUCKETS - 1)
    return jnp.where(n < max_exact, n, large)


def _pair_heads(a, axis, inverse=False):
    shp = a.shape
    a = a.reshape(shp[:axis] + ((4, 2, 64) if inverse else (2, 4, 64)) + shp[axis + 1:])
    a = jnp.swapaxes(a, axis, axis + 1)
    return a.reshape(shp)


def _owner_blocks(w32, w16):
    r, c = w32.shape[0] // N_DEV, w32.shape[1]
    return w32.reshape(N_DEV, r, c), w16.reshape(N_DEV, r, c)


_IN_WIDTHS = (Q_LORA, KV_LORA, QK_ROPE, SWA_HEADS * SWA_HEAD_DIM, SWA_KV_HEADS * SWA_HEAD_DIM, SWA_KV_HEADS * SWA_HEAD_DIM)


def _win_to_kernel(wt):
    parts, o = [], 0
    for wd in _IN_WIDTHS:
        parts.append(wt[o:o + wd])
        o += wd
    cq, ckv, kpe, qs, ks, vs = parts
    return jnp.concatenate([cq, ckv, _pair_heads(qs, 0), ks, vs, jnp.pad(kpe, ((64, 32), (0, 0)))], axis=0)


def _win_from_kernel(dwt):
    o = [0, Q_LORA, Q_LORA + KV_LORA, Q_LORA + KV_LORA + 512, Q_LORA + KV_LORA + 640, Q_LORA + KV_LORA + 768]
    cq, ckv, qs, ks, vs, kpe = [dwt[a:b] for a, b in zip(o, o[1:] + [dwt.shape[0]])]
    return jnp.concatenate([cq, ckv, kpe[64:96], _pair_heads(qs, 0, inverse=True), ks, vs], axis=0)


def _wq_to_kernel(wt):
    return jnp.pad(wt.reshape(MLA_HEADS, QK_NOPE + QK_ROPE, Q_LORA), ((0, 0), (0, 32), (0, 0))).reshape(MLA_HEADS * HEAD_BLOCK, Q_LORA)


def _wq_from_kernel(dwt):
    return dwt.reshape(MLA_HEADS, HEAD_BLOCK, Q_LORA)[:, :QK_NOPE + QK_ROPE].reshape(-1, Q_LORA)


def _wkv_to_kernel(wt):
    w3 = wt.reshape(MLA_HEADS, QK_NOPE + V_HEAD, KV_LORA)
    kpart = jnp.pad(w3[:, :QK_NOPE], ((0, 0), (0, 64), (0, 0))).reshape(MLA_HEADS * HEAD_BLOCK, KV_LORA)
    return jnp.concatenate([kpart, w3[:, QK_NOPE:].reshape(MLA_HEADS * V_HEAD, KV_LORA)], axis=0)


def _wkv_from_kernel(dwt):
    dk = dwt[:MLA_HEADS * HEAD_BLOCK].reshape(MLA_HEADS, HEAD_BLOCK, KV_LORA)[:, :QK_NOPE]
    dv = dwt[MLA_HEADS * HEAD_BLOCK:].reshape(MLA_HEADS, V_HEAD, KV_LORA)
    return jnp.concatenate([dk, dv], axis=1).reshape(-1, KV_LORA)


def _wo_to_kernel(w):
    return jnp.concatenate([w[:512], _pair_heads(w[512:], 0)], axis=0)


def _wo_from_kernel(dw):
    return jnp.concatenate([dw[:512], _pair_heads(dw[512:], 0, inverse=True)], axis=0)


def _kernel_weights(key, stored):
    if key in ("g1", "u1", "g2", "u2"):
        return {key + "t": stored}
    if key in ("d1", "d2"):
        return {key: stored}
    if key == "o":
        return {key: _wo_to_kernel(stored)}
    return {key + "t": _TO_KERNEL[key](stored)}


_SMALL = ("g_ffn1", "g_mix", "g_q_a", "g_kv_a", "attn_sinks", "rel_bias", "g_out_mla", "g_out_swa", "g_ffn2", "g_final")


class _Comm:
    def __init__(self, shards):
        self.shards = shards
        self.own32, self.recv16, self.pending, self.token = {}, {}, [], None

    def gather(self, keys):
        return [("gather", self.shards[k]) for k in keys]

    def gathered(self, keys, results, wts):
        for k, g in zip(keys, results):
            wts.update(_kernel_weights(k, g.reshape(N_DEV * g.shape[1], g.shape[2])))

    def scatter(self, keys, grads):
        jobs = []
        for k in keys:
            g32, g16 = (_FROM_KERNEL[k](g) if k in _FROM_KERNEL else g for g in grads[k])
            self.own32[k], blocks = _owner_blocks(g32, g16)
            jobs.append(("scatter", blocks))
        return jobs

    def scattered(self, keys, results):
        self.recv16.update(zip(keys, results))

    def scatter_start(self, keys, grads, name):
        handle, self.token = _scatter_start([blocks for _, blocks in self.scatter(keys, grads)], name + "_start")
        self.pending.append((keys, handle, name + "_wait"))

    def scatter_wait(self, after):
        keys, handle, name = self.pending.pop(0)
        self.scattered(keys, _scatter_wait(handle, after, name))
        return keys


def _step(x, target, gains, wts, comm):
    def jobs_in(keys):
        return comm.gather(keys) if comm else None

    def jobs_out(keys):
        return comm.scatter(keys, grads) if comm else None

    B, S, D = x.shape
    T = B * S
    x2 = x.reshape(T, D)
    t2 = target.reshape(T, D)

    pos = jnp.arange(S, dtype=F32)
    inv_freq = ROPE_THETA ** (-jnp.arange(0, QK_ROPE, 2, dtype=F32) / QK_ROPE)
    ang = pos[:, None] * inv_freq[None, :]
    cos, sin = jnp.cos(ang), jnp.sin(ang)
    ones, zeros = jnp.ones((S, 64), F32), jnp.zeros((S, 32), F32)
    rope_c = jnp.concatenate([ones, cos, cos, zeros], axis=1)
    rope_s = jnp.concatenate([0.0 * ones, -sin, sin, zeros], axis=1)

    qi = jnp.arange(WINDOW)[:, None]
    kj = jnp.arange(2 * WINDOW)[None, :]
    bucket = _t5_bucket(qi + WINDOW - kj).astype(jnp.int32)
    bias = _bias_expand(gains["rel_bias"], bucket).reshape(SWA_KV_HEADS, SWA_ROWS, 2 * WINDOW)
    sinks = jnp.broadcast_to(gains["attn_sinks"].reshape(SWA_HEADS, 1), (SWA_HEADS, 128))
    g_swa = _pair_heads(gains["g_out_swa"], 1)

    keys = ("in", "q", "kv", "o", "g2")
    (h1, n1, gate1, up1, act1), got = _ffn_fwd(x2, gains["g_ffn1"], wts["g1t"], wts["u1t"], wts["d1"], "ffn1_fwd", jobs_in(keys))
    if comm:
        comm.gathered(keys, got, wts)
    (u, c_q, c_kv, cqn, ckvn, q, k, v, qs, ks, vs) = _proj_fwd(
        h1, gains["g_mix"], wts["int"], gains["g_q_a"], wts["qt"], gains["g_kv_a"], wts["kvt"], rope_c, rope_s, S)
    keys = ("u2", "d2")
    (o_mla, lse), got = _mla_fwd(q, k, v, B, S, jobs_in(keys))
    if comm:
        comm.gathered(keys, got, wts)
    o_swa = _swa_fwd(qs, ks, vs, bias, sinks, B, S)
    h2, oc = _out_fwd(o_mla, o_swa, gains["g_out_mla"], g_swa, wts["o"], h1)
    (dh3, n2, gate2, up2, act2, loss, dg_final), _ = _ffn_fwd(
        h2, gains["g_ffn2"], wts["g2t"], wts["u2t"], wts["d2"], "ffn2_fwd", loss_head=(t2, gains["g_final"].reshape(1, D)))

    grads = {}
    (dh2, dg_ffn2, dgate2, dup2), _ = _ffn_bwd(h2, gains["g_ffn2"], gate2, up2, dh3, wts["g2t"], wts["u2t"], wts["d2"], "ffn2_bwd")
    grads["g2"], _ = _matmul_tn(dgate2, n2, "dw_gate2")
    grads["u2"], _ = _matmul_tn(dup2, n2, "dw_up2")
    grads["d2"], _ = _matmul_tn(act2, dh3, "dw_down2", 0.5)
    do_mla, do_swa, dg_mla, dg_swa = _out_bwd(dh2, o_mla, o_swa, gains["g_out_mla"], g_swa, wts["o"])
    grads["o"], _ = _matmul_tn(oc, dh2, "dw_o")
    keys = ("g2", "u2", "d2")
    (dq, dk, dv), got = _mla_bwd(q, k, v, do_mla, o_mla, lse, B, S, jobs_out(keys))
    if comm:
        comm.scattered(keys, got)
    dqs, dks, dvs, dbias, dsink = _swa_bwd(qs, ks, vs, do_swa, bias, sinks, B, S)
    dh1, dproj, dql, dkvc, dg_mix, dg_q, dg_kv = _proj_bwd(
        dq, dk, dv, dqs, dks, dvs, c_q, c_kv, h1, dh2, gains["g_mix"], wts["int"], gains["g_q_a"], wts["qt"],
        gains["g_kv_a"], wts["kvt"], rope_c, rope_s, S)
    grads["in"], _ = _matmul_tn(dproj, u, "dw_in")
    grads["q"], _ = _matmul_tn(dql, cqn, "dw_q")
    grads["kv"], _ = _matmul_tn(dkvc, ckvn, "dw_kv")
    grads["d1"], _ = _matmul_tn(act1, dh1, "dw_down1", 0.5)
    if comm:
        comm.scatter_start(("o", "in", "q", "kv", "d1"), grads, "grad_scatter_mid")
    (dx, dg_ffn1, dgate1, dup1), _ = _ffn_bwd(
        x2, gains["g_ffn1"], gate1, up1, dh1, wts["g1t"], wts["u1t"], wts["d1"], "ffn1_bwd", after=comm and comm.token)
    grads["g1"], _ = _matmul_tn(dgate1, n1, "dw_gate1")
    if comm:
        comm.scatter_start(("g1",), grads, "grad_scatter_gate1")
    grads["u1"], _ = _matmul_tn(dup1, n1, "dw_up1", after=comm and comm.token)
    if comm:
        comm.scatter_start(("u1",), grads, "grad_scatter_up1")

    d_rel_bias, d_sinks = _bias_reduce(dbias.reshape(SWA_HEADS, WINDOW, 2 * WINDOW), dsink.reshape(SWA_HEADS, WINDOW, 128), bucket)
    small = {
        "g_ffn1": dg_ffn1, "g_mix": dg_mix, "g_q_a": dg_q, "g_kv_a": dg_kv, "attn_sinks": d_sinks,
        "rel_bias": d_rel_bias, "g_out_mla": dg_mla, "g_out_swa": _pair_heads(dg_swa, 1, inverse=True),
        "g_ffn2": dg_ffn2, "g_final": dg_final, "loss": loss,
    }
    return dx.reshape(B, S, D), grads, small


_WEIGHTS = ("g_ffn1", "w_ffn1_gate", "w_ffn1_up", "w_ffn1_down", "g_mix", "w_in", "g_q_a", "w_q_b", "g_kv_a", "w_kv_b",
            "attn_sinks", "rel_bias", "g_out_mla", "g_out_swa", "w_o", "g_ffn2", "w_ffn2_gate", "w_ffn2_up", "w_ffn2_down",
            "g_final")
_BIG = (("w_ffn1_gate", "g1", True), ("w_ffn1_up", "u1", True), ("w_ffn1_down", "d1", False), ("w_in", "in", True),
        ("w_q_b", "q", True), ("w_kv_b", "kv", True), ("w_o", "o", False), ("w_ffn2_gate", "g2", True),
        ("w_ffn2_up", "u2", True), ("w_ffn2_down", "d2", False))
_TO_KERNEL = {"in": _win_to_kernel, "q": _wq_to_kernel, "kv": _wkv_to_kernel}
_FROM_KERNEL = {"in": _win_from_kernel, "q": _wq_from_kernel, "kv": _wkv_from_kernel, "o": _wo_from_kernel}


def kernel(x, g_ffn1, w_ffn1_gate, w_ffn1_up, w_ffn1_down, g_mix, w_in, g_q_a, w_q_b, g_kv_a, w_kv_b, attn_sinks, rel_bias, g_out_mla, g_out_swa, w_o, g_ffn2, w_ffn2_gate, w_ffn2_up, w_ffn2_down, g_final, loss_target, m_g_ffn1, m_w_ffn1_gate, m_w_ffn1_up, m_w_ffn1_down, m_g_mix, m_w_in, m_g_q_a, m_w_q_b, m_g_kv_a, m_w_kv_b, m_attn_sinks, m_rel_bias, m_g_out_mla, m_g_out_swa, m_w_o, m_g_ffn2, m_w_ffn2_gate, m_w_ffn2_up, m_w_ffn2_down, m_g_final, v_g_ffn1, v_w_ffn1_gate, v_w_ffn1_up, v_w_ffn1_down, v_g_mix, v_w_in, v_g_q_a, v_w_q_b, v_g_kv_a, v_w_kv_b, v_attn_sinks, v_rel_bias, v_g_out_mla, v_g_out_swa, v_w_o, v_g_ffn2, v_w_ffn2_gate, v_w_ffn2_up, v_w_ffn2_down, v_g_final):
    w = dict(zip(_WEIGHTS, (g_ffn1, w_ffn1_gate, w_ffn1_up, w_ffn1_down, g_mix, w_in, g_q_a, w_q_b, g_kv_a, w_kv_b, attn_sinks,
                            rel_bias, g_out_mla, g_out_swa, w_o, g_ffn2, w_ffn2_gate, w_ffn2_up, w_ffn2_down, g_final)))
    m = dict(zip(_WEIGHTS, (m_g_ffn1, m_w_ffn1_gate, m_w_ffn1_up, m_w_ffn1_down, m_g_mix, m_w_in, m_g_q_a, m_w_q_b, m_g_kv_a,
                            m_w_kv_b, m_attn_sinks, m_rel_bias, m_g_out_mla, m_g_out_swa, m_w_o, m_g_ffn2, m_w_ffn2_gate,
                            m_w_ffn2_up, m_w_ffn2_down, m_g_final)))
    v = dict(zip(_WEIGHTS, (v_g_ffn1, v_w_ffn1_gate, v_w_ffn1_up, v_w_ffn1_down, v_g_mix, v_w_in, v_g_q_a, v_w_q_b, v_g_kv_a,
                            v_w_kv_b, v_attn_sinks, v_rel_bias, v_g_out_mla, v_g_out_swa, v_w_o, v_g_ffn2, v_w_ffn2_gate,
                            v_w_ffn2_up, v_w_ffn2_down, v_g_final)))
    me = 4 * lax.axis_index("x") + 2 * lax.axis_index("y") + lax.axis_index("c")

    stored = lambda src, name, by_col: src[name][0].T if by_col else src[name][0]
    shards = dict(zip([key for _, key, _ in _BIG], _cast_bf16([stored(w, name, by_col) for name, _, by_col in _BIG])))
    comm = _Comm(shards)
    first = ("g1", "u1", "d1")
    wts = {}
    comm.gathered(first, _all_gather([shards[k] for k in first], "weights_all_gather"), wts)

    gains = {n: (w[n] if n == "rel_bias" else w[n].reshape(1, -1)) for n in _SMALL}
    grad_x, _, small = _step(x, loss_target, gains, wts, comm)

    small_gathered = _all_gather([small[n] for n in _SMALL] + [small["loss"]], "small_all_gather")

    out_g, out_d, out_m, out_v = {}, {}, {}, {}
    me1 = me.astype(jnp.int32).reshape(1)
    big = {key: (name, by_col) for name, key, by_col in _BIG}

    def update(key):
        name, by_col = big[key]
        updates = _adam_big(stored(w, name, by_col), stored(m, name, by_col), stored(v, name, by_col),
                            comm.own32[key], comm.recv16[key], me1, comm.token)
        out_g[name], out_d[name], out_m[name], out_v[name] = ((a.T if by_col else a)[None] for a in updates)
        return updates[0]

    done = [update(key) for key in ("g2", "u2", "d2")]
    for _ in range(2):
        done = [update(key) for key in comm.scatter_wait(after=done[-1])]
    as_2d = lambda src: [src[n] if src[n].ndim == 2 else src[n].reshape(1, -1) for n in _SMALL]
    loss, updates = _adam_small(as_2d(w), as_2d(m), as_2d(v), small_gathered[:-1], small_gathered[-1])
    for n, (g, d, mn, vn) in zip(_SMALL, updates):
        out_g[n], out_d[n], out_m[n], out_v[n] = (a.reshape(w[n].shape) for a in (g, d, mn, vn))
    for key in comm.scatter_wait(after=loss):
        update(key)

    return (loss[0, 0], grad_x, *[out_g[n] for n in _WEIGHTS], *[out_d[n] for n in _WEIGHTS],
            *[out_m[n] for n in _WEIGHTS], *[out_v[n] for n in _WEIGHTS])
```

```python
import functools
import math

import jax
import jax.numpy as jnp
from jax import lax
from jax.experimental import pallas as pl
from jax.experimental.pallas import tpu as pltpu

F32 = jnp.float32
BF16 = jnp.bfloat16
MESH = pl.DeviceIdType.MESH

EPS = 1e-6
NEG = -1e30
N_DEV = 8

MLA_HEADS = 8
Q_LORA = 256
KV_LORA = 128
QK_NOPE = 64
QK_ROPE = 32
V_HEAD = 64
ROPE_THETA = 10000.0
SWA_HEADS = 8
SWA_KV_HEADS = 2
SWA_HEAD_DIM = 64
WINDOW = 128
REL_BUCKETS = 32
REL_MAX_DIST = 128
HEAD_BLOCK = 128
MLA_Q_TILES = (512, 256, 128)
MLA_SCALE = (QK_NOPE + QK_ROPE) ** -0.5
MLA_SCALE_LOG2 = MLA_SCALE * math.log2(math.e)
SWA_SCALE = SWA_HEAD_DIM ** -0.5

ADAM_LR = 0.001
ADAM_B1 = 0.9
ADAM_B2 = 0.999
ADAM_EPS = 1e-08
ADAM_WD = 0.01
ADAM_STEP = 10

VMEM_LIMIT = 56 * 1024 * 1024


def _cparams(semantics=None):
    return pltpu.CompilerParams(dimension_semantics=semantics, vmem_limit_bytes=VMEM_LIMIT)


def _dot(a, b):
    return jnp.dot(a, b, preferred_element_type=F32)


def _dot_nt(a, b):
    return lax.dot_general(a, b, (((1,), (1,)), ((), ())), preferred_element_type=F32)


def _dot_tn(a, b):
    return lax.dot_general(a, b, (((0,), (0,)), ((), ())), preferred_element_type=F32)


def _rms_fwd(x, g):
    r = lax.rsqrt(jnp.mean(x * x, axis=-1, keepdims=True) + EPS)
    return x * r * g, r


def _rms_bwd(x, g, r, dy):
    xh = x * r
    dyg = dy * g
    dx = r * (dyg - xh * jnp.mean(dyg * xh, axis=-1, keepdims=True))
    return dx, jnp.sum(dy * xh, axis=0, keepdims=True)


def _swap16(x):
    n = x.shape[-1]
    up = pltpu.roll(x, n - 16, 1)
    down = pltpu.roll(x, 16, 1)
    lane = lax.broadcasted_iota(jnp.int32, x.shape, 1) % HEAD_BLOCK
    return jnp.where((lane >= 64) & (lane < 80), up, jnp.where((lane >= 80) & (lane < 96), down, 0.0))


def _const_spec(shape):
    return pl.BlockSpec(shape, lambda *_: (0,) * len(shape), pipeline_mode=pl.Buffered(1))


def _acc_spec(shape):
    return pl.BlockSpec(shape, lambda *_: (0,) * len(shape))


def _row_spec(tm, width):
    return pl.BlockSpec((tm, width), lambda i: (i, 0))


def _cast_bf16(arrays):
    n = len(arrays)

    def body(*refs):
        for src, dst in zip(refs[:n], refs[n:]):
            dst[...] = src[...].astype(BF16)

    return pl.pallas_call(
        body, name="cast_weights",
        out_shape=[jax.ShapeDtypeStruct(a.shape, BF16) for a in arrays],
        compiler_params=_cparams(),
    )(*arrays)


def _all_gather(arrays, name):
    n = len(arrays)

    def body(*refs):
        ins, outs = refs[:n], refs[n:2 * n]
        send_sems, recv_sems, local_sems = refs[2 * n:]
        x, y, c = lax.axis_index("x"), lax.axis_index("y"), lax.axis_index("c")
        me, sibling = (x, y, c), (x, y, 1 - c)
        chips = [(1 - x, y), (x, 1 - y), (1 - x, 1 - y)]

        def slot(a, dev):
            return outs[a].at[4 * dev[0] + 2 * dev[1] + dev[2]]

        def copy(a, k, block, to, src=None):
            return pltpu.make_async_remote_copy(
                src_ref=slot(a, block) if src is None else src, dst_ref=slot(a, block),
                send_sem=send_sems.at[a, k], recv_sem=recv_sems.at[a, k], device_id=to, device_id_type=MESH)

        mine = [pltpu.make_async_copy(ins[a], slot(a, me), local_sems.at[a]) for a in range(n)]
        for cp in mine:
            cp.start()
        first = []
        for a in range(n):
            first.append(copy(a, 0, me, sibling, src=ins[a]))
            first += [copy(a, 1 + j, me, (*chip, c), src=ins[a]) for j, chip in enumerate(chips)]
        for cp in first:
            cp.start()
        passed = []
        for j, chip in enumerate(chips):
            for a in range(n):
                copy(a, 1 + j, (*chip, c), me).wait_recv()
                cp = copy(a, 4 + j, (*chip, c), sibling)
                cp.start()
                passed.append(cp)
        for a in range(n):
            copy(a, 0, sibling, me).wait_recv()
            for j, chip in enumerate(chips):
                copy(a, 4 + j, (*chip, 1 - c), me).wait_recv()
        for cp in first + passed:
            cp.wait_send()
        for cp in mine:
            cp.wait()

    any_spec = pl.BlockSpec(memory_space=pl.ANY)
    return pl.pallas_call(
        body, name=name,
        out_shape=[jax.ShapeDtypeStruct((N_DEV,) + a.shape, a.dtype) for a in arrays],
        in_specs=[any_spec] * n, out_specs=[any_spec] * n,
        scratch_shapes=[pltpu.SemaphoreType.DMA((n, 7)), pltpu.SemaphoreType.DMA((n, 7)), pltpu.SemaphoreType.DMA((n,))],
    )(*arrays)


def _exchange_shapes(jobs):
    dsts = [jax.ShapeDtypeStruct(((N_DEV,) + a.shape) if kind == "gather" else a.shape, a.dtype) for kind, a in jobs]
    n = len(jobs)
    sems = [pltpu.SemaphoreType.DMA((n, N_DEV - 1)), pltpu.SemaphoreType.DMA((n, N_DEV - 1)), pltpu.SemaphoreType.DMA((n,))]
    return dsts, sems


def _exchange_copies(kinds, srcs, dsts, send_sems, recv_sems, local_sems):
    x, y, c = lax.axis_index("x"), lax.axis_index("y"), lax.axis_index("c")
    me = 4 * x + 2 * y + c
    remote, local = [], []
    for a, kind in enumerate(kinds):
        for k in range(1, N_DEV):
            peer = (1 - x if k & 4 else x, 1 - y if k & 2 else y, 1 - c if k & 1 else c)
            src = srcs[a] if kind == "gather" else srcs[a].at[4 * peer[0] + 2 * peer[1] + peer[2]]
            remote.append(pltpu.make_async_remote_copy(
                src_ref=src, dst_ref=dsts[a].at[me], send_sem=send_sems.at[a, k - 1], recv_sem=recv_sems.at[a, k - 1],
                device_id=peer, device_id_type=MESH))
        src = srcs[a] if kind == "gather" else srcs[a].at[me]
        local.append(pltpu.make_async_copy(src, dsts[a].at[me], local_sems.at[a]))
    return remote, local


def _scatter_copies(srcs, lands, send_sems, recv_sems):
    x, y, c = lax.axis_index("x"), lax.axis_index("y"), lax.axis_index("c")
    me = 4 * x + 2 * y + c
    copies = []
    for a, (src_ref, land_ref) in enumerate(zip(srcs, lands)):
        for k in range(1, N_DEV):
            peer = (1 - x if k & 4 else x, 1 - y if k & 2 else y, 1 - c if k & 1 else c)
            copies.append(pltpu.make_async_remote_copy(
                src_ref=src_ref.at[4 * peer[0] + 2 * peer[1] + peer[2]], dst_ref=land_ref.at[me],
                send_sem=send_sems.at[a * (N_DEV - 1) + k - 1], recv_sem=recv_sems.at[a * (N_DEV - 1) + k - 1],
                device_id=peer, device_id_type=MESH))
    return copies


_SPLIT_EFFECT = pltpu.SideEffectType.DATAFLOW_SIDE_EFFECTING


def _scatter_start(blocks, name):
    n = len(blocks)
    hbm = pl.BlockSpec(memory_space=pltpu.HBM)
    sem = pl.BlockSpec(memory_space=pltpu.SEMAPHORE)

    def body(*refs):
        srcs, lands, send_sems, recv_sems, token = refs[:n], refs[n:2 * n], refs[2 * n], refs[2 * n + 1], refs[-1]
        for cp in _scatter_copies(srcs, lands, send_sems, recv_sems):
            cp.start()
        token[...] = jnp.zeros_like(token)

    through = [pltpu.HBM(b.shape, b.dtype) for b in blocks]
    *handle, token = pl.pallas_call(
        body, name=name,
        out_shape=(pltpu.SemaphoreType.DMA((n * (N_DEV - 1),)), pltpu.SemaphoreType.DMA((n * (N_DEV - 1),)), *through, *through,
                   jax.ShapeDtypeStruct((8, 128), F32)),
        in_specs=(hbm,) * (2 * n), out_specs=(sem, sem) + (hbm,) * (2 * n) + (pl.BlockSpec(memory_space=pltpu.VMEM),),
        input_output_aliases={i: 2 + i for i in range(2 * n)},
        compiler_params=pltpu.CompilerParams(has_side_effects=_SPLIT_EFFECT),
    )(*[pltpu.with_memory_space_constraint(b, pltpu.HBM) for b in blocks],
      *[pltpu.with_memory_space_constraint(lax.empty(b.shape, b.dtype), pltpu.HBM) for b in blocks])
    return handle, token


def _scatter_wait(handle, after, name):
    hbm = pl.BlockSpec(memory_space=pltpu.HBM)
    sem = pl.BlockSpec(memory_space=pltpu.SEMAPHORE)
    send_sems, recv_sems, *through = handle
    n = len(through) // 2

    def body(*refs):
        for cp in _scatter_copies(refs[:n], refs[n:2 * n], refs[2 * n], refs[2 * n + 1]):
            cp.wait_send()
            cp.wait_recv()

    outs = pl.pallas_call(
        body, name=name, out_shape=tuple(pltpu.HBM(t.shape, t.dtype) for t in through),
        in_specs=(hbm,) * (2 * n) + (sem, sem, pl.BlockSpec(memory_space=pl.ANY)), out_specs=(hbm,) * (2 * n),
        input_output_aliases={i: i for i in range(2 * n)},
        compiler_params=pltpu.CompilerParams(has_side_effects=_SPLIT_EFFECT),
    )(*through, send_sems, recv_sems, after)
    return list(outs[n:])


def _pcall(body, *, name, grid, in_specs, out_specs, out_shape, args, scratch_shapes=(), exchange=None, after=None):
    semantics = ("arbitrary",) * len(grid)
    exchange = exchange or []
    extra = [] if after is None else [after]
    if not exchange and not extra:
        outs = pl.pallas_call(body, name=name, grid=grid, in_specs=list(in_specs), out_specs=list(out_specs),
                              out_shape=list(out_shape), scratch_shapes=list(scratch_shapes),
                              compiler_params=_cparams(semantics))(*args)
        return list(outs), []
    kinds = [kind for kind, _ in exchange]
    n_in, n_out, n_job = len(in_specs), len(out_specs), len(exchange)
    dst_shapes, sems = _exchange_shapes(exchange) if exchange else ([], [])

    def wrapped(*refs):
        ins, srcs = refs[:n_in], refs[n_in:n_in + n_job]
        o0 = n_in + n_job + len(extra)
        outs, dsts, rest = refs[o0:o0 + n_out], refs[o0 + n_out:o0 + n_out + n_job], refs[o0 + n_out + n_job:]
        if not exchange:
            body(*ins, *outs, *rest)
            return
        scratch, sem_refs = rest[:-3], rest[-3:]
        first = functools.reduce(jnp.logical_and, [pl.program_id(d) == 0 for d in range(len(grid))])
        last = functools.reduce(jnp.logical_and, [pl.program_id(d) == grid[d] - 1 for d in range(len(grid))])

        @pl.when(first)
        def _():
            remote, local = _exchange_copies(kinds, srcs, dsts, *sem_refs)
            for cp in local + remote:
                cp.start()

        body(*ins, *outs, *scratch)

        @pl.when(last)
        def _():
            remote, local = _exchange_copies(kinds, srcs, dsts, *sem_refs)
            for cp in remote + local:
                cp.wait()

    any_spec = pl.BlockSpec(memory_space=pl.ANY)
    outs = pl.pallas_call(
        wrapped, name=name, grid=grid, in_specs=list(in_specs) + [any_spec] * (n_job + len(extra)),
        out_specs=list(out_specs) + [any_spec] * n_job, out_shape=list(out_shape) + dst_shapes,
        scratch_shapes=list(scratch_shapes) + sems, compiler_params=_cparams(semantics),
    )(*args, *[a for _, a in exchange], *extra)
    return list(outs[:n_out]), list(outs[n_out:])


def _adam_math(w, g, m, v):
    m = ADAM_B1 * m + (1.0 - ADAM_B1) * g
    v = ADAM_B2 * v + (1.0 - ADAM_B2) * (g * g)
    m_hat = m / (1.0 - ADAM_B1 ** ADAM_STEP)
    v_hat = v / (1.0 - ADAM_B2 ** ADAM_STEP)
    delta = -ADAM_LR * (m_hat / (jnp.sqrt(v_hat) + ADAM_EPS) + ADAM_WD * w)
    return delta, m, v


def _adam_big(w, m, v, own32, recv16, me, after):
    r, c = w.shape
    tr = _pick(r, (176,))
    assert r % tr == 0

    def body(me_ref, w_ref, m_ref, v_ref, s_ref, r_ref, after_ref, g_out, d_out, m_out, v_out):
        g = s_ref[...]
        for k in range(1, N_DEV):
            g = g + r_ref[jnp.bitwise_xor(me_ref[0], k)].astype(F32)
        d, mn, vn = _adam_math(w_ref[...], g, m_ref[...], v_ref[...])
        g_out[...] = g
        d_out[...] = d
        m_out[...] = mn
        v_out[...] = vn

    blk = pl.BlockSpec((tr, c), lambda i, s: (i, 0))
    grid_spec = pltpu.PrefetchScalarGridSpec(
        num_scalar_prefetch=1, grid=(r // tr,),
        in_specs=[blk, blk, blk, pl.BlockSpec((None, tr, c), lambda i, s: (s[0], i, 0)),
                  pl.BlockSpec((N_DEV, tr, c), lambda i, s: (0, i, 0)), pl.BlockSpec(memory_space=pl.ANY)],
        out_specs=[blk, blk, blk, blk])
    return pl.pallas_call(
        body, name="adamw_big", grid_spec=grid_spec,
        out_shape=[jax.ShapeDtypeStruct((r, c), F32)] * 4,
        compiler_params=_cparams(("arbitrary",)),
    )(me, w, m, v, own32, recv16, after)


def _adam_small(ws, ms, vs, partials, loss_parts):
    n = len(ws)

    def body(*refs):
        w_refs, m_refs, v_refs, p_refs = (refs[i * n:(i + 1) * n] for i in range(4))
        l_ref, loss_out = refs[4 * n], refs[4 * n + 1]
        outs = refs[4 * n + 2:]

        def total(ref):
            s = ref[0]
            for dev in range(1, N_DEV):
                s = s + ref[dev]
            return s

        loss_out[...] = total(l_ref)
        for i in range(n):
            r, c = w_refs[i].shape
            g = total(p_refs[i])[:, :c]
            d, mn, vn = _adam_math(w_refs[i][...], g, m_refs[i][...], v_refs[i][...])
            for j, val in enumerate((g, d, mn, vn)):
                outs[4 * i + j][...] = val

    shapes = [jax.ShapeDtypeStruct((1, 128), F32)]
    for w in ws:
        shapes += [jax.ShapeDtypeStruct(w.shape, F32)] * 4
    outs = pl.pallas_call(body, name="adamw_small", out_shape=shapes, compiler_params=_cparams())(*ws, *ms, *vs, *partials, loss_parts)
    return outs[0], [outs[1 + 4 * i:5 + 4 * i] for i in range(n)]


def _pick(n, prefer):
    for t in prefer:
        if n % t == 0:
            return t
    return n


FFN_FWD_GROUP = 512


def _ffn_fwd(x, g, wg, wu, wd, name, exchange=None, loss_head=None):
    T, D = x.shape
    F = wg.shape[0]
    tm = _pick(T, (512, 256, 128))
    fc = _pick(F, (256, 128))

    def body(x_ref, g_ref, wg_ref, wu_ref, wd_ref, *rest):
        h_ref, n_ref, gate_ref, up_ref, act_ref = rest[-7:-2] if loss_head else rest
        xv = x_ref[...]
        n, _ = _rms_fwd(xv, g_ref[...])
        nb = n.astype(BF16)
        n_ref[...] = nb
        acc = jnp.zeros((tm, D), F32)
        for g0 in range(0, F, FFN_FWD_GROUP):
            g1 = min(g0 + FFN_FWD_GROUP, F)
            for f0 in range(g0, g1, fc):
                f1 = min(f0 + fc, g1)
                gate = _dot_nt(nb, wg_ref[f0:f1, :])
                up = _dot_nt(nb, wu_ref[f0:f1, :])
                gate_ref[:, f0:f1] = gate.astype(BF16)
                up_ref[:, f0:f1] = up.astype(BF16)
                act_ref[:, f0:f1] = (gate * (1.0 / (1.0 + jnp.exp(-gate))) * up).astype(BF16)
            acc = acc + _dot(act_ref[:, g0:g1], wd_ref[g0:g1, :])
        hv = xv + 0.5 * acc
        if not loss_head:
            h_ref[...] = hv
            return
        t_ref, gf_ref, loss_ref, dgf_ref = rest[0], rest[1], rest[-2], rest[-1]
        gf = gf_ref[...]
        y, r = _rms_fwd(hv, gf)
        diff = y - t_ref[...]
        dh, dgp = _rms_bwd(hv, gf, r, diff * (1.0 / D))
        h_ref[...] = dh
        part = 0.5 * jnp.sum(jnp.sum(diff * diff, axis=-1, keepdims=True) * (1.0 / D), axis=0, keepdims=True)

        @pl.when(pl.program_id(0) == 0)
        def _():
            loss_ref[...] = jnp.zeros_like(loss_ref)
            dgf_ref[...] = jnp.zeros_like(dgf_ref)

        loss_ref[...] += jnp.broadcast_to(part, loss_ref.shape)
        dgf_ref[...] += dgp

    in_specs = [_row_spec(tm, D), _const_spec((1, D)), _const_spec((F, D)), _const_spec((F, D)), _const_spec((F, D))]
    out_specs = [_row_spec(tm, D), _row_spec(tm, D), _row_spec(tm, F), _row_spec(tm, F), _row_spec(tm, F)]
    out_shape = [jax.ShapeDtypeStruct((T, D), F32), jax.ShapeDtypeStruct((T, D), BF16),
                 jax.ShapeDtypeStruct((T, F), BF16), jax.ShapeDtypeStruct((T, F), BF16), jax.ShapeDtypeStruct((T, F), BF16)]
    args = (x, g, wg, wu, wd)
    if loss_head:
        in_specs += [_row_spec(tm, D), _const_spec((1, D))]
        out_specs += [_acc_spec((1, 128)), _acc_spec((1, D))]
        out_shape += [jax.ShapeDtypeStruct((1, 128), F32), jax.ShapeDtypeStruct((1, D), F32)]
        args += tuple(loss_head)
    return _pcall(body, name=name, grid=(T // tm,), in_specs=in_specs, out_specs=out_specs, out_shape=out_shape, args=args,
                  exchange=exchange)


FFN_BWD_CHUNK = 1408
FFN_BWD_GROUP = 1408


def _ffn_bwd(x, g, gate, up, dh, wgt, wut, wdt, name, exchange=None, after=None):
    T, D = x.shape
    F = wgt.shape[0]
    tm = _pick(T, (512, 256, 128))
    halves = 2
    Fh = F // halves
    assert Fh * halves == F and Fh % 128 == 0
    groups = [(g0, min(g0 + FFN_BWD_GROUP, Fh)) for g0 in range(0, Fh, FFN_BWD_GROUP)]

    def body(x_ref, g_ref, gate_ref, up_ref, dh_ref, wg_ref, wu_ref, wd_ref, dx_ref, dg_ref, dgate_ref, dup_ref, dn_ref):
        i, h = pl.program_id(0), pl.program_id(1)
        dhv = dh_ref[...]
        dho = (0.5 * dhv).astype(BF16)
        dn = jnp.zeros((tm, D), F32)
        for g0, g1 in groups:
            for f0 in range(g0, g1, FFN_BWD_CHUNK):
                f1 = min(f0 + FFN_BWD_CHUNK, g1)
                d_act = _dot_nt(dho, wd_ref[f0:f1, :])
                gt = gate_ref[:, f0:f1].astype(F32)
                u = up_ref[:, f0:f1].astype(F32)
                sg = 1.0 / (1.0 + jnp.exp(-gt))
                silu = gt * sg
                dup_ref[:, f0:f1] = (d_act * silu).astype(BF16)
                dgate_ref[:, f0:f1] = (d_act * u * (sg + silu * (1.0 - sg))).astype(BF16)
            dn = dn + _dot(dgate_ref[:, g0:g1], wg_ref[g0:g1, :]) + _dot(dup_ref[:, g0:g1], wu_ref[g0:g1, :])

        @pl.when(h == 0)
        def _():
            dn_ref[...] = dn

        @pl.when(h > 0)
        def _():
            dn_ref[...] += dn

        @pl.when((i == 0) & (h == 0))
        def _():
            dg_ref[...] = jnp.zeros_like(dg_ref)

        @pl.when(h == halves - 1)
        def _():
            xv, gv = x_ref[...], g_ref[...]
            _, r = _rms_fwd(xv, gv)
            dxn, dgp = _rms_bwd(xv, gv, r, dn_ref[...])
            dx_ref[...] = dhv + dxn
            dg_ref[...] += dgp

    tile = pl.BlockSpec((tm, D), lambda i, h: (i, 0))
    wide = pl.BlockSpec((tm, Fh), lambda i, h: (i, h))
    return _pcall(
        body, name=name, grid=(T // tm, halves),
        in_specs=[tile, _const_spec((1, D)), wide, wide, tile,
                  pl.BlockSpec((Fh, D), lambda i, h: (h, 0)), pl.BlockSpec((Fh, D), lambda i, h: (h, 0)),
                  pl.BlockSpec((Fh, D), lambda i, h: (h, 0))],
        out_specs=[tile, _acc_spec((1, D)), wide, wide],
        out_shape=[jax.ShapeDtypeStruct((T, D), F32), jax.ShapeDtypeStruct((1, D), F32),
                   jax.ShapeDtypeStruct((T, F), BF16), jax.ShapeDtypeStruct((T, F), BF16)],
        scratch_shapes=[pltpu.VMEM((tm, D), F32)],
        args=(x, g, gate, up, dh, wgt, wut, wdt), exchange=exchange, after=after)


def _matmul_tn(a, b, name, scale=None, exchange=None, after=None):
    T, K = a.shape
    N = b.shape[1]
    tk = _pick(K, (1024, 1408, 1280, 768, 512, 256, 128))
    tn = _pick(N, (1024, 1408, 1280, 768, 512, 256, 128))
    tt = _pick(T, (2048, 1024, 512, 256, 128))
    n_t = T // tt

    def body(a_ref, b_ref, o_ref, o16_ref):
        @pl.when(pl.program_id(2) == 0)
        def _():
            o_ref[...] = jnp.zeros_like(o_ref)

        o_ref[...] += _dot_tn(a_ref[...].astype(BF16), b_ref[...].astype(BF16))

        @pl.when(pl.program_id(2) == n_t - 1)
        def _():
            if scale is not None:
                o_ref[...] = o_ref[...] * scale
            o16_ref[...] = o_ref[...].astype(BF16)

    tile = pl.BlockSpec((tk, tn), lambda i, j, t: (i, j))
    outs, got = _pcall(
        body, name=name, grid=(K // tk, N // tn, n_t),
        in_specs=[pl.BlockSpec((tt, tk), lambda i, j, t: (t, i)), pl.BlockSpec((tt, tn), lambda i, j, t: (t, j))],
        out_specs=[tile, tile],
        out_shape=[jax.ShapeDtypeStruct((K, N), F32), jax.ShapeDtypeStruct((K, N), BF16)],
        args=(a, b), exchange=exchange, after=after)
    return tuple(outs), got


def _rope_spec(tm, S):
    per = S // tm
    return pl.BlockSpec((tm, HEAD_BLOCK), lambda i: (i % per, 0))


def _proj_fwd(h1, g_mix, w_in, g_q, wq, g_kv, wkv, rope_c, rope_s, S):
    T, D = h1.shape
    P = w_in.shape[0]
    tm = _pick(S, (512, 256, 128))
    QW = MLA_HEADS * HEAD_BLOCK
    VW = MLA_HEADS * V_HEAD
    SQ = SWA_HEADS * SWA_HEAD_DIM
    SK = SWA_KV_HEADS * SWA_HEAD_DIM
    o_cq, o_ckv, o_qs, o_ks, o_vs, o_kpe = 0, Q_LORA, Q_LORA + KV_LORA, Q_LORA + KV_LORA + SQ, Q_LORA + KV_LORA + SQ + SK, Q_LORA + KV_LORA + SQ + 2 * SK
    assert P == o_kpe + HEAD_BLOCK

    def body(h_ref, gm_ref, win_ref, gq_ref, wq_ref, gkv_ref, wkv_ref, c_ref, s_ref,
             u_ref, cq_ref, ckv_ref, cqn_ref, ckvn_ref, q_ref, k_ref, v_ref, qs_ref, ks_ref, vs_ref):
        u, _ = _rms_fwd(h_ref[...], gm_ref[...])
        ub = u.astype(BF16)
        u_ref[...] = ub
        proj = _dot_nt(ub, win_ref[...])
        c_q = proj[:, o_cq:o_ckv]
        c_kv = proj[:, o_ckv:o_qs]
        cq_ref[...] = c_q
        ckv_ref[...] = c_kv
        qs_ref[...] = proj[:, o_qs:o_ks].astype(BF16)
        ks_ref[...] = proj[:, o_ks:o_vs].astype(BF16)
        vs_ref[...] = proj[:, o_vs:o_kpe].astype(BF16)
        cb, sb = c_ref[...], s_ref[...]
        kpe = proj[:, o_kpe:P]
        kpe = kpe * cb + _swap16(kpe) * sb
        cqn, _ = _rms_fwd(c_q, gq_ref[...])
        cqn = cqn.astype(BF16)
        cqn_ref[...] = cqn
        q = _dot_nt(cqn, wq_ref[...])
        q = q * jnp.tile(cb, (1, MLA_HEADS)) + _swap16(q) * jnp.tile(sb, (1, MLA_HEADS))
        q_ref[...] = q.astype(BF16)
        ckvn, _ = _rms_fwd(c_kv, gkv_ref[...])
        ckvn = ckvn.astype(BF16)
        ckvn_ref[...] = ckvn
        kv = _dot_nt(ckvn, wkv_ref[...])
        k_ref[...] = (kv[:, :QW] + jnp.tile(kpe, (1, MLA_HEADS))).astype(BF16)
        v_ref[...] = kv[:, QW:].astype(BF16)

    widths = [(D, BF16), (Q_LORA, F32), (KV_LORA, F32), (Q_LORA, BF16), (KV_LORA, BF16), (QW, BF16), (QW, BF16), (VW, BF16),
              (SQ, BF16), (SK, BF16), (SK, BF16)]
    return pl.pallas_call(
        body, name="proj_fwd", grid=(T // tm,),
        in_specs=[_row_spec(tm, D), _const_spec((1, D)), _const_spec((P, D)), _const_spec((1, Q_LORA)),
                  _const_spec((QW, Q_LORA)), _const_spec((1, KV_LORA)), _const_spec((QW + VW, KV_LORA)),
                  _rope_spec(tm, S), _rope_spec(tm, S)],
        out_specs=[_row_spec(tm, w) for w, _ in widths],
        out_shape=[jax.ShapeDtypeStruct((T, w), dt) for w, dt in widths],
        compiler_params=_cparams(("arbitrary",)),
    )(h1, g_mix, w_in, g_q, wq, g_kv, wkv, rope_c, rope_s)


def _proj_bwd(dq, dk, dv, dqs, dks, dvs, c_q, c_kv, h1, dh2, g_mix, w_in, g_q, wq, g_kv, wkv, rope_c, rope_s, S):
    T, D = h1.shape
    P = w_in.shape[0]
    tm = _pick(S, (512, 256, 128))
    QW = MLA_HEADS * HEAD_BLOCK
    VW = MLA_HEADS * V_HEAD
    SQ = SWA_HEADS * SWA_HEAD_DIM
    SK = SWA_KV_HEADS * SWA_HEAD_DIM

    def body(dq_ref, dk_ref, dv_ref, dqs_ref, dks_ref, dvs_ref, cq_ref, ckv_ref, h_ref, dh2_ref, gm_ref, win_ref, gq_ref,
             wq_ref, gkv_ref, wkv_ref, c_ref, s_ref, dh1_ref, dproj_ref, dql_ref, dkv_ref, dgm_ref, dgq_ref, dgkv_ref):
        cb, sb = c_ref[...], s_ref[...]
        dqv = dq_ref[...]
        dql = dqv * jnp.tile(cb, (1, MLA_HEADS)) + _swap16(dqv * jnp.tile(sb, (1, MLA_HEADS)))
        dql = dql.astype(BF16)
        dql_ref[...] = dql
        c_q = cq_ref[...]
        _, rq = _rms_fwd(c_q, gq_ref[...])
        d_cq, dgq = _rms_bwd(c_q, gq_ref[...], rq, _dot(dql, wq_ref[...]))

        dkv_all = dk_ref[...]
        dkpe = dkv_all[:, 0:HEAD_BLOCK]
        for h in range(1, MLA_HEADS):
            dkpe = dkpe + dkv_all[:, h * HEAD_BLOCK:(h + 1) * HEAD_BLOCK]
        lane = lax.broadcasted_iota(jnp.int32, dkpe.shape, 1)
        dkpe = jnp.where((lane >= 64) & (lane < 96), dkpe, 0.0)
        dkpe = dkpe * cb + _swap16(dkpe * sb)
        dkvc = jnp.concatenate([dkv_all.astype(BF16), dv_ref[...].astype(BF16)], axis=1)
        dkv_ref[...] = dkvc
        c_kv = ckv_ref[...]
        _, rkv = _rms_fwd(c_kv, gkv_ref[...])
        d_ckv, dgkv = _rms_bwd(c_kv, gkv_ref[...], rkv, _dot(dkvc, wkv_ref[...]))

        dproj = jnp.concatenate([d_cq.astype(BF16), d_ckv.astype(BF16), dqs_ref[...].astype(BF16),
                                 dks_ref[...].astype(BF16), dvs_ref[...].astype(BF16), dkpe.astype(BF16)], axis=1)
        dproj_ref[...] = dproj
        hv = h_ref[...]
        _, rm = _rms_fwd(hv, gm_ref[...])
        dxn, dgm = _rms_bwd(hv, gm_ref[...], rm, _dot(dproj, win_ref[...]))
        dh1_ref[...] = dh2_ref[...] + dxn

        @pl.when(pl.program_id(0) == 0)
        def _():
            dgm_ref[...] = jnp.zeros_like(dgm_ref)
            dgq_ref[...] = jnp.zeros_like(dgq_ref)
            dgkv_ref[...] = jnp.zeros_like(dgkv_ref)

        dgm_ref[...] += dgm
        dgq_ref[...] += dgq
        dgkv_ref[...] += dgkv

    return pl.pallas_call(
        body, name="proj_bwd", grid=(T // tm,),
        in_specs=[_row_spec(tm, QW), _row_spec(tm, QW), _row_spec(tm, VW), _row_spec(tm, SQ), _row_spec(tm, SK), _row_spec(tm, SK),
                  _row_spec(tm, Q_LORA), _row_spec(tm, KV_LORA), _row_spec(tm, D), _row_spec(tm, D),
                  _const_spec((1, D)), _const_spec((P, D)), _const_spec((1, Q_LORA)), _const_spec((QW, Q_LORA)),
                  _const_spec((1, KV_LORA)), _const_spec((QW + VW, KV_LORA)), _rope_spec(tm, S), _rope_spec(tm, S)],
        out_specs=[_row_spec(tm, D), _row_spec(tm, P), _row_spec(tm, QW), _row_spec(tm, QW + VW),
                   _acc_spec((1, D)), _acc_spec((1, Q_LORA)), _acc_spec((1, KV_LORA))],
        out_shape=[jax.ShapeDtypeStruct((T, D), F32), jax.ShapeDtypeStruct((T, P), BF16), jax.ShapeDtypeStruct((T, QW), BF16),
                   jax.ShapeDtypeStruct((T, QW + VW), BF16), jax.ShapeDtypeStruct((1, D), F32),
                   jax.ShapeDtypeStruct((1, Q_LORA), F32), jax.ShapeDtypeStruct((1, KV_LORA), F32)],
        compiler_params=_cparams(("arbitrary",)),
    )(dq, dk, dv, dqs, dks, dvs, c_q, c_kv, h1, dh2, g_mix, w_in, g_q, wq, g_kv, wkv, rope_c, rope_s)


def _out_fwd(o_mla, o_swa, g_mla, g_swa, w_o, h1):
    T, D = h1.shape
    W = o_mla.shape[1]
    tm = _pick(T, (1024, 512, 256, 128))

    def body(om_ref, os_ref, gm_ref, gs_ref, wo_ref, h_ref, h2_ref, oc_ref):
        a, _ = _rms_fwd(om_ref[...], gm_ref[...])
        b, _ = _rms_fwd(os_ref[...], gs_ref[...])
        oc = jnp.concatenate([a.astype(BF16), b.astype(BF16)], axis=1)
        oc_ref[...] = oc
        h2_ref[...] = h_ref[...] + _dot(oc, wo_ref[...])

    return pl.pallas_call(
        body, name="out_fwd", grid=(T // tm,),
        in_specs=[_row_spec(tm, W), _row_spec(tm, W), _const_spec((1, W)), _const_spec((1, W)), _const_spec((2 * W, D)),
                  _row_spec(tm, D)],
        out_specs=[_row_spec(tm, D), _row_spec(tm, 2 * W)],
        out_shape=[jax.ShapeDtypeStruct((T, D), F32), jax.ShapeDtypeStruct((T, 2 * W), BF16)],
        compiler_params=_cparams(("arbitrary",)),
    )(o_mla, o_swa, g_mla, g_swa, w_o, h1)


def _out_bwd(dh2, o_mla, o_swa, g_mla, g_swa, w_o):
    T, D = dh2.shape
    W = o_mla.shape[1]
    tm = _pick(T, (1024, 512, 256, 128))

    def body(dh_ref, om_ref, os_ref, gm_ref, gs_ref, wo_ref, dom_ref, dos_ref, dgm_ref, dgs_ref):
        doc = _dot_nt(dh_ref[...].astype(BF16), wo_ref[...])
        om, osw = om_ref[...], os_ref[...]
        _, ra = _rms_fwd(om, gm_ref[...])
        _, rb = _rms_fwd(osw, gs_ref[...])
        da, dga = _rms_bwd(om, gm_ref[...], ra, doc[:, :W])
        db, dgb = _rms_bwd(osw, gs_ref[...], rb, doc[:, W:])
        dom_ref[...] = da
        dos_ref[...] = db

        @pl.when(pl.program_id(0) == 0)
        def _():
            dgm_ref[...] = jnp.zeros_like(dgm_ref)
            dgs_ref[...] = jnp.zeros_like(dgs_ref)

        dgm_ref[...] += dga
        dgs_ref[...] += dgb

    return pl.pallas_call(
        body, name="out_bwd", grid=(T // tm,),
        in_specs=[_row_spec(tm, D), _row_spec(tm, W), _row_spec(tm, W), _const_spec((1, W)), _const_spec((1, W)),
                  _const_spec((2 * W, D))],
        out_specs=[_row_spec(tm, W), _row_spec(tm, W), _acc_spec((1, W)), _acc_spec((1, W))],
        out_shape=[jax.ShapeDtypeStruct((T, W), F32), jax.ShapeDtypeStruct((T, W), F32),
                   jax.ShapeDtypeStruct((1, W), F32), jax.ShapeDtypeStruct((1, W), F32)],
        compiler_params=_cparams(("arbitrary",)),
    )(dh2, o_mla, o_swa, g_mla, g_swa, w_o)


def _half_mask(shape, half):
    lane = lax.broadcasted_iota(jnp.int32, shape, len(shape) - 1)
    return (lane < 64) if half == 0 else (lane >= 64)


def _mla_fwd(q, k, v, B, S, exchange=None):
    T = B * S
    tq = _pick(S, MLA_Q_TILES)
    nq = S // tq
    pairs = MLA_HEADS // 2

    def body(q_ref, k_ref, v_ref, o_ref, lse_ref):
        causal = lax.broadcasted_iota(jnp.int32, (tq, tq), 1) <= lax.broadcasted_iota(jnp.int32, (tq, tq), 0)
        low = _half_mask((tq, 2 * V_HEAD), 0)
        for qi in range(nq):
            rows = slice(qi * tq, (qi + 1) * tq)
            past = slice(0, qi * tq)
            outs, lses = [], []
            for half in range(2):
                lanes = slice(half * HEAD_BLOCK, (half + 1) * HEAD_BLOCK)
                qv = q_ref[rows, lanes]
                s_d = jnp.where(causal, _dot_nt(qv, k_ref[rows, lanes]) * MLA_SCALE_LOG2, NEG)
                m = jnp.max(s_d, axis=-1, keepdims=True)
                if qi:
                    s_p = _dot_nt(qv, k_ref[past, lanes]) * MLA_SCALE_LOG2
                    m = jnp.maximum(m, jnp.max(s_p, axis=-1, keepdims=True))
                p_d = jnp.exp2(s_d - m)
                l = jnp.sum(p_d, axis=-1, keepdims=True)
                acc = _dot(p_d.astype(BF16), v_ref[rows, :])
                if qi:
                    p_p = jnp.exp2(s_p - m)
                    l = l + jnp.sum(p_p, axis=-1, keepdims=True)
                    acc = acc + _dot(p_p.astype(BF16), v_ref[past, :])
                outs.append(acc * (1.0 / l))
                lses.append(jnp.broadcast_to(m + jnp.log2(l), (tq, 2 * V_HEAD)))
            o_ref[rows, :] = jnp.where(low, outs[0], outs[1])
            lse_ref[rows, :] = jnp.where(low, lses[0], lses[1])

    blk2 = pl.BlockSpec((S, 2 * HEAD_BLOCK), lambda b, j: (b, j))
    blk1 = pl.BlockSpec((S, 2 * V_HEAD), lambda b, j: (b, j))
    return _pcall(
        body, name="mla_fwd", grid=(B, pairs),
        in_specs=[blk2, blk2, blk1], out_specs=[blk1, blk1],
        out_shape=[jax.ShapeDtypeStruct((T, pairs * 2 * V_HEAD), F32)] * 2,
        args=(q, k, v), exchange=exchange)


def _mla_bwd(q, k, v, do, o, lse, B, S, exchange=None):
    T = B * S
    tq = _pick(S, MLA_Q_TILES)
    nq = S // tq
    pairs = MLA_HEADS // 2

    def body(q_ref, k_ref, v_ref, do_ref, o_ref, lse_ref, dq_ref, dk_ref, dv_ref, dkt_acc, dvt_acc):
        causal = lax.broadcasted_iota(jnp.int32, (tq, tq), 1) <= lax.broadcasted_iota(jnp.int32, (tq, tq), 0)
        for half in range(2):
            lanes = slice(half * HEAD_BLOCK, (half + 1) * HEAD_BLOCK)
            col = half * V_HEAD
            own = _half_mask((tq, 2 * V_HEAD), half)
            for qi in reversed(range(nq)):
                rows = slice(qi * tq, (qi + 1) * tq)
                past = slice(0, qi * tq)
                first = qi == nq - 1
                qv = q_ref[rows, lanes]
                dov = jnp.where(own, do_ref[rows, :], 0.0)
                dsum = jnp.sum(dov * o_ref[rows, :], axis=-1, keepdims=True)
                dob = dov.astype(BF16)
                lse_col = lse_ref[rows, col:col + 1]

                def grads(keys, mask):
                    kv, vv = k_ref[keys, lanes], v_ref[keys, :]
                    p = jnp.exp2(_dot_nt(qv, kv) * MLA_SCALE_LOG2 - lse_col)
                    if mask is not None:
                        p = jnp.where(mask, p, 0.0)
                    ds = (p * (_dot_nt(dob, vv) - dsum) * MLA_SCALE).astype(BF16)
                    dkt, dvt = _dot_tn(qv, ds), _dot_tn(dob, p.astype(BF16))
                    if first:
                        dkt_acc[:, keys] = dkt
                    else:
                        dkt_acc[:, keys] += dkt
                    if first and half == 0:
                        dvt_acc[:, keys] = dvt
                    else:
                        dvt_acc[:, keys] += dvt
                    return _dot(ds, kv)

                dq = grads(rows, causal)
                if qi:
                    dq = dq + grads(past, None)
                dq_ref[rows, lanes] = dq
            dk_ref[:, lanes] = dkt_acc[...].T
        dv_ref[...] = dvt_acc[...].T.astype(BF16)

    blk2 = pl.BlockSpec((S, 2 * HEAD_BLOCK), lambda b, j: (b, j))
    blk1 = pl.BlockSpec((S, 2 * V_HEAD), lambda b, j: (b, j))
    return _pcall(
        body, name="mla_bwd", grid=(B, pairs),
        in_specs=[blk2, blk2, blk1, blk1, blk1, blk1], out_specs=[blk2, blk2, blk1],
        out_shape=[jax.ShapeDtypeStruct((T, MLA_HEADS * HEAD_BLOCK), F32), jax.ShapeDtypeStruct((T, MLA_HEADS * HEAD_BLOCK), F32),
                   jax.ShapeDtypeStruct((T, pairs * 2 * V_HEAD), BF16)],
        scratch_shapes=[pltpu.VMEM((HEAD_BLOCK, S), F32), pltpu.VMEM((2 * V_HEAD, S), F32)],
        args=(q, k, v, do, o, lse), exchange=exchange)


SWA_GROUP = SWA_HEADS // SWA_KV_HEADS
SWA_ROWS = SWA_GROUP * WINDOW


SWA_CHUNK = 8


def _bdot(a, b, ca, cb):
    return lax.dot_general(a, b, (((ca,), (cb,)), ((0,), (0,))), preferred_element_type=F32)


def _swa_chunks(nb):
    cb = next(c for c in (SWA_CHUNK, 4, 2, 1) if nb % c == 0)
    return cb, [(n0, slice(n0 * WINDOW, (n0 + cb) * WINDOW)) for n0 in range(0, nb, cb)]


def _swa_keys(ref, n0, cb):
    W = WINDOW
    cur = ref[n0 * W:(n0 + cb) * W, :]
    if n0:
        prev = ref[(n0 - 1) * W:(n0 + cb - 1) * W, :]
    else:
        prev = jnp.concatenate([cur[:W], cur[:(cb - 1) * W]], axis=0) if cb > 1 else cur
    return jnp.concatenate([prev.reshape(cb, W, W), cur.reshape(cb, W, W)], axis=1)


def _swa_stack(ref, rows, cb):
    return jnp.concatenate([ref[rows, j * WINDOW:(j + 1) * WINDOW].reshape(cb, WINDOW, WINDOW) for j in range(SWA_GROUP)], axis=1)


def _swa_unstack(low, high, cb):
    sel = _half_mask((cb, WINDOW, WINDOW), 0)
    return jnp.concatenate([jnp.where(sel, low[:, j * WINDOW:(j + 1) * WINDOW], high[:, j * WINDOW:(j + 1) * WINDOW])
                            .reshape(cb * WINDOW, WINDOW) for j in range(SWA_GROUP)], axis=1)


def _swa_sink_col(sink_ref, g):
    return jnp.concatenate([jnp.broadcast_to(sink_ref[g * SWA_GROUP + j:g * SWA_GROUP + j + 1, 0:1], (WINDOW, 1))
                            for j in range(SWA_GROUP)], axis=0)


def _swa_probs(qm, kcat, bias_g, sink, first_chunk):
    shape = (qm.shape[0], SWA_ROWS, 2 * WINDOW)
    qi = lax.broadcasted_iota(jnp.int32, shape, 1) % WINDOW
    col = lax.broadcasted_iota(jnp.int32, shape, 2)
    valid = (col > qi) & (col <= qi + WINDOW)
    if first_chunk:
        valid = valid & ((col >= WINDOW) | (lax.broadcasted_iota(jnp.int32, shape, 0) > 0))
    s = jnp.where(valid, _bdot(qm, kcat, 2, 2) * SWA_SCALE + bias_g, NEG)
    m = jnp.maximum(jnp.max(s, axis=-1, keepdims=True), sink)
    e = jnp.exp(s - m)
    e_s = jnp.exp(sink - m)
    inv = 1.0 / (jnp.sum(e, axis=-1, keepdims=True) + e_s)
    return e * inv, e_s * inv


def _swa_fwd(q, k, v, bias, sinks, B, S):
    T = B * S
    W = WINDOW
    nb = S // W

    cb, chunks = _swa_chunks(nb)

    def body(q_ref, k_ref, v_ref, bias_ref, sink_ref, o_ref):
        sink_cols = [_swa_sink_col(sink_ref, g) for g in range(SWA_KV_HEADS)]
        for n0, rows in chunks:
            kcat, vcat = _swa_keys(k_ref, n0, cb), _swa_keys(v_ref, n0, cb)
            qcat = _swa_stack(q_ref, rows, cb)
            outs = []
            for g in range(SWA_KV_HEADS):
                qm = jnp.where(_half_mask(qcat.shape, g), qcat, jnp.zeros_like(qcat))
                p, _ = _swa_probs(qm, kcat, bias_ref[g], sink_cols[g], n0 == 0)
                outs.append(_bdot(p.astype(BF16), vcat, 2, 1))
            o_ref[rows, :] = _swa_unstack(outs[0], outs[1], cb)

    SQ, SK = SWA_HEADS * SWA_HEAD_DIM, SWA_KV_HEADS * SWA_HEAD_DIM
    return pl.pallas_call(
        body, name="swa_fwd", grid=(B,),
        in_specs=[pl.BlockSpec((S, SQ), lambda b: (b, 0)), pl.BlockSpec((S, SK), lambda b: (b, 0)),
                  pl.BlockSpec((S, SK), lambda b: (b, 0)), _const_spec((SWA_KV_HEADS, SWA_ROWS, 2 * W)),
                  _const_spec((SWA_HEADS, 128))],
        out_specs=pl.BlockSpec((S, SQ), lambda b: (b, 0)),
        out_shape=jax.ShapeDtypeStruct((T, SQ), F32),
        compiler_params=_cparams(("arbitrary",)),
    )(q, k, v, bias, sinks)


def _swa_bwd(q, k, v, do, bias, sinks, B, S):
    T = B * S
    W = WINDOW
    nb = S // W
    cb, chunks = _swa_chunks(nb)

    def body(q_ref, k_ref, v_ref, do_ref, bias_ref, sink_ref, dq_ref, dk_ref, dv_ref, dbias_ref, dsink_ref, dkt_acc, dvt_acc):
        @pl.when(pl.program_id(0) == 0)
        def _():
            dbias_ref[...] = jnp.zeros_like(dbias_ref)
            dsink_ref[...] = jnp.zeros_like(dsink_ref)

        dkt_acc[...] = jnp.zeros_like(dkt_acc)
        dvt_acc[...] = jnp.zeros_like(dvt_acc)
        sink_cols = [_swa_sink_col(sink_ref, g) for g in range(SWA_KV_HEADS)]
        for n0, rows in chunks:
            kcat, vcat = _swa_keys(k_ref, n0, cb), _swa_keys(v_ref, n0, cb)
            qcat = _swa_stack(q_ref, rows, cb)
            docat = _swa_stack(do_ref, rows, cb)
            dkt = jnp.zeros((cb, W, 2 * W), F32)
            dvt = jnp.zeros((cb, W, 2 * W), F32)
            dqs = []
            for g in range(SWA_KV_HEADS):
                own = _half_mask(qcat.shape, g)
                qm = jnp.where(own, qcat, jnp.zeros_like(qcat))
                dom = jnp.where(own, docat, 0.0).astype(BF16)
                p, p_s = _swa_probs(qm, kcat, bias_ref[g], sink_cols[g], n0 == 0)
                dp = _bdot(dom, vcat, 2, 2)
                dsum = jnp.sum(p * dp, axis=-1, keepdims=True)
                ds = p * (dp - dsum)
                dbias_ref[g] += jnp.sum(ds, axis=0)
                dsink_ref[g] += jnp.broadcast_to(-jnp.sum(p_s * dsum, axis=0), (SWA_ROWS, 128))
                dsb = (ds * SWA_SCALE).astype(BF16)
                dqs.append(_bdot(dsb, kcat, 2, 1))
                dkt = dkt + _bdot(qm, dsb, 1, 1)
                dvt = dvt + _bdot(dom, p.astype(BF16), 1, 1)
            dq_ref[rows, :] = _swa_unstack(dqs[0], dqs[1], cb).astype(BF16)
            for i in range(cb):
                n = n0 + i
                if n:
                    cols = slice((n - 1) * W, (n + 1) * W)
                    dkt_acc[:, cols] += dkt[i]
                    dvt_acc[:, cols] += dvt[i]
                else:
                    dkt_acc[:, :W] += dkt[i][:, W:]
                    dvt_acc[:, :W] += dvt[i][:, W:]
        dk_ref[...] = dkt_acc[...].T.astype(BF16)
        dv_ref[...] = dvt_acc[...].T.astype(BF16)

    SQ, SK = SWA_HEADS * SWA_HEAD_DIM, SWA_KV_HEADS * SWA_HEAD_DIM
    row = lambda w: pl.BlockSpec((S, w), lambda b: (b, 0))
    return pl.pallas_call(
        body, name="swa_bwd", grid=(B,),
        in_specs=[row(SQ), row(SK), row(SK), row(SQ), _const_spec((SWA_KV_HEADS, SWA_ROWS, 2 * W)), _const_spec((SWA_HEADS, 128))],
        out_specs=[row(SQ), row(SK), row(SK), _acc_spec((SWA_KV_HEADS, SWA_ROWS, 2 * W)), _acc_spec((SWA_KV_HEADS, SWA_ROWS, 128))],
        out_shape=[jax.ShapeDtypeStruct((T, SQ), BF16), jax.ShapeDtypeStruct((T, SK), BF16), jax.ShapeDtypeStruct((T, SK), BF16),
                   jax.ShapeDtypeStruct((SWA_KV_HEADS, SWA_ROWS, 2 * W), F32), jax.ShapeDtypeStruct((SWA_KV_HEADS, SWA_ROWS, 128), F32)],
        scratch_shapes=[pltpu.VMEM((SK, S), F32), pltpu.VMEM((SK, S), F32)],
        compiler_params=_cparams(("arbitrary",)),
    )(q, k, v, do, bias, sinks)


def _bias_expand(rel_bias, bucket):
    W = WINDOW

    def body(rb_ref, bucket_ref, out_ref):
        bk = bucket_ref[...]
        for h in range(SWA_HEADS):
            def add(b, acc):
                return jnp.where(bk == b, rb_ref[b, h], acc)

            out_ref[h] = lax.fori_loop(0, REL_BUCKETS, add, jnp.zeros((W, 2 * W), F32))

    return pl.pallas_call(
        body, name="bias_expand",
        in_specs=[pl.BlockSpec(memory_space=pltpu.SMEM), pl.BlockSpec(memory_space=pltpu.VMEM)],
        out_specs=pl.BlockSpec(memory_space=pltpu.VMEM),
        out_shape=jax.ShapeDtypeStruct((SWA_HEADS, W, 2 * W), F32),
        compiler_params=_cparams(),
    )(rel_bias, bucket)


def _bias_reduce(dbias, dsink_rows, bucket):
    def body(db_ref, ds_ref, bucket_ref, out_ref, sink_out_ref, rows_ref):
        sink_lane = lax.broadcasted_iota(jnp.int32, (1, 128), 1)
        sink_acc = jnp.zeros((1, 128), F32)
        for h in range(SWA_HEADS):
            sink_acc = jnp.where(sink_lane == h, jnp.sum(ds_ref[h], axis=0, keepdims=True), sink_acc)
        sink_out_ref[...] = sink_acc
        bk = bucket_ref[...]
        for h in range(SWA_HEADS):
            dbh = db_ref[h]

            def add(b, carry):
                rows_ref[pl.ds(h * REL_BUCKETS + b, 1), :] = jnp.sum(jnp.where(bk == b, dbh, 0.0), axis=0, keepdims=True)
                return carry

            lax.fori_loop(0, REL_BUCKETS, add, 0)
        totals = jnp.sum(rows_ref[...], axis=-1, keepdims=True)
        lane = lax.broadcasted_iota(jnp.int32, (REL_BUCKETS, 128), 1)
        acc = jnp.zeros((REL_BUCKETS, 128), F32)
        for h in range(SWA_HEADS):
            acc = jnp.where(lane == h, totals[h * REL_BUCKETS:(h + 1) * REL_BUCKETS], acc)
        out_ref[...] = acc

    return pl.pallas_call(
        body, name="bias_reduce",
        out_shape=[jax.ShapeDtypeStruct((REL_BUCKETS, 128), F32), jax.ShapeDtypeStruct((1, 128), F32)],
        scratch_shapes=[pltpu.VMEM((SWA_HEADS * REL_BUCKETS, 2 * WINDOW), F32)],
        compiler_params=_cparams(),
    )(dbias, dsink_rows, bucket)


def _t5_bucket(dist):
    n = jnp.maximum(dist, 0)
    max_exact = REL_BUCKETS // 2
    nf = jnp.maximum(n, 1).astype(F32)
    large = max_exact + (jnp.log(nf / max_exact) / math.log(REL_MAX_DIST / max_exact) * (REL_BUCKETS - max_exact)).astype(jnp.int32)
    large = jnp.minimum(large, REL_BUCKETS - 1)
    return jnp.where(n < max_exact, n, large)


def _pair_heads(a, axis, inverse=False):
    shp = a.shape
    a = a.reshape(shp[:axis] + ((4, 2, 64) if inverse else (2, 4, 64)) + shp[axis + 1:])
    a = jnp.swapaxes(a, axis, axis + 1)
    return a.reshape(shp)


def _owner_blocks(w32, w16):
    r, c = w32.shape[0] // N_DEV, w32.shape[1]
    return w32.reshape(N_DEV, r, c), w16.reshape(N_DEV, r, c)


_IN_WIDTHS = (Q_LORA, KV_LORA, QK_ROPE, SWA_HEADS * SWA_HEAD_DIM, SWA_KV_HEADS * SWA_HEAD_DIM, SWA_KV_HEADS * SWA_HEAD_DIM)


def _win_to_kernel(wt):
    parts, o = [], 0
    for wd in _IN_WIDTHS:
        parts.append(wt[o:o + wd])
        o += wd
    cq, ckv, kpe, qs, ks, vs = parts
    return jnp.concatenate([cq, ckv, _pair_heads(qs, 0), ks, vs, jnp.pad(kpe, ((64, 32), (0, 0)))], axis=0)


def _win_from_kernel(dwt):
    o = [0, Q_LORA, Q_LORA + KV_LORA, Q_LORA + KV_LORA + 512, Q_LORA + KV_LORA + 640, Q_LORA + KV_LORA + 768]
    cq, ckv, qs, ks, vs, kpe = [dwt[a:b] for a, b in zip(o, o[1:] + [dwt.shape[0]])]
    return jnp.concatenate([cq, ckv, kpe[64:96], _pair_heads(qs, 0, inverse=True), ks, vs], axis=0)


def _wq_to_kernel(wt):
    return jnp.pad(wt.reshape(MLA_HEADS, QK_NOPE + QK_ROPE, Q_LORA), ((0, 0), (0, 32), (0, 0))).reshape(MLA_HEADS * HEAD_BLOCK, Q_LORA)


def _wq_from_kernel(dwt):
    return dwt.reshape(MLA_HEADS, HEAD_BLOCK, Q_LORA)[:, :QK_NOPE + QK_ROPE].reshape(-1, Q_LORA)


def _wkv_to_kernel(wt):
    w3 = wt.reshape(MLA_HEADS, QK_NOPE + V_HEAD, KV_LORA)
    kpart = jnp.pad(w3[:, :QK_NOPE], ((0, 0), (0, 64), (0, 0))).reshape(MLA_HEADS * HEAD_BLOCK, KV_LORA)
    return jnp.concatenate([kpart, w3[:, QK_NOPE:].reshape(MLA_HEADS * V_HEAD, KV_LORA)], axis=0)


def _wkv_from_kernel(dwt):
    dk = dwt[:MLA_HEADS * HEAD_BLOCK].reshape(MLA_HEADS, HEAD_BLOCK, KV_LORA)[:, :QK_NOPE]
    dv = dwt[MLA_HEADS * HEAD_BLOCK:].reshape(MLA_HEADS, V_HEAD, KV_LORA)
    return jnp.concatenate([dk, dv], axis=1).reshape(-1, KV_LORA)


def _wo_to_kernel(w):
    return jnp.concatenate([w[:512], _pair_heads(w[512:], 0)], axis=0)


def _wo_from_kernel(dw):
    return jnp.concatenate([dw[:512], _pair_heads(dw[512:], 0, inverse=True)], axis=0)


def _kernel_weights(key, stored):
    if key in ("g1", "u1", "g2", "u2"):
        return {key + "t": stored}
    if key in ("d1", "d2"):
        return {key: stored}
    if key == "o":
        return {key: _wo_to_kernel(stored)}
    return {key + "t": _TO_KERNEL[key](stored)}


_SMALL = ("g_ffn1", "g_mix", "g_q_a", "g_kv_a", "attn_sinks", "rel_bias", "g_out_mla", "g_out_swa", "g_ffn2", "g_final")


class _Comm:
    def __init__(self, shards):
        self.shards = shards
        self.own32, self.recv16, self.pending, self.token = {}, {}, [], None

    def gather(self, keys):
        return [("gather", self.shards[k]) for k in keys]

    def gathered(self, keys, results, wts):
        for k, g in zip(keys, results):
            wts.update(_kernel_weights(k, g.reshape(N_DEV * g.shape[1], g.shape[2])))

    def scatter(self, keys, grads):
        jobs = []
        for k in keys:
            g32, g16 = (_FROM_KERNEL[k](g) if k in _FROM_KERNEL else g for g in grads[k])
            self.own32[k], blocks = _owner_blocks(g32, g16)
            jobs.append(("scatter", blocks))
        return jobs

    def scattered(self, keys, results):
        self.recv16.update(zip(keys, results))

    def scatter_start(self, keys, grads, name):
        handle, self.token = _scatter_start([blocks for _, blocks in self.scatter(keys, grads)], name + "_start")
        self.pending.append((keys, handle, name + "_wait"))

    def scatter_wait(self, after):
        keys, handle, name = self.pending.pop(0)
        self.scattered(keys, _scatter_wait(handle, after, name))
        return keys


def _step(x, target, gains, wts, comm):
    def jobs_in(keys):
        return comm.gather(keys) if comm else None

    def jobs_out(keys):
        return comm.scatter(keys, grads) if comm else None

    B, S, D = x.shape
    T = B * S
    x2 = x.reshape(T, D)
    t2 = target.reshape(T, D)

    pos = jnp.arange(S, dtype=F32)
    inv_freq = ROPE_THETA ** (-jnp.arange(0, QK_ROPE, 2, dtype=F32) / QK_ROPE)
    ang = pos[:, None] * inv_freq[None, :]
    cos, sin = jnp.cos(ang), jnp.sin(ang)
    ones, zeros = jnp.ones((S, 64), F32), jnp.zeros((S, 32), F32)
    rope_c = jnp.concatenate([ones, cos, cos, zeros], axis=1)
    rope_s = jnp.concatenate([0.0 * ones, -sin, sin, zeros], axis=1)

    qi = jnp.arange(WINDOW)[:, None]
    kj = jnp.arange(2 * WINDOW)[None, :]
    bucket = _t5_bucket(qi + WINDOW - kj).astype(jnp.int32)
    bias = _bias_expand(gains["rel_bias"], bucket).reshape(SWA_KV_HEADS, SWA_ROWS, 2 * WINDOW)
    sinks = jnp.broadcast_to(gains["attn_sinks"].reshape(SWA_HEADS, 1), (SWA_HEADS, 128))
    g_swa = _pair_heads(gains["g_out_swa"], 1)

    keys = ("in", "q", "kv", "o", "g2")
    (h1, n1, gate1, up1, act1), got = _ffn_fwd(x2, gains["g_ffn1"], wts["g1t"], wts["u1t"], wts["d1"], "ffn1_fwd", jobs_in(keys))
    if comm:
        comm.gathered(keys, got, wts)
    (u, c_q, c_kv, cqn, ckvn, q, k, v, qs, ks, vs) = _proj_fwd(
        h1, gains["g_mix"], wts["int"], gains["g_q_a"], wts["qt"], gains["g_kv_a"], wts["kvt"], rope_c, rope_s, S)
    keys = ("u2", "d2")
    (o_mla, lse), got = _mla_fwd(q, k, v, B, S, jobs_in(keys))
    if comm:
        comm.gathered(keys, got, wts)
    o_swa = _swa_fwd(qs, ks, vs, bias, sinks, B, S)
    h2, oc = _out_fwd(o_mla, o_swa, gains["g_out_mla"], g_swa, wts["o"], h1)
    (dh3, n2, gate2, up2, act2, loss, dg_final), _ = _ffn_fwd(
        h2, gains["g_ffn2"], wts["g2t"], wts["u2t"], wts["d2"], "ffn2_fwd", loss_head=(t2, gains["g_final"].reshape(1, D)))

    grads = {}
    (dh2, dg_ffn2, dgate2, dup2), _ = _ffn_bwd(h2, gains["g_ffn2"], gate2, up2, dh3, wts["g2t"], wts["u2t"], wts["d2"], "ffn2_bwd")
    grads["g2"], _ = _matmul_tn(dgate2, n2, "dw_gate2")
    grads["u2"], _ = _matmul_tn(dup2, n2, "dw_up2")
    grads["d2"], _ = _matmul_tn(act2, dh3, "dw_down2", 0.5)
    do_mla, do_swa, dg_mla, dg_swa = _out_bwd(dh2, o_mla, o_swa, gains["g_out_mla"], g_swa, wts["o"])
    grads["o"], _ = _matmul_tn(oc, dh2, "dw_o")
    keys = ("g2", "u2", "d2")
    (dq, dk, dv), got = _mla_bwd(q, k, v, do_mla, o_mla, lse, B, S, jobs_out(keys))
    if comm:
        comm.scattered(keys, got)
    dqs, dks, dvs, dbias, dsink = _swa_bwd(qs, ks, vs, do_swa, bias, sinks, B, S)
    dh1, dproj, dql, dkvc, dg_mix, dg_q, dg_kv = _proj_bwd(
        dq, dk, dv, dqs, dks, dvs, c_q, c_kv, h1, dh2, gains["g_mix"], wts["int"], gains["g_q_a"], wts["qt"],
        gains["g_kv_a"], wts["kvt"], rope_c, rope_s, S)
    grads["in"], _ = _matmul_tn(dproj, u, "dw_in")
    grads["q"], _ = _matmul_tn(dql, cqn, "dw_q")
    grads["kv"], _ = _matmul_tn(dkvc, ckvn, "dw_kv")
    grads["d1"], _ = _matmul_tn(act1, dh1, "dw_down1", 0.5)
    if comm:
        comm.scatter_start(("o", "in", "q", "kv", "d1"), grads, "grad_scatter_mid")
    (dx, dg_ffn1, dgate1, dup1), _ = _ffn_bwd(
        x2, gains["g_ffn1"], gate1, up1, dh1, wts["g1t"], wts["u1t"], wts["d1"], "ffn1_bwd", after=comm and comm.token)
    d_rel_bias, d_sinks = _bias_reduce(dbias.reshape(SWA_HEADS, WINDOW, 2 * WINDOW), dsink.reshape(SWA_HEADS, WINDOW, 128), bucket)
    small = {
        "g_ffn1": dg_ffn1, "g_mix": dg_mix, "g_q_a": dg_q, "g_kv_a": dg_kv, "attn_sinks": d_sinks,
        "rel_bias": d_rel_bias, "g_out_mla": dg_mla, "g_out_swa": _pair_heads(dg_swa, 1, inverse=True),
        "g_ffn2": dg_ffn2, "g_final": dg_final, "loss": loss,
    }
    small_jobs = [("gather", small[n]) for n in _SMALL + ("loss",)] if comm else None
    grads["g1"], got = _matmul_tn(dgate1, n1, "dw_gate1", exchange=small_jobs)
    if comm:
        comm.small_gathered = got
        comm.scatter_start(("g1",), grads, "grad_scatter_gate1")
    grads["u1"], _ = _matmul_tn(dup1, n1, "dw_up1", after=comm and comm.token)
    if comm:
        comm.scatter_start(("u1",), grads, "grad_scatter_up1")
    return dx.reshape(B, S, D), grads, small


_WEIGHTS = ("g_ffn1", "w_ffn1_gate", "w_ffn1_up", "w_ffn1_down", "g_mix", "w_in", "g_q_a", "w_q_b", "g_kv_a", "w_kv_b",
            "attn_sinks", "rel_bias", "g_out_mla", "g_out_swa", "w_o", "g_ffn2", "w_ffn2_gate", "w_ffn2_up", "w_ffn2_down",
            "g_final")
_BIG = (("w_ffn1_gate", "g1", True), ("w_ffn1_up", "u1", True), ("w_ffn1_down", "d1", False), ("w_in", "in", True),
        ("w_q_b", "q", True), ("w_kv_b", "kv", True), ("w_o", "o", False), ("w_ffn2_gate", "g2", True),
        ("w_ffn2_up", "u2", True), ("w_ffn2_down", "d2", False))
_TO_KERNEL = {"in": _win_to_kernel, "q": _wq_to_kernel, "kv": _wkv_to_kernel}
_FROM_KERNEL = {"in": _win_from_kernel, "q": _wq_from_kernel, "kv": _wkv_from_kernel, "o": _wo_from_kernel}


def kernel(x, g_ffn1, w_ffn1_gate, w_ffn1_up, w_ffn1_down, g_mix, w_in, g_q_a, w_q_b, g_kv_a, w_kv_b, attn_sinks, rel_bias, g_out_mla, g_out_swa, w_o, g_ffn2, w_ffn2_gate, w_ffn2_up, w_ffn2_down, g_final, loss_target, m_g_ffn1, m_w_ffn1_gate, m_w_ffn1_up, m_w_ffn1_down, m_g_mix, m_w_in, m_g_q_a, m_w_q_b, m_g_kv_a, m_w_kv_b, m_attn_sinks, m_rel_bias, m_g_out_mla, m_g_out_swa, m_w_o, m_g_ffn2, m_w_ffn2_gate, m_w_ffn2_up, m_w_ffn2_down, m_g_final, v_g_ffn1, v_w_ffn1_gate, v_w_ffn1_up, v_w_ffn1_down, v_g_mix, v_w_in, v_g_q_a, v_w_q_b, v_g_kv_a, v_w_kv_b, v_attn_sinks, v_rel_bias, v_g_out_mla, v_g_out_swa, v_w_o, v_g_ffn2, v_w_ffn2_gate, v_w_ffn2_up, v_w_ffn2_down, v_g_final):
    w = dict(zip(_WEIGHTS, (g_ffn1, w_ffn1_gate, w_ffn1_up, w_ffn1_down, g_mix, w_in, g_q_a, w_q_b, g_kv_a, w_kv_b, attn_sinks,
                            rel_bias, g_out_mla, g_out_swa, w_o, g_ffn2, w_ffn2_gate, w_ffn2_up, w_ffn2_down, g_final)))
    m = dict(zip(_WEIGHTS, (m_g_ffn1, m_w_ffn1_gate, m_w_ffn1_up, m_w_ffn1_down, m_g_mix, m_w_in, m_g_q_a, m_w_q_b, m_g_kv_a,
                            m_w_kv_b, m_attn_sinks, m_rel_bias, m_g_out_mla, m_g_out_swa, m_w_o, m_g_ffn2, m_w_ffn2_gate,
                            m_w_ffn2_up, m_w_ffn2_down, m_g_final)))
    v = dict(zip(_WEIGHTS, (v_g_ffn1, v_w_ffn1_gate, v_w_ffn1_up, v_w_ffn1_down, v_g_mix, v_w_in, v_g_q_a, v_w_q_b, v_g_kv_a,
                            v_w_kv_b, v_attn_sinks, v_rel_bias, v_g_out_mla, v_g_out_swa, v_w_o, v_g_ffn2, v_w_ffn2_gate,
                            v_w_ffn2_up, v_w_ffn2_down, v_g_final)))
    me = 4 * lax.axis_index("x") + 2 * lax.axis_index("y") + lax.axis_index("c")

    stored = lambda src, name, by_col: src[name][0].T if by_col else src[name][0]
    shards = dict(zip([key for _, key, _ in _BIG], _cast_bf16([stored(w, name, by_col) for name, _, by_col in _BIG])))
    comm = _Comm(shards)
    first = ("g1", "u1", "d1")
    wts = {}
    comm.gathered(first, _all_gather([shards[k] for k in first], "weights_all_gather"), wts)

    gains = {n: (w[n] if n == "rel_bias" else w[n].reshape(1, -1)) for n in _SMALL}
    grad_x, _, _ = _step(x, loss_target, gains, wts, comm)

    small_gathered = comm.small_gathered

    out_g, out_d, out_m, out_v = {}, {}, {}, {}
    me1 = me.astype(jnp.int32).reshape(1)
    big = {key: (name, by_col) for name, key, by_col in _BIG}

    def update(key):
        name, by_col = big[key]
        updates = _adam_big(stored(w, name, by_col), stored(m, name, by_col), stored(v, name, by_col),
                            comm.own32[key], comm.recv16[key], me1, comm.token)
        out_g[name], out_d[name], out_m[name], out_v[name] = ((a.T if by_col else a)[None] for a in updates)
        return updates[0]

    done = [update(key) for key in ("g2", "u2", "d2")]
    for _ in range(2):
        done = [update(key) for key in comm.scatter_wait(after=done[-1])]
    as_2d = lambda src: [src[n] if src[n].ndim == 2 else src[n].reshape(1, -1) for n in _SMALL]
    loss, updates = _adam_small(as_2d(w), as_2d(m), as_2d(v), small_gathered[:-1], small_gathered[-1])
    for n, (g, d, mn, vn) in zip(_SMALL, updates):
        out_g[n], out_d[n], out_m[n], out_v[n] = (a.reshape(w[n].shape) for a in (g, d, mn, vn))
    for key in comm.scatter_wait(after=loss):
        update(key)

    return (loss[0, 0], grad_x, *[out_g[n] for n in _WEIGHTS], *[out_d[n] for n in _WEIGHTS],
            *[out_m[n] for n in _WEIGHTS], *[out_v[n] for n in _WEIGHTS])
```

```python
import functools
import math

import jax
import jax.numpy as jnp
from jax import lax
from jax.experimental import pallas as pl
from jax.experimental.pallas import tpu as pltpu

F32 = jnp.float32
BF16 = jnp.bfloat16
MESH = pl.DeviceIdType.MESH

EPS = 1e-6
NEG = -1e30
N_DEV = 8

MLA_HEADS = 8
Q_LORA = 256
KV_LORA = 128
QK_NOPE = 64
QK_ROPE = 32
V_HEAD = 64
ROPE_THETA = 10000.0
SWA_HEADS = 8
SWA_KV_HEADS = 2
SWA_HEAD_DIM = 64
WINDOW = 128
REL_BUCKETS = 32
REL_MAX_DIST = 128
HEAD_BLOCK = 128
MLA_Q_TILES = (512, 256, 128)
MLA_SCALE = (QK_NOPE + QK_ROPE) ** -0.5
MLA_SCALE_LOG2 = MLA_SCALE * math.log2(math.e)
SWA_SCALE = SWA_HEAD_DIM ** -0.5

ADAM_LR = 0.001
ADAM_B1 = 0.9
ADAM_B2 = 0.999
ADAM_EPS = 1e-08
ADAM_WD = 0.01
ADAM_STEP = 10

VMEM_LIMIT = 56 * 1024 * 1024


def _cparams(semantics=None):
    return pltpu.CompilerParams(dimension_semantics=semantics, vmem_limit_bytes=VMEM_LIMIT)


def _dot(a, b):
    return jnp.dot(a, b, preferred_element_type=F32)


def _dot_nt(a, b):
    return lax.dot_general(a, b, (((1,), (1,)), ((), ())), preferred_element_type=F32)


def _dot_tn(a, b):
    return lax.dot_general(a, b, (((0,), (0,)), ((), ())), preferred_element_type=F32)


def _rms_fwd(x, g):
    r = lax.rsqrt(jnp.mean(x * x, axis=-1, keepdims=True) + EPS)
    return x * r * g, r


def _rms_bwd(x, g, r, dy):
    xh = x * r
    dyg = dy * g
    dx = r * (dyg - xh * jnp.mean(dyg * xh, axis=-1, keepdims=True))
    return dx, jnp.sum(dy * xh, axis=0, keepdims=True)


def _swap16(x):
    n = x.shape[-1]
    up = pltpu.roll(x, n - 16, 1)
    down = pltpu.roll(x, 16, 1)
    lane = lax.broadcasted_iota(jnp.int32, x.shape, 1) % HEAD_BLOCK
    return jnp.where((lane >= 64) & (lane < 80), up, jnp.where((lane >= 80) & (lane < 96), down, 0.0))


def _const_spec(shape):
    return pl.BlockSpec(shape, lambda *_: (0,) * len(shape), pipeline_mode=pl.Buffered(1))


def _acc_spec(shape):
    return pl.BlockSpec(shape, lambda *_: (0,) * len(shape))


def _row_spec(tm, width):
    return pl.BlockSpec((tm, width), lambda i: (i, 0))


def _cast_bf16(arrays):
    n = len(arrays)

    def body(*refs):
        for src, dst in zip(refs[:n], refs[n:]):
            dst[...] = src[...].astype(BF16)

    return pl.pallas_call(
        body, name="cast_weights",
        out_shape=[jax.ShapeDtypeStruct(a.shape, BF16) for a in arrays],
        compiler_params=_cparams(),
    )(*arrays)


def _all_gather(arrays, name):
    n = len(arrays)

    def body(*refs):
        ins, outs = refs[:n], refs[n:2 * n]
        send_sems, recv_sems, local_sems = refs[2 * n:]
        x, y, c = lax.axis_index("x"), lax.axis_index("y"), lax.axis_index("c")
        me, sibling = (x, y, c), (x, y, 1 - c)
        chips = [(1 - x, y), (x, 1 - y), (1 - x, 1 - y)]

        def slot(a, dev):
            return outs[a].at[4 * dev[0] + 2 * dev[1] + dev[2]]

        def copy(a, k, block, to, src=None):
            return pltpu.make_async_remote_copy(
                src_ref=slot(a, block) if src is None else src, dst_ref=slot(a, block),
                send_sem=send_sems.at[a, k], recv_sem=recv_sems.at[a, k], device_id=to, device_id_type=MESH)

        mine = [pltpu.make_async_copy(ins[a], slot(a, me), local_sems.at[a]) for a in range(n)]
        for cp in mine:
            cp.start()
        first = []
        for a in range(n):
            first.append(copy(a, 0, me, sibling, src=ins[a]))
            first += [copy(a, 1 + j, me, (*chip, c), src=ins[a]) for j, chip in enumerate(chips)]
        for cp in first:
            cp.start()
        passed = []
        for j, chip in enumerate(chips):
            for a in range(n):
                copy(a, 1 + j, (*chip, c), me).wait_recv()
                cp = copy(a, 4 + j, (*chip, c), sibling)
                cp.start()
                passed.append(cp)
        for a in range(n):
            copy(a, 0, sibling, me).wait_recv()
            for j, chip in enumerate(chips):
                copy(a, 4 + j, (*chip, 1 - c), me).wait_recv()
        for cp in first + passed:
            cp.wait_send()
        for cp in mine:
            cp.wait()

    any_spec = pl.BlockSpec(memory_space=pl.ANY)
    return pl.pallas_call(
        body, name=name,
        out_shape=[jax.ShapeDtypeStruct((N_DEV,) + a.shape, a.dtype) for a in arrays],
        in_specs=[any_spec] * n, out_specs=[any_spec] * n,
        scratch_shapes=[pltpu.SemaphoreType.DMA((n, 7)), pltpu.SemaphoreType.DMA((n, 7)), pltpu.SemaphoreType.DMA((n,))],
    )(*arrays)


def _exchange_shapes(jobs):
    dsts = [jax.ShapeDtypeStruct(((N_DEV,) + a.shape) if kind == "gather" else a.shape, a.dtype) for kind, a in jobs]
    n = len(jobs)
    sems = [pltpu.SemaphoreType.DMA((n, N_DEV - 1)), pltpu.SemaphoreType.DMA((n, N_DEV - 1)), pltpu.SemaphoreType.DMA((n,))]
    return dsts, sems


def _exchange_copies(kinds, srcs, dsts, send_sems, recv_sems, local_sems):
    x, y, c = lax.axis_index("x"), lax.axis_index("y"), lax.axis_index("c")
    me = 4 * x + 2 * y + c
    remote, local = [], []
    for a, kind in enumerate(kinds):
        for k in range(1, N_DEV):
            peer = (1 - x if k & 4 else x, 1 - y if k & 2 else y, 1 - c if k & 1 else c)
            src = srcs[a] if kind == "gather" else srcs[a].at[4 * peer[0] + 2 * peer[1] + peer[2]]
            remote.append(pltpu.make_async_remote_copy(
                src_ref=src, dst_ref=dsts[a].at[me], send_sem=send_sems.at[a, k - 1], recv_sem=recv_sems.at[a, k - 1],
                device_id=peer, device_id_type=MESH))
        src = srcs[a] if kind == "gather" else srcs[a].at[me]
        local.append(pltpu.make_async_copy(src, dsts[a].at[me], local_sems.at[a]))
    return remote, local


def _scatter_copies(srcs, lands, send_sems, recv_sems):
    x, y, c = lax.axis_index("x"), lax.axis_index("y"), lax.axis_index("c")
    me = 4 * x + 2 * y + c
    copies = []
    for a, (src_ref, land_ref) in enumerate(zip(srcs, lands)):
        for k in range(1, N_DEV):
            peer = (1 - x if k & 4 else x, 1 - y if k & 2 else y, 1 - c if k & 1 else c)
            copies.append(pltpu.make_async_remote_copy(
                src_ref=src_ref.at[4 * peer[0] + 2 * peer[1] + peer[2]], dst_ref=land_ref.at[me],
                send_sem=send_sems.at[a * (N_DEV - 1) + k - 1], recv_sem=recv_sems.at[a * (N_DEV - 1) + k - 1],
                device_id=peer, device_id_type=MESH))
    return copies


_SPLIT_EFFECT = pltpu.SideEffectType.DATAFLOW_SIDE_EFFECTING


def _scatter_start(blocks, name):
    n = len(blocks)
    hbm = pl.BlockSpec(memory_space=pltpu.HBM)
    sem = pl.BlockSpec(memory_space=pltpu.SEMAPHORE)

    def body(*refs):
        srcs, lands, send_sems, recv_sems, token = refs[:n], refs[n:2 * n], refs[2 * n], refs[2 * n + 1], refs[-1]
        for cp in _scatter_copies(srcs, lands, send_sems, recv_sems):
            cp.start()
        token[...] = jnp.zeros_like(token)

    through = [pltpu.HBM(b.shape, b.dtype) for b in blocks]
    *handle, token = pl.pallas_call(
        body, name=name,
        out_shape=(pltpu.SemaphoreType.DMA((n * (N_DEV - 1),)), pltpu.SemaphoreType.DMA((n * (N_DEV - 1),)), *through, *through,
                   jax.ShapeDtypeStruct((8, 128), F32)),
        in_specs=(hbm,) * (2 * n), out_specs=(sem, sem) + (hbm,) * (2 * n) + (pl.BlockSpec(memory_space=pltpu.VMEM),),
        input_output_aliases={i: 2 + i for i in range(2 * n)},
        compiler_params=pltpu.CompilerParams(has_side_effects=_SPLIT_EFFECT),
    )(*[pltpu.with_memory_space_constraint(b, pltpu.HBM) for b in blocks],
      *[pltpu.with_memory_space_constraint(lax.empty(b.shape, b.dtype), pltpu.HBM) for b in blocks])
    return handle, token


def _scatter_wait(handle, after, name):
    hbm = pl.BlockSpec(memory_space=pltpu.HBM)
    sem = pl.BlockSpec(memory_space=pltpu.SEMAPHORE)
    send_sems, recv_sems, *through = handle
    n = len(through) // 2

    def body(*refs):
        for cp in _scatter_copies(refs[:n], refs[n:2 * n], refs[2 * n], refs[2 * n + 1]):
            cp.wait_send()
            cp.wait_recv()

    outs = pl.pallas_call(
        body, name=name, out_shape=tuple(pltpu.HBM(t.shape, t.dtype) for t in through),
        in_specs=(hbm,) * (2 * n) + (sem, sem, pl.BlockSpec(memory_space=pl.ANY)), out_specs=(hbm,) * (2 * n),
        input_output_aliases={i: i for i in range(2 * n)},
        compiler_params=pltpu.CompilerParams(has_side_effects=_SPLIT_EFFECT),
    )(*through, send_sems, recv_sems, after)
    return list(outs[n:])


def _pcall(body, *, name, grid, in_specs, out_specs, out_shape, args, scratch_shapes=(), exchange=None, after=None):
    semantics = ("arbitrary",) * len(grid)
    exchange = exchange or []
    extra = [] if after is None else [after]
    if not exchange and not extra:
        outs = pl.pallas_call(body, name=name, grid=grid, in_specs=list(in_specs), out_specs=list(out_specs),
                              out_shape=list(out_shape), scratch_shapes=list(scratch_shapes),
                              compiler_params=_cparams(semantics))(*args)
        return list(outs), []
    kinds = [kind for kind, _ in exchange]
    n_in, n_out, n_job = len(in_specs), len(out_specs), len(exchange)
    dst_shapes, sems = _exchange_shapes(exchange) if exchange else ([], [])

    def wrapped(*refs):
        ins, srcs = refs[:n_in], refs[n_in:n_in + n_job]
        o0 = n_in + n_job + len(extra)
        outs, dsts, rest = refs[o0:o0 + n_out], refs[o0 + n_out:o0 + n_out + n_job], refs[o0 + n_out + n_job:]
        if not exchange:
            body(*ins, *outs, *rest)
            return
        scratch, sem_refs = rest[:-3], rest[-3:]
        first = functools.reduce(jnp.logical_and, [pl.program_id(d) == 0 for d in range(len(grid))])
        last = functools.reduce(jnp.logical_and, [pl.program_id(d) == grid[d] - 1 for d in range(len(grid))])

        @pl.when(first)
        def _():
            remote, local = _exchange_copies(kinds, srcs, dsts, *sem_refs)
            for cp in local + remote:
                cp.start()

        body(*ins, *outs, *scratch)

        @pl.when(last)
        def _():
            remote, local = _exchange_copies(kinds, srcs, dsts, *sem_refs)
            for cp in remote + local:
                cp.wait()

    any_spec = pl.BlockSpec(memory_space=pl.ANY)
    outs = pl.pallas_call(
        wrapped, name=name, grid=grid, in_specs=list(in_specs) + [any_spec] * (n_job + len(extra)),
        out_specs=list(out_specs) + [any_spec] * n_job, out_shape=list(out_shape) + dst_shapes,
        scratch_shapes=list(scratch_shapes) + sems, compiler_params=_cparams(semantics),
    )(*args, *[a for _, a in exchange], *extra)
    return list(outs[:n_out]), list(outs[n_out:])


def _adam_math(w, g, m, v):
    m = ADAM_B1 * m + (1.0 - ADAM_B1) * g
    v = ADAM_B2 * v + (1.0 - ADAM_B2) * (g * g)
    m_hat = m / (1.0 - ADAM_B1 ** ADAM_STEP)
    v_hat = v / (1.0 - ADAM_B2 ** ADAM_STEP)
    delta = -ADAM_LR * (m_hat / (jnp.sqrt(v_hat) + ADAM_EPS) + ADAM_WD * w)
    return delta, m, v


def _adam_big(w, m, v, own32, recv16, me, after):
    r, c = w.shape
    tr = _pick(r, (176,))
    assert r % tr == 0

    def body(me_ref, w_ref, m_ref, v_ref, s_ref, r_ref, after_ref, g_out, d_out, m_out, v_out):
        g = s_ref[...]
        for k in range(1, N_DEV):
            g = g + r_ref[jnp.bitwise_xor(me_ref[0], k)].astype(F32)
        d, mn, vn = _adam_math(w_ref[...], g, m_ref[...], v_ref[...])
        g_out[...] = g
        d_out[...] = d
        m_out[...] = mn
        v_out[...] = vn

    blk = pl.BlockSpec((tr, c), lambda i, s: (i, 0))
    grid_spec = pltpu.PrefetchScalarGridSpec(
        num_scalar_prefetch=1, grid=(r // tr,),
        in_specs=[blk, blk, blk, pl.BlockSpec((None, tr, c), lambda i, s: (s[0], i, 0)),
                  pl.BlockSpec((N_DEV, tr, c), lambda i, s: (0, i, 0)), pl.BlockSpec(memory_space=pl.ANY)],
        out_specs=[blk, blk, blk, blk])
    return pl.pallas_call(
        body, name="adamw_big", grid_spec=grid_spec,
        out_shape=[jax.ShapeDtypeStruct((r, c), F32)] * 4,
        compiler_params=_cparams(("arbitrary",)),
    )(me, w, m, v, own32, recv16, after)


def _adam_small(ws, ms, vs, partials, loss_parts):
    n = len(ws)

    def body(*refs):
        w_refs, m_refs, v_refs, p_refs = (refs[i * n:(i + 1) * n] for i in range(4))
        l_ref, loss_out = refs[4 * n], refs[4 * n + 1]
        outs = refs[4 * n + 2:]

        def total(ref):
            s = ref[0]
            for dev in range(1, N_DEV):
                s = s + ref[dev]
            return s

        loss_out[...] = total(l_ref)
        for i in range(n):
            r, c = w_refs[i].shape
            g = total(p_refs[i])[:, :c]
            d, mn, vn = _adam_math(w_refs[i][...], g, m_refs[i][...], v_refs[i][...])
            for j, val in enumerate((g, d, mn, vn)):
                outs[4 * i + j][...] = val

    shapes = [jax.ShapeDtypeStruct((1, 128), F32)]
    for w in ws:
        shapes += [jax.ShapeDtypeStruct(w.shape, F32)] * 4
    outs = pl.pallas_call(body, name="adamw_small", out_shape=shapes, compiler_params=_cparams())(*ws, *ms, *vs, *partials, loss_parts)
    return outs[0], [outs[1 + 4 * i:5 + 4 * i] for i in range(n)]


def _pick(n, prefer):
    for t in prefer:
        if n % t == 0:
            return t
    return n


FFN_FWD_GROUP = 512


def _ffn_fwd(x, g, wg, wu, wd, name, exchange=None, loss_head=None):
    T, D = x.shape
    F = wg.shape[0]
    tm = _pick(T, (512, 256, 128))
    fc = _pick(F, (256, 128))

    def body(x_ref, g_ref, wg_ref, wu_ref, wd_ref, *rest):
        h_ref, n_ref, gate_ref, up_ref, act_ref = rest[-7:-2] if loss_head else rest
        xv = x_ref[...]
        n, _ = _rms_fwd(xv, g_ref[...])
        nb = n.astype(BF16)
        n_ref[...] = nb
        acc = jnp.zeros((tm, D), F32)
        for g0 in range(0, F, FFN_FWD_GROUP):
            g1 = min(g0 + FFN_FWD_GROUP, F)
            for f0 in range(g0, g1, fc):
                f1 = min(f0 + fc, g1)
                gate = _dot_nt(nb, wg_ref[f0:f1, :])
                up = _dot_nt(nb, wu_ref[f0:f1, :])
                gate_ref[:, f0:f1] = gate.astype(BF16)
                up_ref[:, f0:f1] = up.astype(BF16)
                act_ref[:, f0:f1] = (gate * (1.0 / (1.0 + jnp.exp(-gate))) * up).astype(BF16)
            acc = acc + _dot(act_ref[:, g0:g1], wd_ref[g0:g1, :])
        hv = xv + 0.5 * acc
        if not loss_head:
            h_ref[...] = hv
            return
        t_ref, gf_ref, loss_ref, dgf_ref = rest[0], rest[1], rest[-2], rest[-1]
        gf = gf_ref[...]
        y, r = _rms_fwd(hv, gf)
        diff = y - t_ref[...]
        dh, dgp = _rms_bwd(hv, gf, r, diff * (1.0 / D))
        h_ref[...] = dh
        part = 0.5 * jnp.sum(jnp.sum(diff * diff, axis=-1, keepdims=True) * (1.0 / D), axis=0, keepdims=True)

        @pl.when(pl.program_id(0) == 0)
        def _():
            loss_ref[...] = jnp.zeros_like(loss_ref)
            dgf_ref[...] = jnp.zeros_like(dgf_ref)

        loss_ref[...] += jnp.broadcast_to(part, loss_ref.shape)
        dgf_ref[...] += dgp

    in_specs = [_row_spec(tm, D), _const_spec((1, D)), _const_spec((F, D)), _const_spec((F, D)), _const_spec((F, D))]
    out_specs = [_row_spec(tm, D), _row_spec(tm, D), _row_spec(tm, F), _row_spec(tm, F), _row_spec(tm, F)]
    out_shape = [jax.ShapeDtypeStruct((T, D), F32), jax.ShapeDtypeStruct((T, D), BF16),
                 jax.ShapeDtypeStruct((T, F), BF16), jax.ShapeDtypeStruct((T, F), BF16), jax.ShapeDtypeStruct((T, F), BF16)]
    args = (x, g, wg, wu, wd)
    if loss_head:
        in_specs += [_row_spec(tm, D), _const_spec((1, D))]
        out_specs += [_acc_spec((1, 128)), _acc_spec((1, D))]
        out_shape += [jax.ShapeDtypeStruct((1, 128), F32), jax.ShapeDtypeStruct((1, D), F32)]
        args += tuple(loss_head)
    return _pcall(body, name=name, grid=(T // tm,), in_specs=in_specs, out_specs=out_specs, out_shape=out_shape, args=args,
                  exchange=exchange)


FFN_BWD_CHUNK = 1408
FFN_BWD_GROUP = 1408


def _ffn_bwd(x, g, gate, up, dh, wgt, wut, wdt, name, exchange=None, after=None):
    T, D = x.shape
    F = wgt.shape[0]
    tm = _pick(T, (512, 256, 128))
    halves = 2
    Fh = F // halves
    assert Fh * halves == F and Fh % 128 == 0
    groups = [(g0, min(g0 + FFN_BWD_GROUP, Fh)) for g0 in range(0, Fh, FFN_BWD_GROUP)]

    def body(x_ref, g_ref, gate_ref, up_ref, dh_ref, wg_ref, wu_ref, wd_ref, dx_ref, dg_ref, dgate_ref, dup_ref, dn_ref):
        i, h = pl.program_id(0), pl.program_id(1)
        dhv = dh_ref[...]
        dho = (0.5 * dhv).astype(BF16)
        dn = jnp.zeros((tm, D), F32)
        for g0, g1 in groups:
            for f0 in range(g0, g1, FFN_BWD_CHUNK):
                f1 = min(f0 + FFN_BWD_CHUNK, g1)
                d_act = _dot_nt(dho, wd_ref[f0:f1, :])
                gt = gate_ref[:, f0:f1].astype(F32)
                u = up_ref[:, f0:f1].astype(F32)
                sg = 1.0 / (1.0 + jnp.exp(-gt))
                silu = gt * sg
                dup_ref[:, f0:f1] = (d_act * silu).astype(BF16)
                dgate_ref[:, f0:f1] = (d_act * u * (sg + silu * (1.0 - sg))).astype(BF16)
            dn = dn + _dot(dgate_ref[:, g0:g1], wg_ref[g0:g1, :]) + _dot(dup_ref[:, g0:g1], wu_ref[g0:g1, :])

        @pl.when(h == 0)
        def _():
            dn_ref[...] = dn

        @pl.when(h > 0)
        def _():
            dn_ref[...] += dn

        @pl.when((i == 0) & (h == 0))
        def _():
            dg_ref[...] = jnp.zeros_like(dg_ref)

        @pl.when(h == halves - 1)
        def _():
            xv, gv = x_ref[...], g_ref[...]
            _, r = _rms_fwd(xv, gv)
            dxn, dgp = _rms_bwd(xv, gv, r, dn_ref[...])
            dx_ref[...] = dhv + dxn
            dg_ref[...] += dgp

    tile = pl.BlockSpec((tm, D), lambda i, h: (i, 0))
    wide = pl.BlockSpec((tm, Fh), lambda i, h: (i, h))
    return _pcall(
        body, name=name, grid=(T // tm, halves),
        in_specs=[tile, _const_spec((1, D)), wide, wide, tile,
                  pl.BlockSpec((Fh, D), lambda i, h: (h, 0)), pl.BlockSpec((Fh, D), lambda i, h: (h, 0)),
                  pl.BlockSpec((Fh, D), lambda i, h: (h, 0))],
        out_specs=[tile, _acc_spec((1, D)), wide, wide],
        out_shape=[jax.ShapeDtypeStruct((T, D), F32), jax.ShapeDtypeStruct((1, D), F32),
                   jax.ShapeDtypeStruct((T, F), BF16), jax.ShapeDtypeStruct((T, F), BF16)],
        scratch_shapes=[pltpu.VMEM((tm, D), F32)],
        args=(x, g, gate, up, dh, wgt, wut, wdt), exchange=exchange, after=after)


def _matmul_tn(a, b, name, scale=None, exchange=None, after=None):
    T, K = a.shape
    N = b.shape[1]
    tk = _pick(K, (1024, 1408, 1280, 768, 512, 256, 128))
    tn = _pick(N, (1024, 1408, 1280, 768, 512, 256, 128))
    tt = _pick(T, (2048, 1024, 512, 256, 128))
    n_t = T // tt

    def body(a_ref, b_ref, o_ref, o16_ref):
        @pl.when(pl.program_id(2) == 0)
        def _():
            o_ref[...] = jnp.zeros_like(o_ref)

        o_ref[...] += _dot_tn(a_ref[...].astype(BF16), b_ref[...].astype(BF16))

        @pl.when(pl.program_id(2) == n_t - 1)
        def _():
            if scale is not None:
                o_ref[...] = o_ref[...] * scale
            o16_ref[...] = o_ref[...].astype(BF16)

    tile = pl.BlockSpec((tk, tn), lambda i, j, t: (i, j))
    outs, got = _pcall(
        body, name=name, grid=(K // tk, N // tn, n_t),
        in_specs=[pl.BlockSpec((tt, tk), lambda i, j, t: (t, i)), pl.BlockSpec((tt, tn), lambda i, j, t: (t, j))],
        out_specs=[tile, tile],
        out_shape=[jax.ShapeDtypeStruct((K, N), F32), jax.ShapeDtypeStruct((K, N), BF16)],
        args=(a, b), exchange=exchange, after=after)
    return tuple(outs), got


def _rope_spec(tm, S):
    per = S // tm
    return pl.BlockSpec((tm, HEAD_BLOCK), lambda i: (i % per, 0))


def _proj_fwd(h1, g_mix, w_in, g_q, wq, g_kv, wkv, rope_c, rope_s, S):
    T, D = h1.shape
    P = w_in.shape[0]
    tm = _pick(S, (512, 256, 128))
    QW = MLA_HEADS * HEAD_BLOCK
    VW = MLA_HEADS * V_HEAD
    SQ = SWA_HEADS * SWA_HEAD_DIM
    SK = SWA_KV_HEADS * SWA_HEAD_DIM
    o_cq, o_ckv, o_qs, o_ks, o_vs, o_kpe = 0, Q_LORA, Q_LORA + KV_LORA, Q_LORA + KV_LORA + SQ, Q_LORA + KV_LORA + SQ + SK, Q_LORA + KV_LORA + SQ + 2 * SK
    assert P == o_kpe + HEAD_BLOCK

    def body(h_ref, gm_ref, win_ref, gq_ref, wq_ref, gkv_ref, wkv_ref, c_ref, s_ref,
             u_ref, cq_ref, ckv_ref, cqn_ref, ckvn_ref, q_ref, k_ref, v_ref, qs_ref, ks_ref, vs_ref):
        u, _ = _rms_fwd(h_ref[...], gm_ref[...])
        ub = u.astype(BF16)
        u_ref[...] = ub
        proj = _dot_nt(ub, win_ref[...])
        c_q = proj[:, o_cq:o_ckv]
        c_kv = proj[:, o_ckv:o_qs]
        cq_ref[...] = c_q
        ckv_ref[...] = c_kv
        qs_ref[...] = proj[:, o_qs:o_ks].astype(BF16)
        ks_ref[...] = proj[:, o_ks:o_vs].astype(BF16)
        vs_ref[...] = proj[:, o_vs:o_kpe].astype(BF16)
        cb, sb = c_ref[...], s_ref[...]
        kpe = proj[:, o_kpe:P]
        kpe = kpe * cb + _swap16(kpe) * sb
        cqn, _ = _rms_fwd(c_q, gq_ref[...])
        cqn = cqn.astype(BF16)
        cqn_ref[...] = cqn
        q = _dot_nt(cqn, wq_ref[...])
        q = q * jnp.tile(cb, (1, MLA_HEADS)) + _swap16(q) * jnp.tile(sb, (1, MLA_HEADS))
        q_ref[...] = q.astype(BF16)
        ckvn, _ = _rms_fwd(c_kv, gkv_ref[...])
        ckvn = ckvn.astype(BF16)
        ckvn_ref[...] = ckvn
        kv = _dot_nt(ckvn, wkv_ref[...])
        k_ref[...] = (kv[:, :QW] + jnp.tile(kpe, (1, MLA_HEADS))).astype(BF16)
        v_ref[...] = kv[:, QW:].astype(BF16)

    widths = [(D, BF16), (Q_LORA, F32), (KV_LORA, F32), (Q_LORA, BF16), (KV_LORA, BF16), (QW, BF16), (QW, BF16), (VW, BF16),
              (SQ, BF16), (SK, BF16), (SK, BF16)]
    return pl.pallas_call(
        body, name="proj_fwd", grid=(T // tm,),
        in_specs=[_row_spec(tm, D), _const_spec((1, D)), _const_spec((P, D)), _const_spec((1, Q_LORA)),
                  _const_spec((QW, Q_LORA)), _const_spec((1, KV_LORA)), _const_spec((QW + VW, KV_LORA)),
                  _rope_spec(tm, S), _rope_spec(tm, S)],
        out_specs=[_row_spec(tm, w) for w, _ in widths],
        out_shape=[jax.ShapeDtypeStruct((T, w), dt) for w, dt in widths],
        compiler_params=_cparams(("arbitrary",)),
    )(h1, g_mix, w_in, g_q, wq, g_kv, wkv, rope_c, rope_s)


def _proj_bwd(dq, dk, dv, dqs, dks, dvs, c_q, c_kv, h1, dh2, g_mix, w_in, g_q, wq, g_kv, wkv, rope_c, rope_s, S):
    T, D = h1.shape
    P = w_in.shape[0]
    tm = _pick(S, (512, 256, 128))
    QW = MLA_HEADS * HEAD_BLOCK
    VW = MLA_HEADS * V_HEAD
    SQ = SWA_HEADS * SWA_HEAD_DIM
    SK = SWA_KV_HEADS * SWA_HEAD_DIM

    def body(dq_ref, dk_ref, dv_ref, dqs_ref, dks_ref, dvs_ref, cq_ref, ckv_ref, h_ref, dh2_ref, gm_ref, win_ref, gq_ref,
             wq_ref, gkv_ref, wkv_ref, c_ref, s_ref, dh1_ref, dproj_ref, dql_ref, dkv_ref, dgm_ref, dgq_ref, dgkv_ref):
        cb, sb = c_ref[...], s_ref[...]
        dqv = dq_ref[...]
        dql = dqv * jnp.tile(cb, (1, MLA_HEADS)) + _swap16(dqv * jnp.tile(sb, (1, MLA_HEADS)))
        dql = dql.astype(BF16)
        dql_ref[...] = dql
        c_q = cq_ref[...]
        _, rq = _rms_fwd(c_q, gq_ref[...])
        d_cq, dgq = _rms_bwd(c_q, gq_ref[...], rq, _dot(dql, wq_ref[...]))

        dkv_all = dk_ref[...]
        dkpe = dkv_all[:, 0:HEAD_BLOCK].astype(F32)
        for h in range(1, MLA_HEADS):
            dkpe = dkpe + dkv_all[:, h * HEAD_BLOCK:(h + 1) * HEAD_BLOCK].astype(F32)
        lane = lax.broadcasted_iota(jnp.int32, dkpe.shape, 1)
        dkpe = jnp.where((lane >= 64) & (lane < 96), dkpe, 0.0)
        dkpe = dkpe * cb + _swap16(dkpe * sb)
        dkvc = jnp.concatenate([dkv_all.astype(BF16), dv_ref[...].astype(BF16)], axis=1)
        dkv_ref[...] = dkvc
        c_kv = ckv_ref[...]
        _, rkv = _rms_fwd(c_kv, gkv_ref[...])
        d_ckv, dgkv = _rms_bwd(c_kv, gkv_ref[...], rkv, _dot(dkvc, wkv_ref[...]))

        dproj = jnp.concatenate([d_cq.astype(BF16), d_ckv.astype(BF16), dqs_ref[...].astype(BF16),
                                 dks_ref[...].astype(BF16), dvs_ref[...].astype(BF16), dkpe.astype(BF16)], axis=1)
        dproj_ref[...] = dproj
        hv = h_ref[...]
        _, rm = _rms_fwd(hv, gm_ref[...])
        dxn, dgm = _rms_bwd(hv, gm_ref[...], rm, _dot(dproj, win_ref[...]))
        dh1_ref[...] = dh2_ref[...] + dxn

        @pl.when(pl.program_id(0) == 0)
        def _():
            dgm_ref[...] = jnp.zeros_like(dgm_ref)
            dgq_ref[...] = jnp.zeros_like(dgq_ref)
            dgkv_ref[...] = jnp.zeros_like(dgkv_ref)

        dgm_ref[...] += dgm
        dgq_ref[...] += dgq
        dgkv_ref[...] += dgkv

    return pl.pallas_call(
        body, name="proj_bwd", grid=(T // tm,),
        in_specs=[_row_spec(tm, QW), _row_spec(tm, QW), _row_spec(tm, VW), _row_spec(tm, SQ), _row_spec(tm, SK), _row_spec(tm, SK),
                  _row_spec(tm, Q_LORA), _row_spec(tm, KV_LORA), _row_spec(tm, D), _row_spec(tm, D),
                  _const_spec((1, D)), _const_spec((P, D)), _const_spec((1, Q_LORA)), _const_spec((QW, Q_LORA)),
                  _const_spec((1, KV_LORA)), _const_spec((QW + VW, KV_LORA)), _rope_spec(tm, S), _rope_spec(tm, S)],
        out_specs=[_row_spec(tm, D), _row_spec(tm, P), _row_spec(tm, QW), _row_spec(tm, QW + VW),
                   _acc_spec((1, D)), _acc_spec((1, Q_LORA)), _acc_spec((1, KV_LORA))],
        out_shape=[jax.ShapeDtypeStruct((T, D), F32), jax.ShapeDtypeStruct((T, P), BF16), jax.ShapeDtypeStruct((T, QW), BF16),
                   jax.ShapeDtypeStruct((T, QW + VW), BF16), jax.ShapeDtypeStruct((1, D), F32),
                   jax.ShapeDtypeStruct((1, Q_LORA), F32), jax.ShapeDtypeStruct((1, KV_LORA), F32)],
        compiler_params=_cparams(("arbitrary",)),
    )(dq, dk, dv, dqs, dks, dvs, c_q, c_kv, h1, dh2, g_mix, w_in, g_q, wq, g_kv, wkv, rope_c, rope_s)


def _out_fwd(o_mla, o_swa, g_mla, g_swa, w_o, h1):
    T, D = h1.shape
    W = o_mla.shape[1]
    tm = _pick(T, (1024, 512, 256, 128))

    def body(om_ref, os_ref, gm_ref, gs_ref, wo_ref, h_ref, h2_ref, oc_ref):
        a, _ = _rms_fwd(om_ref[...], gm_ref[...])
        b, _ = _rms_fwd(os_ref[...], gs_ref[...])
        oc = jnp.concatenate([a.astype(BF16), b.astype(BF16)], axis=1)
        oc_ref[...] = oc
        h2_ref[...] = h_ref[...] + _dot(oc, wo_ref[...])

    return pl.pallas_call(
        body, name="out_fwd", grid=(T // tm,),
        in_specs=[_row_spec(tm, W), _row_spec(tm, W), _const_spec((1, W)), _const_spec((1, W)), _const_spec((2 * W, D)),
                  _row_spec(tm, D)],
        out_specs=[_row_spec(tm, D), _row_spec(tm, 2 * W)],
        out_shape=[jax.ShapeDtypeStruct((T, D), F32), jax.ShapeDtypeStruct((T, 2 * W), BF16)],
        compiler_params=_cparams(("arbitrary",)),
    )(o_mla, o_swa, g_mla, g_swa, w_o, h1)


def _out_bwd(dh2, o_mla, o_swa, g_mla, g_swa, w_o):
    T, D = dh2.shape
    W = o_mla.shape[1]
    tm = _pick(T, (1024, 512, 256, 128))

    def body(dh_ref, om_ref, os_ref, gm_ref, gs_ref, wo_ref, dom_ref, dos_ref, dgm_ref, dgs_ref):
        doc = _dot_nt(dh_ref[...].astype(BF16), wo_ref[...])
        om, osw = om_ref[...], os_ref[...]
        _, ra = _rms_fwd(om, gm_ref[...])
        _, rb = _rms_fwd(osw, gs_ref[...])
        da, dga = _rms_bwd(om, gm_ref[...], ra, doc[:, :W])
        db, dgb = _rms_bwd(osw, gs_ref[...], rb, doc[:, W:])
        dom_ref[...] = da
        dos_ref[...] = db

        @pl.when(pl.program_id(0) == 0)
        def _():
            dgm_ref[...] = jnp.zeros_like(dgm_ref)
            dgs_ref[...] = jnp.zeros_like(dgs_ref)

        dgm_ref[...] += dga
        dgs_ref[...] += dgb

    return pl.pallas_call(
        body, name="out_bwd", grid=(T // tm,),
        in_specs=[_row_spec(tm, D), _row_spec(tm, W), _row_spec(tm, W), _const_spec((1, W)), _const_spec((1, W)),
                  _const_spec((2 * W, D))],
        out_specs=[_row_spec(tm, W), _row_spec(tm, W), _acc_spec((1, W)), _acc_spec((1, W))],
        out_shape=[jax.ShapeDtypeStruct((T, W), F32), jax.ShapeDtypeStruct((T, W), F32),
                   jax.ShapeDtypeStruct((1, W), F32), jax.ShapeDtypeStruct((1, W), F32)],
        compiler_params=_cparams(("arbitrary",)),
    )(dh2, o_mla, o_swa, g_mla, g_swa, w_o)


def _half_mask(shape, half):
    lane = lax.broadcasted_iota(jnp.int32, shape, len(shape) - 1)
    return (lane < 64) if half == 0 else (lane >= 64)


def _mla_fwd(q, k, v, B, S, exchange=None):
    T = B * S
    tq = _pick(S, MLA_Q_TILES)
    nq = S // tq
    pairs = MLA_HEADS // 2

    def body(q_ref, k_ref, v_ref, o_ref, lse_ref):
        causal = lax.broadcasted_iota(jnp.int32, (tq, tq), 1) <= lax.broadcasted_iota(jnp.int32, (tq, tq), 0)
        low = _half_mask((tq, 2 * V_HEAD), 0)
        for qi in range(nq):
            rows = slice(qi * tq, (qi + 1) * tq)
            past = slice(0, qi * tq)
            outs, lses = [], []
            for half in range(2):
                lanes = slice(half * HEAD_BLOCK, (half + 1) * HEAD_BLOCK)
                qv = q_ref[rows, lanes]
                s_d = jnp.where(causal, _dot_nt(qv, k_ref[rows, lanes]) * MLA_SCALE_LOG2, NEG)
                m = jnp.max(s_d, axis=-1, keepdims=True)
                if qi:
                    s_p = _dot_nt(qv, k_ref[past, lanes]) * MLA_SCALE_LOG2
                    m = jnp.maximum(m, jnp.max(s_p, axis=-1, keepdims=True))
                p_d = jnp.exp2(s_d - m)
                l = jnp.sum(p_d, axis=-1, keepdims=True)
                acc = _dot(p_d.astype(BF16), v_ref[rows, :])
                if qi:
                    p_p = jnp.exp2(s_p - m)
                    l = l + jnp.sum(p_p, axis=-1, keepdims=True)
                    acc = acc + _dot(p_p.astype(BF16), v_ref[past, :])
                outs.append(acc * (1.0 / l))
                lses.append(jnp.broadcast_to(m + jnp.log2(l), (tq, 2 * V_HEAD)))
            o_ref[rows, :] = jnp.where(low, outs[0], outs[1])
            lse_ref[rows, :] = jnp.where(low, lses[0], lses[1])

    blk2 = pl.BlockSpec((S, 2 * HEAD_BLOCK), lambda b, j: (b, j))
    blk1 = pl.BlockSpec((S, 2 * V_HEAD), lambda b, j: (b, j))
    return _pcall(
        body, name="mla_fwd", grid=(B, pairs),
        in_specs=[blk2, blk2, blk1], out_specs=[blk1, blk1],
        out_shape=[jax.ShapeDtypeStruct((T, pairs * 2 * V_HEAD), F32)] * 2,
        args=(q, k, v), exchange=exchange)


def _mla_bwd(q, k, v, do, o, lse, B, S, exchange=None):
    T = B * S
    tq = _pick(S, MLA_Q_TILES)
    nq = S // tq
    pairs = MLA_HEADS // 2

    def body(q_ref, k_ref, v_ref, do_ref, o_ref, lse_ref, dq_ref, dk_ref, dv_ref, dkt_acc, dvt_acc):
        causal = lax.broadcasted_iota(jnp.int32, (tq, tq), 1) <= lax.broadcasted_iota(jnp.int32, (tq, tq), 0)
        for half in range(2):
            lanes = slice(half * HEAD_BLOCK, (half + 1) * HEAD_BLOCK)
            col = half * V_HEAD
            own = _half_mask((tq, 2 * V_HEAD), half)
            for qi in reversed(range(nq)):
                rows = slice(qi * tq, (qi + 1) * tq)
                past = slice(0, qi * tq)
                first = qi == nq - 1
                qv = q_ref[rows, lanes]
                dov = jnp.where(own, do_ref[rows, :], 0.0)
                dsum = jnp.sum(dov * o_ref[rows, :], axis=-1, keepdims=True)
                dob = dov.astype(BF16)
                lse_col = lse_ref[rows, col:col + 1]

                def grads(keys, mask):
                    kv, vv = k_ref[keys, lanes], v_ref[keys, :]
                    p = jnp.exp2(_dot_nt(qv, kv) * MLA_SCALE_LOG2 - lse_col)
                    if mask is not None:
                        p = jnp.where(mask, p, 0.0)
                    ds = (p * (_dot_nt(dob, vv) - dsum) * MLA_SCALE).astype(BF16)
                    dkt, dvt = _dot_tn(qv, ds), _dot_tn(dob, p.astype(BF16))
                    if first:
                        dkt_acc[:, keys] = dkt
                    else:
                        dkt_acc[:, keys] += dkt
                    if first and half == 0:
                        dvt_acc[:, keys] = dvt
                    else:
                        dvt_acc[:, keys] += dvt
                    return _dot(ds, kv)

                dq = grads(rows, causal)
                if qi:
                    dq = dq + grads(past, None)
                dq_ref[rows, lanes] = dq.astype(BF16)
            dk_ref[:, lanes] = dkt_acc[...].T.astype(BF16)
        dv_ref[...] = dvt_acc[...].T.astype(BF16)

    blk2 = pl.BlockSpec((S, 2 * HEAD_BLOCK), lambda b, j: (b, j))
    blk1 = pl.BlockSpec((S, 2 * V_HEAD), lambda b, j: (b, j))
    return _pcall(
        body, name="mla_bwd", grid=(B, pairs),
        in_specs=[blk2, blk2, blk1, blk1, blk1, blk1], out_specs=[blk2, blk2, blk1],
        out_shape=[jax.ShapeDtypeStruct((T, MLA_HEADS * HEAD_BLOCK), BF16), jax.ShapeDtypeStruct((T, MLA_HEADS * HEAD_BLOCK), BF16),
                   jax.ShapeDtypeStruct((T, pairs * 2 * V_HEAD), BF16)],
        scratch_shapes=[pltpu.VMEM((HEAD_BLOCK, S), F32), pltpu.VMEM((2 * V_HEAD, S), F32)],
        args=(q, k, v, do, o, lse), exchange=exchange)


SWA_GROUP = SWA_HEADS // SWA_KV_HEADS
SWA_ROWS = SWA_GROUP * WINDOW


SWA_CHUNK = 8


def _bdot(a, b, ca, cb):
    return lax.dot_general(a, b, (((ca,), (cb,)), ((0,), (0,))), preferred_element_type=F32)


def _swa_chunks(nb):
    cb = next(c for c in (SWA_CHUNK, 4, 2, 1) if nb % c == 0)
    return cb, [(n0, slice(n0 * WINDOW, (n0 + cb) * WINDOW)) for n0 in range(0, nb, cb)]


def _swa_keys(ref, n0, cb):
    W = WINDOW
    cur = ref[n0 * W:(n0 + cb) * W, :]
    if n0:
        prev = ref[(n0 - 1) * W:(n0 + cb - 1) * W, :]
    else:
        prev = jnp.concatenate([cur[:W], cur[:(cb - 1) * W]], axis=0) if cb > 1 else cur
    return jnp.concatenate([prev.reshape(cb, W, W), cur.reshape(cb, W, W)], axis=1)


def _swa_stack(ref, rows, cb):
    return jnp.concatenate([ref[rows, j * WINDOW:(j + 1) * WINDOW].reshape(cb, WINDOW, WINDOW) for j in range(SWA_GROUP)], axis=1)


def _swa_unstack(low, high, cb):
    sel = _half_mask((cb, WINDOW, WINDOW), 0)
    return jnp.concatenate([jnp.where(sel, low[:, j * WINDOW:(j + 1) * WINDOW], high[:, j * WINDOW:(j + 1) * WINDOW])
                            .reshape(cb * WINDOW, WINDOW) for j in range(SWA_GROUP)], axis=1)


def _swa_sink_col(sink_ref, g):
    return jnp.concatenate([jnp.broadcast_to(sink_ref[g * SWA_GROUP + j:g * SWA_GROUP + j + 1, 0:1], (WINDOW, 1))
                            for j in range(SWA_GROUP)], axis=0)


def _swa_probs(qm, kcat, bias_g, sink, first_chunk):
    shape = (qm.shape[0], SWA_ROWS, 2 * WINDOW)
    qi = lax.broadcasted_iota(jnp.int32, shape, 1) % WINDOW
    col = lax.broadcasted_iota(jnp.int32, shape, 2)
    valid = (col > qi) & (col <= qi + WINDOW)
    if first_chunk:
        valid = valid & ((col >= WINDOW) | (lax.broadcasted_iota(jnp.int32, shape, 0) > 0))
    s = jnp.where(valid, _bdot(qm, kcat, 2, 2) * SWA_SCALE + bias_g, NEG)
    m = jnp.maximum(jnp.max(s, axis=-1, keepdims=True), sink)
    e = jnp.exp(s - m)
    e_s = jnp.exp(sink - m)
    inv = 1.0 / (jnp.sum(e, axis=-1, keepdims=True) + e_s)
    return e * inv, e_s * inv


def _swa_fwd(q, k, v, bias, sinks, B, S):
    T = B * S
    W = WINDOW
    nb = S // W

    cb, chunks = _swa_chunks(nb)

    def body(q_ref, k_ref, v_ref, bias_ref, sink_ref, o_ref):
        sink_cols = [_swa_sink_col(sink_ref, g) for g in range(SWA_KV_HEADS)]
        for n0, rows in chunks:
            kcat, vcat = _swa_keys(k_ref, n0, cb), _swa_keys(v_ref, n0, cb)
            qcat = _swa_stack(q_ref, rows, cb)
            outs = []
            for g in range(SWA_KV_HEADS):
                qm = jnp.where(_half_mask(qcat.shape, g), qcat, jnp.zeros_like(qcat))
                p, _ = _swa_probs(qm, kcat, bias_ref[g], sink_cols[g], n0 == 0)
                outs.append(_bdot(p.astype(BF16), vcat, 2, 1))
            o_ref[rows, :] = _swa_unstack(outs[0], outs[1], cb)

    SQ, SK = SWA_HEADS * SWA_HEAD_DIM, SWA_KV_HEADS * SWA_HEAD_DIM
    return pl.pallas_call(
        body, name="swa_fwd", grid=(B,),
        in_specs=[pl.BlockSpec((S, SQ), lambda b: (b, 0)), pl.BlockSpec((S, SK), lambda b: (b, 0)),
                  pl.BlockSpec((S, SK), lambda b: (b, 0)), _const_spec((SWA_KV_HEADS, SWA_ROWS, 2 * W)),
                  _const_spec((SWA_HEADS, 128))],
        out_specs=pl.BlockSpec((S, SQ), lambda b: (b, 0)),
        out_shape=jax.ShapeDtypeStruct((T, SQ), F32),
        compiler_params=_cparams(("arbitrary",)),
    )(q, k, v, bias, sinks)


def _swa_bwd(q, k, v, do, bias, sinks, B, S):
    T = B * S
    W = WINDOW
    nb = S // W
    cb, chunks = _swa_chunks(nb)

    def body(q_ref, k_ref, v_ref, do_ref, bias_ref, sink_ref, dq_ref, dk_ref, dv_ref, dbias_ref, dsink_ref, dkt_acc, dvt_acc):
        @pl.when(pl.program_id(0) == 0)
        def _():
            dbias_ref[...] = jnp.zeros_like(dbias_ref)
            dsink_ref[...] = jnp.zeros_like(dsink_ref)

        dkt_acc[...] = jnp.zeros_like(dkt_acc)
        dvt_acc[...] = jnp.zeros_like(dvt_acc)
        sink_cols = [_swa_sink_col(sink_ref, g) for g in range(SWA_KV_HEADS)]
        for n0, rows in chunks:
            kcat, vcat = _swa_keys(k_ref, n0, cb), _swa_keys(v_ref, n0, cb)
            qcat = _swa_stack(q_ref, rows, cb)
            docat = _swa_stack(do_ref, rows, cb)
            dkt = jnp.zeros((cb, W, 2 * W), F32)
            dvt = jnp.zeros((cb, W, 2 * W), F32)
            dqs = []
            for g in range(SWA_KV_HEADS):
                own = _half_mask(qcat.shape, g)
                qm = jnp.where(own, qcat, jnp.zeros_like(qcat))
                dom = jnp.where(own, docat, 0.0).astype(BF16)
                p, p_s = _swa_probs(qm, kcat, bias_ref[g], sink_cols[g], n0 == 0)
                dp = _bdot(dom, vcat, 2, 2)
                dsum = jnp.sum(p * dp, axis=-1, keepdims=True)
                ds = p * (dp - dsum)
                dbias_ref[g] += jnp.sum(ds, axis=0)
                dsink_ref[g] += jnp.broadcast_to(-jnp.sum(p_s * dsum, axis=0), (SWA_ROWS, 128))
                dsb = (ds * SWA_SCALE).astype(BF16)
                dqs.append(_bdot(dsb, kcat, 2, 1))
                dkt = dkt + _bdot(qm, dsb, 1, 1)
                dvt = dvt + _bdot(dom, p.astype(BF16), 1, 1)
            dq_ref[rows, :] = _swa_unstack(dqs[0], dqs[1], cb).astype(BF16)
            for i in range(cb):
                n = n0 + i
                if n:
                    cols = slice((n - 1) * W, (n + 1) * W)
                    dkt_acc[:, cols] += dkt[i]
                    dvt_acc[:, cols] += dvt[i]
                else:
                    dkt_acc[:, :W] += dkt[i][:, W:]
                    dvt_acc[:, :W] += dvt[i][:, W:]
        dk_ref[...] = dkt_acc[...].T.astype(BF16)
        dv_ref[...] = dvt_acc[...].T.astype(BF16)

    SQ, SK = SWA_HEADS * SWA_HEAD_DIM, SWA_KV_HEADS * SWA_HEAD_DIM
    row = lambda w: pl.BlockSpec((S, w), lambda b: (b, 0))
    return pl.pallas_call(
        body, name="swa_bwd", grid=(B,),
        in_specs=[row(SQ), row(SK), row(SK), row(SQ), _const_spec((SWA_KV_HEADS, SWA_ROWS, 2 * W)), _const_spec((SWA_HEADS, 128))],
        out_specs=[row(SQ), row(SK), row(SK), _acc_spec((SWA_KV_HEADS, SWA_ROWS, 2 * W)), _acc_spec((SWA_KV_HEADS, SWA_ROWS, 128))],
        out_shape=[jax.ShapeDtypeStruct((T, SQ), BF16), jax.ShapeDtypeStruct((T, SK), BF16), jax.ShapeDtypeStruct((T, SK), BF16),
                   jax.ShapeDtypeStruct((SWA_KV_HEADS, SWA_ROWS, 2 * W), F32), jax.ShapeDtypeStruct((SWA_KV_HEADS, SWA_ROWS, 128), F32)],
        scratch_shapes=[pltpu.VMEM((SK, S), F32), pltpu.VMEM((SK, S), F32)],
        compiler_params=_cparams(("arbitrary",)),
    )(q, k, v, do, bias, sinks)


def _bias_expand(rel_bias, bucket):
    W = WINDOW

    def body(rb_ref, bucket_ref, out_ref):
        bk = bucket_ref[...]
        for h in range(SWA_HEADS):
            def add(b, acc):
                return jnp.where(bk == b, rb_ref[b, h], acc)

            out_ref[h] = lax.fori_loop(0, REL_BUCKETS, add, jnp.zeros((W, 2 * W), F32))

    return pl.pallas_call(
        body, name="bias_expand",
        in_specs=[pl.BlockSpec(memory_space=pltpu.SMEM), pl.BlockSpec(memory_space=pltpu.VMEM)],
        out_specs=pl.BlockSpec(memory_space=pltpu.VMEM),
        out_shape=jax.ShapeDtypeStruct((SWA_HEADS, W, 2 * W), F32),
        compiler_params=_cparams(),
    )(rel_bias, bucket)


def _bias_reduce(dbias, dsink_rows, bucket):
    def body(db_ref, ds_ref, bucket_ref, out_ref, sink_out_ref, rows_ref):
        sink_lane = lax.broadcasted_iota(jnp.int32, (1, 128), 1)
        sink_acc = jnp.zeros((1, 128), F32)
        for h in range(SWA_HEADS):
            sink_acc = jnp.where(sink_lane == h, jnp.sum(ds_ref[h], axis=0, keepdims=True), sink_acc)
        sink_out_ref[...] = sink_acc
        bk = bucket_ref[...]
        for h in range(SWA_HEADS):
            dbh = db_ref[h]

            def add(b, carry):
                rows_ref[pl.ds(h * REL_BUCKETS + b, 1), :] = jnp.sum(jnp.where(bk == b, dbh, 0.0), axis=0, keepdims=True)
                return carry

            lax.fori_loop(0, REL_BUCKETS, add, 0)
        totals = jnp.sum(rows_ref[...], axis=-1, keepdims=True)
        lane = lax.broadcasted_iota(jnp.int32, (REL_BUCKETS, 128), 1)
        acc = jnp.zeros((REL_BUCKETS, 128), F32)
        for h in range(SWA_HEADS):
            acc = jnp.where(lane == h, totals[h * REL_BUCKETS:(h + 1) * REL_BUCKETS], acc)
        out_ref[...] = acc

    return pl.pallas_call(
        body, name="bias_reduce",
        out_shape=[jax.ShapeDtypeStruct((REL_BUCKETS, 128), F32), jax.ShapeDtypeStruct((1, 128), F32)],
        scratch_shapes=[pltpu.VMEM((SWA_HEADS * REL_BUCKETS, 2 * WINDOW), F32)],
        compiler_params=_cparams(),
    )(dbias, dsink_rows, bucket)


def _t5_bucket(dist):
    n = jnp.maximum(dist, 0)
    max_exact = REL_BUCKETS // 2
    nf = jnp.maximum(n, 1).astype(F32)
    large = max_exact + (jnp.log(nf / max_exact) / math.log(REL_MAX_DIST / max_exact) * (REL_BUCKETS - max_exact)).astype(jnp.int32)
    large = jnp.minimum(large, REL_BUCKETS - 1)
    return jnp.where(n < max_exact, n, large)


def _pair_heads(a, axis, inverse=False):
    shp = a.shape
    a = a.reshape(shp[:axis] + ((4, 2, 64) if inverse else (2, 4, 64)) + shp[axis + 1:])
    a = jnp.swapaxes(a, axis, axis + 1)
    return a.reshape(shp)


def _owner_blocks(w32, w16):
    r, c = w32.shape[0] // N_DEV, w32.shape[1]
    return w32.reshape(N_DEV, r, c), w16.reshape(N_DEV, r, c)


_IN_WIDTHS = (Q_LORA, KV_LORA, QK_ROPE, SWA_HEADS * SWA_HEAD_DIM, SWA_KV_HEADS * SWA_HEAD_DIM, SWA_KV_HEADS * SWA_HEAD_DIM)


def _win_to_kernel(wt):
    parts, o = [], 0
    for wd in _IN_WIDTHS:
        parts.append(wt[o:o + wd])
        o += wd
    cq, ckv, kpe, qs, ks, vs = parts
    return jnp.concatenate([cq, ckv, _pair_heads(qs, 0), ks, vs, jnp.pad(kpe, ((64, 32), (0, 0)))], axis=0)


def _win_from_kernel(dwt):
    o = [0, Q_LORA, Q_LORA + KV_LORA, Q_LORA + KV_LORA + 512, Q_LORA + KV_LORA + 640, Q_LORA + KV_LORA + 768]
    cq, ckv, qs, ks, vs, kpe = [dwt[a:b] for a, b in zip(o, o[1:] + [dwt.shape[0]])]
    return jnp.concatenate([cq, ckv, kpe[64:96], _pair_heads(qs, 0, inverse=True), ks, vs], axis=0)


def _wq_to_kernel(wt):
    return jnp.pad(wt.reshape(MLA_HEADS, QK_NOPE + QK_ROPE, Q_LORA), ((0, 0), (0, 32), (0, 0))).reshape(MLA_HEADS * HEAD_BLOCK, Q_LORA)


def _wq_from_kernel(dwt):
    return dwt.reshape(MLA_HEADS, HEAD_BLOCK, Q_LORA)[:, :QK_NOPE + QK_ROPE].reshape(-1, Q_LORA)


def _wkv_to_kernel(wt):
    w3 = wt.reshape(MLA_HEADS, QK_NOPE + V_HEAD, KV_LORA)
    kpart = jnp.pad(w3[:, :QK_NOPE], ((0, 0), (0, 64), (0, 0))).reshape(MLA_HEADS * HEAD_BLOCK, KV_LORA)
    return jnp.concatenate([kpart, w3[:, QK_NOPE:].reshape(MLA_HEADS * V_HEAD, KV_LORA)], axis=0)


def _wkv_from_kernel(dwt):
    dk = dwt[:MLA_HEADS * HEAD_BLOCK].reshape(MLA_HEADS, HEAD_BLOCK, KV_LORA)[:, :QK_NOPE]
    dv = dwt[MLA_HEADS * HEAD_BLOCK:].reshape(MLA_HEADS, V_HEAD, KV_LORA)
    return jnp.concatenate([dk, dv], axis=1).reshape(-1, KV_LORA)


def _wo_to_kernel(w):
    return jnp.concatenate([w[:512], _pair_heads(w[512:], 0)], axis=0)


def _wo_from_kernel(dw):
    return jnp.concatenate([dw[:512], _pair_heads(dw[512:], 0, inverse=True)], axis=0)


def _kernel_weights(key, stored):
    if key in ("g1", "u1", "g2", "u2"):
        return {key + "t": stored}
    if key in ("d1", "d2"):
        return {key: stored}
    if key == "o":
        return {key: _wo_to_kernel(stored)}
    return {key + "t": _TO_KERNEL[key](stored)}


_SMALL = ("g_ffn1", "g_mix", "g_q_a", "g_kv_a", "attn_sinks", "rel_bias", "g_out_mla", "g_out_swa", "g_ffn2", "g_final")


class _Comm:
    def __init__(self, shards):
        self.shards = shards
        self.own32, self.recv16, self.pending, self.token = {}, {}, [], None

    def gather(self, keys):
        return [("gather", self.shards[k]) for k in keys]

    def gathered(self, keys, results, wts):
        for k, g in zip(keys, results):
            wts.update(_kernel_weights(k, g.reshape(N_DEV * g.shape[1], g.shape[2])))

    def scatter(self, keys, grads):
        jobs = []
        for k in keys:
            g32, g16 = (_FROM_KERNEL[k](g) if k in _FROM_KERNEL else g for g in grads[k])
            self.own32[k], blocks = _owner_blocks(g32, g16)
            jobs.append(("scatter", blocks))
        return jobs

    def scattered(self, keys, results):
        self.recv16.update(zip(keys, results))

    def scatter_start(self, keys, grads, name):
        handle, self.token = _scatter_start([blocks for _, blocks in self.scatter(keys, grads)], name + "_start")
        self.pending.append((keys, handle, name + "_wait"))

    def scatter_wait(self, after):
        keys, handle, name = self.pending.pop(0)
        self.scattered(keys, _scatter_wait(handle, after, name))
        return keys


def _step(x, target, gains, wts, comm):
    def jobs_in(keys):
        return comm.gather(keys) if comm else None

    def jobs_out(keys):
        return comm.scatter(keys, grads) if comm else None

    B, S, D = x.shape
    T = B * S
    x2 = x.reshape(T, D)
    t2 = target.reshape(T, D)

    pos = jnp.arange(S, dtype=F32)
    inv_freq = ROPE_THETA ** (-jnp.arange(0, QK_ROPE, 2, dtype=F32) / QK_ROPE)
    ang = pos[:, None] * inv_freq[None, :]
    cos, sin = jnp.cos(ang), jnp.sin(ang)
    ones, zeros = jnp.ones((S, 64), F32), jnp.zeros((S, 32), F32)
    rope_c = jnp.concatenate([ones, cos, cos, zeros], axis=1)
    rope_s = jnp.concatenate([0.0 * ones, -sin, sin, zeros], axis=1)

    qi = jnp.arange(WINDOW)[:, None]
    kj = jnp.arange(2 * WINDOW)[None, :]
    bucket = _t5_bucket(qi + WINDOW - kj).astype(jnp.int32)
    bias = _bias_expand(gains["rel_bias"], bucket).reshape(SWA_KV_HEADS, SWA_ROWS, 2 * WINDOW)
    sinks = jnp.broadcast_to(gains["attn_sinks"].reshape(SWA_HEADS, 1), (SWA_HEADS, 128))
    g_swa = _pair_heads(gains["g_out_swa"], 1)

    keys = ("in", "q", "kv", "o", "g2")
    (h1, n1, gate1, up1, act1), got = _ffn_fwd(x2, gains["g_ffn1"], wts["g1t"], wts["u1t"], wts["d1"], "ffn1_fwd", jobs_in(keys))
    if comm:
        comm.gathered(keys, got, wts)
    (u, c_q, c_kv, cqn, ckvn, q, k, v, qs, ks, vs) = _proj_fwd(
        h1, gains["g_mix"], wts["int"], gains["g_q_a"], wts["qt"], gains["g_kv_a"], wts["kvt"], rope_c, rope_s, S)
    keys = ("u2", "d2")
    (o_mla, lse), got = _mla_fwd(q, k, v, B, S, jobs_in(keys))
    if comm:
        comm.gathered(keys, got, wts)
    o_swa = _swa_fwd(qs, ks, vs, bias, sinks, B, S)
    h2, oc = _out_fwd(o_mla, o_swa, gains["g_out_mla"], g_swa, wts["o"], h1)
    (dh3, n2, gate2, up2, act2, loss, dg_final), _ = _ffn_fwd(
        h2, gains["g_ffn2"], wts["g2t"], wts["u2t"], wts["d2"], "ffn2_fwd", loss_head=(t2, gains["g_final"].reshape(1, D)))

    grads = {}
    (dh2, dg_ffn2, dgate2, dup2), _ = _ffn_bwd(h2, gains["g_ffn2"], gate2, up2, dh3, wts["g2t"], wts["u2t"], wts["d2"], "ffn2_bwd")
    grads["g2"], _ = _matmul_tn(dgate2, n2, "dw_gate2")
    grads["u2"], _ = _matmul_tn(dup2, n2, "dw_up2")
    grads["d2"], _ = _matmul_tn(act2, dh3, "dw_down2", 0.5)
    do_mla, do_swa, dg_mla, dg_swa = _out_bwd(dh2, o_mla, o_swa, gains["g_out_mla"], g_swa, wts["o"])
    grads["o"], _ = _matmul_tn(oc, dh2, "dw_o")
    keys = ("g2", "u2", "d2")
    (dq, dk, dv), got = _mla_bwd(q, k, v, do_mla, o_mla, lse, B, S, jobs_out(keys))
    if comm:
        comm.scattered(keys, got)
    dqs, dks, dvs, dbias, dsink = _swa_bwd(qs, ks, vs, do_swa, bias, sinks, B, S)
    dh1, dproj, dql, dkvc, dg_mix, dg_q, dg_kv = _proj_bwd(
        dq, dk, dv, dqs, dks, dvs, c_q, c_kv, h1, dh2, gains["g_mix"], wts["int"], gains["g_q_a"], wts["qt"],
        gains["g_kv_a"], wts["kvt"], rope_c, rope_s, S)
    grads["in"], _ = _matmul_tn(dproj, u, "dw_in")
    grads["q"], _ = _matmul_tn(dql, cqn, "dw_q")
    grads["kv"], _ = _matmul_tn(dkvc, ckvn, "dw_kv")
    grads["d1"], _ = _matmul_tn(act1, dh1, "dw_down1", 0.5)
    if comm:
        comm.scatter_start(("o", "in", "q", "kv", "d1"), grads, "grad_scatter_mid")
    (dx, dg_ffn1, dgate1, dup1), _ = _ffn_bwd(
        x2, gains["g_ffn1"], gate1, up1, dh1, wts["g1t"], wts["u1t"], wts["d1"], "ffn1_bwd", after=comm and comm.token)
    d_rel_bias, d_sinks = _bias_reduce(dbias.reshape(SWA_HEADS, WINDOW, 2 * WINDOW), dsink.reshape(SWA_HEADS, WINDOW, 128), bucket)
    small = {
        "g_ffn1": dg_ffn1, "g_mix": dg_mix, "g_q_a": dg_q, "g_kv_a": dg_kv, "attn_sinks": d_sinks,
        "rel_bias": d_rel_bias, "g_out_mla": dg_mla, "g_out_swa": _pair_heads(dg_swa, 1, inverse=True),
        "g_ffn2": dg_ffn2, "g_final": dg_final, "loss": loss,
    }
    small_jobs = [("gather", small[n]) for n in _SMALL + ("loss",)] if comm else None
    grads["g1"], got = _matmul_tn(dgate1, n1, "dw_gate1", exchange=small_jobs)
    if comm:
        comm.small_gathered = got
        comm.scatter_start(("g1",), grads, "grad_scatter_gate1")
    grads["u1"], _ = _matmul_tn(dup1, n1, "dw_up1", after=comm and comm.token)
    if comm:
        comm.scatter_start(("u1",), grads, "grad_scatter_up1")
    return dx.reshape(B, S, D), grads, small


_WEIGHTS = ("g_ffn1", "w_ffn1_gate", "w_ffn1_up", "w_ffn1_down", "g_mix", "w_in", "g_q_a", "w_q_b", "g_kv_a", "w_kv_b",
            "attn_sinks", "rel_bias", "g_out_mla", "g_out_swa", "w_o", "g_ffn2", "w_ffn2_gate", "w_ffn2_up", "w_ffn2_down",
            "g_final")
_BIG = (("w_ffn1_gate", "g1", True), ("w_ffn1_up", "u1", True), ("w_ffn1_down", "d1", False), ("w_in", "in", True),
        ("w_q_b", "q", True), ("w_kv_b", "kv", True), ("w_o", "o", False), ("w_ffn2_gate", "g2", True),
        ("w_ffn2_up", "u2", True), ("w_ffn2_down", "d2", False))
_TO_KERNEL = {"in": _win_to_kernel, "q": _wq_to_kernel, "kv": _wkv_to_kernel}
_FROM_KERNEL = {"in": _win_from_kernel, "q": _wq_from_kernel, "kv": _wkv_from_kernel, "o": _wo_from_kernel}


def kernel(x, g_ffn1, w_ffn1_gate, w_ffn1_up, w_ffn1_down, g_mix, w_in, g_q_a, w_q_b, g_kv_a, w_kv_b, attn_sinks, rel_bias, g_out_mla, g_out_swa, w_o, g_ffn2, w_ffn2_gate, w_ffn2_up, w_ffn2_down, g_final, loss_target, m_g_ffn1, m_w_ffn1_gate, m_w_ffn1_up, m_w_ffn1_down, m_g_mix, m_w_in, m_g_q_a, m_w_q_b, m_g_kv_a, m_w_kv_b, m_attn_sinks, m_rel_bias, m_g_out_mla, m_g_out_swa, m_w_o, m_g_ffn2, m_w_ffn2_gate, m_w_ffn2_up, m_w_ffn2_down, m_g_final, v_g_ffn1, v_w_ffn1_gate, v_w_ffn1_up, v_w_ffn1_down, v_g_mix, v_w_in, v_g_q_a, v_w_q_b, v_g_kv_a, v_w_kv_b, v_attn_sinks, v_rel_bias, v_g_out_mla, v_g_out_swa, v_w_o, v_g_ffn2, v_w_ffn2_gate, v_w_ffn2_up, v_w_ffn2_down, v_g_final):
    w = dict(zip(_WEIGHTS, (g_ffn1, w_ffn1_gate, w_ffn1_up, w_ffn1_down, g_mix, w_in, g_q_a, w_q_b, g_kv_a, w_kv_b, attn_sinks,
                            rel_bias, g_out_mla, g_out_swa, w_o, g_ffn2, w_ffn2_gate, w_ffn2_up, w_ffn2_down, g_final)))
    m = dict(zip(_WEIGHTS, (m_g_ffn1, m_w_ffn1_gate, m_w_ffn1_up, m_w_ffn1_down, m_g_mix, m_w_in, m_g_q_a, m_w_q_b, m_g_kv_a,
                            m_w_kv_b, m_attn_sinks, m_rel_bias, m_g_out_mla, m_g_out_swa, m_w_o, m_g_ffn2, m_w_ffn2_gate,
                            m_w_ffn2_up, m_w_ffn2_down, m_g_final)))
    v = dict(zip(_WEIGHTS, (v_g_ffn1, v_w_ffn1_gate, v_w_ffn1_up, v_w_ffn1_down, v_g_mix, v_w_in, v_g_q_a, v_w_q_b, v_g_kv_a,
                            v_w_kv_b, v_attn_sinks, v_rel_bias, v_g_out_mla, v_g_out_swa, v_w_o, v_g_ffn2, v_w_ffn2_gate,
                            v_w_ffn2_up, v_w_ffn2_down, v_g_final)))
    me = 4 * lax.axis_index("x") + 2 * lax.axis_index("y") + lax.axis_index("c")

    stored = lambda src, name, by_col: src[name][0].T if by_col else src[name][0]
    shards = dict(zip([key for _, key, _ in _BIG], _cast_bf16([stored(w, name, by_col) for name, _, by_col in _BIG])))
    comm = _Comm(shards)
    first = ("g1", "u1", "d1")
    wts = {}
    comm.gathered(first, _all_gather([shards[k] for k in first], "weights_all_gather"), wts)

    gains = {n: (w[n] if n == "rel_bias" else w[n].reshape(1, -1)) for n in _SMALL}
    grad_x, _, _ = _step(x, loss_target, gains, wts, comm)

    small_gathered = comm.small_gathered

    out_g, out_d, out_m, out_v = {}, {}, {}, {}
    me1 = me.astype(jnp.int32).reshape(1)
    big = {key: (name, by_col) for name, key, by_col in _BIG}

    def update(key):
        name, by_col = big[key]
        updates = _adam_big(stored(w, name, by_col), stored(m, name, by_col), stored(v, name, by_col),
                            comm.own32[key], comm.recv16[key], me1, comm.token)
        out_g[name], out_d[name], out_m[name], out_v[name] = ((a.T if by_col else a)[None] for a in updates)
        return updates[0]

    done = [update(key) for key in ("g2", "u2", "d2")]
    for _ in range(2):
        done = [update(key) for key in comm.scatter_wait(after=done[-1])]
    as_2d = lambda src: [src[n] if src[n].ndim == 2 else src[n].reshape(1, -1) for n in _SMALL]
    loss, updates = _adam_small(as_2d(w), as_2d(m), as_2d(v), small_gathered[:-1], small_gathered[-1])
    for n, (g, d, mn, vn) in zip(_SMALL, updates):
        out_g[n], out_d[n], out_m[n], out_v[n] = (a.reshape(w[n].shape) for a in (g, d, mn, vn))
    for key in comm.scatter_wait(after=loss):
        update(key)

    return (loss[0, 0], grad_x, *[out_g[n] for n in _WEIGHTS], *[out_d[n] for n in _WEIGHTS],
            *[out_m[n] for n in _WEIGHTS], *[out_v[n] for n in _WEIGHTS])
```
